```python
import jax, jax.numpy as jnp
from jax import lax
import numpy as np

D_MODEL = 1024
BATCH = 8
SEQ = 8192
DEPTH = 2

EXPAND = 2
D_INNER = EXPAND * D_MODEL
EPS = 1e-6
NEG = -1e30

A_WIDTH = D_INNER // 2
POOL_SIZES = (2, 4, 8, 16)
A_GROUPS = len(POOL_SIZES)
A_GROUP_DIM = A_WIDTH // A_GROUPS
B_WIDTH = D_INNER - A_WIDTH
B_GROUPS = 4
B_GROUP_DIM = B_WIDTH // B_GROUPS
CHUNK = 128
EVEN_IN = 2 * A_WIDTH + 3 * B_WIDTH

DILATED = ((128, 1), (512, 4), (2048, 16))
N_DIL = len(DILATED)
HEAD_DIM = 128
C_SLOTS = 8
C_HEADS = C_SLOTS * N_DIL
C_WIDTH = C_SLOTS * HEAD_DIM
D_WIDTH = D_INNER - C_WIDTH
CONV_W = 3
ATTN_BLOCK = 128
QKV_WIDTH = 3 * C_HEADS * HEAD_DIM
ODD_IN = QKV_WIDTH + C_WIDTH + 4 * D_WIDTH

N_EVEN = (DEPTH + 1) // 2
N_ODD = DEPTH // 2

kernel_name = "hybrid_pool_gmlp_dilattn_shortconv"


def rmsnorm(x, g):
    xf = x.astype(jnp.float32)
    y = xf * lax.rsqrt(jnp.mean(xf * xf, axis=-1, keepdims=True) + EPS)
    return (y * g.astype(jnp.float32)).astype(x.dtype)


def alibi_slopes(n):
    return jnp.asarray(2.0 ** (-8.0 * (np.arange(n) + 1) / n), dtype=jnp.float32)


def multiscale_pool(a, pool_w, pool_scale):
    Bn, S, _ = a.shape
    ag = a.reshape(Bn, S, A_GROUPS, A_GROUP_DIM).astype(jnp.float32)
    cs = jnp.pad(jnp.cumsum(ag, axis=1), ((0, 0), (1, 0), (0, 0), (0, 0)))
    t = jnp.arange(S)
    means = []
    for g, w in enumerate(POOL_SIZES):
        lo = jnp.maximum(t + 1 - w, 0)
        cnt = (t + 1 - lo).astype(jnp.float32)
        means.append((cs[:, 1:, g] - cs[:, lo, g]) / cnt[None, :, None])
    pooled = (jnp.stack(means, axis=2) - ag).astype(a.dtype)
    mixed = jnp.einsum('bsgc,gcd->bsgd', pooled, pool_w)
    return mixed.reshape(Bn, S, A_WIDTH) * pool_scale


def chunk_spatial_gate(u, v, ws, bs):
    Bn, S, _ = v.shape
    nc = S // CHUNK
    vg = v.reshape(Bn, nc, CHUNK, B_GROUPS, B_GROUP_DIM)
    causal = jnp.tril(jnp.ones((CHUNK, CHUNK), dtype=bool))
    w = jnp.where(causal[None], ws, jnp.zeros_like(ws))
    mixed = jnp.einsum('gts,bnsgc->bntgc', w, vg) + bs.T[None, None, :, :, None]
    return u * mixed.reshape(Bn, S, B_WIDTH)


def dilated_group_attention(q, k, v, window, dilation, slopes):
    Bn, S, H, Dh = q.shape
    unit = dilation * ATTN_BLOCK
    Sp = -(-S // unit) * unit
    pad = Sp - S
    L = Sp // dilation
    nb = L // ATTN_BLOCK
    span = window // dilation

    def to_sub(x):
        x = jnp.pad(x, ((0, 0), (0, pad), (0, 0), (0, 0)))
        return x.reshape(Bn, L, dilation, H, Dh).transpose(0, 2, 1, 3, 4)

    def band(x):
        x = jnp.pad(x, ((0, 0), (0, 0), (ATTN_BLOCK, 0), (0, 0), (0, 0)))
        x = x.reshape(Bn, dilation, nb + 1, ATTN_BLOCK, H, Dh)
        return jnp.concatenate([x[:, :, :-1], x[:, :, 1:]], axis=3)

    qb = to_sub(q).reshape(Bn, dilation, nb, ATTN_BLOCK, H, Dh)
    kb = band(to_sub(k))
    vb = band(to_sub(v))
    s = jnp.einsum('bdnqhe,bdnkhe->bdnhqk', qb, kb,
                   preferred_element_type=jnp.float32) * (Dh ** -0.5)
    qi = jnp.arange(ATTN_BLOCK)[:, None] + ATTN_BLOCK
    ki = jnp.arange(2 * ATTN_BLOCK)[None, :]
    steps = qi - ki
    band_ok = (steps >= 0) & (steps <= span)
    blk = jnp.arange(nb)[:, None, None]
    valid = band_ok[None] & ((blk > 0) | (ki >= ATTN_BLOCK)[None])
    dist = (steps * dilation).astype(jnp.float32)
    s = s - slopes[:, None, None] * dist
    s = jnp.where(valid[:, None], s, NEG)
    lse = jax.nn.logsumexp(s, axis=-1)
    p = jnp.exp(s - lse[..., None])
    o = jnp.einsum('bdnhqk,bdnkhe->bdnqhe', p.astype(v.dtype), vb)
    o = o.reshape(Bn, dilation, L, H, Dh).transpose(0, 2, 1, 3, 4).reshape(Bn, Sp, H, Dh)[:, :S]
    lse = lse.transpose(0, 1, 2, 4, 3).reshape(Bn, dilation, L, H)
    lse = lse.transpose(0, 2, 1, 3).reshape(Bn, Sp, H)[:, :S]
    return o, lse


def short_gated_conv(gb, gc, xt, conv_w):
    S = xt.shape[1]
    z = jnp.pad(gc * xt, ((0, 0), (CONV_W - 1, 0), (0, 0)))
    conv = conv_w[0] * z[:, 0:S]
    for j in range(1, CONV_W):
        conv = conv + conv_w[j] * z[:, j:j + S]
    return gb * conv


def even_layer(x, norm_g, w_in, pool_w, pool_scale, ws, bs, w_out):
    h = rmsnorm(x, norm_g)
    z = h @ w_in
    a, g_a, u, v, g_b = jnp.split(
        z, [A_WIDTH, 2 * A_WIDTH, 2 * A_WIDTH + B_WIDTH, 2 * A_WIDTH + 2 * B_WIDTH], axis=-1)
    y_a = multiscale_pool(a, pool_w, pool_scale) * jax.nn.silu(g_a)
    y_b = chunk_spatial_gate(u, v, ws, bs) * jax.nn.silu(g_b)
    return x + jnp.concatenate([y_a, y_b], axis=-1) @ w_out


def odd_layer(x, norm_g, w_in, conv_w, w_out):
    Bn, S, _ = x.shape
    h = rmsnorm(x, norm_g)
    z = h @ w_in
    o1 = QKV_WIDTH
    o2 = o1 + C_WIDTH
    qkv, g_c, d_b, d_c, d_x, g_d = jnp.split(
        z, [o1, o2, o2 + D_WIDTH, o2 + 2 * D_WIDTH, o2 + 3 * D_WIDTH], axis=-1)
    qkv = qkv.reshape(Bn, S, 3, N_DIL, C_SLOTS, HEAD_DIM)
    slopes = alibi_slopes(C_HEADS).reshape(N_DIL, C_SLOTS)
    outs, lses = [], []
    for gi, (window, dil) in enumerate(DILATED):
        o, l = dilated_group_attention(qkv[:, :, 0, gi], qkv[:, :, 1, gi], qkv[:, :, 2, gi],
                                       window, dil, slopes[gi])
        outs.append(o)
        lses.append(l)
    alpha = jax.nn.softmax(jnp.stack(lses, axis=0), axis=0)
    y_c = jnp.einsum('gbsh,gbshe->bshe', alpha.astype(x.dtype), jnp.stack(outs, axis=0))
    y_c = y_c.reshape(Bn, S, C_WIDTH) * jax.nn.silu(g_c)
    y_d = short_gated_conv(d_b, d_c, d_x, conv_w) * jax.nn.silu(g_d)
    return x + jnp.concatenate([y_c, y_d], axis=-1) @ w_out


def _fwd_setup_inputs(seed: int = 0) -> dict:
    key = jax.random.key(seed)
    ks = jax.random.split(key, 16)
    nrm = jax.random.normal
    f32 = jnp.float32
    return {
        "x": nrm(ks[0], (BATCH, SEQ, D_MODEL), f32),
        "even_norm": 1.0 + 0.05 * nrm(ks[1], (N_EVEN, D_MODEL), f32),
        "even_w_in": nrm(ks[2], (N_EVEN, D_MODEL, EVEN_IN), f32) * D_MODEL ** -0.5,
        "even_pool_w": nrm(ks[3], (N_EVEN, A_GROUPS, A_GROUP_DIM, A_GROUP_DIM), f32) * A_GROUP_DIM ** -0.5,
        "even_pool_scale": 1.0 + 0.1 * nrm(ks[4], (N_EVEN, A_WIDTH), f32),
        "even_ws": nrm(ks[5], (N_EVEN, B_GROUPS, CHUNK, CHUNK), f32) * CHUNK ** -0.5,
        "even_bs": 1.0 + 0.1 * nrm(ks[6], (N_EVEN, B_GROUPS, CHUNK), f32),
        "even_w_out": nrm(ks[7], (N_EVEN, D_INNER, D_MODEL), f32) * D_INNER ** -0.5,
        "odd_norm": 1.0 + 0.05 * nrm(ks[8], (N_ODD, D_MODEL), f32),
        "odd_w_in": nrm(ks[9], (N_ODD, D_MODEL, ODD_IN), f32) * D_MODEL ** -0.5,
        "odd_conv_w": nrm(ks[10], (N_ODD, CONV_W, D_WIDTH), f32) * CONV_W ** -0.5,
        "odd_w_out": nrm(ks[11], (N_ODD, D_INNER, D_MODEL), f32) * D_INNER ** -0.5,
        "final_norm": 1.0 + 0.05 * nrm(ks[12], (D_MODEL,), f32),
    }


def _fwd_reference(x, even_norm, even_w_in, even_pool_w, even_pool_scale, even_ws, even_bs,
              even_w_out, odd_norm, odd_w_in, odd_conv_w, odd_w_out, final_norm):
    h = x
    for layer in range(DEPTH):
        i = layer // 2
        if layer % 2 == 0:
            h = even_layer(h, even_norm[i], even_w_in[i], even_pool_w[i], even_pool_scale[i],
                           even_ws[i], even_bs[i], even_w_out[i])
        else:
            h = odd_layer(h, odd_norm[i], odd_w_in[i], odd_conv_w[i], odd_w_out[i])
    return rmsnorm(h, final_norm)


import jax as _jax
import jax.numpy as _jnp

TWIN_FORMAT = 'train_step'
FWD_PARAMS = ['x', 'even_norm', 'even_w_in', 'even_pool_w', 'even_pool_scale', 'even_ws', 'even_bs', 'even_w_out', 'odd_norm', 'odd_w_in', 'odd_conv_w', 'odd_w_out', 'final_norm']
TWIN_WEIGHTS = ['even_norm', 'even_w_in', 'even_pool_w', 'even_pool_scale', 'even_ws', 'even_bs', 'even_w_out', 'odd_norm', 'odd_w_in', 'odd_conv_w', 'odd_w_out', 'final_norm']
TWIN_DIFF_INPUT = 'x'
TWIN_INPUTS = ['x', 'even_norm', 'even_w_in', 'even_pool_w', 'even_pool_scale', 'even_ws', 'even_bs', 'even_w_out', 'odd_norm', 'odd_w_in', 'odd_conv_w', 'odd_w_out', 'final_norm', 'loss_target', 'm_even_norm', 'm_even_w_in', 'm_even_pool_w', 'm_even_pool_scale', 'm_even_ws', 'm_even_bs', 'm_even_w_out', 'm_odd_norm', 'm_odd_w_in', 'm_odd_conv_w', 'm_odd_w_out', 'm_final_norm', 'v_even_norm', 'v_even_w_in', 'v_even_pool_w', 'v_even_pool_scale', 'v_even_ws', 'v_even_bs', 'v_even_w_out', 'v_odd_norm', 'v_odd_w_in', 'v_odd_conv_w', 'v_odd_w_out', 'v_final_norm']
TWIN_OUTPUTS = ['loss', 'grad_x', 'grad_even_norm', 'grad_even_w_in', 'grad_even_pool_w', 'grad_even_pool_scale', 'grad_even_ws', 'grad_even_bs', 'grad_even_w_out', 'grad_odd_norm', 'grad_odd_w_in', 'grad_odd_conv_w', 'grad_odd_w_out', 'grad_final_norm', 'delta_even_norm', 'delta_even_w_in', 'delta_even_pool_w', 'delta_even_pool_scale', 'delta_even_ws', 'delta_even_bs', 'delta_even_w_out', 'delta_odd_norm', 'delta_odd_w_in', 'delta_odd_conv_w', 'delta_odd_w_out', 'delta_final_norm', 'new_m_even_norm', 'new_m_even_w_in', 'new_m_even_pool_w', 'new_m_even_pool_scale', 'new_m_even_ws', 'new_m_even_bs', 'new_m_even_w_out', 'new_m_odd_norm', 'new_m_odd_w_in', 'new_m_odd_conv_w', 'new_m_odd_w_out', 'new_m_final_norm', 'new_v_even_norm', 'new_v_even_w_in', 'new_v_even_pool_w', 'new_v_even_pool_scale', 'new_v_even_ws', 'new_v_even_bs', 'new_v_even_w_out', 'new_v_odd_norm', 'new_v_odd_w_in', 'new_v_odd_conv_w', 'new_v_odd_w_out', 'new_v_final_norm']
TWIN_LEAF_KINDS = {'loss': 'loss', 'grad_x': 'grad_x', 'grad_even_norm': 'grad_w', 'grad_even_w_in': 'grad_w', 'grad_even_pool_w': 'grad_w', 'grad_even_pool_scale': 'grad_w', 'grad_even_ws': 'grad_w', 'grad_even_bs': 'grad_w', 'grad_even_w_out': 'grad_w', 'grad_odd_norm': 'grad_w', 'grad_odd_w_in': 'grad_w', 'grad_odd_conv_w': 'grad_w', 'grad_odd_w_out': 'grad_w', 'grad_final_norm': 'grad_w', 'delta_even_norm': 'delta_w', 'delta_even_w_in': 'delta_w', 'delta_even_pool_w': 'delta_w', 'delta_even_pool_scale': 'delta_w', 'delta_even_ws': 'delta_w', 'delta_even_bs': 'delta_w', 'delta_even_w_out': 'delta_w', 'delta_odd_norm': 'delta_w', 'delta_odd_w_in': 'delta_w', 'delta_odd_conv_w': 'delta_w', 'delta_odd_w_out': 'delta_w', 'delta_final_norm': 'delta_w', 'new_m_even_norm': 'new_m', 'new_m_even_w_in': 'new_m', 'new_m_even_pool_w': 'new_m', 'new_m_even_pool_scale': 'new_m', 'new_m_even_ws': 'new_m', 'new_m_even_bs': 'new_m', 'new_m_even_w_out': 'new_m', 'new_m_odd_norm': 'new_m', 'new_m_odd_w_in': 'new_m', 'new_m_odd_conv_w': 'new_m', 'new_m_odd_w_out': 'new_m', 'new_m_final_norm': 'new_m', 'new_v_even_norm': 'new_v', 'new_v_even_w_in': 'new_v', 'new_v_even_pool_w': 'new_v', 'new_v_even_pool_scale': 'new_v', 'new_v_even_ws': 'new_v', 'new_v_even_bs': 'new_v', 'new_v_even_w_out': 'new_v', 'new_v_odd_norm': 'new_v', 'new_v_odd_w_in': 'new_v', 'new_v_odd_conv_w': 'new_v', 'new_v_odd_w_out': 'new_v', 'new_v_final_norm': 'new_v'}


def _forward(args):
    return _fwd_reference(*[args[k] for k in FWD_PARAMS])


def _output_shape():
    def fwd():
        inp = _fwd_setup_inputs(0)
        return _fwd_reference(*[inp[k] for k in FWD_PARAMS])
    out = _jax.eval_shape(fwd)
    return out.shape, out.dtype

N_MICROBATCH = 1
ADAM_LR = 0.001
ADAM_B1 = 0.9
ADAM_B2 = 0.999
ADAM_EPS = 1e-08
ADAM_WD = 0.01
ADAM_STEP = 10
PER_EXAMPLE_BATCH_AXIS = {'x': 0, 'loss_target': 0}
SHARED_INPUTS = []
_WEIGHT_DTYPES = {'even_norm': _jnp.float32, 'even_w_in': _jnp.float32, 'even_pool_w': _jnp.float32, 'even_pool_scale': _jnp.float32, 'even_ws': _jnp.float32, 'even_bs': _jnp.float32, 'even_w_out': _jnp.float32, 'odd_norm': _jnp.float32, 'odd_w_in': _jnp.float32, 'odd_conv_w': _jnp.float32, 'odd_w_out': _jnp.float32, 'final_norm': _jnp.float32}
MOMENT_SCALE = {'even_norm': 2.473276e-01, 'even_w_in': 1.074088e-01, 'even_pool_w': 9.338478e-02, 'even_pool_scale': 9.263726e-02, 'even_ws': 1.037396e-01, 'even_bs': 1.453866e-01, 'even_w_out': 1.593118e-01, 'odd_norm': 1.755205e-01, 'odd_w_in': 4.727812e-02, 'odd_conv_w': 8.740064e-02, 'odd_w_out': 8.812573e-02, 'final_norm': 6.421959e+01}


def _to_microbatches(a, axis):
    t = _jnp.moveaxis(a, axis, 0)
    t = t.reshape((N_MICROBATCH, t.shape[0] // N_MICROBATCH) + t.shape[1:])
    return _jnp.moveaxis(t, 1, axis + 1)


def setup_inputs(seed: int = 0) -> dict:
    inp = _fwd_setup_inputs(seed)
    key = _jax.random.fold_in(_jax.random.key(seed), 7919)
    shape, _ = _output_shape()
    out = dict(inp)
    out["loss_target"] = _jax.random.normal(_jax.random.fold_in(key, 0), shape, _jnp.float32)
    for i, name in enumerate(TWIN_WEIGHTS):
        w = inp[name].astype(_jnp.float32)
        if MOMENT_SCALE is None:
            s = _jnp.sqrt(_jnp.mean(_jnp.square(w)) + 1e-30)
        else:
            s = MOMENT_SCALE[name]
        km, kv = _jax.random.split(_jax.random.fold_in(key, i + 1))
        out[name] = w
        out["m_" + name] = s * _jax.random.normal(km, w.shape, _jnp.float32)
        out["v_" + name] = (s * s) * _jax.random.uniform(kv, w.shape, _jnp.float32, 0.5, 1.5)
    if N_MICROBATCH > 1:
        for name, axis in PER_EXAMPLE_BATCH_AXIS.items():
            out[name] = _to_microbatches(out[name], axis)
    return {'x': out['x'], 'even_norm': out['even_norm'], 'even_w_in': out['even_w_in'], 'even_pool_w': out['even_pool_w'], 'even_pool_scale': out['even_pool_scale'], 'even_ws': out['even_ws'], 'even_bs': out['even_bs'], 'even_w_out': out['even_w_out'], 'odd_norm': out['odd_norm'], 'odd_w_in': out['odd_w_in'], 'odd_conv_w': out['odd_conv_w'], 'odd_w_out': out['odd_w_out'], 'final_norm': out['final_norm'], 'loss_target': out['loss_target'], 'm_even_norm': out['m_even_norm'], 'm_even_w_in': out['m_even_w_in'], 'm_even_pool_w': out['m_even_pool_w'], 'm_even_pool_scale': out['m_even_pool_scale'], 'm_even_ws': out['m_even_ws'], 'm_even_bs': out['m_even_bs'], 'm_even_w_out': out['m_even_w_out'], 'm_odd_norm': out['m_odd_norm'], 'm_odd_w_in': out['m_odd_w_in'], 'm_odd_conv_w': out['m_odd_conv_w'], 'm_odd_w_out': out['m_odd_w_out'], 'm_final_norm': out['m_final_norm'], 'v_even_norm': out['v_even_norm'], 'v_even_w_in': out['v_even_w_in'], 'v_even_pool_w': out['v_even_pool_w'], 'v_even_pool_scale': out['v_even_pool_scale'], 'v_even_ws': out['v_even_ws'], 'v_even_bs': out['v_even_bs'], 'v_even_w_out': out['v_even_w_out'], 'v_odd_norm': out['v_odd_norm'], 'v_odd_w_in': out['v_odd_w_in'], 'v_odd_conv_w': out['v_odd_conv_w'], 'v_odd_w_out': out['v_odd_w_out'], 'v_final_norm': out['v_final_norm']}


def _loss(weights, diff, rest, loss_target):
    with _jax.named_scope("forward"):
        args = {**rest, TWIN_DIFF_INPUT: diff, **{k: w.astype(_WEIGHT_DTYPES[k]) for k, w in weights.items()}}
        y = _forward(args)
    with _jax.named_scope("loss_head"):
        err = _jnp.square(y.astype(_jnp.float32) - loss_target)
        return 0.5 * _jnp.sum(_jnp.mean(err, axis=-1)) if err.ndim else 0.5 * err


def _adamw(w, g, m, v):
    m = ADAM_B1 * m + (1.0 - ADAM_B1) * g
    v = ADAM_B2 * v + (1.0 - ADAM_B2) * _jnp.square(g)
    m_hat = m / (1.0 - ADAM_B1 ** ADAM_STEP)
    v_hat = v / (1.0 - ADAM_B2 ** ADAM_STEP)
    delta = -ADAM_LR * (m_hat / (_jnp.sqrt(v_hat) + ADAM_EPS) + ADAM_WD * w)
    return delta, m, v


def reference(x, even_norm, even_w_in, even_pool_w, even_pool_scale, even_ws, even_bs, even_w_out, odd_norm, odd_w_in, odd_conv_w, odd_w_out, final_norm, loss_target, m_even_norm, m_even_w_in, m_even_pool_w, m_even_pool_scale, m_even_ws, m_even_bs, m_even_w_out, m_odd_norm, m_odd_w_in, m_odd_conv_w, m_odd_w_out, m_final_norm, v_even_norm, v_even_w_in, v_even_pool_w, v_even_pool_scale, v_even_ws, v_even_bs, v_even_w_out, v_odd_norm, v_odd_w_in, v_odd_conv_w, v_odd_w_out, v_final_norm):
    given = dict(x=x, even_norm=even_norm, even_w_in=even_w_in, even_pool_w=even_pool_w, even_pool_scale=even_pool_scale, even_ws=even_ws, even_bs=even_bs, even_w_out=even_w_out, odd_norm=odd_norm, odd_w_in=odd_w_in, odd_conv_w=odd_conv_w, odd_w_out=odd_w_out, final_norm=final_norm, loss_target=loss_target, m_even_norm=m_even_norm, m_even_w_in=m_even_w_in, m_even_pool_w=m_even_pool_w, m_even_pool_scale=m_even_pool_scale, m_even_ws=m_even_ws, m_even_bs=m_even_bs, m_even_w_out=m_even_w_out, m_odd_norm=m_odd_norm, m_odd_w_in=m_odd_w_in, m_odd_conv_w=m_odd_conv_w, m_odd_w_out=m_odd_w_out, m_final_norm=m_final_norm, v_even_norm=v_even_norm, v_even_w_in=v_even_w_in, v_even_pool_w=v_even_pool_w, v_even_pool_scale=v_even_pool_scale, v_even_ws=v_even_ws, v_even_bs=v_even_bs, v_even_w_out=v_even_w_out, v_odd_norm=v_odd_norm, v_odd_w_in=v_odd_w_in, v_odd_conv_w=v_odd_conv_w, v_odd_w_out=v_odd_w_out, v_final_norm=v_final_norm)
    weights = {n: given[n] for n in TWIN_WEIGHTS}
    shared = {n: given[n] for n in SHARED_INPUTS}
    per_example = {n: given[n] for n in ['x']}
    grad_fn = _jax.value_and_grad(_loss, argnums=(0, 1))

    def one_microbatch(ex, loss_target):
        ex = dict(ex)
        diff = ex.pop(TWIN_DIFF_INPUT)
        return grad_fn(weights, diff, {**shared, **ex}, loss_target)

    if N_MICROBATCH == 1:
        loss, (grad_w, grad_x) = one_microbatch(per_example, given["loss_target"])
    else:
        def body(carry, xs):
            loss_sum, grad_sum = carry
            l_k, (gw_k, gx_k) = one_microbatch(xs[0], xs[1])
            with _jax.named_scope("update"):
                return (loss_sum + l_k, _jax.tree.map(_jnp.add, grad_sum, gw_k)), gx_k

        init = (_jnp.zeros((), _jnp.float32), _jax.tree.map(_jnp.zeros_like, weights))
        (loss, grad_w), grad_x = _jax.lax.scan(body, init, (per_example, given["loss_target"]))
    with _jax.named_scope("update"):
        delta_w, new_m, new_v = {}, {}, {}
        for n in TWIN_WEIGHTS:
            delta_w[n], new_m[n], new_v[n] = _adamw(weights[n], grad_w[n], given["m_" + n], given["v_" + n])
    return (loss, grad_x, *[grad_w[n] for n in TWIN_WEIGHTS], *[delta_w[n] for n in TWIN_WEIGHTS],
            *[new_m[n] for n in TWIN_WEIGHTS], *[new_v[n] for n in TWIN_WEIGHTS])
```

```python
import functools

import jax
import jax.numpy as jnp
from jax import lax
from jax.experimental import pallas as pl
from jax.experimental.pallas import tpu as pltpu

F32 = jnp.float32
BF16 = jnp.bfloat16
MESH_ID = pl.DeviceIdType.MESH

EPS = 1e-6
NEG = -1e30
N_DEV = 8
POOL_SIZES = (2, 4, 8, 16)
DILATIONS = (1, 4, 16)
N_HEADS = 24
HEADS_PER_GROUP = 8
HEAD_DIM = 128
ATTN_BLOCK = 128
CHUNK = 128
CB = 1024
HALO = 16
ODD_COLBLOCKS = 14
ADAM_LR = 0.001
ADAM_B1 = 0.9
ADAM_B2 = 0.999
ADAM_EPS = 1e-08
ADAM_WD = 0.01
ADAM_STEP = 10
VMEM_LIMIT = 52 * 1024 * 1024


def _cp(*sem):
    return pltpu.CompilerParams(dimension_semantics=sem, vmem_limit_bytes=VMEM_LIMIT)


def _dot(a, b):
    return jnp.dot(a, b, preferred_element_type=F32)


def _dot_nt(a, b):
    return lax.dot_general(a, b, (((1,), (1,)), ((), ())), preferred_element_type=F32)


def _dot_tn(a, b):
    return lax.dot_general(a, b, (((0,), (0,)), ((), ())), preferred_element_type=F32)


def _sigmoid(x):
    return 1.0 / (1.0 + jnp.exp(-x))


def _silu_and_grad(x):
    s = _sigmoid(x)
    return x * s, s * (1.0 + x * (1.0 - s))


def _sds(shape, dtype):
    return jax.ShapeDtypeStruct(shape, dtype)


def norm_matmul(x, g, w, name, tm=512, tn=1024):
    t, d = x.shape
    n = w.shape[1]

    def body(x_ref, g_ref, w_ref, z_ref, h_ref):
        @pl.when(pl.program_id(1) == 0)
        def _():
            xv = x_ref[...]
            r = lax.rsqrt(jnp.mean(xv * xv, axis=-1, keepdims=True) + EPS)
            h_ref[...] = ((xv * r) * g_ref[...]).astype(BF16)

        z_ref[...] = _dot(h_ref[...], w_ref[...]).astype(BF16)

    return pl.pallas_call(
        body, name=name, grid=(t // tm, n // tn),
        in_specs=[pl.BlockSpec((tm, d), lambda i, j: (i, 0)),
                  pl.BlockSpec((1, d), lambda i, j: (0, 0)),
                  pl.BlockSpec((d, tn), lambda i, j: (0, j))],
        out_specs=[pl.BlockSpec((tm, tn), lambda i, j: (i, j)),
                   pl.BlockSpec((tm, d), lambda i, j: (i, 0))],
        out_shape=[_sds((t, n), BF16), _sds((t, d), BF16)],
        compiler_params=_cp("parallel", "arbitrary"),
    )(x, g, w)


def matmul_residual(y, w, x, name, tm=512):
    t, k = y.shape
    d = w.shape[1]

    def body(y_ref, w_ref, x_ref, o_ref):
        o_ref[...] = x_ref[...] + _dot(y_ref[...], w_ref[...])

    return pl.pallas_call(
        body, name=name, grid=(t // tm,),
        in_specs=[pl.BlockSpec((tm, k), lambda i: (i, 0)),
                  pl.BlockSpec((k, d), lambda i: (0, 0)),
                  pl.BlockSpec((tm, d), lambda i: (i, 0))],
        out_specs=pl.BlockSpec((tm, d), lambda i: (i, 0)),
        out_shape=_sds((t, d), F32),
        compiler_params=_cp("parallel"),
    )(y, w, x)


def out_proj_bwd(dx, w, y, name, tm=256):
    t, d = dx.shape
    k = w.shape[0]
    steps = t // tm

    def body(dx_ref, w_ref, y_ref, dy_ref, dw_ref, acc):
        i = pl.program_id(0)
        dxb = dx_ref[...].astype(BF16)
        dy_ref[...] = _dot_nt(dxb, w_ref[...]).astype(BF16)
        part = _dot_tn(y_ref[...], dxb)

        @pl.when(i == 0)
        def _():
            acc[...] = part

        @pl.when(i > 0)
        def _():
            acc[...] += part

        @pl.when(i == steps - 1)
        def _():
            dw_ref[...] = acc[...].astype(BF16)

    return pl.pallas_call(
        body, name=name, grid=(steps,),
        in_specs=[pl.BlockSpec((tm, d), lambda i: (i, 0)),
                  pl.BlockSpec((k, d), lambda i: (0, 0)),
                  pl.BlockSpec((tm, k), lambda i: (i, 0))],
        out_specs=[pl.BlockSpec((tm, k), lambda i: (i, 0)),
                   pl.BlockSpec((k, d), lambda i: (0, 0))],
        out_shape=[_sds((t, k), BF16), _sds((k, d), BF16)],
        scratch_shapes=[pltpu.VMEM((k, d), F32)],
        compiler_params=_cp("arbitrary"),
    )(dx, w, y)


def _source_block(kind, rows):
    if kind[0] == "cols":
        return (rows, CB)
    return (None, rows, CB)


def _source_active(kind, j):
    if kind[0] == "cols":
        _, first, n = kind
        return (j >= first) & (j < first + n), jnp.clip(j - first, 0, n - 1)
    _, group = kind
    return (j < 9) & (j % 3 == group), jnp.clip(j // 3, 0, 2)


def _source_index(kind, row_block, inner):
    if kind[0] == "cols":
        return (row_block, inner)
    return (inner, row_block, 0)


def in_proj_dw(h, sources, n_blocks, name, tk=1024):
    t, d = h.shape
    steps = t // tk
    ns = len(sources)

    def body(*refs):
        h_ref, src_refs, dw_ref, acc = refs[0], refs[1:1 + ns], refs[1 + ns], refs[2 + ns]
        j, k = pl.program_id(0), pl.program_id(1)
        for s, (_, kind) in enumerate(sources):
            active, _ = _source_active(kind, j)

            @pl.when(active)
            def _(s=s):
                part = _dot_tn(h_ref[...], src_refs[s][...])

                @pl.when(k == 0)
                def _():
                    acc[...] = part

                @pl.when(k > 0)
                def _():
                    acc[...] += part

        @pl.when(k == steps - 1)
        def _():
            dw_ref[...] = acc[...].astype(BF16)

    def src_spec(kind):
        def index(j, k):
            active, inner = _source_active(kind, j)
            return _source_index(kind, jnp.where(active, k, 0), inner)
        return pl.BlockSpec(_source_block(kind, tk), index)

    return pl.pallas_call(
        body, name=name, grid=(n_blocks, steps),
        in_specs=[pl.BlockSpec((tk, d), lambda j, k: (k, 0))] + [src_spec(kind) for _, kind in sources],
        out_specs=pl.BlockSpec((d, CB), lambda j, k: (0, j)),
        out_shape=_sds((d, n_blocks * CB), BF16),
        scratch_shapes=[pltpu.VMEM((d, CB), F32)],
        compiler_params=_cp("parallel", "arbitrary"),
    )(h, *[a for a, _ in sources])


def in_proj_dx(sources, n_blocks, w, x, g, dres, name, tm=512):
    t, d = x.shape
    ns = len(sources)

    def body(*refs):
        src_refs = refs[:ns]
        w_ref, x_ref, g_ref, dres_ref, dx_ref, dg_ref, acc = refs[ns:]
        i, p = pl.program_id(0), pl.program_id(1)
        for s, (_, kind) in enumerate(sources):
            active, _ = _source_active(kind, p)

            @pl.when(active)
            def _(s=s):
                part = _dot_nt(src_refs[s][...], w_ref[...])

                @pl.when(p == 0)
                def _():
                    acc[...] = part

                @pl.when(p > 0)
                def _():
                    acc[...] += part

        @pl.when(p == n_blocks - 1)
        def _():
            xv = x_ref[...]
            r = lax.rsqrt(jnp.mean(xv * xv, axis=-1, keepdims=True) + EPS)
            xn = xv * r
            dh = acc[...]
            dhg = dh * g_ref[...]
            dx_ref[...] = dres_ref[...] + r * (dhg - xn * jnp.mean(dhg * xn, axis=-1, keepdims=True))
            part = jnp.sum(dh * xn, axis=0, keepdims=True)

            @pl.when(i == 0)
            def _():
                dg_ref[...] = part

            @pl.when(i > 0)
            def _():
                dg_ref[...] += part

    def src_spec(kind):
        def index(i, p):
            if kind[0] == "cols":
                _, first, n = kind
                inner = jnp.clip(p - first, 0, n - 1)
            else:
                _, group = kind
                inner = jnp.clip((p - group + 2) // 3, 0, 2)
            return _source_index(kind, i, inner)
        return pl.BlockSpec(_source_block(kind, tm), index)

    return pl.pallas_call(
        body, name=name, grid=(t // tm, n_blocks),
        in_specs=[src_spec(kind) for _, kind in sources] + [
            pl.BlockSpec((d, CB), lambda i, p: (0, p)),
            pl.BlockSpec((tm, d), lambda i, p: (i, 0)),
            pl.BlockSpec((1, d), lambda i, p: (0, 0)),
            pl.BlockSpec((tm, d), lambda i, p: (i, 0))],
        out_specs=[pl.BlockSpec((tm, d), lambda i, p: (i, 0)),
                   pl.BlockSpec((1, d), lambda i, p: (0, 0))],
        out_shape=[_sds((t, d), F32), _sds((1, d), F32)],
        scratch_shapes=[pltpu.VMEM((tm, d), F32)],
        compiler_params=_cp("arbitrary", "arbitrary"),
    )(*[a for a, _ in sources], w, x, g, dres)


def _window_counts(first_row, rows, w):
    t = first_row + lax.broadcasted_iota(jnp.int32, (rows, 1), 0)
    return jnp.minimum(t + 1, w).astype(F32)


def _tril_bf16(ws_ref, g):
    r = lax.broadcasted_iota(jnp.int32, (CHUNK, CHUNK), 0)
    c = lax.broadcasted_iota(jnp.int32, (CHUNK, CHUNK), 1)
    return jnp.where(r >= c, ws_ref[g], 0.0).astype(BF16), r >= c


def even_mix(z, wp, scale, ws, bs_col, name, tm=256):
    t = z.shape[0]
    gd = CB // len(POOL_SIZES)

    def body(a_ref, ga_ref, u_ref, v_ref, gb_ref, wp_ref, sc_ref, ws_ref, bs_ref, y_ref, pooled_ref, aext):
        i = pl.program_id(0)

        @pl.when(i == 0)
        def _():
            aext[0:HALO, :] = jnp.zeros((HALO, CB), F32)

        @pl.when(i > 0)
        def _():
            aext[0:HALO, :] = aext[tm:tm + HALO, :]

        aext[HALO:HALO + tm, :] = a_ref[...].astype(F32)
        for g, w in enumerate(POOL_SIZES):
            cols = slice(g * gd, (g + 1) * gd)
            tok = aext[HALO:HALO + tm, cols]
            s = tok
            for k in range(1, w):
                s = s + aext[HALO - k:HALO - k + tm, cols]
            pooled = (s / _window_counts(i * tm, tm, w) - tok).astype(BF16)
            pooled_ref[:, cols] = pooled
            mixed = _dot(pooled, wp_ref[g])
            silu_a, _ = _silu_and_grad(ga_ref[:, cols].astype(F32))
            y_ref[:, cols] = (mixed * sc_ref[:, cols] * silu_a).astype(BF16)
        for g in range(4):
            cols = slice(g * gd, (g + 1) * gd)
            wg, _ = _tril_bf16(ws_ref, g)
            for c in range(tm // CHUNK):
                rows = slice(c * CHUNK, (c + 1) * CHUNK)
                m = _dot(wg, v_ref[rows, cols]) + bs_ref[g]
                silu_b, _ = _silu_and_grad(gb_ref[rows, cols].astype(F32))
                y_ref[rows, CB + g * gd:CB + (g + 1) * gd] = (
                    u_ref[rows, cols].astype(F32) * m * silu_b).astype(BF16)

    zspec = lambda cb: pl.BlockSpec((tm, CB), lambda i, cb=cb: (i, cb))
    full = lambda shape: pl.BlockSpec(shape, lambda i: (0,) * len(shape))
    return pl.pallas_call(
        body, name=name, grid=(t // tm,),
        in_specs=[zspec(0), zspec(1), zspec(2), zspec(3), zspec(4),
                  full(wp.shape), full(scale.shape), full(ws.shape), full(bs_col.shape)],
        out_specs=[pl.BlockSpec((tm, 2 * CB), lambda i: (i, 0)), pl.BlockSpec((tm, CB), lambda i: (i, 0))],
        out_shape=[_sds((t, 2 * CB), BF16), _sds((t, CB), BF16)],
        scratch_shapes=[pltpu.VMEM((HALO + tm, CB), F32)],
        compiler_params=_cp("arbitrary"),
    )(z, z, z, z, z, wp, scale, ws, bs_col)


def even_mix_bwd(dy, z, pooled, wp, scale, ws, bs_col, name, tm=256):
    t = z.shape[0]
    nt = t // tm
    gd = CB // len(POOL_SIZES)

    def body(dya_ref, dyb_ref, ga_ref, u_ref, v_ref, gb_ref, pooled_ref, wp_ref, sc_ref, ws_ref, bs_ref,
             dz_ref, dwp_ref, dsc_ref, dws_ref, dbs_ref, dpext):
        i = pl.program_id(0)
        tile = nt - 1 - i

        @pl.when(i == 0)
        def _():
            dpext[tm:tm + HALO, :] = jnp.zeros((HALO, CB), F32)
            dwp_ref[...] = jnp.zeros_like(dwp_ref)
            dsc_ref[...] = jnp.zeros_like(dsc_ref)
            dws_ref[...] = jnp.zeros_like(dws_ref)
            dbs_ref[...] = jnp.zeros_like(dbs_ref)

        @pl.when(i > 0)
        def _():
            dpext[tm:tm + HALO, :] = dpext[0:HALO, :]

        for g, w in enumerate(POOL_SIZES):
            cols = slice(g * gd, (g + 1) * gd)
            pooled_g = pooled_ref[:, cols]
            mixed = _dot(pooled_g, wp_ref[g])
            silu_a, dsilu_a = _silu_and_grad(ga_ref[:, cols].astype(F32))
            dya = dya_ref[:, cols].astype(F32)
            sc = sc_ref[:, cols]
            dmixed = (dya * sc * silu_a).astype(BF16)
            dsc_ref[:, cols] += jnp.sum(dya * mixed * silu_a, axis=0, keepdims=True)
            dz_ref[:, CB + g * gd:CB + (g + 1) * gd] = (dya * mixed * sc * dsilu_a).astype(BF16)
            dwp_ref[g] += _dot_tn(pooled_g, dmixed)
            dpooled = _dot_nt(dmixed, wp_ref[g])
            dpext[0:tm, cols] = dpooled / _window_counts(tile * tm, tm, w)
            s = dpext[0:tm, cols]
            for k in range(1, w):
                s = s + dpext[k:k + tm, cols]
            dz_ref[:, cols] = (s - dpooled).astype(BF16)
        for g in range(4):
            cols = slice(g * gd, (g + 1) * gd)
            wg, lower = _tril_bf16(ws_ref, g)
            dws_g = jnp.zeros((CHUNK, CHUNK), F32)
            dbs_g = jnp.zeros((CHUNK, 1), F32)
            for c in range(tm // CHUNK):
                rows = slice(c * CHUNK, (c + 1) * CHUNK)
                vb = v_ref[rows, cols]
                m = _dot(wg, vb) + bs_ref[g]
                gbv = gb_ref[rows, cols].astype(F32)
                silu_b, dsilu_b = _silu_and_grad(gbv)
                dyb = dyb_ref[rows, cols].astype(F32)
                uv = u_ref[rows, cols].astype(F32)
                dm = dyb * silu_b * uv
                dmb = dm.astype(BF16)
                dz_ref[rows, 2 * CB + g * gd:2 * CB + (g + 1) * gd] = (dyb * silu_b * m).astype(BF16)
                dz_ref[rows, 3 * CB + g * gd:3 * CB + (g + 1) * gd] = _dot_tn(wg, dmb).astype(BF16)
                dz_ref[rows, 4 * CB + g * gd:4 * CB + (g + 1) * gd] = (dyb * uv * m * dsilu_b).astype(BF16)
                dws_g = dws_g + _dot_nt(dmb, vb)
                dbs_g = dbs_g + jnp.sum(dm, axis=1, keepdims=True)
            dws_ref[g] += jnp.where(lower, dws_g, 0.0)
            dbs_ref[g] += dbs_g

    rev = lambda cb: pl.BlockSpec((tm, CB), lambda i, cb=cb: (nt - 1 - i, cb))
    full = lambda shape: pl.BlockSpec(shape, lambda i: (0,) * len(shape))
    return pl.pallas_call(
        body, name=name, grid=(nt,),
        in_specs=[rev(0), rev(1), rev(1), rev(2), rev(3), rev(4), rev(0),
                  full(wp.shape), full(scale.shape), full(ws.shape), full(bs_col.shape)],
        out_specs=[pl.BlockSpec((tm, 5 * CB), lambda i: (nt - 1 - i, 0)),
                   full(wp.shape), full(scale.shape), full(ws.shape), full(bs_col.shape)],
        out_shape=[_sds((t, 5 * CB), BF16), _sds(wp.shape, F32), _sds(scale.shape, F32),
                   _sds(ws.shape, F32), _sds(bs_col.shape, F32)],
        scratch_shapes=[pltpu.VMEM((tm + HALO, CB), F32)],
        compiler_params=_cp("arbitrary"),
    )(dy, dy, z, z, z, z, pooled, wp, scale, ws, bs_col)


def _slope(group, head):
    return float(2.0 ** (-8.0 * (group * HEADS_PER_GROUP + head + 1) / N_HEADS))


def _band(dilation):
    qi = lax.broadcasted_iota(jnp.int32, (ATTN_BLOCK, ATTN_BLOCK), 0)
    ki = lax.broadcasted_iota(jnp.int32, (ATTN_BLOCK, ATTN_BLOCK), 1)
    dist_prev = ((qi + ATTN_BLOCK - ki) * dilation).astype(F32)
    dist_cur = ((qi - ki) * dilation).astype(F32)
    return dist_prev, ki >= qi, dist_cur, ki <= qi


def _attn_views(z, group):
    d = DILATIONS[group]
    zv = z.reshape(z.shape[0] // d, d * z.shape[1])
    last = zv.shape[0] // ATTN_BLOCK - 1
    blk = (ATTN_BLOCK, CB)

    def cur(which):
        return pl.BlockSpec(blk, lambda r, n: (jnp.minimum(n, last), r * ODD_COLBLOCKS + which * 3 + group))

    def prev(which):
        return pl.BlockSpec(blk, lambda r, n: (jnp.clip(n - 1, 0, last), r * ODD_COLBLOCKS + which * 3 + group))

    return zv, [cur(0), prev(1), cur(1), prev(2), cur(2)]


def attn_fwd(z, group, name):
    t = z.shape[0]
    d = DILATIONS[group]
    rows = t // d
    nb = rows // ATTN_BLOCK
    zv, specs = _attn_views(z, group)
    scale = HEAD_DIM ** -0.5

    def body(q_ref, kp_ref, kc_ref, vp_ref, vc_ref, o_ref, lse_ref):
        n = pl.program_id(1)
        dist_p, ok_p, dist_c, ok_c = _band(d)
        ok_p = jnp.logical_and(ok_p, n > 0)
        for h in range(HEADS_PER_GROUP):
            hs = slice(h * HEAD_DIM, (h + 1) * HEAD_DIM)
            slope = _slope(group, h)
            q = q_ref[:, hs]
            sp = jnp.where(ok_p, _dot_nt(q, kp_ref[:, hs]) * scale - slope * dist_p, NEG)
            sc = jnp.where(ok_c, _dot_nt(q, kc_ref[:, hs]) * scale - slope * dist_c, NEG)
            m = jnp.maximum(jnp.max(sp, axis=1, keepdims=True), jnp.max(sc, axis=1, keepdims=True))
            l = jnp.sum(jnp.exp(sp - m), axis=1, keepdims=True) + jnp.sum(jnp.exp(sc - m), axis=1, keepdims=True)
            lse = m + jnp.log(l)
            pp = jnp.exp(sp - lse).astype(BF16)
            pc = jnp.exp(sc - lse).astype(BF16)
            o_ref[:, hs] = (_dot(pp, vp_ref[:, hs]) + _dot(pc, vc_ref[:, hs])).astype(BF16)
            lse_ref[:, hs] = jnp.broadcast_to(lse, (ATTN_BLOCK, HEAD_DIM))

    out_spec = pl.BlockSpec((ATTN_BLOCK, CB), lambda r, n: (n, r))
    o, lse = pl.pallas_call(
        body, name=name, grid=(d, nb),
        in_specs=specs, out_specs=[out_spec, out_spec],
        out_shape=[_sds((rows, d * CB), BF16), _sds((rows, d * CB), F32)],
        compiler_params=_cp("parallel", "arbitrary"),
    )(zv, zv, zv, zv, zv)
    return o.reshape(t, CB), lse.reshape(t, CB)


def attn_bwd(z, dyc, lse, dd, group, name):
    t = z.shape[0]
    d = DILATIONS[group]
    rows = t // d
    nb = rows // ATTN_BLOCK
    zv, specs = _attn_views(z, group)
    scale = HEAD_DIM ** -0.5
    view = lambda a: a.reshape(rows, d * CB)

    def body(q_ref, kp_ref, kc_ref, vp_ref, vc_ref, dy_ref, lse_ref, dd_ref, out_ref, carry):
        n = pl.program_id(1)

        @pl.when(n == 0)
        def _():
            carry[...] = jnp.zeros_like(carry)

        @pl.when(n == nb)
        def _():
            out_ref[...] = carry[...].astype(BF16)

        @pl.when(n < nb)
        def _():
            dist_p, ok_p, dist_c, ok_c = _band(d)
            ok_p = jnp.logical_and(ok_p, n > 0)
            for h in range(HEADS_PER_GROUP):
                hs = slice(h * HEAD_DIM, (h + 1) * HEAD_DIM)
                slope = _slope(group, h)
                q, kp, kc, vp, vc = q_ref[:, hs], kp_ref[:, hs], kc_ref[:, hs], vp_ref[:, hs], vc_ref[:, hs]
                dy = dy_ref[:, hs]
                lse_h = lse_ref[:, h * HEAD_DIM:h * HEAD_DIM + 1]
                dd_h = dd_ref[:, h * HEAD_DIM:h * HEAD_DIM + 1]
                sp = jnp.where(ok_p, _dot_nt(q, kp) * scale - slope * dist_p, NEG)
                sc = jnp.where(ok_c, _dot_nt(q, kc) * scale - slope * dist_c, NEG)
                pp = jnp.exp(sp - lse_h)
                pc = jnp.exp(sc - lse_h)
                dsp = (pp * (_dot_nt(dy, vp) - dd_h) * scale).astype(BF16)
                dsc = (pc * (_dot_nt(dy, vc) - dd_h) * scale).astype(BF16)
                ppb, pcb = pp.astype(BF16), pc.astype(BF16)
                dq = _dot(dsp, kp) + _dot(dsc, kc)
                out_ref[0, :, hs] = carry[0, :, hs].astype(BF16)
                out_ref[1, :, hs] = (carry[1, :, hs] + _dot_tn(dsp, q)).astype(BF16)
                out_ref[2, :, hs] = (carry[2, :, hs] + _dot_tn(ppb, dy)).astype(BF16)
                carry[0, :, hs] = dq
                carry[1, :, hs] = _dot_tn(dsc, q)
                carry[2, :, hs] = _dot_tn(pcb, dy)

    late = lambda r, n: (jnp.minimum(n, nb - 1), r)
    blk = pl.BlockSpec((ATTN_BLOCK, CB), late)
    out = pl.pallas_call(
        body, name=name, grid=(d, nb + 1),
        in_specs=specs + [blk, blk, blk],
        out_specs=pl.BlockSpec((3, ATTN_BLOCK, CB), lambda r, n: (0, jnp.maximum(n, 1) - 1, r)),
        out_shape=_sds((3, rows, d * CB), BF16),
        scratch_shapes=[pltpu.VMEM((3, ATTN_BLOCK, CB), F32)],
        compiler_params=_cp("parallel", "arbitrary"),
    )(zv, zv, zv, zv, zv, view(dyc), view(lse), view(dd))
    return out.reshape(3, t, CB)


def odd_mix(os_, lses, z, conv_w, name, tm=256):
    t = z.shape[0]

    def body(o0, o1, o2, l0, l1, l2, gc_ref, db_ref, dc_ref, dx_ref, gd_ref, cw_ref,
             y_ref, yc_ref, lse_ref, conv_ref, zext):
        i = pl.program_id(0)
        a0, a1, a2 = l0[...], l1[...], l2[...]
        m = jnp.maximum(jnp.maximum(a0, a1), a2)
        tot = m + jnp.log(jnp.exp(a0 - m) + jnp.exp(a1 - m) + jnp.exp(a2 - m))
        yc = (jnp.exp(a0 - tot) * o0[...].astype(F32) + jnp.exp(a1 - tot) * o1[...].astype(F32)
              + jnp.exp(a2 - tot) * o2[...].astype(F32))
        lse_ref[...] = tot
        yc_ref[...] = yc.astype(BF16)
        silu_c, _ = _silu_and_grad(gc_ref[...].astype(F32))
        y_ref[:, 0:CB] = (yc * silu_c).astype(BF16)

        @pl.when(i == 0)
        def _():
            zext[0:HALO, :] = jnp.zeros((HALO, CB), F32)

        @pl.when(i > 0)
        def _():
            zext[0:HALO, :] = zext[tm:tm + HALO, :]

        zext[HALO:HALO + tm, :] = dc_ref[...].astype(F32) * dx_ref[...].astype(F32)
        conv = (cw_ref[0:1, :] * zext[HALO - 2:HALO - 2 + tm, :] + cw_ref[1:2, :] * zext[HALO - 1:HALO - 1 + tm, :]
                + cw_ref[2:3, :] * zext[HALO:HALO + tm, :])
        conv_ref[...] = conv.astype(BF16)
        silu_d, _ = _silu_and_grad(gd_ref[...].astype(F32))
        y_ref[:, CB:2 * CB] = (db_ref[...].astype(F32) * conv * silu_d).astype(BF16)

    row = pl.BlockSpec((tm, CB), lambda i: (i, 0))
    zspec = lambda cb: pl.BlockSpec((tm, CB), lambda i, cb=cb: (i, cb))
    return pl.pallas_call(
        body, name=name, grid=(t // tm,),
        in_specs=[row] * 6 + [zspec(9), zspec(10), zspec(11), zspec(12), zspec(13),
                              pl.BlockSpec(conv_w.shape, lambda i: (0, 0))],
        out_specs=[pl.BlockSpec((tm, 2 * CB), lambda i: (i, 0)), row, row, row],
        out_shape=[_sds((t, 2 * CB), BF16), _sds((t, CB), BF16), _sds((t, CB), F32), _sds((t, CB), BF16)],
        scratch_shapes=[pltpu.VMEM((HALO + tm, CB), F32)],
        compiler_params=_cp("arbitrary"),
    )(*os_, *lses, z, z, z, z, z, conv_w)


def odd_mix_bwd(dy, yc, conv, z, conv_w, name, tm=256):
    t = z.shape[0]
    nt = t // tm

    def body(dyc_ref, dyd_ref, yc_ref, conv_ref, gc_ref, db_ref, dc_ref, dx_ref, gd_ref, cw_ref,
             dz_ref, dyo_ref, dd_ref, dcw_ref, dcext):
        i = pl.program_id(0)

        @pl.when(i == 0)
        def _():
            dcext[tm:tm + HALO, :] = jnp.zeros((HALO, CB), F32)
            dcw_ref[...] = jnp.zeros_like(dcw_ref)

        @pl.when(i > 0)
        def _():
            dcext[tm:tm + HALO, :] = dcext[0:HALO, :]

        silu_c, dsilu_c = _silu_and_grad(gc_ref[...].astype(F32))
        dyc = dyc_ref[...].astype(F32)
        ycv = yc_ref[...].astype(F32)
        dyo = dyc * silu_c
        dyo_ref[...] = dyo.astype(BF16)
        dz_ref[:, 0:CB] = (dyc * ycv * dsilu_c).astype(BF16)
        prod = dyo * ycv
        for h in range(HEADS_PER_GROUP):
            hs = slice(h * HEAD_DIM, (h + 1) * HEAD_DIM)
            dd_ref[:, hs] = jnp.broadcast_to(jnp.sum(prod[:, hs], axis=1, keepdims=True), (tm, HEAD_DIM))

        silu_d, dsilu_d = _silu_and_grad(gd_ref[...].astype(F32))
        dyd = dyd_ref[...].astype(F32)
        convv = conv_ref[...].astype(F32)
        dbv = db_ref[...].astype(F32)
        dz_ref[:, CB:2 * CB] = (dyd * convv * silu_d).astype(BF16)
        dz_ref[:, 4 * CB:5 * CB] = (dyd * dbv * convv * dsilu_d).astype(BF16)
        dcext[0:tm, :] = dyd * dbv * silu_d
        dcv, dxv = dc_ref[...].astype(F32), dx_ref[...].astype(F32)
        zc = dcv * dxv
        d0, d1, d2 = dcext[0:tm, :], dcext[1:1 + tm, :], dcext[2:2 + tm, :]
        dzc = cw_ref[2:3, :] * d0 + cw_ref[1:2, :] * d1 + cw_ref[0:1, :] * d2
        dz_ref[:, 2 * CB:3 * CB] = (dzc * dxv).astype(BF16)
        dz_ref[:, 3 * CB:4 * CB] = (dzc * dcv).astype(BF16)
        dcw_ref[0:1, :] += jnp.sum(zc * d2, axis=0, keepdims=True)
        dcw_ref[1:2, :] += jnp.sum(zc * d1, axis=0, keepdims=True)
        dcw_ref[2:3, :] += jnp.sum(zc * d0, axis=0, keepdims=True)

    rev = lambda cb: pl.BlockSpec((tm, CB), lambda i, cb=cb: (nt - 1 - i, cb))
    return pl.pallas_call(
        body, name=name, grid=(nt,),
        in_specs=[rev(0), rev(1), rev(0), rev(0), rev(9), rev(10), rev(11), rev(12), rev(13),
                  pl.BlockSpec(conv_w.shape, lambda i: (0, 0))],
        out_specs=[pl.BlockSpec((tm, 5 * CB), lambda i: (nt - 1 - i, 0)), rev(0), rev(0),
                   pl.BlockSpec(conv_w.shape, lambda i: (0, 0))],
        out_shape=[_sds((t, 5 * CB), BF16), _sds((t, CB), BF16), _sds((t, CB), F32), _sds(conv_w.shape, F32)],
        scratch_shapes=[pltpu.VMEM((tm + HALO, CB), F32)],
        compiler_params=_cp("arbitrary"),
    )(dy, dy, yc, conv, z, z, z, z, z, conv_w)


def final_norm_loss(x, g, target, name, tm=512):
    t, d = x.shape

    def body(x_ref, g_ref, t_ref, dx_ref, loss_ref, dg_ref):
        i = pl.program_id(0)
        xv = x_ref[...]
        r = lax.rsqrt(jnp.mean(xv * xv, axis=-1, keepdims=True) + EPS)
        xn = xv * r
        gv = g_ref[...]
        err = xn * gv - t_ref[...]
        loss = 0.5 * jnp.sum(jnp.mean(err * err, axis=-1, keepdims=True), axis=0, keepdims=True)
        dy = err * (1.0 / d)
        dyg = dy * gv
        dx_ref[...] = r * (dyg - xn * jnp.mean(dyg * xn, axis=-1, keepdims=True))
        dg = jnp.sum(dy * xn, axis=0, keepdims=True)

        @pl.when(i == 0)
        def _():
            loss_ref[...] = jnp.broadcast_to(loss, loss_ref.shape)
            dg_ref[...] = dg

        @pl.when(i > 0)
        def _():
            loss_ref[...] += jnp.broadcast_to(loss, loss_ref.shape)
            dg_ref[...] += dg

    return pl.pallas_call(
        body, name=name, grid=(t // tm,),
        in_specs=[pl.BlockSpec((tm, d), lambda i: (i, 0)), pl.BlockSpec((1, d), lambda i: (0, 0)),
                  pl.BlockSpec((tm, d), lambda i: (i, 0))],
        out_specs=[pl.BlockSpec((tm, d), lambda i: (i, 0)), pl.BlockSpec((8, 128), lambda i: (0, 0)),
                   pl.BlockSpec((1, d), lambda i: (0, 0))],
        out_shape=[_sds((t, d), F32), _sds((8, 128), F32), _sds((1, d), F32)],
        compiler_params=_cp("arbitrary"),
    )(x, g, target)


def _position():
    x, y, c = lax.axis_index("x"), lax.axis_index("y"), lax.axis_index("c")
    return x, y, c, 4 * x + 2 * y + c


def _peer(x, y, c, k):
    px = 1 - x if k & 4 else x
    py = 1 - y if k & 2 else y
    pc = 1 - c if k & 1 else c
    return (px, py, pc), 4 * px + 2 * py + pc


def _block(ref, axis, size, idx):
    index = [slice(None)] * len(ref.shape)
    index[axis] = pl.ds(idx * size, size)
    return ref.at[tuple(index)]


def all_gather(shards, axes, name):
    n = len(shards)
    fulls = []
    for s, ax in zip(shards, axes):
        shape = list(s.shape)
        shape[ax] *= N_DEV
        fulls.append(_sds(tuple(shape), s.dtype))

    def body(*refs):
        ins, outs = refs[:n], refs[n:2 * n]
        send_sems, recv_sems, local_sems = refs[2 * n:]
        x, y, c, me = _position()
        local = []
        for p in range(n):
            size = ins[p].shape[axes[p]]
            cp = pltpu.make_async_copy(ins[p], _block(outs[p], axes[p], size, me), local_sems.at[p])
            cp.start()
            local.append(cp)
        for k in range(1, N_DEV):
            peer, _ = _peer(x, y, c, k)
            for p in range(n):
                size = ins[p].shape[axes[p]]
                pltpu.make_async_remote_copy(
                    src_ref=ins[p], dst_ref=_block(outs[p], axes[p], size, me),
                    send_sem=send_sems.at[p], recv_sem=recv_sems.at[p],
                    device_id=peer, device_id_type=MESH_ID).start()
        for p in range(n):
            size = ins[p].shape[axes[p]]
            seven = _block(outs[p], axes[p], (N_DEV - 1) * size, 0)
            pltpu.make_async_remote_copy(
                src_ref=seven, dst_ref=seven, send_sem=send_sems.at[p], recv_sem=recv_sems.at[p],
                device_id=(x, y, c), device_id_type=MESH_ID).wait()
            local[p].wait()

    any_spec = pl.BlockSpec(memory_space=pl.ANY)
    return pl.pallas_call(
        body, name=name,
        in_specs=[any_spec] * n, out_specs=[any_spec] * n, out_shape=fulls,
        scratch_shapes=[pltpu.SemaphoreType.DMA((n,)), pltpu.SemaphoreType.DMA((n,)), pltpu.SemaphoreType.DMA((n,))],
    )(*shards)


def scatter_partials(fulls, axes, name):
    n = len(fulls)
    outs_shape = []
    for f, ax in zip(fulls, axes):
        shape = list(f.shape)
        shape[ax] //= N_DEV
        outs_shape.append(_sds((N_DEV, *shape), f.dtype))

    def body(*refs):
        ins, outs = refs[:n], refs[n:2 * n]
        send_sems, recv_sems, local_sems = refs[2 * n:]
        x, y, c, me = _position()
        local = []
        for p in range(n):
            size = ins[p].shape[axes[p]] // N_DEV
            cp = pltpu.make_async_copy(_block(ins[p], axes[p], size, me), outs[p].at[me], local_sems.at[p])
            cp.start()
            local.append(cp)
        for k in range(1, N_DEV):
            peer, pj = _peer(x, y, c, k)
            for p in range(n):
                size = ins[p].shape[axes[p]] // N_DEV
                pltpu.make_async_remote_copy(
                    src_ref=_block(ins[p], axes[p], size, pj), dst_ref=outs[p].at[me],
                    send_sem=send_sems.at[p], recv_sem=recv_sems.at[p],
                    device_id=peer, device_id_type=MESH_ID).start()
        for p in range(n):
            seven = outs[p].at[pl.ds(0, N_DEV - 1)]
            pltpu.make_async_remote_copy(
                src_ref=seven, dst_ref=seven, send_sem=send_sems.at[p], recv_sem=recv_sems.at[p],
                device_id=(x, y, c), device_id_type=MESH_ID).wait()
            local[p].wait()

    any_spec = pl.BlockSpec(memory_space=pl.ANY)
    return pl.pallas_call(
        body, name=name,
        in_specs=[any_spec] * n, out_specs=[any_spec] * n, out_shape=outs_shape,
        scratch_shapes=[pltpu.SemaphoreType.DMA((n,)), pltpu.SemaphoreType.DMA((n,)), pltpu.SemaphoreType.DMA((n,))],
    )(*fulls)


def _adam_math(g, w, m, v):
    m2 = ADAM_B1 * m + (1.0 - ADAM_B1) * g
    v2 = ADAM_B2 * v + (1.0 - ADAM_B2) * (g * g)
    m_hat = m2 / (1.0 - ADAM_B1 ** ADAM_STEP)
    v_hat = v2 / (1.0 - ADAM_B2 ** ADAM_STEP)
    delta = -ADAM_LR * (m_hat / (jnp.sqrt(v_hat) + ADAM_EPS) + ADAM_WD * w)
    return delta, m2, v2


def adamw(g, w, m, v, name, slots, tr=128):
    r, c = w.shape
    tr = min(tr, r)

    def body(g_ref, w_ref, m_ref, v_ref, go_ref, d_ref, mo_ref, vo_ref):
        if slots:
            gs = g_ref[0].astype(F32)
            for s in range(1, N_DEV):
                gs = gs + g_ref[s].astype(F32)
        else:
            gs = g_ref[...]
        go_ref[...] = gs
        d_ref[...], mo_ref[...], vo_ref[...] = _adam_math(gs, w_ref[...], m_ref[...], v_ref[...])

    row = pl.BlockSpec((tr, c), lambda i: (i, 0))
    gspec = pl.BlockSpec((N_DEV, tr, c), lambda i: (0, i, 0)) if slots else row
    return pl.pallas_call(
        body, name=name, grid=(r // tr,),
        in_specs=[gspec, row, row, row], out_specs=[row] * 4, out_shape=[_sds((r, c), F32)] * 4,
        compiler_params=_cp("parallel"),
    )(g, w, m, v)


def sum_slots(g, name):
    _, r, c = g.shape

    def body(g_ref, o_ref):
        gs = g_ref[0]
        for s in range(1, N_DEV):
            gs = gs + g_ref[s]
        o_ref[...] = gs

    return pl.pallas_call(
        body, name=name, grid=(1,),
        in_specs=[pl.BlockSpec((N_DEV, r, c), lambda i: (0, 0, 0))],
        out_specs=pl.BlockSpec((r, c), lambda i: (0, 0)), out_shape=_sds((r, c), F32),
        compiler_params=_cp("arbitrary"),
    )(g)


def _rows128(a, pad_to=8):
    a = a.reshape(-1, 128)
    pad = (-a.shape[0]) % pad_to
    return jnp.pad(a, ((0, pad), (0, 0))) if pad else a


def kernel(x, even_norm, even_w_in, even_pool_w, even_pool_scale, even_ws, even_bs, even_w_out, odd_norm, odd_w_in, odd_conv_w, odd_w_out, final_norm, loss_target, m_even_norm, m_even_w_in, m_even_pool_w, m_even_pool_scale, m_even_ws, m_even_bs, m_even_w_out, m_odd_norm, m_odd_w_in, m_odd_conv_w, m_odd_w_out, m_final_norm, v_even_norm, v_even_w_in, v_even_pool_w, v_even_pool_scale, v_even_ws, v_even_bs, v_even_w_out, v_odd_norm, v_odd_w_in, v_odd_conv_w, v_odd_w_out, v_final_norm):
    x0 = x[0]
    target = loss_target[0]
    me = 4 * lax.axis_index("x") + 2 * lax.axis_index("y") + lax.axis_index("c")

    we_in, wp, we_out = all_gather(
        [even_w_in[0].astype(BF16), even_pool_w[0].astype(BF16), even_w_out[0].astype(BF16)], [1, 1, 0], "gather_even")
    odd_small = jnp.pad(odd_norm, ((0, 7), (0, 0))) + jnp.pad(odd_conv_w[0], ((1, 4), (0, 0)))
    wo_in, wo_out, odd_small = all_gather(
        [odd_w_in[0].astype(BF16), odd_w_out[0].astype(BF16), odd_small], [1, 0, 1], "gather_odd")
    g_odd, conv_w = odd_small[0:1], odd_small[1:4]
    scale = even_pool_scale
    ws = even_ws[0]
    bs_col = even_bs[0][:, :, None]
    g_fin = final_norm[None, :]

    z_e, h_e = norm_matmul(x0, even_norm, we_in, "even_in")
    y_e, pooled = even_mix(z_e, wp, scale, ws, bs_col, "even_mix")
    x1 = matmul_residual(y_e, we_out, x0, "even_out")
    z_o, h_o = norm_matmul(x1, g_odd, wo_in, "odd_in", tn=2048)
    attn = [attn_fwd(z_o, gi, f"attn_fwd_{gi}") for gi in range(3)]
    y_o, yc, lse_tot, conv = odd_mix([a[0] for a in attn], [a[1] for a in attn], z_o, conv_w, "odd_mix")
    x2 = matmul_residual(y_o, wo_out, x1, "odd_out")
    dx2, loss_blk, dg_fin = final_norm_loss(x2, g_fin, target, "final_norm_loss")

    dy_o, dwo_out = out_proj_bwd(dx2, wo_out, y_o, "odd_out_bwd")
    dz5_o, dyc, dd, dconv_w = odd_mix_bwd(dy_o, yc, conv, z_o, conv_w, "odd_mix_bwd")
    dqkv = [attn_bwd(z_o, dyc, lse_tot, dd, gi, f"attn_bwd_{gi}") for gi in range(3)]
    src_o = [(dqkv[0], ("slabs", 0)), (dqkv[1], ("slabs", 1)), (dqkv[2], ("slabs", 2)), (dz5_o, ("cols", 9, 5))]
    dwo_in = in_proj_dw(h_o, src_o, ODD_COLBLOCKS, "odd_in_dw")
    dx1, dg_odd = in_proj_dx(src_o, ODD_COLBLOCKS, wo_in, x1, g_odd, dx2, "odd_in_dx")

    dy_e, dwe_out = out_proj_bwd(dx1, we_out, y_e, "even_out_bwd")
    dz_e, dwp, dscale, dws, dbs_col = even_mix_bwd(dy_e, z_e, pooled, wp, scale, ws, bs_col, "even_mix_bwd")
    src_e = [(dz_e, ("cols", 0, 5))]
    dwe_in = in_proj_dw(h_e, src_e, 5, "even_in_dw")
    dx0, dg_even = in_proj_dx(src_e, 5, we_in, x0, even_norm, dx1, "even_in_dx")

    p_we_in, p_wp, p_we_out, p_wo_in, p_wo_out = scatter_partials(
        [dwe_in, dwp.astype(BF16), dwe_out, dwo_in, dwo_out], [1, 1, 0, 1, 0], "scatter_grads")
    small = jnp.concatenate([
        _rows128(dg_even), _rows128(dscale), _rows128(dws), _rows128(dbs_col), _rows128(dg_fin),
        _rows128(dg_odd), _rows128(dconv_w)], axis=0)
    n_small = small.shape[0]
    (small_all,) = all_gather([small], [0], "gather_small_grads")
    small_sum = sum_slots(small_all.reshape(N_DEV, n_small, 128), "sum_small_grads")
    g_even_norm = small_sum[0:8].reshape(1, 1024)
    g_scale = small_sum[8:16].reshape(1, 1024)
    g_ws = small_sum[16:528]
    g_bs = small_sum[528:532]
    g_final = small_sum[536:544].reshape(1, 1024)
    g_odd_norm = lax.dynamic_slice_in_dim(small_sum[544:552], me, 1, axis=0)
    g_conv = lax.dynamic_index_in_dim(small_sum[552:576].reshape(3, 8, 128), me, axis=1, keepdims=False)

    two_d = lambda a, r, c: a.reshape(r, c)
    sharded = [
        ("even_w_in", p_we_in, even_w_in, m_even_w_in, v_even_w_in, (1024, 640)),
        ("even_pool_w", p_wp.reshape(N_DEV, 128, 256), even_pool_w, m_even_pool_w, v_even_pool_w, (128, 256)),
        ("even_w_out", p_we_out, even_w_out, m_even_w_out, v_even_w_out, (256, 1024)),
        ("odd_w_in", p_wo_in, odd_w_in, m_odd_w_in, v_odd_w_in, (1024, 1792)),
        ("odd_w_out", p_wo_out, odd_w_out, m_odd_w_out, v_odd_w_out, (256, 1024)),
    ]
    plain = [
        ("even_norm", g_even_norm, even_norm, m_even_norm, v_even_norm, (1, 1024)),
        ("even_pool_scale", g_scale, even_pool_scale, m_even_pool_scale, v_even_pool_scale, (1, 1024)),
        ("even_ws", g_ws, even_ws, m_even_ws, v_even_ws, (512, 128)),
        ("even_bs", g_bs, even_bs, m_even_bs, v_even_bs, (4, 128)),
        ("odd_norm", g_odd_norm, odd_norm, m_odd_norm, v_odd_norm, (1, 128)),
        ("odd_conv_w", g_conv, odd_conv_w, m_odd_conv_w, v_odd_conv_w, (3, 128)),
        ("final_norm", g_final, final_norm, m_final_norm, v_final_norm, (1, 1024)),
    ]
    res = {}
    for name, g, w, m, v, (r, c) in sharded:
        outs = adamw(g, two_d(w, r, c), two_d(m, r, c), two_d(v, r, c), "adamw_" + name, slots=True)
        res[name] = [o.reshape(w.shape) for o in outs]
    for name, g, w, m, v, (r, c) in plain:
        outs = adamw(two_d(g, r, c), two_d(w, r, c), two_d(m, r, c), two_d(v, r, c), "adamw_" + name, slots=False)
        res[name] = [o.reshape(w.shape) for o in outs]

    loss = lax.psum(loss_blk[0, 0], ("x", "y", "c"))
    order = ["even_norm", "even_w_in", "even_pool_w", "even_pool_scale", "even_ws", "even_bs", "even_w_out",
             "odd_norm", "odd_w_in", "odd_conv_w", "odd_w_out", "final_norm"]
    return (loss, dx0[None], *[res[n][0] for n in order], *[res[n][1] for n in order],
            *[res[n][2] for n in order], *[res[n][3] for n in order])
```

```python
import functools

import jax
import jax.numpy as jnp
from jax import lax
from jax.experimental import pallas as pl
from jax.experimental.pallas import tpu as pltpu

F32 = jnp.float32
BF16 = jnp.bfloat16
MESH_ID = pl.DeviceIdType.MESH

EPS = 1e-6
NEG = -1e30
N_DEV = 8
POOL_SIZES = (2, 4, 8, 16)
DILATIONS = (1, 4, 16)
N_HEADS = 24
HEADS_PER_GROUP = 8
HEAD_DIM = 128
ATTN_BLOCK = 128
CHUNK = 128
CB = 1024
HALO = 16
ODD_COLBLOCKS = 14
ADAM_LR = 0.001
ADAM_B1 = 0.9
ADAM_B2 = 0.999
ADAM_EPS = 1e-08
ADAM_WD = 0.01
ADAM_STEP = 10
VMEM_LIMIT = 52 * 1024 * 1024


def _cp(*sem):
    return pltpu.CompilerParams(dimension_semantics=sem, vmem_limit_bytes=VMEM_LIMIT)


def _dot(a, b):
    return jnp.dot(a, b, preferred_element_type=F32)


def _dot_nt(a, b):
    return lax.dot_general(a, b, (((1,), (1,)), ((), ())), preferred_element_type=F32)


def _dot_tn(a, b):
    return lax.dot_general(a, b, (((0,), (0,)), ((), ())), preferred_element_type=F32)


def _sigmoid(x):
    return 1.0 / (1.0 + jnp.exp(-x))


def _silu_and_grad(x):
    s = _sigmoid(x)
    return x * s, s * (1.0 + x * (1.0 - s))


def _sds(shape, dtype):
    return jax.ShapeDtypeStruct(shape, dtype)


def norm_matmul(x, g, w, name, tm=512, tn=1024):
    t, d = x.shape
    n = w.shape[1]

    def body(x_ref, g_ref, w_ref, z_ref, h_ref):
        @pl.when(pl.program_id(1) == 0)
        def _():
            xv = x_ref[...]
            r = lax.rsqrt(jnp.mean(xv * xv, axis=-1, keepdims=True) + EPS)
            h_ref[...] = ((xv * r) * g_ref[...]).astype(BF16)

        z_ref[...] = _dot(h_ref[...], w_ref[...]).astype(BF16)

    return pl.pallas_call(
        body, name=name, grid=(t // tm, n // tn),
        in_specs=[pl.BlockSpec((tm, d), lambda i, j: (i, 0)),
                  pl.BlockSpec((1, d), lambda i, j: (0, 0)),
                  pl.BlockSpec((d, tn), lambda i, j: (0, j))],
        out_specs=[pl.BlockSpec((tm, tn), lambda i, j: (i, j)),
                   pl.BlockSpec((tm, d), lambda i, j: (i, 0))],
        out_shape=[_sds((t, n), BF16), _sds((t, d), BF16)],
        compiler_params=_cp("parallel", "arbitrary"),
    )(x, g, w)


def matmul_residual(y, w, x, name, tm=512):
    t, k = y.shape
    d = w.shape[1]

    def body(y_ref, w_ref, x_ref, o_ref):
        o_ref[...] = x_ref[...] + _dot(y_ref[...], w_ref[...])

    return pl.pallas_call(
        body, name=name, grid=(t // tm,),
        in_specs=[pl.BlockSpec((tm, k), lambda i: (i, 0)),
                  pl.BlockSpec((k, d), lambda i: (0, 0)),
                  pl.BlockSpec((tm, d), lambda i: (i, 0))],
        out_specs=pl.BlockSpec((tm, d), lambda i: (i, 0)),
        out_shape=_sds((t, d), F32),
        compiler_params=_cp("parallel"),
    )(y, w, x)


def out_proj_bwd(dx, w, y, name, tm=256):
    t, d = dx.shape
    k = w.shape[0]
    steps = t // tm

    def body(dx_ref, w_ref, y_ref, dy_ref, dw_ref, acc):
        i = pl.program_id(0)
        dxb = dx_ref[...].astype(BF16)
        dy_ref[...] = _dot_nt(dxb, w_ref[...]).astype(BF16)
        part = _dot_tn(y_ref[...], dxb)

        @pl.when(i == 0)
        def _():
            acc[...] = part

        @pl.when(i > 0)
        def _():
            acc[...] += part

        @pl.when(i == steps - 1)
        def _():
            dw_ref[...] = acc[...].astype(BF16)

    return pl.pallas_call(
        body, name=name, grid=(steps,),
        in_specs=[pl.BlockSpec((tm, d), lambda i: (i, 0)),
                  pl.BlockSpec((k, d), lambda i: (0, 0)),
                  pl.BlockSpec((tm, k), lambda i: (i, 0))],
        out_specs=[pl.BlockSpec((tm, k), lambda i: (i, 0)),
                   pl.BlockSpec((k, d), lambda i: (0, 0))],
        out_shape=[_sds((t, k), BF16), _sds((k, d), BF16)],
        scratch_shapes=[pltpu.VMEM((k, d), F32)],
        compiler_params=_cp("arbitrary"),
    )(dx, w, y)


def _source_block(kind, rows):
    if kind[0] == "cols":
        return (rows, CB)
    return (None, rows, CB)


def _source_active(kind, j):
    if kind[0] == "cols":
        _, first, n = kind
        return (j >= first) & (j < first + n), jnp.clip(j - first, 0, n - 1)
    _, group = kind
    return (j < 9) & (j % 3 == group), jnp.clip(j // 3, 0, 2)


def _source_index(kind, row_block, inner):
    if kind[0] == "cols":
        return (row_block, inner)
    return (inner, row_block, 0)


def in_proj_dw(h, sources, n_blocks, name, tk=1024):
    t, d = h.shape
    steps = t // tk
    ns = len(sources)

    def body(*refs):
        h_ref, src_refs, dw_ref, acc = refs[0], refs[1:1 + ns], refs[1 + ns], refs[2 + ns]
        j, k = pl.program_id(0), pl.program_id(1)
        for s, (_, kind) in enumerate(sources):
            active, _ = _source_active(kind, j)

            @pl.when(active)
            def _(s=s):
                part = _dot_tn(h_ref[...], src_refs[s][...])

                @pl.when(k == 0)
                def _():
                    acc[...] = part

                @pl.when(k > 0)
                def _():
                    acc[...] += part

        @pl.when(k == steps - 1)
        def _():
            dw_ref[...] = acc[...].astype(BF16)

    def src_spec(kind):
        def index(j, k):
            active, inner = _source_active(kind, j)
            return _source_index(kind, jnp.where(active, k, 0), inner)
        return pl.BlockSpec(_source_block(kind, tk), index)

    return pl.pallas_call(
        body, name=name, grid=(n_blocks, steps),
        in_specs=[pl.BlockSpec((tk, d), lambda j, k: (k, 0))] + [src_spec(kind) for _, kind in sources],
        out_specs=pl.BlockSpec((d, CB), lambda j, k: (0, j)),
        out_shape=_sds((d, n_blocks * CB), BF16),
        scratch_shapes=[pltpu.VMEM((d, CB), F32)],
        compiler_params=_cp("parallel", "arbitrary"),
    )(h, *[a for a, _ in sources])


def in_proj_dx(sources, n_blocks, w, x, g, dres, name, tm=512):
    t, d = x.shape
    ns = len(sources)

    def body(*refs):
        src_refs = refs[:ns]
        w_ref, x_ref, g_ref, dres_ref, dx_ref, dg_ref, acc = refs[ns:]
        i, p = pl.program_id(0), pl.program_id(1)
        for s, (_, kind) in enumerate(sources):
            active, _ = _source_active(kind, p)

            @pl.when(active)
            def _(s=s):
                part = _dot_nt(src_refs[s][...], w_ref[...])

                @pl.when(p == 0)
                def _():
                    acc[...] = part

                @pl.when(p > 0)
                def _():
                    acc[...] += part

        @pl.when(p == n_blocks - 1)
        def _():
            xv = x_ref[...]
            r = lax.rsqrt(jnp.mean(xv * xv, axis=-1, keepdims=True) + EPS)
            xn = xv * r
            dh = acc[...]
            dhg = dh * g_ref[...]
            dx_ref[...] = dres_ref[...] + r * (dhg - xn * jnp.mean(dhg * xn, axis=-1, keepdims=True))
            part = jnp.sum(dh * xn, axis=0, keepdims=True)

            @pl.when(i == 0)
            def _():
                dg_ref[...] = part

            @pl.when(i > 0)
            def _():
                dg_ref[...] += part

    def src_spec(kind):
        def index(i, p):
            if kind[0] == "cols":
                _, first, n = kind
                inner = jnp.clip(p - first, 0, n - 1)
            else:
                _, group = kind
                inner = jnp.clip((p - group + 2) // 3, 0, 2)
            return _source_index(kind, i, inner)
        return pl.BlockSpec(_source_block(kind, tm), index)

    return pl.pallas_call(
        body, name=name, grid=(t // tm, n_blocks),
        in_specs=[src_spec(kind) for _, kind in sources] + [
            pl.BlockSpec((d, CB), lambda i, p: (0, p)),
            pl.BlockSpec((tm, d), lambda i, p: (i, 0)),
            pl.BlockSpec((1, d), lambda i, p: (0, 0)),
            pl.BlockSpec((tm, d), lambda i, p: (i, 0))],
        out_specs=[pl.BlockSpec((tm, d), lambda i, p: (i, 0)),
                   pl.BlockSpec((1, d), lambda i, p: (0, 0))],
        out_shape=[_sds((t, d), F32), _sds((1, d), F32)],
        scratch_shapes=[pltpu.VMEM((tm, d), F32)],
        compiler_params=_cp("arbitrary", "arbitrary"),
    )(*[a for a, _ in sources], w, x, g, dres)


def _window_counts(first_row, rows, w):
    t = first_row + lax.broadcasted_iota(jnp.int32, (rows, 1), 0)
    return jnp.minimum(t + 1, w).astype(F32)


def _tril_bf16(ws_ref, g):
    r = lax.broadcasted_iota(jnp.int32, (CHUNK, CHUNK), 0)
    c = lax.broadcasted_iota(jnp.int32, (CHUNK, CHUNK), 1)
    return jnp.where(r >= c, ws_ref[g], 0.0).astype(BF16), r >= c


def even_mix(z, wp, scale, ws, bs_col, name, tm=256):
    t = z.shape[0]
    gd = CB // len(POOL_SIZES)

    def body(a_ref, ga_ref, u_ref, v_ref, gb_ref, wp_ref, sc_ref, ws_ref, bs_ref, y_ref, pooled_ref, aext):
        i = pl.program_id(0)

        @pl.when(i == 0)
        def _():
            aext[0:HALO, :] = jnp.zeros((HALO, CB), F32)

        @pl.when(i > 0)
        def _():
            aext[0:HALO, :] = aext[tm:tm + HALO, :]

        aext[HALO:HALO + tm, :] = a_ref[...].astype(F32)
        for g, w in enumerate(POOL_SIZES):
            cols = slice(g * gd, (g + 1) * gd)
            tok = aext[HALO:HALO + tm, cols]
            s = tok
            for k in range(1, w):
                s = s + aext[HALO - k:HALO - k + tm, cols]
            pooled = (s / _window_counts(i * tm, tm, w) - tok).astype(BF16)
            pooled_ref[:, cols] = pooled
            mixed = _dot(pooled, wp_ref[g])
            silu_a, _ = _silu_and_grad(ga_ref[:, cols].astype(F32))
            y_ref[:, cols] = (mixed * sc_ref[:, cols] * silu_a).astype(BF16)
        for g in range(4):
            cols = slice(g * gd, (g + 1) * gd)
            wg, _ = _tril_bf16(ws_ref, g)
            for c in range(tm // CHUNK):
                rows = slice(c * CHUNK, (c + 1) * CHUNK)
                m = _dot(wg, v_ref[rows, cols]) + bs_ref[g]
                silu_b, _ = _silu_and_grad(gb_ref[rows, cols].astype(F32))
                y_ref[rows, CB + g * gd:CB + (g + 1) * gd] = (
                    u_ref[rows, cols].astype(F32) * m * silu_b).astype(BF16)

    zspec = lambda cb: pl.BlockSpec((tm, CB), lambda i, cb=cb: (i, cb))
    full = lambda shape: pl.BlockSpec(shape, lambda i: (0,) * len(shape))
    return pl.pallas_call(
        body, name=name, grid=(t // tm,),
        in_specs=[zspec(0), zspec(1), zspec(2), zspec(3), zspec(4),
                  full(wp.shape), full(scale.shape), full(ws.shape), full(bs_col.shape)],
        out_specs=[pl.BlockSpec((tm, 2 * CB), lambda i: (i, 0)), pl.BlockSpec((tm, CB), lambda i: (i, 0))],
        out_shape=[_sds((t, 2 * CB), BF16), _sds((t, CB), BF16)],
        scratch_shapes=[pltpu.VMEM((HALO + tm, CB), F32)],
        compiler_params=_cp("arbitrary"),
    )(z, z, z, z, z, wp, scale, ws, bs_col)


def even_mix_bwd(dy, z, pooled, wp, scale, ws, bs_col, name, tm=256):
    t = z.shape[0]
    nt = t // tm
    gd = CB // len(POOL_SIZES)

    def body(dya_ref, dyb_ref, ga_ref, u_ref, v_ref, gb_ref, pooled_ref, wp_ref, sc_ref, ws_ref, bs_ref,
             dz_ref, dwp_ref, dsc_ref, dws_ref, dbs_ref, dpext):
        i = pl.program_id(0)
        tile = nt - 1 - i

        @pl.when(i == 0)
        def _():
            dpext[tm:tm + HALO, :] = jnp.zeros((HALO, CB), F32)
            dwp_ref[...] = jnp.zeros_like(dwp_ref)
            dsc_ref[...] = jnp.zeros_like(dsc_ref)
            dws_ref[...] = jnp.zeros_like(dws_ref)
            dbs_ref[...] = jnp.zeros_like(dbs_ref)

        @pl.when(i > 0)
        def _():
            dpext[tm:tm + HALO, :] = dpext[0:HALO, :]

        for g, w in enumerate(POOL_SIZES):
            cols = slice(g * gd, (g + 1) * gd)
            pooled_g = pooled_ref[:, cols]
            mixed = _dot(pooled_g, wp_ref[g])
            silu_a, dsilu_a = _silu_and_grad(ga_ref[:, cols].astype(F32))
            dya = dya_ref[:, cols].astype(F32)
            sc = sc_ref[:, cols]
            dmixed = (dya * sc * silu_a).astype(BF16)
            dsc_ref[:, cols] += jnp.sum(dya * mixed * silu_a, axis=0, keepdims=True)
            dz_ref[:, CB + g * gd:CB + (g + 1) * gd] = (dya * mixed * sc * dsilu_a).astype(BF16)
            dwp_ref[g] += _dot_tn(pooled_g, dmixed)
            dpooled = _dot_nt(dmixed, wp_ref[g])
            dpext[0:tm, cols] = dpooled / _window_counts(tile * tm, tm, w)
            s = dpext[0:tm, cols]
            for k in range(1, w):
                s = s + dpext[k:k + tm, cols]
            dz_ref[:, cols] = (s - dpooled).astype(BF16)
        for g in range(4):
            cols = slice(g * gd, (g + 1) * gd)
            wg, lower = _tril_bf16(ws_ref, g)
            dws_g = jnp.zeros((CHUNK, CHUNK), F32)
            dbs_g = jnp.zeros((CHUNK, 1), F32)
            for c in range(tm // CHUNK):
                rows = slice(c * CHUNK, (c + 1) * CHUNK)
                vb = v_ref[rows, cols]
                m = _dot(wg, vb) + bs_ref[g]
                gbv = gb_ref[rows, cols].astype(F32)
                silu_b, dsilu_b = _silu_and_grad(gbv)
                dyb = dyb_ref[rows, cols].astype(F32)
                uv = u_ref[rows, cols].astype(F32)
                dm = dyb * silu_b * uv
                dmb = dm.astype(BF16)
                dz_ref[rows, 2 * CB + g * gd:2 * CB + (g + 1) * gd] = (dyb * silu_b * m).astype(BF16)
                dz_ref[rows, 3 * CB + g * gd:3 * CB + (g + 1) * gd] = _dot_tn(wg, dmb).astype(BF16)
                dz_ref[rows, 4 * CB + g * gd:4 * CB + (g + 1) * gd] = (dyb * uv * m * dsilu_b).astype(BF16)
                dws_g = dws_g + _dot_nt(dmb, vb)
                dbs_g = dbs_g + jnp.sum(dm, axis=1, keepdims=True)
            dws_ref[g] += jnp.where(lower, dws_g, 0.0)
            dbs_ref[g] += dbs_g

    rev = lambda cb: pl.BlockSpec((tm, CB), lambda i, cb=cb: (nt - 1 - i, cb))
    full = lambda shape: pl.BlockSpec(shape, lambda i: (0,) * len(shape))
    return pl.pallas_call(
        body, name=name, grid=(nt,),
        in_specs=[rev(0), rev(1), rev(1), rev(2), rev(3), rev(4), rev(0),
                  full(wp.shape), full(scale.shape), full(ws.shape), full(bs_col.shape)],
        out_specs=[pl.BlockSpec((tm, 5 * CB), lambda i: (nt - 1 - i, 0)),
                   full(wp.shape), full(scale.shape), full(ws.shape), full(bs_col.shape)],
        out_shape=[_sds((t, 5 * CB), BF16), _sds(wp.shape, F32), _sds(scale.shape, F32),
                   _sds(ws.shape, F32), _sds(bs_col.shape, F32)],
        scratch_shapes=[pltpu.VMEM((tm + HALO, CB), F32)],
        compiler_params=_cp("arbitrary"),
    )(dy, dy, z, z, z, z, pooled, wp, scale, ws, bs_col)


def _slope(group, head):
    return float(2.0 ** (-8.0 * (group * HEADS_PER_GROUP + head + 1) / N_HEADS))


def _band(dilation):
    qi = lax.broadcasted_iota(jnp.int32, (ATTN_BLOCK, ATTN_BLOCK), 0)
    ki = lax.broadcasted_iota(jnp.int32, (ATTN_BLOCK, ATTN_BLOCK), 1)
    dist_prev = ((qi + ATTN_BLOCK - ki) * dilation).astype(F32)
    dist_cur = ((qi - ki) * dilation).astype(F32)
    return dist_prev, ki >= qi, dist_cur, ki <= qi


def _permute_rows(name, d, inverse, arrays, in_specs, out_shapes, out_specs, n_chunks, widths):
    rows = ATTN_BLOCK * d
    t = out_shapes[0].shape[-2]
    n = len(arrays)

    lanes = HEAD_DIM

    def body(*refs):
        for a, o, s, w in zip(refs[:n], refs[n:2 * n], refs[2 * n:], widths):
            for k in range(w // lanes):
                cols = slice(k * lanes, (k + 1) * lanes)
                if inverse:
                    for r in range(d):
                        s[k, pl.ds(r, ATTN_BLOCK, stride=d), :] = (
                            a[r * ATTN_BLOCK:(r + 1) * ATTN_BLOCK, cols].astype(F32))
                    o[:, cols] = s[k].astype(o.dtype)
                else:
                    s[k] = a[:, cols].astype(F32)
                    for r in range(d):
                        o[r * ATTN_BLOCK:(r + 1) * ATTN_BLOCK, cols] = (
                            s[k, pl.ds(r, ATTN_BLOCK, stride=d), :].astype(o.dtype))

    return pl.pallas_call(
        body, name=name, grid=(t // rows, n_chunks),
        in_specs=in_specs, out_specs=out_specs, out_shape=out_shapes,
        scratch_shapes=[pltpu.VMEM((w // lanes, rows, lanes), F32) for w in widths],
        compiler_params=_cp("parallel", "arbitrary"),
    )(*arrays)


def _chunk_width(d):
    return 4 * CB // d


def permute_qkv(z, group, name):
    d = DILATIONS[group]
    rows, cw = ATTN_BLOCK * d, _chunk_width(d)
    per = CB // cw
    src = pl.BlockSpec((rows, cw), lambda i, c: (i, (c // per * 3 + group) * per + c % per))
    dst = pl.BlockSpec((rows, cw), lambda i, c: (i, c))
    (out,) = _permute_rows(name, d, False, [z], [src], [_sds((z.shape[0], 3 * CB), z.dtype)], [dst], 3 * per, [cw])
    return out


def permute_mats(wide, narrow, group, inverse, name):
    d = DILATIONS[group]
    rows, cw = ATTN_BLOCK * d, _chunk_width(d)
    wide_spec = pl.BlockSpec((rows, cw), lambda i, c: (i, c))
    narrow_spec = pl.BlockSpec((rows, HEAD_DIM), lambda i, c: (i, 0))
    arrays = [wide] + list(narrow)
    specs = [wide_spec] + [narrow_spec] * len(narrow)
    return _permute_rows(name, d, inverse, arrays, specs, [_sds(a.shape, a.dtype) for a in arrays], specs,
                         CB // cw, [cw] + [HEAD_DIM] * len(narrow))


def permute_slabs(s, group, inverse, name):
    d = DILATIONS[group]
    rows, cw = ATTN_BLOCK * d, _chunk_width(d)
    per = CB // cw
    spec = pl.BlockSpec((None, rows, cw), lambda i, c: (c // per, i, c % per))
    (out,) = _permute_rows(name, d, inverse, [s], [spec], [_sds(s.shape, s.dtype)], [spec], 3 * per, [cw])
    return out


def _lane_pack(cols):
    rows = cols[0].shape[0]
    lane = lax.broadcasted_iota(jnp.int32, (rows, HEAD_DIM), 1)
    out = jnp.zeros((rows, HEAD_DIM), F32)
    for h, c in enumerate(cols):
        out = jnp.where(lane == h, c, out)
    return out


def _qkv_col(group, from_z):
    return (lambda which: which * 3 + group) if from_z else (lambda which: which)


def _scores(q, kp, kc, group, h, dist_p, ok_p, dist_c, ok_c):
    scale = HEAD_DIM ** -0.5
    slope = _slope(group, h)
    sp = jnp.where(ok_p, _dot_nt(q, kp) * scale - slope * dist_p, NEG)
    sc = jnp.where(ok_c, _dot_nt(q, kc) * scale - slope * dist_c, NEG)
    return sp, sc


def attn_fwd(qkv, group, name):
    t = qkv.shape[0]
    d = DILATIONS[group]
    col = _qkv_col(group, qkv.shape[1] != 3 * CB)
    heads = range(HEADS_PER_GROUP)

    def body(q_ref, kp_ref, kc_ref, vp_ref, vc_ref, o_ref, lse_ref):
        b = pl.program_id(0)
        dist_p, ok_p, dist_c, ok_c = _band(d)
        ok_p = jnp.logical_and(ok_p, b >= d)
        hs = [slice(h * HEAD_DIM, (h + 1) * HEAD_DIM) for h in heads]
        s = [_scores(q_ref[:, hs[h]], kp_ref[:, hs[h]], kc_ref[:, hs[h]], group, h, dist_p, ok_p, dist_c, ok_c)
             for h in heads]
        m = [jnp.maximum(jnp.max(sp, axis=1, keepdims=True), jnp.max(sc, axis=1, keepdims=True)) for sp, sc in s]
        e = [(jnp.exp(sp - m[h]), jnp.exp(sc - m[h])) for h, (sp, sc) in enumerate(s)]
        l = [jnp.sum(ep, axis=1, keepdims=True) + jnp.sum(ec, axis=1, keepdims=True) for ep, ec in e]
        inv = [1.0 / lh for lh in l]
        o = [_dot((ep * inv[h]).astype(BF16), vp_ref[:, hs[h]]) + _dot((ec * inv[h]).astype(BF16), vc_ref[:, hs[h]])
             for h, (ep, ec) in enumerate(e)]
        for h in heads:
            o_ref[:, hs[h]] = o[h].astype(BF16)
        lse_ref[...] = _lane_pack([m[h] + jnp.log(l[h]) for h in heads])

    blk = (ATTN_BLOCK, CB)
    cur = lambda which: pl.BlockSpec(blk, lambda b: (b, col(which)))
    prev = lambda which: pl.BlockSpec(blk, lambda b: (jnp.maximum(b - d, 0), col(which)))
    return pl.pallas_call(
        body, name=name, grid=(t // ATTN_BLOCK,),
        in_specs=[cur(0), prev(1), cur(1), prev(2), cur(2)],
        out_specs=[pl.BlockSpec(blk, lambda b: (b, 0)), pl.BlockSpec((ATTN_BLOCK, HEAD_DIM), lambda b: (b, 0))],
        out_shape=[_sds((t, CB), BF16), _sds((t, HEAD_DIM), F32)],
        compiler_params=_cp("parallel"),
    )(qkv, qkv, qkv, qkv, qkv)


def attn_bwd(z, dyc, lse, dd, group, name):
    t = z.shape[0]
    d = DILATIONS[group]
    nb = t // (ATTN_BLOCK * d)
    col = _qkv_col(group, z.shape[1] != 3 * CB)
    scale = HEAD_DIM ** -0.5
    heads = range(HEADS_PER_GROUP)

    def body(q_ref, kp_ref, kc_ref, vp_ref, vc_ref, dy_ref, lse_ref, dd_ref, out_ref, carry):
        n = pl.program_id(1)

        @pl.when(n == 0)
        def _():
            carry[...] = jnp.zeros_like(carry)

        @pl.when(n == nb)
        def _():
            out_ref[...] = carry[...].astype(BF16)

        @pl.when(n < nb)
        def _():
            dist_p, ok_p, dist_c, ok_c = _band(d)
            ok_p = jnp.logical_and(ok_p, n > 0)
            hs = [slice(h * HEAD_DIM, (h + 1) * HEAD_DIM) for h in heads]
            q = [q_ref[:, hs[h]] for h in heads]
            kp = [kp_ref[:, hs[h]] for h in heads]
            kc = [kc_ref[:, hs[h]] for h in heads]
            dy = [dy_ref[:, hs[h]] for h in heads]
            old = [[carry[w, :, hs[h]] for h in heads] for w in range(3)]
            s = [_scores(q[h], kp[h], kc[h], group, h, dist_p, ok_p, dist_c, ok_c) for h in heads]
            p = [(jnp.exp(sp - lse_ref[:, h:h + 1]), jnp.exp(sc - lse_ref[:, h:h + 1])) for h, (sp, sc) in enumerate(s)]
            ds = [((pp * (_dot_nt(dy[h], vp_ref[:, hs[h]]) - dd_ref[:, h:h + 1]) * scale).astype(BF16),
                   (pc * (_dot_nt(dy[h], vc_ref[:, hs[h]]) - dd_ref[:, h:h + 1]) * scale).astype(BF16))
                  for h, (pp, pc) in enumerate(p)]
            for h in heads:
                dsp, dsc = ds[h]
                pp, pc = p[h]
                out_ref[0, :, hs[h]] = old[0][h].astype(BF16)
                out_ref[1, :, hs[h]] = (old[1][h] + _dot_tn(dsp, q[h])).astype(BF16)
                out_ref[2, :, hs[h]] = (old[2][h] + _dot_tn(pp.astype(BF16), dy[h])).astype(BF16)
                carry[0, :, hs[h]] = _dot(dsp, kp[h]) + _dot(dsc, kc[h])
                carry[1, :, hs[h]] = _dot_tn(dsc, q[h])
                carry[2, :, hs[h]] = _dot_tn(pc.astype(BF16), dy[h])

    blk = (ATTN_BLOCK, CB)
    cur_row = lambda r, n: jnp.minimum(n, nb - 1) * d + r
    prev_row = lambda r, n: jnp.clip(n - 1, 0, nb - 1) * d + r
    cur = lambda which: pl.BlockSpec(blk, lambda r, n: (cur_row(r, n), col(which)))
    prev = lambda which: pl.BlockSpec(blk, lambda r, n: (prev_row(r, n), col(which)))
    narrow = pl.BlockSpec((ATTN_BLOCK, HEAD_DIM), lambda r, n: (cur_row(r, n), 0))
    return pl.pallas_call(
        body, name=name, grid=(d, nb + 1),
        in_specs=[cur(0), prev(1), cur(1), prev(2), cur(2),
                  pl.BlockSpec(blk, lambda r, n: (cur_row(r, n), 0)), narrow, narrow],
        out_specs=pl.BlockSpec((3, ATTN_BLOCK, CB), lambda r, n: (0, prev_row(r, n), 0)),
        out_shape=_sds((3, t, CB), BF16),
        scratch_shapes=[pltpu.VMEM((3, ATTN_BLOCK, CB), F32)],
        compiler_params=_cp("parallel", "arbitrary"),
    )(z, z, z, z, z, dyc, lse, dd)


def odd_mix(os_, lses, z, conv_w, name, tm=256):
    t = z.shape[0]

    def body(o0, o1, o2, l0, l1, l2, gc_ref, db_ref, dc_ref, dx_ref, gd_ref, cw_ref,
             y_ref, yc_ref, lse_ref, conv_ref, zext):
        i = pl.program_id(0)
        a0, a1, a2 = l0[...], l1[...], l2[...]
        m = jnp.maximum(jnp.maximum(a0, a1), a2)
        tot = m + jnp.log(jnp.exp(a0 - m) + jnp.exp(a1 - m) + jnp.exp(a2 - m))
        lse_ref[...] = tot
        w0, w1, w2 = jnp.exp(a0 - tot), jnp.exp(a1 - tot), jnp.exp(a2 - tot)
        for h in range(HEADS_PER_GROUP):
            hs = slice(h * HEAD_DIM, (h + 1) * HEAD_DIM)
            yc = (w0[:, h:h + 1] * o0[:, hs].astype(F32) + w1[:, h:h + 1] * o1[:, hs].astype(F32)
                  + w2[:, h:h + 1] * o2[:, hs].astype(F32))
            yc_ref[:, hs] = yc.astype(BF16)
            silu_c, _ = _silu_and_grad(gc_ref[:, hs].astype(F32))
            y_ref[:, hs] = (yc * silu_c).astype(BF16)

        @pl.when(i == 0)
        def _():
            zext[0:HALO, :] = jnp.zeros((HALO, CB), F32)

        @pl.when(i > 0)
        def _():
            zext[0:HALO, :] = zext[tm:tm + HALO, :]

        zext[HALO:HALO + tm, :] = dc_ref[...].astype(F32) * dx_ref[...].astype(F32)
        conv = (cw_ref[0:1, :] * zext[HALO - 2:HALO - 2 + tm, :] + cw_ref[1:2, :] * zext[HALO - 1:HALO - 1 + tm, :]
                + cw_ref[2:3, :] * zext[HALO:HALO + tm, :])
        conv_ref[...] = conv.astype(BF16)
        silu_d, _ = _silu_and_grad(gd_ref[...].astype(F32))
        y_ref[:, CB:2 * CB] = (db_ref[...].astype(F32) * conv * silu_d).astype(BF16)

    row = pl.BlockSpec((tm, CB), lambda i: (i, 0))
    narrow = pl.BlockSpec((tm, HEAD_DIM), lambda i: (i, 0))
    zspec = lambda cb: pl.BlockSpec((tm, CB), lambda i, cb=cb: (i, cb))
    return pl.pallas_call(
        body, name=name, grid=(t // tm,),
        in_specs=[row] * 3 + [narrow] * 3 + [zspec(9), zspec(10), zspec(11), zspec(12), zspec(13),
                                             pl.BlockSpec(conv_w.shape, lambda i: (0, 0))],
        out_specs=[pl.BlockSpec((tm, 2 * CB), lambda i: (i, 0)), row, narrow, row],
        out_shape=[_sds((t, 2 * CB), BF16), _sds((t, CB), BF16), _sds((t, HEAD_DIM), F32), _sds((t, CB), BF16)],
        scratch_shapes=[pltpu.VMEM((HALO + tm, CB), F32)],
        compiler_params=_cp("arbitrary"),
    )(*os_, *lses, z, z, z, z, z, conv_w)


def odd_mix_bwd(dy, yc, conv, z, conv_w, name, tm=256):
    t = z.shape[0]
    nt = t // tm

    def body(dyc_ref, dyd_ref, yc_ref, conv_ref, gc_ref, db_ref, dc_ref, dx_ref, gd_ref, cw_ref,
             dz_ref, dyo_ref, dd_ref, dcw_ref, dcext):
        i = pl.program_id(0)

        @pl.when(i == 0)
        def _():
            dcext[tm:tm + HALO, :] = jnp.zeros((HALO, CB), F32)
            dcw_ref[...] = jnp.zeros_like(dcw_ref)

        @pl.when(i > 0)
        def _():
            dcext[tm:tm + HALO, :] = dcext[0:HALO, :]

        silu_c, dsilu_c = _silu_and_grad(gc_ref[...].astype(F32))
        dyc = dyc_ref[...].astype(F32)
        ycv = yc_ref[...].astype(F32)
        dyo = dyc * silu_c
        dyo_ref[...] = dyo.astype(BF16)
        dz_ref[:, 0:CB] = (dyc * ycv * dsilu_c).astype(BF16)
        prod = dyo * ycv
        dd_ref[...] = _lane_pack([jnp.sum(prod[:, h * HEAD_DIM:(h + 1) * HEAD_DIM], axis=1, keepdims=True)
                                  for h in range(HEADS_PER_GROUP)])

        silu_d, dsilu_d = _silu_and_grad(gd_ref[...].astype(F32))
        dyd = dyd_ref[...].astype(F32)
        convv = conv_ref[...].astype(F32)
        dbv = db_ref[...].astype(F32)
        dz_ref[:, CB:2 * CB] = (dyd * convv * silu_d).astype(BF16)
        dz_ref[:, 4 * CB:5 * CB] = (dyd * dbv * convv * dsilu_d).astype(BF16)
        dcext[0:tm, :] = dyd * dbv * silu_d
        dcv, dxv = dc_ref[...].astype(F32), dx_ref[...].astype(F32)
        zc = dcv * dxv
        d0, d1, d2 = dcext[0:tm, :], dcext[1:1 + tm, :], dcext[2:2 + tm, :]
        dzc = cw_ref[2:3, :] * d0 + cw_ref[1:2, :] * d1 + cw_ref[0:1, :] * d2
        dz_ref[:, 2 * CB:3 * CB] = (dzc * dxv).astype(BF16)
        dz_ref[:, 3 * CB:4 * CB] = (dzc * dcv).astype(BF16)
        dcw_ref[0:1, :] += jnp.sum(zc * d2, axis=0, keepdims=True)
        dcw_ref[1:2, :] += jnp.sum(zc * d1, axis=0, keepdims=True)
        dcw_ref[2:3, :] += jnp.sum(zc * d0, axis=0, keepdims=True)

    rev = lambda cb: pl.BlockSpec((tm, CB), lambda i, cb=cb: (nt - 1 - i, cb))
    return pl.pallas_call(
        body, name=name, grid=(nt,),
        in_specs=[rev(0), rev(1), rev(0), rev(0), rev(9), rev(10), rev(11), rev(12), rev(13),
                  pl.BlockSpec(conv_w.shape, lambda i: (0, 0))],
        out_specs=[pl.BlockSpec((tm, 5 * CB), lambda i: (nt - 1 - i, 0)), rev(0),
                   pl.BlockSpec((tm, HEAD_DIM), lambda i: (nt - 1 - i, 0)),
                   pl.BlockSpec(conv_w.shape, lambda i: (0, 0))],
        out_shape=[_sds((t, 5 * CB), BF16), _sds((t, CB), BF16), _sds((t, HEAD_DIM), F32), _sds(conv_w.shape, F32)],
        scratch_shapes=[pltpu.VMEM((tm + HALO, CB), F32)],
        compiler_params=_cp("arbitrary"),
    )(dy, dy, yc, conv, z, z, z, z, z, conv_w)


def final_norm_loss(x, g, target, name, tm=512):
    t, d = x.shape

    def body(x_ref, g_ref, t_ref, dx_ref, loss_ref, dg_ref):
        i = pl.program_id(0)
        xv = x_ref[...]
        r = lax.rsqrt(jnp.mean(xv * xv, axis=-1, keepdims=True) + EPS)
        xn = xv * r
        gv = g_ref[...]
        err = xn * gv - t_ref[...]
        loss = 0.5 * jnp.sum(jnp.mean(err * err, axis=-1, keepdims=True), axis=0, keepdims=True)
        dy = err * (1.0 / d)
        dyg = dy * gv
        dx_ref[...] = r * (dyg - xn * jnp.mean(dyg * xn, axis=-1, keepdims=True))
        dg = jnp.sum(dy * xn, axis=0, keepdims=True)

        @pl.when(i == 0)
        def _():
            loss_ref[...] = jnp.broadcast_to(loss, loss_ref.shape)
            dg_ref[...] = dg

        @pl.when(i > 0)
        def _():
            loss_ref[...] += jnp.broadcast_to(loss, loss_ref.shape)
            dg_ref[...] += dg

    return pl.pallas_call(
        body, name=name, grid=(t // tm,),
        in_specs=[pl.BlockSpec((tm, d), lambda i: (i, 0)), pl.BlockSpec((1, d), lambda i: (0, 0)),
                  pl.BlockSpec((tm, d), lambda i: (i, 0))],
        out_specs=[pl.BlockSpec((tm, d), lambda i: (i, 0)), pl.BlockSpec((8, 128), lambda i: (0, 0)),
                   pl.BlockSpec((1, d), lambda i: (0, 0))],
        out_shape=[_sds((t, d), F32), _sds((8, 128), F32), _sds((1, d), F32)],
        compiler_params=_cp("arbitrary"),
    )(x, g, target)


def _position():
    x, y, c = lax.axis_index("x"), lax.axis_index("y"), lax.axis_index("c")
    return x, y, c, 4 * x + 2 * y + c


def _peer(x, y, c, k):
    px = 1 - x if k & 4 else x
    py = 1 - y if k & 2 else y
    pc = 1 - c if k & 1 else c
    return (px, py, pc), 4 * px + 2 * py + pc


def _block(ref, axis, size, idx):
    index = [slice(None)] * len(ref.shape)
    index[axis] = pl.ds(idx * size, size)
    return ref.at[tuple(index)]


def all_gather(shards, axes, name):
    n = len(shards)
    fulls = []
    for s, ax in zip(shards, axes):
        shape = list(s.shape)
        shape[ax] *= N_DEV
        fulls.append(_sds(tuple(shape), s.dtype))

    def body(*refs):
        ins, outs = refs[:n], refs[n:2 * n]
        send_sems, recv_sems, local_sems = refs[2 * n:]
        x, y, c, me = _position()
        local = []
        for p in range(n):
            size = ins[p].shape[axes[p]]
            cp = pltpu.make_async_copy(ins[p], _block(outs[p], axes[p], size, me), local_sems.at[p])
            cp.start()
            local.append(cp)
        for k in range(1, N_DEV):
            peer, _ = _peer(x, y, c, k)
            for p in range(n):
                size = ins[p].shape[axes[p]]
                pltpu.make_async_remote_copy(
                    src_ref=ins[p], dst_ref=_block(outs[p], axes[p], size, me),
                    send_sem=send_sems.at[p], recv_sem=recv_sems.at[p],
                    device_id=peer, device_id_type=MESH_ID).start()
        for p in range(n):
            size = ins[p].shape[axes[p]]
            seven = _block(outs[p], axes[p], (N_DEV - 1) * size, 0)
            pltpu.make_async_remote_copy(
                src_ref=seven, dst_ref=seven, send_sem=send_sems.at[p], recv_sem=recv_sems.at[p],
                device_id=(x, y, c), device_id_type=MESH_ID).wait()
            local[p].wait()

    any_spec = pl.BlockSpec(memory_space=pl.ANY)
    return pl.pallas_call(
        body, name=name,
        in_specs=[any_spec] * n, out_specs=[any_spec] * n, out_shape=fulls,
        scratch_shapes=[pltpu.SemaphoreType.DMA((n,)), pltpu.SemaphoreType.DMA((n,)), pltpu.SemaphoreType.DMA((n,))],
    )(*shards)


def scatter_partials(fulls, axes, name):
    n = len(fulls)
    outs_shape = []
    for f, ax in zip(fulls, axes):
        shape = list(f.shape)
        shape[ax] //= N_DEV
        outs_shape.append(_sds((N_DEV, *shape), f.dtype))

    def body(*refs):
        ins, outs = refs[:n], refs[n:2 * n]
        send_sems, recv_sems, local_sems = refs[2 * n:]
        x, y, c, me = _position()
        local = []
        for p in range(n):
            size = ins[p].shape[axes[p]] // N_DEV
            cp = pltpu.make_async_copy(_block(ins[p], axes[p], size, me), outs[p].at[me], local_sems.at[p])
            cp.start()
            local.append(cp)
        for k in range(1, N_DEV):
            peer, pj = _peer(x, y, c, k)
            for p in range(n):
                size = ins[p].shape[axes[p]] // N_DEV
                pltpu.make_async_remote_copy(
                    src_ref=_block(ins[p], axes[p], size, pj), dst_ref=outs[p].at[me],
                    send_sem=send_sems.at[p], recv_sem=recv_sems.at[p],
                    device_id=peer, device_id_type=MESH_ID).start()
        for p in range(n):
            seven = outs[p].at[pl.ds(0, N_DEV - 1)]
            pltpu.make_async_remote_copy(
                src_ref=seven, dst_ref=seven, send_sem=send_sems.at[p], recv_sem=recv_sems.at[p],
                device_id=(x, y, c), device_id_type=MESH_ID).wait()
            local[p].wait()

    any_spec = pl.BlockSpec(memory_space=pl.ANY)
    return pl.pallas_call(
        body, name=name,
        in_specs=[any_spec] * n, out_specs=[any_spec] * n, out_shape=outs_shape,
        scratch_shapes=[pltpu.SemaphoreType.DMA((n,)), pltpu.SemaphoreType.DMA((n,)), pltpu.SemaphoreType.DMA((n,))],
    )(*fulls)


def _adam_math(g, w, m, v):
    m2 = ADAM_B1 * m + (1.0 - ADAM_B1) * g
    v2 = ADAM_B2 * v + (1.0 - ADAM_B2) * (g * g)
    m_hat = m2 / (1.0 - ADAM_B1 ** ADAM_STEP)
    v_hat = v2 / (1.0 - ADAM_B2 ** ADAM_STEP)
    delta = -ADAM_LR * (m_hat / (jnp.sqrt(v_hat) + ADAM_EPS) + ADAM_WD * w)
    return delta, m2, v2


def adamw(g, w, m, v, name, slots, tr=128):
    r, c = w.shape
    tr = min(tr, r)

    def body(g_ref, w_ref, m_ref, v_ref, go_ref, d_ref, mo_ref, vo_ref):
        if slots:
            gs = g_ref[0].astype(F32)
            for s in range(1, N_DEV):
                gs = gs + g_ref[s].astype(F32)
        else:
            gs = g_ref[...]
        go_ref[...] = gs
        d_ref[...], mo_ref[...], vo_ref[...] = _adam_math(gs, w_ref[...], m_ref[...], v_ref[...])

    row = pl.BlockSpec((tr, c), lambda i: (i, 0))
    gspec = pl.BlockSpec((N_DEV, tr, c), lambda i: (0, i, 0)) if slots else row
    return pl.pallas_call(
        body, name=name, grid=(r // tr,),
        in_specs=[gspec, row, row, row], out_specs=[row] * 4, out_shape=[_sds((r, c), F32)] * 4,
        compiler_params=_cp("parallel"),
    )(g, w, m, v)


def sum_slots(g, name):
    _, r, c = g.shape

    def body(g_ref, o_ref):
        gs = g_ref[0]
        for s in range(1, N_DEV):
            gs = gs + g_ref[s]
        o_ref[...] = gs

    return pl.pallas_call(
        body, name=name, grid=(1,),
        in_specs=[pl.BlockSpec((N_DEV, r, c), lambda i: (0, 0, 0))],
        out_specs=pl.BlockSpec((r, c), lambda i: (0, 0)), out_shape=_sds((r, c), F32),
        compiler_params=_cp("arbitrary"),
    )(g)


def _rows128(a, pad_to=8):
    a = a.reshape(-1, 128)
    pad = (-a.shape[0]) % pad_to
    return jnp.pad(a, ((0, pad), (0, 0))) if pad else a


def kernel(x, even_norm, even_w_in, even_pool_w, even_pool_scale, even_ws, even_bs, even_w_out, odd_norm, odd_w_in, odd_conv_w, odd_w_out, final_norm, loss_target, m_even_norm, m_even_w_in, m_even_pool_w, m_even_pool_scale, m_even_ws, m_even_bs, m_even_w_out, m_odd_norm, m_odd_w_in, m_odd_conv_w, m_odd_w_out, m_final_norm, v_even_norm, v_even_w_in, v_even_pool_w, v_even_pool_scale, v_even_ws, v_even_bs, v_even_w_out, v_odd_norm, v_odd_w_in, v_odd_conv_w, v_odd_w_out, v_final_norm):
    x0 = x[0]
    target = loss_target[0]
    me = 4 * lax.axis_index("x") + 2 * lax.axis_index("y") + lax.axis_index("c")

    we_in, wp, we_out = all_gather(
        [even_w_in[0].astype(BF16), even_pool_w[0].astype(BF16), even_w_out[0].astype(BF16)], [1, 1, 0], "gather_even")
    odd_small = jnp.pad(odd_norm, ((0, 7), (0, 0))) + jnp.pad(odd_conv_w[0], ((1, 4), (0, 0)))
    wo_in, wo_out, odd_small = all_gather(
        [odd_w_in[0].astype(BF16), odd_w_out[0].astype(BF16), odd_small], [1, 0, 1], "gather_odd")
    g_odd, conv_w = odd_small[0:1], odd_small[1:4]
    scale = even_pool_scale
    ws = even_ws[0]
    bs_col = even_bs[0][:, :, None]
    g_fin = final_norm[None, :]

    z_e, h_e = norm_matmul(x0, even_norm, we_in, "even_in")
    y_e, pooled = even_mix(z_e, wp, scale, ws, bs_col, "even_mix")
    x1 = matmul_residual(y_e, we_out, x0, "even_out")
    z_o, h_o = norm_matmul(x1, g_odd, wo_in, "odd_in", tn=2048)
    qkv = [z_o] + [permute_qkv(z_o, gi, f"dilate_qkv_{gi}") for gi in (1, 2)]
    attn = [attn_fwd(qkv[gi], gi, f"attn_fwd_{gi}") for gi in range(3)]
    for gi in (1, 2):
        attn[gi] = permute_mats(attn[gi][0], [attn[gi][1]], gi, True, f"undilate_attn_{gi}")
    y_o, yc, lse_tot, conv = odd_mix([a[0] for a in attn], [a[1] for a in attn], z_o, conv_w, "odd_mix")
    x2 = matmul_residual(y_o, wo_out, x1, "odd_out")
    dx2, loss_blk, dg_fin = final_norm_loss(x2, g_fin, target, "final_norm_loss")

    dy_o, dwo_out = out_proj_bwd(dx2, wo_out, y_o, "odd_out_bwd")
    dz5_o, dyc, dd, dconv_w = odd_mix_bwd(dy_o, yc, conv, z_o, conv_w, "odd_mix_bwd")
    dqkv = [attn_bwd(z_o, dyc, lse_tot, dd, 0, "attn_bwd_0")]
    for gi in (1, 2):
        dyc_g, lse_g, dd_g = permute_mats(dyc, [lse_tot, dd], gi, False, f"dilate_dy_{gi}")
        dqkv.append(permute_slabs(attn_bwd(qkv[gi], dyc_g, lse_g, dd_g, gi, f"attn_bwd_{gi}"),
                                  gi, True, f"undilate_dqkv_{gi}"))
    src_o = [(dqkv[0], ("slabs", 0)), (dqkv[1], ("slabs", 1)), (dqkv[2], ("slabs", 2)), (dz5_o, ("cols", 9, 5))]
    dwo_in = in_proj_dw(h_o, src_o, ODD_COLBLOCKS, "odd_in_dw")
    dx1, dg_odd = in_proj_dx(src_o, ODD_COLBLOCKS, wo_in, x1, g_odd, dx2, "odd_in_dx")

    dy_e, dwe_out = out_proj_bwd(dx1, we_out, y_e, "even_out_bwd")
    dz_e, dwp, dscale, dws, dbs_col = even_mix_bwd(dy_e, z_e, pooled, wp, scale, ws, bs_col, "even_mix_bwd")
    src_e = [(dz_e, ("cols", 0, 5))]
    dwe_in = in_proj_dw(h_e, src_e, 5, "even_in_dw")
    dx0, dg_even = in_proj_dx(src_e, 5, we_in, x0, even_norm, dx1, "even_in_dx")

    p_we_in, p_wp, p_we_out, p_wo_in, p_wo_out = scatter_partials(
        [dwe_in, dwp.astype(BF16), dwe_out, dwo_in, dwo_out], [1, 1, 0, 1, 0], "scatter_grads")
    small = jnp.concatenate([
        _rows128(dg_even), _rows128(dscale), _rows128(dws), _rows128(dbs_col), _rows128(dg_fin),
        _rows128(dg_odd), _rows128(dconv_w)], axis=0)
    n_small = small.shape[0]
    (small_all,) = all_gather([small], [0], "gather_small_grads")
    small_sum = sum_slots(small_all.reshape(N_DEV, n_small, 128), "sum_small_grads")
    g_even_norm = small_sum[0:8].reshape(1, 1024)
    g_scale = small_sum[8:16].reshape(1, 1024)
    g_ws = small_sum[16:528]
    g_bs = small_sum[528:532]
    g_final = small_sum[536:544].reshape(1, 1024)
    g_odd_norm = lax.dynamic_slice_in_dim(small_sum[544:552], me, 1, axis=0)
    g_conv = lax.dynamic_index_in_dim(small_sum[552:576].reshape(3, 8, 128), me, axis=1, keepdims=False)

    two_d = lambda a, r, c: a.reshape(r, c)
    sharded = [
        ("even_w_in", p_we_in, even_w_in, m_even_w_in, v_even_w_in, (1024, 640)),
        ("even_pool_w", p_wp.reshape(N_DEV, 128, 256), even_pool_w, m_even_pool_w, v_even_pool_w, (128, 256)),
        ("even_w_out", p_we_out, even_w_out, m_even_w_out, v_even_w_out, (256, 1024)),
        ("odd_w_in", p_wo_in, odd_w_in, m_odd_w_in, v_odd_w_in, (1024, 1792)),
        ("odd_w_out", p_wo_out, odd_w_out, m_odd_w_out, v_odd_w_out, (256, 1024)),
    ]
    plain = [
        ("even_norm", g_even_norm, even_norm, m_even_norm, v_even_norm, (1, 1024)),
        ("even_pool_scale", g_scale, even_pool_scale, m_even_pool_scale, v_even_pool_scale, (1, 1024)),
        ("even_ws", g_ws, even_ws, m_even_ws, v_even_ws, (512, 128)),
        ("even_bs", g_bs, even_bs, m_even_bs, v_even_bs, (4, 128)),
        ("odd_norm", g_odd_norm, odd_norm, m_odd_norm, v_odd_norm, (1, 128)),
        ("odd_conv_w", g_conv, odd_conv_w, m_odd_conv_w, v_odd_conv_w, (3, 128)),
        ("final_norm", g_final, final_norm, m_final_norm, v_final_norm, (1, 1024)),
    ]
    res = {}
    for name, g, w, m, v, (r, c) in sharded:
        outs = adamw(g, two_d(w, r, c), two_d(m, r, c), two_d(v, r, c), "adamw_" + name, slots=True)
        res[name] = [o.reshape(w.shape) for o in outs]
    for name, g, w, m, v, (r, c) in plain:
        outs = adamw(two_d(g, r, c), two_d(w, r, c), two_d(m, r, c), two_d(v, r, c), "adamw_" + name, slots=False)
        res[name] = [o.reshape(w.shape) for o in outs]

    loss = lax.psum(loss_blk[0, 0], ("x", "y", "c"))
    order = ["even_norm", "even_w_in", "even_pool_w", "even_pool_scale", "even_ws", "even_bs", "even_w_out",
             "odd_norm", "odd_w_in", "odd_conv_w", "odd_w_out", "final_norm"]
    return (loss, dx0[None], *[res[n][0] for n in order], *[res[n][1] for n in order],
            *[res[n][2] for n in order], *[res[n][3] for n in order])
```

```python
import functools

import jax
import jax.numpy as jnp
from jax import lax
from jax.experimental import pallas as pl
from jax.experimental.pallas import tpu as pltpu

F32 = jnp.float32
BF16 = jnp.bfloat16
MESH_ID = pl.DeviceIdType.MESH

EPS = 1e-6
NEG = -1e30
N_DEV = 8
POOL_SIZES = (2, 4, 8, 16)
DILATIONS = (1, 4, 16)
N_HEADS = 24
HEADS_PER_GROUP = 8
HEAD_DIM = 128
ATTN_BLOCK = 128
CHUNK = 128
CB = 1024
HALO = 16
ODD_COLBLOCKS = 14
ADAM_LR = 0.001
ADAM_B1 = 0.9
ADAM_B2 = 0.999
ADAM_EPS = 1e-08
ADAM_WD = 0.01
ADAM_STEP = 10
VMEM_LIMIT = 52 * 1024 * 1024


def _cp(*sem):
    return pltpu.CompilerParams(dimension_semantics=sem, vmem_limit_bytes=VMEM_LIMIT)


def _dot(a, b):
    return jnp.dot(a, b, preferred_element_type=F32)


def _dot_nt(a, b):
    return lax.dot_general(a, b, (((1,), (1,)), ((), ())), preferred_element_type=F32)


def _dot_tn(a, b):
    return lax.dot_general(a, b, (((0,), (0,)), ((), ())), preferred_element_type=F32)


def _sigmoid(x):
    return 1.0 / (1.0 + jnp.exp(-x))


def _silu_and_grad(x):
    s = _sigmoid(x)
    return x * s, s * (1.0 + x * (1.0 - s))


def _sds(shape, dtype):
    return jax.ShapeDtypeStruct(shape, dtype)


def norm_matmul(x, g, w, name, tm=512, tn=1024):
    t, d = x.shape
    n = w.shape[1]

    def body(x_ref, g_ref, w_ref, z_ref, h_ref):
        @pl.when(pl.program_id(1) == 0)
        def _():
            xv = x_ref[...]
            r = lax.rsqrt(jnp.mean(xv * xv, axis=-1, keepdims=True) + EPS)
            h_ref[...] = ((xv * r) * g_ref[...]).astype(BF16)

        z_ref[...] = _dot(h_ref[...], w_ref[...]).astype(BF16)

    return pl.pallas_call(
        body, name=name, grid=(t // tm, n // tn),
        in_specs=[pl.BlockSpec((tm, d), lambda i, j: (i, 0)),
                  pl.BlockSpec((1, d), lambda i, j: (0, 0)),
                  pl.BlockSpec((d, tn), lambda i, j: (0, j))],
        out_specs=[pl.BlockSpec((tm, tn), lambda i, j: (i, j)),
                   pl.BlockSpec((tm, d), lambda i, j: (i, 0))],
        out_shape=[_sds((t, n), BF16), _sds((t, d), BF16)],
        compiler_params=_cp("parallel", "arbitrary"),
    )(x, g, w)


def matmul_residual(y, w, x, name, tm=512):
    t, k = y.shape
    d = w.shape[1]

    def body(y_ref, w_ref, x_ref, o_ref):
        o_ref[...] = x_ref[...] + _dot(y_ref[...], w_ref[...])

    return pl.pallas_call(
        body, name=name, grid=(t // tm,),
        in_specs=[pl.BlockSpec((tm, k), lambda i: (i, 0)),
                  pl.BlockSpec((k, d), lambda i: (0, 0)),
                  pl.BlockSpec((tm, d), lambda i: (i, 0))],
        out_specs=pl.BlockSpec((tm, d), lambda i: (i, 0)),
        out_shape=_sds((t, d), F32),
        compiler_params=_cp("parallel"),
    )(y, w, x)


def out_proj_bwd(dx, w, y, name, tm=256):
    t, d = dx.shape
    k = w.shape[0]
    steps = t // tm

    def body(dx_ref, w_ref, y_ref, dy_ref, dw_ref, acc):
        i = pl.program_id(0)
        dxb = dx_ref[...].astype(BF16)
        dy_ref[...] = _dot_nt(dxb, w_ref[...]).astype(BF16)
        part = _dot_tn(y_ref[...], dxb)

        @pl.when(i == 0)
        def _():
            acc[...] = part

        @pl.when(i > 0)
        def _():
            acc[...] += part

        @pl.when(i == steps - 1)
        def _():
            dw_ref[...] = acc[...].astype(BF16)

    return pl.pallas_call(
        body, name=name, grid=(steps,),
        in_specs=[pl.BlockSpec((tm, d), lambda i: (i, 0)),
                  pl.BlockSpec((k, d), lambda i: (0, 0)),
                  pl.BlockSpec((tm, k), lambda i: (i, 0))],
        out_specs=[pl.BlockSpec((tm, k), lambda i: (i, 0)),
                   pl.BlockSpec((k, d), lambda i: (0, 0))],
        out_shape=[_sds((t, k), BF16), _sds((k, d), BF16)],
        scratch_shapes=[pltpu.VMEM((k, d), F32)],
        compiler_params=_cp("arbitrary"),
    )(dx, w, y)


def _source_block(kind, rows):
    if kind[0] == "cols":
        return (rows, CB)
    return (None, rows, CB)


def _source_active(kind, j):
    if kind[0] == "cols":
        _, first, n = kind
        return (j >= first) & (j < first + n), jnp.clip(j - first, 0, n - 1)
    _, group = kind
    return (j < 9) & (j % 3 == group), jnp.clip(j // 3, 0, 2)


def _source_index(kind, row_block, inner):
    if kind[0] == "cols":
        return (row_block, inner)
    return (inner, row_block, 0)


def in_proj_dw(h, sources, n_blocks, name, tk=1024):
    t, d = h.shape
    steps = t // tk
    ns = len(sources)

    def body(*refs):
        h_ref, src_refs, dw_ref, acc = refs[0], refs[1:1 + ns], refs[1 + ns], refs[2 + ns]
        j, k = pl.program_id(0), pl.program_id(1)
        for s, (_, kind) in enumerate(sources):
            active, _ = _source_active(kind, j)

            @pl.when(active)
            def _(s=s):
                part = _dot_tn(h_ref[...], src_refs[s][...])

                @pl.when(k == 0)
                def _():
                    acc[...] = part

                @pl.when(k > 0)
                def _():
                    acc[...] += part

        @pl.when(k == steps - 1)
        def _():
            dw_ref[...] = acc[...].astype(BF16)

    def src_spec(kind):
        def index(j, k):
            active, inner = _source_active(kind, j)
            return _source_index(kind, jnp.where(active, k, 0), inner)
        return pl.BlockSpec(_source_block(kind, tk), index)

    return pl.pallas_call(
        body, name=name, grid=(n_blocks, steps),
        in_specs=[pl.BlockSpec((tk, d), lambda j, k: (k, 0))] + [src_spec(kind) for _, kind in sources],
        out_specs=pl.BlockSpec((d, CB), lambda j, k: (0, j)),
        out_shape=_sds((d, n_blocks * CB), BF16),
        scratch_shapes=[pltpu.VMEM((d, CB), F32)],
        compiler_params=_cp("parallel", "arbitrary"),
    )(h, *[a for a, _ in sources])


def in_proj_dx(sources, n_blocks, w, x, g, dres, name, tm=512):
    t, d = x.shape
    ns = len(sources)

    def body(*refs):
        src_refs = refs[:ns]
        w_ref, x_ref, g_ref, dres_ref, dx_ref, dg_ref, acc = refs[ns:]
        i, p = pl.program_id(0), pl.program_id(1)
        for s, (_, kind) in enumerate(sources):
            active, _ = _source_active(kind, p)

            @pl.when(active)
            def _(s=s):
                part = _dot_nt(src_refs[s][...], w_ref[...])

                @pl.when(p == 0)
                def _():
                    acc[...] = part

                @pl.when(p > 0)
                def _():
                    acc[...] += part

        @pl.when(p == n_blocks - 1)
        def _():
            xv = x_ref[...]
            r = lax.rsqrt(jnp.mean(xv * xv, axis=-1, keepdims=True) + EPS)
            xn = xv * r
            dh = acc[...]
            dhg = dh * g_ref[...]
            dx_ref[...] = dres_ref[...] + r * (dhg - xn * jnp.mean(dhg * xn, axis=-1, keepdims=True))
            part = jnp.sum(dh * xn, axis=0, keepdims=True)

            @pl.when(i == 0)
            def _():
                dg_ref[...] = part

            @pl.when(i > 0)
            def _():
                dg_ref[...] += part

    def src_spec(kind):
        def index(i, p):
            if kind[0] == "cols":
                _, first, n = kind
                inner = jnp.clip(p - first, 0, n - 1)
            else:
                _, group = kind
                inner = jnp.clip((p - group + 2) // 3, 0, 2)
            return _source_index(kind, i, inner)
        return pl.BlockSpec(_source_block(kind, tm), index)

    return pl.pallas_call(
        body, name=name, grid=(t // tm, n_blocks),
        in_specs=[src_spec(kind) for _, kind in sources] + [
            pl.BlockSpec((d, CB), lambda i, p: (0, p)),
            pl.BlockSpec((tm, d), lambda i, p: (i, 0)),
            pl.BlockSpec((1, d), lambda i, p: (0, 0)),
            pl.BlockSpec((tm, d), lambda i, p: (i, 0))],
        out_specs=[pl.BlockSpec((tm, d), lambda i, p: (i, 0)),
                   pl.BlockSpec((1, d), lambda i, p: (0, 0))],
        out_shape=[_sds((t, d), F32), _sds((1, d), F32)],
        scratch_shapes=[pltpu.VMEM((tm, d), F32)],
        compiler_params=_cp("arbitrary", "arbitrary"),
    )(*[a for a, _ in sources], w, x, g, dres)


def _window_counts(first_row, rows, w):
    t = first_row + lax.broadcasted_iota(jnp.int32, (rows, 1), 0)
    return jnp.minimum(t + 1, w).astype(F32)


def _tril_bf16(ws_ref, g):
    r = lax.broadcasted_iota(jnp.int32, (CHUNK, CHUNK), 0)
    c = lax.broadcasted_iota(jnp.int32, (CHUNK, CHUNK), 1)
    return jnp.where(r >= c, ws_ref[g], 0.0).astype(BF16), r >= c


def even_mix(z, wp, scale, ws, bs_col, name, tm=256):
    t = z.shape[0]
    gd = CB // len(POOL_SIZES)

    def body(a_ref, ga_ref, u_ref, v_ref, gb_ref, wp_ref, sc_ref, ws_ref, bs_ref, y_ref, pooled_ref, aext):
        i = pl.program_id(0)

        @pl.when(i == 0)
        def _():
            aext[0:HALO, :] = jnp.zeros((HALO, CB), F32)

        @pl.when(i > 0)
        def _():
            aext[0:HALO, :] = aext[tm:tm + HALO, :]

        aext[HALO:HALO + tm, :] = a_ref[...].astype(F32)
        for g, w in enumerate(POOL_SIZES):
            cols = slice(g * gd, (g + 1) * gd)
            tok = aext[HALO:HALO + tm, cols]
            s = tok
            for k in range(1, w):
                s = s + aext[HALO - k:HALO - k + tm, cols]
            pooled = (s / _window_counts(i * tm, tm, w) - tok).astype(BF16)
            pooled_ref[:, cols] = pooled
            mixed = _dot(pooled, wp_ref[g])
            silu_a, _ = _silu_and_grad(ga_ref[:, cols].astype(F32))
            y_ref[:, cols] = (mixed * sc_ref[:, cols] * silu_a).astype(BF16)
        for g in range(4):
            cols = slice(g * gd, (g + 1) * gd)
            wg, _ = _tril_bf16(ws_ref, g)
            for c in range(tm // CHUNK):
                rows = slice(c * CHUNK, (c + 1) * CHUNK)
                m = _dot(wg, v_ref[rows, cols]) + bs_ref[g]
                silu_b, _ = _silu_and_grad(gb_ref[rows, cols].astype(F32))
                y_ref[rows, CB + g * gd:CB + (g + 1) * gd] = (
                    u_ref[rows, cols].astype(F32) * m * silu_b).astype(BF16)

    zspec = lambda cb: pl.BlockSpec((tm, CB), lambda i, cb=cb: (i, cb))
    full = lambda shape: pl.BlockSpec(shape, lambda i: (0,) * len(shape))
    return pl.pallas_call(
        body, name=name, grid=(t // tm,),
        in_specs=[zspec(0), zspec(1), zspec(2), zspec(3), zspec(4),
                  full(wp.shape), full(scale.shape), full(ws.shape), full(bs_col.shape)],
        out_specs=[pl.BlockSpec((tm, 2 * CB), lambda i: (i, 0)), pl.BlockSpec((tm, CB), lambda i: (i, 0))],
        out_shape=[_sds((t, 2 * CB), BF16), _sds((t, CB), BF16)],
        scratch_shapes=[pltpu.VMEM((HALO + tm, CB), F32)],
        compiler_params=_cp("arbitrary"),
    )(z, z, z, z, z, wp, scale, ws, bs_col)


def even_mix_bwd(dy, z, pooled, wp, scale, ws, bs_col, name, tm=256):
    t = z.shape[0]
    nt = t // tm
    gd = CB // len(POOL_SIZES)

    def body(dya_ref, dyb_ref, ga_ref, u_ref, v_ref, gb_ref, pooled_ref, wp_ref, sc_ref, ws_ref, bs_ref,
             dz_ref, dwp_ref, dsc_ref, dws_ref, dbs_ref, dpext):
        i = pl.program_id(0)
        tile = nt - 1 - i

        @pl.when(i == 0)
        def _():
            dpext[tm:tm + HALO, :] = jnp.zeros((HALO, CB), F32)
            dwp_ref[...] = jnp.zeros_like(dwp_ref)
            dsc_ref[...] = jnp.zeros_like(dsc_ref)
            dws_ref[...] = jnp.zeros_like(dws_ref)
            dbs_ref[...] = jnp.zeros_like(dbs_ref)

        @pl.when(i > 0)
        def _():
            dpext[tm:tm + HALO, :] = dpext[0:HALO, :]

        for g, w in enumerate(POOL_SIZES):
            cols = slice(g * gd, (g + 1) * gd)
            pooled_g = pooled_ref[:, cols]
            mixed = _dot(pooled_g, wp_ref[g])
            silu_a, dsilu_a = _silu_and_grad(ga_ref[:, cols].astype(F32))
            dya = dya_ref[:, cols].astype(F32)
            sc = sc_ref[:, cols]
            dmixed = (dya * sc * silu_a).astype(BF16)
            dsc_ref[:, cols] += jnp.sum(dya * mixed * silu_a, axis=0, keepdims=True)
            dz_ref[:, CB + g * gd:CB + (g + 1) * gd] = (dya * mixed * sc * dsilu_a).astype(BF16)
            dwp_ref[g] += _dot_tn(pooled_g, dmixed)
            dpooled = _dot_nt(dmixed, wp_ref[g])
            dpext[0:tm, cols] = dpooled / _window_counts(tile * tm, tm, w)
            s = dpext[0:tm, cols]
            for k in range(1, w):
                s = s + dpext[k:k + tm, cols]
            dz_ref[:, cols] = (s - dpooled).astype(BF16)
        for g in range(4):
            cols = slice(g * gd, (g + 1) * gd)
            wg, lower = _tril_bf16(ws_ref, g)
            dws_g = jnp.zeros((CHUNK, CHUNK), F32)
            dbs_g = jnp.zeros((CHUNK, 1), F32)
            for c in range(tm // CHUNK):
                rows = slice(c * CHUNK, (c + 1) * CHUNK)
                vb = v_ref[rows, cols]
                m = _dot(wg, vb) + bs_ref[g]
                gbv = gb_ref[rows, cols].astype(F32)
                silu_b, dsilu_b = _silu_and_grad(gbv)
                dyb = dyb_ref[rows, cols].astype(F32)
                uv = u_ref[rows, cols].astype(F32)
                dm = dyb * silu_b * uv
                dmb = dm.astype(BF16)
                dz_ref[rows, 2 * CB + g * gd:2 * CB + (g + 1) * gd] = (dyb * silu_b * m).astype(BF16)
                dz_ref[rows, 3 * CB + g * gd:3 * CB + (g + 1) * gd] = _dot_tn(wg, dmb).astype(BF16)
                dz_ref[rows, 4 * CB + g * gd:4 * CB + (g + 1) * gd] = (dyb * uv * m * dsilu_b).astype(BF16)
                dws_g = dws_g + _dot_nt(dmb, vb)
                dbs_g = dbs_g + jnp.sum(dm, axis=1, keepdims=True)
            dws_ref[g] += jnp.where(lower, dws_g, 0.0)
            dbs_ref[g] += dbs_g

    rev = lambda cb: pl.BlockSpec((tm, CB), lambda i, cb=cb: (nt - 1 - i, cb))
    full = lambda shape: pl.BlockSpec(shape, lambda i: (0,) * len(shape))
    return pl.pallas_call(
        body, name=name, grid=(nt,),
        in_specs=[rev(0), rev(1), rev(1), rev(2), rev(3), rev(4), rev(0),
                  full(wp.shape), full(scale.shape), full(ws.shape), full(bs_col.shape)],
        out_specs=[pl.BlockSpec((tm, 5 * CB), lambda i: (nt - 1 - i, 0)),
                   full(wp.shape), full(scale.shape), full(ws.shape), full(bs_col.shape)],
        out_shape=[_sds((t, 5 * CB), BF16), _sds(wp.shape, F32), _sds(scale.shape, F32),
                   _sds(ws.shape, F32), _sds(bs_col.shape, F32)],
        scratch_shapes=[pltpu.VMEM((tm + HALO, CB), F32)],
        compiler_params=_cp("arbitrary"),
    )(dy, dy, z, z, z, z, pooled, wp, scale, ws, bs_col)


def _slope(group, head):
    return float(2.0 ** (-8.0 * (group * HEADS_PER_GROUP + head + 1) / N_HEADS))


def _band(dilation):
    qi = lax.broadcasted_iota(jnp.int32, (ATTN_BLOCK, ATTN_BLOCK), 0)
    ki = lax.broadcasted_iota(jnp.int32, (ATTN_BLOCK, ATTN_BLOCK), 1)
    dist_prev = ((qi + ATTN_BLOCK - ki) * dilation).astype(F32)
    dist_cur = ((qi - ki) * dilation).astype(F32)
    return dist_prev, ki >= qi, dist_cur, ki <= qi


def _permute_rows(name, d, inverse, arrays, in_specs, out_shapes, out_specs, n_chunks, widths):
    rows = ATTN_BLOCK * d
    t = out_shapes[0].shape[-2]
    n = len(arrays)

    lanes = HEAD_DIM

    def body(*refs):
        for a, o, s, w in zip(refs[:n], refs[n:2 * n], refs[2 * n:], widths):
            for k in range(w // lanes):
                cols = slice(k * lanes, (k + 1) * lanes)
                if inverse:
                    for r in range(d):
                        s[k, pl.ds(r, ATTN_BLOCK, stride=d), :] = (
                            a[r * ATTN_BLOCK:(r + 1) * ATTN_BLOCK, cols].astype(F32))
                    o[:, cols] = s[k].astype(o.dtype)
                else:
                    s[k] = a[:, cols].astype(F32)
                    for r in range(d):
                        o[r * ATTN_BLOCK:(r + 1) * ATTN_BLOCK, cols] = (
                            s[k, pl.ds(r, ATTN_BLOCK, stride=d), :].astype(o.dtype))

    return pl.pallas_call(
        body, name=name, grid=(t // rows, n_chunks),
        in_specs=in_specs, out_specs=out_specs, out_shape=out_shapes,
        scratch_shapes=[pltpu.VMEM((w // lanes, rows, lanes), F32) for w in widths],
        compiler_params=_cp("parallel", "arbitrary"),
    )(*arrays)


def _chunk_width(d):
    return 4 * CB // d


def permute_qkv(z, group, name):
    d = DILATIONS[group]
    rows, cw = ATTN_BLOCK * d, _chunk_width(d)
    per = CB // cw
    src = pl.BlockSpec((rows, cw), lambda i, c: (i, (c // per * 3 + group) * per + c % per))
    dst = pl.BlockSpec((rows, cw), lambda i, c: (i, c))
    (out,) = _permute_rows(name, d, False, [z], [src], [_sds((z.shape[0], 3 * CB), z.dtype)], [dst], 3 * per, [cw])
    return out


def permute_mats(wide, narrow, group, inverse, name):
    d = DILATIONS[group]
    rows, cw = ATTN_BLOCK * d, _chunk_width(d)
    wide_spec = pl.BlockSpec((rows, cw), lambda i, c: (i, c))
    narrow_spec = pl.BlockSpec((rows, HEAD_DIM), lambda i, c: (i, 0))
    arrays = [wide] + list(narrow)
    specs = [wide_spec] + [narrow_spec] * len(narrow)
    return _permute_rows(name, d, inverse, arrays, specs, [_sds(a.shape, a.dtype) for a in arrays], specs,
                         CB // cw, [cw] + [HEAD_DIM] * len(narrow))


def permute_slabs(s, group, inverse, name):
    d = DILATIONS[group]
    rows, cw = ATTN_BLOCK * d, _chunk_width(d)
    per = CB // cw
    spec = pl.BlockSpec((None, rows, cw), lambda i, c: (c // per, i, c % per))
    (out,) = _permute_rows(name, d, inverse, [s], [spec], [_sds(s.shape, s.dtype)], [spec], 3 * per, [cw])
    return out


def _lane_pack(cols):
    rows = cols[0].shape[0]
    lane = lax.broadcasted_iota(jnp.int32, (rows, HEAD_DIM), 1)
    out = jnp.zeros((rows, HEAD_DIM), F32)
    for h, c in enumerate(cols):
        out = jnp.where(lane == h, c, out)
    return out


def _qkv_col(group, from_z):
    return (lambda which: which * 3 + group) if from_z else (lambda which: which)


def _scores(q, kp, kc, group, h, dist_p, ok_p, dist_c, ok_c):
    scale = HEAD_DIM ** -0.5
    slope = _slope(group, h)
    sp = jnp.where(ok_p, _dot_nt(q, kp) * scale - slope * dist_p, NEG)
    sc = jnp.where(ok_c, _dot_nt(q, kc) * scale - slope * dist_c, NEG)
    return sp, sc


def attn_fwd(qkv, group, name):
    t = qkv.shape[0]
    d = DILATIONS[group]
    col = _qkv_col(group, qkv.shape[1] != 3 * CB)
    heads = range(HEADS_PER_GROUP)

    def body(q_ref, kp_ref, kc_ref, vp_ref, vc_ref, o_ref, lse_ref):
        b = pl.program_id(0)
        dist_p, ok_p, dist_c, ok_c = _band(d)
        ok_p = jnp.logical_and(ok_p, b >= d)
        hs = [slice(h * HEAD_DIM, (h + 1) * HEAD_DIM) for h in heads]
        s = [_scores(q_ref[:, hs[h]], kp_ref[:, hs[h]], kc_ref[:, hs[h]], group, h, dist_p, ok_p, dist_c, ok_c)
             for h in heads]
        m = [jnp.maximum(jnp.max(sp, axis=1, keepdims=True), jnp.max(sc, axis=1, keepdims=True)) for sp, sc in s]
        e = [(jnp.exp(sp - m[h]), jnp.exp(sc - m[h])) for h, (sp, sc) in enumerate(s)]
        l = [jnp.sum(ep, axis=1, keepdims=True) + jnp.sum(ec, axis=1, keepdims=True) for ep, ec in e]
        inv = [1.0 / lh for lh in l]
        o = [_dot((ep * inv[h]).astype(BF16), vp_ref[:, hs[h]]) + _dot((ec * inv[h]).astype(BF16), vc_ref[:, hs[h]])
             for h, (ep, ec) in enumerate(e)]
        for h in heads:
            o_ref[:, hs[h]] = o[h].astype(BF16)
        lse_ref[...] = _lane_pack([m[h] + jnp.log(l[h]) for h in heads])

    blk = (ATTN_BLOCK, CB)
    cur = lambda which: pl.BlockSpec(blk, lambda b: (b, col(which)))
    prev = lambda which: pl.BlockSpec(blk, lambda b: (jnp.maximum(b - d, 0), col(which)))
    return pl.pallas_call(
        body, name=name, grid=(t // ATTN_BLOCK,),
        in_specs=[cur(0), prev(1), cur(1), prev(2), cur(2)],
        out_specs=[pl.BlockSpec(blk, lambda b: (b, 0)), pl.BlockSpec((ATTN_BLOCK, HEAD_DIM), lambda b: (b, 0))],
        out_shape=[_sds((t, CB), BF16), _sds((t, HEAD_DIM), F32)],
        compiler_params=_cp("parallel"),
    )(qkv, qkv, qkv, qkv, qkv)


def attn_bwd(z, dyc, lse, dd, group, name):
    t = z.shape[0]
    d = DILATIONS[group]
    nb = t // (ATTN_BLOCK * d)
    col = _qkv_col(group, z.shape[1] != 3 * CB)
    scale = HEAD_DIM ** -0.5
    heads = range(HEADS_PER_GROUP)

    def body(q_ref, kp_ref, kc_ref, vp_ref, vc_ref, dy_ref, lse_ref, dd_ref, out_ref, carry):
        n = pl.program_id(1)

        @pl.when(n == 0)
        def _():
            carry[...] = jnp.zeros_like(carry)

        @pl.when(n == nb)
        def _():
            out_ref[...] = carry[...].astype(BF16)

        @pl.when(n < nb)
        def _():
            dist_p, ok_p, dist_c, ok_c = _band(d)
            ok_p = jnp.logical_and(ok_p, n > 0)
            hs = [slice(h * HEAD_DIM, (h + 1) * HEAD_DIM) for h in heads]
            q = [q_ref[:, hs[h]] for h in heads]
            kp = [kp_ref[:, hs[h]] for h in heads]
            kc = [kc_ref[:, hs[h]] for h in heads]
            dy = [dy_ref[:, hs[h]] for h in heads]
            old = [[carry[w, :, hs[h]] for h in heads] for w in range(3)]
            s = [_scores(q[h], kp[h], kc[h], group, h, dist_p, ok_p, dist_c, ok_c) for h in heads]
            p = [(jnp.exp(sp - lse_ref[:, h:h + 1]), jnp.exp(sc - lse_ref[:, h:h + 1])) for h, (sp, sc) in enumerate(s)]
            ds = [((pp * (_dot_nt(dy[h], vp_ref[:, hs[h]]) - dd_ref[:, h:h + 1]) * scale).astype(BF16),
                   (pc * (_dot_nt(dy[h], vc_ref[:, hs[h]]) - dd_ref[:, h:h + 1]) * scale).astype(BF16))
                  for h, (pp, pc) in enumerate(p)]
            for h in heads:
                dsp, dsc = ds[h]
                pp, pc = p[h]
                out_ref[0, :, hs[h]] = old[0][h].astype(BF16)
                out_ref[1, :, hs[h]] = (old[1][h] + _dot_tn(dsp, q[h])).astype(BF16)
                out_ref[2, :, hs[h]] = (old[2][h] + _dot_tn(pp.astype(BF16), dy[h])).astype(BF16)
                carry[0, :, hs[h]] = _dot(dsp, kp[h]) + _dot(dsc, kc[h])
                carry[1, :, hs[h]] = _dot_tn(dsc, q[h])
                carry[2, :, hs[h]] = _dot_tn(pc.astype(BF16), dy[h])

    blk = (ATTN_BLOCK, CB)
    cur_row = lambda r, n: jnp.minimum(n, nb - 1) * d + r
    prev_row = lambda r, n: jnp.clip(n - 1, 0, nb - 1) * d + r
    cur = lambda which: pl.BlockSpec(blk, lambda r, n: (cur_row(r, n), col(which)))
    prev = lambda which: pl.BlockSpec(blk, lambda r, n: (prev_row(r, n), col(which)))
    narrow = pl.BlockSpec((ATTN_BLOCK, HEAD_DIM), lambda r, n: (cur_row(r, n), 0))
    return pl.pallas_call(
        body, name=name, grid=(d, nb + 1),
        in_specs=[cur(0), prev(1), cur(1), prev(2), cur(2),
                  pl.BlockSpec(blk, lambda r, n: (cur_row(r, n), 0)), narrow, narrow],
        out_specs=pl.BlockSpec((3, ATTN_BLOCK, CB), lambda r, n: (0, prev_row(r, n), 0)),
        out_shape=_sds((3, t, CB), BF16),
        scratch_shapes=[pltpu.VMEM((3, ATTN_BLOCK, CB), F32)],
        compiler_params=_cp("parallel", "arbitrary"),
    )(z, z, z, z, z, dyc, lse, dd)


def odd_mix(os_, lses, z, conv_w, name, tm=256):
    t = z.shape[0]

    def body(o0, o1, o2, l0, l1, l2, gc_ref, db_ref, dc_ref, dx_ref, gd_ref, cw_ref,
             y_ref, yc_ref, lse_ref, conv_ref, zext):
        i = pl.program_id(0)
        a0, a1, a2 = l0[...], l1[...], l2[...]
        m = jnp.maximum(jnp.maximum(a0, a1), a2)
        tot = m + jnp.log(jnp.exp(a0 - m) + jnp.exp(a1 - m) + jnp.exp(a2 - m))
        lse_ref[...] = tot
        w0, w1, w2 = jnp.exp(a0 - tot), jnp.exp(a1 - tot), jnp.exp(a2 - tot)
        for h in range(HEADS_PER_GROUP):
            hs = slice(h * HEAD_DIM, (h + 1) * HEAD_DIM)
            yc = (w0[:, h:h + 1] * o0[:, hs].astype(F32) + w1[:, h:h + 1] * o1[:, hs].astype(F32)
                  + w2[:, h:h + 1] * o2[:, hs].astype(F32))
            yc_ref[:, hs] = yc.astype(BF16)
            silu_c, _ = _silu_and_grad(gc_ref[:, hs].astype(F32))
            y_ref[:, hs] = (yc * silu_c).astype(BF16)

        @pl.when(i == 0)
        def _():
            zext[0:HALO, :] = jnp.zeros((HALO, CB), F32)

        @pl.when(i > 0)
        def _():
            zext[0:HALO, :] = zext[tm:tm + HALO, :]

        zext[HALO:HALO + tm, :] = dc_ref[...].astype(F32) * dx_ref[...].astype(F32)
        conv = (cw_ref[0:1, :] * zext[HALO - 2:HALO - 2 + tm, :] + cw_ref[1:2, :] * zext[HALO - 1:HALO - 1 + tm, :]
                + cw_ref[2:3, :] * zext[HALO:HALO + tm, :])
        conv_ref[...] = conv.astype(BF16)
        silu_d, _ = _silu_and_grad(gd_ref[...].astype(F32))
        y_ref[:, CB:2 * CB] = (db_ref[...].astype(F32) * conv * silu_d).astype(BF16)

    row = pl.BlockSpec((tm, CB), lambda i: (i, 0))
    narrow = pl.BlockSpec((tm, HEAD_DIM), lambda i: (i, 0))
    zspec = lambda cb: pl.BlockSpec((tm, CB), lambda i, cb=cb: (i, cb))
    return pl.pallas_call(
        body, name=name, grid=(t // tm,),
        in_specs=[row] * 3 + [narrow] * 3 + [zspec(9), zspec(10), zspec(11), zspec(12), zspec(13),
                                             pl.BlockSpec(conv_w.shape, lambda i: (0, 0))],
        out_specs=[pl.BlockSpec((tm, 2 * CB), lambda i: (i, 0)), row, narrow, row],
        out_shape=[_sds((t, 2 * CB), BF16), _sds((t, CB), BF16), _sds((t, HEAD_DIM), F32), _sds((t, CB), BF16)],
        scratch_shapes=[pltpu.VMEM((HALO + tm, CB), F32)],
        compiler_params=_cp("arbitrary"),
    )(*os_, *lses, z, z, z, z, z, conv_w)


def odd_mix_bwd(dy, yc, conv, z, conv_w, name, tm=256):
    t = z.shape[0]
    nt = t // tm

    def body(dyc_ref, dyd_ref, yc_ref, conv_ref, gc_ref, db_ref, dc_ref, dx_ref, gd_ref, cw_ref,
             dz_ref, dyo_ref, dd_ref, dcw_ref, dcext):
        i = pl.program_id(0)

        @pl.when(i == 0)
        def _():
            dcext[tm:tm + HALO, :] = jnp.zeros((HALO, CB), F32)
            dcw_ref[...] = jnp.zeros_like(dcw_ref)

        @pl.when(i > 0)
        def _():
            dcext[tm:tm + HALO, :] = dcext[0:HALO, :]

        silu_c, dsilu_c = _silu_and_grad(gc_ref[...].astype(F32))
        dyc = dyc_ref[...].astype(F32)
        ycv = yc_ref[...].astype(F32)
        dyo = dyc * silu_c
        dyo_ref[...] = dyo.astype(BF16)
        dz_ref[:, 0:CB] = (dyc * ycv * dsilu_c).astype(BF16)
        prod = dyo * ycv
        dd_ref[...] = _lane_pack([jnp.sum(prod[:, h * HEAD_DIM:(h + 1) * HEAD_DIM], axis=1, keepdims=True)
                                  for h in range(HEADS_PER_GROUP)])

        silu_d, dsilu_d = _silu_and_grad(gd_ref[...].astype(F32))
        dyd = dyd_ref[...].astype(F32)
        convv = conv_ref[...].astype(F32)
        dbv = db_ref[...].astype(F32)
        dz_ref[:, CB:2 * CB] = (dyd * convv * silu_d).astype(BF16)
        dz_ref[:, 4 * CB:5 * CB] = (dyd * dbv * convv * dsilu_d).astype(BF16)
        dcext[0:tm, :] = dyd * dbv * silu_d
        dcv, dxv = dc_ref[...].astype(F32), dx_ref[...].astype(F32)
        zc = dcv * dxv
        d0, d1, d2 = dcext[0:tm, :], dcext[1:1 + tm, :], dcext[2:2 + tm, :]
        dzc = cw_ref[2:3, :] * d0 + cw_ref[1:2, :] * d1 + cw_ref[0:1, :] * d2
        dz_ref[:, 2 * CB:3 * CB] = (dzc * dxv).astype(BF16)
        dz_ref[:, 3 * CB:4 * CB] = (dzc * dcv).astype(BF16)
        dcw_ref[0:1, :] += jnp.sum(zc * d2, axis=0, keepdims=True)
        dcw_ref[1:2, :] += jnp.sum(zc * d1, axis=0, keepdims=True)
        dcw_ref[2:3, :] += jnp.sum(zc * d0, axis=0, keepdims=True)

    rev = lambda cb: pl.BlockSpec((tm, CB), lambda i, cb=cb: (nt - 1 - i, cb))
    return pl.pallas_call(
        body, name=name, grid=(nt,),
        in_specs=[rev(0), rev(1), rev(0), rev(0), rev(9), rev(10), rev(11), rev(12), rev(13),
                  pl.BlockSpec(conv_w.shape, lambda i: (0, 0))],
        out_specs=[pl.BlockSpec((tm, 5 * CB), lambda i: (nt - 1 - i, 0)), rev(0),
                   pl.BlockSpec((tm, HEAD_DIM), lambda i: (nt - 1 - i, 0)),
                   pl.BlockSpec(conv_w.shape, lambda i: (0, 0))],
        out_shape=[_sds((t, 5 * CB), BF16), _sds((t, CB), BF16), _sds((t, HEAD_DIM), F32), _sds(conv_w.shape, F32)],
        scratch_shapes=[pltpu.VMEM((tm + HALO, CB), F32)],
        compiler_params=_cp("arbitrary"),
    )(dy, dy, yc, conv, z, z, z, z, z, conv_w)


def final_norm_loss(x, g, target, name, tm=512):
    t, d = x.shape

    def body(x_ref, g_ref, t_ref, dx_ref, loss_ref, dg_ref):
        i = pl.program_id(0)
        xv = x_ref[...]
        r = lax.rsqrt(jnp.mean(xv * xv, axis=-1, keepdims=True) + EPS)
        xn = xv * r
        gv = g_ref[...]
        err = xn * gv - t_ref[...]
        loss = 0.5 * jnp.sum(jnp.mean(err * err, axis=-1, keepdims=True), axis=0, keepdims=True)
        dy = err * (1.0 / d)
        dyg = dy * gv
        dx_ref[...] = r * (dyg - xn * jnp.mean(dyg * xn, axis=-1, keepdims=True))
        dg = jnp.sum(dy * xn, axis=0, keepdims=True)

        @pl.when(i == 0)
        def _():
            loss_ref[...] = jnp.broadcast_to(loss, loss_ref.shape)
            dg_ref[...] = dg

        @pl.when(i > 0)
        def _():
            loss_ref[...] += jnp.broadcast_to(loss, loss_ref.shape)
            dg_ref[...] += dg

    return pl.pallas_call(
        body, name=name, grid=(t // tm,),
        in_specs=[pl.BlockSpec((tm, d), lambda i: (i, 0)), pl.BlockSpec((1, d), lambda i: (0, 0)),
                  pl.BlockSpec((tm, d), lambda i: (i, 0))],
        out_specs=[pl.BlockSpec((tm, d), lambda i: (i, 0)), pl.BlockSpec((8, 128), lambda i: (0, 0)),
                   pl.BlockSpec((1, d), lambda i: (0, 0))],
        out_shape=[_sds((t, d), F32), _sds((8, 128), F32), _sds((1, d), F32)],
        compiler_params=_cp("arbitrary"),
    )(x, g, target)


def _position():
    x, y, c = lax.axis_index("x"), lax.axis_index("y"), lax.axis_index("c")
    return x, y, c, 4 * x + 2 * y + c


def _peer(x, y, c, k):
    px = 1 - x if k & 4 else x
    py = 1 - y if k & 2 else y
    pc = 1 - c if k & 1 else c
    return (px, py, pc), 4 * px + 2 * py + pc


def _block(ref, axis, size, idx):
    index = [slice(None)] * len(ref.shape)
    index[axis] = pl.ds(idx * size, size)
    return ref.at[tuple(index)]


def all_gather(shards, axes, name):
    n = len(shards)
    fulls = []
    for s, ax in zip(shards, axes):
        shape = list(s.shape)
        shape[ax] *= N_DEV
        fulls.append(_sds(tuple(shape), s.dtype))

    def body(*refs):
        ins, outs = refs[:n], refs[n:2 * n]
        send_sems, recv_sems, local_sems = refs[2 * n:]
        x, y, c, me = _position()
        local = []
        for p in range(n):
            size = ins[p].shape[axes[p]]
            cp = pltpu.make_async_copy(ins[p], _block(outs[p], axes[p], size, me), local_sems.at[p])
            cp.start()
            local.append(cp)
        for k in range(1, N_DEV):
            peer, _ = _peer(x, y, c, k)
            for p in range(n):
                size = ins[p].shape[axes[p]]
                pltpu.make_async_remote_copy(
                    src_ref=ins[p], dst_ref=_block(outs[p], axes[p], size, me),
                    send_sem=send_sems.at[p], recv_sem=recv_sems.at[p],
                    device_id=peer, device_id_type=MESH_ID).start()
        for p in range(n):
            size = ins[p].shape[axes[p]]
            seven = _block(outs[p], axes[p], (N_DEV - 1) * size, 0)
            pltpu.make_async_remote_copy(
                src_ref=seven, dst_ref=seven, send_sem=send_sems.at[p], recv_sem=recv_sems.at[p],
                device_id=(x, y, c), device_id_type=MESH_ID).wait()
            local[p].wait()

    any_spec = pl.BlockSpec(memory_space=pl.ANY)
    return pl.pallas_call(
        body, name=name,
        in_specs=[any_spec] * n, out_specs=[any_spec] * n, out_shape=fulls,
        scratch_shapes=[pltpu.SemaphoreType.DMA((n,)), pltpu.SemaphoreType.DMA((n,)), pltpu.SemaphoreType.DMA((n,))],
    )(*shards)


_HBM = pl.BlockSpec(memory_space=pltpu.HBM)
_SEM = pl.BlockSpec(memory_space=pltpu.SEMAPHORE)
_EFFECT = pltpu.SideEffectType.DATAFLOW_SIDE_EFFECTING


def _in_hbm(a):
    return pltpu.with_memory_space_constraint(a, pltpu.HBM)


def _landing(mode, src, axis):
    me = 4 * lax.axis_index("x") + 2 * lax.axis_index("y") + lax.axis_index("c")
    if mode == "gather":
        shape = list(src.shape)
        size = shape[axis]
        shape[axis] *= N_DEV
        return lax.dynamic_update_slice_in_dim(lax.empty(tuple(shape), src.dtype), src, me * size, axis)
    size = src.shape[axis] // N_DEV
    own = lax.dynamic_slice_in_dim(src, me * size, size, axis)
    return lax.dynamic_update_slice_in_dim(lax.empty((N_DEV, *own.shape), src.dtype), own[None], me, 0)


def _seven(mode, land_ref, axis):
    if mode == "gather":
        return _block(land_ref, axis, (N_DEV - 1) * (land_ref.shape[axis] // N_DEV), 0)
    return land_ref.at[pl.ds(0, N_DEV - 1)]


def exchange_start(mode, srcs, axes, after, name):
    n = len(srcs)
    lands = [_landing(mode, s, ax) for s, ax in zip(srcs, axes)]

    def body(*refs):
        src, land = refs[:n], refs[n:2 * n]
        send_sems, recv_sems = refs[2 * n + 1], refs[2 * n + 2]
        token = refs[-1]
        x, y, c, me = _position()
        for k in range(1, N_DEV):
            peer, pj = _peer(x, y, c, k)
            for p in range(n):
                if mode == "gather":
                    s = src[p]
                    dst = _block(land[p], axes[p], src[p].shape[axes[p]], me)
                else:
                    s = _block(src[p], axes[p], src[p].shape[axes[p]] // N_DEV, pj)
                    dst = land[p].at[me]
                pltpu.make_async_remote_copy(
                    src_ref=s, dst_ref=dst, send_sem=send_sems.at[p], recv_sem=recv_sems.at[p],
                    device_id=peer, device_id_type=MESH_ID).start()
        token[...] = jnp.zeros_like(token)

    outs = pl.pallas_call(
        body, name=name,
        out_shape=(pltpu.SemaphoreType.DMA((n,)), pltpu.SemaphoreType.DMA((n,)),
                   *[pltpu.HBM(s.shape, s.dtype) for s in srcs], *[pltpu.HBM(l.shape, l.dtype) for l in lands],
                   _sds((8, 128), F32)),
        in_specs=[_HBM] * (2 * n) + [pl.BlockSpec(memory_space=pl.ANY)],
        out_specs=(_SEM, _SEM, *[_HBM] * (2 * n), pl.BlockSpec(memory_space=pltpu.VMEM)),
        input_output_aliases={i: 2 + i for i in range(2 * n)},
        compiler_params=pltpu.CompilerParams(has_side_effects=_EFFECT),
    )(*[_in_hbm(s) for s in srcs], *[_in_hbm(l) for l in lands], after)
    return outs[0], outs[1], list(outs[2:2 + n]), list(outs[2 + n:2 + 2 * n]), outs[-1]


def exchange_wait(mode, started, axes, after, name):
    send_sems, recv_sems, srcs, lands, _ = started
    n = len(srcs)

    def body(*refs):
        land = refs[n:2 * n]
        send_ref, recv_ref = refs[2 * n], refs[2 * n + 1]
        x, y, c, _ = _position()
        for p in range(n):
            seven = _seven(mode, land[p], axes[p])
            cp = pltpu.make_async_remote_copy(
                src_ref=seven, dst_ref=seven, send_sem=send_ref.at[p], recv_sem=recv_ref.at[p],
                device_id=(x, y, c), device_id_type=MESH_ID)
            cp.wait_send()
            cp.wait_recv()

    outs = pl.pallas_call(
        body, name=name,
        out_shape=(*[pltpu.HBM(s.shape, s.dtype) for s in srcs], *[pltpu.HBM(l.shape, l.dtype) for l in lands]),
        in_specs=[_HBM] * (2 * n) + [_SEM, _SEM, pl.BlockSpec(memory_space=pl.ANY)],
        out_specs=tuple([_HBM] * (2 * n)),
        input_output_aliases={i: i for i in range(2 * n)},
        compiler_params=pltpu.CompilerParams(has_side_effects=_EFFECT),
    )(*srcs, *lands, send_sems, recv_sems, after)
    return list(outs[n:])


def _adam_math(g, w, m, v):
    m2 = ADAM_B1 * m + (1.0 - ADAM_B1) * g
    v2 = ADAM_B2 * v + (1.0 - ADAM_B2) * (g * g)
    m_hat = m2 / (1.0 - ADAM_B1 ** ADAM_STEP)
    v_hat = v2 / (1.0 - ADAM_B2 ** ADAM_STEP)
    delta = -ADAM_LR * (m_hat / (jnp.sqrt(v_hat) + ADAM_EPS) + ADAM_WD * w)
    return delta, m2, v2


def adamw(g, w, m, v, name, slots, tr=128):
    r, c = w.shape
    tr = min(tr, r)

    def body(g_ref, w_ref, m_ref, v_ref, go_ref, d_ref, mo_ref, vo_ref):
        if slots:
            gs = g_ref[0].astype(F32)
            for s in range(1, N_DEV):
                gs = gs + g_ref[s].astype(F32)
        else:
            gs = g_ref[...]
        go_ref[...] = gs
        d_ref[...], mo_ref[...], vo_ref[...] = _adam_math(gs, w_ref[...], m_ref[...], v_ref[...])

    row = pl.BlockSpec((tr, c), lambda i: (i, 0))
    gspec = pl.BlockSpec((N_DEV, tr, c), lambda i: (0, i, 0)) if slots else row
    return pl.pallas_call(
        body, name=name, grid=(r // tr,),
        in_specs=[gspec, row, row, row], out_specs=[row] * 4, out_shape=[_sds((r, c), F32)] * 4,
        compiler_params=_cp("parallel"),
    )(g, w, m, v)


def sum_slots(g, name):
    _, r, c = g.shape

    def body(g_ref, o_ref):
        gs = g_ref[0]
        for s in range(1, N_DEV):
            gs = gs + g_ref[s]
        o_ref[...] = gs

    return pl.pallas_call(
        body, name=name, grid=(1,),
        in_specs=[pl.BlockSpec((N_DEV, r, c), lambda i: (0, 0, 0))],
        out_specs=pl.BlockSpec((r, c), lambda i: (0, 0)), out_shape=_sds((r, c), F32),
        compiler_params=_cp("arbitrary"),
    )(g)


def _rows128(a, pad_to=8):
    a = a.reshape(-1, 128)
    pad = (-a.shape[0]) % pad_to
    return jnp.pad(a, ((0, pad), (0, 0))) if pad else a


def kernel(x, even_norm, even_w_in, even_pool_w, even_pool_scale, even_ws, even_bs, even_w_out, odd_norm, odd_w_in, odd_conv_w, odd_w_out, final_norm, loss_target, m_even_norm, m_even_w_in, m_even_pool_w, m_even_pool_scale, m_even_ws, m_even_bs, m_even_w_out, m_odd_norm, m_odd_w_in, m_odd_conv_w, m_odd_w_out, m_final_norm, v_even_norm, v_even_w_in, v_even_pool_w, v_even_pool_scale, v_even_ws, v_even_bs, v_even_w_out, v_odd_norm, v_odd_w_in, v_odd_conv_w, v_odd_w_out, v_final_norm):
    x0 = x[0]
    target = loss_target[0]
    me = 4 * lax.axis_index("x") + 2 * lax.axis_index("y") + lax.axis_index("c")

    we_in, wp, we_out = all_gather(
        [even_w_in[0].astype(BF16), even_pool_w[0].astype(BF16), even_w_out[0].astype(BF16)], [1, 1, 0], "gather_even")
    odd_small = jnp.pad(odd_norm, ((0, 7), (0, 0))) + jnp.pad(odd_conv_w[0], ((1, 4), (0, 0)))
    odd_axes = [1, 0, 1]
    odd_flight = exchange_start("gather", [odd_w_in[0].astype(BF16), odd_w_out[0].astype(BF16), odd_small],
                                odd_axes, we_in, "gather_odd_start")
    scale = even_pool_scale
    ws = even_ws[0]
    bs_col = even_bs[0][:, :, None]
    g_fin = final_norm[None, :]

    z_e, h_e = norm_matmul(x0, even_norm + odd_flight[4][0:1, 0:1], we_in, "even_in")
    y_e, pooled = even_mix(z_e, wp, scale, ws, bs_col, "even_mix")
    x1 = matmul_residual(y_e, we_out, x0, "even_out")
    wo_in, wo_out, odd_small = exchange_wait("gather", odd_flight, odd_axes, x1, "gather_odd_wait")
    g_odd, conv_w = odd_small[0:1], odd_small[1:4]
    z_o, h_o = norm_matmul(x1, g_odd, wo_in, "odd_in", tn=2048)
    qkv = [z_o] + [permute_qkv(z_o, gi, f"dilate_qkv_{gi}") for gi in (1, 2)]
    attn = [attn_fwd(qkv[gi], gi, f"attn_fwd_{gi}") for gi in range(3)]
    for gi in (1, 2):
        attn[gi] = permute_mats(attn[gi][0], [attn[gi][1]], gi, True, f"undilate_attn_{gi}")
    y_o, yc, lse_tot, conv = odd_mix([a[0] for a in attn], [a[1] for a in attn], z_o, conv_w, "odd_mix")
    x2 = matmul_residual(y_o, wo_out, x1, "odd_out")
    dx2, loss_blk, dg_fin = final_norm_loss(x2, g_fin, target, "final_norm_loss")

    dy_o, dwo_out = out_proj_bwd(dx2, wo_out, y_o, "odd_out_bwd")
    dz5_o, dyc, dd, dconv_w = odd_mix_bwd(dy_o, yc, conv, z_o, conv_w, "odd_mix_bwd")
    dqkv = [attn_bwd(z_o, dyc, lse_tot, dd, 0, "attn_bwd_0")]
    for gi in (1, 2):
        dyc_g, lse_g, dd_g = permute_mats(dyc, [lse_tot, dd], gi, False, f"dilate_dy_{gi}")
        dqkv.append(permute_slabs(attn_bwd(qkv[gi], dyc_g, lse_g, dd_g, gi, f"attn_bwd_{gi}"),
                                  gi, True, f"undilate_dqkv_{gi}"))
    src_o = [(dqkv[0], ("slabs", 0)), (dqkv[1], ("slabs", 1)), (dqkv[2], ("slabs", 2)), (dz5_o, ("cols", 9, 5))]
    dwo_in = in_proj_dw(h_o, src_o, ODD_COLBLOCKS, "odd_in_dw")
    odd_grads = exchange_start("scatter", [dwo_in, dwo_out], [1, 0], dwo_in, "scatter_odd_start")
    dx1, dg_odd = in_proj_dx(src_o, ODD_COLBLOCKS, wo_in, x1, g_odd + odd_grads[4][0:1, 0:1], dx2, "odd_in_dx")

    dy_e, dwe_out = out_proj_bwd(dx1, we_out, y_e, "even_out_bwd")
    dz_e, dwp, dscale, dws, dbs_col = even_mix_bwd(dy_e, z_e, pooled, wp, scale, ws, bs_col, "even_mix_bwd")
    src_e = [(dz_e, ("cols", 0, 5))]
    dwe_in = in_proj_dw(h_e, src_e, 5, "even_in_dw")
    even_grads = exchange_start("scatter", [dwe_in, dwp.astype(BF16), dwe_out], [1, 1, 0], dwe_in, "scatter_even_start")
    dx0, dg_even = in_proj_dx(src_e, 5, we_in, x0, even_norm + even_grads[4][0:1, 0:1], dx1, "even_in_dx")
    p_wo_in, p_wo_out = exchange_wait("scatter", odd_grads, [1, 0], dx0, "scatter_odd_wait")
    p_we_in, p_wp, p_we_out = exchange_wait("scatter", even_grads, [1, 1, 0], dx0, "scatter_even_wait")
    small = jnp.concatenate([
        _rows128(dg_even), _rows128(dscale), _rows128(dws), _rows128(dbs_col), _rows128(dg_fin),
        _rows128(dg_odd), _rows128(dconv_w)], axis=0)
    n_small = small.shape[0]
    (small_all,) = all_gather([small], [0], "gather_small_grads")
    small_sum = sum_slots(small_all.reshape(N_DEV, n_small, 128), "sum_small_grads")
    g_even_norm = small_sum[0:8].reshape(1, 1024)
    g_scale = small_sum[8:16].reshape(1, 1024)
    g_ws = small_sum[16:528]
    g_bs = small_sum[528:532]
    g_final = small_sum[536:544].reshape(1, 1024)
    g_odd_norm = lax.dynamic_slice_in_dim(small_sum[544:552], me, 1, axis=0)
    g_conv = lax.dynamic_index_in_dim(small_sum[552:576].reshape(3, 8, 128), me, axis=1, keepdims=False)

    two_d = lambda a, r, c: a.reshape(r, c)
    sharded = [
        ("even_w_in", p_we_in, even_w_in, m_even_w_in, v_even_w_in, (1024, 640)),
        ("even_pool_w", p_wp.reshape(N_DEV, 128, 256), even_pool_w, m_even_pool_w, v_even_pool_w, (128, 256)),
        ("even_w_out", p_we_out, even_w_out, m_even_w_out, v_even_w_out, (256, 1024)),
        ("odd_w_in", p_wo_in, odd_w_in, m_odd_w_in, v_odd_w_in, (1024, 1792)),
        ("odd_w_out", p_wo_out, odd_w_out, m_odd_w_out, v_odd_w_out, (256, 1024)),
    ]
    plain = [
        ("even_norm", g_even_norm, even_norm, m_even_norm, v_even_norm, (1, 1024)),
        ("even_pool_scale", g_scale, even_pool_scale, m_even_pool_scale, v_even_pool_scale, (1, 1024)),
        ("even_ws", g_ws, even_ws, m_even_ws, v_even_ws, (512, 128)),
        ("even_bs", g_bs, even_bs, m_even_bs, v_even_bs, (4, 128)),
        ("odd_norm", g_odd_norm, odd_norm, m_odd_norm, v_odd_norm, (1, 128)),
        ("odd_conv_w", g_conv, odd_conv_w, m_odd_conv_w, v_odd_conv_w, (3, 128)),
        ("final_norm", g_final, final_norm, m_final_norm, v_final_norm, (1, 1024)),
    ]
    res = {}
    for name, g, w, m, v, (r, c) in sharded:
        outs = adamw(g, two_d(w, r, c), two_d(m, r, c), two_d(v, r, c), "adamw_" + name, slots=True)
        res[name] = [o.reshape(w.shape) for o in outs]
    for name, g, w, m, v, (r, c) in plain:
        outs = adamw(two_d(g, r, c), two_d(w, r, c), two_d(m, r, c), two_d(v, r, c), "adamw_" + name, slots=False)
        res[name] = [o.reshape(w.shape) for o in outs]

    loss = lax.psum(loss_blk[0, 0], ("x", "y", "c"))
    order = ["even_norm", "even_w_in", "even_pool_w", "even_pool_scale", "even_ws", "even_bs", "even_w_out",
             "odd_norm", "odd_w_in", "odd_conv_w", "odd_w_out", "final_norm"]
    return (loss, dx0[None], *[res[n][0] for n in order], *[res[n][1] for n in order],
            *[res[n][2] for n in order], *[res[n][3] for n in order])
```

```python
import functools

import jax
import jax.numpy as jnp
from jax import lax
from jax.experimental import pallas as pl
from jax.experimental.pallas import tpu as pltpu

F32 = jnp.float32
BF16 = jnp.bfloat16
MESH_ID = pl.DeviceIdType.MESH

EPS = 1e-6
NEG = -1e30
N_DEV = 8
POOL_SIZES = (2, 4, 8, 16)
DILATIONS = (1, 4, 16)
N_HEADS = 24
HEADS_PER_GROUP = 8
HEAD_DIM = 128
ATTN_BLOCK = 128
CHUNK = 128
CB = 1024
HALO = 16
ODD_COLBLOCKS = 14
ADAM_LR = 0.001
ADAM_B1 = 0.9
ADAM_B2 = 0.999
ADAM_EPS = 1e-08
ADAM_WD = 0.01
ADAM_STEP = 10
VMEM_LIMIT = 52 * 1024 * 1024


def _cp(*sem):
    return pltpu.CompilerParams(dimension_semantics=sem, vmem_limit_bytes=VMEM_LIMIT)


def _dot(a, b):
    return jnp.dot(a, b, preferred_element_type=F32)


def _dot_nt(a, b):
    return lax.dot_general(a, b, (((1,), (1,)), ((), ())), preferred_element_type=F32)


def _dot_tn(a, b):
    return lax.dot_general(a, b, (((0,), (0,)), ((), ())), preferred_element_type=F32)


def _sigmoid(x):
    return 1.0 / (1.0 + jnp.exp(-x))


def _silu_and_grad(x):
    s = _sigmoid(x)
    return x * s, s * (1.0 + x * (1.0 - s))


def _sds(shape, dtype):
    return jax.ShapeDtypeStruct(shape, dtype)


def norm_matmul(x, g, w, name, tn, tm=1024, rows=256):
    t, d = x.shape
    n = w.shape[1]

    def body(x_ref, g_ref, w_ref, z_ref, h_ref):
        @pl.when(pl.program_id(1) == 0)
        def _():
            for c in range(tm // rows):
                rs = slice(c * rows, (c + 1) * rows)
                xv = x_ref[rs, :]
                r = lax.rsqrt(jnp.mean(xv * xv, axis=-1, keepdims=True) + EPS)
                h_ref[rs, :] = ((xv * r) * g_ref[...]).astype(BF16)

        z_ref[...] = _dot(h_ref[...], w_ref[...]).astype(BF16)

    return pl.pallas_call(
        body, name=name, grid=(t // tm, n // tn),
        in_specs=[pl.BlockSpec((tm, d), lambda i, j: (i, 0)),
                  pl.BlockSpec((1, d), lambda i, j: (0, 0)),
                  pl.BlockSpec((d, tn), lambda i, j: (0, j))],
        out_specs=[pl.BlockSpec((tm, tn), lambda i, j: (i, j)),
                   pl.BlockSpec((tm, d), lambda i, j: (i, 0))],
        out_shape=[_sds((t, n), BF16), _sds((t, d), BF16)],
        compiler_params=_cp("parallel", "arbitrary"),
    )(x, g, w)


def matmul_residual(y, w, x, name, tm=1024):
    t, k = y.shape
    d = w.shape[1]

    def body(y_ref, w_ref, x_ref, o_ref):
        o_ref[...] = x_ref[...] + _dot(y_ref[...], w_ref[...])

    return pl.pallas_call(
        body, name=name, grid=(t // tm,),
        in_specs=[pl.BlockSpec((tm, k), lambda i: (i, 0)),
                  pl.BlockSpec((k, d), lambda i: (0, 0)),
                  pl.BlockSpec((tm, d), lambda i: (i, 0))],
        out_specs=pl.BlockSpec((tm, d), lambda i: (i, 0)),
        out_shape=_sds((t, d), F32),
        compiler_params=_cp("parallel"),
    )(y, w, x)


def out_proj_bwd(dx, wt, y, name, tm=512):
    t, d = dx.shape
    k = wt.shape[1]
    steps = t // tm
    half = d // 2

    def body(dx_ref, wt_ref, y_ref, dy_ref, dw_ref, acc):
        i = pl.program_id(0)
        dxb = dx_ref[...].astype(BF16)
        dy_ref[...] = _dot(dxb, wt_ref[...]).astype(BF16)

        @pl.when(i == 0)
        def _():
            acc[...] = jnp.zeros_like(acc)

        for c in range(2):
            acc[:, c * half:(c + 1) * half] += _dot_tn(y_ref[...], dxb[:, c * half:(c + 1) * half])

        @pl.when(i == steps - 1)
        def _():
            dw_ref[...] = acc[...].astype(BF16)

    return pl.pallas_call(
        body, name=name, grid=(steps,),
        in_specs=[pl.BlockSpec((tm, d), lambda i: (i, 0)),
                  pl.BlockSpec((d, k), lambda i: (0, 0)),
                  pl.BlockSpec((tm, k), lambda i: (i, 0))],
        out_specs=[pl.BlockSpec((tm, k), lambda i: (i, 0)),
                   pl.BlockSpec((k, d), lambda i: (0, 0))],
        out_shape=[_sds((t, k), BF16), _sds((k, d), BF16)],
        scratch_shapes=[pltpu.VMEM((k, d), F32)],
        compiler_params=_cp("arbitrary"),
    )(dx, wt, y)


def _source_block(kind, rows):
    if kind[0] == "cols":
        return (rows, CB)
    return (None, rows, CB)


def _source_active(kind, j):
    if kind[0] == "cols":
        _, first, n = kind
        return (j >= first) & (j < first + n), jnp.clip(j - first, 0, n - 1)
    _, group = kind
    return (j < 9) & (j % 3 == group), jnp.clip(j // 3, 0, 2)


def _source_index(kind, row_block, inner):
    if kind[0] == "cols":
        return (row_block, inner)
    return (inner, row_block, 0)


def in_proj_dw(h, sources, n_blocks, name, tk=1024):
    t, d = h.shape
    steps = t // tk
    ns = len(sources)

    def body(*refs):
        h_ref, src_refs, dw_ref, acc = refs[0], refs[1:1 + ns], refs[1 + ns], refs[2 + ns]
        j, k = pl.program_id(0), pl.program_id(1)

        @pl.when(k == 0)
        def _():
            acc[...] = jnp.zeros_like(acc)

        for s, (_, kind) in enumerate(sources):
            active, _ = _source_active(kind, j)

            @pl.when(active)
            def _(s=s):
                acc[...] += _dot_tn(h_ref[...], src_refs[s][...])

        @pl.when(k == steps - 1)
        def _():
            dw_ref[...] = acc[...].astype(BF16)

    def src_spec(kind):
        def index(j, k):
            active, inner = _source_active(kind, j)
            return _source_index(kind, jnp.where(active, k, 0), inner)
        return pl.BlockSpec(_source_block(kind, tk), index)

    return pl.pallas_call(
        body, name=name, grid=(n_blocks, steps),
        in_specs=[pl.BlockSpec((tk, d), lambda j, k: (k, 0))] + [src_spec(kind) for _, kind in sources],
        out_specs=pl.BlockSpec((d, CB), lambda j, k: (0, j)),
        out_shape=_sds((d, n_blocks * CB), BF16),
        scratch_shapes=[pltpu.VMEM((d, CB), F32)],
        compiler_params=_cp("parallel", "arbitrary"),
    )(h, *[a for a, _ in sources])


def in_proj_dx(sources, n_blocks, wt, x, g, dres, name, tm=1024, rows=256, epilogue_buffers=2):
    t, d = x.shape
    ns = len(sources)

    def body(*refs):
        src_refs = refs[:ns]
        w_ref, x_ref, g_ref, dres_ref, dx_ref, dg_ref, acc = refs[ns:]
        i, p = pl.program_id(0), pl.program_id(1)

        @pl.when(p == 0)
        def _():
            acc[...] = jnp.zeros_like(acc)

        for s, (_, kind) in enumerate(sources):
            active, _ = _source_active(kind, p)

            @pl.when(active)
            def _(s=s):
                acc[...] += _dot(src_refs[s][...], w_ref[...])

        @pl.when(p == n_blocks - 1)
        def _():
            part = jnp.zeros((1, d), F32)
            for c in range(tm // rows):
                rs = slice(c * rows, (c + 1) * rows)
                xv = x_ref[rs, :]
                r = lax.rsqrt(jnp.mean(xv * xv, axis=-1, keepdims=True) + EPS)
                xn = xv * r
                dh = acc[rs, :]
                dhg = dh * g_ref[...]
                dx_ref[rs, :] = dres_ref[rs, :] + r * (dhg - xn * jnp.mean(dhg * xn, axis=-1, keepdims=True))
                part = part + jnp.sum(dh * xn, axis=0, keepdims=True)

            @pl.when(i == 0)
            def _():
                dg_ref[...] = part

            @pl.when(i > 0)
            def _():
                dg_ref[...] += part

    def src_spec(kind):
        def index(i, p):
            if kind[0] == "cols":
                _, first, n = kind
                inner = jnp.clip(p - first, 0, n - 1)
            else:
                _, group = kind
                inner = jnp.clip((p - group + 2) // 3, 0, 2)
            return _source_index(kind, i, inner)
        return pl.BlockSpec(_source_block(kind, tm), index)

    return pl.pallas_call(
        body, name=name, grid=(t // tm, n_blocks),
        in_specs=[src_spec(kind) for _, kind in sources] + [
            pl.BlockSpec((CB, d), lambda i, p: (p, 0)),
            pl.BlockSpec((tm, d), lambda i, p: (i, 0), pipeline_mode=pl.Buffered(epilogue_buffers)),
            pl.BlockSpec((1, d), lambda i, p: (0, 0)),
            pl.BlockSpec((tm, d), lambda i, p: (i, 0), pipeline_mode=pl.Buffered(epilogue_buffers))],
        out_specs=[pl.BlockSpec((tm, d), lambda i, p: (i, 0)),
                   pl.BlockSpec((1, d), lambda i, p: (0, 0))],
        out_shape=[_sds((t, d), F32), _sds((1, d), F32)],
        scratch_shapes=[pltpu.VMEM((tm, d), F32)],
        compiler_params=_cp("arbitrary", "arbitrary"),
    )(*[a for a, _ in sources], wt, x, g, dres)


def _window_counts(first_row, rows, w):
    t = first_row + lax.broadcasted_iota(jnp.int32, (rows, 1), 0)
    return jnp.minimum(t + 1, w).astype(F32)


def _tril_bf16(ws_ref, g):
    r = lax.broadcasted_iota(jnp.int32, (CHUNK, CHUNK), 0)
    c = lax.broadcasted_iota(jnp.int32, (CHUNK, CHUNK), 1)
    return jnp.where(r >= c, ws_ref[g], 0.0).astype(BF16), r >= c


def even_mix(z, wp, scale, ws, bs_col, name, tm=256):
    t = z.shape[0]
    gd = CB // len(POOL_SIZES)

    def body(a_ref, ga_ref, u_ref, v_ref, gb_ref, wp_ref, sc_ref, ws_ref, bs_ref, y_ref, pooled_ref, aext):
        i = pl.program_id(0)

        @pl.when(i == 0)
        def _():
            aext[0:HALO, :] = jnp.zeros((HALO, CB), F32)

        @pl.when(i > 0)
        def _():
            aext[0:HALO, :] = aext[tm:tm + HALO, :]

        aext[HALO:HALO + tm, :] = a_ref[...].astype(F32)
        for g, w in enumerate(POOL_SIZES):
            cols = slice(g * gd, (g + 1) * gd)
            tok = aext[HALO:HALO + tm, cols]
            s = tok
            for k in range(1, w):
                s = s + aext[HALO - k:HALO - k + tm, cols]
            pooled = (s / _window_counts(i * tm, tm, w) - tok).astype(BF16)
            pooled_ref[:, cols] = pooled
            mixed = _dot(pooled, wp_ref[g])
            silu_a, _ = _silu_and_grad(ga_ref[:, cols].astype(F32))
            y_ref[:, cols] = (mixed * sc_ref[:, cols] * silu_a).astype(BF16)
        for g in range(4):
            cols = slice(g * gd, (g + 1) * gd)
            wg, _ = _tril_bf16(ws_ref, g)
            for c in range(tm // CHUNK):
                rows = slice(c * CHUNK, (c + 1) * CHUNK)
                m = _dot(wg, v_ref[rows, cols]) + bs_ref[g]
                silu_b, _ = _silu_and_grad(gb_ref[rows, cols].astype(F32))
                y_ref[rows, CB + g * gd:CB + (g + 1) * gd] = (
                    u_ref[rows, cols].astype(F32) * m * silu_b).astype(BF16)

    zspec = lambda cb: pl.BlockSpec((tm, CB), lambda i, cb=cb: (i, cb))
    full = lambda shape: pl.BlockSpec(shape, lambda i: (0,) * len(shape))
    return pl.pallas_call(
        body, name=name, grid=(t // tm,),
        in_specs=[zspec(0), zspec(1), zspec(2), zspec(3), zspec(4),
                  full(wp.shape), full(scale.shape), full(ws.shape), full(bs_col.shape)],
        out_specs=[pl.BlockSpec((tm, 2 * CB), lambda i: (i, 0)), pl.BlockSpec((tm, CB), lambda i: (i, 0))],
        out_shape=[_sds((t, 2 * CB), BF16), _sds((t, CB), BF16)],
        scratch_shapes=[pltpu.VMEM((HALO + tm, CB), F32)],
        compiler_params=_cp("arbitrary"),
    )(z, z, z, z, z, wp, scale, ws, bs_col)


def even_mix_bwd(dy, z, pooled, wp, scale, ws, bs_col, name, tm=256):
    t = z.shape[0]
    nt = t // tm
    gd = CB // len(POOL_SIZES)

    def body(dya_ref, dyb_ref, ga_ref, u_ref, v_ref, gb_ref, pooled_ref, wp_ref, sc_ref, ws_ref, bs_ref,
             dz_ref, dwp_ref, dsc_ref, dws_ref, dbs_ref, dpext):
        i = pl.program_id(0)
        tile = nt - 1 - i

        @pl.when(i == 0)
        def _():
            dpext[tm:tm + HALO, :] = jnp.zeros((HALO, CB), F32)
            dwp_ref[...] = jnp.zeros_like(dwp_ref)
            dsc_ref[...] = jnp.zeros_like(dsc_ref)
            dws_ref[...] = jnp.zeros_like(dws_ref)
            dbs_ref[...] = jnp.zeros_like(dbs_ref)

        @pl.when(i > 0)
        def _():
            dpext[tm:tm + HALO, :] = dpext[0:HALO, :]

        for g, w in enumerate(POOL_SIZES):
            cols = slice(g * gd, (g + 1) * gd)
            pooled_g = pooled_ref[:, cols]
            mixed = _dot(pooled_g, wp_ref[g])
            silu_a, dsilu_a = _silu_and_grad(ga_ref[:, cols].astype(F32))
            dya = dya_ref[:, cols].astype(F32)
            sc = sc_ref[:, cols]
            dmixed = (dya * sc * silu_a).astype(BF16)
            dsc_ref[:, cols] += jnp.sum(dya * mixed * silu_a, axis=0, keepdims=True)
            dz_ref[:, CB + g * gd:CB + (g + 1) * gd] = (dya * mixed * sc * dsilu_a).astype(BF16)
            dwp_ref[g] += _dot_tn(pooled_g, dmixed)
            dpooled = _dot_nt(dmixed, wp_ref[g])
            dpext[0:tm, cols] = dpooled / _window_counts(tile * tm, tm, w)
            s = dpext[0:tm, cols]
            for k in range(1, w):
                s = s + dpext[k:k + tm, cols]
            dz_ref[:, cols] = (s - dpooled).astype(BF16)
        for g in range(4):
            cols = slice(g * gd, (g + 1) * gd)
            wg, lower = _tril_bf16(ws_ref, g)
            dws_g = jnp.zeros((CHUNK, CHUNK), F32)
            dbs_g = jnp.zeros((CHUNK, 1), F32)
            for c in range(tm // CHUNK):
                rows = slice(c * CHUNK, (c + 1) * CHUNK)
                vb = v_ref[rows, cols]
                m = _dot(wg, vb) + bs_ref[g]
                gbv = gb_ref[rows, cols].astype(F32)
                silu_b, dsilu_b = _silu_and_grad(gbv)
                dyb = dyb_ref[rows, cols].astype(F32)
                uv = u_ref[rows, cols].astype(F32)
                dm = dyb * silu_b * uv
                dmb = dm.astype(BF16)
                dz_ref[rows, 2 * CB + g * gd:2 * CB + (g + 1) * gd] = (dyb * silu_b * m).astype(BF16)
                dz_ref[rows, 3 * CB + g * gd:3 * CB + (g + 1) * gd] = _dot_tn(wg, dmb).astype(BF16)
                dz_ref[rows, 4 * CB + g * gd:4 * CB + (g + 1) * gd] = (dyb * uv * m * dsilu_b).astype(BF16)
                dws_g = dws_g + _dot_nt(dmb, vb)
                dbs_g = dbs_g + jnp.sum(dm, axis=1, keepdims=True)
            dws_ref[g] += jnp.where(lower, dws_g, 0.0)
            dbs_ref[g] += dbs_g

    rev = lambda cb: pl.BlockSpec((tm, CB), lambda i, cb=cb: (nt - 1 - i, cb))
    full = lambda shape: pl.BlockSpec(shape, lambda i: (0,) * len(shape))
    return pl.pallas_call(
        body, name=name, grid=(nt,),
        in_specs=[rev(0), rev(1), rev(1), rev(2), rev(3), rev(4), rev(0),
                  full(wp.shape), full(scale.shape), full(ws.shape), full(bs_col.shape)],
        out_specs=[pl.BlockSpec((tm, 5 * CB), lambda i: (nt - 1 - i, 0)),
                   full(wp.shape), full(scale.shape), full(ws.shape), full(bs_col.shape)],
        out_shape=[_sds((t, 5 * CB), BF16), _sds(wp.shape, F32), _sds(scale.shape, F32),
                   _sds(ws.shape, F32), _sds(bs_col.shape, F32)],
        scratch_shapes=[pltpu.VMEM((tm + HALO, CB), F32)],
        compiler_params=_cp("arbitrary"),
    )(dy, dy, z, z, z, z, pooled, wp, scale, ws, bs_col)


def _slope(group, head):
    return float(2.0 ** (-8.0 * (group * HEADS_PER_GROUP + head + 1) / N_HEADS))


def _band(dilation):
    qi = lax.broadcasted_iota(jnp.int32, (ATTN_BLOCK, ATTN_BLOCK), 0)
    ki = lax.broadcasted_iota(jnp.int32, (ATTN_BLOCK, ATTN_BLOCK), 1)
    dist_prev = ((qi + ATTN_BLOCK - ki) * dilation).astype(F32)
    dist_cur = ((qi - ki) * dilation).astype(F32)
    return dist_prev, ki >= qi, dist_cur, ki <= qi


def _permute_rows(name, d, inverse, arrays, in_specs, out_shapes, out_specs, n_chunks, widths):
    rows = ATTN_BLOCK * d
    t = out_shapes[0].shape[-2]
    n = len(arrays)

    lanes = HEAD_DIM

    def body(*refs):
        for a, o, s, w in zip(refs[:n], refs[n:2 * n], refs[2 * n:], widths):
            for k in range(w // lanes):
                cols = slice(k * lanes, (k + 1) * lanes)
                if inverse:
                    for r in range(d):
                        s[k, pl.ds(r, ATTN_BLOCK, stride=d), :] = (
                            a[r * ATTN_BLOCK:(r + 1) * ATTN_BLOCK, cols].astype(F32))
                    o[:, cols] = s[k].astype(o.dtype)
                else:
                    s[k] = a[:, cols].astype(F32)
                    for r in range(d):
                        o[r * ATTN_BLOCK:(r + 1) * ATTN_BLOCK, cols] = (
                            s[k, pl.ds(r, ATTN_BLOCK, stride=d), :].astype(o.dtype))

    return pl.pallas_call(
        body, name=name, grid=(t // rows, n_chunks),
        in_specs=in_specs, out_specs=out_specs, out_shape=out_shapes,
        scratch_shapes=[pltpu.VMEM((w // lanes, rows, lanes), F32) for w in widths],
        compiler_params=_cp("parallel", "arbitrary"),
    )(*arrays)


def _chunk_width(d):
    return 4 * CB // d


def permute_qkv(z, group, name):
    d = DILATIONS[group]
    rows, cw = ATTN_BLOCK * d, _chunk_width(d)
    per = CB // cw
    src = pl.BlockSpec((rows, cw), lambda i, c: (i, (c // per * 3 + group) * per + c % per))
    dst = pl.BlockSpec((rows, cw), lambda i, c: (i, c))
    (out,) = _permute_rows(name, d, False, [z], [src], [_sds((z.shape[0], 3 * CB), z.dtype)], [dst], 3 * per, [cw])
    return out


def permute_mats(wide, narrow, group, inverse, name):
    d = DILATIONS[group]
    rows, cw = ATTN_BLOCK * d, _chunk_width(d)
    wide_spec = pl.BlockSpec((rows, cw), lambda i, c: (i, c))
    narrow_spec = pl.BlockSpec((rows, HEAD_DIM), lambda i, c: (i, 0))
    arrays = [wide] + list(narrow)
    specs = [wide_spec] + [narrow_spec] * len(narrow)
    return _permute_rows(name, d, inverse, arrays, specs, [_sds(a.shape, a.dtype) for a in arrays], specs,
                         CB // cw, [cw] + [HEAD_DIM] * len(narrow))


def permute_slabs(s, group, inverse, name):
    d = DILATIONS[group]
    rows, cw = ATTN_BLOCK * d, _chunk_width(d)
    per = CB // cw
    spec = pl.BlockSpec((None, rows, cw), lambda i, c: (c // per, i, c % per))
    (out,) = _permute_rows(name, d, inverse, [s], [spec], [_sds(s.shape, s.dtype)], [spec], 3 * per, [cw])
    return out


def _lane_pack(cols):
    rows = cols[0].shape[0]
    lane = lax.broadcasted_iota(jnp.int32, (rows, HEAD_DIM), 1)
    out = jnp.zeros((rows, HEAD_DIM), F32)
    for h, c in enumerate(cols):
        out = jnp.where(lane == h, c, out)
    return out


def _qkv_col(group, from_z):
    return (lambda which: which * 3 + group) if from_z else (lambda which: which)


def _scores(q, kp, kc, group, h, dist_p, ok_p, dist_c, ok_c):
    scale = HEAD_DIM ** -0.5
    slope = _slope(group, h)
    sp = jnp.where(ok_p, _dot_nt(q, kp) * scale - slope * dist_p, NEG)
    sc = jnp.where(ok_c, _dot_nt(q, kc) * scale - slope * dist_c, NEG)
    return sp, sc


def attn_fwd(qkv, group, name):
    t = qkv.shape[0]
    d = DILATIONS[group]
    col = _qkv_col(group, qkv.shape[1] != 3 * CB)
    heads = range(HEADS_PER_GROUP)

    def body(q_ref, kp_ref, kc_ref, vp_ref, vc_ref, o_ref, lse_ref):
        b = pl.program_id(0)
        dist_p, ok_p, dist_c, ok_c = _band(d)
        ok_p = jnp.logical_and(ok_p, b >= d)
        hs = [slice(h * HEAD_DIM, (h + 1) * HEAD_DIM) for h in heads]
        s = [_scores(q_ref[:, hs[h]], kp_ref[:, hs[h]], kc_ref[:, hs[h]], group, h, dist_p, ok_p, dist_c, ok_c)
             for h in heads]
        m = [jnp.maximum(jnp.max(sp, axis=1, keepdims=True), jnp.max(sc, axis=1, keepdims=True)) for sp, sc in s]
        e = [(jnp.exp(sp - m[h]), jnp.exp(sc - m[h])) for h, (sp, sc) in enumerate(s)]
        l = [jnp.sum(ep, axis=1, keepdims=True) + jnp.sum(ec, axis=1, keepdims=True) for ep, ec in e]
        inv = [1.0 / lh for lh in l]
        o = [_dot((ep * inv[h]).astype(BF16), vp_ref[:, hs[h]]) + _dot((ec * inv[h]).astype(BF16), vc_ref[:, hs[h]])
             for h, (ep, ec) in enumerate(e)]
        for h in heads:
            o_ref[:, hs[h]] = o[h].astype(BF16)
        lse_ref[...] = _lane_pack([m[h] + jnp.log(l[h]) for h in heads])

    blk = (ATTN_BLOCK, CB)
    cur = lambda which: pl.BlockSpec(blk, lambda b: (b, col(which)))
    prev = lambda which: pl.BlockSpec(blk, lambda b: (jnp.maximum(b - d, 0), col(which)))
    return pl.pallas_call(
        body, name=name, grid=(t // ATTN_BLOCK,),
        in_specs=[cur(0), prev(1), cur(1), prev(2), cur(2)],
        out_specs=[pl.BlockSpec(blk, lambda b: (b, 0)), pl.BlockSpec((ATTN_BLOCK, HEAD_DIM), lambda b: (b, 0))],
        out_shape=[_sds((t, CB), BF16), _sds((t, HEAD_DIM), F32)],
        compiler_params=_cp("parallel"),
    )(qkv, qkv, qkv, qkv, qkv)


def attn_bwd(z, dyc, lse, dd, group, name):
    t = z.shape[0]
    d = DILATIONS[group]
    nb = t // (ATTN_BLOCK * d)
    col = _qkv_col(group, z.shape[1] != 3 * CB)
    scale = HEAD_DIM ** -0.5
    heads = range(HEADS_PER_GROUP)

    def body(q_ref, kp_ref, kc_ref, vp_ref, vc_ref, dy_ref, lse_ref, dd_ref, out_ref, carry):
        n = pl.program_id(1)

        @pl.when(n == 0)
        def _():
            carry[...] = jnp.zeros_like(carry)

        @pl.when(n == nb)
        def _():
            out_ref[...] = carry[...].astype(BF16)

        @pl.when(n < nb)
        def _():
            dist_p, ok_p, dist_c, ok_c = _band(d)
            ok_p = jnp.logical_and(ok_p, n > 0)
            hs = [slice(h * HEAD_DIM, (h + 1) * HEAD_DIM) for h in heads]
            q = [q_ref[:, hs[h]] for h in heads]
            kp = [kp_ref[:, hs[h]] for h in heads]
            kc = [kc_ref[:, hs[h]] for h in heads]
            dy = [dy_ref[:, hs[h]] for h in heads]
            old = [[carry[w, :, hs[h]] for h in heads] for w in range(3)]
            s = [_scores(q[h], kp[h], kc[h], group, h, dist_p, ok_p, dist_c, ok_c) for h in heads]
            p = [(jnp.exp(sp - lse_ref[:, h:h + 1]), jnp.exp(sc - lse_ref[:, h:h + 1])) for h, (sp, sc) in enumerate(s)]
            ds = [((pp * (_dot_nt(dy[h], vp_ref[:, hs[h]]) - dd_ref[:, h:h + 1]) * scale).astype(BF16),
                   (pc * (_dot_nt(dy[h], vc_ref[:, hs[h]]) - dd_ref[:, h:h + 1]) * scale).astype(BF16))
                  for h, (pp, pc) in enumerate(p)]
            for h in heads:
                dsp, dsc = ds[h]
                pp, pc = p[h]
                out_ref[0, :, hs[h]] = old[0][h].astype(BF16)
                out_ref[1, :, hs[h]] = (old[1][h] + _dot_tn(dsp, q[h])).astype(BF16)
                out_ref[2, :, hs[h]] = (old[2][h] + _dot_tn(pp.astype(BF16), dy[h])).astype(BF16)
                carry[0, :, hs[h]] = _dot(dsp, kp[h]) + _dot(dsc, kc[h])
                carry[1, :, hs[h]] = _dot_tn(dsc, q[h])
                carry[2, :, hs[h]] = _dot_tn(pc.astype(BF16), dy[h])

    blk = (ATTN_BLOCK, CB)
    cur_row = lambda r, n: jnp.minimum(n, nb - 1) * d + r
    prev_row = lambda r, n: jnp.clip(n - 1, 0, nb - 1) * d + r
    cur = lambda which: pl.BlockSpec(blk, lambda r, n: (cur_row(r, n), col(which)))
    prev = lambda which: pl.BlockSpec(blk, lambda r, n: (prev_row(r, n), col(which)))
    narrow = pl.BlockSpec((ATTN_BLOCK, HEAD_DIM), lambda r, n: (cur_row(r, n), 0))
    return pl.pallas_call(
        body, name=name, grid=(d, nb + 1),
        in_specs=[cur(0), prev(1), cur(1), prev(2), cur(2),
                  pl.BlockSpec(blk, lambda r, n: (cur_row(r, n), 0)), narrow, narrow],
        out_specs=pl.BlockSpec((3, ATTN_BLOCK, CB), lambda r, n: (0, prev_row(r, n), 0)),
        out_shape=_sds((3, t, CB), BF16),
        scratch_shapes=[pltpu.VMEM((3, ATTN_BLOCK, CB), F32)],
        compiler_params=_cp("parallel", "arbitrary"),
    )(z, z, z, z, z, dyc, lse, dd)


def odd_mix(os_, lses, z, conv_w, name, tm=256):
    t = z.shape[0]

    def body(o0, o1, o2, l0, l1, l2, gc_ref, db_ref, dc_ref, dx_ref, gd_ref, cw_ref,
             y_ref, yc_ref, lse_ref, conv_ref, zext):
        i = pl.program_id(0)
        a0, a1, a2 = l0[...], l1[...], l2[...]
        m = jnp.maximum(jnp.maximum(a0, a1), a2)
        tot = m + jnp.log(jnp.exp(a0 - m) + jnp.exp(a1 - m) + jnp.exp(a2 - m))
        lse_ref[...] = tot
        w0, w1, w2 = jnp.exp(a0 - tot), jnp.exp(a1 - tot), jnp.exp(a2 - tot)
        for h in range(HEADS_PER_GROUP):
            hs = slice(h * HEAD_DIM, (h + 1) * HEAD_DIM)
            yc = (w0[:, h:h + 1] * o0[:, hs].astype(F32) + w1[:, h:h + 1] * o1[:, hs].astype(F32)
                  + w2[:, h:h + 1] * o2[:, hs].astype(F32))
            yc_ref[:, hs] = yc.astype(BF16)
            silu_c, _ = _silu_and_grad(gc_ref[:, hs].astype(F32))
            y_ref[:, hs] = (yc * silu_c).astype(BF16)

        @pl.when(i == 0)
        def _():
            zext[0:HALO, :] = jnp.zeros((HALO, CB), F32)

        @pl.when(i > 0)
        def _():
            zext[0:HALO, :] = zext[tm:tm + HALO, :]

        zext[HALO:HALO + tm, :] = dc_ref[...].astype(F32) * dx_ref[...].astype(F32)
        conv = (cw_ref[0:1, :] * zext[HALO - 2:HALO - 2 + tm, :] + cw_ref[1:2, :] * zext[HALO - 1:HALO - 1 + tm, :]
                + cw_ref[2:3, :] * zext[HALO:HALO + tm, :])
        conv_ref[...] = conv.astype(BF16)
        silu_d, _ = _silu_and_grad(gd_ref[...].astype(F32))
        y_ref[:, CB:2 * CB] = (db_ref[...].astype(F32) * conv * silu_d).astype(BF16)

    row = pl.BlockSpec((tm, CB), lambda i: (i, 0))
    narrow = pl.BlockSpec((tm, HEAD_DIM), lambda i: (i, 0))
    zspec = lambda cb: pl.BlockSpec((tm, CB), lambda i, cb=cb: (i, cb))
    return pl.pallas_call(
        body, name=name, grid=(t // tm,),
        in_specs=[row] * 3 + [narrow] * 3 + [zspec(9), zspec(10), zspec(11), zspec(12), zspec(13),
                                             pl.BlockSpec(conv_w.shape, lambda i: (0, 0))],
        out_specs=[pl.BlockSpec((tm, 2 * CB), lambda i: (i, 0)), row, narrow, row],
        out_shape=[_sds((t, 2 * CB), BF16), _sds((t, CB), BF16), _sds((t, HEAD_DIM), F32), _sds((t, CB), BF16)],
        scratch_shapes=[pltpu.VMEM((HALO + tm, CB), F32)],
        compiler_params=_cp("arbitrary"),
    )(*os_, *lses, z, z, z, z, z, conv_w)


def odd_mix_bwd(dy, yc, conv, z, conv_w, name, tm=256):
    t = z.shape[0]
    nt = t // tm

    def body(dyc_ref, dyd_ref, yc_ref, conv_ref, gc_ref, db_ref, dc_ref, dx_ref, gd_ref, cw_ref,
             dz_ref, dyo_ref, dd_ref, dcw_ref, dcext):
        i = pl.program_id(0)

        @pl.when(i == 0)
        def _():
            dcext[tm:tm + HALO, :] = jnp.zeros((HALO, CB), F32)
            dcw_ref[...] = jnp.zeros_like(dcw_ref)

        @pl.when(i > 0)
        def _():
            dcext[tm:tm + HALO, :] = dcext[0:HALO, :]

        silu_c, dsilu_c = _silu_and_grad(gc_ref[...].astype(F32))
        dyc = dyc_ref[...].astype(F32)
        ycv = yc_ref[...].astype(F32)
        dyo = dyc * silu_c
        dyo_ref[...] = dyo.astype(BF16)
        dz_ref[:, 0:CB] = (dyc * ycv * dsilu_c).astype(BF16)
        prod = dyo * ycv
        dd_ref[...] = _lane_pack([jnp.sum(prod[:, h * HEAD_DIM:(h + 1) * HEAD_DIM], axis=1, keepdims=True)
                                  for h in range(HEADS_PER_GROUP)])

        silu_d, dsilu_d = _silu_and_grad(gd_ref[...].astype(F32))
        dyd = dyd_ref[...].astype(F32)
        convv = conv_ref[...].astype(F32)
        dbv = db_ref[...].astype(F32)
        dz_ref[:, CB:2 * CB] = (dyd * convv * silu_d).astype(BF16)
        dz_ref[:, 4 * CB:5 * CB] = (dyd * dbv * convv * dsilu_d).astype(BF16)
        dcext[0:tm, :] = dyd * dbv * silu_d
        dcv, dxv = dc_ref[...].astype(F32), dx_ref[...].astype(F32)
        zc = dcv * dxv
        d0, d1, d2 = dcext[0:tm, :], dcext[1:1 + tm, :], dcext[2:2 + tm, :]
        dzc = cw_ref[2:3, :] * d0 + cw_ref[1:2, :] * d1 + cw_ref[0:1, :] * d2
        dz_ref[:, 2 * CB:3 * CB] = (dzc * dxv).astype(BF16)
        dz_ref[:, 3 * CB:4 * CB] = (dzc * dcv).astype(BF16)
        dcw_ref[0:1, :] += jnp.sum(zc * d2, axis=0, keepdims=True)
        dcw_ref[1:2, :] += jnp.sum(zc * d1, axis=0, keepdims=True)
        dcw_ref[2:3, :] += jnp.sum(zc * d0, axis=0, keepdims=True)

    rev = lambda cb: pl.BlockSpec((tm, CB), lambda i, cb=cb: (nt - 1 - i, cb))
    return pl.pallas_call(
        body, name=name, grid=(nt,),
        in_specs=[rev(0), rev(1), rev(0), rev(0), rev(9), rev(10), rev(11), rev(12), rev(13),
                  pl.BlockSpec(conv_w.shape, lambda i: (0, 0))],
        out_specs=[pl.BlockSpec((tm, 5 * CB), lambda i: (nt - 1 - i, 0)), rev(0),
                   pl.BlockSpec((tm, HEAD_DIM), lambda i: (nt - 1 - i, 0)),
                   pl.BlockSpec(conv_w.shape, lambda i: (0, 0))],
        out_shape=[_sds((t, 5 * CB), BF16), _sds((t, CB), BF16), _sds((t, HEAD_DIM), F32), _sds(conv_w.shape, F32)],
        scratch_shapes=[pltpu.VMEM((tm + HALO, CB), F32)],
        compiler_params=_cp("arbitrary"),
    )(dy, dy, yc, conv, z, z, z, z, z, conv_w)


def final_norm_loss(x, g, target, name, tm=512):
    t, d = x.shape

    def body(x_ref, g_ref, t_ref, dx_ref, loss_ref, dg_ref):
        i = pl.program_id(0)
        xv = x_ref[...]
        r = lax.rsqrt(jnp.mean(xv * xv, axis=-1, keepdims=True) + EPS)
        xn = xv * r
        gv = g_ref[...]
        err = xn * gv - t_ref[...]
        loss = 0.5 * jnp.sum(jnp.mean(err * err, axis=-1, keepdims=True), axis=0, keepdims=True)
        dy = err * (1.0 / d)
        dyg = dy * gv
        dx_ref[...] = r * (dyg - xn * jnp.mean(dyg * xn, axis=-1, keepdims=True))
        dg = jnp.sum(dy * xn, axis=0, keepdims=True)

        @pl.when(i == 0)
        def _():
            loss_ref[...] = jnp.broadcast_to(loss, loss_ref.shape)
            dg_ref[...] = dg

        @pl.when(i > 0)
        def _():
            loss_ref[...] += jnp.broadcast_to(loss, loss_ref.shape)
            dg_ref[...] += dg

    return pl.pallas_call(
        body, name=name, grid=(t // tm,),
        in_specs=[pl.BlockSpec((tm, d), lambda i: (i, 0)), pl.BlockSpec((1, d), lambda i: (0, 0)),
                  pl.BlockSpec((tm, d), lambda i: (i, 0))],
        out_specs=[pl.BlockSpec((tm, d), lambda i: (i, 0)), pl.BlockSpec((8, 128), lambda i: (0, 0)),
                   pl.BlockSpec((1, d), lambda i: (0, 0))],
        out_shape=[_sds((t, d), F32), _sds((8, 128), F32), _sds((1, d), F32)],
        compiler_params=_cp("arbitrary"),
    )(x, g, target)


def _position():
    x, y, c = lax.axis_index("x"), lax.axis_index("y"), lax.axis_index("c")
    return x, y, c, 4 * x + 2 * y + c


def _peer(x, y, c, k):
    px = 1 - x if k & 4 else x
    py = 1 - y if k & 2 else y
    pc = 1 - c if k & 1 else c
    return (px, py, pc), 4 * px + 2 * py + pc


def _block(ref, axis, size, idx):
    index = [slice(None)] * len(ref.shape)
    index[axis] = pl.ds(idx * size, size)
    return ref.at[tuple(index)]


def all_gather(shards, axes, name):
    n = len(shards)
    fulls = []
    for s, ax in zip(shards, axes):
        shape = list(s.shape)
        shape[ax] *= N_DEV
        fulls.append(_sds(tuple(shape), s.dtype))

    def body(*refs):
        ins, outs = refs[:n], refs[n:2 * n]
        send_sems, recv_sems, local_sems = refs[2 * n:]
        x, y, c, me = _position()
        local = []
        for p in range(n):
            size = ins[p].shape[axes[p]]
            cp = pltpu.make_async_copy(ins[p], _block(outs[p], axes[p], size, me), local_sems.at[p])
            cp.start()
            local.append(cp)
        for k in range(1, N_DEV):
            peer, _ = _peer(x, y, c, k)
            for p in range(n):
                size = ins[p].shape[axes[p]]
                pltpu.make_async_remote_copy(
                    src_ref=ins[p], dst_ref=_block(outs[p], axes[p], size, me),
                    send_sem=send_sems.at[p], recv_sem=recv_sems.at[p],
                    device_id=peer, device_id_type=MESH_ID).start()
        for p in range(n):
            size = ins[p].shape[axes[p]]
            seven = _block(outs[p], axes[p], (N_DEV - 1) * size, 0)
            pltpu.make_async_remote_copy(
                src_ref=seven, dst_ref=seven, send_sem=send_sems.at[p], recv_sem=recv_sems.at[p],
                device_id=(x, y, c), device_id_type=MESH_ID).wait()
            local[p].wait()

    any_spec = pl.BlockSpec(memory_space=pl.ANY)
    return pl.pallas_call(
        body, name=name,
        in_specs=[any_spec] * n, out_specs=[any_spec] * n, out_shape=fulls,
        scratch_shapes=[pltpu.SemaphoreType.DMA((n,)), pltpu.SemaphoreType.DMA((n,)), pltpu.SemaphoreType.DMA((n,))],
    )(*shards)


_HBM = pl.BlockSpec(memory_space=pltpu.HBM)
_SEM = pl.BlockSpec(memory_space=pltpu.SEMAPHORE)
_EFFECT = pltpu.SideEffectType.DATAFLOW_SIDE_EFFECTING


def _in_hbm(a):
    return pltpu.with_memory_space_constraint(a, pltpu.HBM)


def _landing(mode, src, axis):
    me = 4 * lax.axis_index("x") + 2 * lax.axis_index("y") + lax.axis_index("c")
    if mode == "gather":
        shape = list(src.shape)
        size = shape[axis]
        shape[axis] *= N_DEV
        return lax.dynamic_update_slice_in_dim(lax.empty(tuple(shape), src.dtype), src, me * size, axis)
    size = src.shape[axis] // N_DEV
    own = lax.dynamic_slice_in_dim(src, me * size, size, axis)
    return lax.dynamic_update_slice_in_dim(lax.empty((N_DEV, *own.shape), src.dtype), own[None], me, 0)


def _seven(mode, land_ref, axis):
    if mode == "gather":
        return _block(land_ref, axis, (N_DEV - 1) * (land_ref.shape[axis] // N_DEV), 0)
    return land_ref.at[pl.ds(0, N_DEV - 1)]


def exchange_start(mode, srcs, axes, after, name):
    n = len(srcs)
    lands = [_landing(mode, s, ax) for s, ax in zip(srcs, axes)]

    def body(*refs):
        src, land = refs[:n], refs[n:2 * n]
        send_sems, recv_sems = refs[2 * n + 1], refs[2 * n + 2]
        token = refs[-1]
        x, y, c, me = _position()
        for k in range(1, N_DEV):
            peer, pj = _peer(x, y, c, k)
            for p in range(n):
                if mode == "gather":
                    s = src[p]
                    dst = _block(land[p], axes[p], src[p].shape[axes[p]], me)
                else:
                    s = _block(src[p], axes[p], src[p].shape[axes[p]] // N_DEV, pj)
                    dst = land[p].at[me]
                pltpu.make_async_remote_copy(
                    src_ref=s, dst_ref=dst, send_sem=send_sems.at[p], recv_sem=recv_sems.at[p],
                    device_id=peer, device_id_type=MESH_ID).start()
        token[...] = jnp.zeros_like(token)

    outs = pl.pallas_call(
        body, name=name,
        out_shape=(pltpu.SemaphoreType.DMA((n,)), pltpu.SemaphoreType.DMA((n,)),
                   *[pltpu.HBM(s.shape, s.dtype) for s in srcs], *[pltpu.HBM(l.shape, l.dtype) for l in lands],
                   _sds((8, 128), F32)),
        in_specs=[_HBM] * (2 * n) + [pl.BlockSpec(memory_space=pl.ANY)],
        out_specs=(_SEM, _SEM, *[_HBM] * (2 * n), pl.BlockSpec(memory_space=pltpu.VMEM)),
        input_output_aliases={i: 2 + i for i in range(2 * n)},
        compiler_params=pltpu.CompilerParams(has_side_effects=_EFFECT),
    )(*[_in_hbm(s) for s in srcs], *[_in_hbm(l) for l in lands], after)
    return outs[0], outs[1], list(outs[2:2 + n]), list(outs[2 + n:2 + 2 * n]), outs[-1]


def exchange_wait(mode, started, axes, after, name):
    send_sems, recv_sems, srcs, lands, _ = started
    n = len(srcs)

    def body(*refs):
        land = refs[n:2 * n]
        send_ref, recv_ref = refs[2 * n], refs[2 * n + 1]
        x, y, c, _ = _position()
        for p in range(n):
            seven = _seven(mode, land[p], axes[p])
            cp = pltpu.make_async_remote_copy(
                src_ref=seven, dst_ref=seven, send_sem=send_ref.at[p], recv_sem=recv_ref.at[p],
                device_id=(x, y, c), device_id_type=MESH_ID)
            cp.wait_send()
            cp.wait_recv()

    outs = pl.pallas_call(
        body, name=name,
        out_shape=(*[pltpu.HBM(s.shape, s.dtype) for s in srcs], *[pltpu.HBM(l.shape, l.dtype) for l in lands]),
        in_specs=[_HBM] * (2 * n) + [_SEM, _SEM, pl.BlockSpec(memory_space=pl.ANY)],
        out_specs=tuple([_HBM] * (2 * n)),
        input_output_aliases={i: i for i in range(2 * n)},
        compiler_params=pltpu.CompilerParams(has_side_effects=_EFFECT),
    )(*srcs, *lands, send_sems, recv_sems, after)
    return list(outs[n:])


def _adam_math(g, w, m, v):
    m2 = ADAM_B1 * m + (1.0 - ADAM_B1) * g
    v2 = ADAM_B2 * v + (1.0 - ADAM_B2) * (g * g)
    m_hat = m2 / (1.0 - ADAM_B1 ** ADAM_STEP)
    v_hat = v2 / (1.0 - ADAM_B2 ** ADAM_STEP)
    delta = -ADAM_LR * (m_hat / (jnp.sqrt(v_hat) + ADAM_EPS) + ADAM_WD * w)
    return delta, m2, v2


def adamw(g, w, m, v, name, slots, tr=128):
    r, c = w.shape
    tr = min(tr, r)

    def body(g_ref, w_ref, m_ref, v_ref, go_ref, d_ref, mo_ref, vo_ref):
        if slots:
            gs = g_ref[0].astype(F32)
            for s in range(1, N_DEV):
                gs = gs + g_ref[s].astype(F32)
        else:
            gs = g_ref[...]
        go_ref[...] = gs
        d_ref[...], mo_ref[...], vo_ref[...] = _adam_math(gs, w_ref[...], m_ref[...], v_ref[...])

    row = pl.BlockSpec((tr, c), lambda i: (i, 0))
    gspec = pl.BlockSpec((N_DEV, tr, c), lambda i: (0, i, 0)) if slots else row
    return pl.pallas_call(
        body, name=name, grid=(r // tr,),
        in_specs=[gspec, row, row, row], out_specs=[row] * 4, out_shape=[_sds((r, c), F32)] * 4,
        compiler_params=_cp("parallel"),
    )(g, w, m, v)


def sum_slots(g, name):
    _, r, c = g.shape

    def body(g_ref, o_ref):
        gs = g_ref[0]
        for s in range(1, N_DEV):
            gs = gs + g_ref[s]
        o_ref[...] = gs

    return pl.pallas_call(
        body, name=name, grid=(1,),
        in_specs=[pl.BlockSpec((N_DEV, r, c), lambda i: (0, 0, 0))],
        out_specs=pl.BlockSpec((r, c), lambda i: (0, 0)), out_shape=_sds((r, c), F32),
        compiler_params=_cp("arbitrary"),
    )(g)


def _rows128(a, pad_to=8):
    a = a.reshape(-1, 128)
    pad = (-a.shape[0]) % pad_to
    return jnp.pad(a, ((0, pad), (0, 0))) if pad else a


def kernel(x, even_norm, even_w_in, even_pool_w, even_pool_scale, even_ws, even_bs, even_w_out, odd_norm, odd_w_in, odd_conv_w, odd_w_out, final_norm, loss_target, m_even_norm, m_even_w_in, m_even_pool_w, m_even_pool_scale, m_even_ws, m_even_bs, m_even_w_out, m_odd_norm, m_odd_w_in, m_odd_conv_w, m_odd_w_out, m_final_norm, v_even_norm, v_even_w_in, v_even_pool_w, v_even_pool_scale, v_even_ws, v_even_bs, v_even_w_out, v_odd_norm, v_odd_w_in, v_odd_conv_w, v_odd_w_out, v_final_norm):
    x0 = x[0]
    target = loss_target[0]
    me = 4 * lax.axis_index("x") + 2 * lax.axis_index("y") + lax.axis_index("c")

    (we_in,) = all_gather([even_w_in[0].astype(BF16)], [1], "gather_even_in")
    odd_small = jnp.pad(odd_norm, ((0, 7), (0, 0))) + jnp.pad(odd_conv_w[0], ((1, 4), (0, 0)))
    rest_axes, odd_axes = [1, 0], [1, 0, 1]
    rest_flight = exchange_start("gather", [even_pool_w[0].astype(BF16), even_w_out[0].astype(BF16)],
                                 rest_axes, we_in, "gather_even_rest_start")
    odd_flight = exchange_start("gather", [odd_w_in[0].astype(BF16), odd_w_out[0].astype(BF16), odd_small],
                                odd_axes, rest_flight[4], "gather_odd_start")
    scale = even_pool_scale
    ws = even_ws[0]
    bs_col = even_bs[0][:, :, None]
    g_fin = final_norm[None, :]

    z_e, h_e = norm_matmul(x0, even_norm + odd_flight[4][0:1, 0:1], we_in, "even_in", tn=1280)
    wp, we_out = exchange_wait("gather", rest_flight, rest_axes, z_e, "gather_even_rest_wait")
    y_e, pooled = even_mix(z_e, wp, scale, ws, bs_col, "even_mix")
    x1 = matmul_residual(y_e, we_out, x0, "even_out")
    wo_in, wo_out, odd_small = exchange_wait("gather", odd_flight, odd_axes, x1, "gather_odd_wait")
    g_odd, conv_w = odd_small[0:1], odd_small[1:4]
    z_o, h_o = norm_matmul(x1, g_odd, wo_in, "odd_in", tn=2048)
    qkv = [z_o] + [permute_qkv(z_o, gi, f"dilate_qkv_{gi}") for gi in (1, 2)]
    attn = [attn_fwd(qkv[gi], gi, f"attn_fwd_{gi}") for gi in range(3)]
    for gi in (1, 2):
        attn[gi] = permute_mats(attn[gi][0], [attn[gi][1]], gi, True, f"undilate_attn_{gi}")
    y_o, yc, lse_tot, conv = odd_mix([a[0] for a in attn], [a[1] for a in attn], z_o, conv_w, "odd_mix")
    x2 = matmul_residual(y_o, wo_out, x1, "odd_out")
    dx2, loss_blk, dg_fin = final_norm_loss(x2, g_fin, target, "final_norm_loss")

    dy_o, dwo_out = out_proj_bwd(dx2, wo_out.T, y_o, "odd_out_bwd")
    dz5_o, dyc, dd, dconv_w = odd_mix_bwd(dy_o, yc, conv, z_o, conv_w, "odd_mix_bwd")
    dqkv = [attn_bwd(z_o, dyc, lse_tot, dd, 0, "attn_bwd_0")]
    for gi in (1, 2):
        dyc_g, lse_g, dd_g = permute_mats(dyc, [lse_tot, dd], gi, False, f"dilate_dy_{gi}")
        dqkv.append(permute_slabs(attn_bwd(qkv[gi], dyc_g, lse_g, dd_g, gi, f"attn_bwd_{gi}"),
                                  gi, True, f"undilate_dqkv_{gi}"))
    src_o = [(dqkv[0], ("slabs", 0)), (dqkv[1], ("slabs", 1)), (dqkv[2], ("slabs", 2)), (dz5_o, ("cols", 9, 5))]
    dwo_in = in_proj_dw(h_o, src_o, ODD_COLBLOCKS, "odd_in_dw")
    odd_grads = exchange_start("scatter", [dwo_in, dwo_out], [1, 0], dconv_w, "scatter_odd_start")
    dx1, dg_odd = in_proj_dx(src_o, ODD_COLBLOCKS, wo_in.T, x1, g_odd + odd_grads[4][0:1, 0:1], dx2, "odd_in_dx",
                             epilogue_buffers=1)

    dy_e, dwe_out = out_proj_bwd(dx1, we_out.T, y_e, "even_out_bwd")
    dz_e, dwp, dscale, dws, dbs_col = even_mix_bwd(dy_e, z_e, pooled, wp, scale, ws, bs_col, "even_mix_bwd")
    src_e = [(dz_e, ("cols", 0, 5))]
    dwe_in = in_proj_dw(h_e, src_e, 5, "even_in_dw")
    even_grads = exchange_start("scatter", [dwe_in, dwp.astype(BF16), dwe_out], [1, 1, 0], dscale, "scatter_even_start")
    dx0, dg_even = in_proj_dx(src_e, 5, we_in.T, x0, even_norm + even_grads[4][0:1, 0:1], dx1, "even_in_dx")
    p_wo_in, p_wo_out = exchange_wait("scatter", odd_grads, [1, 0], dx0, "scatter_odd_wait")
    p_we_in, p_wp, p_we_out = exchange_wait("scatter", even_grads, [1, 1, 0], dx0, "scatter_even_wait")
    small = jnp.concatenate([
        _rows128(dg_even), _rows128(dscale), _rows128(dws), _rows128(dbs_col), _rows128(dg_fin),
        _rows128(dg_odd), _rows128(dconv_w)], axis=0)
    n_small = small.shape[0]
    (small_all,) = all_gather([small], [0], "gather_small_grads")
    small_sum = sum_slots(small_all.reshape(N_DEV, n_small, 128), "sum_small_grads")
    g_even_norm = small_sum[0:8].reshape(1, 1024)
    g_scale = small_sum[8:16].reshape(1, 1024)
    g_ws = small_sum[16:528]
    g_bs = small_sum[528:532]
    g_final = small_sum[536:544].reshape(1, 1024)
    g_odd_norm = lax.dynamic_slice_in_dim(small_sum[544:552], me, 1, axis=0)
    g_conv = lax.dynamic_index_in_dim(small_sum[552:576].reshape(3, 8, 128), me, axis=1, keepdims=False)

    two_d = lambda a, r, c: a.reshape(r, c)
    sharded = [
        ("even_w_in", p_we_in, even_w_in, m_even_w_in, v_even_w_in, (1024, 640)),
        ("even_pool_w", p_wp.reshape(N_DEV, 128, 256), even_pool_w, m_even_pool_w, v_even_pool_w, (128, 256)),
        ("even_w_out", p_we_out, even_w_out, m_even_w_out, v_even_w_out, (256, 1024)),
        ("odd_w_in", p_wo_in, odd_w_in, m_odd_w_in, v_odd_w_in, (1024, 1792)),
        ("odd_w_out", p_wo_out, odd_w_out, m_odd_w_out, v_odd_w_out, (256, 1024)),
    ]
    plain = [
        ("even_norm", g_even_norm, even_norm, m_even_norm, v_even_norm, (1, 1024)),
        ("even_pool_scale", g_scale, even_pool_scale, m_even_pool_scale, v_even_pool_scale, (1, 1024)),
        ("even_ws", g_ws, even_ws, m_even_ws, v_even_ws, (512, 128)),
        ("even_bs", g_bs, even_bs, m_even_bs, v_even_bs, (4, 128)),
        ("odd_norm", g_odd_norm, odd_norm, m_odd_norm, v_odd_norm, (1, 128)),
        ("odd_conv_w", g_conv, odd_conv_w, m_odd_conv_w, v_odd_conv_w, (3, 128)),
        ("final_norm", g_final, final_norm, m_final_norm, v_final_norm, (1, 1024)),
    ]
    res = {}
    for name, g, w, m, v, (r, c) in sharded:
        outs = adamw(g, two_d(w, r, c), two_d(m, r, c), two_d(v, r, c), "adamw_" + name, slots=True)
        res[name] = [o.reshape(w.shape) for o in outs]
    for name, g, w, m, v, (r, c) in plain:
        outs = adamw(two_d(g, r, c), two_d(w, r, c), two_d(m, r, c), two_d(v, r, c), "adamw_" + name, slots=False)
        res[name] = [o.reshape(w.shape) for o in outs]

    loss = lax.psum(loss_blk[0, 0], ("x", "y", "c"))
    order = ["even_norm", "even_w_in", "even_pool_w", "even_pool_scale", "even_ws", "even_bs", "even_w_out",
             "odd_norm", "odd_w_in", "odd_conv_w", "odd_w_out", "final_norm"]
    return (loss, dx0[None], *[res[n][0] for n in order], *[res[n][1] for n in order],
            *[res[n][2] for n in order], *[res[n][3] for n in order])
```

```python
import functools

import jax
import jax.numpy as jnp
from jax import lax
from jax.experimental import pallas as pl
from jax.experimental.pallas import tpu as pltpu

F32 = jnp.float32
BF16 = jnp.bfloat16
MESH_ID = pl.DeviceIdType.MESH

EPS = 1e-6
NEG = -1e30
N_DEV = 8
POOL_SIZES = (2, 4, 8, 16)
DILATIONS = (1, 4, 16)
N_HEADS = 24
HEADS_PER_GROUP = 8
HEAD_DIM = 128
ATTN_BLOCK = 128
CHUNK = 128
CB = 1024
HALO = 16
ODD_COLBLOCKS = 14
ADAM_LR = 0.001
ADAM_B1 = 0.9
ADAM_B2 = 0.999
ADAM_EPS = 1e-08
ADAM_WD = 0.01
ADAM_STEP = 10
VMEM_LIMIT = 52 * 1024 * 1024


def _cp(*sem):
    return pltpu.CompilerParams(dimension_semantics=sem, vmem_limit_bytes=VMEM_LIMIT)


def _dot(a, b):
    return jnp.dot(a, b, preferred_element_type=F32)


def _dot_nt(a, b):
    return lax.dot_general(a, b, (((1,), (1,)), ((), ())), preferred_element_type=F32)


def _dot_tn(a, b):
    return lax.dot_general(a, b, (((0,), (0,)), ((), ())), preferred_element_type=F32)


def _sigmoid(x):
    return 1.0 / (1.0 + jnp.exp(-x))


def _silu_and_grad(x):
    s = _sigmoid(x)
    return x * s, s * (1.0 + x * (1.0 - s))


def _sds(shape, dtype):
    return jax.ShapeDtypeStruct(shape, dtype)


def norm_matmul(x, g, w, name, tn, tm=1024, rows=256):
    t, d = x.shape
    n = w.shape[1]

    def body(x_ref, g_ref, w_ref, z_ref, h_ref):
        @pl.when(pl.program_id(1) == 0)
        def _():
            for c in range(tm // rows):
                rs = slice(c * rows, (c + 1) * rows)
                xv = x_ref[rs, :]
                r = lax.rsqrt(jnp.mean(xv * xv, axis=-1, keepdims=True) + EPS)
                h_ref[rs, :] = ((xv * r) * g_ref[...]).astype(BF16)

        z_ref[...] = _dot(h_ref[...], w_ref[...]).astype(BF16)

    return pl.pallas_call(
        body, name=name, grid=(t // tm, n // tn),
        in_specs=[pl.BlockSpec((tm, d), lambda i, j: (i, 0)),
                  pl.BlockSpec((1, d), lambda i, j: (0, 0)),
                  pl.BlockSpec((d, tn), lambda i, j: (0, j))],
        out_specs=[pl.BlockSpec((tm, tn), lambda i, j: (i, j)),
                   pl.BlockSpec((tm, d), lambda i, j: (i, 0))],
        out_shape=[_sds((t, n), BF16), _sds((t, d), BF16)],
        compiler_params=_cp("parallel", "arbitrary"),
    )(x, g, w)


def matmul_residual(y, w, x, name, tm=1024):
    t, k = y.shape
    d = w.shape[1]

    def body(y_ref, w_ref, x_ref, o_ref):
        o_ref[...] = x_ref[...] + _dot(y_ref[...], w_ref[...])

    return pl.pallas_call(
        body, name=name, grid=(t // tm,),
        in_specs=[pl.BlockSpec((tm, k), lambda i: (i, 0)),
                  pl.BlockSpec((k, d), lambda i: (0, 0)),
                  pl.BlockSpec((tm, d), lambda i: (i, 0))],
        out_specs=pl.BlockSpec((tm, d), lambda i: (i, 0)),
        out_shape=_sds((t, d), F32),
        compiler_params=_cp("parallel"),
    )(y, w, x)


def out_proj_bwd(dx, wt, y, name, tm=512):
    t, d = dx.shape
    k = wt.shape[1]
    steps = t // tm
    half = d // 2

    def body(dx_ref, wt_ref, y_ref, dy_ref, dw_ref, acc):
        i = pl.program_id(0)
        dxb = dx_ref[...].astype(BF16)
        dy_ref[...] = _dot(dxb, wt_ref[...]).astype(BF16)

        @pl.when(i == 0)
        def _():
            acc[...] = jnp.zeros_like(acc)

        for c in range(2):
            acc[:, c * half:(c + 1) * half] += _dot_tn(y_ref[...], dxb[:, c * half:(c + 1) * half])

        @pl.when(i == steps - 1)
        def _():
            dw_ref[...] = acc[...].astype(BF16)

    return pl.pallas_call(
        body, name=name, grid=(steps,),
        in_specs=[pl.BlockSpec((tm, d), lambda i: (i, 0)),
                  pl.BlockSpec((d, k), lambda i: (0, 0)),
                  pl.BlockSpec((tm, k), lambda i: (i, 0))],
        out_specs=[pl.BlockSpec((tm, k), lambda i: (i, 0)),
                   pl.BlockSpec((k, d), lambda i: (0, 0))],
        out_shape=[_sds((t, k), BF16), _sds((k, d), BF16)],
        scratch_shapes=[pltpu.VMEM((k, d), F32)],
        compiler_params=_cp("arbitrary"),
    )(dx, wt, y)


def _source_block(kind, rows):
    if kind[0] == "cols":
        return (rows, CB)
    return (None, rows, CB)


def _source_active(kind, j):
    if kind[0] == "cols":
        _, first, n = kind
        return (j >= first) & (j < first + n), jnp.clip(j - first, 0, n - 1)
    _, group = kind
    return (j < 9) & (j % 3 == group), jnp.clip(j // 3, 0, 2)


def _source_index(kind, row_block, inner):
    if kind[0] == "cols":
        return (row_block, inner)
    return (inner, row_block, 0)


def in_proj_dw(h, sources, n_blocks, name, tk=1024):
    t, d = h.shape
    steps = t // tk
    ns = len(sources)

    def body(*refs):
        h_ref, src_refs, dw_ref, acc = refs[0], refs[1:1 + ns], refs[1 + ns], refs[2 + ns]
        j, k = pl.program_id(0), pl.program_id(1)

        @pl.when(k == 0)
        def _():
            acc[...] = jnp.zeros_like(acc)

        for s, (_, kind) in enumerate(sources):
            active, _ = _source_active(kind, j)

            @pl.when(active)
            def _(s=s):
                acc[...] += _dot_tn(h_ref[...], src_refs[s][...])

        @pl.when(k == steps - 1)
        def _():
            dw_ref[...] = acc[...].astype(BF16)

    def src_spec(kind):
        def index(j, k):
            active, inner = _source_active(kind, j)
            return _source_index(kind, jnp.where(active, k, 0), inner)
        return pl.BlockSpec(_source_block(kind, tk), index)

    return pl.pallas_call(
        body, name=name, grid=(n_blocks, steps),
        in_specs=[pl.BlockSpec((tk, d), lambda j, k: (k, 0))] + [src_spec(kind) for _, kind in sources],
        out_specs=pl.BlockSpec((d, CB), lambda j, k: (0, j)),
        out_shape=_sds((d, n_blocks * CB), BF16),
        scratch_shapes=[pltpu.VMEM((d, CB), F32)],
        compiler_params=_cp("parallel", "arbitrary"),
    )(h, *[a for a, _ in sources])


def in_proj_dx(sources, n_blocks, wt, x, g, dres, name, tm=1024, rows=256, epilogue_buffers=2):
    t, d = x.shape
    ns = len(sources)

    def body(*refs):
        src_refs = refs[:ns]
        w_ref, x_ref, g_ref, dres_ref, dx_ref, dg_ref, acc = refs[ns:]
        i, p = pl.program_id(0), pl.program_id(1)

        @pl.when(p == 0)
        def _():
            acc[...] = jnp.zeros_like(acc)

        for s, (_, kind) in enumerate(sources):
            active, _ = _source_active(kind, p)

            @pl.when(active)
            def _(s=s):
                acc[...] += _dot(src_refs[s][...], w_ref[...])

        @pl.when(p == n_blocks - 1)
        def _():
            part = jnp.zeros((1, d), F32)
            for c in range(tm // rows):
                rs = slice(c * rows, (c + 1) * rows)
                xv = x_ref[rs, :]
                r = lax.rsqrt(jnp.mean(xv * xv, axis=-1, keepdims=True) + EPS)
                xn = xv * r
                dh = acc[rs, :]
                dhg = dh * g_ref[...]
                dx_ref[rs, :] = dres_ref[rs, :] + r * (dhg - xn * jnp.mean(dhg * xn, axis=-1, keepdims=True))
                part = part + jnp.sum(dh * xn, axis=0, keepdims=True)

            @pl.when(i == 0)
            def _():
                dg_ref[...] = part

            @pl.when(i > 0)
            def _():
                dg_ref[...] += part

    def src_spec(kind):
        def index(i, p):
            if kind[0] == "cols":
                _, first, n = kind
                inner = jnp.clip(p - first, 0, n - 1)
            else:
                _, group = kind
                inner = jnp.clip((p - group + 2) // 3, 0, 2)
            return _source_index(kind, i, inner)
        return pl.BlockSpec(_source_block(kind, tm), index)

    return pl.pallas_call(
        body, name=name, grid=(t // tm, n_blocks),
        in_specs=[src_spec(kind) for _, kind in sources] + [
            pl.BlockSpec((CB, d), lambda i, p: (p, 0)),
            pl.BlockSpec((tm, d), lambda i, p: (i, 0), pipeline_mode=pl.Buffered(epilogue_buffers)),
            pl.BlockSpec((1, d), lambda i, p: (0, 0)),
            pl.BlockSpec((tm, d), lambda i, p: (i, 0), pipeline_mode=pl.Buffered(epilogue_buffers))],
        out_specs=[pl.BlockSpec((tm, d), lambda i, p: (i, 0)),
                   pl.BlockSpec((1, d), lambda i, p: (0, 0))],
        out_shape=[_sds((t, d), F32), _sds((1, d), F32)],
        scratch_shapes=[pltpu.VMEM((tm, d), F32)],
        compiler_params=_cp("arbitrary", "arbitrary"),
    )(*[a for a, _ in sources], wt, x, g, dres)


def _window_counts(first_row, rows, w):
    t = first_row + lax.broadcasted_iota(jnp.int32, (rows, 1), 0)
    return jnp.minimum(t + 1, w).astype(F32)


def _tril_bf16(ws_ref, g):
    r = lax.broadcasted_iota(jnp.int32, (CHUNK, CHUNK), 0)
    c = lax.broadcasted_iota(jnp.int32, (CHUNK, CHUNK), 1)
    return jnp.where(r >= c, ws_ref[g], 0.0).astype(BF16), r >= c


def even_mix(z, wp, scale, ws, bs_col, name, tm=256):
    t = z.shape[0]
    gd = CB // len(POOL_SIZES)

    def body(a_ref, ga_ref, u_ref, v_ref, gb_ref, wp_ref, sc_ref, ws_ref, bs_ref, y_ref, pooled_ref, aext):
        i = pl.program_id(0)

        @pl.when(i == 0)
        def _():
            aext[0:HALO, :] = jnp.zeros((HALO, CB), F32)

        @pl.when(i > 0)
        def _():
            aext[0:HALO, :] = aext[tm:tm + HALO, :]

        aext[HALO:HALO + tm, :] = a_ref[...].astype(F32)
        for g, w in enumerate(POOL_SIZES):
            cols = slice(g * gd, (g + 1) * gd)
            tok = aext[HALO:HALO + tm, cols]
            s = tok
            for k in range(1, w):
                s = s + aext[HALO - k:HALO - k + tm, cols]
            pooled = (s / _window_counts(i * tm, tm, w) - tok).astype(BF16)
            pooled_ref[:, cols] = pooled
            mixed = _dot(pooled, wp_ref[g])
            silu_a, _ = _silu_and_grad(ga_ref[:, cols].astype(F32))
            y_ref[:, cols] = (mixed * sc_ref[:, cols] * silu_a).astype(BF16)
        for g in range(4):
            cols = slice(g * gd, (g + 1) * gd)
            wg, _ = _tril_bf16(ws_ref, g)
            for c in range(tm // CHUNK):
                rows = slice(c * CHUNK, (c + 1) * CHUNK)
                m = _dot(wg, v_ref[rows, cols]) + bs_ref[g]
                silu_b, _ = _silu_and_grad(gb_ref[rows, cols].astype(F32))
                y_ref[rows, CB + g * gd:CB + (g + 1) * gd] = (
                    u_ref[rows, cols].astype(F32) * m * silu_b).astype(BF16)

    zspec = lambda cb: pl.BlockSpec((tm, CB), lambda i, cb=cb: (i, cb))
    full = lambda shape: pl.BlockSpec(shape, lambda i: (0,) * len(shape))
    return pl.pallas_call(
        body, name=name, grid=(t // tm,),
        in_specs=[zspec(0), zspec(1), zspec(2), zspec(3), zspec(4),
                  full(wp.shape), full(scale.shape), full(ws.shape), full(bs_col.shape)],
        out_specs=[pl.BlockSpec((tm, 2 * CB), lambda i: (i, 0)), pl.BlockSpec((tm, CB), lambda i: (i, 0))],
        out_shape=[_sds((t, 2 * CB), BF16), _sds((t, CB), BF16)],
        scratch_shapes=[pltpu.VMEM((HALO + tm, CB), F32)],
        compiler_params=_cp("arbitrary"),
    )(z, z, z, z, z, wp, scale, ws, bs_col)


def even_mix_bwd(dy, z, pooled, wp, scale, ws, bs_col, name, tm=256):
    t = z.shape[0]
    nt = t // tm
    gd = CB // len(POOL_SIZES)

    def body(dya_ref, dyb_ref, ga_ref, u_ref, v_ref, gb_ref, pooled_ref, wp_ref, sc_ref, ws_ref, bs_ref,
             dz_ref, dwp_ref, dsc_ref, dws_ref, dbs_ref, dpext):
        i = pl.program_id(0)
        tile = nt - 1 - i

        @pl.when(i == 0)
        def _():
            dpext[tm:tm + HALO, :] = jnp.zeros((HALO, CB), F32)
            dwp_ref[...] = jnp.zeros_like(dwp_ref)
            dsc_ref[...] = jnp.zeros_like(dsc_ref)
            dws_ref[...] = jnp.zeros_like(dws_ref)
            dbs_ref[...] = jnp.zeros_like(dbs_ref)

        @pl.when(i > 0)
        def _():
            dpext[tm:tm + HALO, :] = dpext[0:HALO, :]

        for g, w in enumerate(POOL_SIZES):
            cols = slice(g * gd, (g + 1) * gd)
            pooled_g = pooled_ref[:, cols]
            mixed = _dot(pooled_g, wp_ref[g])
            silu_a, dsilu_a = _silu_and_grad(ga_ref[:, cols].astype(F32))
            dya = dya_ref[:, cols].astype(F32)
            sc = sc_ref[:, cols]
            dmixed = (dya * sc * silu_a).astype(BF16)
            dsc_ref[:, cols] += jnp.sum(dya * mixed * silu_a, axis=0, keepdims=True)
            dz_ref[:, CB + g * gd:CB + (g + 1) * gd] = (dya * mixed * sc * dsilu_a).astype(BF16)
            dwp_ref[g] += _dot_tn(pooled_g, dmixed)
            dpooled = _dot_nt(dmixed, wp_ref[g])
            dpext[0:tm, cols] = dpooled / _window_counts(tile * tm, tm, w)
            s = dpext[0:tm, cols]
            for k in range(1, w):
                s = s + dpext[k:k + tm, cols]
            dz_ref[:, cols] = (s - dpooled).astype(BF16)
        for g in range(4):
            cols = slice(g * gd, (g + 1) * gd)
            wg, lower = _tril_bf16(ws_ref, g)
            dws_g = jnp.zeros((CHUNK, CHUNK), F32)
            dbs_g = jnp.zeros((CHUNK, 1), F32)
            for c in range(tm // CHUNK):
                rows = slice(c * CHUNK, (c + 1) * CHUNK)
                vb = v_ref[rows, cols]
                m = _dot(wg, vb) + bs_ref[g]
                gbv = gb_ref[rows, cols].astype(F32)
                silu_b, dsilu_b = _silu_and_grad(gbv)
                dyb = dyb_ref[rows, cols].astype(F32)
                uv = u_ref[rows, cols].astype(F32)
                dm = dyb * silu_b * uv
                dmb = dm.astype(BF16)
                dz_ref[rows, 2 * CB + g * gd:2 * CB + (g + 1) * gd] = (dyb * silu_b * m).astype(BF16)
                dz_ref[rows, 3 * CB + g * gd:3 * CB + (g + 1) * gd] = _dot_tn(wg, dmb).astype(BF16)
                dz_ref[rows, 4 * CB + g * gd:4 * CB + (g + 1) * gd] = (dyb * uv * m * dsilu_b).astype(BF16)
                dws_g = dws_g + _dot_nt(dmb, vb)
                dbs_g = dbs_g + jnp.sum(dm, axis=1, keepdims=True)
            dws_ref[g] += jnp.where(lower, dws_g, 0.0)
            dbs_ref[g] += dbs_g

    rev = lambda cb: pl.BlockSpec((tm, CB), lambda i, cb=cb: (nt - 1 - i, cb))
    full = lambda shape: pl.BlockSpec(shape, lambda i: (0,) * len(shape))
    return pl.pallas_call(
        body, name=name, grid=(nt,),
        in_specs=[rev(0), rev(1), rev(1), rev(2), rev(3), rev(4), rev(0),
                  full(wp.shape), full(scale.shape), full(ws.shape), full(bs_col.shape)],
        out_specs=[pl.BlockSpec((tm, 5 * CB), lambda i: (nt - 1 - i, 0)),
                   full(wp.shape), full(scale.shape), full(ws.shape), full(bs_col.shape)],
        out_shape=[_sds((t, 5 * CB), BF16), _sds(wp.shape, F32), _sds(scale.shape, F32),
                   _sds(ws.shape, F32), _sds(bs_col.shape, F32)],
        scratch_shapes=[pltpu.VMEM((tm + HALO, CB), F32)],
        compiler_params=_cp("arbitrary"),
    )(dy, dy, z, z, z, z, pooled, wp, scale, ws, bs_col)


def _slope(group, head):
    return float(2.0 ** (-8.0 * (group * HEADS_PER_GROUP + head + 1) / N_HEADS))


def _band(dilation):
    qi = lax.broadcasted_iota(jnp.int32, (ATTN_BLOCK, ATTN_BLOCK), 0)
    ki = lax.broadcasted_iota(jnp.int32, (ATTN_BLOCK, ATTN_BLOCK), 1)
    dist_prev = ((qi + ATTN_BLOCK - ki) * dilation).astype(F32)
    dist_cur = ((qi - ki) * dilation).astype(F32)
    return dist_prev, ki >= qi, dist_cur, ki <= qi


def _permute_rows(name, d, inverse, arrays, in_specs, out_shapes, out_specs, n_chunks, widths):
    rows = ATTN_BLOCK * d
    t = out_shapes[0].shape[-2]
    n = len(arrays)

    lanes = HEAD_DIM

    def body(*refs):
        for a, o, s, w in zip(refs[:n], refs[n:2 * n], refs[2 * n:], widths):
            for k in range(w // lanes):
                cols = slice(k * lanes, (k + 1) * lanes)
                if inverse:
                    for r in range(d):
                        s[k, pl.ds(r, ATTN_BLOCK, stride=d), :] = (
                            a[r * ATTN_BLOCK:(r + 1) * ATTN_BLOCK, cols].astype(F32))
                    o[:, cols] = s[k].astype(o.dtype)
                else:
                    s[k] = a[:, cols].astype(F32)
                    for r in range(d):
                        o[r * ATTN_BLOCK:(r + 1) * ATTN_BLOCK, cols] = (
                            s[k, pl.ds(r, ATTN_BLOCK, stride=d), :].astype(o.dtype))

    return pl.pallas_call(
        body, name=name, grid=(t // rows, n_chunks),
        in_specs=in_specs, out_specs=out_specs, out_shape=out_shapes,
        scratch_shapes=[pltpu.VMEM((w // lanes, rows, lanes), F32) for w in widths],
        compiler_params=_cp("parallel", "arbitrary"),
    )(*arrays)


def _chunk_width(d):
    return 4 * CB // d


def permute_qkv(z, group, name):
    d = DILATIONS[group]
    rows, cw = ATTN_BLOCK * d, _chunk_width(d)
    per = CB // cw
    src = pl.BlockSpec((rows, cw), lambda i, c: (i, (c // per * 3 + group) * per + c % per))
    dst = pl.BlockSpec((rows, cw), lambda i, c: (i, c))
    (out,) = _permute_rows(name, d, False, [z], [src], [_sds((z.shape[0], 3 * CB), z.dtype)], [dst], 3 * per, [cw])
    return out


def permute_mats(wide, narrow, group, inverse, name):
    d = DILATIONS[group]
    rows, cw = ATTN_BLOCK * d, _chunk_width(d)
    wide_spec = pl.BlockSpec((rows, cw), lambda i, c: (i, c))
    narrow_spec = pl.BlockSpec((rows, HEAD_DIM), lambda i, c: (i, 0))
    arrays = [wide] + list(narrow)
    specs = [wide_spec] + [narrow_spec] * len(narrow)
    return _permute_rows(name, d, inverse, arrays, specs, [_sds(a.shape, a.dtype) for a in arrays], specs,
                         CB // cw, [cw] + [HEAD_DIM] * len(narrow))


def permute_slabs(s, group, inverse, name):
    d = DILATIONS[group]
    rows, cw = ATTN_BLOCK * d, _chunk_width(d)
    per = CB // cw
    spec = pl.BlockSpec((None, rows, cw), lambda i, c: (c // per, i, c % per))
    (out,) = _permute_rows(name, d, inverse, [s], [spec], [_sds(s.shape, s.dtype)], [spec], 3 * per, [cw])
    return out


def _lane_pack(cols):
    rows = cols[0].shape[0]
    lane = lax.broadcasted_iota(jnp.int32, (rows, HEAD_DIM), 1)
    out = jnp.zeros((rows, HEAD_DIM), F32)
    for h, c in enumerate(cols):
        out = jnp.where(lane == h, c, out)
    return out


def _qkv_col(group, from_z):
    return (lambda which: which * 3 + group) if from_z else (lambda which: which)


def _scores(q, kp, kc, group, h, dist_p, ok_p, dist_c, ok_c):
    scale = HEAD_DIM ** -0.5
    slope = _slope(group, h)
    sp = jnp.where(ok_p, _dot_nt(q, kp) * scale - slope * dist_p, NEG)
    sc = jnp.where(ok_c, _dot_nt(q, kc) * scale - slope * dist_c, NEG)
    return sp, sc


def attn_fwd(qkv, group, name):
    t = qkv.shape[0]
    d = DILATIONS[group]
    col = _qkv_col(group, qkv.shape[1] != 3 * CB)
    heads = range(HEADS_PER_GROUP)

    def body(q_ref, kp_ref, kc_ref, vp_ref, vc_ref, o_ref, lse_ref):
        b = pl.program_id(0)
        dist_p, ok_p, dist_c, ok_c = _band(d)
        ok_p = jnp.logical_and(ok_p, b >= d)
        hs = [slice(h * HEAD_DIM, (h + 1) * HEAD_DIM) for h in heads]
        s = [_scores(q_ref[:, hs[h]], kp_ref[:, hs[h]], kc_ref[:, hs[h]], group, h, dist_p, ok_p, dist_c, ok_c)
             for h in heads]
        m = [jnp.maximum(jnp.max(sp, axis=1, keepdims=True), jnp.max(sc, axis=1, keepdims=True)) for sp, sc in s]
        e = [(jnp.exp(sp - m[h]), jnp.exp(sc - m[h])) for h, (sp, sc) in enumerate(s)]
        l = [jnp.sum(ep, axis=1, keepdims=True) + jnp.sum(ec, axis=1, keepdims=True) for ep, ec in e]
        inv = [1.0 / lh for lh in l]
        o = [_dot((ep * inv[h]).astype(BF16), vp_ref[:, hs[h]]) + _dot((ec * inv[h]).astype(BF16), vc_ref[:, hs[h]])
             for h, (ep, ec) in enumerate(e)]
        for h in heads:
            o_ref[:, hs[h]] = o[h].astype(BF16)
        lse_ref[...] = _lane_pack([m[h] + jnp.log(l[h]) for h in heads])

    blk = (ATTN_BLOCK, CB)
    cur = lambda which: pl.BlockSpec(blk, lambda b: (b, col(which)))
    prev = lambda which: pl.BlockSpec(blk, lambda b: (jnp.maximum(b - d, 0), col(which)))
    return pl.pallas_call(
        body, name=name, grid=(t // ATTN_BLOCK,),
        in_specs=[cur(0), prev(1), cur(1), prev(2), cur(2)],
        out_specs=[pl.BlockSpec(blk, lambda b: (b, 0)), pl.BlockSpec((ATTN_BLOCK, HEAD_DIM), lambda b: (b, 0))],
        out_shape=[_sds((t, CB), BF16), _sds((t, HEAD_DIM), F32)],
        compiler_params=_cp("parallel"),
    )(qkv, qkv, qkv, qkv, qkv)


def attn_bwd(z, dyc, lse, dd, group, name):
    t = z.shape[0]
    d = DILATIONS[group]
    nb = t // (ATTN_BLOCK * d)
    col = _qkv_col(group, z.shape[1] != 3 * CB)
    scale = HEAD_DIM ** -0.5
    heads = range(HEADS_PER_GROUP)

    def body(q_ref, kp_ref, kc_ref, vp_ref, vc_ref, dy_ref, lse_ref, dd_ref, out_ref, carry):
        n = pl.program_id(1)

        @pl.when(n == 0)
        def _():
            carry[...] = jnp.zeros_like(carry)

        @pl.when(n == nb)
        def _():
            out_ref[...] = carry[...].astype(BF16)

        @pl.when(n < nb)
        def _():
            dist_p, ok_p, dist_c, ok_c = _band(d)
            ok_p = jnp.logical_and(ok_p, n > 0)
            hs = [slice(h * HEAD_DIM, (h + 1) * HEAD_DIM) for h in heads]
            q = [q_ref[:, hs[h]] for h in heads]
            kp = [kp_ref[:, hs[h]] for h in heads]
            kc = [kc_ref[:, hs[h]] for h in heads]
            dy = [dy_ref[:, hs[h]] for h in heads]
            old = [[carry[w, :, hs[h]] for h in heads] for w in range(3)]
            s = [_scores(q[h], kp[h], kc[h], group, h, dist_p, ok_p, dist_c, ok_c) for h in heads]
            p = [(jnp.exp(sp - lse_ref[:, h:h + 1]), jnp.exp(sc - lse_ref[:, h:h + 1])) for h, (sp, sc) in enumerate(s)]
            ds = [((pp * (_dot_nt(dy[h], vp_ref[:, hs[h]]) - dd_ref[:, h:h + 1]) * scale).astype(BF16),
                   (pc * (_dot_nt(dy[h], vc_ref[:, hs[h]]) - dd_ref[:, h:h + 1]) * scale).astype(BF16))
                  for h, (pp, pc) in enumerate(p)]
            for h in heads:
                dsp, dsc = ds[h]
                pp, pc = p[h]
                out_ref[0, :, hs[h]] = old[0][h].astype(BF16)
                out_ref[1, :, hs[h]] = (old[1][h] + _dot_tn(dsp, q[h])).astype(BF16)
                out_ref[2, :, hs[h]] = (old[2][h] + _dot_tn(pp.astype(BF16), dy[h])).astype(BF16)
                carry[0, :, hs[h]] = _dot(dsp, kp[h]) + _dot(dsc, kc[h])
                carry[1, :, hs[h]] = _dot_tn(dsc, q[h])
                carry[2, :, hs[h]] = _dot_tn(pc.astype(BF16), dy[h])

    blk = (ATTN_BLOCK, CB)
    cur_row = lambda r, n: jnp.minimum(n, nb - 1) * d + r
    prev_row = lambda r, n: jnp.clip(n - 1, 0, nb - 1) * d + r
    cur = lambda which: pl.BlockSpec(blk, lambda r, n: (cur_row(r, n), col(which)))
    prev = lambda which: pl.BlockSpec(blk, lambda r, n: (prev_row(r, n), col(which)))
    narrow = pl.BlockSpec((ATTN_BLOCK, HEAD_DIM), lambda r, n: (cur_row(r, n), 0))
    return pl.pallas_call(
        body, name=name, grid=(d, nb + 1),
        in_specs=[cur(0), prev(1), cur(1), prev(2), cur(2),
                  pl.BlockSpec(blk, lambda r, n: (cur_row(r, n), 0)), narrow, narrow],
        out_specs=pl.BlockSpec((3, ATTN_BLOCK, CB), lambda r, n: (0, prev_row(r, n), 0)),
        out_shape=_sds((3, t, CB), BF16),
        scratch_shapes=[pltpu.VMEM((3, ATTN_BLOCK, CB), F32)],
        compiler_params=_cp("parallel", "arbitrary"),
    )(z, z, z, z, z, dyc, lse, dd)


def odd_mix(os_, lses, z, conv_w, name, tm=256):
    t = z.shape[0]

    def body(o0, o1, o2, l0, l1, l2, gc_ref, db_ref, dc_ref, dx_ref, gd_ref, cw_ref,
             y_ref, yc_ref, lse_ref, conv_ref, zext):
        i = pl.program_id(0)
        a0, a1, a2 = l0[...], l1[...], l2[...]
        m = jnp.maximum(jnp.maximum(a0, a1), a2)
        tot = m + jnp.log(jnp.exp(a0 - m) + jnp.exp(a1 - m) + jnp.exp(a2 - m))
        lse_ref[...] = tot
        w0, w1, w2 = jnp.exp(a0 - tot), jnp.exp(a1 - tot), jnp.exp(a2 - tot)
        for h in range(HEADS_PER_GROUP):
            hs = slice(h * HEAD_DIM, (h + 1) * HEAD_DIM)
            yc = (w0[:, h:h + 1] * o0[:, hs].astype(F32) + w1[:, h:h + 1] * o1[:, hs].astype(F32)
                  + w2[:, h:h + 1] * o2[:, hs].astype(F32))
            yc_ref[:, hs] = yc.astype(BF16)
            silu_c, _ = _silu_and_grad(gc_ref[:, hs].astype(F32))
            y_ref[:, hs] = (yc * silu_c).astype(BF16)

        @pl.when(i == 0)
        def _():
            zext[0:HALO, :] = jnp.zeros((HALO, CB), F32)

        @pl.when(i > 0)
        def _():
            zext[0:HALO, :] = zext[tm:tm + HALO, :]

        zext[HALO:HALO + tm, :] = dc_ref[...].astype(F32) * dx_ref[...].astype(F32)
        conv = (cw_ref[0:1, :] * zext[HALO - 2:HALO - 2 + tm, :] + cw_ref[1:2, :] * zext[HALO - 1:HALO - 1 + tm, :]
                + cw_ref[2:3, :] * zext[HALO:HALO + tm, :])
        conv_ref[...] = conv.astype(BF16)
        silu_d, _ = _silu_and_grad(gd_ref[...].astype(F32))
        y_ref[:, CB:2 * CB] = (db_ref[...].astype(F32) * conv * silu_d).astype(BF16)

    row = pl.BlockSpec((tm, CB), lambda i: (i, 0))
    narrow = pl.BlockSpec((tm, HEAD_DIM), lambda i: (i, 0))
    zspec = lambda cb: pl.BlockSpec((tm, CB), lambda i, cb=cb: (i, cb))
    return pl.pallas_call(
        body, name=name, grid=(t // tm,),
        in_specs=[row] * 3 + [narrow] * 3 + [zspec(9), zspec(10), zspec(11), zspec(12), zspec(13),
                                             pl.BlockSpec(conv_w.shape, lambda i: (0, 0))],
        out_specs=[pl.BlockSpec((tm, 2 * CB), lambda i: (i, 0)), row, narrow, row],
        out_shape=[_sds((t, 2 * CB), BF16), _sds((t, CB), BF16), _sds((t, HEAD_DIM), F32), _sds((t, CB), BF16)],
        scratch_shapes=[pltpu.VMEM((HALO + tm, CB), F32)],
        compiler_params=_cp("arbitrary"),
    )(*os_, *lses, z, z, z, z, z, conv_w)


def odd_mix_bwd(dy, yc, conv, z, conv_w, name, tm=256):
    t = z.shape[0]
    nt = t // tm

    def body(dyc_ref, dyd_ref, yc_ref, conv_ref, gc_ref, db_ref, dc_ref, dx_ref, gd_ref, cw_ref,
             dz_ref, dyo_ref, dd_ref, dcw_ref, dcext):
        i = pl.program_id(0)

        @pl.when(i == 0)
        def _():
            dcext[tm:tm + HALO, :] = jnp.zeros((HALO, CB), F32)
            dcw_ref[...] = jnp.zeros_like(dcw_ref)

        @pl.when(i > 0)
        def _():
            dcext[tm:tm + HALO, :] = dcext[0:HALO, :]

        silu_c, dsilu_c = _silu_and_grad(gc_ref[...].astype(F32))
        dyc = dyc_ref[...].astype(F32)
        ycv = yc_ref[...].astype(F32)
        dyo = dyc * silu_c
        dyo_ref[...] = dyo.astype(BF16)
        dz_ref[:, 0:CB] = (dyc * ycv * dsilu_c).astype(BF16)
        prod = dyo * ycv
        dd_ref[...] = _lane_pack([jnp.sum(prod[:, h * HEAD_DIM:(h + 1) * HEAD_DIM], axis=1, keepdims=True)
                                  for h in range(HEADS_PER_GROUP)])

        silu_d, dsilu_d = _silu_and_grad(gd_ref[...].astype(F32))
        dyd = dyd_ref[...].astype(F32)
        convv = conv_ref[...].astype(F32)
        dbv = db_ref[...].astype(F32)
        dz_ref[:, CB:2 * CB] = (dyd * convv * silu_d).astype(BF16)
        dz_ref[:, 4 * CB:5 * CB] = (dyd * dbv * convv * dsilu_d).astype(BF16)
        dcext[0:tm, :] = dyd * dbv * silu_d
        dcv, dxv = dc_ref[...].astype(F32), dx_ref[...].astype(F32)
        zc = dcv * dxv
        d0, d1, d2 = dcext[0:tm, :], dcext[1:1 + tm, :], dcext[2:2 + tm, :]
        dzc = cw_ref[2:3, :] * d0 + cw_ref[1:2, :] * d1 + cw_ref[0:1, :] * d2
        dz_ref[:, 2 * CB:3 * CB] = (dzc * dxv).astype(BF16)
        dz_ref[:, 3 * CB:4 * CB] = (dzc * dcv).astype(BF16)
        dcw_ref[0:1, :] += jnp.sum(zc * d2, axis=0, keepdims=True)
        dcw_ref[1:2, :] += jnp.sum(zc * d1, axis=0, keepdims=True)
        dcw_ref[2:3, :] += jnp.sum(zc * d0, axis=0, keepdims=True)

    rev = lambda cb: pl.BlockSpec((tm, CB), lambda i, cb=cb: (nt - 1 - i, cb))
    return pl.pallas_call(
        body, name=name, grid=(nt,),
        in_specs=[rev(0), rev(1), rev(0), rev(0), rev(9), rev(10), rev(11), rev(12), rev(13),
                  pl.BlockSpec(conv_w.shape, lambda i: (0, 0))],
        out_specs=[pl.BlockSpec((tm, 5 * CB), lambda i: (nt - 1 - i, 0)), rev(0),
                   pl.BlockSpec((tm, HEAD_DIM), lambda i: (nt - 1 - i, 0)),
                   pl.BlockSpec(conv_w.shape, lambda i: (0, 0))],
        out_shape=[_sds((t, 5 * CB), BF16), _sds((t, CB), BF16), _sds((t, HEAD_DIM), F32), _sds(conv_w.shape, F32)],
        scratch_shapes=[pltpu.VMEM((tm + HALO, CB), F32)],
        compiler_params=_cp("arbitrary"),
    )(dy, dy, yc, conv, z, z, z, z, z, conv_w)


def final_norm_loss(x, g, target, name, tm=512):
    t, d = x.shape

    def body(x_ref, g_ref, t_ref, dx_ref, loss_ref, dg_ref):
        i = pl.program_id(0)
        xv = x_ref[...]
        r = lax.rsqrt(jnp.mean(xv * xv, axis=-1, keepdims=True) + EPS)
        xn = xv * r
        gv = g_ref[...]
        err = xn * gv - t_ref[...]
        loss = 0.5 * jnp.sum(jnp.mean(err * err, axis=-1, keepdims=True), axis=0, keepdims=True)
        dy = err * (1.0 / d)
        dyg = dy * gv
        dx_ref[...] = r * (dyg - xn * jnp.mean(dyg * xn, axis=-1, keepdims=True))
        dg = jnp.sum(dy * xn, axis=0, keepdims=True)

        @pl.when(i == 0)
        def _():
            loss_ref[...] = jnp.broadcast_to(loss, loss_ref.shape)
            dg_ref[...] = dg

        @pl.when(i > 0)
        def _():
            loss_ref[...] += jnp.broadcast_to(loss, loss_ref.shape)
            dg_ref[...] += dg

    return pl.pallas_call(
        body, name=name, grid=(t // tm,),
        in_specs=[pl.BlockSpec((tm, d), lambda i: (i, 0)), pl.BlockSpec((1, d), lambda i: (0, 0)),
                  pl.BlockSpec((tm, d), lambda i: (i, 0))],
        out_specs=[pl.BlockSpec((tm, d), lambda i: (i, 0)), pl.BlockSpec((8, 128), lambda i: (0, 0)),
                   pl.BlockSpec((1, d), lambda i: (0, 0))],
        out_shape=[_sds((t, d), F32), _sds((8, 128), F32), _sds((1, d), F32)],
        compiler_params=_cp("arbitrary"),
    )(x, g, target)


def _position():
    x, y, c = lax.axis_index("x"), lax.axis_index("y"), lax.axis_index("c")
    return x, y, c, 4 * x + 2 * y + c


def _peer(x, y, c, k):
    px = 1 - x if k & 4 else x
    py = 1 - y if k & 2 else y
    pc = 1 - c if k & 1 else c
    return (px, py, pc), 4 * px + 2 * py + pc


def _block(ref, axis, size, idx):
    index = [slice(None)] * len(ref.shape)
    index[axis] = pl.ds(idx * size, size)
    return ref.at[tuple(index)]


def all_gather(shards, axes, name):
    n = len(shards)
    fulls = []
    for s, ax in zip(shards, axes):
        shape = list(s.shape)
        shape[ax] *= N_DEV
        fulls.append(_sds(tuple(shape), s.dtype))

    def body(*refs):
        ins, outs = refs[:n], refs[n:2 * n]
        send_sems, recv_sems, local_sems = refs[2 * n:]
        x, y, c, me = _position()
        local = []
        for p in range(n):
            size = ins[p].shape[axes[p]]
            cp = pltpu.make_async_copy(ins[p], _block(outs[p], axes[p], size, me), local_sems.at[p])
            cp.start()
            local.append(cp)
        for k in range(1, N_DEV):
            peer, _ = _peer(x, y, c, k)
            for p in range(n):
                size = ins[p].shape[axes[p]]
                pltpu.make_async_remote_copy(
                    src_ref=ins[p], dst_ref=_block(outs[p], axes[p], size, me),
                    send_sem=send_sems.at[p], recv_sem=recv_sems.at[p],
                    device_id=peer, device_id_type=MESH_ID).start()
        for p in range(n):
            size = ins[p].shape[axes[p]]
            seven = _block(outs[p], axes[p], (N_DEV - 1) * size, 0)
            pltpu.make_async_remote_copy(
                src_ref=seven, dst_ref=seven, send_sem=send_sems.at[p], recv_sem=recv_sems.at[p],
                device_id=(x, y, c), device_id_type=MESH_ID).wait()
            local[p].wait()

    any_spec = pl.BlockSpec(memory_space=pl.ANY)
    return pl.pallas_call(
        body, name=name,
        in_specs=[any_spec] * n, out_specs=[any_spec] * n, out_shape=fulls,
        scratch_shapes=[pltpu.SemaphoreType.DMA((n,)), pltpu.SemaphoreType.DMA((n,)), pltpu.SemaphoreType.DMA((n,))],
    )(*shards)


def _other_chips(x, y):
    return [(1 - x, y), (x, 1 - y), (1 - x, 1 - y)]


def _gather_copy(ins, lands, axes, send_sems, recv_sems, k, p, block, to, from_shard=False):
    dst = _block(lands[p], axes[p], lands[p].shape[axes[p]] // N_DEV, block)
    sem = k * len(lands) + p
    return pltpu.make_async_remote_copy(
        src_ref=ins[p] if from_shard else dst, dst_ref=dst, send_sem=send_sems.at[sem], recv_sem=recv_sems.at[sem],
        device_id=to, device_id_type=MESH_ID)


def all_gather_two_level(shards, axes, name):
    n = len(shards)
    fulls = []
    for s, ax in zip(shards, axes):
        shape = list(s.shape)
        shape[ax] *= N_DEV
        fulls.append(_sds(tuple(shape), s.dtype))

    def body(*refs):
        ins, outs = refs[:n], refs[n:2 * n]
        send_sems, recv_sems, local_sems = refs[2 * n:]
        x, y, c, me = _position()
        sibling, here = (x, y, 1 - c), (x, y, c)
        chips = _other_chips(x, y)
        copy = functools.partial(_gather_copy, ins, outs, axes, send_sems, recv_sems)
        local = [pltpu.make_async_copy(ins[p], _block(outs[p], axes[p], ins[p].shape[axes[p]], me), local_sems.at[p])
                 for p in range(n)]
        sent = []
        for p in range(n):
            sent.append(copy(0, p, me, sibling, from_shard=True))
            sent += [copy(1 + j, p, me, (*chip, c), from_shard=True) for j, chip in enumerate(chips)]
        for cp in local + sent:
            cp.start()
        for j, (px, py) in enumerate(chips):
            for p in range(n):
                arrived = 4 * px + 2 * py + c
                copy(1 + j, p, arrived, here).wait_recv()
                sent.append(copy(4 + j, p, arrived, sibling))
                sent[-1].start()
        for p in range(n):
            copy(0, p, 4 * x + 2 * y + 1 - c, here).wait_recv()
            for j, (px, py) in enumerate(chips):
                copy(4 + j, p, 4 * px + 2 * py + 1 - c, here).wait_recv()
        for cp in sent:
            cp.wait_send()
        for cp in local:
            cp.wait()

    any_spec = pl.BlockSpec(memory_space=pl.ANY)
    return pl.pallas_call(
        body, name=name,
        in_specs=[any_spec] * n, out_specs=[any_spec] * n, out_shape=fulls,
        scratch_shapes=[pltpu.SemaphoreType.DMA((7 * n,)), pltpu.SemaphoreType.DMA((7 * n,)), pltpu.SemaphoreType.DMA((n,))],
    )(*shards)


def gather_two_level_start(shards, axes, after, name):
    n = len(shards)
    lands = [_landing("gather", s, ax) for s, ax in zip(shards, axes)]

    def body(*refs):
        ins, land = refs[:n], refs[n:2 * n]
        send_sems, recv_sems, token = refs[2 * n + 1], refs[2 * n + 2], refs[-1]
        x, y, c, me = _position()
        copy = functools.partial(_gather_copy, ins, land, axes, send_sems, recv_sems)
        for p in range(n):
            copy(0, p, me, (x, y, 1 - c), from_shard=True).start()
            for j, chip in enumerate(_other_chips(x, y)):
                copy(1 + j, p, me, (*chip, c), from_shard=True).start()
        token[...] = jnp.zeros_like(token)

    outs = pl.pallas_call(
        body, name=name,
        out_shape=(pltpu.SemaphoreType.DMA((4 * n,)), pltpu.SemaphoreType.DMA((4 * n,)),
                   *[pltpu.HBM(s.shape, s.dtype) for s in shards], *[pltpu.HBM(l.shape, l.dtype) for l in lands],
                   _sds((8, 128), F32)),
        in_specs=[_HBM] * (2 * n) + [pl.BlockSpec(memory_space=pl.ANY)],
        out_specs=(_SEM, _SEM, *[_HBM] * (2 * n), pl.BlockSpec(memory_space=pltpu.VMEM)),
        input_output_aliases={i: 2 + i for i in range(2 * n)},
        compiler_params=pltpu.CompilerParams(has_side_effects=_EFFECT),
    )(*[_in_hbm(s) for s in shards], *[_in_hbm(l) for l in lands], after)
    return outs[0], outs[1], list(outs[2:2 + n]), list(outs[2 + n:2 + 2 * n]), outs[-1]


def gather_two_level_forward(started, axes, after, name):
    send_a, recv_a, shards, lands, _ = started
    n = len(shards)

    def body(*refs):
        land = refs[:n]
        send_first, recv_first = refs[n], refs[n + 1]
        send_fwd, recv_fwd, token = refs[n + 3], refs[n + 4], refs[-1]
        x, y, c, _ = _position()
        for j, (px, py) in enumerate(_other_chips(x, y)):
            for p in range(n):
                arrived = 4 * px + 2 * py + c
                _gather_copy(None, land, axes, send_first, recv_first, 1 + j, p, arrived, (x, y, c)).wait_recv()
                _gather_copy(None, land, axes, send_fwd, recv_fwd, j, p, arrived, (x, y, 1 - c)).start()
        token[...] = jnp.zeros_like(token)

    outs = pl.pallas_call(
        body, name=name,
        out_shape=(pltpu.SemaphoreType.DMA((3 * n,)), pltpu.SemaphoreType.DMA((3 * n,)),
                   *[pltpu.HBM(l.shape, l.dtype) for l in lands], _sds((8, 128), F32)),
        in_specs=[_HBM] * n + [_SEM, _SEM, pl.BlockSpec(memory_space=pl.ANY)],
        out_specs=(_SEM, _SEM, *[_HBM] * n, pl.BlockSpec(memory_space=pltpu.VMEM)),
        input_output_aliases={i: 2 + i for i in range(n)},
        compiler_params=pltpu.CompilerParams(has_side_effects=_EFFECT),
    )(*lands, send_a, recv_a, after)
    return send_a, recv_a, outs[0], outs[1], shards, list(outs[2:2 + n]), outs[-1]


def gather_two_level_wait(forwarded, axes, after, name):
    send_a, recv_a, send_f, recv_f, shards, lands, _ = forwarded
    n = len(shards)

    def body(*refs):
        ins, land = refs[:n], refs[n:2 * n]
        sa, ra, sf, rf = refs[2 * n:2 * n + 4]
        x, y, c, me = _position()
        here = (x, y, c)
        chips = _other_chips(x, y)
        for p in range(n):
            for k in range(4):
                _gather_copy(ins, land, axes, sa, ra, k, p, me, here, from_shard=True).wait_send()
            _gather_copy(ins, land, axes, sa, ra, 0, p, 4 * x + 2 * y + 1 - c, here).wait_recv()
            for j, (px, py) in enumerate(chips):
                _gather_copy(ins, land, axes, sf, rf, j, p, 4 * px + 2 * py + c, here).wait_send()
                _gather_copy(ins, land, axes, sf, rf, j, p, 4 * px + 2 * py + 1 - c, here).wait_recv()

    outs = pl.pallas_call(
        body, name=name,
        out_shape=(*[pltpu.HBM(s.shape, s.dtype) for s in shards], *[pltpu.HBM(l.shape, l.dtype) for l in lands]),
        in_specs=[_HBM] * (2 * n) + [_SEM] * 4 + [pl.BlockSpec(memory_space=pl.ANY)],
        out_specs=tuple([_HBM] * (2 * n)),
        input_output_aliases={i: i for i in range(2 * n)},
        compiler_params=pltpu.CompilerParams(has_side_effects=_EFFECT),
    )(*shards, *lands, send_a, recv_a, send_f, recv_f, after)
    return list(outs[n:])


_HBM = pl.BlockSpec(memory_space=pltpu.HBM)
_SEM = pl.BlockSpec(memory_space=pltpu.SEMAPHORE)
_EFFECT = pltpu.SideEffectType.DATAFLOW_SIDE_EFFECTING


def _in_hbm(a):
    return pltpu.with_memory_space_constraint(a, pltpu.HBM)


def _landing(mode, src, axis):
    me = 4 * lax.axis_index("x") + 2 * lax.axis_index("y") + lax.axis_index("c")
    if mode == "gather":
        shape = list(src.shape)
        size = shape[axis]
        shape[axis] *= N_DEV
        return lax.dynamic_update_slice_in_dim(lax.empty(tuple(shape), src.dtype), src, me * size, axis)
    size = src.shape[axis] // N_DEV
    own = lax.dynamic_slice_in_dim(src, me * size, size, axis)
    return lax.dynamic_update_slice_in_dim(lax.empty((N_DEV, *own.shape), src.dtype), own[None], me, 0)


def _seven(mode, land_ref, axis):
    if mode == "gather":
        return _block(land_ref, axis, (N_DEV - 1) * (land_ref.shape[axis] // N_DEV), 0)
    return land_ref.at[pl.ds(0, N_DEV - 1)]


def exchange_start(mode, srcs, axes, after, name):
    n = len(srcs)
    lands = [_landing(mode, s, ax) for s, ax in zip(srcs, axes)]

    def body(*refs):
        src, land = refs[:n], refs[n:2 * n]
        send_sems, recv_sems = refs[2 * n + 1], refs[2 * n + 2]
        token = refs[-1]
        x, y, c, me = _position()
        for k in range(1, N_DEV):
            peer, pj = _peer(x, y, c, k)
            for p in range(n):
                if mode == "gather":
                    s = src[p]
                    dst = _block(land[p], axes[p], src[p].shape[axes[p]], me)
                else:
                    s = _block(src[p], axes[p], src[p].shape[axes[p]] // N_DEV, pj)
                    dst = land[p].at[me]
                pltpu.make_async_remote_copy(
                    src_ref=s, dst_ref=dst, send_sem=send_sems.at[p], recv_sem=recv_sems.at[p],
                    device_id=peer, device_id_type=MESH_ID).start()
        token[...] = jnp.zeros_like(token)

    outs = pl.pallas_call(
        body, name=name,
        out_shape=(pltpu.SemaphoreType.DMA((n,)), pltpu.SemaphoreType.DMA((n,)),
                   *[pltpu.HBM(s.shape, s.dtype) for s in srcs], *[pltpu.HBM(l.shape, l.dtype) for l in lands],
                   _sds((8, 128), F32)),
        in_specs=[_HBM] * (2 * n) + [pl.BlockSpec(memory_space=pl.ANY)],
        out_specs=(_SEM, _SEM, *[_HBM] * (2 * n), pl.BlockSpec(memory_space=pltpu.VMEM)),
        input_output_aliases={i: 2 + i for i in range(2 * n)},
        compiler_params=pltpu.CompilerParams(has_side_effects=_EFFECT),
    )(*[_in_hbm(s) for s in srcs], *[_in_hbm(l) for l in lands], after)
    return outs[0], outs[1], list(outs[2:2 + n]), list(outs[2 + n:2 + 2 * n]), outs[-1]


def exchange_wait(mode, started, axes, after, name):
    send_sems, recv_sems, srcs, lands, _ = started
    n = len(srcs)

    def body(*refs):
        land = refs[n:2 * n]
        send_ref, recv_ref = refs[2 * n], refs[2 * n + 1]
        x, y, c, _ = _position()
        for p in range(n):
            seven = _seven(mode, land[p], axes[p])
            cp = pltpu.make_async_remote_copy(
                src_ref=seven, dst_ref=seven, send_sem=send_ref.at[p], recv_sem=recv_ref.at[p],
                device_id=(x, y, c), device_id_type=MESH_ID)
            cp.wait_send()
            cp.wait_recv()

    outs = pl.pallas_call(
        body, name=name,
        out_shape=(*[pltpu.HBM(s.shape, s.dtype) for s in srcs], *[pltpu.HBM(l.shape, l.dtype) for l in lands]),
        in_specs=[_HBM] * (2 * n) + [_SEM, _SEM, pl.BlockSpec(memory_space=pl.ANY)],
        out_specs=tuple([_HBM] * (2 * n)),
        input_output_aliases={i: i for i in range(2 * n)},
        compiler_params=pltpu.CompilerParams(has_side_effects=_EFFECT),
    )(*srcs, *lands, send_sems, recv_sems, after)
    return list(outs[n:])


def _adam_math(g, w, m, v):
    m2 = ADAM_B1 * m + (1.0 - ADAM_B1) * g
    v2 = ADAM_B2 * v + (1.0 - ADAM_B2) * (g * g)
    m_hat = m2 / (1.0 - ADAM_B1 ** ADAM_STEP)
    v_hat = v2 / (1.0 - ADAM_B2 ** ADAM_STEP)
    delta = -ADAM_LR * (m_hat / (jnp.sqrt(v_hat) + ADAM_EPS) + ADAM_WD * w)
    return delta, m2, v2


def adamw(g, w, m, v, name, slots, tr=128):
    r, c = w.shape
    tr = min(tr, r)

    def body(g_ref, w_ref, m_ref, v_ref, go_ref, d_ref, mo_ref, vo_ref):
        if slots:
            gs = g_ref[0].astype(F32)
            for s in range(1, N_DEV):
                gs = gs + g_ref[s].astype(F32)
        else:
            gs = g_ref[...]
        go_ref[...] = gs
        d_ref[...], mo_ref[...], vo_ref[...] = _adam_math(gs, w_ref[...], m_ref[...], v_ref[...])

    row = pl.BlockSpec((tr, c), lambda i: (i, 0))
    gspec = pl.BlockSpec((N_DEV, tr, c), lambda i: (0, i, 0)) if slots else row
    return pl.pallas_call(
        body, name=name, grid=(r // tr,),
        in_specs=[gspec, row, row, row], out_specs=[row] * 4, out_shape=[_sds((r, c), F32)] * 4,
        compiler_params=_cp("parallel"),
    )(g, w, m, v)


def sum_slots(g, name):
    _, r, c = g.shape

    def body(g_ref, o_ref):
        gs = g_ref[0]
        for s in range(1, N_DEV):
            gs = gs + g_ref[s]
        o_ref[...] = gs

    return pl.pallas_call(
        body, name=name, grid=(1,),
        in_specs=[pl.BlockSpec((N_DEV, r, c), lambda i: (0, 0, 0))],
        out_specs=pl.BlockSpec((r, c), lambda i: (0, 0)), out_shape=_sds((r, c), F32),
        compiler_params=_cp("arbitrary"),
    )(g)


def _rows128(a, pad_to=8):
    a = a.reshape(-1, 128)
    pad = (-a.shape[0]) % pad_to
    return jnp.pad(a, ((0, pad), (0, 0))) if pad else a


def kernel(x, even_norm, even_w_in, even_pool_w, even_pool_scale, even_ws, even_bs, even_w_out, odd_norm, odd_w_in, odd_conv_w, odd_w_out, final_norm, loss_target, m_even_norm, m_even_w_in, m_even_pool_w, m_even_pool_scale, m_even_ws, m_even_bs, m_even_w_out, m_odd_norm, m_odd_w_in, m_odd_conv_w, m_odd_w_out, m_final_norm, v_even_norm, v_even_w_in, v_even_pool_w, v_even_pool_scale, v_even_ws, v_even_bs, v_even_w_out, v_odd_norm, v_odd_w_in, v_odd_conv_w, v_odd_w_out, v_final_norm):
    x0 = x[0]
    target = loss_target[0]
    me = 4 * lax.axis_index("x") + 2 * lax.axis_index("y") + lax.axis_index("c")

    (we_in,) = all_gather_two_level([even_w_in[0].astype(BF16)], [1], "gather_even_in")
    odd_small = jnp.pad(odd_norm, ((0, 7), (0, 0))) + jnp.pad(odd_conv_w[0], ((1, 4), (0, 0)))
    rest_axes, odd_axes = [1, 0], [1, 0, 1]
    rest_flight = exchange_start("gather", [even_pool_w[0].astype(BF16), even_w_out[0].astype(BF16)],
                                 rest_axes, we_in, "gather_even_rest_start")
    odd_flight = gather_two_level_start([odd_w_in[0].astype(BF16), odd_w_out[0].astype(BF16), odd_small],
                                        odd_axes, rest_flight[4], "gather_odd_start")
    scale = even_pool_scale
    ws = even_ws[0]
    bs_col = even_bs[0][:, :, None]
    g_fin = final_norm[None, :]

    z_e, h_e = norm_matmul(x0, even_norm + odd_flight[4][0:1, 0:1], we_in, "even_in", tn=1280)
    wp, we_out = exchange_wait("gather", rest_flight, rest_axes, z_e, "gather_even_rest_wait")
    y_e, pooled = even_mix(z_e, wp, scale, ws, bs_col, "even_mix")
    odd_flight = gather_two_level_forward(odd_flight, odd_axes, y_e, "gather_odd_forward")
    x1 = matmul_residual(y_e, we_out + odd_flight[6][0:1, 0:1].astype(BF16), x0, "even_out")
    wo_in, wo_out, odd_small = gather_two_level_wait(odd_flight, odd_axes, x1, "gather_odd_wait")
    g_odd, conv_w = odd_small[0:1], odd_small[1:4]
    z_o, h_o = norm_matmul(x1, g_odd, wo_in, "odd_in", tn=2048)
    qkv = [z_o] + [permute_qkv(z_o, gi, f"dilate_qkv_{gi}") for gi in (1, 2)]
    attn = [attn_fwd(qkv[gi], gi, f"attn_fwd_{gi}") for gi in range(3)]
    for gi in (1, 2):
        attn[gi] = permute_mats(attn[gi][0], [attn[gi][1]], gi, True, f"undilate_attn_{gi}")
    y_o, yc, lse_tot, conv = odd_mix([a[0] for a in attn], [a[1] for a in attn], z_o, conv_w, "odd_mix")
    x2 = matmul_residual(y_o, wo_out, x1, "odd_out")
    dx2, loss_blk, dg_fin = final_norm_loss(x2, g_fin, target, "final_norm_loss")

    dy_o, dwo_out = out_proj_bwd(dx2, wo_out.T, y_o, "odd_out_bwd")
    dz5_o, dyc, dd, dconv_w = odd_mix_bwd(dy_o, yc, conv, z_o, conv_w, "odd_mix_bwd")
    dqkv = [attn_bwd(z_o, dyc, lse_tot, dd, 0, "attn_bwd_0")]
    for gi in (1, 2):
        dyc_g, lse_g, dd_g = permute_mats(dyc, [lse_tot, dd], gi, False, f"dilate_dy_{gi}")
        dqkv.append(permute_slabs(attn_bwd(qkv[gi], dyc_g, lse_g, dd_g, gi, f"attn_bwd_{gi}"),
                                  gi, True, f"undilate_dqkv_{gi}"))
    src_o = [(dqkv[0], ("slabs", 0)), (dqkv[1], ("slabs", 1)), (dqkv[2], ("slabs", 2)), (dz5_o, ("cols", 9, 5))]
    dwo_in = in_proj_dw(h_o, src_o, ODD_COLBLOCKS, "odd_in_dw")
    odd_grads = exchange_start("scatter", [dwo_in, dwo_out], [1, 0], dconv_w, "scatter_odd_start")
    dx1, dg_odd = in_proj_dx(src_o, ODD_COLBLOCKS, wo_in.T, x1, g_odd + odd_grads[4][0:1, 0:1], dx2, "odd_in_dx",
                             epilogue_buffers=1)

    dy_e, dwe_out = out_proj_bwd(dx1, we_out.T, y_e, "even_out_bwd")
    dz_e, dwp, dscale, dws, dbs_col = even_mix_bwd(dy_e, z_e, pooled, wp, scale, ws, bs_col, "even_mix_bwd")
    src_e = [(dz_e, ("cols", 0, 5))]
    dwe_in = in_proj_dw(h_e, src_e, 5, "even_in_dw")
    even_grads = exchange_start("scatter", [dwe_in, dwp.astype(BF16), dwe_out], [1, 1, 0], dscale, "scatter_even_start")
    dx0, dg_even = in_proj_dx(src_e, 5, we_in.T, x0, even_norm + even_grads[4][0:1, 0:1], dx1, "even_in_dx")
    p_wo_in, p_wo_out = exchange_wait("scatter", odd_grads, [1, 0], dx0, "scatter_odd_wait")
    p_we_in, p_wp, p_we_out = exchange_wait("scatter", even_grads, [1, 1, 0], dx0, "scatter_even_wait")
    small = jnp.concatenate([
        _rows128(dg_even), _rows128(dscale), _rows128(dws), _rows128(dbs_col), _rows128(dg_fin),
        _rows128(dg_odd), _rows128(dconv_w)], axis=0)
    n_small = small.shape[0]
    (small_all,) = all_gather([small], [0], "gather_small_grads")
    small_sum = sum_slots(small_all.reshape(N_DEV, n_small, 128), "sum_small_grads")
    g_even_norm = small_sum[0:8].reshape(1, 1024)
    g_scale = small_sum[8:16].reshape(1, 1024)
    g_ws = small_sum[16:528]
    g_bs = small_sum[528:532]
    g_final = small_sum[536:544].reshape(1, 1024)
    g_odd_norm = lax.dynamic_slice_in_dim(small_sum[544:552], me, 1, axis=0)
    g_conv = lax.dynamic_index_in_dim(small_sum[552:576].reshape(3, 8, 128), me, axis=1, keepdims=False)

    two_d = lambda a, r, c: a.reshape(r, c)
    sharded = [
        ("even_w_in", p_we_in, even_w_in, m_even_w_in, v_even_w_in, (1024, 640)),
        ("even_pool_w", p_wp.reshape(N_DEV, 128, 256), even_pool_w, m_even_pool_w, v_even_pool_w, (128, 256)),
        ("even_w_out", p_we_out, even_w_out, m_even_w_out, v_even_w_out, (256, 1024)),
        ("odd_w_in", p_wo_in, odd_w_in, m_odd_w_in, v_odd_w_in, (1024, 1792)),
        ("odd_w_out", p_wo_out, odd_w_out, m_odd_w_out, v_odd_w_out, (256, 1024)),
    ]
    plain = [
        ("even_norm", g_even_norm, even_norm, m_even_norm, v_even_norm, (1, 1024)),
        ("even_pool_scale", g_scale, even_pool_scale, m_even_pool_scale, v_even_pool_scale, (1, 1024)),
        ("even_ws", g_ws, even_ws, m_even_ws, v_even_ws, (512, 128)),
        ("even_bs", g_bs, even_bs, m_even_bs, v_even_bs, (4, 128)),
        ("odd_norm", g_odd_norm, odd_norm, m_odd_norm, v_odd_norm, (1, 128)),
        ("odd_conv_w", g_conv, odd_conv_w, m_odd_conv_w, v_odd_conv_w, (3, 128)),
        ("final_norm", g_final, final_norm, m_final_norm, v_final_norm, (1, 1024)),
    ]
    res = {}
    for name, g, w, m, v, (r, c) in sharded:
        outs = adamw(g, two_d(w, r, c), two_d(m, r, c), two_d(v, r, c), "adamw_" + name, slots=True)
        res[name] = [o.reshape(w.shape) for o in outs]
    for name, g, w, m, v, (r, c) in plain:
        outs = adamw(two_d(g, r, c), two_d(w, r, c), two_d(m, r, c), two_d(v, r, c), "adamw_" + name, slots=False)
        res[name] = [o.reshape(w.shape) for o in outs]

    loss = lax.psum(loss_blk[0, 0], ("x", "y", "c"))
    order = ["even_norm", "even_w_in", "even_pool_w", "even_pool_scale", "even_ws", "even_bs", "even_w_out",
             "odd_norm", "odd_w_in", "odd_conv_w", "odd_w_out", "final_norm"]
    return (loss, dx0[None], *[res[n][0] for n in order], *[res[n][1] for n in order],
            *[res[n][2] for n in order], *[res[n][3] for n in order])
```

```python
import functools

import jax
import jax.numpy as jnp
from jax import lax
from jax.experimental import pallas as pl
from jax.experimental.pallas import tpu as pltpu

F32 = jnp.float32
BF16 = jnp.bfloat16
MESH_ID = pl.DeviceIdType.MESH

EPS = 1e-6
NEG = -1e30
N_DEV = 8
POOL_SIZES = (2, 4, 8, 16)
DILATIONS = (1, 4, 16)
N_HEADS = 24
HEADS_PER_GROUP = 8
HEAD_DIM = 128
ATTN_BLOCK = 128
CHUNK = 128
CB = 1024
HALO = 16
ODD_COLBLOCKS = 14
ADAM_LR = 0.001
ADAM_B1 = 0.9
ADAM_B2 = 0.999
ADAM_EPS = 1e-08
ADAM_WD = 0.01
ADAM_STEP = 10
VMEM_LIMIT = 52 * 1024 * 1024


def _cp(*sem):
    return pltpu.CompilerParams(dimension_semantics=sem, vmem_limit_bytes=VMEM_LIMIT)


def _dot(a, b):
    return jnp.dot(a, b, preferred_element_type=F32)


def _dot_nt(a, b):
    return lax.dot_general(a, b, (((1,), (1,)), ((), ())), preferred_element_type=F32)


def _dot_tn(a, b):
    return lax.dot_general(a, b, (((0,), (0,)), ((), ())), preferred_element_type=F32)


def _sigmoid(x):
    return 0.5 * jnp.tanh(0.5 * x) + 0.5


def _silu_and_grad(x):
    s = _sigmoid(x)
    return x * s, s * (1.0 + x * (1.0 - s))


def _sds(shape, dtype):
    return jax.ShapeDtypeStruct(shape, dtype)


def norm_matmul(x, g, w, name, tn, tm=1024, rows=256):
    t, d = x.shape
    n = w.shape[1]

    def body(x_ref, g_ref, w_ref, z_ref, h_ref):
        @pl.when(pl.program_id(1) == 0)
        def _():
            for c in range(tm // rows):
                rs = slice(c * rows, (c + 1) * rows)
                xv = x_ref[rs, :]
                r = lax.rsqrt(jnp.mean(xv * xv, axis=-1, keepdims=True) + EPS)
                h_ref[rs, :] = ((xv * r) * g_ref[...]).astype(BF16)

        z_ref[...] = _dot(h_ref[...], w_ref[...]).astype(BF16)

    return pl.pallas_call(
        body, name=name, grid=(t // tm, n // tn),
        in_specs=[pl.BlockSpec((tm, d), lambda i, j: (i, 0)),
                  pl.BlockSpec((1, d), lambda i, j: (0, 0)),
                  pl.BlockSpec((d, tn), lambda i, j: (0, j))],
        out_specs=[pl.BlockSpec((tm, tn), lambda i, j: (i, j)),
                   pl.BlockSpec((tm, d), lambda i, j: (i, 0))],
        out_shape=[_sds((t, n), BF16), _sds((t, d), BF16)],
        compiler_params=_cp("parallel", "arbitrary"),
    )(x, g, w)


def matmul_residual(y, w, x, name, next_gain=None, tm=1024, rows=256):
    t, k = y.shape
    d = w.shape[1]

    def body(y_ref, w_ref, x_ref, *rest):
        o_ref = rest[-1] if next_gain is None else rest[-2]
        o_ref[...] = x_ref[...] + _dot(y_ref[...], w_ref[...])
        if next_gain is not None:
            g_ref, h_ref = rest[0], rest[-1]
            for c in range(tm // rows):
                rs = slice(c * rows, (c + 1) * rows)
                xv = o_ref[rs, :]
                r = lax.rsqrt(jnp.mean(xv * xv, axis=-1, keepdims=True) + EPS)
                h_ref[rs, :] = ((xv * r) * g_ref[...]).astype(BF16)

    row = pl.BlockSpec((tm, d), lambda i: (i, 0))
    in_specs = [pl.BlockSpec((tm, k), lambda i: (i, 0)), pl.BlockSpec((k, d), lambda i: (0, 0)), row]
    if next_gain is None:
        return pl.pallas_call(
            body, name=name, grid=(t // tm,), in_specs=in_specs, out_specs=row, out_shape=_sds((t, d), F32),
            compiler_params=_cp("parallel"),
        )(y, w, x)
    return pl.pallas_call(
        body, name=name, grid=(t // tm,),
        in_specs=in_specs + [pl.BlockSpec((1, d), lambda i: (0, 0))],
        out_specs=[row, row], out_shape=[_sds((t, d), F32), _sds((t, d), BF16)],
        compiler_params=_cp("parallel"),
    )(y, w, x, next_gain)


def matmul_layouts(hs, w, name, tm=2048):
    t, d = hs[0].shape
    n_blocks = w.shape[1] // CB

    def layout_of(j):
        return jnp.where(j < 9, j // 3, 0)

    def column_of(j):
        return jnp.where(j < 9, (j % 3) * 3 + j // 3, j)

    def body(h0, h1, h2, w_ref, z_ref):
        layout = layout_of(pl.program_id(1))
        for v, h_ref in enumerate((h0, h1, h2)):
            @pl.when(layout == v)
            def _(h_ref=h_ref):
                z_ref[...] = _dot(h_ref[...], w_ref[...]).astype(BF16)

    row = pl.BlockSpec((tm, d), lambda i, j: (i, 0))
    return pl.pallas_call(
        body, name=name, grid=(t // tm, n_blocks),
        in_specs=[row, row, row, pl.BlockSpec((d, CB), lambda i, j: (0, column_of(j)))],
        out_specs=pl.BlockSpec((tm, CB), lambda i, j: (i, column_of(j))),
        out_shape=_sds((t, n_blocks * CB), BF16),
        compiler_params=_cp("parallel", "arbitrary"),
    )(*hs, w)


def out_proj_bwd(dx, wt, y, name, tm=512):
    t, d = dx.shape
    k = wt.shape[1]
    steps = t // tm
    half = d // 2

    def body(dx_ref, wt_ref, y_ref, dy_ref, dw_ref, acc):
        i = pl.program_id(0)
        dxb = dx_ref[...].astype(BF16)
        dy_ref[...] = _dot(dxb, wt_ref[...]).astype(BF16)

        @pl.when(i == 0)
        def _():
            acc[...] = jnp.zeros_like(acc)

        for c in range(2):
            acc[:, c * half:(c + 1) * half] += _dot_tn(y_ref[...], dxb[:, c * half:(c + 1) * half])

        @pl.when(i == steps - 1)
        def _():
            dw_ref[...] = acc[...].astype(BF16)

    return pl.pallas_call(
        body, name=name, grid=(steps,),
        in_specs=[pl.BlockSpec((tm, d), lambda i: (i, 0)),
                  pl.BlockSpec((d, k), lambda i: (0, 0)),
                  pl.BlockSpec((tm, k), lambda i: (i, 0))],
        out_specs=[pl.BlockSpec((tm, k), lambda i: (i, 0)),
                   pl.BlockSpec((k, d), lambda i: (0, 0))],
        out_shape=[_sds((t, k), BF16), _sds((k, d), BF16)],
        scratch_shapes=[pltpu.VMEM((k, d), F32)],
        compiler_params=_cp("arbitrary"),
    )(dx, wt, y)


def _source_block(kind, rows):
    if kind[0] == "cols":
        return (rows, CB)
    return (None, rows, CB)


def _source_active(kind, j):
    if kind[0] == "cols":
        _, first, n = kind
        return (j >= first) & (j < first + n), jnp.clip(j - first, 0, n - 1)
    _, group = kind
    return (j < 9) & (j % 3 == group), jnp.clip(j // 3, 0, 2)


def _source_index(kind, row_block, inner):
    if kind[0] == "cols":
        return (row_block, inner)
    return (inner, row_block, 0)


def in_proj_dw(h, sources, n_blocks, name, tk=1024):
    t, d = h.shape
    steps = t // tk
    ns = len(sources)

    def body(*refs):
        h_ref, src_refs, dw_ref, acc = refs[0], refs[1:1 + ns], refs[1 + ns], refs[2 + ns]
        j, k = pl.program_id(0), pl.program_id(1)

        @pl.when(k == 0)
        def _():
            acc[...] = jnp.zeros_like(acc)

        for s, (_, kind) in enumerate(sources):
            active, _ = _source_active(kind, j)

            @pl.when(active)
            def _(s=s):
                acc[...] += _dot_tn(h_ref[...], src_refs[s][...])

        @pl.when(k == steps - 1)
        def _():
            dw_ref[...] = acc[...].astype(BF16)

    def src_spec(kind):
        def index(j, k):
            active, inner = _source_active(kind, j)
            return _source_index(kind, jnp.where(active, k, 0), inner)
        return pl.BlockSpec(_source_block(kind, tk), index)

    return pl.pallas_call(
        body, name=name, grid=(n_blocks, steps),
        in_specs=[pl.BlockSpec((tk, d), lambda j, k: (k, 0))] + [src_spec(kind) for _, kind in sources],
        out_specs=pl.BlockSpec((d, CB), lambda j, k: (0, j)),
        out_shape=_sds((d, n_blocks * CB), BF16),
        scratch_shapes=[pltpu.VMEM((d, CB), F32)],
        compiler_params=_cp("parallel", "arbitrary"),
    )(h, *[a for a, _ in sources])


def in_proj_dx(sources, n_blocks, wt, x, g, dres, name, tm=1024, rows=256, epilogue_buffers=2):
    t, d = x.shape
    ns = len(sources)

    def body(*refs):
        src_refs = refs[:ns]
        w_ref, x_ref, g_ref, dres_ref, dx_ref, dg_ref, acc = refs[ns:]
        i, p = pl.program_id(0), pl.program_id(1)

        @pl.when(p == 0)
        def _():
            acc[...] = jnp.zeros_like(acc)

        for s, (_, kind) in enumerate(sources):
            active, _ = _source_active(kind, p)

            @pl.when(active)
            def _(s=s):
                acc[...] += _dot(src_refs[s][...], w_ref[...])

        @pl.when(p == n_blocks - 1)
        def _():
            part = jnp.zeros((1, d), F32)
            for c in range(tm // rows):
                rs = slice(c * rows, (c + 1) * rows)
                xv = x_ref[rs, :]
                r = lax.rsqrt(jnp.mean(xv * xv, axis=-1, keepdims=True) + EPS)
                xn = xv * r
                dh = acc[rs, :]
                dhg = dh * g_ref[...]
                dx_ref[rs, :] = dres_ref[rs, :] + r * (dhg - xn * jnp.mean(dhg * xn, axis=-1, keepdims=True))
                part = part + jnp.sum(dh * xn, axis=0, keepdims=True)

            @pl.when(i == 0)
            def _():
                dg_ref[...] = part

            @pl.when(i > 0)
            def _():
                dg_ref[...] += part

    def src_spec(kind):
        def index(i, p):
            if kind[0] == "cols":
                _, first, n = kind
                inner = jnp.clip(p - first, 0, n - 1)
            else:
                _, group = kind
                inner = jnp.clip((p - group + 2) // 3, 0, 2)
            return _source_index(kind, i, inner)
        return pl.BlockSpec(_source_block(kind, tm), index)

    return pl.pallas_call(
        body, name=name, grid=(t // tm, n_blocks),
        in_specs=[src_spec(kind) for _, kind in sources] + [
            pl.BlockSpec((CB, d), lambda i, p: (p, 0)),
            pl.BlockSpec((tm, d), lambda i, p: (i, 0), pipeline_mode=pl.Buffered(epilogue_buffers)),
            pl.BlockSpec((1, d), lambda i, p: (0, 0)),
            pl.BlockSpec((tm, d), lambda i, p: (i, 0), pipeline_mode=pl.Buffered(epilogue_buffers))],
        out_specs=[pl.BlockSpec((tm, d), lambda i, p: (i, 0)),
                   pl.BlockSpec((1, d), lambda i, p: (0, 0))],
        out_shape=[_sds((t, d), F32), _sds((1, d), F32)],
        scratch_shapes=[pltpu.VMEM((tm, d), F32)],
        compiler_params=_cp("arbitrary", "arbitrary"),
    )(*[a for a, _ in sources], wt, x, g, dres)


def _window_counts(first_row, rows, w):
    t = first_row + lax.broadcasted_iota(jnp.int32, (rows, 1), 0)
    return jnp.minimum(t + 1, w).astype(F32)


def _tril_bf16(ws_ref, g):
    r = lax.broadcasted_iota(jnp.int32, (CHUNK, CHUNK), 0)
    c = lax.broadcasted_iota(jnp.int32, (CHUNK, CHUNK), 1)
    return jnp.where(r >= c, ws_ref[g], 0.0).astype(BF16), r >= c


def even_mix(z, wp, scale, ws, bs_col, name, tm=256):
    t = z.shape[0]
    gd = CB // len(POOL_SIZES)

    def body(a_ref, ga_ref, u_ref, v_ref, gb_ref, wp_ref, sc_ref, ws_ref, bs_ref, y_ref, pooled_ref, aext):
        i = pl.program_id(0)

        @pl.when(i == 0)
        def _():
            aext[0:HALO, :] = jnp.zeros((HALO, CB), F32)

        @pl.when(i > 0)
        def _():
            aext[0:HALO, :] = aext[tm:tm + HALO, :]

        aext[HALO:HALO + tm, :] = a_ref[...].astype(F32)
        for g, w in enumerate(POOL_SIZES):
            cols = slice(g * gd, (g + 1) * gd)
            tok = aext[HALO:HALO + tm, cols]
            s = tok
            for k in range(1, w):
                s = s + aext[HALO - k:HALO - k + tm, cols]
            pooled = (s / _window_counts(i * tm, tm, w) - tok).astype(BF16)
            pooled_ref[:, cols] = pooled
            mixed = _dot(pooled, wp_ref[g])
            silu_a, _ = _silu_and_grad(ga_ref[:, cols].astype(F32))
            y_ref[:, cols] = (mixed * sc_ref[:, cols] * silu_a).astype(BF16)
        for g in range(4):
            cols = slice(g * gd, (g + 1) * gd)
            wg, _ = _tril_bf16(ws_ref, g)
            for c in range(tm // CHUNK):
                rows = slice(c * CHUNK, (c + 1) * CHUNK)
                m = _dot(wg, v_ref[rows, cols]) + bs_ref[g]
                silu_b, _ = _silu_and_grad(gb_ref[rows, cols].astype(F32))
                y_ref[rows, CB + g * gd:CB + (g + 1) * gd] = (
                    u_ref[rows, cols].astype(F32) * m * silu_b).astype(BF16)

    zspec = lambda cb: pl.BlockSpec((tm, CB), lambda i, cb=cb: (i, cb))
    full = lambda shape: pl.BlockSpec(shape, lambda i: (0,) * len(shape))
    return pl.pallas_call(
        body, name=name, grid=(t // tm,),
        in_specs=[zspec(0), zspec(1), zspec(2), zspec(3), zspec(4),
                  full(wp.shape), full(scale.shape), full(ws.shape), full(bs_col.shape)],
        out_specs=[pl.BlockSpec((tm, 2 * CB), lambda i: (i, 0)), pl.BlockSpec((tm, CB), lambda i: (i, 0))],
        out_shape=[_sds((t, 2 * CB), BF16), _sds((t, CB), BF16)],
        scratch_shapes=[pltpu.VMEM((HALO + tm, CB), F32)],
        compiler_params=_cp("arbitrary"),
    )(z, z, z, z, z, wp, scale, ws, bs_col)


def even_mix_bwd(dy, z, pooled, wp, scale, ws, bs_col, name, tm=256):
    t = z.shape[0]
    nt = t // tm
    gd = CB // len(POOL_SIZES)

    def body(dya_ref, dyb_ref, ga_ref, u_ref, v_ref, gb_ref, pooled_ref, wp_ref, sc_ref, ws_ref, bs_ref,
             dz_ref, dwp_ref, dsc_ref, dws_ref, dbs_ref, dpext):
        i = pl.program_id(0)
        tile = nt - 1 - i

        @pl.when(i == 0)
        def _():
            dpext[tm:tm + HALO, :] = jnp.zeros((HALO, CB), F32)
            dwp_ref[...] = jnp.zeros_like(dwp_ref)
            dsc_ref[...] = jnp.zeros_like(dsc_ref)
            dws_ref[...] = jnp.zeros_like(dws_ref)
            dbs_ref[...] = jnp.zeros_like(dbs_ref)

        @pl.when(i > 0)
        def _():
            dpext[tm:tm + HALO, :] = dpext[0:HALO, :]

        for g, w in enumerate(POOL_SIZES):
            cols = slice(g * gd, (g + 1) * gd)
            pooled_g = pooled_ref[:, cols]
            mixed = _dot(pooled_g, wp_ref[g])
            silu_a, dsilu_a = _silu_and_grad(ga_ref[:, cols].astype(F32))
            dya = dya_ref[:, cols].astype(F32)
            sc = sc_ref[:, cols]
            dmixed = (dya * sc * silu_a).astype(BF16)
            dsc_ref[:, cols] += jnp.sum(dya * mixed * silu_a, axis=0, keepdims=True)
            dz_ref[:, CB + g * gd:CB + (g + 1) * gd] = (dya * mixed * sc * dsilu_a).astype(BF16)
            dwp_ref[g] += _dot_tn(pooled_g, dmixed)
            dpooled = _dot_nt(dmixed, wp_ref[g])
            dpext[0:tm, cols] = dpooled / _window_counts(tile * tm, tm, w)
            s = dpext[0:tm, cols]
            for k in range(1, w):
                s = s + dpext[k:k + tm, cols]
            dz_ref[:, cols] = (s - dpooled).astype(BF16)
        for g in range(4):
            cols = slice(g * gd, (g + 1) * gd)
            wg, lower = _tril_bf16(ws_ref, g)
            dws_g = jnp.zeros((CHUNK, CHUNK), F32)
            dbs_g = jnp.zeros((CHUNK, 1), F32)
            for c in range(tm // CHUNK):
                rows = slice(c * CHUNK, (c + 1) * CHUNK)
                vb = v_ref[rows, cols]
                m = _dot(wg, vb) + bs_ref[g]
                gbv = gb_ref[rows, cols].astype(F32)
                silu_b, dsilu_b = _silu_and_grad(gbv)
                dyb = dyb_ref[rows, cols].astype(F32)
                uv = u_ref[rows, cols].astype(F32)
                dm = dyb * silu_b * uv
                dmb = dm.astype(BF16)
                dz_ref[rows, 2 * CB + g * gd:2 * CB + (g + 1) * gd] = (dyb * silu_b * m).astype(BF16)
                dz_ref[rows, 3 * CB + g * gd:3 * CB + (g + 1) * gd] = _dot_tn(wg, dmb).astype(BF16)
                dz_ref[rows, 4 * CB + g * gd:4 * CB + (g + 1) * gd] = (dyb * uv * m * dsilu_b).astype(BF16)
                dws_g = dws_g + _dot_nt(dmb, vb)
                dbs_g = dbs_g + jnp.sum(dm, axis=1, keepdims=True)
            dws_ref[g] += jnp.where(lower, dws_g, 0.0)
            dbs_ref[g] += dbs_g

    rev = lambda cb: pl.BlockSpec((tm, CB), lambda i, cb=cb: (nt - 1 - i, cb))
    full = lambda shape: pl.BlockSpec(shape, lambda i: (0,) * len(shape))
    return pl.pallas_call(
        body, name=name, grid=(nt,),
        in_specs=[rev(0), rev(1), rev(1), rev(2), rev(3), rev(4), rev(0),
                  full(wp.shape), full(scale.shape), full(ws.shape), full(bs_col.shape)],
        out_specs=[pl.BlockSpec((tm, 5 * CB), lambda i: (nt - 1 - i, 0)),
                   full(wp.shape), full(scale.shape), full(ws.shape), full(bs_col.shape)],
        out_shape=[_sds((t, 5 * CB), BF16), _sds(wp.shape, F32), _sds(scale.shape, F32),
                   _sds(ws.shape, F32), _sds(bs_col.shape, F32)],
        scratch_shapes=[pltpu.VMEM((tm + HALO, CB), F32)],
        compiler_params=_cp("arbitrary"),
    )(dy, dy, z, z, z, z, pooled, wp, scale, ws, bs_col)


def _slope(group, head):
    return float(2.0 ** (-8.0 * (group * HEADS_PER_GROUP + head + 1) / N_HEADS))


def _band(dilation):
    qi = lax.broadcasted_iota(jnp.int32, (ATTN_BLOCK, ATTN_BLOCK), 0)
    ki = lax.broadcasted_iota(jnp.int32, (ATTN_BLOCK, ATTN_BLOCK), 1)
    dist_prev = ((qi + ATTN_BLOCK - ki) * dilation).astype(F32)
    dist_cur = ((qi - ki) * dilation).astype(F32)
    return dist_prev, ki >= qi, dist_cur, ki <= qi


def _permute_rows(name, d, inverse, arrays, in_specs, out_shapes, out_specs, n_chunks, widths):
    rows = ATTN_BLOCK * d
    t = out_shapes[0].shape[-2]
    n = len(arrays)

    lanes = HEAD_DIM

    def body(*refs):
        for a, o, s, w in zip(refs[:n], refs[n:2 * n], refs[2 * n:], widths):
            for k in range(w // lanes):
                cols = slice(k * lanes, (k + 1) * lanes)
                if inverse:
                    for r in range(d):
                        s[k, pl.ds(r, ATTN_BLOCK, stride=d), :] = (
                            a[r * ATTN_BLOCK:(r + 1) * ATTN_BLOCK, cols].astype(F32))
                    o[:, cols] = s[k].astype(o.dtype)
                else:
                    s[k] = a[:, cols].astype(F32)
                    for r in range(d):
                        o[r * ATTN_BLOCK:(r + 1) * ATTN_BLOCK, cols] = (
                            s[k, pl.ds(r, ATTN_BLOCK, stride=d), :].astype(o.dtype))

    return pl.pallas_call(
        body, name=name, grid=(t // rows, n_chunks),
        in_specs=in_specs, out_specs=out_specs, out_shape=out_shapes,
        scratch_shapes=[pltpu.VMEM((w // lanes, rows, lanes), F32) for w in widths],
        compiler_params=_cp("parallel", "arbitrary"),
    )(*arrays)


def _chunk_width(d):
    return 4 * CB // d


def permute_mats(wide, narrow, group, inverse, name):
    d = DILATIONS[group]
    rows, cw = ATTN_BLOCK * d, _chunk_width(d)
    wide_spec = pl.BlockSpec((rows, cw), lambda i, c: (i, c))
    narrow_spec = pl.BlockSpec((rows, HEAD_DIM), lambda i, c: (i, 0))
    arrays = [wide] + list(narrow)
    specs = [wide_spec] + [narrow_spec] * len(narrow)
    return _permute_rows(name, d, inverse, arrays, specs, [_sds(a.shape, a.dtype) for a in arrays], specs,
                         CB // cw, [cw] + [HEAD_DIM] * len(narrow))


def permute_slabs(s, group, inverse, name):
    d = DILATIONS[group]
    rows, cw = ATTN_BLOCK * d, _chunk_width(d)
    per = CB // cw
    spec = pl.BlockSpec((None, rows, cw), lambda i, c: (c // per, i, c % per))
    (out,) = _permute_rows(name, d, inverse, [s], [spec], [_sds(s.shape, s.dtype)], [spec], 3 * per, [cw])
    return out


def _lane_pack(cols):
    rows = cols[0].shape[0]
    lane = lax.broadcasted_iota(jnp.int32, (rows, HEAD_DIM), 1)
    out = jnp.zeros((rows, HEAD_DIM), F32)
    for h, c in enumerate(cols):
        out = jnp.where(lane == h, c, out)
    return out


def _qkv_col(group, from_z):
    return (lambda which: which * 3 + group) if from_z else (lambda which: which)


SCORE_SCALE = HEAD_DIM ** -0.5
LOG2_E = 1.4426950408889634
EXP2_SCALE = SCORE_SCALE * LOG2_E


def _fill_bias(bias_ref, group, d):
    dist_p, ok_p, dist_c, ok_c = _band(d)
    for h in range(HEADS_PER_GROUP):
        k = -_slope(group, h) / SCORE_SCALE
        bias_ref[2 * h] = jnp.where(ok_p, k * dist_p, NEG)
        bias_ref[2 * h + 1] = jnp.where(ok_c, k * dist_c, NEG)


def _raw_scores(q, kp, kc, bias_ref, h, has_prev):
    tp = jnp.where(has_prev, _dot_nt(q, kp) + bias_ref[2 * h], NEG)
    tc = _dot_nt(q, kc) + bias_ref[2 * h + 1]
    return tp, tc


def attn_fwd(qkv, group, name):
    t = qkv.shape[0]
    d = DILATIONS[group]
    col = _qkv_col(group, qkv.shape[1] != 3 * CB)
    heads = range(HEADS_PER_GROUP)

    def body(q_ref, kp_ref, kc_ref, vp_ref, vc_ref, o_ref, lse_ref, bias_ref):
        b = pl.program_id(0)

        @pl.when(b == 0)
        def _():
            _fill_bias(bias_ref, group, d)

        hs = [slice(h * HEAD_DIM, (h + 1) * HEAD_DIM) for h in heads]
        s = [_raw_scores(q_ref[:, hs[h]], kp_ref[:, hs[h]], kc_ref[:, hs[h]], bias_ref, h, b >= d) for h in heads]
        m = [jnp.maximum(jnp.max(tp, axis=1, keepdims=True), jnp.max(tc, axis=1, keepdims=True)) for tp, tc in s]
        e = [(jnp.exp2(EXP2_SCALE * (tp - m[h])), jnp.exp2(EXP2_SCALE * (tc - m[h]))) for h, (tp, tc) in enumerate(s)]
        l = [jnp.sum(ep, axis=1, keepdims=True) + jnp.sum(ec, axis=1, keepdims=True) for ep, ec in e]
        o = [(_dot(ep.astype(BF16), vp_ref[:, hs[h]]) + _dot(ec.astype(BF16), vc_ref[:, hs[h]])) * (1.0 / l[h])
             for h, (ep, ec) in enumerate(e)]
        for h in heads:
            o_ref[:, hs[h]] = o[h].astype(BF16)
        lse_ref[...] = _lane_pack([SCORE_SCALE * m[h] + jnp.log(l[h]) for h in heads])

    blk = (ATTN_BLOCK, CB)
    cur = lambda which: pl.BlockSpec(blk, lambda b: (b, col(which)))
    prev = lambda which: pl.BlockSpec(blk, lambda b: (jnp.maximum(b - d, 0), col(which)))
    return pl.pallas_call(
        body, name=name, grid=(t // ATTN_BLOCK,),
        in_specs=[cur(0), prev(1), cur(1), prev(2), cur(2)],
        out_specs=[pl.BlockSpec(blk, lambda b: (b, 0)), pl.BlockSpec((ATTN_BLOCK, HEAD_DIM), lambda b: (b, 0))],
        out_shape=[_sds((t, CB), BF16), _sds((t, HEAD_DIM), F32)],
        scratch_shapes=[pltpu.VMEM((2 * HEADS_PER_GROUP, ATTN_BLOCK, ATTN_BLOCK), F32)],
        compiler_params=_cp("arbitrary"),
    )(qkv, qkv, qkv, qkv, qkv)


def attn_bwd(z, dyc, lse, dd, group, name):
    t = z.shape[0]
    d = DILATIONS[group]
    nb = t // (ATTN_BLOCK * d)
    col = _qkv_col(group, z.shape[1] != 3 * CB)
    heads = range(HEADS_PER_GROUP)

    def body(q_ref, kp_ref, kc_ref, vp_ref, vc_ref, dy_ref, lse_ref, dd_ref, out_ref, carry, bias_ref):
        r, n = pl.program_id(0), pl.program_id(1)

        @pl.when(jnp.logical_and(r == 0, n == 0))
        def _():
            _fill_bias(bias_ref, group, d)

        @pl.when(n == 0)
        def _():
            carry[...] = jnp.zeros_like(carry)

        @pl.when(n == nb)
        def _():
            out_ref[...] = carry[...].astype(BF16)

        @pl.when(n < nb)
        def _():
            hs = [slice(h * HEAD_DIM, (h + 1) * HEAD_DIM) for h in heads]
            q = [q_ref[:, hs[h]] for h in heads]
            kp = [kp_ref[:, hs[h]] for h in heads]
            kc = [kc_ref[:, hs[h]] for h in heads]
            dy = [dy_ref[:, hs[h]] for h in heads]
            old = [[carry[w, :, hs[h]] for h in heads] for w in range(3)]
            s = [_raw_scores(q[h], kp[h], kc[h], bias_ref, h, n > 0) for h in heads]
            l2 = [lse_ref[:, h:h + 1] * LOG2_E for h in heads]
            p = [(jnp.exp2(EXP2_SCALE * tp - l2[h]), jnp.exp2(EXP2_SCALE * tc - l2[h])) for h, (tp, tc) in enumerate(s)]
            ds = [((pp * (_dot_nt(dy[h], vp_ref[:, hs[h]]) - dd_ref[:, h:h + 1]) * SCORE_SCALE).astype(BF16),
                   (pc * (_dot_nt(dy[h], vc_ref[:, hs[h]]) - dd_ref[:, h:h + 1]) * SCORE_SCALE).astype(BF16))
                  for h, (pp, pc) in enumerate(p)]
            for h in heads:
                dsp, dsc = ds[h]
                pp, pc = p[h]
                out_ref[0, :, hs[h]] = old[0][h].astype(BF16)
                out_ref[1, :, hs[h]] = (old[1][h] + _dot_tn(dsp, q[h])).astype(BF16)
                out_ref[2, :, hs[h]] = (old[2][h] + _dot_tn(pp.astype(BF16), dy[h])).astype(BF16)
                carry[0, :, hs[h]] = _dot(dsp, kp[h]) + _dot(dsc, kc[h])
                carry[1, :, hs[h]] = _dot_tn(dsc, q[h])
                carry[2, :, hs[h]] = _dot_tn(pc.astype(BF16), dy[h])

    blk = (ATTN_BLOCK, CB)
    cur_row = lambda r, n: jnp.minimum(n, nb - 1) * d + r
    prev_row = lambda r, n: jnp.clip(n - 1, 0, nb - 1) * d + r
    cur = lambda which: pl.BlockSpec(blk, lambda r, n: (cur_row(r, n), col(which)))
    prev = lambda which: pl.BlockSpec(blk, lambda r, n: (prev_row(r, n), col(which)))
    narrow = pl.BlockSpec((ATTN_BLOCK, HEAD_DIM), lambda r, n: (cur_row(r, n), 0))
    return pl.pallas_call(
        body, name=name, grid=(d, nb + 1),
        in_specs=[cur(0), prev(1), cur(1), prev(2), cur(2),
                  pl.BlockSpec(blk, lambda r, n: (cur_row(r, n), 0)), narrow, narrow],
        out_specs=pl.BlockSpec((3, ATTN_BLOCK, CB), lambda r, n: (0, prev_row(r, n), 0)),
        out_shape=_sds((3, t, CB), BF16),
        scratch_shapes=[pltpu.VMEM((3, ATTN_BLOCK, CB), F32),
                        pltpu.VMEM((2 * HEADS_PER_GROUP, ATTN_BLOCK, ATTN_BLOCK), F32)],
        compiler_params=_cp("arbitrary", "arbitrary"),
    )(z, z, z, z, z, dyc, lse, dd)


def odd_mix(os_, lses, z, conv_w, name, tm=256):
    t = z.shape[0]

    def body(o0, o1, o2, l0, l1, l2, gc_ref, db_ref, dc_ref, dx_ref, gd_ref, cw_ref,
             y_ref, yc_ref, lse_ref, conv_ref, zext):
        i = pl.program_id(0)
        a0, a1, a2 = l0[...], l1[...], l2[...]
        m = jnp.maximum(jnp.maximum(a0, a1), a2)
        tot = m + jnp.log(jnp.exp(a0 - m) + jnp.exp(a1 - m) + jnp.exp(a2 - m))
        lse_ref[...] = tot
        w0, w1, w2 = jnp.exp(a0 - tot), jnp.exp(a1 - tot), jnp.exp(a2 - tot)
        for h in range(HEADS_PER_GROUP):
            hs = slice(h * HEAD_DIM, (h + 1) * HEAD_DIM)
            yc = (w0[:, h:h + 1] * o0[:, hs].astype(F32) + w1[:, h:h + 1] * o1[:, hs].astype(F32)
                  + w2[:, h:h + 1] * o2[:, hs].astype(F32))
            yc_ref[:, hs] = yc.astype(BF16)
            silu_c, _ = _silu_and_grad(gc_ref[:, hs].astype(F32))
            y_ref[:, hs] = (yc * silu_c).astype(BF16)

        @pl.when(i == 0)
        def _():
            zext[0:HALO, :] = jnp.zeros((HALO, CB), F32)

        @pl.when(i > 0)
        def _():
            zext[0:HALO, :] = zext[tm:tm + HALO, :]

        zext[HALO:HALO + tm, :] = dc_ref[...].astype(F32) * dx_ref[...].astype(F32)
        conv = (cw_ref[0:1, :] * zext[HALO - 2:HALO - 2 + tm, :] + cw_ref[1:2, :] * zext[HALO - 1:HALO - 1 + tm, :]
                + cw_ref[2:3, :] * zext[HALO:HALO + tm, :])
        conv_ref[...] = conv.astype(BF16)
        silu_d, _ = _silu_and_grad(gd_ref[...].astype(F32))
        y_ref[:, CB:2 * CB] = (db_ref[...].astype(F32) * conv * silu_d).astype(BF16)

    row = pl.BlockSpec((tm, CB), lambda i: (i, 0))
    narrow = pl.BlockSpec((tm, HEAD_DIM), lambda i: (i, 0))
    zspec = lambda cb: pl.BlockSpec((tm, CB), lambda i, cb=cb: (i, cb))
    return pl.pallas_call(
        body, name=name, grid=(t // tm,),
        in_specs=[row] * 3 + [narrow] * 3 + [zspec(9), zspec(10), zspec(11), zspec(12), zspec(13),
                                             pl.BlockSpec(conv_w.shape, lambda i: (0, 0))],
        out_specs=[pl.BlockSpec((tm, 2 * CB), lambda i: (i, 0)), row, narrow, row],
        out_shape=[_sds((t, 2 * CB), BF16), _sds((t, CB), BF16), _sds((t, HEAD_DIM), F32), _sds((t, CB), BF16)],
        scratch_shapes=[pltpu.VMEM((HALO + tm, CB), F32)],
        compiler_params=_cp("arbitrary"),
    )(*os_, *lses, z, z, z, z, z, conv_w)


def odd_mix_bwd(dy, yc, conv, z, conv_w, name, tm=256):
    t = z.shape[0]
    nt = t // tm

    def body(dyc_ref, dyd_ref, yc_ref, conv_ref, gc_ref, db_ref, dc_ref, dx_ref, gd_ref, cw_ref,
             dz_ref, dyo_ref, dd_ref, dcw_ref, dcext):
        i = pl.program_id(0)

        @pl.when(i == 0)
        def _():
            dcext[tm:tm + HALO, :] = jnp.zeros((HALO, CB), F32)
            dcw_ref[...] = jnp.zeros_like(dcw_ref)

        @pl.when(i > 0)
        def _():
            dcext[tm:tm + HALO, :] = dcext[0:HALO, :]

        silu_c, dsilu_c = _silu_and_grad(gc_ref[...].astype(F32))
        dyc = dyc_ref[...].astype(F32)
        ycv = yc_ref[...].astype(F32)
        dyo = dyc * silu_c
        dyo_ref[...] = dyo.astype(BF16)
        dz_ref[:, 0:CB] = (dyc * ycv * dsilu_c).astype(BF16)
        prod = dyo * ycv
        dd_ref[...] = _lane_pack([jnp.sum(prod[:, h * HEAD_DIM:(h + 1) * HEAD_DIM], axis=1, keepdims=True)
                                  for h in range(HEADS_PER_GROUP)])

        silu_d, dsilu_d = _silu_and_grad(gd_ref[...].astype(F32))
        dyd = dyd_ref[...].astype(F32)
        convv = conv_ref[...].astype(F32)
        dbv = db_ref[...].astype(F32)
        dz_ref[:, CB:2 * CB] = (dyd * convv * silu_d).astype(BF16)
        dz_ref[:, 4 * CB:5 * CB] = (dyd * dbv * convv * dsilu_d).astype(BF16)
        dcext[0:tm, :] = dyd * dbv * silu_d
        dcv, dxv = dc_ref[...].astype(F32), dx_ref[...].astype(F32)
        zc = dcv * dxv
        d0, d1, d2 = dcext[0:tm, :], dcext[1:1 + tm, :], dcext[2:2 + tm, :]
        dzc = cw_ref[2:3, :] * d0 + cw_ref[1:2, :] * d1 + cw_ref[0:1, :] * d2
        dz_ref[:, 2 * CB:3 * CB] = (dzc * dxv).astype(BF16)
        dz_ref[:, 3 * CB:4 * CB] = (dzc * dcv).astype(BF16)
        dcw_ref[0:1, :] += jnp.sum(zc * d2, axis=0, keepdims=True)
        dcw_ref[1:2, :] += jnp.sum(zc * d1, axis=0, keepdims=True)
        dcw_ref[2:3, :] += jnp.sum(zc * d0, axis=0, keepdims=True)

    rev = lambda cb: pl.BlockSpec((tm, CB), lambda i, cb=cb: (nt - 1 - i, cb))
    return pl.pallas_call(
        body, name=name, grid=(nt,),
        in_specs=[rev(0), rev(1), rev(0), rev(0), rev(9), rev(10), rev(11), rev(12), rev(13),
                  pl.BlockSpec(conv_w.shape, lambda i: (0, 0))],
        out_specs=[pl.BlockSpec((tm, 5 * CB), lambda i: (nt - 1 - i, 0)), rev(0),
                   pl.BlockSpec((tm, HEAD_DIM), lambda i: (nt - 1 - i, 0)),
                   pl.BlockSpec(conv_w.shape, lambda i: (0, 0))],
        out_shape=[_sds((t, 5 * CB), BF16), _sds((t, CB), BF16), _sds((t, HEAD_DIM), F32), _sds(conv_w.shape, F32)],
        scratch_shapes=[pltpu.VMEM((tm + HALO, CB), F32)],
        compiler_params=_cp("arbitrary"),
    )(dy, dy, yc, conv, z, z, z, z, z, conv_w)


def final_norm_loss(x, g, target, name, tm=512):
    t, d = x.shape

    def body(x_ref, g_ref, t_ref, dx_ref, loss_ref, dg_ref):
        i = pl.program_id(0)
        xv = x_ref[...]
        r = lax.rsqrt(jnp.mean(xv * xv, axis=-1, keepdims=True) + EPS)
        xn = xv * r
        gv = g_ref[...]
        err = xn * gv - t_ref[...]
        loss = 0.5 * jnp.sum(jnp.mean(err * err, axis=-1, keepdims=True), axis=0, keepdims=True)
        dy = err * (1.0 / d)
        dyg = dy * gv
        dx_ref[...] = r * (dyg - xn * jnp.mean(dyg * xn, axis=-1, keepdims=True))
        dg = jnp.sum(dy * xn, axis=0, keepdims=True)

        @pl.when(i == 0)
        def _():
            loss_ref[...] = jnp.broadcast_to(loss, loss_ref.shape)
            dg_ref[...] = dg

        @pl.when(i > 0)
        def _():
            loss_ref[...] += jnp.broadcast_to(loss, loss_ref.shape)
            dg_ref[...] += dg

    return pl.pallas_call(
        body, name=name, grid=(t // tm,),
        in_specs=[pl.BlockSpec((tm, d), lambda i: (i, 0)), pl.BlockSpec((1, d), lambda i: (0, 0)),
                  pl.BlockSpec((tm, d), lambda i: (i, 0))],
        out_specs=[pl.BlockSpec((tm, d), lambda i: (i, 0)), pl.BlockSpec((8, 128), lambda i: (0, 0)),
                   pl.BlockSpec((1, d), lambda i: (0, 0))],
        out_shape=[_sds((t, d), F32), _sds((8, 128), F32), _sds((1, d), F32)],
        compiler_params=_cp("arbitrary"),
    )(x, g, target)


def _position():
    x, y, c = lax.axis_index("x"), lax.axis_index("y"), lax.axis_index("c")
    return x, y, c, 4 * x + 2 * y + c


def _peer(x, y, c, k):
    px = 1 - x if k & 4 else x
    py = 1 - y if k & 2 else y
    pc = 1 - c if k & 1 else c
    return (px, py, pc), 4 * px + 2 * py + pc


def _block(ref, axis, size, idx):
    index = [slice(None)] * len(ref.shape)
    index[axis] = pl.ds(idx * size, size)
    return ref.at[tuple(index)]


def all_gather(shards, axes, name):
    n = len(shards)
    fulls = []
    for s, ax in zip(shards, axes):
        shape = list(s.shape)
        shape[ax] *= N_DEV
        fulls.append(_sds(tuple(shape), s.dtype))

    def body(*refs):
        ins, outs = refs[:n], refs[n:2 * n]
        send_sems, recv_sems, local_sems = refs[2 * n:]
        x, y, c, me = _position()
        local = []
        for p in range(n):
            size = ins[p].shape[axes[p]]
            cp = pltpu.make_async_copy(ins[p], _block(outs[p], axes[p], size, me), local_sems.at[p])
            cp.start()
            local.append(cp)
        for k in range(1, N_DEV):
            peer, _ = _peer(x, y, c, k)
            for p in range(n):
                size = ins[p].shape[axes[p]]
                pltpu.make_async_remote_copy(
                    src_ref=ins[p], dst_ref=_block(outs[p], axes[p], size, me),
                    send_sem=send_sems.at[p], recv_sem=recv_sems.at[p],
                    device_id=peer, device_id_type=MESH_ID).start()
        for p in range(n):
            size = ins[p].shape[axes[p]]
            seven = _block(outs[p], axes[p], (N_DEV - 1) * size, 0)
            pltpu.make_async_remote_copy(
                src_ref=seven, dst_ref=seven, send_sem=send_sems.at[p], recv_sem=recv_sems.at[p],
                device_id=(x, y, c), device_id_type=MESH_ID).wait()
            local[p].wait()

    any_spec = pl.BlockSpec(memory_space=pl.ANY)
    return pl.pallas_call(
        body, name=name,
        in_specs=[any_spec] * n, out_specs=[any_spec] * n, out_shape=fulls,
        scratch_shapes=[pltpu.SemaphoreType.DMA((n,)), pltpu.SemaphoreType.DMA((n,)), pltpu.SemaphoreType.DMA((n,))],
    )(*shards)


def _other_chips(x, y):
    return [(1 - x, y), (x, 1 - y), (1 - x, 1 - y)]


def _gather_copy(ins, lands, axes, send_sems, recv_sems, k, p, block, to, from_shard=False):
    dst = _block(lands[p], axes[p], lands[p].shape[axes[p]] // N_DEV, block)
    sem = k * len(lands) + p
    return pltpu.make_async_remote_copy(
        src_ref=ins[p] if from_shard else dst, dst_ref=dst, send_sem=send_sems.at[sem], recv_sem=recv_sems.at[sem],
        device_id=to, device_id_type=MESH_ID)


def all_gather_two_level(shards, axes, name):
    n = len(shards)
    fulls = []
    for s, ax in zip(shards, axes):
        shape = list(s.shape)
        shape[ax] *= N_DEV
        fulls.append(_sds(tuple(shape), s.dtype))

    def body(*refs):
        ins, outs = refs[:n], refs[n:2 * n]
        send_sems, recv_sems, local_sems = refs[2 * n:]
        x, y, c, me = _position()
        sibling, here = (x, y, 1 - c), (x, y, c)
        chips = _other_chips(x, y)
        copy = functools.partial(_gather_copy, ins, outs, axes, send_sems, recv_sems)
        local = [pltpu.make_async_copy(ins[p], _block(outs[p], axes[p], ins[p].shape[axes[p]], me), local_sems.at[p])
                 for p in range(n)]
        sent = []
        for p in range(n):
            sent.append(copy(0, p, me, sibling, from_shard=True))
            sent += [copy(1 + j, p, me, (*chip, c), from_shard=True) for j, chip in enumerate(chips)]
        for cp in local + sent:
            cp.start()
        for j, (px, py) in enumerate(chips):
            for p in range(n):
                arrived = 4 * px + 2 * py + c
                copy(1 + j, p, arrived, here).wait_recv()
                sent.append(copy(4 + j, p, arrived, sibling))
                sent[-1].start()
        for p in range(n):
            copy(0, p, 4 * x + 2 * y + 1 - c, here).wait_recv()
            for j, (px, py) in enumerate(chips):
                copy(4 + j, p, 4 * px + 2 * py + 1 - c, here).wait_recv()
        for cp in sent:
            cp.wait_send()
        for cp in local:
            cp.wait()

    any_spec = pl.BlockSpec(memory_space=pl.ANY)
    return pl.pallas_call(
        body, name=name,
        in_specs=[any_spec] * n, out_specs=[any_spec] * n, out_shape=fulls,
        scratch_shapes=[pltpu.SemaphoreType.DMA((7 * n,)), pltpu.SemaphoreType.DMA((7 * n,)), pltpu.SemaphoreType.DMA((n,))],
    )(*shards)


def gather_two_level_start(shards, axes, after, name):
    n = len(shards)
    lands = [_landing("gather", s, ax) for s, ax in zip(shards, axes)]

    def body(*refs):
        ins, land = refs[:n], refs[n:2 * n]
        send_sems, recv_sems, token = refs[2 * n + 1], refs[2 * n + 2], refs[-1]
        x, y, c, me = _position()
        copy = functools.partial(_gather_copy, ins, land, axes, send_sems, recv_sems)
        for p in range(n):
            copy(0, p, me, (x, y, 1 - c), from_shard=True).start()
            for j, chip in enumerate(_other_chips(x, y)):
                copy(1 + j, p, me, (*chip, c), from_shard=True).start()
        token[...] = jnp.zeros_like(token)

    outs = pl.pallas_call(
        body, name=name,
        out_shape=(pltpu.SemaphoreType.DMA((4 * n,)), pltpu.SemaphoreType.DMA((4 * n,)),
                   *[pltpu.HBM(s.shape, s.dtype) for s in shards], *[pltpu.HBM(l.shape, l.dtype) for l in lands],
                   _sds((8, 128), F32)),
        in_specs=[_HBM] * (2 * n) + [pl.BlockSpec(memory_space=pl.ANY)],
        out_specs=(_SEM, _SEM, *[_HBM] * (2 * n), pl.BlockSpec(memory_space=pltpu.VMEM)),
        input_output_aliases={i: 2 + i for i in range(2 * n)},
        compiler_params=pltpu.CompilerParams(has_side_effects=_EFFECT),
    )(*[_in_hbm(s) for s in shards], *[_in_hbm(l) for l in lands], after)
    return outs[0], outs[1], list(outs[2:2 + n]), list(outs[2 + n:2 + 2 * n]), outs[-1]


def gather_two_level_forward(started, axes, after, name):
    send_a, recv_a, shards, lands, _ = started
    n = len(shards)

    def body(*refs):
        land = refs[:n]
        send_first, recv_first = refs[n], refs[n + 1]
        send_fwd, recv_fwd, token = refs[n + 3], refs[n + 4], refs[-1]
        x, y, c, _ = _position()
        for j, (px, py) in enumerate(_other_chips(x, y)):
            for p in range(n):
                arrived = 4 * px + 2 * py + c
                _gather_copy(None, land, axes, send_first, recv_first, 1 + j, p, arrived, (x, y, c)).wait_recv()
                _gather_copy(None, land, axes, send_fwd, recv_fwd, j, p, arrived, (x, y, 1 - c)).start()
        token[...] = jnp.zeros_like(token)

    outs = pl.pallas_call(
        body, name=name,
        out_shape=(pltpu.SemaphoreType.DMA((3 * n,)), pltpu.SemaphoreType.DMA((3 * n,)),
                   *[pltpu.HBM(l.shape, l.dtype) for l in lands], _sds((8, 128), F32)),
        in_specs=[_HBM] * n + [_SEM, _SEM, pl.BlockSpec(memory_space=pl.ANY)],
        out_specs=(_SEM, _SEM, *[_HBM] * n, pl.BlockSpec(memory_space=pltpu.VMEM)),
        input_output_aliases={i: 2 + i for i in range(n)},
        compiler_params=pltpu.CompilerParams(has_side_effects=_EFFECT),
    )(*lands, send_a, recv_a, after)
    return send_a, recv_a, outs[0], outs[1], shards, list(outs[2:2 + n]), outs[-1]


def gather_two_level_wait(forwarded, axes, after, name):
    send_a, recv_a, send_f, recv_f, shards, lands, _ = forwarded
    n = len(shards)

    def body(*refs):
        ins, land = refs[:n], refs[n:2 * n]
        sa, ra, sf, rf = refs[2 * n:2 * n + 4]
        x, y, c, me = _position()
        here = (x, y, c)
        chips = _other_chips(x, y)
        for p in range(n):
            for k in range(4):
                _gather_copy(ins, land, axes, sa, ra, k, p, me, here, from_shard=True).wait_send()
            _gather_copy(ins, land, axes, sa, ra, 0, p, 4 * x + 2 * y + 1 - c, here).wait_recv()
            for j, (px, py) in enumerate(chips):
                _gather_copy(ins, land, axes, sf, rf, j, p, 4 * px + 2 * py + c, here).wait_send()
                _gather_copy(ins, land, axes, sf, rf, j, p, 4 * px + 2 * py + 1 - c, here).wait_recv()

    outs = pl.pallas_call(
        body, name=name,
        out_shape=(*[pltpu.HBM(s.shape, s.dtype) for s in shards], *[pltpu.HBM(l.shape, l.dtype) for l in lands]),
        in_specs=[_HBM] * (2 * n) + [_SEM] * 4 + [pl.BlockSpec(memory_space=pl.ANY)],
        out_specs=tuple([_HBM] * (2 * n)),
        input_output_aliases={i: i for i in range(2 * n)},
        compiler_params=pltpu.CompilerParams(has_side_effects=_EFFECT),
    )(*shards, *lands, send_a, recv_a, send_f, recv_f, after)
    return list(outs[n:])


_HBM = pl.BlockSpec(memory_space=pltpu.HBM)
_SEM = pl.BlockSpec(memory_space=pltpu.SEMAPHORE)
_EFFECT = pltpu.SideEffectType.DATAFLOW_SIDE_EFFECTING


def _in_hbm(a):
    return pltpu.with_memory_space_constraint(a, pltpu.HBM)


def _landing(mode, src, axis):
    me = 4 * lax.axis_index("x") + 2 * lax.axis_index("y") + lax.axis_index("c")
    if mode == "gather":
        shape = list(src.shape)
        size = shape[axis]
        shape[axis] *= N_DEV
        return lax.dynamic_update_slice_in_dim(lax.empty(tuple(shape), src.dtype), src, me * size, axis)
    size = src.shape[axis] // N_DEV
    own = lax.dynamic_slice_in_dim(src, me * size, size, axis)
    return lax.dynamic_update_slice_in_dim(lax.empty((N_DEV, *own.shape), src.dtype), own[None], me, 0)


def _seven(mode, land_ref, axis):
    if mode == "gather":
        return _block(land_ref, axis, (N_DEV - 1) * (land_ref.shape[axis] // N_DEV), 0)
    return land_ref.at[pl.ds(0, N_DEV - 1)]


def exchange_start(mode, srcs, axes, after, name):
    n = len(srcs)
    lands = [_landing(mode, s, ax) for s, ax in zip(srcs, axes)]

    def body(*refs):
        src, land = refs[:n], refs[n:2 * n]
        send_sems, recv_sems = refs[2 * n + 1], refs[2 * n + 2]
        token = refs[-1]
        x, y, c, me = _position()
        for k in range(1, N_DEV):
            peer, pj = _peer(x, y, c, k)
            for p in range(n):
                if mode == "gather":
                    s = src[p]
                    dst = _block(land[p], axes[p], src[p].shape[axes[p]], me)
                else:
                    s = _block(src[p], axes[p], src[p].shape[axes[p]] // N_DEV, pj)
                    dst = land[p].at[me]
                pltpu.make_async_remote_copy(
                    src_ref=s, dst_ref=dst, send_sem=send_sems.at[p], recv_sem=recv_sems.at[p],
                    device_id=peer, device_id_type=MESH_ID).start()
        token[...] = jnp.zeros_like(token)

    outs = pl.pallas_call(
        body, name=name,
        out_shape=(pltpu.SemaphoreType.DMA((n,)), pltpu.SemaphoreType.DMA((n,)),
                   *[pltpu.HBM(s.shape, s.dtype) for s in srcs], *[pltpu.HBM(l.shape, l.dtype) for l in lands],
                   _sds((8, 128), F32)),
        in_specs=[_HBM] * (2 * n) + [pl.BlockSpec(memory_space=pl.ANY)],
        out_specs=(_SEM, _SEM, *[_HBM] * (2 * n), pl.BlockSpec(memory_space=pltpu.VMEM)),
        input_output_aliases={i: 2 + i for i in range(2 * n)},
        compiler_params=pltpu.CompilerParams(has_side_effects=_EFFECT),
    )(*[_in_hbm(s) for s in srcs], *[_in_hbm(l) for l in lands], after)
    return outs[0], outs[1], list(outs[2:2 + n]), list(outs[2 + n:2 + 2 * n]), outs[-1]


def exchange_wait(mode, started, axes, after, name):
    send_sems, recv_sems, srcs, lands, _ = started
    n = len(srcs)

    def body(*refs):
        land = refs[n:2 * n]
        send_ref, recv_ref = refs[2 * n], refs[2 * n + 1]
        x, y, c, _ = _position()
        for p in range(n):
            seven = _seven(mode, land[p], axes[p])
            cp = pltpu.make_async_remote_copy(
                src_ref=seven, dst_ref=seven, send_sem=send_ref.at[p], recv_sem=recv_ref.at[p],
                device_id=(x, y, c), device_id_type=MESH_ID)
            cp.wait_send()
            cp.wait_recv()

    outs = pl.pallas_call(
        body, name=name,
        out_shape=(*[pltpu.HBM(s.shape, s.dtype) for s in srcs], *[pltpu.HBM(l.shape, l.dtype) for l in lands]),
        in_specs=[_HBM] * (2 * n) + [_SEM, _SEM, pl.BlockSpec(memory_space=pl.ANY)],
        out_specs=tuple([_HBM] * (2 * n)),
        input_output_aliases={i: i for i in range(2 * n)},
        compiler_params=pltpu.CompilerParams(has_side_effects=_EFFECT),
    )(*srcs, *lands, send_sems, recv_sems, after)
    return list(outs[n:])


def _adam_math(g, w, m, v):
    m2 = ADAM_B1 * m + (1.0 - ADAM_B1) * g
    v2 = ADAM_B2 * v + (1.0 - ADAM_B2) * (g * g)
    m_hat = m2 / (1.0 - ADAM_B1 ** ADAM_STEP)
    v_hat = v2 / (1.0 - ADAM_B2 ** ADAM_STEP)
    delta = -ADAM_LR * (m_hat / (jnp.sqrt(v_hat) + ADAM_EPS) + ADAM_WD * w)
    return delta, m2, v2


def adamw(g, w, m, v, name, slots, tr=128):
    r, c = w.shape
    tr = min(tr, r)

    def body(g_ref, w_ref, m_ref, v_ref, go_ref, d_ref, mo_ref, vo_ref):
        if slots:
            gs = g_ref[0].astype(F32)
            for s in range(1, N_DEV):
                gs = gs + g_ref[s].astype(F32)
        else:
            gs = g_ref[...]
        go_ref[...] = gs
        d_ref[...], mo_ref[...], vo_ref[...] = _adam_math(gs, w_ref[...], m_ref[...], v_ref[...])

    row = pl.BlockSpec((tr, c), lambda i: (i, 0))
    gspec = pl.BlockSpec((N_DEV, tr, c), lambda i: (0, i, 0)) if slots else row
    return pl.pallas_call(
        body, name=name, grid=(r // tr,),
        in_specs=[gspec, row, row, row], out_specs=[row] * 4, out_shape=[_sds((r, c), F32)] * 4,
        compiler_params=_cp("parallel"),
    )(g, w, m, v)


def sum_slots(g, name):
    _, r, c = g.shape

    def body(g_ref, o_ref):
        gs = g_ref[0]
        for s in range(1, N_DEV):
            gs = gs + g_ref[s]
        o_ref[...] = gs

    return pl.pallas_call(
        body, name=name, grid=(1,),
        in_specs=[pl.BlockSpec((N_DEV, r, c), lambda i: (0, 0, 0))],
        out_specs=pl.BlockSpec((r, c), lambda i: (0, 0)), out_shape=_sds((r, c), F32),
        compiler_params=_cp("arbitrary"),
    )(g)


def _rows128(a, pad_to=8):
    a = a.reshape(-1, 128)
    pad = (-a.shape[0]) % pad_to
    return jnp.pad(a, ((0, pad), (0, 0))) if pad else a


def kernel(x, even_norm, even_w_in, even_pool_w, even_pool_scale, even_ws, even_bs, even_w_out, odd_norm, odd_w_in, odd_conv_w, odd_w_out, final_norm, loss_target, m_even_norm, m_even_w_in, m_even_pool_w, m_even_pool_scale, m_even_ws, m_even_bs, m_even_w_out, m_odd_norm, m_odd_w_in, m_odd_conv_w, m_odd_w_out, m_final_norm, v_even_norm, v_even_w_in, v_even_pool_w, v_even_pool_scale, v_even_ws, v_even_bs, v_even_w_out, v_odd_norm, v_odd_w_in, v_odd_conv_w, v_odd_w_out, v_final_norm):
    x0 = x[0]
    target = loss_target[0]
    me = 4 * lax.axis_index("x") + 2 * lax.axis_index("y") + lax.axis_index("c")

    (we_in,) = all_gather_two_level([even_w_in[0].astype(BF16)], [1], "gather_even_in")
    odd_small = jnp.pad(odd_norm, ((0, 7), (0, 0))) + jnp.pad(odd_conv_w[0], ((1, 4), (0, 0)))
    rest_axes, odd_axes = [1, 0, 1], [1, 0]
    rest_flight = exchange_start("gather", [even_pool_w[0].astype(BF16), even_w_out[0].astype(BF16), odd_small],
                                 rest_axes, we_in, "gather_even_rest_start")
    odd_flight = gather_two_level_start([odd_w_in[0].astype(BF16), odd_w_out[0].astype(BF16)],
                                        odd_axes, rest_flight[4], "gather_odd_start")
    scale = even_pool_scale
    ws = even_ws[0]
    bs_col = even_bs[0][:, :, None]
    g_fin = final_norm[None, :]

    z_e, h_e = norm_matmul(x0, even_norm + odd_flight[4][0:1, 0:1], we_in, "even_in", tn=1280)
    wp, we_out, odd_small = exchange_wait("gather", rest_flight, rest_axes, z_e, "gather_even_rest_wait")
    g_odd, conv_w = odd_small[0:1], odd_small[1:4]
    y_e, pooled = even_mix(z_e, wp, scale, ws, bs_col, "even_mix")
    x1, h_o = matmul_residual(y_e, we_out, x0, "even_out", next_gain=g_odd)
    odd_flight = gather_two_level_forward(odd_flight, odd_axes, x1, "gather_odd_forward")
    h_layouts = [h_o] + [permute_mats(h_o, [], gi, False, f"dilate_h_{gi}")[0] for gi in (1, 2)]
    wo_in, wo_out = gather_two_level_wait(odd_flight, odd_axes, h_layouts[2], "gather_odd_wait")
    z_o = matmul_layouts(h_layouts, wo_in, "odd_in")
    attn = [attn_fwd(z_o, gi, f"attn_fwd_{gi}") for gi in range(3)]
    for gi in (1, 2):
        attn[gi] = permute_mats(attn[gi][0], [attn[gi][1]], gi, True, f"undilate_attn_{gi}")
    y_o, yc, lse_tot, conv = odd_mix([a[0] for a in attn], [a[1] for a in attn], z_o, conv_w, "odd_mix")
    x2 = matmul_residual(y_o, wo_out, x1, "odd_out")
    dx2, loss_blk, dg_fin = final_norm_loss(x2, g_fin, target, "final_norm_loss")

    dy_o, dwo_out = out_proj_bwd(dx2, wo_out.T, y_o, "odd_out_bwd")
    dz5_o, dyc, dd, dconv_w = odd_mix_bwd(dy_o, yc, conv, z_o, conv_w, "odd_mix_bwd")
    dqkv = [attn_bwd(z_o, dyc, lse_tot, dd, 0, "attn_bwd_0")]
    for gi in (1, 2):
        dyc_g, lse_g, dd_g = permute_mats(dyc, [lse_tot, dd], gi, False, f"dilate_dy_{gi}")
        dqkv.append(permute_slabs(attn_bwd(z_o, dyc_g, lse_g, dd_g, gi, f"attn_bwd_{gi}"),
                                  gi, True, f"undilate_dqkv_{gi}"))
    src_o = [(dqkv[0], ("slabs", 0)), (dqkv[1], ("slabs", 1)), (dqkv[2], ("slabs", 2)), (dz5_o, ("cols", 9, 5))]
    dwo_in = in_proj_dw(h_o, src_o, ODD_COLBLOCKS, "odd_in_dw")
    odd_grads = exchange_start("scatter", [dwo_in, dwo_out], [1, 0], dconv_w, "scatter_odd_start")
    dx1, dg_odd = in_proj_dx(src_o, ODD_COLBLOCKS, wo_in.T, x1, g_odd + odd_grads[4][0:1, 0:1], dx2, "odd_in_dx",
                             epilogue_buffers=1)

    dy_e, dwe_out = out_proj_bwd(dx1, we_out.T, y_e, "even_out_bwd")
    dz_e, dwp, dscale, dws, dbs_col = even_mix_bwd(dy_e, z_e, pooled, wp, scale, ws, bs_col, "even_mix_bwd")
    src_e = [(dz_e, ("cols", 0, 5))]
    even_grads_a = exchange_start("scatter", [dwp.astype(BF16), dwe_out], [1, 0], dscale, "scatter_even_rest_start")
    dwe_in = in_proj_dw(h_e, src_e, 5, "even_in_dw")
    even_grads_b = exchange_start("scatter", [dwe_in], [1], even_grads_a[4], "scatter_even_in_start")
    dx0, dg_even = in_proj_dx(src_e, 5, we_in.T, x0, even_norm + even_grads_b[4][0:1, 0:1], dx1, "even_in_dx")
    small = jnp.concatenate([
        _rows128(dg_even), _rows128(dscale), _rows128(dws), _rows128(dbs_col), _rows128(dg_fin),
        _rows128(dg_odd), _rows128(dconv_w)], axis=0)
    n_small = small.shape[0]
    small_flight = exchange_start("gather", [small], [0], dg_fin, "gather_small_start")
    p_wo_in, p_wo_out = exchange_wait("scatter", odd_grads, [1, 0], small_flight[4], "scatter_odd_wait")
    p_wp, p_we_out = exchange_wait("scatter", even_grads_a, [1, 0], small_flight[4], "scatter_even_rest_wait")
    (p_we_in,) = exchange_wait("scatter", even_grads_b, [1], small_flight[4], "scatter_even_in_wait")

    two_d = lambda a, r, c: a.reshape(r, c)
    sharded = [
        ("even_w_in", p_we_in, even_w_in, m_even_w_in, v_even_w_in, (1024, 640)),
        ("even_pool_w", p_wp.reshape(N_DEV, 128, 256), even_pool_w, m_even_pool_w, v_even_pool_w, (128, 256)),
        ("even_w_out", p_we_out, even_w_out, m_even_w_out, v_even_w_out, (256, 1024)),
        ("odd_w_in", p_wo_in, odd_w_in, m_odd_w_in, v_odd_w_in, (1024, 1792)),
        ("odd_w_out", p_wo_out, odd_w_out, m_odd_w_out, v_odd_w_out, (256, 1024)),
    ]
    res = {}
    for name, g, w, m, v, (r, c) in sharded:
        outs = adamw(g, two_d(w, r, c), two_d(m, r, c), two_d(v, r, c), "adamw_" + name, slots=True)
        res[name] = [o.reshape(w.shape) for o in outs]

    (small_all,) = exchange_wait("gather", small_flight, [0], res["odd_w_out"][0], "gather_small_wait")
    small_sum = sum_slots(small_all.reshape(N_DEV, n_small, 128), "sum_small_grads")
    g_even_norm = small_sum[0:8].reshape(1, 1024)
    g_scale = small_sum[8:16].reshape(1, 1024)
    g_ws = small_sum[16:528]
    g_bs = small_sum[528:532]
    g_final = small_sum[536:544].reshape(1, 1024)
    g_odd_norm = lax.dynamic_slice_in_dim(small_sum[544:552], me, 1, axis=0)
    g_conv = lax.dynamic_index_in_dim(small_sum[552:576].reshape(3, 8, 128), me, axis=1, keepdims=False)
    plain = [
        ("even_norm", g_even_norm, even_norm, m_even_norm, v_even_norm, (1, 1024)),
        ("even_pool_scale", g_scale, even_pool_scale, m_even_pool_scale, v_even_pool_scale, (1, 1024)),
        ("even_ws", g_ws, even_ws, m_even_ws, v_even_ws, (512, 128)),
        ("even_bs", g_bs, even_bs, m_even_bs, v_even_bs, (4, 128)),
        ("odd_norm", g_odd_norm, odd_norm, m_odd_norm, v_odd_norm, (1, 128)),
        ("odd_conv_w", g_conv, odd_conv_w, m_odd_conv_w, v_odd_conv_w, (3, 128)),
        ("final_norm", g_final, final_norm, m_final_norm, v_final_norm, (1, 1024)),
    ]
    for name, g, w, m, v, (r, c) in plain:
        outs = adamw(two_d(g, r, c), two_d(w, r, c), two_d(m, r, c), two_d(v, r, c), "adamw_" + name, slots=False)
        res[name] = [o.reshape(w.shape) for o in outs]

    loss = lax.psum(loss_blk[0, 0], ("x", "y", "c"))
    order = ["even_norm", "even_w_in", "even_pool_w", "even_pool_scale", "even_ws", "even_bs", "even_w_out",
             "odd_norm", "odd_w_in", "odd_conv_w", "odd_w_out", "final_norm"]
    return (loss, dx0[None], *[res[n][0] for n in order], *[res[n][1] for n in order],
            *[res[n][2] for n in order], *[res[n][3] for n in order])
```

```python
import functools

import jax
import jax.numpy as jnp
from jax import lax
from jax.experimental import pallas as pl
from jax.experimental.pallas import tpu as pltpu

F32 = jnp.float32
BF16 = jnp.bfloat16
MESH_ID = pl.DeviceIdType.MESH

EPS = 1e-6
NEG = -1e30
N_DEV = 8
POOL_SIZES = (2, 4, 8, 16)
DILATIONS = (1, 4, 16)
N_HEADS = 24
HEADS_PER_GROUP = 8
HEAD_DIM = 128
ATTN_BLOCK = 128
CHUNK = 128
CB = 1024
HALO = 16
ODD_COLBLOCKS = 14
ADAM_LR = 0.001
ADAM_B1 = 0.9
ADAM_B2 = 0.999
ADAM_EPS = 1e-08
ADAM_WD = 0.01
ADAM_STEP = 10
VMEM_LIMIT = 52 * 1024 * 1024


def _cp(*sem):
    return pltpu.CompilerParams(dimension_semantics=sem, vmem_limit_bytes=VMEM_LIMIT)


def _dot(a, b):
    return jnp.dot(a, b, preferred_element_type=F32)


def _dot_nt(a, b):
    return lax.dot_general(a, b, (((1,), (1,)), ((), ())), preferred_element_type=F32)


def _dot_tn(a, b):
    return lax.dot_general(a, b, (((0,), (0,)), ((), ())), preferred_element_type=F32)


def _sigmoid(x):
    return 0.5 * jnp.tanh(0.5 * x) + 0.5


def _silu_and_grad(x):
    s = _sigmoid(x)
    return x * s, s * (1.0 + x * (1.0 - s))


def _sds(shape, dtype):
    return jax.ShapeDtypeStruct(shape, dtype)


def norm_matmul(x, g, w, name, tn, tm=1024, rows=256):
    t, d = x.shape
    n = w.shape[1]

    def body(x_ref, g_ref, w_ref, z_ref, h_ref):
        @pl.when(pl.program_id(1) == 0)
        def _():
            for c in range(tm // rows):
                rs = slice(c * rows, (c + 1) * rows)
                xv = x_ref[rs, :]
                r = lax.rsqrt(jnp.mean(xv * xv, axis=-1, keepdims=True) + EPS)
                h_ref[rs, :] = ((xv * r) * g_ref[...]).astype(BF16)

        z_ref[...] = _dot(h_ref[...], w_ref[...]).astype(BF16)

    return pl.pallas_call(
        body, name=name, grid=(t // tm, n // tn),
        in_specs=[pl.BlockSpec((tm, d), lambda i, j: (i, 0)),
                  pl.BlockSpec((1, d), lambda i, j: (0, 0)),
                  pl.BlockSpec((d, tn), lambda i, j: (0, j))],
        out_specs=[pl.BlockSpec((tm, tn), lambda i, j: (i, j)),
                   pl.BlockSpec((tm, d), lambda i, j: (i, 0))],
        out_shape=[_sds((t, n), BF16), _sds((t, d), BF16)],
        compiler_params=_cp("parallel", "arbitrary"),
    )(x, g, w)


def matmul_residual(y, w, x, name, next_gain=None, tm=1024, rows=256):
    t, k = y.shape
    d = w.shape[1]

    def body(y_ref, w_ref, x_ref, *rest):
        o_ref = rest[-1] if next_gain is None else rest[-2]
        o_ref[...] = x_ref[...] + _dot(y_ref[...], w_ref[...])
        if next_gain is not None:
            g_ref, h_ref = rest[0], rest[-1]
            for c in range(tm // rows):
                rs = slice(c * rows, (c + 1) * rows)
                xv = o_ref[rs, :]
                r = lax.rsqrt(jnp.mean(xv * xv, axis=-1, keepdims=True) + EPS)
                h_ref[rs, :] = ((xv * r) * g_ref[...]).astype(BF16)

    row = pl.BlockSpec((tm, d), lambda i: (i, 0))
    in_specs = [pl.BlockSpec((tm, k), lambda i: (i, 0)), pl.BlockSpec((k, d), lambda i: (0, 0)), row]
    if next_gain is None:
        return pl.pallas_call(
            body, name=name, grid=(t // tm,), in_specs=in_specs, out_specs=row, out_shape=_sds((t, d), F32),
            compiler_params=_cp("parallel"),
        )(y, w, x)
    return pl.pallas_call(
        body, name=name, grid=(t // tm,),
        in_specs=in_specs + [pl.BlockSpec((1, d), lambda i: (0, 0))],
        out_specs=[row, row], out_shape=[_sds((t, d), F32), _sds((t, d), BF16)],
        compiler_params=_cp("parallel"),
    )(y, w, x, next_gain)


def matmul_layouts(hs, w, name, tm=2048):
    t, d = hs[0].shape
    n_blocks = w.shape[1] // CB

    def layout_of(j):
        return jnp.where(j < 9, j // 3, 0)

    def column_of(j):
        return jnp.where(j < 9, (j % 3) * 3 + j // 3, j)

    def body(h0, h1, h2, w_ref, z_ref, wt_ref):
        layout = layout_of(pl.program_id(1))
        for v, h_ref in enumerate((h0, h1, h2)):
            @pl.when(layout == v)
            def _(h_ref=h_ref):
                z_ref[...] = _dot(h_ref[...], w_ref[...]).astype(BF16)
        @pl.when(pl.program_id(0) == 0)
        def _():
            wt_ref[...] = w_ref[...].T

    row = pl.BlockSpec((tm, d), lambda i, j: (i, 0))
    wt_block = lambda i, j: (jnp.where(i == 0, column_of(j), n_blocks), 0)
    return pl.pallas_call(
        body, name=name, grid=(t // tm, n_blocks),
        in_specs=[row, row, row, pl.BlockSpec((d, CB), lambda i, j: (0, column_of(j)))],
        out_specs=[pl.BlockSpec((tm, CB), lambda i, j: (i, column_of(j))), pl.BlockSpec((CB, d), wt_block)],
        out_shape=[_sds((t, n_blocks * CB), BF16), _sds(((n_blocks + 1) * CB, d), BF16)],
        compiler_params=_cp("arbitrary", "arbitrary"),
    )(*hs, w)


def out_proj_bwd(dx, wt, y, name, tm=512):
    t, d = dx.shape
    k = wt.shape[1]
    steps = t // tm
    half = d // 2

    def body(dx_ref, wt_ref, y_ref, dy_ref, dw_ref, acc):
        i = pl.program_id(0)
        dxb = dx_ref[...].astype(BF16)
        dy_ref[...] = _dot(dxb, wt_ref[...]).astype(BF16)

        @pl.when(i == 0)
        def _():
            acc[...] = jnp.zeros_like(acc)

        for c in range(2):
            acc[:, c * half:(c + 1) * half] += _dot_tn(y_ref[...], dxb[:, c * half:(c + 1) * half])

        @pl.when(i == steps - 1)
        def _():
            dw_ref[...] = acc[...].astype(BF16)

    return pl.pallas_call(
        body, name=name, grid=(steps,),
        in_specs=[pl.BlockSpec((tm, d), lambda i: (i, 0)),
                  pl.BlockSpec((d, k), lambda i: (0, 0)),
                  pl.BlockSpec((tm, k), lambda i: (i, 0))],
        out_specs=[pl.BlockSpec((tm, k), lambda i: (i, 0)),
                   pl.BlockSpec((k, d), lambda i: (0, 0))],
        out_shape=[_sds((t, k), BF16), _sds((k, d), BF16)],
        scratch_shapes=[pltpu.VMEM((k, d), F32)],
        compiler_params=_cp("arbitrary"),
    )(dx, wt, y)


def _source_block(kind, rows):
    if kind[0] == "cols":
        return (rows, CB)
    return (None, rows, CB)


def _source_active(kind, j):
    if kind[0] == "cols":
        _, first, n = kind
        return (j >= first) & (j < first + n), jnp.clip(j - first, 0, n - 1)
    _, group = kind
    return (j < 9) & (j % 3 == group), jnp.clip(j // 3, 0, 2)


def _source_index(kind, row_block, inner):
    if kind[0] == "cols":
        return (row_block, inner)
    return (inner, row_block, 0)


def in_proj_dw(h, sources, n_blocks, name, tk=1024):
    t, d = h.shape
    steps = t // tk
    ns = len(sources)

    def body(*refs):
        h_ref, src_refs, dw_ref, acc = refs[0], refs[1:1 + ns], refs[1 + ns], refs[2 + ns]
        j, k = pl.program_id(0), pl.program_id(1)

        @pl.when(k == 0)
        def _():
            acc[...] = jnp.zeros_like(acc)

        for s, (_, kind) in enumerate(sources):
            active, _ = _source_active(kind, j)

            @pl.when(active)
            def _(s=s):
                acc[...] += _dot_tn(h_ref[...], src_refs[s][...])

        @pl.when(k == steps - 1)
        def _():
            dw_ref[...] = acc[...].astype(BF16)

    def src_spec(kind):
        def index(j, k):
            active, inner = _source_active(kind, j)
            return _source_index(kind, jnp.where(active, k, 0), inner)
        return pl.BlockSpec(_source_block(kind, tk), index)

    return pl.pallas_call(
        body, name=name, grid=(n_blocks, steps),
        in_specs=[pl.BlockSpec((tk, d), lambda j, k: (k, 0))] + [src_spec(kind) for _, kind in sources],
        out_specs=pl.BlockSpec((d, CB), lambda j, k: (0, j)),
        out_shape=_sds((d, n_blocks * CB), BF16),
        scratch_shapes=[pltpu.VMEM((d, CB), F32)],
        compiler_params=_cp("parallel", "arbitrary"),
    )(h, *[a for a, _ in sources])


def in_proj_dx(sources, n_blocks, wt, x, g, dres, name, tm=1024, rows=256, epilogue_buffers=2):
    t, d = x.shape
    ns = len(sources)

    def body(*refs):
        src_refs = refs[:ns]
        w_ref, x_ref, g_ref, dres_ref, dx_ref, dg_ref, acc = refs[ns:]
        i, p = pl.program_id(0), pl.program_id(1)

        @pl.when(p == 0)
        def _():
            acc[...] = jnp.zeros_like(acc)

        for s, (_, kind) in enumerate(sources):
            active, _ = _source_active(kind, p)

            @pl.when(active)
            def _(s=s):
                acc[...] += _dot(src_refs[s][...], w_ref[...])

        @pl.when(p == n_blocks - 1)
        def _():
            part = jnp.zeros((1, d), F32)
            for c in range(tm // rows):
                rs = slice(c * rows, (c + 1) * rows)
                xv = x_ref[rs, :]
                r = lax.rsqrt(jnp.mean(xv * xv, axis=-1, keepdims=True) + EPS)
                xn = xv * r
                dh = acc[rs, :]
                dhg = dh * g_ref[...]
                dx_ref[rs, :] = dres_ref[rs, :] + r * (dhg - xn * jnp.mean(dhg * xn, axis=-1, keepdims=True))
                part = part + jnp.sum(dh * xn, axis=0, keepdims=True)

            @pl.when(i == 0)
            def _():
                dg_ref[...] = part

            @pl.when(i > 0)
            def _():
                dg_ref[...] += part

    def src_spec(kind):
        def index(i, p):
            if kind[0] == "cols":
                _, first, n = kind
                inner = jnp.clip(p - first, 0, n - 1)
            else:
                _, group = kind
                inner = jnp.clip((p - group + 2) // 3, 0, 2)
            return _source_index(kind, i, inner)
        return pl.BlockSpec(_source_block(kind, tm), index)

    return pl.pallas_call(
        body, name=name, grid=(t // tm, n_blocks),
        in_specs=[src_spec(kind) for _, kind in sources] + [
            pl.BlockSpec((CB, d), lambda i, p: (p, 0)),
            pl.BlockSpec((tm, d), lambda i, p: (i, 0), pipeline_mode=pl.Buffered(epilogue_buffers)),
            pl.BlockSpec((1, d), lambda i, p: (0, 0)),
            pl.BlockSpec((tm, d), lambda i, p: (i, 0), pipeline_mode=pl.Buffered(epilogue_buffers))],
        out_specs=[pl.BlockSpec((tm, d), lambda i, p: (i, 0)),
                   pl.BlockSpec((1, d), lambda i, p: (0, 0))],
        out_shape=[_sds((t, d), F32), _sds((1, d), F32)],
        scratch_shapes=[pltpu.VMEM((tm, d), F32)],
        compiler_params=_cp("arbitrary", "arbitrary"),
    )(*[a for a, _ in sources], wt, x, g, dres)


def _window_counts(first_row, rows, w):
    t = first_row + lax.broadcasted_iota(jnp.int32, (rows, 1), 0)
    return jnp.minimum(t + 1, w).astype(F32)


def _tril_bf16(ws_ref, g):
    r = lax.broadcasted_iota(jnp.int32, (CHUNK, CHUNK), 0)
    c = lax.broadcasted_iota(jnp.int32, (CHUNK, CHUNK), 1)
    return jnp.where(r >= c, ws_ref[g], 0.0).astype(BF16), r >= c


def even_mix(z, wp, scale, ws, bs_col, name, tm=256):
    t = z.shape[0]
    gd = CB // len(POOL_SIZES)

    def body(a_ref, ga_ref, u_ref, v_ref, gb_ref, wp_ref, sc_ref, ws_ref, bs_ref, y_ref, pooled_ref, aext):
        i = pl.program_id(0)

        @pl.when(i == 0)
        def _():
            aext[0:HALO, :] = jnp.zeros((HALO, CB), F32)

        @pl.when(i > 0)
        def _():
            aext[0:HALO, :] = aext[tm:tm + HALO, :]

        aext[HALO:HALO + tm, :] = a_ref[...].astype(F32)
        for g, w in enumerate(POOL_SIZES):
            cols = slice(g * gd, (g + 1) * gd)
            tok = aext[HALO:HALO + tm, cols]
            s = tok
            for k in range(1, w):
                s = s + aext[HALO - k:HALO - k + tm, cols]
            pooled = (s / _window_counts(i * tm, tm, w) - tok).astype(BF16)
            pooled_ref[:, cols] = pooled
            mixed = _dot(pooled, wp_ref[g])
            silu_a, _ = _silu_and_grad(ga_ref[:, cols].astype(F32))
            y_ref[:, cols] = (mixed * sc_ref[:, cols] * silu_a).astype(BF16)
        for g in range(4):
            cols = slice(g * gd, (g + 1) * gd)
            wg, _ = _tril_bf16(ws_ref, g)
            for c in range(tm // CHUNK):
                rows = slice(c * CHUNK, (c + 1) * CHUNK)
                m = _dot(wg, v_ref[rows, cols]) + bs_ref[g]
                silu_b, _ = _silu_and_grad(gb_ref[rows, cols].astype(F32))
                y_ref[rows, CB + g * gd:CB + (g + 1) * gd] = (
                    u_ref[rows, cols].astype(F32) * m * silu_b).astype(BF16)

    zspec = lambda cb: pl.BlockSpec((tm, CB), lambda i, cb=cb: (i, cb))
    full = lambda shape: pl.BlockSpec(shape, lambda i: (0,) * len(shape))
    return pl.pallas_call(
        body, name=name, grid=(t // tm,),
        in_specs=[zspec(0), zspec(1), zspec(2), zspec(3), zspec(4),
                  full(wp.shape), full(scale.shape), full(ws.shape), full(bs_col.shape)],
        out_specs=[pl.BlockSpec((tm, 2 * CB), lambda i: (i, 0)), pl.BlockSpec((tm, CB), lambda i: (i, 0))],
        out_shape=[_sds((t, 2 * CB), BF16), _sds((t, CB), BF16)],
        scratch_shapes=[pltpu.VMEM((HALO + tm, CB), F32)],
        compiler_params=_cp("arbitrary"),
    )(z, z, z, z, z, wp, scale, ws, bs_col)


def even_mix_bwd(dy, z, pooled, wp, scale, ws, bs_col, name, tm=256):
    t = z.shape[0]
    nt = t // tm
    gd = CB // len(POOL_SIZES)

    def body(dya_ref, dyb_ref, ga_ref, u_ref, v_ref, gb_ref, pooled_ref, wp_ref, sc_ref, ws_ref, bs_ref,
             dz_ref, dwp_ref, dsc_ref, dws_ref, dbs_ref, dpext):
        i = pl.program_id(0)
        tile = nt - 1 - i

        @pl.when(i == 0)
        def _():
            dpext[tm:tm + HALO, :] = jnp.zeros((HALO, CB), F32)
            dwp_ref[...] = jnp.zeros_like(dwp_ref)
            dsc_ref[...] = jnp.zeros_like(dsc_ref)
            dws_ref[...] = jnp.zeros_like(dws_ref)
            dbs_ref[...] = jnp.zeros_like(dbs_ref)

        @pl.when(i > 0)
        def _():
            dpext[tm:tm + HALO, :] = dpext[0:HALO, :]

        for g, w in enumerate(POOL_SIZES):
            cols = slice(g * gd, (g + 1) * gd)
            pooled_g = pooled_ref[:, cols]
            mixed = _dot(pooled_g, wp_ref[g])
            silu_a, dsilu_a = _silu_and_grad(ga_ref[:, cols].astype(F32))
            dya = dya_ref[:, cols].astype(F32)
            sc = sc_ref[:, cols]
            dmixed = (dya * sc * silu_a).astype(BF16)
            dsc_ref[:, cols] += jnp.sum(dya * mixed * silu_a, axis=0, keepdims=True)
            dz_ref[:, CB + g * gd:CB + (g + 1) * gd] = (dya * mixed * sc * dsilu_a).astype(BF16)
            dwp_ref[g] += _dot_tn(pooled_g, dmixed)
            dpooled = _dot_nt(dmixed, wp_ref[g])
            dpext[0:tm, cols] = dpooled / _window_counts(tile * tm, tm, w)
            s = dpext[0:tm, cols]
            for k in range(1, w):
                s = s + dpext[k:k + tm, cols]
            dz_ref[:, cols] = (s - dpooled).astype(BF16)
        for g in range(4):
            cols = slice(g * gd, (g + 1) * gd)
            wg, lower = _tril_bf16(ws_ref, g)
            dws_g = jnp.zeros((CHUNK, CHUNK), F32)
            dbs_g = jnp.zeros((CHUNK, 1), F32)
            for c in range(tm // CHUNK):
                rows = slice(c * CHUNK, (c + 1) * CHUNK)
                vb = v_ref[rows, cols]
                m = _dot(wg, vb) + bs_ref[g]
                gbv = gb_ref[rows, cols].astype(F32)
                silu_b, dsilu_b = _silu_and_grad(gbv)
                dyb = dyb_ref[rows, cols].astype(F32)
                uv = u_ref[rows, cols].astype(F32)
                dm = dyb * silu_b * uv
                dmb = dm.astype(BF16)
                dz_ref[rows, 2 * CB + g * gd:2 * CB + (g + 1) * gd] = (dyb * silu_b * m).astype(BF16)
                dz_ref[rows, 3 * CB + g * gd:3 * CB + (g + 1) * gd] = _dot_tn(wg, dmb).astype(BF16)
                dz_ref[rows, 4 * CB + g * gd:4 * CB + (g + 1) * gd] = (dyb * uv * m * dsilu_b).astype(BF16)
                dws_g = dws_g + _dot_nt(dmb, vb)
                dbs_g = dbs_g + jnp.sum(dm, axis=1, keepdims=True)
            dws_ref[g] += jnp.where(lower, dws_g, 0.0)
            dbs_ref[g] += dbs_g

    rev = lambda cb: pl.BlockSpec((tm, CB), lambda i, cb=cb: (nt - 1 - i, cb))
    full = lambda shape: pl.BlockSpec(shape, lambda i: (0,) * len(shape))
    return pl.pallas_call(
        body, name=name, grid=(nt,),
        in_specs=[rev(0), rev(1), rev(1), rev(2), rev(3), rev(4), rev(0),
                  full(wp.shape), full(scale.shape), full(ws.shape), full(bs_col.shape)],
        out_specs=[pl.BlockSpec((tm, 5 * CB), lambda i: (nt - 1 - i, 0)),
                   full(wp.shape), full(scale.shape), full(ws.shape), full(bs_col.shape)],
        out_shape=[_sds((t, 5 * CB), BF16), _sds(wp.shape, F32), _sds(scale.shape, F32),
                   _sds(ws.shape, F32), _sds(bs_col.shape, F32)],
        scratch_shapes=[pltpu.VMEM((tm + HALO, CB), F32)],
        compiler_params=_cp("arbitrary"),
    )(dy, dy, z, z, z, z, pooled, wp, scale, ws, bs_col)


def _slope(group, head):
    return float(2.0 ** (-8.0 * (group * HEADS_PER_GROUP + head + 1) / N_HEADS))


def _band(dilation):
    qi = lax.broadcasted_iota(jnp.int32, (ATTN_BLOCK, ATTN_BLOCK), 0)
    ki = lax.broadcasted_iota(jnp.int32, (ATTN_BLOCK, ATTN_BLOCK), 1)
    dist_prev = ((qi + ATTN_BLOCK - ki) * dilation).astype(F32)
    dist_cur = ((qi - ki) * dilation).astype(F32)
    return dist_prev, ki >= qi, dist_cur, ki <= qi


def _permute_rows(name, d, inverse, arrays, in_specs, out_shapes, out_specs, n_chunks, widths):
    rows = ATTN_BLOCK * d
    t = out_shapes[0].shape[-2]
    n = len(arrays)

    lanes = HEAD_DIM

    def body(*refs):
        for a, o, s, w in zip(refs[:n], refs[n:2 * n], refs[2 * n:], widths):
            for k in range(w // lanes):
                cols = slice(k * lanes, (k + 1) * lanes)
                if inverse:
                    for r in range(d):
                        s[k, pl.ds(r, ATTN_BLOCK, stride=d), :] = (
                            a[r * ATTN_BLOCK:(r + 1) * ATTN_BLOCK, cols].astype(F32))
                    o[:, cols] = s[k].astype(o.dtype)
                else:
                    s[k] = a[:, cols].astype(F32)
                    for r in range(d):
                        o[r * ATTN_BLOCK:(r + 1) * ATTN_BLOCK, cols] = (
                            s[k, pl.ds(r, ATTN_BLOCK, stride=d), :].astype(o.dtype))

    return pl.pallas_call(
        body, name=name, grid=(t // rows, n_chunks),
        in_specs=in_specs, out_specs=out_specs, out_shape=out_shapes,
        scratch_shapes=[pltpu.VMEM((w // lanes, rows, lanes), F32) for w in widths],
        compiler_params=_cp("parallel", "arbitrary"),
    )(*arrays)


def _chunk_width(d):
    return 4 * CB // d


def permute_mats(wide, narrow, group, inverse, name):
    d = DILATIONS[group]
    rows, cw = ATTN_BLOCK * d, _chunk_width(d)
    wide_spec = pl.BlockSpec((rows, cw), lambda i, c: (i, c))
    narrow_spec = pl.BlockSpec((rows, HEAD_DIM), lambda i, c: (i, 0))
    arrays = [wide] + list(narrow)
    specs = [wide_spec] + [narrow_spec] * len(narrow)
    return _permute_rows(name, d, inverse, arrays, specs, [_sds(a.shape, a.dtype) for a in arrays], specs,
                         CB // cw, [cw] + [HEAD_DIM] * len(narrow))


def permute_slabs(s, group, inverse, name):
    d = DILATIONS[group]
    rows, cw = ATTN_BLOCK * d, _chunk_width(d)
    per = CB // cw
    spec = pl.BlockSpec((None, rows, cw), lambda i, c: (c // per, i, c % per))
    (out,) = _permute_rows(name, d, inverse, [s], [spec], [_sds(s.shape, s.dtype)], [spec], 3 * per, [cw])
    return out


def _lane_pack(cols):
    rows = cols[0].shape[0]
    lane = lax.broadcasted_iota(jnp.int32, (rows, HEAD_DIM), 1)
    out = jnp.zeros((rows, HEAD_DIM), F32)
    for h, c in enumerate(cols):
        out = jnp.where(lane == h, c, out)
    return out


def _qkv_col(group, from_z):
    return (lambda which: which * 3 + group) if from_z else (lambda which: which)


SCORE_SCALE = HEAD_DIM ** -0.5
LOG2_E = 1.4426950408889634
EXP2_SCALE = SCORE_SCALE * LOG2_E


def _fill_bias(bias_ref, group, d):
    dist_p, ok_p, dist_c, ok_c = _band(d)
    for h in range(HEADS_PER_GROUP):
        k = -_slope(group, h) / SCORE_SCALE
        bias_ref[2 * h] = jnp.where(ok_p, k * dist_p, NEG)
        bias_ref[2 * h + 1] = jnp.where(ok_c, k * dist_c, NEG)


def _raw_scores(q, kp, kc, bias_ref, h, has_prev):
    tp = jnp.where(has_prev, _dot_nt(q, kp) + bias_ref[2 * h], NEG)
    tc = _dot_nt(q, kc) + bias_ref[2 * h + 1]
    return tp, tc


def attn_fwd(qkv, group, name):
    t = qkv.shape[0]
    d = DILATIONS[group]
    col = _qkv_col(group, qkv.shape[1] != 3 * CB)
    heads = range(HEADS_PER_GROUP)

    def body(q_ref, kp_ref, kc_ref, vp_ref, vc_ref, o_ref, lse_ref, bias_ref):
        b = pl.program_id(0)

        @pl.when(b == 0)
        def _():
            _fill_bias(bias_ref, group, d)

        hs = [slice(h * HEAD_DIM, (h + 1) * HEAD_DIM) for h in heads]
        s = [_raw_scores(q_ref[:, hs[h]], kp_ref[:, hs[h]], kc_ref[:, hs[h]], bias_ref, h, b >= d) for h in heads]
        m = [jnp.maximum(jnp.max(tp, axis=1, keepdims=True), jnp.max(tc, axis=1, keepdims=True)) for tp, tc in s]
        e = [(jnp.exp2(EXP2_SCALE * (tp - m[h])), jnp.exp2(EXP2_SCALE * (tc - m[h]))) for h, (tp, tc) in enumerate(s)]
        l = [jnp.sum(ep, axis=1, keepdims=True) + jnp.sum(ec, axis=1, keepdims=True) for ep, ec in e]
        o = [(_dot(ep.astype(BF16), vp_ref[:, hs[h]]) + _dot(ec.astype(BF16), vc_ref[:, hs[h]])) * (1.0 / l[h])
             for h, (ep, ec) in enumerate(e)]
        for h in heads:
            o_ref[:, hs[h]] = o[h].astype(BF16)
        lse_ref[...] = _lane_pack([SCORE_SCALE * m[h] + jnp.log(l[h]) for h in heads])

    blk = (ATTN_BLOCK, CB)
    cur = lambda which: pl.BlockSpec(blk, lambda b: (b, col(which)))
    prev = lambda which: pl.BlockSpec(blk, lambda b: (jnp.maximum(b - d, 0), col(which)))
    return pl.pallas_call(
        body, name=name, grid=(t // ATTN_BLOCK,),
        in_specs=[cur(0), prev(1), cur(1), prev(2), cur(2)],
        out_specs=[pl.BlockSpec(blk, lambda b: (b, 0)), pl.BlockSpec((ATTN_BLOCK, HEAD_DIM), lambda b: (b, 0))],
        out_shape=[_sds((t, CB), BF16), _sds((t, HEAD_DIM), F32)],
        scratch_shapes=[pltpu.VMEM((2 * HEADS_PER_GROUP, ATTN_BLOCK, ATTN_BLOCK), F32)],
        compiler_params=_cp("arbitrary"),
    )(qkv, qkv, qkv, qkv, qkv)


def attn_bwd(z, dyc, lse, dd, group, name):
    t = z.shape[0]
    d = DILATIONS[group]
    nb = t // (ATTN_BLOCK * d)
    col = _qkv_col(group, z.shape[1] != 3 * CB)
    heads = range(HEADS_PER_GROUP)

    def body(q_ref, kp_ref, kc_ref, vp_ref, vc_ref, dy_ref, lse_ref, dd_ref, out_ref, carry, bias_ref):
        r, n = pl.program_id(0), pl.program_id(1)

        @pl.when(jnp.logical_and(r == 0, n == 0))
        def _():
            _fill_bias(bias_ref, group, d)

        @pl.when(n == 0)
        def _():
            carry[...] = jnp.zeros_like(carry)

        @pl.when(n == nb)
        def _():
            out_ref[...] = carry[...].astype(BF16)

        @pl.when(n < nb)
        def _():
            hs = [slice(h * HEAD_DIM, (h + 1) * HEAD_DIM) for h in heads]
            q = [q_ref[:, hs[h]] for h in heads]
            kp = [kp_ref[:, hs[h]] for h in heads]
            kc = [kc_ref[:, hs[h]] for h in heads]
            dy = [dy_ref[:, hs[h]] for h in heads]
            old = [[carry[w, :, hs[h]] for h in heads] for w in range(3)]
            s = [_raw_scores(q[h], kp[h], kc[h], bias_ref, h, n > 0) for h in heads]
            l2 = [lse_ref[:, h:h + 1] * LOG2_E for h in heads]
            p = [(jnp.exp2(EXP2_SCALE * tp - l2[h]), jnp.exp2(EXP2_SCALE * tc - l2[h])) for h, (tp, tc) in enumerate(s)]
            ds = [((pp * (_dot_nt(dy[h], vp_ref[:, hs[h]]) - dd_ref[:, h:h + 1]) * SCORE_SCALE).astype(BF16),
                   (pc * (_dot_nt(dy[h], vc_ref[:, hs[h]]) - dd_ref[:, h:h + 1]) * SCORE_SCALE).astype(BF16))
                  for h, (pp, pc) in enumerate(p)]
            for h in heads:
                dsp, dsc = ds[h]
                pp, pc = p[h]
                out_ref[0, :, hs[h]] = old[0][h].astype(BF16)
                out_ref[1, :, hs[h]] = (old[1][h] + _dot_tn(dsp, q[h])).astype(BF16)
                out_ref[2, :, hs[h]] = (old[2][h] + _dot_tn(pp.astype(BF16), dy[h])).astype(BF16)
                carry[0, :, hs[h]] = _dot(dsp, kp[h]) + _dot(dsc, kc[h])
                carry[1, :, hs[h]] = _dot_tn(dsc, q[h])
                carry[2, :, hs[h]] = _dot_tn(pc.astype(BF16), dy[h])

    blk = (ATTN_BLOCK, CB)
    cur_row = lambda r, n: jnp.minimum(n, nb - 1) * d + r
    prev_row = lambda r, n: jnp.clip(n - 1, 0, nb - 1) * d + r
    cur = lambda which: pl.BlockSpec(blk, lambda r, n: (cur_row(r, n), col(which)))
    prev = lambda which: pl.BlockSpec(blk, lambda r, n: (prev_row(r, n), col(which)))
    narrow = pl.BlockSpec((ATTN_BLOCK, HEAD_DIM), lambda r, n: (cur_row(r, n), 0))
    return pl.pallas_call(
        body, name=name, grid=(d, nb + 1),
        in_specs=[cur(0), prev(1), cur(1), prev(2), cur(2),
                  pl.BlockSpec(blk, lambda r, n: (cur_row(r, n), 0)), narrow, narrow],
        out_specs=pl.BlockSpec((3, ATTN_BLOCK, CB), lambda r, n: (0, prev_row(r, n), 0)),
        out_shape=_sds((3, t, CB), BF16),
        scratch_shapes=[pltpu.VMEM((3, ATTN_BLOCK, CB), F32),
                        pltpu.VMEM((2 * HEADS_PER_GROUP, ATTN_BLOCK, ATTN_BLOCK), F32)],
        compiler_params=_cp("arbitrary", "arbitrary"),
    )(z, z, z, z, z, dyc, lse, dd)


def odd_mix(os_, lses, z, conv_w, name, tm=256):
    t = z.shape[0]

    def body(o0, o1, o2, l0, l1, l2, gc_ref, db_ref, dc_ref, dx_ref, gd_ref, cw_ref,
             y_ref, yc_ref, lse_ref, conv_ref, zext):
        i = pl.program_id(0)
        a0, a1, a2 = l0[...], l1[...], l2[...]
        m = jnp.maximum(jnp.maximum(a0, a1), a2)
        tot = m + jnp.log(jnp.exp(a0 - m) + jnp.exp(a1 - m) + jnp.exp(a2 - m))
        lse_ref[...] = tot
        w0, w1, w2 = jnp.exp(a0 - tot), jnp.exp(a1 - tot), jnp.exp(a2 - tot)
        for h in range(HEADS_PER_GROUP):
            hs = slice(h * HEAD_DIM, (h + 1) * HEAD_DIM)
            yc = (w0[:, h:h + 1] * o0[:, hs].astype(F32) + w1[:, h:h + 1] * o1[:, hs].astype(F32)
                  + w2[:, h:h + 1] * o2[:, hs].astype(F32))
            yc_ref[:, hs] = yc.astype(BF16)
            silu_c, _ = _silu_and_grad(gc_ref[:, hs].astype(F32))
            y_ref[:, hs] = (yc * silu_c).astype(BF16)

        @pl.when(i == 0)
        def _():
            zext[0:HALO, :] = jnp.zeros((HALO, CB), F32)

        @pl.when(i > 0)
        def _():
            zext[0:HALO, :] = zext[tm:tm + HALO, :]

        zext[HALO:HALO + tm, :] = dc_ref[...].astype(F32) * dx_ref[...].astype(F32)
        conv = (cw_ref[0:1, :] * zext[HALO - 2:HALO - 2 + tm, :] + cw_ref[1:2, :] * zext[HALO - 1:HALO - 1 + tm, :]
                + cw_ref[2:3, :] * zext[HALO:HALO + tm, :])
        conv_ref[...] = conv.astype(BF16)
        silu_d, _ = _silu_and_grad(gd_ref[...].astype(F32))
        y_ref[:, CB:2 * CB] = (db_ref[...].astype(F32) * conv * silu_d).astype(BF16)

    row = pl.BlockSpec((tm, CB), lambda i: (i, 0))
    narrow = pl.BlockSpec((tm, HEAD_DIM), lambda i: (i, 0))
    zspec = lambda cb: pl.BlockSpec((tm, CB), lambda i, cb=cb: (i, cb))
    return pl.pallas_call(
        body, name=name, grid=(t // tm,),
        in_specs=[row] * 3 + [narrow] * 3 + [zspec(9), zspec(10), zspec(11), zspec(12), zspec(13),
                                             pl.BlockSpec(conv_w.shape, lambda i: (0, 0))],
        out_specs=[pl.BlockSpec((tm, 2 * CB), lambda i: (i, 0)), row, narrow, row],
        out_shape=[_sds((t, 2 * CB), BF16), _sds((t, CB), BF16), _sds((t, HEAD_DIM), F32), _sds((t, CB), BF16)],
        scratch_shapes=[pltpu.VMEM((HALO + tm, CB), F32)],
        compiler_params=_cp("arbitrary"),
    )(*os_, *lses, z, z, z, z, z, conv_w)


def odd_mix_bwd(dy, yc, conv, z, conv_w, name, tm=256):
    t = z.shape[0]
    nt = t // tm

    def body(dyc_ref, dyd_ref, yc_ref, conv_ref, gc_ref, db_ref, dc_ref, dx_ref, gd_ref, cw_ref,
             dz_ref, dyo_ref, dd_ref, dcw_ref, dcext):
        i = pl.program_id(0)

        @pl.when(i == 0)
        def _():
            dcext[tm:tm + HALO, :] = jnp.zeros((HALO, CB), F32)
            dcw_ref[...] = jnp.zeros_like(dcw_ref)

        @pl.when(i > 0)
        def _():
            dcext[tm:tm + HALO, :] = dcext[0:HALO, :]

        silu_c, dsilu_c = _silu_and_grad(gc_ref[...].astype(F32))
        dyc = dyc_ref[...].astype(F32)
        ycv = yc_ref[...].astype(F32)
        dyo = dyc * silu_c
        dyo_ref[...] = dyo.astype(BF16)
        dz_ref[:, 0:CB] = (dyc * ycv * dsilu_c).astype(BF16)
        prod = dyo * ycv
        dd_ref[...] = _lane_pack([jnp.sum(prod[:, h * HEAD_DIM:(h + 1) * HEAD_DIM], axis=1, keepdims=True)
                                  for h in range(HEADS_PER_GROUP)])

        silu_d, dsilu_d = _silu_and_grad(gd_ref[...].astype(F32))
        dyd = dyd_ref[...].astype(F32)
        convv = conv_ref[...].astype(F32)
        dbv = db_ref[...].astype(F32)
        dz_ref[:, CB:2 * CB] = (dyd * convv * silu_d).astype(BF16)
        dz_ref[:, 4 * CB:5 * CB] = (dyd * dbv * convv * dsilu_d).astype(BF16)
        dcext[0:tm, :] = dyd * dbv * silu_d
        dcv, dxv = dc_ref[...].astype(F32), dx_ref[...].astype(F32)
        zc = dcv * dxv
        d0, d1, d2 = dcext[0:tm, :], dcext[1:1 + tm, :], dcext[2:2 + tm, :]
        dzc = cw_ref[2:3, :] * d0 + cw_ref[1:2, :] * d1 + cw_ref[0:1, :] * d2
        dz_ref[:, 2 * CB:3 * CB] = (dzc * dxv).astype(BF16)
        dz_ref[:, 3 * CB:4 * CB] = (dzc * dcv).astype(BF16)
        dcw_ref[0:1, :] += jnp.sum(zc * d2, axis=0, keepdims=True)
        dcw_ref[1:2, :] += jnp.sum(zc * d1, axis=0, keepdims=True)
        dcw_ref[2:3, :] += jnp.sum(zc * d0, axis=0, keepdims=True)

    rev = lambda cb: pl.BlockSpec((tm, CB), lambda i, cb=cb: (nt - 1 - i, cb))
    return pl.pallas_call(
        body, name=name, grid=(nt,),
        in_specs=[rev(0), rev(1), rev(0), rev(0), rev(9), rev(10), rev(11), rev(12), rev(13),
                  pl.BlockSpec(conv_w.shape, lambda i: (0, 0))],
        out_specs=[pl.BlockSpec((tm, 5 * CB), lambda i: (nt - 1 - i, 0)), rev(0),
                   pl.BlockSpec((tm, HEAD_DIM), lambda i: (nt - 1 - i, 0)),
                   pl.BlockSpec(conv_w.shape, lambda i: (0, 0))],
        out_shape=[_sds((t, 5 * CB), BF16), _sds((t, CB), BF16), _sds((t, HEAD_DIM), F32), _sds(conv_w.shape, F32)],
        scratch_shapes=[pltpu.VMEM((tm + HALO, CB), F32)],
        compiler_params=_cp("arbitrary"),
    )(dy, dy, yc, conv, z, z, z, z, z, conv_w)


def out_proj_final_loss(y, w, x, g, target, name, tm=1024, rows=256):
    t, d = x.shape
    k = y.shape[1]

    def body(y_ref, w_ref, x_ref, g_ref, t_ref, dx_ref, loss_ref, dg_ref):
        i = pl.program_id(0)
        dx_ref[...] = x_ref[...] + _dot(y_ref[...], w_ref[...])
        gv = g_ref[...]
        loss = jnp.zeros((1, 1), F32)
        dg = jnp.zeros((1, d), F32)
        for c in range(tm // rows):
            rs = slice(c * rows, (c + 1) * rows)
            xv = dx_ref[rs, :]
            r = lax.rsqrt(jnp.mean(xv * xv, axis=-1, keepdims=True) + EPS)
            xn = xv * r
            err = xn * gv - t_ref[rs, :]
            loss = loss + 0.5 * jnp.sum(jnp.mean(err * err, axis=-1, keepdims=True), axis=0, keepdims=True)
            dy = err * (1.0 / d)
            dyg = dy * gv
            dx_ref[rs, :] = r * (dyg - xn * jnp.mean(dyg * xn, axis=-1, keepdims=True))
            dg = dg + jnp.sum(dy * xn, axis=0, keepdims=True)

        @pl.when(i == 0)
        def _():
            loss_ref[...] = jnp.broadcast_to(loss, loss_ref.shape)
            dg_ref[...] = dg

        @pl.when(i > 0)
        def _():
            loss_ref[...] += jnp.broadcast_to(loss, loss_ref.shape)
            dg_ref[...] += dg

    row = pl.BlockSpec((tm, d), lambda i: (i, 0))
    return pl.pallas_call(
        body, name=name, grid=(t // tm,),
        in_specs=[pl.BlockSpec((tm, k), lambda i: (i, 0)), pl.BlockSpec((k, d), lambda i: (0, 0)), row,
                  pl.BlockSpec((1, d), lambda i: (0, 0)), row],
        out_specs=[row, pl.BlockSpec((8, 128), lambda i: (0, 0)), pl.BlockSpec((1, d), lambda i: (0, 0))],
        out_shape=[_sds((t, d), F32), _sds((8, 128), F32), _sds((1, d), F32)],
        compiler_params=_cp("arbitrary"),
    )(y, w, x, g, target)


def _position():
    x, y, c = lax.axis_index("x"), lax.axis_index("y"), lax.axis_index("c")
    return x, y, c, 4 * x + 2 * y + c


def _peer(x, y, c, k):
    px = 1 - x if k & 4 else x
    py = 1 - y if k & 2 else y
    pc = 1 - c if k & 1 else c
    return (px, py, pc), 4 * px + 2 * py + pc


def _block(ref, axis, size, idx):
    index = [slice(None)] * len(ref.shape)
    index[axis] = pl.ds(idx * size, size)
    return ref.at[tuple(index)]


def all_gather(shards, axes, name):
    n = len(shards)
    fulls = []
    for s, ax in zip(shards, axes):
        shape = list(s.shape)
        shape[ax] *= N_DEV
        fulls.append(_sds(tuple(shape), s.dtype))

    def body(*refs):
        ins, outs = refs[:n], refs[n:2 * n]
        send_sems, recv_sems, local_sems = refs[2 * n:]
        x, y, c, me = _position()
        local = []
        for p in range(n):
            size = ins[p].shape[axes[p]]
            cp = pltpu.make_async_copy(ins[p], _block(outs[p], axes[p], size, me), local_sems.at[p])
            cp.start()
            local.append(cp)
        for k in range(1, N_DEV):
            peer, _ = _peer(x, y, c, k)
            for p in range(n):
                size = ins[p].shape[axes[p]]
                pltpu.make_async_remote_copy(
                    src_ref=ins[p], dst_ref=_block(outs[p], axes[p], size, me),
                    send_sem=send_sems.at[p], recv_sem=recv_sems.at[p],
                    device_id=peer, device_id_type=MESH_ID).start()
        for p in range(n):
            size = ins[p].shape[axes[p]]
            seven = _block(outs[p], axes[p], (N_DEV - 1) * size, 0)
            pltpu.make_async_remote_copy(
                src_ref=seven, dst_ref=seven, send_sem=send_sems.at[p], recv_sem=recv_sems.at[p],
                device_id=(x, y, c), device_id_type=MESH_ID).wait()
            local[p].wait()

    any_spec = pl.BlockSpec(memory_space=pl.ANY)
    return pl.pallas_call(
        body, name=name,
        in_specs=[any_spec] * n, out_specs=[any_spec] * n, out_shape=fulls,
        scratch_shapes=[pltpu.SemaphoreType.DMA((n,)), pltpu.SemaphoreType.DMA((n,)), pltpu.SemaphoreType.DMA((n,))],
    )(*shards)


def _other_chips(x, y):
    return [(1 - x, y), (x, 1 - y), (1 - x, 1 - y)]


def _gather_copy(ins, lands, axes, send_sems, recv_sems, k, p, block, to, from_shard=False):
    dst = _block(lands[p], axes[p], lands[p].shape[axes[p]] // N_DEV, block)
    sem = k * len(lands) + p
    return pltpu.make_async_remote_copy(
        src_ref=ins[p] if from_shard else dst, dst_ref=dst, send_sem=send_sems.at[sem], recv_sem=recv_sems.at[sem],
        device_id=to, device_id_type=MESH_ID)


def _own_block_copy(ins, lands, axes, sems, p, me):
    n = len(lands)
    dst = _block(lands[p], axes[p], lands[p].shape[axes[p]] // N_DEV, me)
    return pltpu.make_async_copy(ins[p], dst, sems.at[4 * n + p])


def all_gather_two_level(shards, axes, name):
    n = len(shards)
    fulls = []
    for s, ax in zip(shards, axes):
        shape = list(s.shape)
        shape[ax] *= N_DEV
        fulls.append(_sds(tuple(shape), s.dtype))

    def body(*refs):
        ins, outs = refs[:n], refs[n:2 * n]
        send_sems, recv_sems, local_sems = refs[2 * n:]
        x, y, c, me = _position()
        sibling, here = (x, y, 1 - c), (x, y, c)
        chips = _other_chips(x, y)
        copy = functools.partial(_gather_copy, ins, outs, axes, send_sems, recv_sems)
        local = [pltpu.make_async_copy(ins[p], _block(outs[p], axes[p], ins[p].shape[axes[p]], me), local_sems.at[p])
                 for p in range(n)]
        sent = []
        for p in range(n):
            sent.append(copy(0, p, me, sibling, from_shard=True))
            sent += [copy(1 + j, p, me, (*chip, c), from_shard=True) for j, chip in enumerate(chips)]
        for cp in local + sent:
            cp.start()
        for j, (px, py) in enumerate(chips):
            for p in range(n):
                arrived = 4 * px + 2 * py + c
                copy(1 + j, p, arrived, here).wait_recv()
                sent.append(copy(4 + j, p, arrived, sibling))
                sent[-1].start()
        for p in range(n):
            copy(0, p, 4 * x + 2 * y + 1 - c, here).wait_recv()
            for j, (px, py) in enumerate(chips):
                copy(4 + j, p, 4 * px + 2 * py + 1 - c, here).wait_recv()
        for cp in sent:
            cp.wait_send()
        for cp in local:
            cp.wait()

    any_spec = pl.BlockSpec(memory_space=pl.ANY)
    return pl.pallas_call(
        body, name=name,
        in_specs=[any_spec] * n, out_specs=[any_spec] * n, out_shape=fulls,
        scratch_shapes=[pltpu.SemaphoreType.DMA((7 * n,)), pltpu.SemaphoreType.DMA((7 * n,)), pltpu.SemaphoreType.DMA((n,))],
    )(*shards)


def gather_two_level_start(shards, axes, after, name):
    n = len(shards)
    lands = []
    for s, ax in zip(shards, axes):
        shape = list(s.shape)
        shape[ax] *= N_DEV
        lands.append(lax.empty(tuple(shape), s.dtype))

    def body(*refs):
        ins, land = refs[:n], refs[n:2 * n]
        send_sems, recv_sems, token = refs[2 * n + 1], refs[2 * n + 2], refs[-1]
        x, y, c, me = _position()
        copy = functools.partial(_gather_copy, ins, land, axes, send_sems, recv_sems)
        for p in range(n):
            copy(0, p, me, (x, y, 1 - c), from_shard=True).start()
            for j, chip in enumerate(_other_chips(x, y)):
                copy(1 + j, p, me, (*chip, c), from_shard=True).start()
            _own_block_copy(ins, land, axes, send_sems, p, me).start()
        token[...] = jnp.zeros_like(token)

    outs = pl.pallas_call(
        body, name=name,
        out_shape=(pltpu.SemaphoreType.DMA((5 * n,)), pltpu.SemaphoreType.DMA((4 * n,)),
                   *[pltpu.HBM(s.shape, s.dtype) for s in shards], *[pltpu.HBM(l.shape, l.dtype) for l in lands],
                   _sds((8, 128), F32)),
        in_specs=[_HBM] * (2 * n) + [pl.BlockSpec(memory_space=pl.ANY)],
        out_specs=(_SEM, _SEM, *[_HBM] * (2 * n), pl.BlockSpec(memory_space=pltpu.VMEM)),
        input_output_aliases={i: 2 + i for i in range(2 * n)},
        compiler_params=pltpu.CompilerParams(has_side_effects=_EFFECT),
    )(*[_in_hbm(s) for s in shards], *[_in_hbm(l) for l in lands], after)
    return outs[0], outs[1], list(outs[2:2 + n]), list(outs[2 + n:2 + 2 * n]), outs[-1]


def gather_two_level_forward(started, axes, after, name):
    send_a, recv_a, shards, lands, _ = started
    n = len(shards)

    def body(*refs):
        land = refs[:n]
        send_first, recv_first = refs[n], refs[n + 1]
        send_fwd, recv_fwd, token = refs[n + 3], refs[n + 4], refs[-1]
        x, y, c, _ = _position()
        for j, (px, py) in enumerate(_other_chips(x, y)):
            for p in range(n):
                arrived = 4 * px + 2 * py + c
                _gather_copy(None, land, axes, send_first, recv_first, 1 + j, p, arrived, (x, y, c)).wait_recv()
                _gather_copy(None, land, axes, send_fwd, recv_fwd, j, p, arrived, (x, y, 1 - c)).start()
        token[...] = jnp.zeros_like(token)

    outs = pl.pallas_call(
        body, name=name,
        out_shape=(pltpu.SemaphoreType.DMA((3 * n,)), pltpu.SemaphoreType.DMA((3 * n,)),
                   *[pltpu.HBM(l.shape, l.dtype) for l in lands], _sds((8, 128), F32)),
        in_specs=[_HBM] * n + [_SEM, _SEM, pl.BlockSpec(memory_space=pl.ANY)],
        out_specs=(_SEM, _SEM, *[_HBM] * n, pl.BlockSpec(memory_space=pltpu.VMEM)),
        input_output_aliases={i: 2 + i for i in range(n)},
        compiler_params=pltpu.CompilerParams(has_side_effects=_EFFECT),
    )(*lands, send_a, recv_a, after)
    return send_a, recv_a, outs[0], outs[1], shards, list(outs[2:2 + n]), outs[-1]


def gather_two_level_wait(forwarded, axes, after, name):
    send_a, recv_a, send_f, recv_f, shards, lands, _ = forwarded
    n = len(shards)

    def body(*refs):
        ins, land = refs[:n], refs[n:2 * n]
        sa, ra, sf, rf = refs[2 * n:2 * n + 4]
        x, y, c, me = _position()
        here = (x, y, c)
        chips = _other_chips(x, y)
        for p in range(n):
            for k in range(4):
                _gather_copy(ins, land, axes, sa, ra, k, p, me, here, from_shard=True).wait_send()
            _own_block_copy(ins, land, axes, sa, p, me).wait()
            _gather_copy(ins, land, axes, sa, ra, 0, p, 4 * x + 2 * y + 1 - c, here).wait_recv()
            for j, (px, py) in enumerate(chips):
                _gather_copy(ins, land, axes, sf, rf, j, p, 4 * px + 2 * py + c, here).wait_send()
                _gather_copy(ins, land, axes, sf, rf, j, p, 4 * px + 2 * py + 1 - c, here).wait_recv()

    outs = pl.pallas_call(
        body, name=name,
        out_shape=(*[pltpu.HBM(s.shape, s.dtype) for s in shards], *[pltpu.HBM(l.shape, l.dtype) for l in lands]),
        in_specs=[_HBM] * (2 * n) + [_SEM] * 4 + [pl.BlockSpec(memory_space=pl.ANY)],
        out_specs=tuple([_HBM] * (2 * n)),
        input_output_aliases={i: i for i in range(2 * n)},
        compiler_params=pltpu.CompilerParams(has_side_effects=_EFFECT),
    )(*shards, *lands, send_a, recv_a, send_f, recv_f, after)
    return list(outs[n:])


_HBM = pl.BlockSpec(memory_space=pltpu.HBM)
_SEM = pl.BlockSpec(memory_space=pltpu.SEMAPHORE)
_EFFECT = pltpu.SideEffectType.DATAFLOW_SIDE_EFFECTING


def _in_hbm(a):
    return pltpu.with_memory_space_constraint(a, pltpu.HBM)


def _landing(mode, src, axis):
    me = 4 * lax.axis_index("x") + 2 * lax.axis_index("y") + lax.axis_index("c")
    if mode == "gather":
        shape = list(src.shape)
        size = shape[axis]
        shape[axis] *= N_DEV
        return lax.dynamic_update_slice_in_dim(lax.empty(tuple(shape), src.dtype), src, me * size, axis)
    size = src.shape[axis] // N_DEV
    own = lax.dynamic_slice_in_dim(src, me * size, size, axis)
    return lax.dynamic_update_slice_in_dim(lax.empty((N_DEV, *own.shape), src.dtype), own[None], me, 0)


def _seven(mode, land_ref, axis):
    if mode == "gather":
        return _block(land_ref, axis, (N_DEV - 1) * (land_ref.shape[axis] // N_DEV), 0)
    return land_ref.at[pl.ds(0, N_DEV - 1)]


def exchange_start(mode, srcs, axes, after, name):
    n = len(srcs)
    lands = [_landing(mode, s, ax) for s, ax in zip(srcs, axes)]

    def body(*refs):
        src, land = refs[:n], refs[n:2 * n]
        send_sems, recv_sems = refs[2 * n + 1], refs[2 * n + 2]
        token = refs[-1]
        x, y, c, me = _position()
        for k in range(1, N_DEV):
            peer, pj = _peer(x, y, c, k)
            for p in range(n):
                if mode == "gather":
                    s = src[p]
                    dst = _block(land[p], axes[p], src[p].shape[axes[p]], me)
                else:
                    s = _block(src[p], axes[p], src[p].shape[axes[p]] // N_DEV, pj)
                    dst = land[p].at[me]
                pltpu.make_async_remote_copy(
                    src_ref=s, dst_ref=dst, send_sem=send_sems.at[p], recv_sem=recv_sems.at[p],
                    device_id=peer, device_id_type=MESH_ID).start()
        token[...] = jnp.zeros_like(token)

    outs = pl.pallas_call(
        body, name=name,
        out_shape=(pltpu.SemaphoreType.DMA((n,)), pltpu.SemaphoreType.DMA((n,)),
                   *[pltpu.HBM(s.shape, s.dtype) for s in srcs], *[pltpu.HBM(l.shape, l.dtype) for l in lands],
                   _sds((8, 128), F32)),
        in_specs=[_HBM] * (2 * n) + [pl.BlockSpec(memory_space=pl.ANY)],
        out_specs=(_SEM, _SEM, *[_HBM] * (2 * n), pl.BlockSpec(memory_space=pltpu.VMEM)),
        input_output_aliases={i: 2 + i for i in range(2 * n)},
        compiler_params=pltpu.CompilerParams(has_side_effects=_EFFECT),
    )(*[_in_hbm(s) for s in srcs], *[_in_hbm(l) for l in lands], after)
    return outs[0], outs[1], list(outs[2:2 + n]), list(outs[2 + n:2 + 2 * n]), outs[-1]


def exchange_wait(mode, started, axes, after, name):
    send_sems, recv_sems, srcs, lands, _ = started
    n = len(srcs)

    def body(*refs):
        land = refs[n:2 * n]
        send_ref, recv_ref = refs[2 * n], refs[2 * n + 1]
        x, y, c, _ = _position()
        for p in range(n):
            seven = _seven(mode, land[p], axes[p])
            cp = pltpu.make_async_remote_copy(
                src_ref=seven, dst_ref=seven, send_sem=send_ref.at[p], recv_sem=recv_ref.at[p],
                device_id=(x, y, c), device_id_type=MESH_ID)
            cp.wait_send()
            cp.wait_recv()

    outs = pl.pallas_call(
        body, name=name,
        out_shape=(*[pltpu.HBM(s.shape, s.dtype) for s in srcs], *[pltpu.HBM(l.shape, l.dtype) for l in lands]),
        in_specs=[_HBM] * (2 * n) + [_SEM, _SEM, pl.BlockSpec(memory_space=pl.ANY)],
        out_specs=tuple([_HBM] * (2 * n)),
        input_output_aliases={i: i for i in range(2 * n)},
        compiler_params=pltpu.CompilerParams(has_side_effects=_EFFECT),
    )(*srcs, *lands, send_sems, recv_sems, after)
    return list(outs[n:])


def _adam_math(g, w, m, v):
    m2 = ADAM_B1 * m + (1.0 - ADAM_B1) * g
    v2 = ADAM_B2 * v + (1.0 - ADAM_B2) * (g * g)
    m_hat = m2 / (1.0 - ADAM_B1 ** ADAM_STEP)
    v_hat = v2 / (1.0 - ADAM_B2 ** ADAM_STEP)
    delta = -ADAM_LR * (m_hat / (jnp.sqrt(v_hat) + ADAM_EPS) + ADAM_WD * w)
    return delta, m2, v2


def adamw(g, w, m, v, name, slots, tr=128):
    r, c = w.shape
    tr = min(tr, r)

    def body(g_ref, w_ref, m_ref, v_ref, go_ref, d_ref, mo_ref, vo_ref):
        if slots:
            gs = g_ref[0].astype(F32)
            for s in range(1, N_DEV):
                gs = gs + g_ref[s].astype(F32)
        else:
            gs = g_ref[...]
        go_ref[...] = gs
        d_ref[...], mo_ref[...], vo_ref[...] = _adam_math(gs, w_ref[...], m_ref[...], v_ref[...])

    row = pl.BlockSpec((tr, c), lambda i: (i, 0))
    gspec = pl.BlockSpec((N_DEV, tr, c), lambda i: (0, i, 0)) if slots else row
    return pl.pallas_call(
        body, name=name, grid=(r // tr,),
        in_specs=[gspec, row, row, row], out_specs=[row] * 4, out_shape=[_sds((r, c), F32)] * 4,
        compiler_params=_cp("parallel"),
    )(g, w, m, v)


def sum_slots(g, name):
    _, r, c = g.shape

    def body(g_ref, o_ref):
        gs = g_ref[0]
        for s in range(1, N_DEV):
            gs = gs + g_ref[s]
        o_ref[...] = gs

    return pl.pallas_call(
        body, name=name, grid=(1,),
        in_specs=[pl.BlockSpec((N_DEV, r, c), lambda i: (0, 0, 0))],
        out_specs=pl.BlockSpec((r, c), lambda i: (0, 0)), out_shape=_sds((r, c), F32),
        compiler_params=_cp("arbitrary"),
    )(g)


def _rows128(a, pad_to=8):
    a = a.reshape(-1, 128)
    pad = (-a.shape[0]) % pad_to
    return jnp.pad(a, ((0, pad), (0, 0))) if pad else a


def kernel(x, even_norm, even_w_in, even_pool_w, even_pool_scale, even_ws, even_bs, even_w_out, odd_norm, odd_w_in, odd_conv_w, odd_w_out, final_norm, loss_target, m_even_norm, m_even_w_in, m_even_pool_w, m_even_pool_scale, m_even_ws, m_even_bs, m_even_w_out, m_odd_norm, m_odd_w_in, m_odd_conv_w, m_odd_w_out, m_final_norm, v_even_norm, v_even_w_in, v_even_pool_w, v_even_pool_scale, v_even_ws, v_even_bs, v_even_w_out, v_odd_norm, v_odd_w_in, v_odd_conv_w, v_odd_w_out, v_final_norm):
    x0 = x[0]
    target = loss_target[0]
    me = 4 * lax.axis_index("x") + 2 * lax.axis_index("y") + lax.axis_index("c")

    (we_in,) = all_gather_two_level([even_w_in[0].astype(BF16)], [1], "gather_even_in")
    odd_small = jnp.pad(odd_norm, ((0, 7), (0, 0))) + jnp.pad(odd_conv_w[0], ((1, 4), (0, 0)))
    rest_axes, odd_axes = [1, 0, 1], [1, 0]
    rest_flight = exchange_start("gather", [even_pool_w[0].astype(BF16), even_w_out[0].astype(BF16), odd_small],
                                 rest_axes, we_in, "gather_even_rest_start")
    odd_flight = gather_two_level_start([odd_w_in[0].astype(BF16), odd_w_out[0].astype(BF16)],
                                        odd_axes, rest_flight[4], "gather_odd_start")
    scale = even_pool_scale
    ws = even_ws[0]
    bs_col = even_bs[0][:, :, None]
    g_fin = final_norm[None, :]

    z_e, h_e = norm_matmul(x0, even_norm + odd_flight[4][0:1, 0:1], we_in, "even_in", tn=1280)
    wp, we_out, odd_small = exchange_wait("gather", rest_flight, rest_axes, z_e, "gather_even_rest_wait")
    g_odd, conv_w = odd_small[0:1], odd_small[1:4]
    y_e, pooled = even_mix(z_e, wp, scale, ws, bs_col, "even_mix")
    x1, h_o = matmul_residual(y_e, we_out, x0, "even_out", next_gain=g_odd)
    odd_flight = gather_two_level_forward(odd_flight, odd_axes, x1, "gather_odd_forward")
    h_layouts = [h_o] + [permute_mats(h_o, [], gi, False, f"dilate_h_{gi}")[0] for gi in (1, 2)]
    wo_in, wo_out = gather_two_level_wait(odd_flight, odd_axes, h_layouts[2], "gather_odd_wait")
    z_o, wo_in_t = matmul_layouts(h_layouts, wo_in, "odd_in")
    attn = [attn_fwd(z_o, gi, f"attn_fwd_{gi}") for gi in range(3)]
    for gi in (1, 2):
        attn[gi] = permute_mats(attn[gi][0], [attn[gi][1]], gi, True, f"undilate_attn_{gi}")
    y_o, yc, lse_tot, conv = odd_mix([a[0] for a in attn], [a[1] for a in attn], z_o, conv_w, "odd_mix")
    dx2, loss_blk, dg_fin = out_proj_final_loss(y_o, wo_out, x1, g_fin, target, "odd_out_loss")

    dy_o, dwo_out = out_proj_bwd(dx2, wo_out.T, y_o, "odd_out_bwd")
    dz5_o, dyc, dd, dconv_w = odd_mix_bwd(dy_o, yc, conv, z_o, conv_w, "odd_mix_bwd")
    dqkv = [attn_bwd(z_o, dyc, lse_tot, dd, 0, "attn_bwd_0")]
    for gi in (1, 2):
        dyc_g, lse_g, dd_g = permute_mats(dyc, [lse_tot, dd], gi, False, f"dilate_dy_{gi}")
        dqkv.append(permute_slabs(attn_bwd(z_o, dyc_g, lse_g, dd_g, gi, f"attn_bwd_{gi}"),
                                  gi, True, f"undilate_dqkv_{gi}"))
    src_o = [(dqkv[0], ("slabs", 0)), (dqkv[1], ("slabs", 1)), (dqkv[2], ("slabs", 2)), (dz5_o, ("cols", 9, 5))]
    dwo_in = in_proj_dw(h_o, src_o, ODD_COLBLOCKS, "odd_in_dw")
    odd_grads = exchange_start("scatter", [dwo_in, dwo_out], [1, 0], dconv_w, "scatter_odd_start")
    dx1, dg_odd = in_proj_dx(src_o, ODD_COLBLOCKS, wo_in_t, x1, g_odd + odd_grads[4][0:1, 0:1], dx2, "odd_in_dx",
                             epilogue_buffers=1)

    dy_e, dwe_out = out_proj_bwd(dx1, we_out.T, y_e, "even_out_bwd")
    dz_e, dwp, dscale, dws, dbs_col = even_mix_bwd(dy_e, z_e, pooled, wp, scale, ws, bs_col, "even_mix_bwd")
    src_e = [(dz_e, ("cols", 0, 5))]
    even_grads_a = exchange_start("scatter", [dwp.astype(BF16), dwe_out], [1, 0], dscale, "scatter_even_rest_start")
    dwe_in = in_proj_dw(h_e, src_e, 5, "even_in_dw")
    even_grads_b = exchange_start("scatter", [dwe_in], [1], even_grads_a[4], "scatter_even_in_start")
    dx0, dg_even = in_proj_dx(src_e, 5, we_in.T, x0, even_norm + even_grads_b[4][0:1, 0:1], dx1, "even_in_dx")
    small = jnp.concatenate([
        _rows128(dg_even), _rows128(dscale), _rows128(dws), _rows128(dbs_col), _rows128(dg_fin),
        _rows128(dg_odd), _rows128(dconv_w), loss_blk], axis=0)
    n_small = small.shape[0]
    small_flight = exchange_start("gather", [small], [0], dg_fin, "gather_small_start")
    p_wo_in, p_wo_out = exchange_wait("scatter", odd_grads, [1, 0], small_flight[4], "scatter_odd_wait")
    p_wp, p_we_out = exchange_wait("scatter", even_grads_a, [1, 0], small_flight[4], "scatter_even_rest_wait")
    (p_we_in,) = exchange_wait("scatter", even_grads_b, [1], small_flight[4], "scatter_even_in_wait")

    two_d = lambda a, r, c: a.reshape(r, c)
    sharded = [
        ("even_w_in", p_we_in, even_w_in, m_even_w_in, v_even_w_in, (1024, 640)),
        ("even_pool_w", p_wp.reshape(N_DEV, 128, 256), even_pool_w, m_even_pool_w, v_even_pool_w, (128, 256)),
        ("even_w_out", p_we_out, even_w_out, m_even_w_out, v_even_w_out, (256, 1024)),
        ("odd_w_in", p_wo_in, odd_w_in, m_odd_w_in, v_odd_w_in, (1024, 1792)),
        ("odd_w_out", p_wo_out, odd_w_out, m_odd_w_out, v_odd_w_out, (256, 1024)),
    ]
    res = {}
    for name, g, w, m, v, (r, c) in sharded:
        outs = adamw(g, two_d(w, r, c), two_d(m, r, c), two_d(v, r, c), "adamw_" + name, slots=True)
        res[name] = [o.reshape(w.shape) for o in outs]

    (small_all,) = exchange_wait("gather", small_flight, [0], res["odd_w_out"][0], "gather_small_wait")
    small_sum = sum_slots(small_all.reshape(N_DEV, n_small, 128), "sum_small_grads")
    g_even_norm = small_sum[0:8].reshape(1, 1024)
    g_scale = small_sum[8:16].reshape(1, 1024)
    g_ws = small_sum[16:528]
    g_bs = small_sum[528:532]
    g_final = small_sum[536:544].reshape(1, 1024)
    g_odd_norm = lax.dynamic_slice_in_dim(small_sum[544:552], me, 1, axis=0)
    g_conv = lax.dynamic_index_in_dim(small_sum[552:576].reshape(3, 8, 128), me, axis=1, keepdims=False)
    plain = [
        ("even_norm", g_even_norm, even_norm, m_even_norm, v_even_norm, (1, 1024)),
        ("even_pool_scale", g_scale, even_pool_scale, m_even_pool_scale, v_even_pool_scale, (1, 1024)),
        ("even_ws", g_ws, even_ws, m_even_ws, v_even_ws, (512, 128)),
        ("even_bs", g_bs, even_bs, m_even_bs, v_even_bs, (4, 128)),
        ("odd_norm", g_odd_norm, odd_norm, m_odd_norm, v_odd_norm, (1, 128)),
        ("odd_conv_w", g_conv, odd_conv_w, m_odd_conv_w, v_odd_conv_w, (3, 128)),
        ("final_norm", g_final, final_norm, m_final_norm, v_final_norm, (1, 1024)),
    ]
    for name, g, w, m, v, (r, c) in plain:
        outs = adamw(two_d(g, r, c), two_d(w, r, c), two_d(m, r, c), two_d(v, r, c), "adamw_" + name, slots=False)
        res[name] = [o.reshape(w.shape) for o in outs]

    loss = small_sum[576, 0]
    order = ["even_norm", "even_w_in", "even_pool_w", "even_pool_scale", "even_ws", "even_bs", "even_w_out",
             "odd_norm", "odd_w_in", "odd_conv_w", "odd_w_out", "final_norm"]
    return (loss, dx0[None], *[res[n][0] for n in order], *[res[n][1] for n in order],
            *[res[n][2] for n in order], *[res[n][3] for n in order])
```

```python
import functools

import jax
import jax.numpy as jnp
from jax import lax
from jax.experimental import pallas as pl
from jax.experimental.pallas import tpu as pltpu

F32 = jnp.float32
BF16 = jnp.bfloat16
MESH_ID = pl.DeviceIdType.MESH

EPS = 1e-6
NEG = -1e30
N_DEV = 8
POOL_SIZES = (2, 4, 8, 16)
DILATIONS = (1, 4, 16)
N_HEADS = 24
HEADS_PER_GROUP = 8
HEAD_DIM = 128
ATTN_BLOCK = 128
CHUNK = 128
CB = 1024
HALO = 16
ODD_COLBLOCKS = 14
ADAM_LR = 0.001
ADAM_B1 = 0.9
ADAM_B2 = 0.999
ADAM_EPS = 1e-08
ADAM_WD = 0.01
ADAM_STEP = 10
VMEM_LIMIT = 52 * 1024 * 1024


def _cp(*sem):
    return pltpu.CompilerParams(dimension_semantics=sem, vmem_limit_bytes=VMEM_LIMIT)


def _dot(a, b):
    return jnp.dot(a, b, preferred_element_type=F32)


def _dot_nt(a, b):
    return lax.dot_general(a, b, (((1,), (1,)), ((), ())), preferred_element_type=F32)


def _dot_tn(a, b):
    return lax.dot_general(a, b, (((0,), (0,)), ((), ())), preferred_element_type=F32)


def _sigmoid(x):
    return 0.5 * jnp.tanh(0.5 * x) + 0.5


def _silu_and_grad(x):
    s = _sigmoid(x)
    return x * s, s * (1.0 + x * (1.0 - s))


def _sds(shape, dtype):
    return jax.ShapeDtypeStruct(shape, dtype)


def norm_matmul(x, g, w, name, tn, tm=1024, rows=256):
    t, d = x.shape
    n = w.shape[1]

    def body(x_ref, g_ref, w_ref, z_ref, h_ref):
        @pl.when(pl.program_id(1) == 0)
        def _():
            for c in range(tm // rows):
                rs = slice(c * rows, (c + 1) * rows)
                xv = x_ref[rs, :]
                r = lax.rsqrt(jnp.mean(xv * xv, axis=-1, keepdims=True) + EPS)
                h_ref[rs, :] = ((xv * r) * g_ref[...]).astype(BF16)

        z_ref[...] = _dot(h_ref[...], w_ref[...]).astype(BF16)

    return pl.pallas_call(
        body, name=name, grid=(t // tm, n // tn),
        in_specs=[pl.BlockSpec((tm, d), lambda i, j: (i, 0)),
                  pl.BlockSpec((1, d), lambda i, j: (0, 0)),
                  pl.BlockSpec((d, tn), lambda i, j: (0, j))],
        out_specs=[pl.BlockSpec((tm, tn), lambda i, j: (i, j)),
                   pl.BlockSpec((tm, d), lambda i, j: (i, 0))],
        out_shape=[_sds((t, n), BF16), _sds((t, d), BF16)],
        compiler_params=_cp("parallel", "arbitrary"),
    )(x, g, w)


def matmul_residual(y, w, x, name, next_gain=None, tm=1024, rows=256):
    t, k = y.shape
    d = w.shape[1]

    def body(y_ref, w_ref, x_ref, *rest):
        o_ref = rest[-1] if next_gain is None else rest[-2]
        o_ref[...] = x_ref[...] + _dot(y_ref[...], w_ref[...])
        if next_gain is not None:
            g_ref, h_ref = rest[0], rest[-1]
            for c in range(tm // rows):
                rs = slice(c * rows, (c + 1) * rows)
                xv = o_ref[rs, :]
                r = lax.rsqrt(jnp.mean(xv * xv, axis=-1, keepdims=True) + EPS)
                h_ref[rs, :] = ((xv * r) * g_ref[...]).astype(BF16)

    row = pl.BlockSpec((tm, d), lambda i: (i, 0))
    in_specs = [pl.BlockSpec((tm, k), lambda i: (i, 0)), pl.BlockSpec((k, d), lambda i: (0, 0)), row]
    if next_gain is None:
        return pl.pallas_call(
            body, name=name, grid=(t // tm,), in_specs=in_specs, out_specs=row, out_shape=_sds((t, d), F32),
            compiler_params=_cp("parallel"),
        )(y, w, x)
    return pl.pallas_call(
        body, name=name, grid=(t // tm,),
        in_specs=in_specs + [pl.BlockSpec((1, d), lambda i: (0, 0))],
        out_specs=[row, row], out_shape=[_sds((t, d), F32), _sds((t, d), BF16)],
        compiler_params=_cp("parallel"),
    )(y, w, x, next_gain)


def matmul_layouts(hs, w, name, tm=2048):
    t, d = hs[0].shape
    n_blocks = w.shape[1] // CB

    def layout_of(j):
        return jnp.where(j < 9, j // 3, 0)

    def column_of(j):
        return jnp.where(j < 9, (j % 3) * 3 + j // 3, j)

    def body(h0, h1, h2, w_ref, z_ref, wt_ref):
        layout = layout_of(pl.program_id(1))
        for v, h_ref in enumerate((h0, h1, h2)):
            @pl.when(layout == v)
            def _(h_ref=h_ref):
                z_ref[...] = _dot(h_ref[...], w_ref[...]).astype(BF16)
        @pl.when(pl.program_id(0) == 0)
        def _():
            wt_ref[...] = w_ref[...].T

    row = pl.BlockSpec((tm, d), lambda i, j: (i, 0))
    wt_block = lambda i, j: (jnp.where(i == 0, column_of(j), n_blocks), 0)
    return pl.pallas_call(
        body, name=name, grid=(t // tm, n_blocks),
        in_specs=[row, row, row, pl.BlockSpec((d, CB), lambda i, j: (0, column_of(j)))],
        out_specs=[pl.BlockSpec((tm, CB), lambda i, j: (i, column_of(j))), pl.BlockSpec((CB, d), wt_block)],
        out_shape=[_sds((t, n_blocks * CB), BF16), _sds(((n_blocks + 1) * CB, d), BF16)],
        compiler_params=_cp("arbitrary", "arbitrary"),
    )(*hs, w)


def out_proj_bwd(dx, wt, y, name, tm=512):
    t, d = dx.shape
    k = wt.shape[1]
    steps = t // tm
    half = d // 2

    def body(dx_ref, wt_ref, y_ref, dy_ref, dw_ref, acc):
        i = pl.program_id(0)
        dxb = dx_ref[...].astype(BF16)
        dy_ref[...] = _dot(dxb, wt_ref[...]).astype(BF16)

        @pl.when(i == 0)
        def _():
            acc[...] = jnp.zeros_like(acc)

        for c in range(2):
            acc[:, c * half:(c + 1) * half] += _dot_tn(y_ref[...], dxb[:, c * half:(c + 1) * half])

        @pl.when(i == steps - 1)
        def _():
            dw_ref[...] = acc[...].astype(BF16)

    return pl.pallas_call(
        body, name=name, grid=(steps,),
        in_specs=[pl.BlockSpec((tm, d), lambda i: (i, 0)),
                  pl.BlockSpec((d, k), lambda i: (0, 0)),
                  pl.BlockSpec((tm, k), lambda i: (i, 0))],
        out_specs=[pl.BlockSpec((tm, k), lambda i: (i, 0)),
                   pl.BlockSpec((k, d), lambda i: (0, 0))],
        out_shape=[_sds((t, k), BF16), _sds((k, d), BF16)],
        scratch_shapes=[pltpu.VMEM((k, d), F32)],
        compiler_params=_cp("arbitrary"),
    )(dx, wt, y)


def _source_block(kind, rows):
    if kind[0] == "cols":
        return (rows, CB)
    return (None, rows, CB)


def _source_active(kind, j):
    if kind[0] == "cols":
        _, first, n = kind
        return (j >= first) & (j < first + n), jnp.clip(j - first, 0, n - 1)
    _, group = kind
    return (j < 9) & (j % 3 == group), jnp.clip(j // 3, 0, 2)


def _source_index(kind, row_block, inner):
    if kind[0] == "cols":
        return (row_block, inner)
    return (inner, row_block, 0)


def in_proj_dw(h, sources, n_blocks, name, tk=1024):
    t, d = h.shape
    steps = t // tk
    ns = len(sources)

    def body(*refs):
        h_ref, src_refs, dw_ref, acc = refs[0], refs[1:1 + ns], refs[1 + ns], refs[2 + ns]
        j, k = pl.program_id(0), pl.program_id(1)

        @pl.when(k == 0)
        def _():
            acc[...] = jnp.zeros_like(acc)

        for s, (_, kind) in enumerate(sources):
            active, _ = _source_active(kind, j)

            @pl.when(active)
            def _(s=s):
                acc[...] += _dot_tn(h_ref[...], src_refs[s][...])

        @pl.when(k == steps - 1)
        def _():
            dw_ref[...] = acc[...].astype(BF16)

    def src_spec(kind):
        def index(j, k):
            active, inner = _source_active(kind, j)
            return _source_index(kind, jnp.where(active, k, 0), inner)
        return pl.BlockSpec(_source_block(kind, tk), index)

    return pl.pallas_call(
        body, name=name, grid=(n_blocks, steps),
        in_specs=[pl.BlockSpec((tk, d), lambda j, k: (k, 0))] + [src_spec(kind) for _, kind in sources],
        out_specs=pl.BlockSpec((d, CB), lambda j, k: (0, j)),
        out_shape=_sds((d, n_blocks * CB), BF16),
        scratch_shapes=[pltpu.VMEM((d, CB), F32)],
        compiler_params=_cp("parallel", "arbitrary"),
    )(h, *[a for a, _ in sources])


def in_proj_dx(sources, n_blocks, wt, x, g, dres, name, tm=1024, rows=256):
    t, d = x.shape
    ns = len(sources)

    def body(*refs):
        src_refs = refs[:ns]
        w_ref, x_hbm, g_ref, dres_hbm, dx_ref, dg_ref, acc, x_ref, dres_ref, sems = refs[ns:]
        i, p = pl.program_id(0), pl.program_id(1)
        tile = pl.ds(pl.multiple_of(i * tm, tm), tm)
        fetch = [pltpu.make_async_copy(x_hbm.at[tile, :], x_ref, sems.at[0]),
                 pltpu.make_async_copy(dres_hbm.at[tile, :], dres_ref, sems.at[1])]

        @pl.when(p == 0)
        def _():
            acc[...] = jnp.zeros_like(acc)
            for cp in fetch:
                cp.start()

        for s, (_, kind) in enumerate(sources):
            active, _ = _source_active(kind, p)

            @pl.when(active)
            def _(s=s):
                acc[...] += _dot(src_refs[s][...], w_ref[...])

        @pl.when(p == n_blocks - 1)
        def _():
            for cp in fetch:
                cp.wait()
            part = jnp.zeros((1, d), F32)
            for c in range(tm // rows):
                rs = slice(c * rows, (c + 1) * rows)
                xv = x_ref[rs, :]
                r = lax.rsqrt(jnp.mean(xv * xv, axis=-1, keepdims=True) + EPS)
                xn = xv * r
                dh = acc[rs, :]
                dhg = dh * g_ref[...]
                dx_ref[rs, :] = dres_ref[rs, :] + r * (dhg - xn * jnp.mean(dhg * xn, axis=-1, keepdims=True))
                part = part + jnp.sum(dh * xn, axis=0, keepdims=True)

            @pl.when(i == 0)
            def _():
                dg_ref[...] = part

            @pl.when(i > 0)
            def _():
                dg_ref[...] += part

    def src_spec(kind):
        def index(i, p):
            if kind[0] == "cols":
                _, first, n = kind
                inner = jnp.clip(p - first, 0, n - 1)
            else:
                _, group = kind
                inner = jnp.clip((p - group + 2) // 3, 0, 2)
            return _source_index(kind, i, inner)
        return pl.BlockSpec(_source_block(kind, tm), index)

    any_spec = pl.BlockSpec(memory_space=pl.ANY)
    return pl.pallas_call(
        body, name=name, grid=(t // tm, n_blocks),
        in_specs=[src_spec(kind) for _, kind in sources] + [
            pl.BlockSpec((CB, d), lambda i, p: (p, 0)), any_spec, pl.BlockSpec((1, d), lambda i, p: (0, 0)), any_spec],
        out_specs=[pl.BlockSpec((tm, d), lambda i, p: (i, 0)),
                   pl.BlockSpec((1, d), lambda i, p: (0, 0))],
        out_shape=[_sds((t, d), F32), _sds((1, d), F32)],
        scratch_shapes=[pltpu.VMEM((tm, d), F32), pltpu.VMEM((tm, d), F32), pltpu.VMEM((tm, d), F32),
                        pltpu.SemaphoreType.DMA((2,))],
        compiler_params=_cp("arbitrary", "arbitrary"),
    )(*[a for a, _ in sources], wt, x, g, dres)


def _window_counts(first_row, rows, w):
    t = first_row + lax.broadcasted_iota(jnp.int32, (rows, 1), 0)
    return jnp.minimum(t + 1, w).astype(F32)


def _tril_bf16(ws_ref, g):
    r = lax.broadcasted_iota(jnp.int32, (CHUNK, CHUNK), 0)
    c = lax.broadcasted_iota(jnp.int32, (CHUNK, CHUNK), 1)
    return jnp.where(r >= c, ws_ref[g], 0.0).astype(BF16), r >= c


def even_mix(z, wp, scale, ws, bs_col, name, tm=256):
    t = z.shape[0]
    gd = CB // len(POOL_SIZES)

    def body(a_ref, ga_ref, u_ref, v_ref, gb_ref, wp_ref, sc_ref, ws_ref, bs_ref, y_ref, pooled_ref, aext):
        i = pl.program_id(0)

        @pl.when(i == 0)
        def _():
            aext[0:HALO, :] = jnp.zeros((HALO, CB), F32)

        @pl.when(i > 0)
        def _():
            aext[0:HALO, :] = aext[tm:tm + HALO, :]

        aext[HALO:HALO + tm, :] = a_ref[...].astype(F32)
        for g, w in enumerate(POOL_SIZES):
            cols = slice(g * gd, (g + 1) * gd)
            tok = aext[HALO:HALO + tm, cols]
            s = tok
            for k in range(1, w):
                s = s + aext[HALO - k:HALO - k + tm, cols]
            pooled = (s / _window_counts(i * tm, tm, w) - tok).astype(BF16)
            pooled_ref[:, cols] = pooled
            mixed = _dot(pooled, wp_ref[g])
            silu_a, _ = _silu_and_grad(ga_ref[:, cols].astype(F32))
            y_ref[:, cols] = (mixed * sc_ref[:, cols] * silu_a).astype(BF16)
        for g in range(4):
            cols = slice(g * gd, (g + 1) * gd)
            wg, _ = _tril_bf16(ws_ref, g)
            for c in range(tm // CHUNK):
                rows = slice(c * CHUNK, (c + 1) * CHUNK)
                m = _dot(wg, v_ref[rows, cols]) + bs_ref[g]
                silu_b, _ = _silu_and_grad(gb_ref[rows, cols].astype(F32))
                y_ref[rows, CB + g * gd:CB + (g + 1) * gd] = (
                    u_ref[rows, cols].astype(F32) * m * silu_b).astype(BF16)

    zspec = lambda cb: pl.BlockSpec((tm, CB), lambda i, cb=cb: (i, cb))
    full = lambda shape: pl.BlockSpec(shape, lambda i: (0,) * len(shape))
    return pl.pallas_call(
        body, name=name, grid=(t // tm,),
        in_specs=[zspec(0), zspec(1), zspec(2), zspec(3), zspec(4),
                  full(wp.shape), full(scale.shape), full(ws.shape), full(bs_col.shape)],
        out_specs=[pl.BlockSpec((tm, 2 * CB), lambda i: (i, 0)), pl.BlockSpec((tm, CB), lambda i: (i, 0))],
        out_shape=[_sds((t, 2 * CB), BF16), _sds((t, CB), BF16)],
        scratch_shapes=[pltpu.VMEM((HALO + tm, CB), F32)],
        compiler_params=_cp("arbitrary"),
    )(z, z, z, z, z, wp, scale, ws, bs_col)


def even_mix_bwd(dy, z, pooled, wp, scale, ws, bs_col, name, tm=256):
    t = z.shape[0]
    nt = t // tm
    gd = CB // len(POOL_SIZES)

    def body(dya_ref, dyb_ref, ga_ref, u_ref, v_ref, gb_ref, pooled_ref, wp_ref, sc_ref, ws_ref, bs_ref,
             dz_ref, dwp_ref, dsc_ref, dws_ref, dbs_ref, dpext):
        i = pl.program_id(0)
        tile = nt - 1 - i

        @pl.when(i == 0)
        def _():
            dpext[tm:tm + HALO, :] = jnp.zeros((HALO, CB), F32)
            dwp_ref[...] = jnp.zeros_like(dwp_ref)
            dsc_ref[...] = jnp.zeros_like(dsc_ref)
            dws_ref[...] = jnp.zeros_like(dws_ref)
            dbs_ref[...] = jnp.zeros_like(dbs_ref)

        @pl.when(i > 0)
        def _():
            dpext[tm:tm + HALO, :] = dpext[0:HALO, :]

        for g, w in enumerate(POOL_SIZES):
            cols = slice(g * gd, (g + 1) * gd)
            pooled_g = pooled_ref[:, cols]
            mixed = _dot(pooled_g, wp_ref[g])
            silu_a, dsilu_a = _silu_and_grad(ga_ref[:, cols].astype(F32))
            dya = dya_ref[:, cols].astype(F32)
            sc = sc_ref[:, cols]
            dmixed = (dya * sc * silu_a).astype(BF16)
            dsc_ref[:, cols] += jnp.sum(dya * mixed * silu_a, axis=0, keepdims=True)
            dz_ref[:, CB + g * gd:CB + (g + 1) * gd] = (dya * mixed * sc * dsilu_a).astype(BF16)
            dwp_ref[g] += _dot_tn(pooled_g, dmixed)
            dpooled = _dot_nt(dmixed, wp_ref[g])
            dpext[0:tm, cols] = dpooled / _window_counts(tile * tm, tm, w)
            s = dpext[0:tm, cols]
            for k in range(1, w):
                s = s + dpext[k:k + tm, cols]
            dz_ref[:, cols] = (s - dpooled).astype(BF16)
        for g in range(4):
            cols = slice(g * gd, (g + 1) * gd)
            wg, lower = _tril_bf16(ws_ref, g)
            dws_g = jnp.zeros((CHUNK, CHUNK), F32)
            dbs_g = jnp.zeros((CHUNK, 1), F32)
            for c in range(tm // CHUNK):
                rows = slice(c * CHUNK, (c + 1) * CHUNK)
                vb = v_ref[rows, cols]
                m = _dot(wg, vb) + bs_ref[g]
                gbv = gb_ref[rows, cols].astype(F32)
                silu_b, dsilu_b = _silu_and_grad(gbv)
                dyb = dyb_ref[rows, cols].astype(F32)
                uv = u_ref[rows, cols].astype(F32)
                dm = dyb * silu_b * uv
                dmb = dm.astype(BF16)
                dz_ref[rows, 2 * CB + g * gd:2 * CB + (g + 1) * gd] = (dyb * silu_b * m).astype(BF16)
                dz_ref[rows, 3 * CB + g * gd:3 * CB + (g + 1) * gd] = _dot_tn(wg, dmb).astype(BF16)
                dz_ref[rows, 4 * CB + g * gd:4 * CB + (g + 1) * gd] = (dyb * uv * m * dsilu_b).astype(BF16)
                dws_g = dws_g + _dot_nt(dmb, vb)
                dbs_g = dbs_g + jnp.sum(dm, axis=1, keepdims=True)
            dws_ref[g] += jnp.where(lower, dws_g, 0.0)
            dbs_ref[g] += dbs_g

    rev = lambda cb: pl.BlockSpec((tm, CB), lambda i, cb=cb: (nt - 1 - i, cb))
    full = lambda shape: pl.BlockSpec(shape, lambda i: (0,) * len(shape))
    return pl.pallas_call(
        body, name=name, grid=(nt,),
        in_specs=[rev(0), rev(1), rev(1), rev(2), rev(3), rev(4), rev(0),
                  full(wp.shape), full(scale.shape), full(ws.shape), full(bs_col.shape)],
        out_specs=[pl.BlockSpec((tm, 5 * CB), lambda i: (nt - 1 - i, 0)),
                   full(wp.shape), full(scale.shape), full(ws.shape), full(bs_col.shape)],
        out_shape=[_sds((t, 5 * CB), BF16), _sds(wp.shape, F32), _sds(scale.shape, F32),
                   _sds(ws.shape, F32), _sds(bs_col.shape, F32)],
        scratch_shapes=[pltpu.VMEM((tm + HALO, CB), F32)],
        compiler_params=_cp("arbitrary"),
    )(dy, dy, z, z, z, z, pooled, wp, scale, ws, bs_col)


def _slope(group, head):
    return float(2.0 ** (-8.0 * (group * HEADS_PER_GROUP + head + 1) / N_HEADS))


def _band(dilation):
    qi = lax.broadcasted_iota(jnp.int32, (ATTN_BLOCK, ATTN_BLOCK), 0)
    ki = lax.broadcasted_iota(jnp.int32, (ATTN_BLOCK, ATTN_BLOCK), 1)
    dist_prev = ((qi + ATTN_BLOCK - ki) * dilation).astype(F32)
    dist_cur = ((qi - ki) * dilation).astype(F32)
    return dist_prev, ki >= qi, dist_cur, ki <= qi


def _permute_rows(name, d, inverse, arrays, in_specs, out_shapes, out_specs, n_chunks, widths):
    rows = ATTN_BLOCK * d
    t = out_shapes[0].shape[-2]
    n = len(arrays)

    lanes = HEAD_DIM

    def body(*refs):
        for a, o, s, w in zip(refs[:n], refs[n:2 * n], refs[2 * n:], widths):
            for k in range(w // lanes):
                cols = slice(k * lanes, (k + 1) * lanes)
                if inverse:
                    for r in range(d):
                        s[k, pl.ds(r, ATTN_BLOCK, stride=d), :] = (
                            a[r * ATTN_BLOCK:(r + 1) * ATTN_BLOCK, cols].astype(F32))
                    o[:, cols] = s[k].astype(o.dtype)
                else:
                    s[k] = a[:, cols].astype(F32)
                    for r in range(d):
                        o[r * ATTN_BLOCK:(r + 1) * ATTN_BLOCK, cols] = (
                            s[k, pl.ds(r, ATTN_BLOCK, stride=d), :].astype(o.dtype))

    return pl.pallas_call(
        body, name=name, grid=(t // rows, n_chunks),
        in_specs=in_specs, out_specs=out_specs, out_shape=out_shapes,
        scratch_shapes=[pltpu.VMEM((w // lanes, rows, lanes), F32) for w in widths],
        compiler_params=_cp("parallel", "arbitrary"),
    )(*arrays)


def _chunk_width(d):
    return 4 * CB // d


def permute_mats(wide, narrow, group, inverse, name):
    d = DILATIONS[group]
    rows, cw = ATTN_BLOCK * d, _chunk_width(d)
    wide_spec = pl.BlockSpec((rows, cw), lambda i, c: (i, c))
    narrow_spec = pl.BlockSpec((rows, HEAD_DIM), lambda i, c: (i, 0))
    arrays = [wide] + list(narrow)
    specs = [wide_spec] + [narrow_spec] * len(narrow)
    return _permute_rows(name, d, inverse, arrays, specs, [_sds(a.shape, a.dtype) for a in arrays], specs,
                         CB // cw, [cw] + [HEAD_DIM] * len(narrow))


def permute_slabs(s, group, inverse, name):
    d = DILATIONS[group]
    rows, cw = ATTN_BLOCK * d, _chunk_width(d)
    per = CB // cw
    spec = pl.BlockSpec((None, rows, cw), lambda i, c: (c // per, i, c % per))
    (out,) = _permute_rows(name, d, inverse, [s], [spec], [_sds(s.shape, s.dtype)], [spec], 3 * per, [cw])
    return out


def _lane_pack(cols):
    rows = cols[0].shape[0]
    lane = lax.broadcasted_iota(jnp.int32, (rows, HEAD_DIM), 1)
    out = jnp.zeros((rows, HEAD_DIM), F32)
    for h, c in enumerate(cols):
        out = jnp.where(lane == h, c, out)
    return out


def _qkv_col(group, from_z):
    return (lambda which: which * 3 + group) if from_z else (lambda which: which)


SCORE_SCALE = HEAD_DIM ** -0.5
LOG2_E = 1.4426950408889634
EXP2_SCALE = SCORE_SCALE * LOG2_E


def _fill_bias(bias_ref, group, d):
    dist_p, ok_p, dist_c, ok_c = _band(d)
    for h in range(HEADS_PER_GROUP):
        k = -_slope(group, h) / SCORE_SCALE
        bias_ref[h, :, 0:ATTN_BLOCK] = jnp.where(ok_p, k * dist_p, NEG)
        bias_ref[h, :, ATTN_BLOCK:2 * ATTN_BLOCK] = jnp.where(ok_c, k * dist_c, NEG)


def _joined(prev_ref, cur_ref, cols):
    return jnp.concatenate([prev_ref[:, cols], cur_ref[:, cols]], axis=0)


def _raw_scores(q, kk, bias_ref, h, has_prev):
    is_cur = lax.broadcasted_iota(jnp.int32, (ATTN_BLOCK, 2 * ATTN_BLOCK), 1) >= ATTN_BLOCK
    return jnp.where(jnp.logical_or(is_cur, has_prev), _dot_nt(q, kk) + bias_ref[h], NEG)


def attn_fwd(qkv, group, name):
    t = qkv.shape[0]
    d = DILATIONS[group]
    col = _qkv_col(group, qkv.shape[1] != 3 * CB)
    heads = range(HEADS_PER_GROUP)

    def body(q_ref, kp_ref, kc_ref, vp_ref, vc_ref, o_ref, lse_ref, bias_ref):
        b = pl.program_id(0)

        @pl.when(b == 0)
        def _():
            _fill_bias(bias_ref, group, d)

        hs = [slice(h * HEAD_DIM, (h + 1) * HEAD_DIM) for h in heads]
        s = [_raw_scores(q_ref[:, hs[h]], _joined(kp_ref, kc_ref, hs[h]), bias_ref, h, b >= d) for h in heads]
        m = [jnp.max(t_, axis=1, keepdims=True) for t_ in s]
        e = [jnp.exp2(EXP2_SCALE * (t_ - m[h])) for h, t_ in enumerate(s)]
        l = [jnp.sum(e_, axis=1, keepdims=True) for e_ in e]
        o = [_dot(e_.astype(BF16), _joined(vp_ref, vc_ref, hs[h])) * (1.0 / l[h]) for h, e_ in enumerate(e)]
        for h in heads:
            o_ref[:, hs[h]] = o[h].astype(BF16)
        lse_ref[...] = _lane_pack([SCORE_SCALE * m[h] + jnp.log(l[h]) for h in heads])

    blk = (ATTN_BLOCK, CB)
    cur = lambda which: pl.BlockSpec(blk, lambda b: (b, col(which)))
    prev = lambda which: pl.BlockSpec(blk, lambda b: (jnp.maximum(b - d, 0), col(which)))
    return pl.pallas_call(
        body, name=name, grid=(t // ATTN_BLOCK,),
        in_specs=[cur(0), prev(1), cur(1), prev(2), cur(2)],
        out_specs=[pl.BlockSpec(blk, lambda b: (b, 0)), pl.BlockSpec((ATTN_BLOCK, HEAD_DIM), lambda b: (b, 0))],
        out_shape=[_sds((t, CB), BF16), _sds((t, HEAD_DIM), F32)],
        scratch_shapes=[pltpu.VMEM((HEADS_PER_GROUP, ATTN_BLOCK, 2 * ATTN_BLOCK), F32)],
        compiler_params=_cp("arbitrary"),
    )(qkv, qkv, qkv, qkv, qkv)


def attn_bwd(z, dyc, lse, dd, group, name):
    t = z.shape[0]
    d = DILATIONS[group]
    nb = t // (ATTN_BLOCK * d)
    col = _qkv_col(group, z.shape[1] != 3 * CB)
    heads = range(HEADS_PER_GROUP)

    steps = d * nb

    def body(q_ref, kp_ref, kc_ref, vp_ref, vc_ref, dy_ref, lse_ref, dd_ref, out_ref, carry, bias_ref):
        s_ = pl.program_id(0)
        n = s_ % nb

        @pl.when(s_ == 0)
        def _():
            _fill_bias(bias_ref, group, d)
            carry[...] = jnp.zeros_like(carry)

        @pl.when(s_ == steps)
        def _():
            out_ref[...] = carry[...].astype(BF16)

        @pl.when(s_ < steps)
        def _():
            hs = [slice(h * HEAD_DIM, (h + 1) * HEAD_DIM) for h in heads]
            q = [q_ref[:, hs[h]] for h in heads]
            kk = [_joined(kp_ref, kc_ref, hs[h]) for h in heads]
            dy = [dy_ref[:, hs[h]] for h in heads]
            old = [[carry[w, :, hs[h]] for h in heads] for w in range(3)]
            s = [_raw_scores(q[h], kk[h], bias_ref, h, n > 0) for h in heads]
            p = [jnp.exp2(EXP2_SCALE * t_ - lse_ref[:, h:h + 1] * LOG2_E) for h, t_ in enumerate(s)]
            ds = [(p[h] * (_dot_nt(dy[h], _joined(vp_ref, vc_ref, hs[h])) - dd_ref[:, h:h + 1]) * SCORE_SCALE).astype(BF16)
                  for h in heads]
            for h in heads:
                dkk = _dot_tn(ds[h], q[h])
                dvv = _dot_tn(p[h].astype(BF16), dy[h])
                out_ref[0, :, hs[h]] = old[0][h].astype(BF16)
                out_ref[1, :, hs[h]] = (old[1][h] + dkk[0:ATTN_BLOCK]).astype(BF16)
                out_ref[2, :, hs[h]] = (old[2][h] + dvv[0:ATTN_BLOCK]).astype(BF16)
                carry[0, :, hs[h]] = _dot(ds[h], kk[h])
                carry[1, :, hs[h]] = dkk[ATTN_BLOCK:2 * ATTN_BLOCK]
                carry[2, :, hs[h]] = dvv[ATTN_BLOCK:2 * ATTN_BLOCK]

    blk = (ATTN_BLOCK, CB)

    def row_of(s_):
        s_ = jnp.clip(s_, 0, steps - 1)
        return (s_ % nb) * d + s_ // nb

    def prev_row_of(s_):
        s_ = jnp.clip(s_, 0, steps - 1)
        return row_of(s_) - jnp.where(s_ % nb > 0, d, 0)

    cur = lambda which: pl.BlockSpec(blk, lambda s_: (row_of(s_), col(which)))
    prev = lambda which: pl.BlockSpec(blk, lambda s_: (prev_row_of(s_), col(which)))
    narrow = pl.BlockSpec((ATTN_BLOCK, HEAD_DIM), lambda s_: (row_of(s_), 0))
    return pl.pallas_call(
        body, name=name, grid=(steps + 1,),
        in_specs=[cur(0), prev(1), cur(1), prev(2), cur(2), pl.BlockSpec(blk, lambda s_: (row_of(s_), 0)), narrow, narrow],
        out_specs=pl.BlockSpec((3, ATTN_BLOCK, CB), lambda s_: (0, row_of(s_ - 1), 0)),
        out_shape=_sds((3, t, CB), BF16),
        scratch_shapes=[pltpu.VMEM((3, ATTN_BLOCK, CB), F32),
                        pltpu.VMEM((HEADS_PER_GROUP, ATTN_BLOCK, 2 * ATTN_BLOCK), F32)],
        compiler_params=_cp("arbitrary"),
    )(z, z, z, z, z, dyc, lse, dd)


def odd_mix(os_, lses, z, conv_w, name, tm=256):
    t = z.shape[0]

    def body(o0, o1, o2, l0, l1, l2, gc_ref, db_ref, dc_ref, dx_ref, gd_ref, cw_ref,
             y_ref, yc_ref, lse_ref, conv_ref, zext):
        i = pl.program_id(0)
        a0, a1, a2 = l0[...], l1[...], l2[...]
        m = jnp.maximum(jnp.maximum(a0, a1), a2)
        tot = m + jnp.log(jnp.exp(a0 - m) + jnp.exp(a1 - m) + jnp.exp(a2 - m))
        lse_ref[...] = tot
        w0, w1, w2 = jnp.exp(a0 - tot), jnp.exp(a1 - tot), jnp.exp(a2 - tot)
        for h in range(HEADS_PER_GROUP):
            hs = slice(h * HEAD_DIM, (h + 1) * HEAD_DIM)
            yc = (w0[:, h:h + 1] * o0[:, hs].astype(F32) + w1[:, h:h + 1] * o1[:, hs].astype(F32)
                  + w2[:, h:h + 1] * o2[:, hs].astype(F32))
            yc_ref[:, hs] = yc.astype(BF16)
            silu_c, _ = _silu_and_grad(gc_ref[:, hs].astype(F32))
            y_ref[:, hs] = (yc * silu_c).astype(BF16)

        @pl.when(i == 0)
        def _():
            zext[0:HALO, :] = jnp.zeros((HALO, CB), F32)

        @pl.when(i > 0)
        def _():
            zext[0:HALO, :] = zext[tm:tm + HALO, :]

        zext[HALO:HALO + tm, :] = dc_ref[...].astype(F32) * dx_ref[...].astype(F32)
        conv = (cw_ref[0:1, :] * zext[HALO - 2:HALO - 2 + tm, :] + cw_ref[1:2, :] * zext[HALO - 1:HALO - 1 + tm, :]
                + cw_ref[2:3, :] * zext[HALO:HALO + tm, :])
        conv_ref[...] = conv.astype(BF16)
        silu_d, _ = _silu_and_grad(gd_ref[...].astype(F32))
        y_ref[:, CB:2 * CB] = (db_ref[...].astype(F32) * conv * silu_d).astype(BF16)

    row = pl.BlockSpec((tm, CB), lambda i: (i, 0))
    narrow = pl.BlockSpec((tm, HEAD_DIM), lambda i: (i, 0))
    zspec = lambda cb: pl.BlockSpec((tm, CB), lambda i, cb=cb: (i, cb))
    return pl.pallas_call(
        body, name=name, grid=(t // tm,),
        in_specs=[row] * 3 + [narrow] * 3 + [zspec(9), zspec(10), zspec(11), zspec(12), zspec(13),
                                             pl.BlockSpec(conv_w.shape, lambda i: (0, 0))],
        out_specs=[pl.BlockSpec((tm, 2 * CB), lambda i: (i, 0)), row, narrow, row],
        out_shape=[_sds((t, 2 * CB), BF16), _sds((t, CB), BF16), _sds((t, HEAD_DIM), F32), _sds((t, CB), BF16)],
        scratch_shapes=[pltpu.VMEM((HALO + tm, CB), F32)],
        compiler_params=_cp("arbitrary"),
    )(*os_, *lses, z, z, z, z, z, conv_w)


def odd_mix_bwd(dy, yc, conv, z, conv_w, name, tm=256):
    t = z.shape[0]
    nt = t // tm

    def body(dyc_ref, dyd_ref, yc_ref, conv_ref, gc_ref, db_ref, dc_ref, dx_ref, gd_ref, cw_ref,
             dz_ref, dyo_ref, dd_ref, dcw_ref, dcext):
        i = pl.program_id(0)

        @pl.when(i == 0)
        def _():
            dcext[tm:tm + HALO, :] = jnp.zeros((HALO, CB), F32)
            dcw_ref[...] = jnp.zeros_like(dcw_ref)

        @pl.when(i > 0)
        def _():
            dcext[tm:tm + HALO, :] = dcext[0:HALO, :]

        silu_c, dsilu_c = _silu_and_grad(gc_ref[...].astype(F32))
        dyc = dyc_ref[...].astype(F32)
        ycv = yc_ref[...].astype(F32)
        dyo = dyc * silu_c
        dyo_ref[...] = dyo.astype(BF16)
        dz_ref[:, 0:CB] = (dyc * ycv * dsilu_c).astype(BF16)
        prod = dyo * ycv
        dd_ref[...] = _lane_pack([jnp.sum(prod[:, h * HEAD_DIM:(h + 1) * HEAD_DIM], axis=1, keepdims=True)
                                  for h in range(HEADS_PER_GROUP)])

        silu_d, dsilu_d = _silu_and_grad(gd_ref[...].astype(F32))
        dyd = dyd_ref[...].astype(F32)
        convv = conv_ref[...].astype(F32)
        dbv = db_ref[...].astype(F32)
        dz_ref[:, CB:2 * CB] = (dyd * convv * silu_d).astype(BF16)
        dz_ref[:, 4 * CB:5 * CB] = (dyd * dbv * convv * dsilu_d).astype(BF16)
        dcext[0:tm, :] = dyd * dbv * silu_d
        dcv, dxv = dc_ref[...].astype(F32), dx_ref[...].astype(F32)
        zc = dcv * dxv
        d0, d1, d2 = dcext[0:tm, :], dcext[1:1 + tm, :], dcext[2:2 + tm, :]
        dzc = cw_ref[2:3, :] * d0 + cw_ref[1:2, :] * d1 + cw_ref[0:1, :] * d2
        dz_ref[:, 2 * CB:3 * CB] = (dzc * dxv).astype(BF16)
        dz_ref[:, 3 * CB:4 * CB] = (dzc * dcv).astype(BF16)
        dcw_ref[0:1, :] += jnp.sum(zc * d2, axis=0, keepdims=True)
        dcw_ref[1:2, :] += jnp.sum(zc * d1, axis=0, keepdims=True)
        dcw_ref[2:3, :] += jnp.sum(zc * d0, axis=0, keepdims=True)

    rev = lambda cb: pl.BlockSpec((tm, CB), lambda i, cb=cb: (nt - 1 - i, cb))
    return pl.pallas_call(
        body, name=name, grid=(nt,),
        in_specs=[rev(0), rev(1), rev(0), rev(0), rev(9), rev(10), rev(11), rev(12), rev(13),
                  pl.BlockSpec(conv_w.shape, lambda i: (0, 0))],
        out_specs=[pl.BlockSpec((tm, 5 * CB), lambda i: (nt - 1 - i, 0)), rev(0),
                   pl.BlockSpec((tm, HEAD_DIM), lambda i: (nt - 1 - i, 0)),
                   pl.BlockSpec(conv_w.shape, lambda i: (0, 0))],
        out_shape=[_sds((t, 5 * CB), BF16), _sds((t, CB), BF16), _sds((t, HEAD_DIM), F32), _sds(conv_w.shape, F32)],
        scratch_shapes=[pltpu.VMEM((tm + HALO, CB), F32)],
        compiler_params=_cp("arbitrary"),
    )(dy, dy, yc, conv, z, z, z, z, z, conv_w)


def out_proj_final_loss(y, w, x, g, target, name, tm=1024, rows=256):
    t, d = x.shape
    k = y.shape[1]

    def body(y_ref, w_ref, x_ref, g_ref, t_ref, dx_ref, loss_ref, dg_ref):
        i = pl.program_id(0)
        dx_ref[...] = x_ref[...] + _dot(y_ref[...], w_ref[...])
        gv = g_ref[...]
        loss = jnp.zeros((1, 1), F32)
        dg = jnp.zeros((1, d), F32)
        for c in range(tm // rows):
            rs = slice(c * rows, (c + 1) * rows)
            xv = dx_ref[rs, :]
            r = lax.rsqrt(jnp.mean(xv * xv, axis=-1, keepdims=True) + EPS)
            xn = xv * r
            err = xn * gv - t_ref[rs, :]
            loss = loss + 0.5 * jnp.sum(jnp.mean(err * err, axis=-1, keepdims=True), axis=0, keepdims=True)
            dy = err * (1.0 / d)
            dyg = dy * gv
            dx_ref[rs, :] = r * (dyg - xn * jnp.mean(dyg * xn, axis=-1, keepdims=True))
            dg = dg + jnp.sum(dy * xn, axis=0, keepdims=True)

        @pl.when(i == 0)
        def _():
            loss_ref[...] = jnp.broadcast_to(loss, loss_ref.shape)
            dg_ref[...] = dg

        @pl.when(i > 0)
        def _():
            loss_ref[...] += jnp.broadcast_to(loss, loss_ref.shape)
            dg_ref[...] += dg

    row = pl.BlockSpec((tm, d), lambda i: (i, 0))
    return pl.pallas_call(
        body, name=name, grid=(t // tm,),
        in_specs=[pl.BlockSpec((tm, k), lambda i: (i, 0)), pl.BlockSpec((k, d), lambda i: (0, 0)), row,
                  pl.BlockSpec((1, d), lambda i: (0, 0)), row],
        out_specs=[row, pl.BlockSpec((8, 128), lambda i: (0, 0)), pl.BlockSpec((1, d), lambda i: (0, 0))],
        out_shape=[_sds((t, d), F32), _sds((8, 128), F32), _sds((1, d), F32)],
        compiler_params=_cp("arbitrary"),
    )(y, w, x, g, target)


def _position():
    x, y, c = lax.axis_index("x"), lax.axis_index("y"), lax.axis_index("c")
    return x, y, c, 4 * x + 2 * y + c


def _peer(x, y, c, k):
    px = 1 - x if k & 4 else x
    py = 1 - y if k & 2 else y
    pc = 1 - c if k & 1 else c
    return (px, py, pc), 4 * px + 2 * py + pc


def _block(ref, axis, size, idx):
    index = [slice(None)] * len(ref.shape)
    index[axis] = pl.ds(idx * size, size)
    return ref.at[tuple(index)]


def all_gather(shards, axes, name):
    n = len(shards)
    fulls = []
    for s, ax in zip(shards, axes):
        shape = list(s.shape)
        shape[ax] *= N_DEV
        fulls.append(_sds(tuple(shape), s.dtype))

    def body(*refs):
        ins, outs = refs[:n], refs[n:2 * n]
        send_sems, recv_sems, local_sems = refs[2 * n:]
        x, y, c, me = _position()
        local = []
        for p in range(n):
            size = ins[p].shape[axes[p]]
            cp = pltpu.make_async_copy(ins[p], _block(outs[p], axes[p], size, me), local_sems.at[p])
            cp.start()
            local.append(cp)
        for k in range(1, N_DEV):
            peer, _ = _peer(x, y, c, k)
            for p in range(n):
                size = ins[p].shape[axes[p]]
                pltpu.make_async_remote_copy(
                    src_ref=ins[p], dst_ref=_block(outs[p], axes[p], size, me),
                    send_sem=send_sems.at[p], recv_sem=recv_sems.at[p],
                    device_id=peer, device_id_type=MESH_ID).start()
        for p in range(n):
            size = ins[p].shape[axes[p]]
            seven = _block(outs[p], axes[p], (N_DEV - 1) * size, 0)
            pltpu.make_async_remote_copy(
                src_ref=seven, dst_ref=seven, send_sem=send_sems.at[p], recv_sem=recv_sems.at[p],
                device_id=(x, y, c), device_id_type=MESH_ID).wait()
            local[p].wait()

    any_spec = pl.BlockSpec(memory_space=pl.ANY)
    return pl.pallas_call(
        body, name=name,
        in_specs=[any_spec] * n, out_specs=[any_spec] * n, out_shape=fulls,
        scratch_shapes=[pltpu.SemaphoreType.DMA((n,)), pltpu.SemaphoreType.DMA((n,)), pltpu.SemaphoreType.DMA((n,))],
    )(*shards)


def _other_chips(x, y):
    return [(1 - x, y), (x, 1 - y), (1 - x, 1 - y)]


def _gather_copy(ins, lands, axes, send_sems, recv_sems, k, p, block, to, from_shard=False):
    dst = _block(lands[p], axes[p], lands[p].shape[axes[p]] // N_DEV, block)
    sem = k * len(lands) + p
    return pltpu.make_async_remote_copy(
        src_ref=ins[p] if from_shard else dst, dst_ref=dst, send_sem=send_sems.at[sem], recv_sem=recv_sems.at[sem],
        device_id=to, device_id_type=MESH_ID)


def _own_block_copy(ins, lands, axes, sems, p, me):
    n = len(lands)
    dst = _block(lands[p], axes[p], lands[p].shape[axes[p]] // N_DEV, me)
    return pltpu.make_async_copy(ins[p], dst, sems.at[4 * n + p])


def all_gather_two_level(shards, axes, name):
    n = len(shards)
    fulls = []
    for s, ax in zip(shards, axes):
        shape = list(s.shape)
        shape[ax] *= N_DEV
        fulls.append(_sds(tuple(shape), s.dtype))

    def body(*refs):
        ins, outs = refs[:n], refs[n:2 * n]
        send_sems, recv_sems, local_sems = refs[2 * n:]
        x, y, c, me = _position()
        sibling, here = (x, y, 1 - c), (x, y, c)
        chips = _other_chips(x, y)
        copy = functools.partial(_gather_copy, ins, outs, axes, send_sems, recv_sems)
        local = [pltpu.make_async_copy(ins[p], _block(outs[p], axes[p], ins[p].shape[axes[p]], me), local_sems.at[p])
                 for p in range(n)]
        sent = []
        for p in range(n):
            sent.append(copy(0, p, me, sibling, from_shard=True))
            sent += [copy(1 + j, p, me, (*chip, c), from_shard=True) for j, chip in enumerate(chips)]
        for cp in local + sent:
            cp.start()
        for j, (px, py) in enumerate(chips):
            for p in range(n):
                arrived = 4 * px + 2 * py + c
                copy(1 + j, p, arrived, here).wait_recv()
                sent.append(copy(4 + j, p, arrived, sibling))
                sent[-1].start()
        for p in range(n):
            copy(0, p, 4 * x + 2 * y + 1 - c, here).wait_recv()
            for j, (px, py) in enumerate(chips):
                copy(4 + j, p, 4 * px + 2 * py + 1 - c, here).wait_recv()
        for cp in sent:
            cp.wait_send()
        for cp in local:
            cp.wait()

    any_spec = pl.BlockSpec(memory_space=pl.ANY)
    return pl.pallas_call(
        body, name=name,
        in_specs=[any_spec] * n, out_specs=[any_spec] * n, out_shape=fulls,
        scratch_shapes=[pltpu.SemaphoreType.DMA((7 * n,)), pltpu.SemaphoreType.DMA((7 * n,)), pltpu.SemaphoreType.DMA((n,))],
    )(*shards)


def gather_two_level_start(shards, axes, after, name):
    n = len(shards)
    lands = []
    for s, ax in zip(shards, axes):
        shape = list(s.shape)
        shape[ax] *= N_DEV
        lands.append(lax.empty(tuple(shape), s.dtype))

    def body(*refs):
        ins, land = refs[:n], refs[n:2 * n]
        send_sems, recv_sems, token = refs[2 * n + 1], refs[2 * n + 2], refs[-1]
        x, y, c, me = _position()
        copy = functools.partial(_gather_copy, ins, land, axes, send_sems, recv_sems)
        for p in range(n):
            copy(0, p, me, (x, y, 1 - c), from_shard=True).start()
            for j, chip in enumerate(_other_chips(x, y)):
                copy(1 + j, p, me, (*chip, c), from_shard=True).start()
            _own_block_copy(ins, land, axes, send_sems, p, me).start()
        token[...] = jnp.zeros_like(token)

    outs = pl.pallas_call(
        body, name=name,
        out_shape=(pltpu.SemaphoreType.DMA((5 * n,)), pltpu.SemaphoreType.DMA((4 * n,)),
                   *[pltpu.HBM(s.shape, s.dtype) for s in shards], *[pltpu.HBM(l.shape, l.dtype) for l in lands],
                   _sds((8, 128), F32)),
        in_specs=[_HBM] * (2 * n) + [pl.BlockSpec(memory_space=pl.ANY)],
        out_specs=(_SEM, _SEM, *[_HBM] * (2 * n), pl.BlockSpec(memory_space=pltpu.VMEM)),
        input_output_aliases={i: 2 + i for i in range(2 * n)},
        compiler_params=pltpu.CompilerParams(has_side_effects=_EFFECT),
    )(*[_in_hbm(s) for s in shards], *[_in_hbm(l) for l in lands], after)
    return outs[0], outs[1], list(outs[2:2 + n]), list(outs[2 + n:2 + 2 * n]), outs[-1]


def gather_two_level_forward(started, axes, after, name):
    send_a, recv_a, shards, lands, _ = started
    n = len(shards)

    def body(*refs):
        land = refs[:n]
        send_first, recv_first = refs[n], refs[n + 1]
        send_fwd, recv_fwd, token = refs[n + 3], refs[n + 4], refs[-1]
        x, y, c, _ = _position()
        for j, (px, py) in enumerate(_other_chips(x, y)):
            for p in range(n):
                arrived = 4 * px + 2 * py + c
                _gather_copy(None, land, axes, send_first, recv_first, 1 + j, p, arrived, (x, y, c)).wait_recv()
                _gather_copy(None, land, axes, send_fwd, recv_fwd, j, p, arrived, (x, y, 1 - c)).start()
        token[...] = jnp.zeros_like(token)

    outs = pl.pallas_call(
        body, name=name,
        out_shape=(pltpu.SemaphoreType.DMA((3 * n,)), pltpu.SemaphoreType.DMA((3 * n,)),
                   *[pltpu.HBM(l.shape, l.dtype) for l in lands], _sds((8, 128), F32)),
        in_specs=[_HBM] * n + [_SEM, _SEM, pl.BlockSpec(memory_space=pl.ANY)],
        out_specs=(_SEM, _SEM, *[_HBM] * n, pl.BlockSpec(memory_space=pltpu.VMEM)),
        input_output_aliases={i: 2 + i for i in range(n)},
        compiler_params=pltpu.CompilerParams(has_side_effects=_EFFECT),
    )(*lands, send_a, recv_a, after)
    return send_a, recv_a, outs[0], outs[1], shards, list(outs[2:2 + n]), outs[-1]


def gather_two_level_wait(forwarded, axes, after, name):
    send_a, recv_a, send_f, recv_f, shards, lands, _ = forwarded
    n = len(shards)

    def body(*refs):
        ins, land = refs[:n], refs[n:2 * n]
        sa, ra, sf, rf = refs[2 * n:2 * n + 4]
        x, y, c, me = _position()
        here = (x, y, c)
        chips = _other_chips(x, y)
        for p in range(n):
            for k in range(4):
                _gather_copy(ins, land, axes, sa, ra, k, p, me, here, from_shard=True).wait_send()
            _own_block_copy(ins, land, axes, sa, p, me).wait()
            _gather_copy(ins, land, axes, sa, ra, 0, p, 4 * x + 2 * y + 1 - c, here).wait_recv()
            for j, (px, py) in enumerate(chips):
                _gather_copy(ins, land, axes, sf, rf, j, p, 4 * px + 2 * py + c, here).wait_send()
                _gather_copy(ins, land, axes, sf, rf, j, p, 4 * px + 2 * py + 1 - c, here).wait_recv()

    outs = pl.pallas_call(
        body, name=name,
        out_shape=(*[pltpu.HBM(s.shape, s.dtype) for s in shards], *[pltpu.HBM(l.shape, l.dtype) for l in lands]),
        in_specs=[_HBM] * (2 * n) + [_SEM] * 4 + [pl.BlockSpec(memory_space=pl.ANY)],
        out_specs=tuple([_HBM] * (2 * n)),
        input_output_aliases={i: i for i in range(2 * n)},
        compiler_params=pltpu.CompilerParams(has_side_effects=_EFFECT),
    )(*shards, *lands, send_a, recv_a, send_f, recv_f, after)
    return list(outs[n:])


_HBM = pl.BlockSpec(memory_space=pltpu.HBM)
_SEM = pl.BlockSpec(memory_space=pltpu.SEMAPHORE)
_EFFECT = pltpu.SideEffectType.DATAFLOW_SIDE_EFFECTING


def _in_hbm(a):
    return pltpu.with_memory_space_constraint(a, pltpu.HBM)


def _landing(mode, src, axis):
    me = 4 * lax.axis_index("x") + 2 * lax.axis_index("y") + lax.axis_index("c")
    if mode == "gather":
        shape = list(src.shape)
        size = shape[axis]
        shape[axis] *= N_DEV
        return lax.dynamic_update_slice_in_dim(lax.empty(tuple(shape), src.dtype), src, me * size, axis)
    size = src.shape[axis] // N_DEV
    own = lax.dynamic_slice_in_dim(src, me * size, size, axis)
    return lax.dynamic_update_slice_in_dim(lax.empty((N_DEV, *own.shape), src.dtype), own[None], me, 0)


def _seven(mode, land_ref, axis):
    if mode == "gather":
        return _block(land_ref, axis, (N_DEV - 1) * (land_ref.shape[axis] // N_DEV), 0)
    return land_ref.at[pl.ds(0, N_DEV - 1)]


def exchange_start(mode, srcs, axes, after, name):
    n = len(srcs)
    lands = [_landing(mode, s, ax) for s, ax in zip(srcs, axes)]

    def body(*refs):
        src, land = refs[:n], refs[n:2 * n]
        send_sems, recv_sems = refs[2 * n + 1], refs[2 * n + 2]
        token = refs[-1]
        x, y, c, me = _position()
        for k in range(1, N_DEV):
            peer, pj = _peer(x, y, c, k)
            for p in range(n):
                if mode == "gather":
                    s = src[p]
                    dst = _block(land[p], axes[p], src[p].shape[axes[p]], me)
                else:
                    s = _block(src[p], axes[p], src[p].shape[axes[p]] // N_DEV, pj)
                    dst = land[p].at[me]
                pltpu.make_async_remote_copy(
                    src_ref=s, dst_ref=dst, send_sem=send_sems.at[p], recv_sem=recv_sems.at[p],
                    device_id=peer, device_id_type=MESH_ID).start()
        token[...] = jnp.zeros_like(token)

    outs = pl.pallas_call(
        body, name=name,
        out_shape=(pltpu.SemaphoreType.DMA((n,)), pltpu.SemaphoreType.DMA((n,)),
                   *[pltpu.HBM(s.shape, s.dtype) for s in srcs], *[pltpu.HBM(l.shape, l.dtype) for l in lands],
                   _sds((8, 128), F32)),
        in_specs=[_HBM] * (2 * n) + [pl.BlockSpec(memory_space=pl.ANY)],
        out_specs=(_SEM, _SEM, *[_HBM] * (2 * n), pl.BlockSpec(memory_space=pltpu.VMEM)),
        input_output_aliases={i: 2 + i for i in range(2 * n)},
        compiler_params=pltpu.CompilerParams(has_side_effects=_EFFECT),
    )(*[_in_hbm(s) for s in srcs], *[_in_hbm(l) for l in lands], after)
    return outs[0], outs[1], list(outs[2:2 + n]), list(outs[2 + n:2 + 2 * n]), outs[-1]


def exchange_wait(mode, started, axes, after, name):
    send_sems, recv_sems, srcs, lands, _ = started
    n = len(srcs)

    def body(*refs):
        land = refs[n:2 * n]
        send_ref, recv_ref = refs[2 * n], refs[2 * n + 1]
        x, y, c, _ = _position()
        for p in range(n):
            seven = _seven(mode, land[p], axes[p])
            cp = pltpu.make_async_remote_copy(
                src_ref=seven, dst_ref=seven, send_sem=send_ref.at[p], recv_sem=recv_ref.at[p],
                device_id=(x, y, c), device_id_type=MESH_ID)
            cp.wait_send()
            cp.wait_recv()

    outs = pl.pallas_call(
        body, name=name,
        out_shape=(*[pltpu.HBM(s.shape, s.dtype) for s in srcs], *[pltpu.HBM(l.shape, l.dtype) for l in lands]),
        in_specs=[_HBM] * (2 * n) + [_SEM, _SEM, pl.BlockSpec(memory_space=pl.ANY)],
        out_specs=tuple([_HBM] * (2 * n)),
        input_output_aliases={i: i for i in range(2 * n)},
        compiler_params=pltpu.CompilerParams(has_side_effects=_EFFECT),
    )(*srcs, *lands, send_sems, recv_sems, after)
    return list(outs[n:])


def _adam_math(g, w, m, v):
    m2 = ADAM_B1 * m + (1.0 - ADAM_B1) * g
    v2 = ADAM_B2 * v + (1.0 - ADAM_B2) * (g * g)
    m_hat = m2 / (1.0 - ADAM_B1 ** ADAM_STEP)
    v_hat = v2 / (1.0 - ADAM_B2 ** ADAM_STEP)
    delta = -ADAM_LR * (m_hat / (jnp.sqrt(v_hat) + ADAM_EPS) + ADAM_WD * w)
    return delta, m2, v2


def adamw(g, w, m, v, name, slots, tr=128):
    r, c = w.shape
    tr = min(tr, r)

    def body(g_ref, w_ref, m_ref, v_ref, go_ref, d_ref, mo_ref, vo_ref):
        if slots:
            gs = g_ref[0].astype(F32)
            for s in range(1, N_DEV):
                gs = gs + g_ref[s].astype(F32)
        else:
            gs = g_ref[...]
        go_ref[...] = gs
        d_ref[...], mo_ref[...], vo_ref[...] = _adam_math(gs, w_ref[...], m_ref[...], v_ref[...])

    row = pl.BlockSpec((tr, c), lambda i: (i, 0))
    gspec = pl.BlockSpec((N_DEV, tr, c), lambda i: (0, i, 0)) if slots else row
    return pl.pallas_call(
        body, name=name, grid=(r // tr,),
        in_specs=[gspec, row, row, row], out_specs=[row] * 4, out_shape=[_sds((r, c), F32)] * 4,
        compiler_params=_cp("parallel"),
    )(g, w, m, v)


def sum_slots(g, name):
    _, r, c = g.shape

    def body(g_ref, o_ref):
        gs = g_ref[0]
        for s in range(1, N_DEV):
            gs = gs + g_ref[s]
        o_ref[...] = gs

    return pl.pallas_call(
        body, name=name, grid=(1,),
        in_specs=[pl.BlockSpec((N_DEV, r, c), lambda i: (0, 0, 0))],
        out_specs=pl.BlockSpec((r, c), lambda i: (0, 0)), out_shape=_sds((r, c), F32),
        compiler_params=_cp("arbitrary"),
    )(g)


def _rows128(a, pad_to=8):
    a = a.reshape(-1, 128)
    pad = (-a.shape[0]) % pad_to
    return jnp.pad(a, ((0, pad), (0, 0))) if pad else a


def kernel(x, even_norm, even_w_in, even_pool_w, even_pool_scale, even_ws, even_bs, even_w_out, odd_norm, odd_w_in, odd_conv_w, odd_w_out, final_norm, loss_target, m_even_norm, m_even_w_in, m_even_pool_w, m_even_pool_scale, m_even_ws, m_even_bs, m_even_w_out, m_odd_norm, m_odd_w_in, m_odd_conv_w, m_odd_w_out, m_final_norm, v_even_norm, v_even_w_in, v_even_pool_w, v_even_pool_scale, v_even_ws, v_even_bs, v_even_w_out, v_odd_norm, v_odd_w_in, v_odd_conv_w, v_odd_w_out, v_final_norm):
    x0 = x[0]
    target = loss_target[0]
    me = 4 * lax.axis_index("x") + 2 * lax.axis_index("y") + lax.axis_index("c")

    (we_in,) = all_gather_two_level([even_w_in[0].astype(BF16)], [1], "gather_even_in")
    odd_small = jnp.pad(odd_norm, ((0, 7), (0, 0))) + jnp.pad(odd_conv_w[0], ((1, 4), (0, 0)))
    rest_axes, odd_axes = [1, 0, 1], [1, 0]
    rest_flight = exchange_start("gather", [even_pool_w[0].astype(BF16), even_w_out[0].astype(BF16), odd_small],
                                 rest_axes, we_in, "gather_even_rest_start")
    odd_flight = gather_two_level_start([odd_w_in[0].astype(BF16), odd_w_out[0].astype(BF16)],
                                        odd_axes, rest_flight[4], "gather_odd_start")
    scale = even_pool_scale
    ws = even_ws[0]
    bs_col = even_bs[0][:, :, None]
    g_fin = final_norm[None, :]

    z_e, h_e = norm_matmul(x0, even_norm + odd_flight[4][0:1, 0:1], we_in, "even_in", tn=1280)
    wp, we_out, odd_small = exchange_wait("gather", rest_flight, rest_axes, z_e, "gather_even_rest_wait")
    g_odd, conv_w = odd_small[0:1], odd_small[1:4]
    y_e, pooled = even_mix(z_e, wp, scale, ws, bs_col, "even_mix")
    x1, h_o = matmul_residual(y_e, we_out, x0, "even_out", next_gain=g_odd)
    odd_flight = gather_two_level_forward(odd_flight, odd_axes, x1, "gather_odd_forward")
    h_layouts = [h_o] + [permute_mats(h_o, [], gi, False, f"dilate_h_{gi}")[0] for gi in (1, 2)]
    wo_in, wo_out = gather_two_level_wait(odd_flight, odd_axes, h_layouts[2], "gather_odd_wait")
    z_o, wo_in_t = matmul_layouts(h_layouts, wo_in, "odd_in")
    attn = [attn_fwd(z_o, gi, f"attn_fwd_{gi}") for gi in range(3)]
    for gi in (1, 2):
        attn[gi] = permute_mats(attn[gi][0], [attn[gi][1]], gi, True, f"undilate_attn_{gi}")
    y_o, yc, lse_tot, conv = odd_mix([a[0] for a in attn], [a[1] for a in attn], z_o, conv_w, "odd_mix")
    dx2, loss_blk, dg_fin = out_proj_final_loss(y_o, wo_out, x1, g_fin, target, "odd_out_loss")

    dy_o, dwo_out = out_proj_bwd(dx2, wo_out.T, y_o, "odd_out_bwd")
    dz5_o, dyc, dd, dconv_w = odd_mix_bwd(dy_o, yc, conv, z_o, conv_w, "odd_mix_bwd")
    dqkv = [attn_bwd(z_o, dyc, lse_tot, dd, 0, "attn_bwd_0")]
    for gi in (1, 2):
        dyc_g, lse_g, dd_g = permute_mats(dyc, [lse_tot, dd], gi, False, f"dilate_dy_{gi}")
        dqkv.append(permute_slabs(attn_bwd(z_o, dyc_g, lse_g, dd_g, gi, f"attn_bwd_{gi}"),
                                  gi, True, f"undilate_dqkv_{gi}"))
    src_o = [(dqkv[0], ("slabs", 0)), (dqkv[1], ("slabs", 1)), (dqkv[2], ("slabs", 2)), (dz5_o, ("cols", 9, 5))]
    dwo_in = in_proj_dw(h_o, src_o, ODD_COLBLOCKS, "odd_in_dw")
    odd_grads = exchange_start("scatter", [dwo_in, dwo_out], [1, 0], dconv_w, "scatter_odd_start")
    dx1, dg_odd = in_proj_dx(src_o, ODD_COLBLOCKS, wo_in_t, x1, g_odd + odd_grads[4][0:1, 0:1], dx2, "odd_in_dx")

    dy_e, dwe_out = out_proj_bwd(dx1, we_out.T, y_e, "even_out_bwd")
    dz_e, dwp, dscale, dws, dbs_col = even_mix_bwd(dy_e, z_e, pooled, wp, scale, ws, bs_col, "even_mix_bwd")
    src_e = [(dz_e, ("cols", 0, 5))]
    even_grads_a = exchange_start("scatter", [dwp.astype(BF16), dwe_out], [1, 0], dscale, "scatter_even_rest_start")
    dwe_in = in_proj_dw(h_e, src_e, 5, "even_in_dw")
    even_grads_b = exchange_start("scatter", [dwe_in], [1], even_grads_a[4], "scatter_even_in_start")
    dx0, dg_even = in_proj_dx(src_e, 5, we_in.T, x0, even_norm + even_grads_b[4][0:1, 0:1], dx1, "even_in_dx")
    small = jnp.concatenate([
        _rows128(dg_even), _rows128(dscale), _rows128(dws), _rows128(dbs_col), _rows128(dg_fin),
        _rows128(dg_odd), _rows128(dconv_w), loss_blk], axis=0)
    n_small = small.shape[0]
    small_flight = exchange_start("gather", [small], [0], dg_fin, "gather_small_start")
    p_wo_in, p_wo_out = exchange_wait("scatter", odd_grads, [1, 0], small_flight[4], "scatter_odd_wait")
    p_wp, p_we_out = exchange_wait("scatter", even_grads_a, [1, 0], small_flight[4], "scatter_even_rest_wait")
    (p_we_in,) = exchange_wait("scatter", even_grads_b, [1], small_flight[4], "scatter_even_in_wait")

    two_d = lambda a, r, c: a.reshape(r, c)
    sharded = [
        ("even_w_in", p_we_in, even_w_in, m_even_w_in, v_even_w_in, (1024, 640)),
        ("even_pool_w", p_wp.reshape(N_DEV, 128, 256), even_pool_w, m_even_pool_w, v_even_pool_w, (128, 256)),
        ("even_w_out", p_we_out, even_w_out, m_even_w_out, v_even_w_out, (256, 1024)),
        ("odd_w_in", p_wo_in, odd_w_in, m_odd_w_in, v_odd_w_in, (1024, 1792)),
        ("odd_w_out", p_wo_out, odd_w_out, m_odd_w_out, v_odd_w_out, (256, 1024)),
    ]
    res = {}
    for name, g, w, m, v, (r, c) in sharded:
        outs = adamw(g, two_d(w, r, c), two_d(m, r, c), two_d(v, r, c), "adamw_" + name, slots=True)
        res[name] = [o.reshape(w.shape) for o in outs]

    (small_all,) = exchange_wait("gather", small_flight, [0], res["odd_w_out"][0], "gather_small_wait")
    small_sum = sum_slots(small_all.reshape(N_DEV, n_small, 128), "sum_small_grads")
    g_even_norm = small_sum[0:8].reshape(1, 1024)
    g_scale = small_sum[8:16].reshape(1, 1024)
    g_ws = small_sum[16:528]
    g_bs = small_sum[528:532]
    g_final = small_sum[536:544].reshape(1, 1024)
    g_odd_norm = lax.dynamic_slice_in_dim(small_sum[544:552], me, 1, axis=0)
    g_conv = lax.dynamic_index_in_dim(small_sum[552:576].reshape(3, 8, 128), me, axis=1, keepdims=False)
    plain = [
        ("even_norm", g_even_norm, even_norm, m_even_norm, v_even_norm, (1, 1024)),
        ("even_pool_scale", g_scale, even_pool_scale, m_even_pool_scale, v_even_pool_scale, (1, 1024)),
        ("even_ws", g_ws, even_ws, m_even_ws, v_even_ws, (512, 128)),
        ("even_bs", g_bs, even_bs, m_even_bs, v_even_bs, (4, 128)),
        ("odd_norm", g_odd_norm, odd_norm, m_odd_norm, v_odd_norm, (1, 128)),
        ("odd_conv_w", g_conv, odd_conv_w, m_odd_conv_w, v_odd_conv_w, (3, 128)),
        ("final_norm", g_final, final_norm, m_final_norm, v_final_norm, (1, 1024)),
    ]
    for name, g, w, m, v, (r, c) in plain:
        outs = adamw(two_d(g, r, c), two_d(w, r, c), two_d(m, r, c), two_d(v, r, c), "adamw_" + name, slots=False)
        res[name] = [o.reshape(w.shape) for o in outs]

    loss = small_sum[576, 0]
    order = ["even_norm", "even_w_in", "even_pool_w", "even_pool_scale", "even_ws", "even_bs", "even_w_out",
             "odd_norm", "odd_w_in", "odd_conv_w", "odd_w_out", "final_norm"]
    return (loss, dx0[None], *[res[n][0] for n in order], *[res[n][1] for n in order],
            *[res[n][2] for n in order], *[res[n][3] for n in order])
```

```python
import functools

import jax
import jax.numpy as jnp
from jax import lax
from jax.experimental import pallas as pl
from jax.experimental.pallas import tpu as pltpu

F32 = jnp.float32
BF16 = jnp.bfloat16
MESH_ID = pl.DeviceIdType.MESH

EPS = 1e-6
NEG = -1e30
N_DEV = 8
POOL_SIZES = (2, 4, 8, 16)
DILATIONS = (1, 4, 16)
N_HEADS = 24
HEADS_PER_GROUP = 8
HEAD_DIM = 128
ATTN_BLOCK = 128
CHUNK = 128
CB = 1024
HALO = 16
ODD_COLBLOCKS = 14
ADAM_LR = 0.001
ADAM_B1 = 0.9
ADAM_B2 = 0.999
ADAM_EPS = 1e-08
ADAM_WD = 0.01
ADAM_STEP = 10
VMEM_LIMIT = 52 * 1024 * 1024


def _cp(*sem):
    return pltpu.CompilerParams(dimension_semantics=sem, vmem_limit_bytes=VMEM_LIMIT)


def _dot(a, b):
    return jnp.dot(a, b, preferred_element_type=F32)


def _dot_nt(a, b):
    return lax.dot_general(a, b, (((1,), (1,)), ((), ())), preferred_element_type=F32)


def _dot_tn(a, b):
    return lax.dot_general(a, b, (((0,), (0,)), ((), ())), preferred_element_type=F32)


def _sigmoid(x):
    return 0.5 * jnp.tanh(0.5 * x) + 0.5


def _silu_and_grad(x):
    s = _sigmoid(x)
    return x * s, s * (1.0 + x * (1.0 - s))


def _sds(shape, dtype):
    return jax.ShapeDtypeStruct(shape, dtype)


def norm_matmul(x, g, w, name, tn, tm=1024, rows=256):
    t, d = x.shape
    n = w.shape[1]
    n_col = n // tn

    def body(x_ref, g_ref, w_ref, z_ref, ht_ref, wt_ref, h_ref):
        @pl.when(pl.program_id(1) == 0)
        def _():
            for c in range(tm // rows):
                rs = slice(c * rows, (c + 1) * rows)
                xv = x_ref[rs, :]
                r = lax.rsqrt(jnp.mean(xv * xv, axis=-1, keepdims=True) + EPS)
                h_ref[rs, :] = ((xv * r) * g_ref[...]).astype(BF16)
            ht_ref[...] = h_ref[...].T

        z_ref[...] = _dot(h_ref[...], w_ref[...]).astype(BF16)

        @pl.when(pl.program_id(0) == 0)
        def _():
            wt_ref[...] = w_ref[...].T

    return pl.pallas_call(
        body, name=name, grid=(t // tm, n_col),
        in_specs=[pl.BlockSpec((tm, d), lambda i, j: (i, 0)),
                  pl.BlockSpec((1, d), lambda i, j: (0, 0)),
                  pl.BlockSpec((d, tn), lambda i, j: (0, j))],
        out_specs=[pl.BlockSpec((tm, tn), lambda i, j: (i, j)),
                   pl.BlockSpec((d, tm), lambda i, j: (0, i)),
                   pl.BlockSpec((tn, d), lambda i, j: (jnp.where(i == 0, j, n_col), 0))],
        out_shape=[_sds((t, n), BF16), _sds((d, t), BF16), _sds((n + tn, d), BF16)],
        scratch_shapes=[pltpu.VMEM((tm, d), BF16)],
        compiler_params=_cp("arbitrary", "arbitrary"),
    )(x, g, w)


def matmul_residual(y, w, x, name, next_gain=None, tm=1024, rows=256):
    t, k = y.shape
    d = w.shape[1]

    def body(y_ref, w_ref, x_ref, *rest):
        o_ref = rest[-1] if next_gain is None else rest[-3]
        o_ref[...] = x_ref[...] + _dot(y_ref[...], w_ref[...])
        if next_gain is not None:
            g_ref, h_ref, ht_ref = rest[0], rest[-2], rest[-1]
            for c in range(tm // rows):
                rs = slice(c * rows, (c + 1) * rows)
                xv = o_ref[rs, :]
                r = lax.rsqrt(jnp.mean(xv * xv, axis=-1, keepdims=True) + EPS)
                h_ref[rs, :] = ((xv * r) * g_ref[...]).astype(BF16)
            ht_ref[...] = h_ref[...].T

    row = pl.BlockSpec((tm, d), lambda i: (i, 0))
    in_specs = [pl.BlockSpec((tm, k), lambda i: (i, 0)), pl.BlockSpec((k, d), lambda i: (0, 0)), row]
    if next_gain is None:
        return pl.pallas_call(
            body, name=name, grid=(t // tm,), in_specs=in_specs, out_specs=row, out_shape=_sds((t, d), F32),
            compiler_params=_cp("parallel"),
        )(y, w, x)
    return pl.pallas_call(
        body, name=name, grid=(t // tm,),
        in_specs=in_specs + [pl.BlockSpec((1, d), lambda i: (0, 0))],
        out_specs=[row, row, pl.BlockSpec((d, tm), lambda i: (0, i))],
        out_shape=[_sds((t, d), F32), _sds((t, d), BF16), _sds((d, t), BF16)],
        compiler_params=_cp("parallel"),
    )(y, w, x, next_gain)


def matmul_layouts(hs, w, name, tm=2048):
    t, d = hs[0].shape
    n_blocks = w.shape[1] // CB

    def layout_of(j):
        return jnp.where(j < 9, j // 3, 0)

    def column_of(j):
        return jnp.where(j < 9, (j % 3) * 3 + j // 3, j)

    def body(h0, h1, h2, w_ref, z_ref, wt_ref):
        layout = layout_of(pl.program_id(1))
        for v, h_ref in enumerate((h0, h1, h2)):
            @pl.when(layout == v)
            def _(h_ref=h_ref):
                z_ref[...] = _dot(h_ref[...], w_ref[...]).astype(BF16)
        @pl.when(pl.program_id(0) == 0)
        def _():
            wt_ref[...] = w_ref[...].T

    row = pl.BlockSpec((tm, d), lambda i, j: (i, 0))
    wt_block = lambda i, j: (jnp.where(i == 0, column_of(j), n_blocks), 0)
    return pl.pallas_call(
        body, name=name, grid=(t // tm, n_blocks),
        in_specs=[row, row, row, pl.BlockSpec((d, CB), lambda i, j: (0, column_of(j)))],
        out_specs=[pl.BlockSpec((tm, CB), lambda i, j: (i, column_of(j))), pl.BlockSpec((CB, d), wt_block)],
        out_shape=[_sds((t, n_blocks * CB), BF16), _sds(((n_blocks + 1) * CB, d), BF16)],
        compiler_params=_cp("arbitrary", "arbitrary"),
    )(*hs, w)


def out_proj_bwd(dx, wt, y, name, tm=512):
    t, d = dx.shape
    k = wt.shape[1]
    steps = t // tm
    half = d // 2

    def body(dx_ref, wt_ref, y_ref, dy_ref, dw_ref, acc):
        i = pl.program_id(0)
        dxb = dx_ref[...].astype(BF16)
        dy_ref[...] = _dot(dxb, wt_ref[...]).astype(BF16)

        @pl.when(i == 0)
        def _():
            acc[...] = jnp.zeros_like(acc)

        for c in range(2):
            acc[:, c * half:(c + 1) * half] += _dot_tn(y_ref[...], dxb[:, c * half:(c + 1) * half])

        @pl.when(i == steps - 1)
        def _():
            dw_ref[...] = acc[...].astype(BF16)

    return pl.pallas_call(
        body, name=name, grid=(steps,),
        in_specs=[pl.BlockSpec((tm, d), lambda i: (i, 0)),
                  pl.BlockSpec((d, k), lambda i: (0, 0)),
                  pl.BlockSpec((tm, k), lambda i: (i, 0))],
        out_specs=[pl.BlockSpec((tm, k), lambda i: (i, 0)),
                   pl.BlockSpec((k, d), lambda i: (0, 0))],
        out_shape=[_sds((t, k), BF16), _sds((k, d), BF16)],
        scratch_shapes=[pltpu.VMEM((k, d), F32)],
        compiler_params=_cp("arbitrary"),
    )(dx, wt, y)


def _source_block(kind, rows):
    if kind[0] == "cols":
        return (rows, CB)
    return (None, rows, CB)


def _source_active(kind, j):
    if kind[0] == "cols":
        _, first, n = kind
        return (j >= first) & (j < first + n), jnp.clip(j - first, 0, n - 1)
    _, group = kind
    return (j < 9) & (j % 3 == group), jnp.clip(j // 3, 0, 2)


def _source_index(kind, row_block, inner):
    if kind[0] == "cols":
        return (row_block, inner)
    return (inner, row_block, 0)


def in_proj_dw(ht, sources, n_blocks, name, tk=1024):
    d, t = ht.shape
    steps = t // tk
    ns = len(sources)

    def body(*refs):
        h_ref, src_refs, dw_ref, acc = refs[0], refs[1:1 + ns], refs[1 + ns], refs[2 + ns]
        j, k = pl.program_id(0), pl.program_id(1)

        @pl.when(k == 0)
        def _():
            acc[...] = jnp.zeros_like(acc)

        for s, (_, kind) in enumerate(sources):
            active, _ = _source_active(kind, j)

            @pl.when(active)
            def _(s=s):
                acc[...] += _dot(h_ref[...], src_refs[s][...])

        @pl.when(k == steps - 1)
        def _():
            dw_ref[...] = acc[...].astype(BF16)

    def src_spec(kind):
        def index(j, k):
            active, inner = _source_active(kind, j)
            return _source_index(kind, jnp.where(active, k, 0), inner)
        return pl.BlockSpec(_source_block(kind, tk), index)

    return pl.pallas_call(
        body, name=name, grid=(n_blocks, steps),
        in_specs=[pl.BlockSpec((d, tk), lambda j, k: (0, k))] + [src_spec(kind) for _, kind in sources],
        out_specs=pl.BlockSpec((d, CB), lambda j, k: (0, j)),
        out_shape=_sds((d, n_blocks * CB), BF16),
        scratch_shapes=[pltpu.VMEM((d, CB), F32)],
        compiler_params=_cp("parallel", "arbitrary"),
    )(ht, *[a for a, _ in sources])


def in_proj_dx(sources, n_blocks, wt, x, g, dres, name, tm=1024, rows=256):
    t, d = x.shape
    ns = len(sources)

    def body(*refs):
        src_refs = refs[:ns]
        w_ref, x_hbm, g_ref, dres_hbm, dx_ref, dg_ref, acc, x_ref, dres_ref, sems = refs[ns:]
        i, p = pl.program_id(0), pl.program_id(1)
        tile = pl.ds(pl.multiple_of(i * tm, tm), tm)
        fetch = [pltpu.make_async_copy(x_hbm.at[tile, :], x_ref, sems.at[0]),
                 pltpu.make_async_copy(dres_hbm.at[tile, :], dres_ref, sems.at[1])]

        @pl.when(p == 0)
        def _():
            acc[...] = jnp.zeros_like(acc)
            for cp in fetch:
                cp.start()

        for s, (_, kind) in enumerate(sources):
            active, _ = _source_active(kind, p)

            @pl.when(active)
            def _(s=s):
                acc[...] += _dot(src_refs[s][...], w_ref[...])

        @pl.when(p == n_blocks - 1)
        def _():
            for cp in fetch:
                cp.wait()
            part = jnp.zeros((1, d), F32)
            for c in range(tm // rows):
                rs = slice(c * rows, (c + 1) * rows)
                xv = x_ref[rs, :]
                r = lax.rsqrt(jnp.mean(xv * xv, axis=-1, keepdims=True) + EPS)
                xn = xv * r
                dh = acc[rs, :]
                dhg = dh * g_ref[...]
                dx_ref[rs, :] = dres_ref[rs, :] + r * (dhg - xn * jnp.mean(dhg * xn, axis=-1, keepdims=True))
                part = part + jnp.sum(dh * xn, axis=0, keepdims=True)

            @pl.when(i == 0)
            def _():
                dg_ref[...] = part

            @pl.when(i > 0)
            def _():
                dg_ref[...] += part

    def src_spec(kind):
        def index(i, p):
            if kind[0] == "cols":
                _, first, n = kind
                inner = jnp.clip(p - first, 0, n - 1)
            else:
                _, group = kind
                inner = jnp.clip((p - group + 2) // 3, 0, 2)
            return _source_index(kind, i, inner)
        return pl.BlockSpec(_source_block(kind, tm), index)

    any_spec = pl.BlockSpec(memory_space=pl.ANY)
    return pl.pallas_call(
        body, name=name, grid=(t // tm, n_blocks),
        in_specs=[src_spec(kind) for _, kind in sources] + [
            pl.BlockSpec((CB, d), lambda i, p: (p, 0)), any_spec, pl.BlockSpec((1, d), lambda i, p: (0, 0)), any_spec],
        out_specs=[pl.BlockSpec((tm, d), lambda i, p: (i, 0)),
                   pl.BlockSpec((1, d), lambda i, p: (0, 0))],
        out_shape=[_sds((t, d), F32), _sds((1, d), F32)],
        scratch_shapes=[pltpu.VMEM((tm, d), F32), pltpu.VMEM((tm, d), F32), pltpu.VMEM((tm, d), F32),
                        pltpu.SemaphoreType.DMA((2,))],
        compiler_params=_cp("arbitrary", "arbitrary"),
    )(*[a for a, _ in sources], wt, x, g, dres)


def _window_counts(first_row, rows, w):
    t = first_row + lax.broadcasted_iota(jnp.int32, (rows, 1), 0)
    return jnp.minimum(t + 1, w).astype(F32)


def _tril_bf16(ws_ref, g):
    r = lax.broadcasted_iota(jnp.int32, (CHUNK, CHUNK), 0)
    c = lax.broadcasted_iota(jnp.int32, (CHUNK, CHUNK), 1)
    return jnp.where(r >= c, ws_ref[g], 0.0).astype(BF16), r >= c


def even_mix(z, wp, scale, ws, bs_col, name, tm=256):
    t = z.shape[0]
    gd = CB // len(POOL_SIZES)

    def body(a_ref, ga_ref, u_ref, v_ref, gb_ref, wp_ref, sc_ref, ws_ref, bs_ref, y_ref, pooled_ref, aext):
        i = pl.program_id(0)

        @pl.when(i == 0)
        def _():
            aext[0:HALO, :] = jnp.zeros((HALO, CB), F32)

        @pl.when(i > 0)
        def _():
            aext[0:HALO, :] = aext[tm:tm + HALO, :]

        aext[HALO:HALO + tm, :] = a_ref[...].astype(F32)
        for g, w in enumerate(POOL_SIZES):
            cols = slice(g * gd, (g + 1) * gd)
            tok = aext[HALO:HALO + tm, cols]
            s = tok
            for k in range(1, w):
                s = s + aext[HALO - k:HALO - k + tm, cols]
            pooled = (s / _window_counts(i * tm, tm, w) - tok).astype(BF16)
            pooled_ref[:, cols] = pooled
            mixed = _dot(pooled, wp_ref[g])
            silu_a, _ = _silu_and_grad(ga_ref[:, cols].astype(F32))
            y_ref[:, cols] = (mixed * sc_ref[:, cols] * silu_a).astype(BF16)
        for g in range(4):
            cols = slice(g * gd, (g + 1) * gd)
            wg, _ = _tril_bf16(ws_ref, g)
            for c in range(tm // CHUNK):
                rows = slice(c * CHUNK, (c + 1) * CHUNK)
                m = _dot(wg, v_ref[rows, cols]) + bs_ref[g]
                silu_b, _ = _silu_and_grad(gb_ref[rows, cols].astype(F32))
                y_ref[rows, CB + g * gd:CB + (g + 1) * gd] = (
                    u_ref[rows, cols].astype(F32) * m * silu_b).astype(BF16)

    zspec = lambda cb: pl.BlockSpec((tm, CB), lambda i, cb=cb: (i, cb))
    full = lambda shape: pl.BlockSpec(shape, lambda i: (0,) * len(shape))
    return pl.pallas_call(
        body, name=name, grid=(t // tm,),
        in_specs=[zspec(0), zspec(1), zspec(2), zspec(3), zspec(4),
                  full(wp.shape), full(scale.shape), full(ws.shape), full(bs_col.shape)],
        out_specs=[pl.BlockSpec((tm, 2 * CB), lambda i: (i, 0)), pl.BlockSpec((tm, CB), lambda i: (i, 0))],
        out_shape=[_sds((t, 2 * CB), BF16), _sds((t, CB), BF16)],
        scratch_shapes=[pltpu.VMEM((HALO + tm, CB), F32)],
        compiler_params=_cp("arbitrary"),
    )(z, z, z, z, z, wp, scale, ws, bs_col)


def even_mix_bwd(dy, z, pooled, wp, scale, ws, bs_col, name, tm=256):
    t = z.shape[0]
    nt = t // tm
    gd = CB // len(POOL_SIZES)

    def body(dya_ref, dyb_ref, ga_ref, u_ref, v_ref, gb_ref, pooled_ref, wp_ref, sc_ref, ws_ref, bs_ref,
             dz_ref, dwp_ref, dsc_ref, dws_ref, dbs_ref, dpext):
        i = pl.program_id(0)
        tile = nt - 1 - i

        @pl.when(i == 0)
        def _():
            dpext[tm:tm + HALO, :] = jnp.zeros((HALO, CB), F32)
            dwp_ref[...] = jnp.zeros_like(dwp_ref)
            dsc_ref[...] = jnp.zeros_like(dsc_ref)
            dws_ref[...] = jnp.zeros_like(dws_ref)
            dbs_ref[...] = jnp.zeros_like(dbs_ref)

        @pl.when(i > 0)
        def _():
            dpext[tm:tm + HALO, :] = dpext[0:HALO, :]

        for g, w in enumerate(POOL_SIZES):
            cols = slice(g * gd, (g + 1) * gd)
            pooled_g = pooled_ref[:, cols]
            mixed = _dot(pooled_g, wp_ref[g])
            silu_a, dsilu_a = _silu_and_grad(ga_ref[:, cols].astype(F32))
            dya = dya_ref[:, cols].astype(F32)
            sc = sc_ref[:, cols]
            dmixed = (dya * sc * silu_a).astype(BF16)
            dsc_ref[:, cols] += jnp.sum(dya * mixed * silu_a, axis=0, keepdims=True)
            dz_ref[:, CB + g * gd:CB + (g + 1) * gd] = (dya * mixed * sc * dsilu_a).astype(BF16)
            dwp_ref[g] += _dot_tn(pooled_g, dmixed)
            dpooled = _dot_nt(dmixed, wp_ref[g])
            dpext[0:tm, cols] = dpooled / _window_counts(tile * tm, tm, w)
            s = dpext[0:tm, cols]
            for k in range(1, w):
                s = s + dpext[k:k + tm, cols]
            dz_ref[:, cols] = (s - dpooled).astype(BF16)
        for g in range(4):
            cols = slice(g * gd, (g + 1) * gd)
            wg, lower = _tril_bf16(ws_ref, g)
            dws_g = jnp.zeros((CHUNK, CHUNK), F32)
            dbs_g = jnp.zeros((CHUNK, 1), F32)
            for c in range(tm // CHUNK):
                rows = slice(c * CHUNK, (c + 1) * CHUNK)
                vb = v_ref[rows, cols]
                m = _dot(wg, vb) + bs_ref[g]
                gbv = gb_ref[rows, cols].astype(F32)
                silu_b, dsilu_b = _silu_and_grad(gbv)
                dyb = dyb_ref[rows, cols].astype(F32)
                uv = u_ref[rows, cols].astype(F32)
                dm = dyb * silu_b * uv
                dmb = dm.astype(BF16)
                dz_ref[rows, 2 * CB + g * gd:2 * CB + (g + 1) * gd] = (dyb * silu_b * m).astype(BF16)
                dz_ref[rows, 3 * CB + g * gd:3 * CB + (g + 1) * gd] = _dot_tn(wg, dmb).astype(BF16)
                dz_ref[rows, 4 * CB + g * gd:4 * CB + (g + 1) * gd] = (dyb * uv * m * dsilu_b).astype(BF16)
                dws_g = dws_g + _dot_nt(dmb, vb)
                dbs_g = dbs_g + jnp.sum(dm, axis=1, keepdims=True)
            dws_ref[g] += jnp.where(lower, dws_g, 0.0)
            dbs_ref[g] += dbs_g

    rev = lambda cb: pl.BlockSpec((tm, CB), lambda i, cb=cb: (nt - 1 - i, cb))
    full = lambda shape: pl.BlockSpec(shape, lambda i: (0,) * len(shape))
    return pl.pallas_call(
        body, name=name, grid=(nt,),
        in_specs=[rev(0), rev(1), rev(1), rev(2), rev(3), rev(4), rev(0),
                  full(wp.shape), full(scale.shape), full(ws.shape), full(bs_col.shape)],
        out_specs=[pl.BlockSpec((tm, 5 * CB), lambda i: (nt - 1 - i, 0)),
                   full(wp.shape), full(scale.shape), full(ws.shape), full(bs_col.shape)],
        out_shape=[_sds((t, 5 * CB), BF16), _sds(wp.shape, F32), _sds(scale.shape, F32),
                   _sds(ws.shape, F32), _sds(bs_col.shape, F32)],
        scratch_shapes=[pltpu.VMEM((tm + HALO, CB), F32)],
        compiler_params=_cp("arbitrary"),
    )(dy, dy, z, z, z, z, pooled, wp, scale, ws, bs_col)


def _slope(group, head):
    return float(2.0 ** (-8.0 * (group * HEADS_PER_GROUP + head + 1) / N_HEADS))


def _band(dilation):
    qi = lax.broadcasted_iota(jnp.int32, (ATTN_BLOCK, ATTN_BLOCK), 0)
    ki = lax.broadcasted_iota(jnp.int32, (ATTN_BLOCK, ATTN_BLOCK), 1)
    dist_prev = ((qi + ATTN_BLOCK - ki) * dilation).astype(F32)
    dist_cur = ((qi - ki) * dilation).astype(F32)
    return dist_prev, ki >= qi, dist_cur, ki <= qi


def _permute_rows(name, d, inverse, arrays, in_specs, out_shapes, out_specs, n_chunks, widths):
    rows = ATTN_BLOCK * d
    t = out_shapes[0].shape[-2]
    n = len(arrays)

    lanes = HEAD_DIM

    def body(*refs):
        for a, o, s, w in zip(refs[:n], refs[n:2 * n], refs[2 * n:], widths):
            for k in range(w // lanes):
                cols = slice(k * lanes, (k + 1) * lanes)
                if inverse:
                    for r in range(d):
                        s[k, pl.ds(r, ATTN_BLOCK, stride=d), :] = (
                            a[r * ATTN_BLOCK:(r + 1) * ATTN_BLOCK, cols].astype(F32))
                    o[:, cols] = s[k].astype(o.dtype)
                else:
                    s[k] = a[:, cols].astype(F32)
                    for r in range(d):
                        o[r * ATTN_BLOCK:(r + 1) * ATTN_BLOCK, cols] = (
                            s[k, pl.ds(r, ATTN_BLOCK, stride=d), :].astype(o.dtype))

    return pl.pallas_call(
        body, name=name, grid=(t // rows, n_chunks),
        in_specs=in_specs, out_specs=out_specs, out_shape=out_shapes,
        scratch_shapes=[pltpu.VMEM((w // lanes, rows, lanes), F32) for w in widths],
        compiler_params=_cp("parallel", "arbitrary"),
    )(*arrays)


def _chunk_width(d):
    return 4 * CB // d


def permute_mats(wide, narrow, group, inverse, name):
    d = DILATIONS[group]
    rows, cw = ATTN_BLOCK * d, _chunk_width(d)
    wide_spec = pl.BlockSpec((rows, cw), lambda i, c: (i, c))
    narrow_spec = pl.BlockSpec((rows, HEAD_DIM), lambda i, c: (i, 0))
    arrays = [wide] + list(narrow)
    specs = [wide_spec] + [narrow_spec] * len(narrow)
    return _permute_rows(name, d, inverse, arrays, specs, [_sds(a.shape, a.dtype) for a in arrays], specs,
                         CB // cw, [cw] + [HEAD_DIM] * len(narrow))


def permute_slabs(s, group, inverse, name):
    d = DILATIONS[group]
    rows, cw = ATTN_BLOCK * d, _chunk_width(d)
    per = CB // cw
    spec = pl.BlockSpec((None, rows, cw), lambda i, c: (c // per, i, c % per))
    (out,) = _permute_rows(name, d, inverse, [s], [spec], [_sds(s.shape, s.dtype)], [spec], 3 * per, [cw])
    return out


def _lane_pack(cols):
    rows = cols[0].shape[0]
    lane = lax.broadcasted_iota(jnp.int32, (rows, HEAD_DIM), 1)
    out = jnp.zeros((rows, HEAD_DIM), F32)
    for h, c in enumerate(cols):
        out = jnp.where(lane == h, c, out)
    return out


def _qkv_col(group, from_z):
    return (lambda which: which * 3 + group) if from_z else (lambda which: which)


SCORE_SCALE = HEAD_DIM ** -0.5
LOG2_E = 1.4426950408889634
EXP2_SCALE = SCORE_SCALE * LOG2_E


def _fill_bias(bias_ref, group, d):
    dist_p, ok_p, dist_c, ok_c = _band(d)
    for h in range(HEADS_PER_GROUP):
        k = -_slope(group, h) / SCORE_SCALE
        bias_ref[h, :, 0:ATTN_BLOCK] = jnp.where(ok_p, k * dist_p, NEG)
        bias_ref[h, :, ATTN_BLOCK:2 * ATTN_BLOCK] = jnp.where(ok_c, k * dist_c, NEG)


def _joined(prev_ref, cur_ref, rows, cols):
    return jnp.concatenate([prev_ref[rows, cols], cur_ref[rows, cols]], axis=0)


def _residues_per_step(d):
    return min(d, 4)


def _raw_scores(q, kk, bias_ref, h, has_prev):
    is_cur = lax.broadcasted_iota(jnp.int32, (ATTN_BLOCK, 2 * ATTN_BLOCK), 1) >= ATTN_BLOCK
    return jnp.where(jnp.logical_or(is_cur, has_prev), _dot_nt(q, kk) + bias_ref[h], NEG)


def attn_fwd(qkv, group, name):
    t = qkv.shape[0]
    d = DILATIONS[group]
    sub = _residues_per_step(d)
    col = _qkv_col(group, qkv.shape[1] != 3 * CB)
    heads = range(HEADS_PER_GROUP)
    back = d // sub

    def body(q_ref, kp_ref, kc_ref, vp_ref, vc_ref, o_ref, lse_ref, bias_ref):
        b = pl.program_id(0)

        @pl.when(b == 0)
        def _():
            _fill_bias(bias_ref, group, d)

        hs = [slice(h * HEAD_DIM, (h + 1) * HEAD_DIM) for h in heads]
        for j in range(sub):
            rows = slice(j * ATTN_BLOCK, (j + 1) * ATTN_BLOCK)
            s = [_raw_scores(q_ref[rows, hs[h]], _joined(kp_ref, kc_ref, rows, hs[h]), bias_ref, h, b >= back)
                 for h in heads]
            m = [jnp.max(t_, axis=1, keepdims=True) for t_ in s]
            e = [jnp.exp2(EXP2_SCALE * (t_ - m[h])) for h, t_ in enumerate(s)]
            l = [jnp.sum(e_, axis=1, keepdims=True) for e_ in e]
            o = [_dot(e_.astype(BF16), _joined(vp_ref, vc_ref, rows, hs[h])) * (1.0 / l[h]) for h, e_ in enumerate(e)]
            for h in heads:
                o_ref[rows, hs[h]] = o[h].astype(BF16)
            lse_ref[rows, :] = _lane_pack([SCORE_SCALE * m[h] + jnp.log(l[h]) for h in heads])

    blk = (sub * ATTN_BLOCK, CB)
    cur = lambda which: pl.BlockSpec(blk, lambda b: (b, col(which)))
    prev = lambda which: pl.BlockSpec(blk, lambda b: (jnp.maximum(b - back, 0), col(which)))
    return pl.pallas_call(
        body, name=name, grid=(t // blk[0],),
        in_specs=[cur(0), prev(1), cur(1), prev(2), cur(2)],
        out_specs=[pl.BlockSpec(blk, lambda b: (b, 0)), pl.BlockSpec((blk[0], HEAD_DIM), lambda b: (b, 0))],
        out_shape=[_sds((t, CB), BF16), _sds((t, HEAD_DIM), F32)],
        scratch_shapes=[pltpu.VMEM((HEADS_PER_GROUP, ATTN_BLOCK, 2 * ATTN_BLOCK), F32)],
        compiler_params=_cp("arbitrary"),
    )(qkv, qkv, qkv, qkv, qkv)


def attn_bwd(z, dyc, lse, dd, group, name):
    t = z.shape[0]
    d = DILATIONS[group]
    sub = _residues_per_step(d)
    nb = t // (ATTN_BLOCK * d)
    col = _qkv_col(group, z.shape[1] != 3 * CB)
    heads = range(HEADS_PER_GROUP)
    back = d // sub
    steps = back * nb

    def body(q_ref, kp_ref, kc_ref, vp_ref, vc_ref, dy_ref, lse_ref, dd_ref, out_ref, carry, bias_ref):
        s_ = pl.program_id(0)
        n = s_ % nb

        @pl.when(s_ == 0)
        def _():
            _fill_bias(bias_ref, group, d)
            carry[...] = jnp.zeros_like(carry)

        @pl.when(s_ == steps)
        def _():
            out_ref[...] = carry[...].astype(BF16)

        @pl.when(s_ < steps)
        def _():
            hs = [slice(h * HEAD_DIM, (h + 1) * HEAD_DIM) for h in heads]
            for j in range(sub):
                rows = slice(j * ATTN_BLOCK, (j + 1) * ATTN_BLOCK)
                q = [q_ref[rows, hs[h]] for h in heads]
                kk = [_joined(kp_ref, kc_ref, rows, hs[h]) for h in heads]
                dy = [dy_ref[rows, hs[h]] for h in heads]
                old = [[carry[w, rows, hs[h]] for h in heads] for w in range(3)]
                s = [_raw_scores(q[h], kk[h], bias_ref, h, n > 0) for h in heads]
                p = [jnp.exp2(EXP2_SCALE * t_ - lse_ref[rows, h:h + 1] * LOG2_E) for h, t_ in enumerate(s)]
                ds = [(p[h] * (_dot_nt(dy[h], _joined(vp_ref, vc_ref, rows, hs[h])) - dd_ref[rows, h:h + 1])
                       * SCORE_SCALE).astype(BF16) for h in heads]
                for h in heads:
                    dkk = _dot_tn(ds[h], q[h])
                    dvv = _dot_tn(p[h].astype(BF16), dy[h])
                    out_ref[0, rows, hs[h]] = old[0][h].astype(BF16)
                    out_ref[1, rows, hs[h]] = (old[1][h] + dkk[0:ATTN_BLOCK]).astype(BF16)
                    out_ref[2, rows, hs[h]] = (old[2][h] + dvv[0:ATTN_BLOCK]).astype(BF16)
                    carry[0, rows, hs[h]] = _dot(ds[h], kk[h])
                    carry[1, rows, hs[h]] = dkk[ATTN_BLOCK:2 * ATTN_BLOCK]
                    carry[2, rows, hs[h]] = dvv[ATTN_BLOCK:2 * ATTN_BLOCK]

    blk = (sub * ATTN_BLOCK, CB)

    def row_of(s_):
        s_ = jnp.clip(s_, 0, steps - 1)
        return (s_ % nb) * back + s_ // nb

    def prev_row_of(s_):
        s_ = jnp.clip(s_, 0, steps - 1)
        return row_of(s_) - jnp.where(s_ % nb > 0, back, 0)

    cur = lambda which: pl.BlockSpec(blk, lambda s_: (row_of(s_), col(which)))
    prev = lambda which: pl.BlockSpec(blk, lambda s_: (prev_row_of(s_), col(which)))
    narrow = pl.BlockSpec((blk[0], HEAD_DIM), lambda s_: (row_of(s_), 0))
    return pl.pallas_call(
        body, name=name, grid=(steps + 1,),
        in_specs=[cur(0), prev(1), cur(1), prev(2), cur(2), pl.BlockSpec(blk, lambda s_: (row_of(s_), 0)), narrow, narrow],
        out_specs=pl.BlockSpec((3, blk[0], CB), lambda s_: (0, row_of(s_ - 1), 0)),
        out_shape=_sds((3, t, CB), BF16),
        scratch_shapes=[pltpu.VMEM((3, blk[0], CB), F32),
                        pltpu.VMEM((HEADS_PER_GROUP, ATTN_BLOCK, 2 * ATTN_BLOCK), F32)],
        compiler_params=_cp("arbitrary"),
    )(z, z, z, z, z, dyc, lse, dd)


def odd_mix(os_, lses, z, conv_w, name, tm=256):
    t = z.shape[0]

    def body(o0, o1, o2, l0, l1, l2, gc_ref, db_ref, dc_ref, dx_ref, gd_ref, cw_ref,
             y_ref, yc_ref, lse_ref, conv_ref, zext):
        i = pl.program_id(0)
        a0, a1, a2 = l0[...], l1[...], l2[...]
        m = jnp.maximum(jnp.maximum(a0, a1), a2)
        tot = m + jnp.log(jnp.exp(a0 - m) + jnp.exp(a1 - m) + jnp.exp(a2 - m))
        lse_ref[...] = tot
        w0, w1, w2 = jnp.exp(a0 - tot), jnp.exp(a1 - tot), jnp.exp(a2 - tot)
        for h in range(HEADS_PER_GROUP):
            hs = slice(h * HEAD_DIM, (h + 1) * HEAD_DIM)
            yc = (w0[:, h:h + 1] * o0[:, hs].astype(F32) + w1[:, h:h + 1] * o1[:, hs].astype(F32)
                  + w2[:, h:h + 1] * o2[:, hs].astype(F32))
            yc_ref[:, hs] = yc.astype(BF16)
            silu_c, _ = _silu_and_grad(gc_ref[:, hs].astype(F32))
            y_ref[:, hs] = (yc * silu_c).astype(BF16)

        @pl.when(i == 0)
        def _():
            zext[0:HALO, :] = jnp.zeros((HALO, CB), F32)

        @pl.when(i > 0)
        def _():
            zext[0:HALO, :] = zext[tm:tm + HALO, :]

        zext[HALO:HALO + tm, :] = dc_ref[...].astype(F32) * dx_ref[...].astype(F32)
        conv = (cw_ref[0:1, :] * zext[HALO - 2:HALO - 2 + tm, :] + cw_ref[1:2, :] * zext[HALO - 1:HALO - 1 + tm, :]
                + cw_ref[2:3, :] * zext[HALO:HALO + tm, :])
        conv_ref[...] = conv.astype(BF16)
        silu_d, _ = _silu_and_grad(gd_ref[...].astype(F32))
        y_ref[:, CB:2 * CB] = (db_ref[...].astype(F32) * conv * silu_d).astype(BF16)

    row = pl.BlockSpec((tm, CB), lambda i: (i, 0))
    narrow = pl.BlockSpec((tm, HEAD_DIM), lambda i: (i, 0))
    zspec = lambda cb: pl.BlockSpec((tm, CB), lambda i, cb=cb: (i, cb))
    return pl.pallas_call(
        body, name=name, grid=(t // tm,),
        in_specs=[row] * 3 + [narrow] * 3 + [zspec(9), zspec(10), zspec(11), zspec(12), zspec(13),
                                             pl.BlockSpec(conv_w.shape, lambda i: (0, 0))],
        out_specs=[pl.BlockSpec((tm, 2 * CB), lambda i: (i, 0)), row, narrow, row],
        out_shape=[_sds((t, 2 * CB), BF16), _sds((t, CB), BF16), _sds((t, HEAD_DIM), F32), _sds((t, CB), BF16)],
        scratch_shapes=[pltpu.VMEM((HALO + tm, CB), F32)],
        compiler_params=_cp("arbitrary"),
    )(*os_, *lses, z, z, z, z, z, conv_w)


def odd_mix_bwd(dy, yc, conv, z, conv_w, name, tm=256):
    t = z.shape[0]
    nt = t // tm

    def body(dyc_ref, dyd_ref, yc_ref, conv_ref, gc_ref, db_ref, dc_ref, dx_ref, gd_ref, cw_ref,
             dz_ref, dyo_ref, dd_ref, dcw_ref, dcext):
        i = pl.program_id(0)

        @pl.when(i == 0)
        def _():
            dcext[tm:tm + HALO, :] = jnp.zeros((HALO, CB), F32)
            dcw_ref[...] = jnp.zeros_like(dcw_ref)

        @pl.when(i > 0)
        def _():
            dcext[tm:tm + HALO, :] = dcext[0:HALO, :]

        silu_c, dsilu_c = _silu_and_grad(gc_ref[...].astype(F32))
        dyc = dyc_ref[...].astype(F32)
        ycv = yc_ref[...].astype(F32)
        dyo = dyc * silu_c
        dyo_ref[...] = dyo.astype(BF16)
        dz_ref[:, 0:CB] = (dyc * ycv * dsilu_c).astype(BF16)
        prod = dyo * ycv
        dd_ref[...] = _lane_pack([jnp.sum(prod[:, h * HEAD_DIM:(h + 1) * HEAD_DIM], axis=1, keepdims=True)
                                  for h in range(HEADS_PER_GROUP)])

        silu_d, dsilu_d = _silu_and_grad(gd_ref[...].astype(F32))
        dyd = dyd_ref[...].astype(F32)
        convv = conv_ref[...].astype(F32)
        dbv = db_ref[...].astype(F32)
        dz_ref[:, CB:2 * CB] = (dyd * convv * silu_d).astype(BF16)
        dz_ref[:, 4 * CB:5 * CB] = (dyd * dbv * convv * dsilu_d).astype(BF16)
        dcext[0:tm, :] = dyd * dbv * silu_d
        dcv, dxv = dc_ref[...].astype(F32), dx_ref[...].astype(F32)
        zc = dcv * dxv
        d0, d1, d2 = dcext[0:tm, :], dcext[1:1 + tm, :], dcext[2:2 + tm, :]
        dzc = cw_ref[2:3, :] * d0 + cw_ref[1:2, :] * d1 + cw_ref[0:1, :] * d2
        dz_ref[:, 2 * CB:3 * CB] = (dzc * dxv).astype(BF16)
        dz_ref[:, 3 * CB:4 * CB] = (dzc * dcv).astype(BF16)
        dcw_ref[0:1, :] += jnp.sum(zc * d2, axis=0, keepdims=True)
        dcw_ref[1:2, :] += jnp.sum(zc * d1, axis=0, keepdims=True)
        dcw_ref[2:3, :] += jnp.sum(zc * d0, axis=0, keepdims=True)

    rev = lambda cb: pl.BlockSpec((tm, CB), lambda i, cb=cb: (nt - 1 - i, cb))
    return pl.pallas_call(
        body, name=name, grid=(nt,),
        in_specs=[rev(0), rev(1), rev(0), rev(0), rev(9), rev(10), rev(11), rev(12), rev(13),
                  pl.BlockSpec(conv_w.shape, lambda i: (0, 0))],
        out_specs=[pl.BlockSpec((tm, 5 * CB), lambda i: (nt - 1 - i, 0)), rev(0),
                   pl.BlockSpec((tm, HEAD_DIM), lambda i: (nt - 1 - i, 0)),
                   pl.BlockSpec(conv_w.shape, lambda i: (0, 0))],
        out_shape=[_sds((t, 5 * CB), BF16), _sds((t, CB), BF16), _sds((t, HEAD_DIM), F32), _sds(conv_w.shape, F32)],
        scratch_shapes=[pltpu.VMEM((tm + HALO, CB), F32)],
        compiler_params=_cp("arbitrary"),
    )(dy, dy, yc, conv, z, z, z, z, z, conv_w)


def out_proj_final_loss(y, w, x, g, target, name, tm=1024, rows=256):
    t, d = x.shape
    k = y.shape[1]

    def body(y_ref, w_ref, x_ref, g_ref, t_ref, dx_ref, loss_ref, dg_ref):
        i = pl.program_id(0)
        dx_ref[...] = x_ref[...] + _dot(y_ref[...], w_ref[...])
        gv = g_ref[...]
        loss = jnp.zeros((1, 1), F32)
        dg = jnp.zeros((1, d), F32)
        for c in range(tm // rows):
            rs = slice(c * rows, (c + 1) * rows)
            xv = dx_ref[rs, :]
            r = lax.rsqrt(jnp.mean(xv * xv, axis=-1, keepdims=True) + EPS)
            xn = xv * r
            err = xn * gv - t_ref[rs, :]
            loss = loss + 0.5 * jnp.sum(jnp.mean(err * err, axis=-1, keepdims=True), axis=0, keepdims=True)
            dy = err * (1.0 / d)
            dyg = dy * gv
            dx_ref[rs, :] = r * (dyg - xn * jnp.mean(dyg * xn, axis=-1, keepdims=True))
            dg = dg + jnp.sum(dy * xn, axis=0, keepdims=True)

        @pl.when(i == 0)
        def _():
            loss_ref[...] = jnp.broadcast_to(loss, loss_ref.shape)
            dg_ref[...] = dg

        @pl.when(i > 0)
        def _():
            loss_ref[...] += jnp.broadcast_to(loss, loss_ref.shape)
            dg_ref[...] += dg

    row = pl.BlockSpec((tm, d), lambda i: (i, 0))
    return pl.pallas_call(
        body, name=name, grid=(t // tm,),
        in_specs=[pl.BlockSpec((tm, k), lambda i: (i, 0)), pl.BlockSpec((k, d), lambda i: (0, 0)), row,
                  pl.BlockSpec((1, d), lambda i: (0, 0)), row],
        out_specs=[row, pl.BlockSpec((8, 128), lambda i: (0, 0)), pl.BlockSpec((1, d), lambda i: (0, 0))],
        out_shape=[_sds((t, d), F32), _sds((8, 128), F32), _sds((1, d), F32)],
        compiler_params=_cp("arbitrary"),
    )(y, w, x, g, target)


def _position():
    x, y, c = lax.axis_index("x"), lax.axis_index("y"), lax.axis_index("c")
    return x, y, c, 4 * x + 2 * y + c


def _peer(x, y, c, k):
    px = 1 - x if k & 4 else x
    py = 1 - y if k & 2 else y
    pc = 1 - c if k & 1 else c
    return (px, py, pc), 4 * px + 2 * py + pc


def _block(ref, axis, size, idx):
    index = [slice(None)] * len(ref.shape)
    index[axis] = pl.ds(idx * size, size)
    return ref.at[tuple(index)]


def all_gather(shards, axes, name):
    n = len(shards)
    fulls = []
    for s, ax in zip(shards, axes):
        shape = list(s.shape)
        shape[ax] *= N_DEV
        fulls.append(_sds(tuple(shape), s.dtype))

    def body(*refs):
        ins, outs = refs[:n], refs[n:2 * n]
        send_sems, recv_sems, local_sems = refs[2 * n:]
        x, y, c, me = _position()
        local = []
        for p in range(n):
            size = ins[p].shape[axes[p]]
            cp = pltpu.make_async_copy(ins[p], _block(outs[p], axes[p], size, me), local_sems.at[p])
            cp.start()
            local.append(cp)
        for k in range(1, N_DEV):
            peer, _ = _peer(x, y, c, k)
            for p in range(n):
                size = ins[p].shape[axes[p]]
                pltpu.make_async_remote_copy(
                    src_ref=ins[p], dst_ref=_block(outs[p], axes[p], size, me),
                    send_sem=send_sems.at[p], recv_sem=recv_sems.at[p],
                    device_id=peer, device_id_type=MESH_ID).start()
        for p in range(n):
            size = ins[p].shape[axes[p]]
            seven = _block(outs[p], axes[p], (N_DEV - 1) * size, 0)
            pltpu.make_async_remote_copy(
                src_ref=seven, dst_ref=seven, send_sem=send_sems.at[p], recv_sem=recv_sems.at[p],
                device_id=(x, y, c), device_id_type=MESH_ID).wait()
            local[p].wait()

    any_spec = pl.BlockSpec(memory_space=pl.ANY)
    return pl.pallas_call(
        body, name=name,
        in_specs=[any_spec] * n, out_specs=[any_spec] * n, out_shape=fulls,
        scratch_shapes=[pltpu.SemaphoreType.DMA((n,)), pltpu.SemaphoreType.DMA((n,)), pltpu.SemaphoreType.DMA((n,))],
    )(*shards)


def _other_chips(x, y):
    return [(1 - x, y), (x, 1 - y), (1 - x, 1 - y)]


def _gather_copy(ins, lands, axes, send_sems, recv_sems, k, p, block, to, from_shard=False):
    dst = _block(lands[p], axes[p], lands[p].shape[axes[p]] // N_DEV, block)
    sem = k * len(lands) + p
    return pltpu.make_async_remote_copy(
        src_ref=ins[p] if from_shard else dst, dst_ref=dst, send_sem=send_sems.at[sem], recv_sem=recv_sems.at[sem],
        device_id=to, device_id_type=MESH_ID)


def _own_block_copy(ins, lands, axes, sems, p, me):
    n = len(lands)
    dst = _block(lands[p], axes[p], lands[p].shape[axes[p]] // N_DEV, me)
    return pltpu.make_async_copy(ins[p], dst, sems.at[4 * n + p])


def all_gather_two_level(shards, axes, name):
    n = len(shards)
    fulls = []
    for s, ax in zip(shards, axes):
        shape = list(s.shape)
        shape[ax] *= N_DEV
        fulls.append(_sds(tuple(shape), s.dtype))

    def body(*refs):
        ins, outs = refs[:n], refs[n:2 * n]
        send_sems, recv_sems, local_sems = refs[2 * n:]
        x, y, c, me = _position()
        sibling, here = (x, y, 1 - c), (x, y, c)
        chips = _other_chips(x, y)
        copy = functools.partial(_gather_copy, ins, outs, axes, send_sems, recv_sems)
        local = [pltpu.make_async_copy(ins[p], _block(outs[p], axes[p], ins[p].shape[axes[p]], me), local_sems.at[p])
                 for p in range(n)]
        sent = []
        for p in range(n):
            sent.append(copy(0, p, me, sibling, from_shard=True))
            sent += [copy(1 + j, p, me, (*chip, c), from_shard=True) for j, chip in enumerate(chips)]
        for cp in local + sent:
            cp.start()
        for j, (px, py) in enumerate(chips):
            for p in range(n):
                arrived = 4 * px + 2 * py + c
                copy(1 + j, p, arrived, here).wait_recv()
                sent.append(copy(4 + j, p, arrived, sibling))
                sent[-1].start()
        for p in range(n):
            copy(0, p, 4 * x + 2 * y + 1 - c, here).wait_recv()
            for j, (px, py) in enumerate(chips):
                copy(4 + j, p, 4 * px + 2 * py + 1 - c, here).wait_recv()
        for cp in sent:
            cp.wait_send()
        for cp in local:
            cp.wait()

    any_spec = pl.BlockSpec(memory_space=pl.ANY)
    return pl.pallas_call(
        body, name=name,
        in_specs=[any_spec] * n, out_specs=[any_spec] * n, out_shape=fulls,
        scratch_shapes=[pltpu.SemaphoreType.DMA((7 * n,)), pltpu.SemaphoreType.DMA((7 * n,)), pltpu.SemaphoreType.DMA((n,))],
    )(*shards)


def gather_two_level_start(shards, axes, after, name):
    n = len(shards)
    lands = []
    for s, ax in zip(shards, axes):
        shape = list(s.shape)
        shape[ax] *= N_DEV
        lands.append(lax.empty(tuple(shape), s.dtype))

    def body(*refs):
        ins, land = refs[:n], refs[n:2 * n]
        send_sems, recv_sems, token = refs[2 * n + 1], refs[2 * n + 2], refs[-1]
        x, y, c, me = _position()
        copy = functools.partial(_gather_copy, ins, land, axes, send_sems, recv_sems)
        for p in range(n):
            copy(0, p, me, (x, y, 1 - c), from_shard=True).start()
            for j, chip in enumerate(_other_chips(x, y)):
                copy(1 + j, p, me, (*chip, c), from_shard=True).start()
            _own_block_copy(ins, land, axes, send_sems, p, me).start()
        token[...] = jnp.zeros_like(token)

    outs = pl.pallas_call(
        body, name=name,
        out_shape=(pltpu.SemaphoreType.DMA((5 * n,)), pltpu.SemaphoreType.DMA((4 * n,)),
                   *[pltpu.HBM(s.shape, s.dtype) for s in shards], *[pltpu.HBM(l.shape, l.dtype) for l in lands],
                   _sds((8, 128), F32)),
        in_specs=[_HBM] * (2 * n) + [pl.BlockSpec(memory_space=pl.ANY)],
        out_specs=(_SEM, _SEM, *[_HBM] * (2 * n), pl.BlockSpec(memory_space=pltpu.VMEM)),
        input_output_aliases={i: 2 + i for i in range(2 * n)},
        compiler_params=pltpu.CompilerParams(has_side_effects=_EFFECT),
    )(*[_in_hbm(s) for s in shards], *[_in_hbm(l) for l in lands], after)
    return outs[0], outs[1], list(outs[2:2 + n]), list(outs[2 + n:2 + 2 * n]), outs[-1]


def gather_two_level_forward(started, axes, after, name):
    send_a, recv_a, shards, lands, _ = started
    n = len(shards)

    def body(*refs):
        land = refs[:n]
        send_first, recv_first = refs[n], refs[n + 1]
        send_fwd, recv_fwd, token = refs[n + 3], refs[n + 4], refs[-1]
        x, y, c, _ = _position()
        for j, (px, py) in enumerate(_other_chips(x, y)):
            for p in range(n):
                arrived = 4 * px + 2 * py + c
                _gather_copy(None, land, axes, send_first, recv_first, 1 + j, p, arrived, (x, y, c)).wait_recv()
                _gather_copy(None, land, axes, send_fwd, recv_fwd, j, p, arrived, (x, y, 1 - c)).start()
        token[...] = jnp.zeros_like(token)

    outs = pl.pallas_call(
        body, name=name,
        out_shape=(pltpu.SemaphoreType.DMA((3 * n,)), pltpu.SemaphoreType.DMA((3 * n,)),
                   *[pltpu.HBM(l.shape, l.dtype) for l in lands], _sds((8, 128), F32)),
        in_specs=[_HBM] * n + [_SEM, _SEM, pl.BlockSpec(memory_space=pl.ANY)],
        out_specs=(_SEM, _SEM, *[_HBM] * n, pl.BlockSpec(memory_space=pltpu.VMEM)),
        input_output_aliases={i: 2 + i for i in range(n)},
        compiler_params=pltpu.CompilerParams(has_side_effects=_EFFECT),
    )(*lands, send_a, recv_a, after)
    return send_a, recv_a, outs[0], outs[1], shards, list(outs[2:2 + n]), outs[-1]


def gather_two_level_wait(forwarded, axes, after, name):
    send_a, recv_a, send_f, recv_f, shards, lands, _ = forwarded
    n = len(shards)

    def body(*refs):
        ins, land = refs[:n], refs[n:2 * n]
        sa, ra, sf, rf = refs[2 * n:2 * n + 4]
        x, y, c, me = _position()
        here = (x, y, c)
        chips = _other_chips(x, y)
        for p in range(n):
            for k in range(4):
                _gather_copy(ins, land, axes, sa, ra, k, p, me, here, from_shard=True).wait_send()
            _own_block_copy(ins, land, axes, sa, p, me).wait()
            _gather_copy(ins, land, axes, sa, ra, 0, p, 4 * x + 2 * y + 1 - c, here).wait_recv()
            for j, (px, py) in enumerate(chips):
                _gather_copy(ins, land, axes, sf, rf, j, p, 4 * px + 2 * py + c, here).wait_send()
                _gather_copy(ins, land, axes, sf, rf, j, p, 4 * px + 2 * py + 1 - c, here).wait_recv()

    outs = pl.pallas_call(
        body, name=name,
        out_shape=(*[pltpu.HBM(s.shape, s.dtype) for s in shards], *[pltpu.HBM(l.shape, l.dtype) for l in lands]),
        in_specs=[_HBM] * (2 * n) + [_SEM] * 4 + [pl.BlockSpec(memory_space=pl.ANY)],
        out_specs=tuple([_HBM] * (2 * n)),
        input_output_aliases={i: i for i in range(2 * n)},
        compiler_params=pltpu.CompilerParams(has_side_effects=_EFFECT),
    )(*shards, *lands, send_a, recv_a, send_f, recv_f, after)
    return list(outs[n:])


_HBM = pl.BlockSpec(memory_space=pltpu.HBM)
_SEM = pl.BlockSpec(memory_space=pltpu.SEMAPHORE)
_EFFECT = pltpu.SideEffectType.DATAFLOW_SIDE_EFFECTING


def _in_hbm(a):
    return pltpu.with_memory_space_constraint(a, pltpu.HBM)


def _landing(mode, src, axis):
    me = 4 * lax.axis_index("x") + 2 * lax.axis_index("y") + lax.axis_index("c")
    if mode == "gather":
        shape = list(src.shape)
        size = shape[axis]
        shape[axis] *= N_DEV
        return lax.dynamic_update_slice_in_dim(lax.empty(tuple(shape), src.dtype), src, me * size, axis)
    size = src.shape[axis] // N_DEV
    own = lax.dynamic_slice_in_dim(src, me * size, size, axis)
    return lax.dynamic_update_slice_in_dim(lax.empty((N_DEV, *own.shape), src.dtype), own[None], me, 0)


def _seven(mode, land_ref, axis):
    if mode == "gather":
        return _block(land_ref, axis, (N_DEV - 1) * (land_ref.shape[axis] // N_DEV), 0)
    return land_ref.at[pl.ds(0, N_DEV - 1)]


def exchange_start(mode, srcs, axes, after, name):
    n = len(srcs)
    lands = [_landing(mode, s, ax) for s, ax in zip(srcs, axes)]

    def body(*refs):
        src, land = refs[:n], refs[n:2 * n]
        send_sems, recv_sems = refs[2 * n + 1], refs[2 * n + 2]
        token = refs[-1]
        x, y, c, me = _position()
        for k in range(1, N_DEV):
            peer, pj = _peer(x, y, c, k)
            for p in range(n):
                if mode == "gather":
                    s = src[p]
                    dst = _block(land[p], axes[p], src[p].shape[axes[p]], me)
                else:
                    s = _block(src[p], axes[p], src[p].shape[axes[p]] // N_DEV, pj)
                    dst = land[p].at[me]
                pltpu.make_async_remote_copy(
                    src_ref=s, dst_ref=dst, send_sem=send_sems.at[p], recv_sem=recv_sems.at[p],
                    device_id=peer, device_id_type=MESH_ID).start()
        token[...] = jnp.zeros_like(token)

    outs = pl.pallas_call(
        body, name=name,
        out_shape=(pltpu.SemaphoreType.DMA((n,)), pltpu.SemaphoreType.DMA((n,)),
                   *[pltpu.HBM(s.shape, s.dtype) for s in srcs], *[pltpu.HBM(l.shape, l.dtype) for l in lands],
                   _sds((8, 128), F32)),
        in_specs=[_HBM] * (2 * n) + [pl.BlockSpec(memory_space=pl.ANY)],
        out_specs=(_SEM, _SEM, *[_HBM] * (2 * n), pl.BlockSpec(memory_space=pltpu.VMEM)),
        input_output_aliases={i: 2 + i for i in range(2 * n)},
        compiler_params=pltpu.CompilerParams(has_side_effects=_EFFECT),
    )(*[_in_hbm(s) for s in srcs], *[_in_hbm(l) for l in lands], after)
    return outs[0], outs[1], list(outs[2:2 + n]), list(outs[2 + n:2 + 2 * n]), outs[-1]


def exchange_wait(mode, started, axes, after, name):
    send_sems, recv_sems, srcs, lands, _ = started
    n = len(srcs)

    def body(*refs):
        land = refs[n:2 * n]
        send_ref, recv_ref = refs[2 * n], refs[2 * n + 1]
        x, y, c, _ = _position()
        for p in range(n):
            seven = _seven(mode, land[p], axes[p])
            cp = pltpu.make_async_remote_copy(
                src_ref=seven, dst_ref=seven, send_sem=send_ref.at[p], recv_sem=recv_ref.at[p],
                device_id=(x, y, c), device_id_type=MESH_ID)
            cp.wait_send()
            cp.wait_recv()

    outs = pl.pallas_call(
        body, name=name,
        out_shape=(*[pltpu.HBM(s.shape, s.dtype) for s in srcs], *[pltpu.HBM(l.shape, l.dtype) for l in lands]),
        in_specs=[_HBM] * (2 * n) + [_SEM, _SEM, pl.BlockSpec(memory_space=pl.ANY)],
        out_specs=tuple([_HBM] * (2 * n)),
        input_output_aliases={i: i for i in range(2 * n)},
        compiler_params=pltpu.CompilerParams(has_side_effects=_EFFECT),
    )(*srcs, *lands, send_sems, recv_sems, after)
    return list(outs[n:])


def _adam_math(g, w, m, v):
    m2 = ADAM_B1 * m + (1.0 - ADAM_B1) * g
    v2 = ADAM_B2 * v + (1.0 - ADAM_B2) * (g * g)
    m_hat = m2 / (1.0 - ADAM_B1 ** ADAM_STEP)
    v_hat = v2 / (1.0 - ADAM_B2 ** ADAM_STEP)
    delta = -ADAM_LR * (m_hat / (jnp.sqrt(v_hat) + ADAM_EPS) + ADAM_WD * w)
    return delta, m2, v2


def adamw(g, w, m, v, name, slots, tr=128):
    r, c = w.shape
    tr = min(tr, r)

    def body(g_ref, w_ref, m_ref, v_ref, go_ref, d_ref, mo_ref, vo_ref):
        if slots:
            gs = g_ref[0].astype(F32)
            for s in range(1, N_DEV):
                gs = gs + g_ref[s].astype(F32)
        else:
            gs = g_ref[...]
        go_ref[...] = gs
        d_ref[...], mo_ref[...], vo_ref[...] = _adam_math(gs, w_ref[...], m_ref[...], v_ref[...])

    row = pl.BlockSpec((tr, c), lambda i: (i, 0))
    gspec = pl.BlockSpec((N_DEV, tr, c), lambda i: (0, i, 0)) if slots else row
    return pl.pallas_call(
        body, name=name, grid=(r // tr,),
        in_specs=[gspec, row, row, row], out_specs=[row] * 4, out_shape=[_sds((r, c), F32)] * 4,
        compiler_params=_cp("parallel"),
    )(g, w, m, v)


def sum_slots(g, name):
    _, r, c = g.shape

    def body(g_ref, o_ref):
        gs = g_ref[0]
        for s in range(1, N_DEV):
            gs = gs + g_ref[s]
        o_ref[...] = gs

    return pl.pallas_call(
        body, name=name, grid=(1,),
        in_specs=[pl.BlockSpec((N_DEV, r, c), lambda i: (0, 0, 0))],
        out_specs=pl.BlockSpec((r, c), lambda i: (0, 0)), out_shape=_sds((r, c), F32),
        compiler_params=_cp("arbitrary"),
    )(g)


def _rows128(a, pad_to=8):
    a = a.reshape(-1, 128)
    pad = (-a.shape[0]) % pad_to
    return jnp.pad(a, ((0, pad), (0, 0))) if pad else a


def kernel(x, even_norm, even_w_in, even_pool_w, even_pool_scale, even_ws, even_bs, even_w_out, odd_norm, odd_w_in, odd_conv_w, odd_w_out, final_norm, loss_target, m_even_norm, m_even_w_in, m_even_pool_w, m_even_pool_scale, m_even_ws, m_even_bs, m_even_w_out, m_odd_norm, m_odd_w_in, m_odd_conv_w, m_odd_w_out, m_final_norm, v_even_norm, v_even_w_in, v_even_pool_w, v_even_pool_scale, v_even_ws, v_even_bs, v_even_w_out, v_odd_norm, v_odd_w_in, v_odd_conv_w, v_odd_w_out, v_final_norm):
    x0 = x[0]
    target = loss_target[0]
    me = 4 * lax.axis_index("x") + 2 * lax.axis_index("y") + lax.axis_index("c")

    (we_in,) = all_gather_two_level([even_w_in[0].astype(BF16)], [1], "gather_even_in")
    odd_small = jnp.pad(odd_norm, ((0, 7), (0, 0))) + jnp.pad(odd_conv_w[0], ((1, 4), (0, 0)))
    rest_axes, odd_axes = [1, 0, 1], [1, 0]
    rest_flight = exchange_start("gather", [even_pool_w[0].astype(BF16), even_w_out[0].astype(BF16), odd_small],
                                 rest_axes, we_in, "gather_even_rest_start")
    odd_flight = gather_two_level_start([odd_w_in[0].astype(BF16), odd_w_out[0].astype(BF16)],
                                        odd_axes, rest_flight[4], "gather_odd_start")
    scale = even_pool_scale
    ws = even_ws[0]
    bs_col = even_bs[0][:, :, None]
    g_fin = final_norm[None, :]

    z_e, h_e_t, we_in_t = norm_matmul(x0, even_norm + odd_flight[4][0:1, 0:1], we_in, "even_in", tn=1280)
    wp, we_out, odd_small = exchange_wait("gather", rest_flight, rest_axes, z_e, "gather_even_rest_wait")
    g_odd, conv_w = odd_small[0:1], odd_small[1:4]
    y_e, pooled = even_mix(z_e, wp, scale, ws, bs_col, "even_mix")
    x1, h_o, h_o_t = matmul_residual(y_e, we_out, x0, "even_out", next_gain=g_odd)
    odd_flight = gather_two_level_forward(odd_flight, odd_axes, x1, "gather_odd_forward")
    h_layouts = [h_o] + [permute_mats(h_o, [], gi, False, f"dilate_h_{gi}")[0] for gi in (1, 2)]
    wo_in, wo_out = gather_two_level_wait(odd_flight, odd_axes, h_layouts[2], "gather_odd_wait")
    z_o, wo_in_t = matmul_layouts(h_layouts, wo_in, "odd_in")
    attn = [attn_fwd(z_o, gi, f"attn_fwd_{gi}") for gi in range(3)]
    for gi in (1, 2):
        attn[gi] = permute_mats(attn[gi][0], [attn[gi][1]], gi, True, f"undilate_attn_{gi}")
    y_o, yc, lse_tot, conv = odd_mix([a[0] for a in attn], [a[1] for a in attn], z_o, conv_w, "odd_mix")
    dx2, loss_blk, dg_fin = out_proj_final_loss(y_o, wo_out, x1, g_fin, target, "odd_out_loss")

    dy_o, dwo_out = out_proj_bwd(dx2, wo_out.T, y_o, "odd_out_bwd")
    dz5_o, dyc, dd, dconv_w = odd_mix_bwd(dy_o, yc, conv, z_o, conv_w, "odd_mix_bwd")
    dqkv = [attn_bwd(z_o, dyc, lse_tot, dd, 0, "attn_bwd_0")]
    for gi in (1, 2):
        dyc_g, lse_g, dd_g = permute_mats(dyc, [lse_tot, dd], gi, False, f"dilate_dy_{gi}")
        dqkv.append(permute_slabs(attn_bwd(z_o, dyc_g, lse_g, dd_g, gi, f"attn_bwd_{gi}"),
                                  gi, True, f"undilate_dqkv_{gi}"))
    src_o = [(dqkv[0], ("slabs", 0)), (dqkv[1], ("slabs", 1)), (dqkv[2], ("slabs", 2)), (dz5_o, ("cols", 9, 5))]
    dwo_in = in_proj_dw(h_o_t, src_o, ODD_COLBLOCKS, "odd_in_dw")
    odd_grads = exchange_start("scatter", [dwo_in, dwo_out], [1, 0], dconv_w, "scatter_odd_start")
    dx1, dg_odd = in_proj_dx(src_o, ODD_COLBLOCKS, wo_in_t, x1, g_odd + odd_grads[4][0:1, 0:1], dx2, "odd_in_dx")

    dy_e, dwe_out = out_proj_bwd(dx1, we_out.T, y_e, "even_out_bwd")
    dz_e, dwp, dscale, dws, dbs_col = even_mix_bwd(dy_e, z_e, pooled, wp, scale, ws, bs_col, "even_mix_bwd")
    src_e = [(dz_e, ("cols", 0, 5))]
    even_grads_a = exchange_start("scatter", [dwp.astype(BF16), dwe_out], [1, 0], dscale, "scatter_even_rest_start")
    dwe_in = in_proj_dw(h_e_t, src_e, 5, "even_in_dw")
    even_grads_b = exchange_start("scatter", [dwe_in], [1], even_grads_a[4], "scatter_even_in_start")
    dx0, dg_even = in_proj_dx(src_e, 5, we_in_t, x0, even_norm + even_grads_b[4][0:1, 0:1], dx1, "even_in_dx")
    small = jnp.concatenate([
        _rows128(dg_even), _rows128(dscale), _rows128(dws), _rows128(dbs_col), _rows128(dg_fin),
        _rows128(dg_odd), _rows128(dconv_w), loss_blk], axis=0)
    n_small = small.shape[0]
    small_flight = exchange_start("gather", [small], [0], dg_fin, "gather_small_start")
    p_wo_in, p_wo_out = exchange_wait("scatter", odd_grads, [1, 0], small_flight[4], "scatter_odd_wait")
    p_wp, p_we_out = exchange_wait("scatter", even_grads_a, [1, 0], small_flight[4], "scatter_even_rest_wait")
    (p_we_in,) = exchange_wait("scatter", even_grads_b, [1], small_flight[4], "scatter_even_in_wait")

    two_d = lambda a, r, c: a.reshape(r, c)
    sharded = [
        ("even_w_in", p_we_in, even_w_in, m_even_w_in, v_even_w_in, (1024, 640)),
        ("even_pool_w", p_wp.reshape(N_DEV, 128, 256), even_pool_w, m_even_pool_w, v_even_pool_w, (128, 256)),
        ("even_w_out", p_we_out, even_w_out, m_even_w_out, v_even_w_out, (256, 1024)),
        ("odd_w_in", p_wo_in, odd_w_in, m_odd_w_in, v_odd_w_in, (1024, 1792)),
        ("odd_w_out", p_wo_out, odd_w_out, m_odd_w_out, v_odd_w_out, (256, 1024)),
    ]
    res = {}
    for name, g, w, m, v, (r, c) in sharded:
        outs = adamw(g, two_d(w, r, c), two_d(m, r, c), two_d(v, r, c), "adamw_" + name, slots=True)
        res[name] = [o.reshape(w.shape) for o in outs]

    (small_all,) = exchange_wait("gather", small_flight, [0], res["odd_w_out"][0], "gather_small_wait")
    small_sum = sum_slots(small_all.reshape(N_DEV, n_small, 128), "sum_small_grads")
    g_even_norm = small_sum[0:8].reshape(1, 1024)
    g_scale = small_sum[8:16].reshape(1, 1024)
    g_ws = small_sum[16:528]
    g_bs = small_sum[528:532]
    g_final = small_sum[536:544].reshape(1, 1024)
    g_odd_norm = lax.dynamic_slice_in_dim(small_sum[544:552], me, 1, axis=0)
    g_conv = lax.dynamic_index_in_dim(small_sum[552:576].reshape(3, 8, 128), me, axis=1, keepdims=False)
    plain = [
        ("even_norm", g_even_norm, even_norm, m_even_norm, v_even_norm, (1, 1024)),
        ("even_pool_scale", g_scale, even_pool_scale, m_even_pool_scale, v_even_pool_scale, (1, 1024)),
        ("even_ws", g_ws, even_ws, m_even_ws, v_even_ws, (512, 128)),
        ("even_bs", g_bs, even_bs, m_even_bs, v_even_bs, (4, 128)),
        ("odd_norm", g_odd_norm, odd_norm, m_odd_norm, v_odd_norm, (1, 128)),
        ("odd_conv_w", g_conv, odd_conv_w, m_odd_conv_w, v_odd_conv_w, (3, 128)),
        ("final_norm", g_final, final_norm, m_final_norm, v_final_norm, (1, 1024)),
    ]
    for name, g, w, m, v, (r, c) in plain:
        outs = adamw(two_d(g, r, c), two_d(w, r, c), two_d(m, r, c), two_d(v, r, c), "adamw_" + name, slots=False)
        res[name] = [o.reshape(w.shape) for o in outs]

    loss = small_sum[576, 0]
    order = ["even_norm", "even_w_in", "even_pool_w", "even_pool_scale", "even_ws", "even_bs", "even_w_out",
             "odd_norm", "odd_w_in", "odd_conv_w", "odd_w_out", "final_norm"]
    return (loss, dx0[None], *[res[n][0] for n in order], *[res[n][1] for n in order],
            *[res[n][2] for n in order], *[res[n][3] for n in order])
```

```python
import functools

import jax
import jax.numpy as jnp
from jax import lax
from jax.experimental import pallas as pl
from jax.experimental.pallas import tpu as pltpu

F32 = jnp.float32
BF16 = jnp.bfloat16
MESH_ID = pl.DeviceIdType.MESH

EPS = 1e-6
NEG = -1e30
N_DEV = 8
POOL_SIZES = (2, 4, 8, 16)
DILATIONS = (1, 4, 16)
N_HEADS = 24
HEADS_PER_GROUP = 8
HEAD_DIM = 128
ATTN_BLOCK = 128
CHUNK = 128
CB = 1024
HALO = 16
ODD_COLBLOCKS = 14
ADAM_LR = 0.001
ADAM_B1 = 0.9
ADAM_B2 = 0.999
ADAM_EPS = 1e-08
ADAM_WD = 0.01
ADAM_STEP = 10
VMEM_LIMIT = 52 * 1024 * 1024


def _cp(*sem):
    return pltpu.CompilerParams(dimension_semantics=sem, vmem_limit_bytes=VMEM_LIMIT)


def _dot(a, b):
    return jnp.dot(a, b, preferred_element_type=F32)


def _dot_nt(a, b):
    return lax.dot_general(a, b, (((1,), (1,)), ((), ())), preferred_element_type=F32)


def _dot_tn(a, b):
    return lax.dot_general(a, b, (((0,), (0,)), ((), ())), preferred_element_type=F32)


def _sigmoid(x):
    return 0.5 * jnp.tanh(0.5 * x) + 0.5


def _silu_and_grad(x):
    s = _sigmoid(x)
    return x * s, s * (1.0 + x * (1.0 - s))


def _sds(shape, dtype):
    return jax.ShapeDtypeStruct(shape, dtype)


def norm_matmul(x, g, w, name, tn, tm=1024, rows=256):
    t, d = x.shape
    n = w.shape[1]
    n_col = n // tn

    def body(x_ref, g_ref, w_ref, z_ref, ht_ref, wt_ref, h_ref):
        @pl.when(pl.program_id(1) == 0)
        def _():
            for c in range(tm // rows):
                rs = slice(c * rows, (c + 1) * rows)
                xv = x_ref[rs, :]
                r = lax.rsqrt(jnp.mean(xv * xv, axis=-1, keepdims=True) + EPS)
                h_ref[rs, :] = ((xv * r) * g_ref[...]).astype(BF16)
            ht_ref[...] = h_ref[...].T

        z_ref[...] = _dot(h_ref[...], w_ref[...]).astype(BF16)

        @pl.when(pl.program_id(0) == 0)
        def _():
            wt_ref[...] = w_ref[...].T

    return pl.pallas_call(
        body, name=name, grid=(t // tm, n_col),
        in_specs=[pl.BlockSpec((tm, d), lambda i, j: (i, 0)),
                  pl.BlockSpec((1, d), lambda i, j: (0, 0)),
                  pl.BlockSpec((d, tn), lambda i, j: (0, j))],
        out_specs=[pl.BlockSpec((tm, tn), lambda i, j: (i, j)),
                   pl.BlockSpec((d, tm), lambda i, j: (0, i)),
                   pl.BlockSpec((tn, d), lambda i, j: (jnp.where(i == 0, j, n_col), 0))],
        out_shape=[_sds((t, n), BF16), _sds((d, t), BF16), _sds((n + tn, d), BF16)],
        scratch_shapes=[pltpu.VMEM((tm, d), BF16)],
        compiler_params=_cp("arbitrary", "arbitrary"),
    )(x, g, w)


def matmul_residual(y, w, x, name, next_gain=None, tm=1024, rows=256):
    t, k = y.shape
    d = w.shape[1]

    def body(y_ref, w_ref, x_ref, *rest):
        o_ref = rest[-1] if next_gain is None else rest[-3]
        o_ref[...] = x_ref[...] + _dot(y_ref[...], w_ref[...])
        if next_gain is not None:
            g_ref, h_ref, ht_ref = rest[0], rest[-2], rest[-1]
            for c in range(tm // rows):
                rs = slice(c * rows, (c + 1) * rows)
                xv = o_ref[rs, :]
                r = lax.rsqrt(jnp.mean(xv * xv, axis=-1, keepdims=True) + EPS)
                h_ref[rs, :] = ((xv * r) * g_ref[...]).astype(BF16)
            ht_ref[...] = h_ref[...].T

    row = pl.BlockSpec((tm, d), lambda i: (i, 0))
    in_specs = [pl.BlockSpec((tm, k), lambda i: (i, 0)), pl.BlockSpec((k, d), lambda i: (0, 0)), row]
    if next_gain is None:
        return pl.pallas_call(
            body, name=name, grid=(t // tm,), in_specs=in_specs, out_specs=row, out_shape=_sds((t, d), F32),
            compiler_params=_cp("parallel"),
        )(y, w, x)
    return pl.pallas_call(
        body, name=name, grid=(t // tm,),
        in_specs=in_specs + [pl.BlockSpec((1, d), lambda i: (0, 0))],
        out_specs=[row, row, pl.BlockSpec((d, tm), lambda i: (0, i))],
        out_shape=[_sds((t, d), F32), _sds((t, d), BF16), _sds((d, t), BF16)],
        compiler_params=_cp("parallel"),
    )(y, w, x, next_gain)


def matmul_layouts(hs, w, name, tm=2048):
    t, d = hs[0].shape
    n_blocks = w.shape[1] // CB

    def layout_of(j):
        return jnp.where(j < 9, j // 3, 0)

    def column_of(j):
        return jnp.where(j < 9, (j % 3) * 3 + j // 3, j)

    def body(h0, h1, h2, w_ref, z_ref, wt_ref):
        layout = layout_of(pl.program_id(1))
        for v, h_ref in enumerate((h0, h1, h2)):
            @pl.when(layout == v)
            def _(h_ref=h_ref):
                z_ref[...] = _dot(h_ref[...], w_ref[...]).astype(BF16)
        @pl.when(pl.program_id(0) == 0)
        def _():
            wt_ref[...] = w_ref[...].T

    row = pl.BlockSpec((tm, d), lambda i, j: (i, 0))
    wt_block = lambda i, j: (jnp.where(i == 0, column_of(j), n_blocks), 0)
    return pl.pallas_call(
        body, name=name, grid=(t // tm, n_blocks),
        in_specs=[row, row, row, pl.BlockSpec((d, CB), lambda i, j: (0, column_of(j)))],
        out_specs=[pl.BlockSpec((tm, CB), lambda i, j: (i, column_of(j))), pl.BlockSpec((CB, d), wt_block)],
        out_shape=[_sds((t, n_blocks * CB), BF16), _sds(((n_blocks + 1) * CB, d), BF16)],
        compiler_params=_cp("arbitrary", "arbitrary"),
    )(*hs, w)


def out_proj_bwd(dx, wt, y, name, tm=512):
    t, d = dx.shape
    k = wt.shape[1]
    steps = t // tm
    half = d // 2

    def body(dx_ref, wt_ref, y_ref, dy_ref, dw_ref, acc):
        i = pl.program_id(0)
        dxb = dx_ref[...].astype(BF16)
        dy_ref[...] = _dot(dxb, wt_ref[...]).astype(BF16)

        @pl.when(i == 0)
        def _():
            acc[...] = jnp.zeros_like(acc)

        for c in range(2):
            acc[:, c * half:(c + 1) * half] += _dot_tn(y_ref[...], dxb[:, c * half:(c + 1) * half])

        @pl.when(i == steps - 1)
        def _():
            dw_ref[...] = acc[...].astype(BF16)

    return pl.pallas_call(
        body, name=name, grid=(steps,),
        in_specs=[pl.BlockSpec((tm, d), lambda i: (i, 0)),
                  pl.BlockSpec((d, k), lambda i: (0, 0)),
                  pl.BlockSpec((tm, k), lambda i: (i, 0))],
        out_specs=[pl.BlockSpec((tm, k), lambda i: (i, 0)),
                   pl.BlockSpec((k, d), lambda i: (0, 0))],
        out_shape=[_sds((t, k), BF16), _sds((k, d), BF16)],
        scratch_shapes=[pltpu.VMEM((k, d), F32)],
        compiler_params=_cp("arbitrary"),
    )(dx, wt, y)


def _source_block(kind, rows):
    if kind[0] == "cols":
        return (rows, CB)
    return (None, rows, CB)


def _source_active(kind, j):
    if kind[0] == "cols":
        _, first, n = kind
        return (j >= first) & (j < first + n), jnp.clip(j - first, 0, n - 1)
    _, group = kind
    return (j < 9) & (j % 3 == group), jnp.clip(j // 3, 0, 2)


def _source_index(kind, row_block, inner):
    if kind[0] == "cols":
        return (row_block, inner)
    return (inner, row_block, 0)


def in_proj_dw(ht, sources, n_blocks, name, tk=1024):
    d, t = ht.shape
    steps = t // tk
    ns = len(sources)

    def body(*refs):
        h_ref, src_refs, dw_ref, acc = refs[0], refs[1:1 + ns], refs[1 + ns], refs[2 + ns]
        j, k = pl.program_id(0), pl.program_id(1)

        @pl.when(k == 0)
        def _():
            acc[...] = jnp.zeros_like(acc)

        for s, (_, kind) in enumerate(sources):
            active, _ = _source_active(kind, j)

            @pl.when(active)
            def _(s=s):
                acc[...] += _dot(h_ref[...], src_refs[s][...])

        @pl.when(k == steps - 1)
        def _():
            dw_ref[...] = acc[...].astype(BF16)

    def src_spec(kind):
        def index(j, k):
            active, inner = _source_active(kind, j)
            return _source_index(kind, jnp.where(active, k, 0), inner)
        return pl.BlockSpec(_source_block(kind, tk), index)

    return pl.pallas_call(
        body, name=name, grid=(n_blocks, steps),
        in_specs=[pl.BlockSpec((d, tk), lambda j, k: (0, k))] + [src_spec(kind) for _, kind in sources],
        out_specs=pl.BlockSpec((d, CB), lambda j, k: (0, j)),
        out_shape=_sds((d, n_blocks * CB), BF16),
        scratch_shapes=[pltpu.VMEM((d, CB), F32)],
        compiler_params=_cp("parallel", "arbitrary"),
    )(ht, *[a for a, _ in sources])


def in_proj_dx(sources, n_blocks, wt, x, g, dres, name, tm=1024, rows=256):
    t, d = x.shape
    ns = len(sources)

    def body(*refs):
        src_refs = refs[:ns]
        w_ref, x_hbm, g_ref, dres_hbm, dx_ref, dg_ref, acc, x_ref, dres_ref, sems = refs[ns:]
        i, p = pl.program_id(0), pl.program_id(1)
        tile = pl.ds(pl.multiple_of(i * tm, tm), tm)
        fetch = [pltpu.make_async_copy(x_hbm.at[tile, :], x_ref, sems.at[0]),
                 pltpu.make_async_copy(dres_hbm.at[tile, :], dres_ref, sems.at[1])]

        @pl.when(p == 0)
        def _():
            acc[...] = jnp.zeros_like(acc)
            for cp in fetch:
                cp.start()

        for s, (_, kind) in enumerate(sources):
            active, _ = _source_active(kind, p)

            @pl.when(active)
            def _(s=s):
                acc[...] += _dot(src_refs[s][...], w_ref[...])

        @pl.when(p == n_blocks - 1)
        def _():
            for cp in fetch:
                cp.wait()
            part = jnp.zeros((1, d), F32)
            for c in range(tm // rows):
                rs = slice(c * rows, (c + 1) * rows)
                xv = x_ref[rs, :]
                r = lax.rsqrt(jnp.mean(xv * xv, axis=-1, keepdims=True) + EPS)
                xn = xv * r
                dh = acc[rs, :]
                dhg = dh * g_ref[...]
                dx_ref[rs, :] = dres_ref[rs, :] + r * (dhg - xn * jnp.mean(dhg * xn, axis=-1, keepdims=True))
                part = part + jnp.sum(dh * xn, axis=0, keepdims=True)

            @pl.when(i == 0)
            def _():
                dg_ref[...] = part

            @pl.when(i > 0)
            def _():
                dg_ref[...] += part

    def src_spec(kind):
        def index(i, p):
            if kind[0] == "cols":
                _, first, n = kind
                inner = jnp.clip(p - first, 0, n - 1)
            else:
                _, group = kind
                inner = jnp.clip((p - group + 2) // 3, 0, 2)
            return _source_index(kind, i, inner)
        return pl.BlockSpec(_source_block(kind, tm), index)

    any_spec = pl.BlockSpec(memory_space=pl.ANY)
    return pl.pallas_call(
        body, name=name, grid=(t // tm, n_blocks),
        in_specs=[src_spec(kind) for _, kind in sources] + [
            pl.BlockSpec((CB, d), lambda i, p: (p, 0)), any_spec, pl.BlockSpec((1, d), lambda i, p: (0, 0)), any_spec],
        out_specs=[pl.BlockSpec((tm, d), lambda i, p: (i, 0)),
                   pl.BlockSpec((1, d), lambda i, p: (0, 0))],
        out_shape=[_sds((t, d), F32), _sds((1, d), F32)],
        scratch_shapes=[pltpu.VMEM((tm, d), F32), pltpu.VMEM((tm, d), F32), pltpu.VMEM((tm, d), F32),
                        pltpu.SemaphoreType.DMA((2,))],
        compiler_params=_cp("arbitrary", "arbitrary"),
    )(*[a for a, _ in sources], wt, x, g, dres)


def _window_counts(first_row, rows, w):
    t = first_row + lax.broadcasted_iota(jnp.int32, (rows, 1), 0)
    return jnp.minimum(t + 1, w).astype(F32)


def _tril_bf16(ws_ref, g):
    r = lax.broadcasted_iota(jnp.int32, (CHUNK, CHUNK), 0)
    c = lax.broadcasted_iota(jnp.int32, (CHUNK, CHUNK), 1)
    return jnp.where(r >= c, ws_ref[g], 0.0).astype(BF16), r >= c


def even_mix(z, wp, scale, ws, bs_col, name, tm=512):
    t = z.shape[0]
    gd = CB // len(POOL_SIZES)

    def body(a_ref, ga_ref, u_ref, v_ref, gb_ref, wp_ref, sc_ref, ws_ref, bs_ref, y_ref, pooled_ref, aext):
        i = pl.program_id(0)

        @pl.when(i == 0)
        def _():
            aext[0:HALO, :] = jnp.zeros((HALO, CB), F32)

        @pl.when(i > 0)
        def _():
            aext[0:HALO, :] = aext[tm:tm + HALO, :]

        aext[HALO:HALO + tm, :] = a_ref[...].astype(F32)
        for g, w in enumerate(POOL_SIZES):
            cols = slice(g * gd, (g + 1) * gd)
            tok = aext[HALO:HALO + tm, cols]
            s = tok
            for k in range(1, w):
                s = s + aext[HALO - k:HALO - k + tm, cols]
            pooled = (s / _window_counts(i * tm, tm, w) - tok).astype(BF16)
            pooled_ref[:, cols] = pooled
            mixed = _dot(pooled, wp_ref[g])
            silu_a, _ = _silu_and_grad(ga_ref[:, cols].astype(F32))
            y_ref[:, cols] = (mixed * sc_ref[:, cols] * silu_a).astype(BF16)
        for g in range(4):
            cols = slice(g * gd, (g + 1) * gd)
            wg, _ = _tril_bf16(ws_ref, g)
            for c in range(tm // CHUNK):
                rows = slice(c * CHUNK, (c + 1) * CHUNK)
                m = _dot(wg, v_ref[rows, cols]) + bs_ref[g]
                silu_b, _ = _silu_and_grad(gb_ref[rows, cols].astype(F32))
                y_ref[rows, CB + g * gd:CB + (g + 1) * gd] = (
                    u_ref[rows, cols].astype(F32) * m * silu_b).astype(BF16)

    zspec = lambda cb: pl.BlockSpec((tm, CB), lambda i, cb=cb: (i, cb))
    full = lambda shape: pl.BlockSpec(shape, lambda i: (0,) * len(shape))
    return pl.pallas_call(
        body, name=name, grid=(t // tm,),
        in_specs=[zspec(0), zspec(1), zspec(2), zspec(3), zspec(4),
                  full(wp.shape), full(scale.shape), full(ws.shape), full(bs_col.shape)],
        out_specs=[pl.BlockSpec((tm, 2 * CB), lambda i: (i, 0)), pl.BlockSpec((tm, CB), lambda i: (i, 0))],
        out_shape=[_sds((t, 2 * CB), BF16), _sds((t, CB), BF16)],
        scratch_shapes=[pltpu.VMEM((HALO + tm, CB), F32)],
        compiler_params=_cp("arbitrary"),
    )(z, z, z, z, z, wp, scale, ws, bs_col)


def even_mix_bwd(dy, z, pooled, wp, scale, ws, bs_col, name, tm=512):
    t = z.shape[0]
    nt = t // tm
    gd = CB // len(POOL_SIZES)

    def body(dya_ref, dyb_ref, ga_ref, u_ref, v_ref, gb_ref, pooled_ref, wp_ref, sc_ref, ws_ref, bs_ref,
             dz_ref, dwp_ref, dsc_ref, dws_ref, dbs_ref, dpext):
        i = pl.program_id(0)
        tile = nt - 1 - i

        @pl.when(i == 0)
        def _():
            dpext[tm:tm + HALO, :] = jnp.zeros((HALO, CB), F32)
            dwp_ref[...] = jnp.zeros_like(dwp_ref)
            dsc_ref[...] = jnp.zeros_like(dsc_ref)
            dws_ref[...] = jnp.zeros_like(dws_ref)
            dbs_ref[...] = jnp.zeros_like(dbs_ref)

        @pl.when(i > 0)
        def _():
            dpext[tm:tm + HALO, :] = dpext[0:HALO, :]

        for g, w in enumerate(POOL_SIZES):
            cols = slice(g * gd, (g + 1) * gd)
            pooled_g = pooled_ref[:, cols]
            mixed = _dot(pooled_g, wp_ref[g])
            silu_a, dsilu_a = _silu_and_grad(ga_ref[:, cols].astype(F32))
            dya = dya_ref[:, cols].astype(F32)
            sc = sc_ref[:, cols]
            dmixed = (dya * sc * silu_a).astype(BF16)
            dsc_ref[:, cols] += jnp.sum(dya * mixed * silu_a, axis=0, keepdims=True)
            dz_ref[:, CB + g * gd:CB + (g + 1) * gd] = (dya * mixed * sc * dsilu_a).astype(BF16)
            dwp_ref[g] += _dot_tn(pooled_g, dmixed)
            dpooled = _dot_nt(dmixed, wp_ref[g])
            dpext[0:tm, cols] = dpooled / _window_counts(tile * tm, tm, w)
            s = dpext[0:tm, cols]
            for k in range(1, w):
                s = s + dpext[k:k + tm, cols]
            dz_ref[:, cols] = (s - dpooled).astype(BF16)
        for g in range(4):
            cols = slice(g * gd, (g + 1) * gd)
            wg, lower = _tril_bf16(ws_ref, g)
            dws_g = jnp.zeros((CHUNK, CHUNK), F32)
            dbs_g = jnp.zeros((CHUNK, 1), F32)
            for c in range(tm // CHUNK):
                rows = slice(c * CHUNK, (c + 1) * CHUNK)
                vb = v_ref[rows, cols]
                m = _dot(wg, vb) + bs_ref[g]
                gbv = gb_ref[rows, cols].astype(F32)
                silu_b, dsilu_b = _silu_and_grad(gbv)
                dyb = dyb_ref[rows, cols].astype(F32)
                uv = u_ref[rows, cols].astype(F32)
                dm = dyb * silu_b * uv
                dmb = dm.astype(BF16)
                dz_ref[rows, 2 * CB + g * gd:2 * CB + (g + 1) * gd] = (dyb * silu_b * m).astype(BF16)
                dz_ref[rows, 3 * CB + g * gd:3 * CB + (g + 1) * gd] = _dot_tn(wg, dmb).astype(BF16)
                dz_ref[rows, 4 * CB + g * gd:4 * CB + (g + 1) * gd] = (dyb * uv * m * dsilu_b).astype(BF16)
                dws_g = dws_g + _dot_nt(dmb, vb)
                dbs_g = dbs_g + jnp.sum(dm, axis=1, keepdims=True)
            dws_ref[g] += jnp.where(lower, dws_g, 0.0)
            dbs_ref[g] += dbs_g

    rev = lambda cb: pl.BlockSpec((tm, CB), lambda i, cb=cb: (nt - 1 - i, cb))
    full = lambda shape: pl.BlockSpec(shape, lambda i: (0,) * len(shape))
    return pl.pallas_call(
        body, name=name, grid=(nt,),
        in_specs=[rev(0), rev(1), rev(1), rev(2), rev(3), rev(4), rev(0),
                  full(wp.shape), full(scale.shape), full(ws.shape), full(bs_col.shape)],
        out_specs=[pl.BlockSpec((tm, 5 * CB), lambda i: (nt - 1 - i, 0)),
                   full(wp.shape), full(scale.shape), full(ws.shape), full(bs_col.shape)],
        out_shape=[_sds((t, 5 * CB), BF16), _sds(wp.shape, F32), _sds(scale.shape, F32),
                   _sds(ws.shape, F32), _sds(bs_col.shape, F32)],
        scratch_shapes=[pltpu.VMEM((tm + HALO, CB), F32)],
        compiler_params=_cp("arbitrary"),
    )(dy, dy, z, z, z, z, pooled, wp, scale, ws, bs_col)


def _slope(group, head):
    return float(2.0 ** (-8.0 * (group * HEADS_PER_GROUP + head + 1) / N_HEADS))


def _band(dilation):
    qi = lax.broadcasted_iota(jnp.int32, (ATTN_BLOCK, ATTN_BLOCK), 0)
    ki = lax.broadcasted_iota(jnp.int32, (ATTN_BLOCK, ATTN_BLOCK), 1)
    dist_prev = ((qi + ATTN_BLOCK - ki) * dilation).astype(F32)
    dist_cur = ((qi - ki) * dilation).astype(F32)
    return dist_prev, ki >= qi, dist_cur, ki <= qi


def _permute_rows(name, d, inverse, arrays, in_specs, out_shapes, out_specs, n_chunks, widths):
    rows = ATTN_BLOCK * d
    t = out_shapes[0].shape[-2]
    n = len(arrays)

    lanes = HEAD_DIM

    def body(*refs):
        for a, o, s, w in zip(refs[:n], refs[n:2 * n], refs[2 * n:], widths):
            for k in range(w // lanes):
                cols = slice(k * lanes, (k + 1) * lanes)
                if inverse:
                    for r in range(d):
                        s[k, pl.ds(r, ATTN_BLOCK, stride=d), :] = (
                            a[r * ATTN_BLOCK:(r + 1) * ATTN_BLOCK, cols].astype(F32))
                    o[:, cols] = s[k].astype(o.dtype)
                else:
                    s[k] = a[:, cols].astype(F32)
                    for r in range(d):
                        o[r * ATTN_BLOCK:(r + 1) * ATTN_BLOCK, cols] = (
                            s[k, pl.ds(r, ATTN_BLOCK, stride=d), :].astype(o.dtype))

    return pl.pallas_call(
        body, name=name, grid=(t // rows, n_chunks),
        in_specs=in_specs, out_specs=out_specs, out_shape=out_shapes,
        scratch_shapes=[pltpu.VMEM((w // lanes, rows, lanes), F32) for w in widths],
        compiler_params=_cp("parallel", "arbitrary"),
    )(*arrays)


def _chunk_width(d):
    return 4 * CB // d


def permute_mats(wide, narrow, group, inverse, name):
    d = DILATIONS[group]
    rows, cw = ATTN_BLOCK * d, _chunk_width(d)
    wide_spec = pl.BlockSpec((rows, cw), lambda i, c: (i, c))
    narrow_spec = pl.BlockSpec((rows, HEAD_DIM), lambda i, c: (i, 0))
    arrays = [wide] + list(narrow)
    specs = [wide_spec] + [narrow_spec] * len(narrow)
    return _permute_rows(name, d, inverse, arrays, specs, [_sds(a.shape, a.dtype) for a in arrays], specs,
                         CB // cw, [cw] + [HEAD_DIM] * len(narrow))


def permute_slabs(s, group, inverse, name):
    d = DILATIONS[group]
    rows, cw = ATTN_BLOCK * d, _chunk_width(d)
    per = CB // cw
    spec = pl.BlockSpec((None, rows, cw), lambda i, c: (c // per, i, c % per))
    (out,) = _permute_rows(name, d, inverse, [s], [spec], [_sds(s.shape, s.dtype)], [spec], 3 * per, [cw])
    return out


def _lane_pack(cols):
    rows = cols[0].shape[0]
    lane = lax.broadcasted_iota(jnp.int32, (rows, HEAD_DIM), 1)
    out = jnp.zeros((rows, HEAD_DIM), F32)
    for h, c in enumerate(cols):
        out = jnp.where(lane == h, c, out)
    return out


def _qkv_col(group, from_z):
    return (lambda which: which * 3 + group) if from_z else (lambda which: which)


SCORE_SCALE = HEAD_DIM ** -0.5
LOG2_E = 1.4426950408889634
EXP2_SCALE = SCORE_SCALE * LOG2_E


def _fill_bias(bias_ref, group, d):
    dist_p, ok_p, dist_c, ok_c = _band(d)
    for h in range(HEADS_PER_GROUP):
        k = -_slope(group, h) / SCORE_SCALE
        bias_ref[h, :, 0:ATTN_BLOCK] = jnp.where(ok_p, k * dist_p, NEG)
        bias_ref[h, :, ATTN_BLOCK:2 * ATTN_BLOCK] = jnp.where(ok_c, k * dist_c, NEG)


STEP_BLOCKS = 4


def _step_plan(d):
    return d < STEP_BLOCKS, max(d // STEP_BLOCKS, 1)


def _joined(prev_ref, cur_ref, j, cols, chained):
    rows = slice(j * ATTN_BLOCK, (j + 1) * ATTN_BLOCK)
    if not chained:
        before = prev_ref[rows, cols]
    elif j == 0:
        before = prev_ref[(STEP_BLOCKS - 1) * ATTN_BLOCK:STEP_BLOCKS * ATTN_BLOCK, cols]
    else:
        before = cur_ref[(j - 1) * ATTN_BLOCK:j * ATTN_BLOCK, cols]
    return jnp.concatenate([before, cur_ref[rows, cols]], axis=0)


def _raw_scores(q, kk, bias_ref, h, has_prev):
    is_cur = lax.broadcasted_iota(jnp.int32, (ATTN_BLOCK, 2 * ATTN_BLOCK), 1) >= ATTN_BLOCK
    return jnp.where(jnp.logical_or(is_cur, has_prev), _dot_nt(q, kk) + bias_ref[h], NEG)


def attn_fwd(qkv, group, name):
    t = qkv.shape[0]
    d = DILATIONS[group]
    sub = STEP_BLOCKS
    chained, back = _step_plan(d)
    col = _qkv_col(group, qkv.shape[1] != 3 * CB)
    heads = range(HEADS_PER_GROUP)

    def body(q_ref, kp_ref, kc_ref, vp_ref, vc_ref, o_ref, lse_ref, bias_ref):
        b = pl.program_id(0)

        @pl.when(b == 0)
        def _():
            _fill_bias(bias_ref, group, d)

        hs = [slice(h * HEAD_DIM, (h + 1) * HEAD_DIM) for h in heads]
        for j in range(sub):
            rows = slice(j * ATTN_BLOCK, (j + 1) * ATTN_BLOCK)
            has_prev = True if chained and j > 0 else b >= back
            s = [_raw_scores(q_ref[rows, hs[h]], _joined(kp_ref, kc_ref, j, hs[h], chained), bias_ref, h, has_prev)
                 for h in heads]
            m = [jnp.max(t_, axis=1, keepdims=True) for t_ in s]
            e = [jnp.exp2(EXP2_SCALE * (t_ - m[h])) for h, t_ in enumerate(s)]
            l = [jnp.sum(e_, axis=1, keepdims=True) for e_ in e]
            o = [_dot(e_.astype(BF16), _joined(vp_ref, vc_ref, j, hs[h], chained)) * (1.0 / l[h])
                 for h, e_ in enumerate(e)]
            for h in heads:
                o_ref[rows, hs[h]] = o[h].astype(BF16)
            lse_ref[rows, :] = _lane_pack([SCORE_SCALE * m[h] + jnp.log(l[h]) for h in heads])

    blk = (sub * ATTN_BLOCK, CB)
    cur = lambda which: pl.BlockSpec(blk, lambda b: (b, col(which)))
    prev = lambda which: pl.BlockSpec(blk, lambda b: (jnp.maximum(b - back, 0), col(which)))
    return pl.pallas_call(
        body, name=name, grid=(t // blk[0],),
        in_specs=[cur(0), prev(1), cur(1), prev(2), cur(2)],
        out_specs=[pl.BlockSpec(blk, lambda b: (b, 0)), pl.BlockSpec((blk[0], HEAD_DIM), lambda b: (b, 0))],
        out_shape=[_sds((t, CB), BF16), _sds((t, HEAD_DIM), F32)],
        scratch_shapes=[pltpu.VMEM((HEADS_PER_GROUP, ATTN_BLOCK, 2 * ATTN_BLOCK), F32)],
        compiler_params=_cp("arbitrary"),
    )(qkv, qkv, qkv, qkv, qkv)


def attn_bwd(z, dyc, lse, dd, group, name):
    t = z.shape[0]
    d = DILATIONS[group]
    sub = STEP_BLOCKS
    chained, back = _step_plan(d)
    nb = t // (ATTN_BLOCK * d) // (sub if chained else 1)
    col = _qkv_col(group, z.shape[1] != 3 * CB)
    heads = range(HEADS_PER_GROUP)
    steps = back * nb

    def body(q_ref, kp_ref, kc_ref, vp_ref, vc_ref, dy_ref, lse_ref, dd_ref, out_ref, carry, bias_ref):
        s_ = pl.program_id(0)
        n = s_ % nb

        @pl.when(s_ == 0)
        def _():
            _fill_bias(bias_ref, group, d)
            carry[...] = jnp.zeros_like(carry)

        @pl.when(s_ == steps)
        def _():
            out_ref[...] = carry[...].astype(BF16)

        @pl.when(s_ < steps)
        def _():
            hs = [slice(h * HEAD_DIM, (h + 1) * HEAD_DIM) for h in heads]
            rows = [slice(j * ATTN_BLOCK, (j + 1) * ATTN_BLOCK) for j in range(sub)]
            old = [[[carry[w, rows[j], hs[h]] for h in heads] for j in range(sub)] for w in range(3)]
            grads = []
            first, second = slice(0, ATTN_BLOCK), slice(ATTN_BLOCK, 2 * ATTN_BLOCK)

            def emit(j, grads):
                for h in heads:
                    out_ref[0, rows[j], hs[h]] = old[0][j][h].astype(BF16)
                    carry[0, rows[j], hs[h]] = grads[j][h][0]
                    for w in (1, 2):
                        if not chained:
                            done = old[w][j][h] + grads[j][h][w][first]
                            kept = grads[j][h][w][second]
                        else:
                            done = old[w][j][h] + grads[0][h][w][first] if j == sub - 1 else old[w][j][h]
                            kept = grads[j][h][w][second]
                            if j < sub - 1:
                                kept = kept + grads[j + 1][h][w][first]
                        out_ref[w, rows[j], hs[h]] = done.astype(BF16)
                        carry[w, rows[j], hs[h]] = kept

            for j in range(sub):
                has_prev = True if chained and j > 0 else n > 0
                q = [q_ref[rows[j], hs[h]] for h in heads]
                kk = [_joined(kp_ref, kc_ref, j, hs[h], chained) for h in heads]
                dy = [dy_ref[rows[j], hs[h]] for h in heads]
                s = [_raw_scores(q[h], kk[h], bias_ref, h, has_prev) for h in heads]
                p = [jnp.exp2(EXP2_SCALE * t_ - lse_ref[rows[j], h:h + 1] * LOG2_E) for h, t_ in enumerate(s)]
                ds = [(p[h] * (_dot_nt(dy[h], _joined(vp_ref, vc_ref, j, hs[h], chained)) - dd_ref[rows[j], h:h + 1])
                       * SCORE_SCALE).astype(BF16) for h in heads]
                grads.append([(_dot(ds[h], kk[h]), _dot_tn(ds[h], q[h]), _dot_tn(p[h].astype(BF16), dy[h]))
                              for h in heads])
                if not chained:
                    emit(j, grads)
            if chained:
                for j in range(sub):
                    emit(j, grads)

    blk = (sub * ATTN_BLOCK, CB)

    def row_of(s_):
        s_ = jnp.clip(s_, 0, steps - 1)
        return (s_ % nb) * back + s_ // nb

    def prev_row_of(s_):
        s_ = jnp.clip(s_, 0, steps - 1)
        return row_of(s_) - jnp.where(s_ % nb > 0, back, 0)

    cur = lambda which: pl.BlockSpec(blk, lambda s_: (row_of(s_), col(which)))
    prev = lambda which: pl.BlockSpec(blk, lambda s_: (prev_row_of(s_), col(which)))
    narrow = pl.BlockSpec((blk[0], HEAD_DIM), lambda s_: (row_of(s_), 0))
    return pl.pallas_call(
        body, name=name, grid=(steps + 1,),
        in_specs=[cur(0), prev(1), cur(1), prev(2), cur(2), pl.BlockSpec(blk, lambda s_: (row_of(s_), 0)), narrow, narrow],
        out_specs=pl.BlockSpec((3, blk[0], CB), lambda s_: (0, row_of(s_ - 1), 0)),
        out_shape=_sds((3, t, CB), BF16),
        scratch_shapes=[pltpu.VMEM((3, blk[0], CB), F32),
                        pltpu.VMEM((HEADS_PER_GROUP, ATTN_BLOCK, 2 * ATTN_BLOCK), F32)],
        compiler_params=_cp("arbitrary"),
    )(z, z, z, z, z, dyc, lse, dd)


def odd_mix(os_, lses, z, conv_w, name, tm=512):
    t = z.shape[0]

    def body(o0, o1, o2, l0, l1, l2, gc_ref, db_ref, dc_ref, dx_ref, gd_ref, cw_ref,
             y_ref, yc_ref, lse_ref, conv_ref, zext):
        i = pl.program_id(0)
        a0, a1, a2 = l0[...], l1[...], l2[...]
        m = jnp.maximum(jnp.maximum(a0, a1), a2)
        tot = m + jnp.log(jnp.exp(a0 - m) + jnp.exp(a1 - m) + jnp.exp(a2 - m))
        lse_ref[...] = tot
        w0, w1, w2 = jnp.exp(a0 - tot), jnp.exp(a1 - tot), jnp.exp(a2 - tot)
        for h in range(HEADS_PER_GROUP):
            hs = slice(h * HEAD_DIM, (h + 1) * HEAD_DIM)
            yc = (w0[:, h:h + 1] * o0[:, hs].astype(F32) + w1[:, h:h + 1] * o1[:, hs].astype(F32)
                  + w2[:, h:h + 1] * o2[:, hs].astype(F32))
            yc_ref[:, hs] = yc.astype(BF16)
            silu_c, _ = _silu_and_grad(gc_ref[:, hs].astype(F32))
            y_ref[:, hs] = (yc * silu_c).astype(BF16)

        @pl.when(i == 0)
        def _():
            zext[0:HALO, :] = jnp.zeros((HALO, CB), F32)

        @pl.when(i > 0)
        def _():
            zext[0:HALO, :] = zext[tm:tm + HALO, :]

        zext[HALO:HALO + tm, :] = dc_ref[...].astype(F32) * dx_ref[...].astype(F32)
        conv = (cw_ref[0:1, :] * zext[HALO - 2:HALO - 2 + tm, :] + cw_ref[1:2, :] * zext[HALO - 1:HALO - 1 + tm, :]
                + cw_ref[2:3, :] * zext[HALO:HALO + tm, :])
        conv_ref[...] = conv.astype(BF16)
        silu_d, _ = _silu_and_grad(gd_ref[...].astype(F32))
        y_ref[:, CB:2 * CB] = (db_ref[...].astype(F32) * conv * silu_d).astype(BF16)

    row = pl.BlockSpec((tm, CB), lambda i: (i, 0))
    narrow = pl.BlockSpec((tm, HEAD_DIM), lambda i: (i, 0))
    zspec = lambda cb: pl.BlockSpec((tm, CB), lambda i, cb=cb: (i, cb))
    return pl.pallas_call(
        body, name=name, grid=(t // tm,),
        in_specs=[row] * 3 + [narrow] * 3 + [zspec(9), zspec(10), zspec(11), zspec(12), zspec(13),
                                             pl.BlockSpec(conv_w.shape, lambda i: (0, 0))],
        out_specs=[pl.BlockSpec((tm, 2 * CB), lambda i: (i, 0)), row, narrow, row],
        out_shape=[_sds((t, 2 * CB), BF16), _sds((t, CB), BF16), _sds((t, HEAD_DIM), F32), _sds((t, CB), BF16)],
        scratch_shapes=[pltpu.VMEM((HALO + tm, CB), F32)],
        compiler_params=_cp("arbitrary"),
    )(*os_, *lses, z, z, z, z, z, conv_w)


def odd_mix_bwd(dy, yc, conv, z, conv_w, name, tm=256):
    t = z.shape[0]
    nt = t // tm

    def body(dyc_ref, dyd_ref, yc_ref, conv_ref, gc_ref, db_ref, dc_ref, dx_ref, gd_ref, cw_ref,
             dz_ref, dyo_ref, dd_ref, dcw_ref, dcext):
        i = pl.program_id(0)

        @pl.when(i == 0)
        def _():
            dcext[tm:tm + HALO, :] = jnp.zeros((HALO, CB), F32)
            dcw_ref[...] = jnp.zeros_like(dcw_ref)

        @pl.when(i > 0)
        def _():
            dcext[tm:tm + HALO, :] = dcext[0:HALO, :]

        silu_c, dsilu_c = _silu_and_grad(gc_ref[...].astype(F32))
        dyc = dyc_ref[...].astype(F32)
        ycv = yc_ref[...].astype(F32)
        dyo = dyc * silu_c
        dyo_ref[...] = dyo.astype(BF16)
        dz_ref[:, 0:CB] = (dyc * ycv * dsilu_c).astype(BF16)
        prod = dyo * ycv
        dd_ref[...] = _lane_pack([jnp.sum(prod[:, h * HEAD_DIM:(h + 1) * HEAD_DIM], axis=1, keepdims=True)
                                  for h in range(HEADS_PER_GROUP)])

        silu_d, dsilu_d = _silu_and_grad(gd_ref[...].astype(F32))
        dyd = dyd_ref[...].astype(F32)
        convv = conv_ref[...].astype(F32)
        dbv = db_ref[...].astype(F32)
        dz_ref[:, CB:2 * CB] = (dyd * convv * silu_d).astype(BF16)
        dz_ref[:, 4 * CB:5 * CB] = (dyd * dbv * convv * dsilu_d).astype(BF16)
        dcext[0:tm, :] = dyd * dbv * silu_d
        dcv, dxv = dc_ref[...].astype(F32), dx_ref[...].astype(F32)
        zc = dcv * dxv
        d0, d1, d2 = dcext[0:tm, :], dcext[1:1 + tm, :], dcext[2:2 + tm, :]
        dzc = cw_ref[2:3, :] * d0 + cw_ref[1:2, :] * d1 + cw_ref[0:1, :] * d2
        dz_ref[:, 2 * CB:3 * CB] = (dzc * dxv).astype(BF16)
        dz_ref[:, 3 * CB:4 * CB] = (dzc * dcv).astype(BF16)
        dcw_ref[0:1, :] += jnp.sum(zc * d2, axis=0, keepdims=True)
        dcw_ref[1:2, :] += jnp.sum(zc * d1, axis=0, keepdims=True)
        dcw_ref[2:3, :] += jnp.sum(zc * d0, axis=0, keepdims=True)

    rev = lambda cb: pl.BlockSpec((tm, CB), lambda i, cb=cb: (nt - 1 - i, cb))
    return pl.pallas_call(
        body, name=name, grid=(nt,),
        in_specs=[rev(0), rev(1), rev(0), rev(0), rev(9), rev(10), rev(11), rev(12), rev(13),
                  pl.BlockSpec(conv_w.shape, lambda i: (0, 0))],
        out_specs=[pl.BlockSpec((tm, 5 * CB), lambda i: (nt - 1 - i, 0)), rev(0),
                   pl.BlockSpec((tm, HEAD_DIM), lambda i: (nt - 1 - i, 0)),
                   pl.BlockSpec(conv_w.shape, lambda i: (0, 0))],
        out_shape=[_sds((t, 5 * CB), BF16), _sds((t, CB), BF16), _sds((t, HEAD_DIM), F32), _sds(conv_w.shape, F32)],
        scratch_shapes=[pltpu.VMEM((tm + HALO, CB), F32)],
        compiler_params=_cp("arbitrary"),
    )(dy, dy, yc, conv, z, z, z, z, z, conv_w)


def out_proj_final_loss(y, w, x, g, target, name, tm=1024, rows=256):
    t, d = x.shape
    k = y.shape[1]

    def body(y_ref, w_ref, x_ref, g_ref, t_ref, dx_ref, loss_ref, dg_ref):
        i = pl.program_id(0)
        dx_ref[...] = x_ref[...] + _dot(y_ref[...], w_ref[...])
        gv = g_ref[...]
        loss = jnp.zeros((1, 1), F32)
        dg = jnp.zeros((1, d), F32)
        for c in range(tm // rows):
            rs = slice(c * rows, (c + 1) * rows)
            xv = dx_ref[rs, :]
            r = lax.rsqrt(jnp.mean(xv * xv, axis=-1, keepdims=True) + EPS)
            xn = xv * r
            err = xn * gv - t_ref[rs, :]
            loss = loss + 0.5 * jnp.sum(jnp.mean(err * err, axis=-1, keepdims=True), axis=0, keepdims=True)
            dy = err * (1.0 / d)
            dyg = dy * gv
            dx_ref[rs, :] = r * (dyg - xn * jnp.mean(dyg * xn, axis=-1, keepdims=True))
            dg = dg + jnp.sum(dy * xn, axis=0, keepdims=True)

        @pl.when(i == 0)
        def _():
            loss_ref[...] = jnp.broadcast_to(loss, loss_ref.shape)
            dg_ref[...] = dg

        @pl.when(i > 0)
        def _():
            loss_ref[...] += jnp.broadcast_to(loss, loss_ref.shape)
            dg_ref[...] += dg

    row = pl.BlockSpec((tm, d), lambda i: (i, 0))
    return pl.pallas_call(
        body, name=name, grid=(t // tm,),
        in_specs=[pl.BlockSpec((tm, k), lambda i: (i, 0)), pl.BlockSpec((k, d), lambda i: (0, 0)), row,
                  pl.BlockSpec((1, d), lambda i: (0, 0)), row],
        out_specs=[row, pl.BlockSpec((8, 128), lambda i: (0, 0)), pl.BlockSpec((1, d), lambda i: (0, 0))],
        out_shape=[_sds((t, d), F32), _sds((8, 128), F32), _sds((1, d), F32)],
        compiler_params=_cp("arbitrary"),
    )(y, w, x, g, target)


def _position():
    x, y, c = lax.axis_index("x"), lax.axis_index("y"), lax.axis_index("c")
    return x, y, c, 4 * x + 2 * y + c


def _peer(x, y, c, k):
    px = 1 - x if k & 4 else x
    py = 1 - y if k & 2 else y
    pc = 1 - c if k & 1 else c
    return (px, py, pc), 4 * px + 2 * py + pc


def _block(ref, axis, size, idx):
    index = [slice(None)] * len(ref.shape)
    index[axis] = pl.ds(idx * size, size)
    return ref.at[tuple(index)]


def _other_chips(x, y):
    return [(1 - x, y), (x, 1 - y), (1 - x, 1 - y)]


def _gather_copy(ins, lands, axes, send_sems, recv_sems, k, p, block, to, from_shard=False):
    dst = _block(lands[p], axes[p], lands[p].shape[axes[p]] // N_DEV, block)
    sem = k * len(lands) + p
    return pltpu.make_async_remote_copy(
        src_ref=ins[p] if from_shard else dst, dst_ref=dst, send_sem=send_sems.at[sem], recv_sem=recv_sems.at[sem],
        device_id=to, device_id_type=MESH_ID)


def _own_block_copy(ins, lands, axes, sems, p, me):
    n = len(lands)
    dst = _block(lands[p], axes[p], lands[p].shape[axes[p]] // N_DEV, me)
    return pltpu.make_async_copy(ins[p], dst, sems.at[4 * n + p])


def all_gather_two_level(shards, axes, name):
    n = len(shards)
    fulls = []
    for s, ax in zip(shards, axes):
        shape = list(s.shape)
        shape[ax] *= N_DEV
        fulls.append(_sds(tuple(shape), s.dtype))

    def body(*refs):
        ins, outs = refs[:n], refs[n:2 * n]
        send_sems, recv_sems, local_sems = refs[2 * n:]
        x, y, c, me = _position()
        sibling, here = (x, y, 1 - c), (x, y, c)
        chips = _other_chips(x, y)
        copy = functools.partial(_gather_copy, ins, outs, axes, send_sems, recv_sems)
        local = [pltpu.make_async_copy(ins[p], _block(outs[p], axes[p], ins[p].shape[axes[p]], me), local_sems.at[p])
                 for p in range(n)]
        sent = []
        for p in range(n):
            sent.append(copy(0, p, me, sibling, from_shard=True))
            sent += [copy(1 + j, p, me, (*chip, c), from_shard=True) for j, chip in enumerate(chips)]
        for cp in local + sent:
            cp.start()
        for j, (px, py) in enumerate(chips):
            for p in range(n):
                arrived = 4 * px + 2 * py + c
                copy(1 + j, p, arrived, here).wait_recv()
                sent.append(copy(4 + j, p, arrived, sibling))
                sent[-1].start()
        for p in range(n):
            copy(0, p, 4 * x + 2 * y + 1 - c, here).wait_recv()
            for j, (px, py) in enumerate(chips):
                copy(4 + j, p, 4 * px + 2 * py + 1 - c, here).wait_recv()
        for cp in sent:
            cp.wait_send()
        for cp in local:
            cp.wait()

    any_spec = pl.BlockSpec(memory_space=pl.ANY)
    return pl.pallas_call(
        body, name=name,
        in_specs=[any_spec] * n, out_specs=[any_spec] * n, out_shape=fulls,
        scratch_shapes=[pltpu.SemaphoreType.DMA((7 * n,)), pltpu.SemaphoreType.DMA((7 * n,)), pltpu.SemaphoreType.DMA((n,))],
    )(*shards)


def gather_two_level_start(shards, axes, after, name):
    n = len(shards)
    lands = []
    for s, ax in zip(shards, axes):
        shape = list(s.shape)
        shape[ax] *= N_DEV
        lands.append(lax.empty(tuple(shape), s.dtype))

    def body(*refs):
        ins, land = refs[:n], refs[n:2 * n]
        send_sems, recv_sems, token = refs[2 * n + 1], refs[2 * n + 2], refs[-1]
        x, y, c, me = _position()
        copy = functools.partial(_gather_copy, ins, land, axes, send_sems, recv_sems)
        for p in range(n):
            copy(0, p, me, (x, y, 1 - c), from_shard=True).start()
            for j, chip in enumerate(_other_chips(x, y)):
                copy(1 + j, p, me, (*chip, c), from_shard=True).start()
            _own_block_copy(ins, land, axes, send_sems, p, me).start()
        token[...] = jnp.zeros_like(token)

    outs = pl.pallas_call(
        body, name=name,
        out_shape=(pltpu.SemaphoreType.DMA((5 * n,)), pltpu.SemaphoreType.DMA((4 * n,)),
                   *[pltpu.HBM(s.shape, s.dtype) for s in shards], *[pltpu.HBM(l.shape, l.dtype) for l in lands],
                   _sds((8, 128), F32)),
        in_specs=[_HBM] * (2 * n) + [pl.BlockSpec(memory_space=pl.ANY)],
        out_specs=(_SEM, _SEM, *[_HBM] * (2 * n), pl.BlockSpec(memory_space=pltpu.VMEM)),
        input_output_aliases={i: 2 + i for i in range(2 * n)},
        compiler_params=pltpu.CompilerParams(has_side_effects=_EFFECT),
    )(*[_in_hbm(s) for s in shards], *[_in_hbm(l) for l in lands], after)
    return outs[0], outs[1], list(outs[2:2 + n]), list(outs[2 + n:2 + 2 * n]), outs[-1]


def gather_two_level_forward(started, axes, after, name):
    send_a, recv_a, shards, lands, _ = started
    n = len(shards)

    def body(*refs):
        land = refs[:n]
        send_first, recv_first = refs[n], refs[n + 1]
        send_fwd, recv_fwd, token = refs[n + 3], refs[n + 4], refs[-1]
        x, y, c, _ = _position()
        for j, (px, py) in enumerate(_other_chips(x, y)):
            for p in range(n):
                arrived = 4 * px + 2 * py + c
                _gather_copy(None, land, axes, send_first, recv_first, 1 + j, p, arrived, (x, y, c)).wait_recv()
                _gather_copy(None, land, axes, send_fwd, recv_fwd, j, p, arrived, (x, y, 1 - c)).start()
        token[...] = jnp.zeros_like(token)

    outs = pl.pallas_call(
        body, name=name,
        out_shape=(pltpu.SemaphoreType.DMA((3 * n,)), pltpu.SemaphoreType.DMA((3 * n,)),
                   *[pltpu.HBM(l.shape, l.dtype) for l in lands], _sds((8, 128), F32)),
        in_specs=[_HBM] * n + [_SEM, _SEM, pl.BlockSpec(memory_space=pl.ANY)],
        out_specs=(_SEM, _SEM, *[_HBM] * n, pl.BlockSpec(memory_space=pltpu.VMEM)),
        input_output_aliases={i: 2 + i for i in range(n)},
        compiler_params=pltpu.CompilerParams(has_side_effects=_EFFECT),
    )(*lands, send_a, recv_a, after)
    return send_a, recv_a, outs[0], outs[1], shards, list(outs[2:2 + n]), outs[-1]


def gather_two_level_wait(forwarded, axes, after, name):
    send_a, recv_a, send_f, recv_f, shards, lands, _ = forwarded
    n = len(shards)

    def body(*refs):
        ins, land = refs[:n], refs[n:2 * n]
        sa, ra, sf, rf = refs[2 * n:2 * n + 4]
        x, y, c, me = _position()
        here = (x, y, c)
        chips = _other_chips(x, y)
        for p in range(n):
            for k in range(4):
                _gather_copy(ins, land, axes, sa, ra, k, p, me, here, from_shard=True).wait_send()
            _own_block_copy(ins, land, axes, sa, p, me).wait()
            _gather_copy(ins, land, axes, sa, ra, 0, p, 4 * x + 2 * y + 1 - c, here).wait_recv()
            for j, (px, py) in enumerate(chips):
                _gather_copy(ins, land, axes, sf, rf, j, p, 4 * px + 2 * py + c, here).wait_send()
                _gather_copy(ins, land, axes, sf, rf, j, p, 4 * px + 2 * py + 1 - c, here).wait_recv()

    outs = pl.pallas_call(
        body, name=name,
        out_shape=(*[pltpu.HBM(s.shape, s.dtype) for s in shards], *[pltpu.HBM(l.shape, l.dtype) for l in lands]),
        in_specs=[_HBM] * (2 * n) + [_SEM] * 4 + [pl.BlockSpec(memory_space=pl.ANY)],
        out_specs=tuple([_HBM] * (2 * n)),
        input_output_aliases={i: i for i in range(2 * n)},
        compiler_params=pltpu.CompilerParams(has_side_effects=_EFFECT),
    )(*shards, *lands, send_a, recv_a, send_f, recv_f, after)
    return list(outs[n:])


_HBM = pl.BlockSpec(memory_space=pltpu.HBM)
_SEM = pl.BlockSpec(memory_space=pltpu.SEMAPHORE)
_EFFECT = pltpu.SideEffectType.DATAFLOW_SIDE_EFFECTING


def _in_hbm(a):
    return pltpu.with_memory_space_constraint(a, pltpu.HBM)


def _landing(mode, src, axis):
    me = 4 * lax.axis_index("x") + 2 * lax.axis_index("y") + lax.axis_index("c")
    if mode == "gather":
        shape = list(src.shape)
        size = shape[axis]
        shape[axis] *= N_DEV
        return lax.dynamic_update_slice_in_dim(lax.empty(tuple(shape), src.dtype), src, me * size, axis)
    size = src.shape[axis] // N_DEV
    own = lax.dynamic_slice_in_dim(src, me * size, size, axis)
    return lax.dynamic_update_slice_in_dim(lax.empty((N_DEV, *own.shape), src.dtype), own[None], me, 0)


def _seven(mode, land_ref, axis):
    if mode == "gather":
        return _block(land_ref, axis, (N_DEV - 1) * (land_ref.shape[axis] // N_DEV), 0)
    return land_ref.at[pl.ds(0, N_DEV - 1)]


def exchange_start(mode, srcs, axes, after, name):
    n = len(srcs)
    lands = [_landing(mode, s, ax) for s, ax in zip(srcs, axes)]

    def body(*refs):
        src, land = refs[:n], refs[n:2 * n]
        send_sems, recv_sems = refs[2 * n + 1], refs[2 * n + 2]
        token = refs[-1]
        x, y, c, me = _position()
        for k in range(1, N_DEV):
            peer, pj = _peer(x, y, c, k)
            for p in range(n):
                if mode == "gather":
                    s = src[p]
                    dst = _block(land[p], axes[p], src[p].shape[axes[p]], me)
                else:
                    s = _block(src[p], axes[p], src[p].shape[axes[p]] // N_DEV, pj)
                    dst = land[p].at[me]
                pltpu.make_async_remote_copy(
                    src_ref=s, dst_ref=dst, send_sem=send_sems.at[p], recv_sem=recv_sems.at[p],
                    device_id=peer, device_id_type=MESH_ID).start()
        token[...] = jnp.zeros_like(token)

    outs = pl.pallas_call(
        body, name=name,
        out_shape=(pltpu.SemaphoreType.DMA((n,)), pltpu.SemaphoreType.DMA((n,)),
                   *[pltpu.HBM(s.shape, s.dtype) for s in srcs], *[pltpu.HBM(l.shape, l.dtype) for l in lands],
                   _sds((8, 128), F32)),
        in_specs=[_HBM] * (2 * n) + [pl.BlockSpec(memory_space=pl.ANY)],
        out_specs=(_SEM, _SEM, *[_HBM] * (2 * n), pl.BlockSpec(memory_space=pltpu.VMEM)),
        input_output_aliases={i: 2 + i for i in range(2 * n)},
        compiler_params=pltpu.CompilerParams(has_side_effects=_EFFECT),
    )(*[_in_hbm(s) for s in srcs], *[_in_hbm(l) for l in lands], after)
    return outs[0], outs[1], list(outs[2:2 + n]), list(outs[2 + n:2 + 2 * n]), outs[-1]


def exchange_wait(mode, started, axes, after, name):
    send_sems, recv_sems, srcs, lands, _ = started
    n = len(srcs)

    def body(*refs):
        land = refs[n:2 * n]
        send_ref, recv_ref = refs[2 * n], refs[2 * n + 1]
        x, y, c, _ = _position()
        for p in range(n):
            seven = _seven(mode, land[p], axes[p])
            cp = pltpu.make_async_remote_copy(
                src_ref=seven, dst_ref=seven, send_sem=send_ref.at[p], recv_sem=recv_ref.at[p],
                device_id=(x, y, c), device_id_type=MESH_ID)
            cp.wait_send()
            cp.wait_recv()

    outs = pl.pallas_call(
        body, name=name,
        out_shape=(*[pltpu.HBM(s.shape, s.dtype) for s in srcs], *[pltpu.HBM(l.shape, l.dtype) for l in lands]),
        in_specs=[_HBM] * (2 * n) + [_SEM, _SEM, pl.BlockSpec(memory_space=pl.ANY)],
        out_specs=tuple([_HBM] * (2 * n)),
        input_output_aliases={i: i for i in range(2 * n)},
        compiler_params=pltpu.CompilerParams(has_side_effects=_EFFECT),
    )(*srcs, *lands, send_sems, recv_sems, after)
    return list(outs[n:])


def _adam_math(g, w, m, v):
    m2 = ADAM_B1 * m + (1.0 - ADAM_B1) * g
    v2 = ADAM_B2 * v + (1.0 - ADAM_B2) * (g * g)
    m_hat = m2 / (1.0 - ADAM_B1 ** ADAM_STEP)
    v_hat = v2 / (1.0 - ADAM_B2 ** ADAM_STEP)
    delta = -ADAM_LR * (m_hat / (jnp.sqrt(v_hat) + ADAM_EPS) + ADAM_WD * w)
    return delta, m2, v2


def adamw(g, w, m, v, name, slots, tr=128):
    r, c = w.shape
    tr = min(tr, r)

    def body(g_ref, w_ref, m_ref, v_ref, go_ref, d_ref, mo_ref, vo_ref):
        if slots:
            gs = g_ref[0].astype(F32)
            for s in range(1, N_DEV):
                gs = gs + g_ref[s].astype(F32)
        else:
            gs = g_ref[...]
        go_ref[...] = gs
        d_ref[...], mo_ref[...], vo_ref[...] = _adam_math(gs, w_ref[...], m_ref[...], v_ref[...])

    row = pl.BlockSpec((tr, c), lambda i: (i, 0))
    gspec = pl.BlockSpec((N_DEV, tr, c), lambda i: (0, i, 0)) if slots else row
    return pl.pallas_call(
        body, name=name, grid=(r // tr,),
        in_specs=[gspec, row, row, row], out_specs=[row] * 4, out_shape=[_sds((r, c), F32)] * 4,
        compiler_params=_cp("parallel"),
    )(g, w, m, v)


def sum_slots(g, name):
    _, r, c = g.shape

    def body(g_ref, o_ref):
        gs = g_ref[0]
        for s in range(1, N_DEV):
            gs = gs + g_ref[s]
        o_ref[...] = gs

    return pl.pallas_call(
        body, name=name, grid=(1,),
        in_specs=[pl.BlockSpec((N_DEV, r, c), lambda i: (0, 0, 0))],
        out_specs=pl.BlockSpec((r, c), lambda i: (0, 0)), out_shape=_sds((r, c), F32),
        compiler_params=_cp("arbitrary"),
    )(g)


def _rows128(a, pad_to=8):
    a = a.reshape(-1, 128)
    pad = (-a.shape[0]) % pad_to
    return jnp.pad(a, ((0, pad), (0, 0))) if pad else a


def kernel(x, even_norm, even_w_in, even_pool_w, even_pool_scale, even_ws, even_bs, even_w_out, odd_norm, odd_w_in, odd_conv_w, odd_w_out, final_norm, loss_target, m_even_norm, m_even_w_in, m_even_pool_w, m_even_pool_scale, m_even_ws, m_even_bs, m_even_w_out, m_odd_norm, m_odd_w_in, m_odd_conv_w, m_odd_w_out, m_final_norm, v_even_norm, v_even_w_in, v_even_pool_w, v_even_pool_scale, v_even_ws, v_even_bs, v_even_w_out, v_odd_norm, v_odd_w_in, v_odd_conv_w, v_odd_w_out, v_final_norm):
    x0 = x[0]
    target = loss_target[0]
    me = 4 * lax.axis_index("x") + 2 * lax.axis_index("y") + lax.axis_index("c")

    (we_in,) = all_gather_two_level([even_w_in[0].astype(BF16)], [1], "gather_even_in")
    odd_small = jnp.pad(odd_norm, ((0, 7), (0, 0))) + jnp.pad(odd_conv_w[0], ((1, 4), (0, 0)))
    rest_axes, odd_axes = [1, 0, 1], [1, 0]
    rest_flight = exchange_start("gather", [even_pool_w[0].astype(BF16), even_w_out[0].astype(BF16), odd_small],
                                 rest_axes, we_in, "gather_even_rest_start")
    odd_flight = gather_two_level_start([odd_w_in[0].astype(BF16), odd_w_out[0].astype(BF16)],
                                        odd_axes, rest_flight[4], "gather_odd_start")
    scale = even_pool_scale
    ws = even_ws[0]
    bs_col = even_bs[0][:, :, None]
    g_fin = final_norm[None, :]

    z_e, h_e_t, we_in_t = norm_matmul(x0, even_norm + odd_flight[4][0:1, 0:1], we_in, "even_in", tn=1280)
    wp, we_out, odd_small = exchange_wait("gather", rest_flight, rest_axes, z_e, "gather_even_rest_wait")
    g_odd, conv_w = odd_small[0:1], odd_small[1:4]
    y_e, pooled = even_mix(z_e, wp, scale, ws, bs_col, "even_mix")
    x1, h_o, h_o_t = matmul_residual(y_e, we_out, x0, "even_out", next_gain=g_odd)
    odd_flight = gather_two_level_forward(odd_flight, odd_axes, x1, "gather_odd_forward")
    h_layouts = [h_o] + [permute_mats(h_o, [], gi, False, f"dilate_h_{gi}")[0] for gi in (1, 2)]
    wo_in, wo_out = gather_two_level_wait(odd_flight, odd_axes, h_layouts[2], "gather_odd_wait")
    z_o, wo_in_t = matmul_layouts(h_layouts, wo_in, "odd_in")
    attn = [attn_fwd(z_o, gi, f"attn_fwd_{gi}") for gi in range(3)]
    for gi in (1, 2):
        attn[gi] = permute_mats(attn[gi][0], [attn[gi][1]], gi, True, f"undilate_attn_{gi}")
    y_o, yc, lse_tot, conv = odd_mix([a[0] for a in attn], [a[1] for a in attn], z_o, conv_w, "odd_mix")
    dx2, loss_blk, dg_fin = out_proj_final_loss(y_o, wo_out, x1, g_fin, target, "odd_out_loss")

    dy_o, dwo_out = out_proj_bwd(dx2, wo_out.T, y_o, "odd_out_bwd")
    dz5_o, dyc, dd, dconv_w = odd_mix_bwd(dy_o, yc, conv, z_o, conv_w, "odd_mix_bwd")
    dqkv = [attn_bwd(z_o, dyc, lse_tot, dd, 0, "attn_bwd_0")]
    for gi in (1, 2):
        dyc_g, lse_g, dd_g = permute_mats(dyc, [lse_tot, dd], gi, False, f"dilate_dy_{gi}")
        dqkv.append(permute_slabs(attn_bwd(z_o, dyc_g, lse_g, dd_g, gi, f"attn_bwd_{gi}"),
                                  gi, True, f"undilate_dqkv_{gi}"))
    src_o = [(dqkv[0], ("slabs", 0)), (dqkv[1], ("slabs", 1)), (dqkv[2], ("slabs", 2)), (dz5_o, ("cols", 9, 5))]
    dwo_in = in_proj_dw(h_o_t, src_o, ODD_COLBLOCKS, "odd_in_dw")
    odd_grads = exchange_start("scatter", [dwo_in, dwo_out], [1, 0], dconv_w, "scatter_odd_start")
    dx1, dg_odd = in_proj_dx(src_o, ODD_COLBLOCKS, wo_in_t, x1, g_odd + odd_grads[4][0:1, 0:1], dx2, "odd_in_dx")

    dy_e, dwe_out = out_proj_bwd(dx1, we_out.T, y_e, "even_out_bwd")
    dz_e, dwp, dscale, dws, dbs_col = even_mix_bwd(dy_e, z_e, pooled, wp, scale, ws, bs_col, "even_mix_bwd")
    src_e = [(dz_e, ("cols", 0, 5))]
    even_grads_a = exchange_start("scatter", [dwp.astype(BF16), dwe_out], [1, 0], dscale, "scatter_even_rest_start")
    dwe_in = in_proj_dw(h_e_t, src_e, 5, "even_in_dw")
    even_grads_b = exchange_start("scatter", [dwe_in], [1], even_grads_a[4], "scatter_even_in_start")
    dx0, dg_even = in_proj_dx(src_e, 5, we_in_t, x0, even_norm + even_grads_b[4][0:1, 0:1], dx1, "even_in_dx")
    small = jnp.concatenate([
        _rows128(dg_even), _rows128(dscale), _rows128(dws), _rows128(dbs_col), _rows128(dg_fin),
        _rows128(dg_odd), _rows128(dconv_w), loss_blk], axis=0)
    n_small = small.shape[0]
    small_flight = exchange_start("gather", [small], [0], dg_fin, "gather_small_start")
    p_wo_in, p_wo_out = exchange_wait("scatter", odd_grads, [1, 0], small_flight[4], "scatter_odd_wait")
    p_wp, p_we_out = exchange_wait("scatter", even_grads_a, [1, 0], small_flight[4], "scatter_even_rest_wait")
    (p_we_in,) = exchange_wait("scatter", even_grads_b, [1], small_flight[4], "scatter_even_in_wait")

    two_d = lambda a, r, c: a.reshape(r, c)
    sharded = [
        ("even_w_in", p_we_in, even_w_in, m_even_w_in, v_even_w_in, (1024, 640)),
        ("even_pool_w", p_wp.reshape(N_DEV, 128, 256), even_pool_w, m_even_pool_w, v_even_pool_w, (128, 256)),
        ("even_w_out", p_we_out, even_w_out, m_even_w_out, v_even_w_out, (256, 1024)),
        ("odd_w_in", p_wo_in, odd_w_in, m_odd_w_in, v_odd_w_in, (1024, 1792)),
        ("odd_w_out", p_wo_out, odd_w_out, m_odd_w_out, v_odd_w_out, (256, 1024)),
    ]
    res = {}
    for name, g, w, m, v, (r, c) in sharded:
        outs = adamw(g, two_d(w, r, c), two_d(m, r, c), two_d(v, r, c), "adamw_" + name, slots=True)
        res[name] = [o.reshape(w.shape) for o in outs]

    (small_all,) = exchange_wait("gather", small_flight, [0], res["odd_w_out"][0], "gather_small_wait")
    small_sum = sum_slots(small_all.reshape(N_DEV, n_small, 128), "sum_small_grads")
    g_even_norm = small_sum[0:8].reshape(1, 1024)
    g_scale = small_sum[8:16].reshape(1, 1024)
    g_ws = small_sum[16:528]
    g_bs = small_sum[528:532]
    g_final = small_sum[536:544].reshape(1, 1024)
    g_odd_norm = lax.dynamic_slice_in_dim(small_sum[544:552], me, 1, axis=0)
    g_conv = lax.dynamic_index_in_dim(small_sum[552:576].reshape(3, 8, 128), me, axis=1, keepdims=False)
    plain = [
        ("even_norm", g_even_norm, even_norm, m_even_norm, v_even_norm, (1, 1024)),
        ("even_pool_scale", g_scale, even_pool_scale, m_even_pool_scale, v_even_pool_scale, (1, 1024)),
        ("even_ws", g_ws, even_ws, m_even_ws, v_even_ws, (512, 128)),
        ("even_bs", g_bs, even_bs, m_even_bs, v_even_bs, (4, 128)),
        ("odd_norm", g_odd_norm, odd_norm, m_odd_norm, v_odd_norm, (1, 128)),
        ("odd_conv_w", g_conv, odd_conv_w, m_odd_conv_w, v_odd_conv_w, (3, 128)),
        ("final_norm", g_final, final_norm, m_final_norm, v_final_norm, (1, 1024)),
    ]
    for name, g, w, m, v, (r, c) in plain:
        outs = adamw(two_d(g, r, c), two_d(w, r, c), two_d(m, r, c), two_d(v, r, c), "adamw_" + name, slots=False)
        res[name] = [o.reshape(w.shape) for o in outs]

    loss = small_sum[576, 0]
    order = ["even_norm", "even_w_in", "even_pool_w", "even_pool_scale", "even_ws", "even_bs", "even_w_out",
             "odd_norm", "odd_w_in", "odd_conv_w", "odd_w_out", "final_norm"]
    return (loss, dx0[None], *[res[n][0] for n in order], *[res[n][1] for n in order],
            *[res[n][2] for n in order], *[res[n][3] for n in order])
```

```python
import functools

import jax
import jax.numpy as jnp
from jax import lax
from jax.experimental import pallas as pl
from jax.experimental.pallas import tpu as pltpu

F32 = jnp.float32
BF16 = jnp.bfloat16
MESH_ID = pl.DeviceIdType.MESH

EPS = 1e-6
NEG = -1e30
N_DEV = 8
POOL_SIZES = (2, 4, 8, 16)
DILATIONS = (1, 4, 16)
N_HEADS = 24
HEADS_PER_GROUP = 8
HEAD_DIM = 128
ATTN_BLOCK = 128
CHUNK = 128
CB = 1024
HALO = 16
ODD_COLBLOCKS = 14
ADAM_LR = 0.001
ADAM_B1 = 0.9
ADAM_B2 = 0.999
ADAM_EPS = 1e-08
ADAM_WD = 0.01
ADAM_STEP = 10
VMEM_LIMIT = 52 * 1024 * 1024


def _cp(*sem):
    return pltpu.CompilerParams(dimension_semantics=sem, vmem_limit_bytes=VMEM_LIMIT)


def _dot(a, b):
    return jnp.dot(a, b, preferred_element_type=F32)


def _dot_nt(a, b):
    return lax.dot_general(a, b, (((1,), (1,)), ((), ())), preferred_element_type=F32)


def _dot_tn(a, b):
    return lax.dot_general(a, b, (((0,), (0,)), ((), ())), preferred_element_type=F32)


def _sigmoid(x):
    return 0.5 * jnp.tanh(0.5 * x) + 0.5


def _silu_and_grad(x):
    s = _sigmoid(x)
    return x * s, s * (1.0 + x * (1.0 - s))


def _sds(shape, dtype):
    return jax.ShapeDtypeStruct(shape, dtype)


def norm_matmul(x, g, w, name, tn, tm=1024, rows=256):
    t, d = x.shape
    n = w.shape[1]
    n_col = n // tn

    def body(x_ref, g_ref, w_ref, z_ref, ht_ref, wt_ref, h_ref):
        @pl.when(pl.program_id(1) == 0)
        def _():
            for c in range(tm // rows):
                rs = slice(c * rows, (c + 1) * rows)
                xv = x_ref[rs, :]
                r = lax.rsqrt(jnp.mean(xv * xv, axis=-1, keepdims=True) + EPS)
                h_ref[rs, :] = ((xv * r) * g_ref[...]).astype(BF16)
            ht_ref[...] = h_ref[...].T

        z_ref[...] = _dot(h_ref[...], w_ref[...]).astype(BF16)

        @pl.when(pl.program_id(0) == 0)
        def _():
            wt_ref[...] = w_ref[...].T

    return pl.pallas_call(
        body, name=name, grid=(t // tm, n_col),
        in_specs=[pl.BlockSpec((tm, d), lambda i, j: (i, 0)),
                  pl.BlockSpec((1, d), lambda i, j: (0, 0)),
                  pl.BlockSpec((d, tn), lambda i, j: (0, j))],
        out_specs=[pl.BlockSpec((tm, tn), lambda i, j: (i, j)),
                   pl.BlockSpec((d, tm), lambda i, j: (0, i)),
                   pl.BlockSpec((tn, d), lambda i, j: (jnp.where(i == 0, j, n_col), 0))],
        out_shape=[_sds((t, n), BF16), _sds((d, t), BF16), _sds((n + tn, d), BF16)],
        scratch_shapes=[pltpu.VMEM((tm, d), BF16)],
        compiler_params=_cp("arbitrary", "arbitrary"),
    )(x, g, w)


def matmul_residual(y, w, x, name, next_gain=None, tm=1024, rows=256):
    t, k = y.shape
    d = w.shape[1]

    def body(y_ref, w_ref, x_ref, *rest):
        o_ref = rest[-1] if next_gain is None else rest[-3]
        o_ref[...] = x_ref[...] + _dot(y_ref[...], w_ref[...])
        if next_gain is not None:
            g_ref, h_ref, ht_ref = rest[0], rest[-2], rest[-1]
            for c in range(tm // rows):
                rs = slice(c * rows, (c + 1) * rows)
                xv = o_ref[rs, :]
                r = lax.rsqrt(jnp.mean(xv * xv, axis=-1, keepdims=True) + EPS)
                h_ref[rs, :] = ((xv * r) * g_ref[...]).astype(BF16)
            ht_ref[...] = h_ref[...].T

    row = pl.BlockSpec((tm, d), lambda i: (i, 0))
    in_specs = [pl.BlockSpec((tm, k), lambda i: (i, 0)), pl.BlockSpec((k, d), lambda i: (0, 0)), row]
    if next_gain is None:
        return pl.pallas_call(
            body, name=name, grid=(t // tm,), in_specs=in_specs, out_specs=row, out_shape=_sds((t, d), F32),
            compiler_params=_cp("parallel"),
        )(y, w, x)
    return pl.pallas_call(
        body, name=name, grid=(t // tm,),
        in_specs=in_specs + [pl.BlockSpec((1, d), lambda i: (0, 0))],
        out_specs=[row, row, pl.BlockSpec((d, tm), lambda i: (0, i))],
        out_shape=[_sds((t, d), F32), _sds((t, d), BF16), _sds((d, t), BF16)],
        compiler_params=_cp("parallel"),
    )(y, w, x, next_gain)


def matmul_layouts(hs, w, name, tm=2048):
    t, d = hs[0].shape
    n_blocks = w.shape[1] // CB

    def layout_of(j):
        return jnp.where(j < 9, j // 3, 0)

    def column_of(j):
        return jnp.where(j < 9, (j % 3) * 3 + j // 3, j)

    def body(h0, h1, h2, w_ref, z_ref, wt_ref):
        layout = layout_of(pl.program_id(1))
        for v, h_ref in enumerate((h0, h1, h2)):
            @pl.when(layout == v)
            def _(h_ref=h_ref):
                z_ref[...] = _dot(h_ref[...], w_ref[...]).astype(BF16)
        @pl.when(pl.program_id(0) == 0)
        def _():
            wt_ref[...] = w_ref[...].T

    row = pl.BlockSpec((tm, d), lambda i, j: (i, 0))
    wt_block = lambda i, j: (jnp.where(i == 0, column_of(j), n_blocks), 0)
    return pl.pallas_call(
        body, name=name, grid=(t // tm, n_blocks),
        in_specs=[row, row, row, pl.BlockSpec((d, CB), lambda i, j: (0, column_of(j)))],
        out_specs=[pl.BlockSpec((tm, CB), lambda i, j: (i, column_of(j))), pl.BlockSpec((CB, d), wt_block)],
        out_shape=[_sds((t, n_blocks * CB), BF16), _sds(((n_blocks + 1) * CB, d), BF16)],
        compiler_params=_cp("arbitrary", "arbitrary"),
    )(*hs, w)


def out_proj_bwd(dx, wt, y, name, tm=512):
    t, d = dx.shape
    k = wt.shape[1]
    steps = t // tm
    half = d // 2

    def body(dx_ref, wt_ref, y_ref, dy_ref, dw_ref, acc):
        i = pl.program_id(0)
        dxb = dx_ref[...].astype(BF16)
        dy_ref[...] = _dot(dxb, wt_ref[...]).astype(BF16)

        @pl.when(i == 0)
        def _():
            acc[...] = jnp.zeros_like(acc)

        for c in range(2):
            acc[:, c * half:(c + 1) * half] += _dot_tn(y_ref[...], dxb[:, c * half:(c + 1) * half])

        @pl.when(i == steps - 1)
        def _():
            dw_ref[...] = acc[...].astype(BF16)

    return pl.pallas_call(
        body, name=name, grid=(steps,),
        in_specs=[pl.BlockSpec((tm, d), lambda i: (i, 0)),
                  pl.BlockSpec((d, k), lambda i: (0, 0)),
                  pl.BlockSpec((tm, k), lambda i: (i, 0))],
        out_specs=[pl.BlockSpec((tm, k), lambda i: (i, 0)),
                   pl.BlockSpec((k, d), lambda i: (0, 0))],
        out_shape=[_sds((t, k), BF16), _sds((k, d), BF16)],
        scratch_shapes=[pltpu.VMEM((k, d), F32)],
        compiler_params=_cp("arbitrary"),
    )(dx, wt, y)


def _source_block(kind, rows):
    if kind[0] == "cols":
        return (rows, CB)
    return (None, rows, CB)


def _source_active(kind, j):
    if kind[0] == "cols":
        _, first, n = kind
        return (j >= first) & (j < first + n), jnp.clip(j - first, 0, n - 1)
    _, group = kind
    return (j < 9) & (j % 3 == group), jnp.clip(j // 3, 0, 2)


def _source_index(kind, row_block, inner):
    if kind[0] == "cols":
        return (row_block, inner)
    return (inner, row_block, 0)


def in_proj_dw(ht, sources, n_blocks, name, tk=1024):
    d, t = ht.shape
    steps = t // tk
    ns = len(sources)

    def body(*refs):
        h_ref, src_refs, dw_ref, acc = refs[0], refs[1:1 + ns], refs[1 + ns], refs[2 + ns]
        j, k = pl.program_id(0), pl.program_id(1)

        @pl.when(k == 0)
        def _():
            acc[...] = jnp.zeros_like(acc)

        for s, (_, kind) in enumerate(sources):
            active, _ = _source_active(kind, j)

            @pl.when(active)
            def _(s=s):
                acc[...] += _dot(h_ref[...], src_refs[s][...])

        @pl.when(k == steps - 1)
        def _():
            dw_ref[...] = acc[...].astype(BF16)

    def src_spec(kind):
        def index(j, k):
            active, inner = _source_active(kind, j)
            return _source_index(kind, jnp.where(active, k, 0), inner)
        return pl.BlockSpec(_source_block(kind, tk), index)

    return pl.pallas_call(
        body, name=name, grid=(n_blocks, steps),
        in_specs=[pl.BlockSpec((d, tk), lambda j, k: (0, k))] + [src_spec(kind) for _, kind in sources],
        out_specs=pl.BlockSpec((d, CB), lambda j, k: (0, j)),
        out_shape=_sds((d, n_blocks * CB), BF16),
        scratch_shapes=[pltpu.VMEM((d, CB), F32)],
        compiler_params=_cp("parallel", "arbitrary"),
    )(ht, *[a for a, _ in sources])


def in_proj_dx(sources, n_blocks, wt, x, g, dres, name, tm=1024, rows=256):
    t, d = x.shape
    ns = len(sources)

    def body(*refs):
        src_refs = refs[:ns]
        w_ref, x_hbm, g_ref, dres_hbm, dx_ref, dg_ref, acc, x_ref, dres_ref, sems = refs[ns:]
        i, p = pl.program_id(0), pl.program_id(1)
        tile = pl.ds(pl.multiple_of(i * tm, tm), tm)
        fetch = [pltpu.make_async_copy(x_hbm.at[tile, :], x_ref, sems.at[0]),
                 pltpu.make_async_copy(dres_hbm.at[tile, :], dres_ref, sems.at[1])]

        @pl.when(p == 0)
        def _():
            acc[...] = jnp.zeros_like(acc)
            for cp in fetch:
                cp.start()

        for s, (_, kind) in enumerate(sources):
            active, _ = _source_active(kind, p)

            @pl.when(active)
            def _(s=s):
                acc[...] += _dot(src_refs[s][...], w_ref[...])

        @pl.when(p == n_blocks - 1)
        def _():
            for cp in fetch:
                cp.wait()
            part = jnp.zeros((1, d), F32)
            for c in range(tm // rows):
                rs = slice(c * rows, (c + 1) * rows)
                xv = x_ref[rs, :]
                r = lax.rsqrt(jnp.mean(xv * xv, axis=-1, keepdims=True) + EPS)
                xn = xv * r
                dh = acc[rs, :]
                dhg = dh * g_ref[...]
                dx_ref[rs, :] = dres_ref[rs, :] + r * (dhg - xn * jnp.mean(dhg * xn, axis=-1, keepdims=True))
                part = part + jnp.sum(dh * xn, axis=0, keepdims=True)

            @pl.when(i == 0)
            def _():
                dg_ref[...] = part

            @pl.when(i > 0)
            def _():
                dg_ref[...] += part

    def src_spec(kind):
        def index(i, p):
            if kind[0] == "cols":
                _, first, n = kind
                inner = jnp.clip(p - first, 0, n - 1)
            else:
                _, group = kind
                inner = jnp.clip((p - group + 2) // 3, 0, 2)
            return _source_index(kind, i, inner)
        return pl.BlockSpec(_source_block(kind, tm), index)

    any_spec = pl.BlockSpec(memory_space=pl.ANY)
    return pl.pallas_call(
        body, name=name, grid=(t // tm, n_blocks),
        in_specs=[src_spec(kind) for _, kind in sources] + [
            pl.BlockSpec((CB, d), lambda i, p: (p, 0)), any_spec, pl.BlockSpec((1, d), lambda i, p: (0, 0)), any_spec],
        out_specs=[pl.BlockSpec((tm, d), lambda i, p: (i, 0)),
                   pl.BlockSpec((1, d), lambda i, p: (0, 0))],
        out_shape=[_sds((t, d), F32), _sds((1, d), F32)],
        scratch_shapes=[pltpu.VMEM((tm, d), F32), pltpu.VMEM((tm, d), F32), pltpu.VMEM((tm, d), F32),
                        pltpu.SemaphoreType.DMA((2,))],
        compiler_params=_cp("arbitrary", "arbitrary"),
    )(*[a for a, _ in sources], wt, x, g, dres)


def _window_counts(first_row, rows, w):
    t = first_row + lax.broadcasted_iota(jnp.int32, (rows, 1), 0)
    return jnp.minimum(t + 1, w).astype(F32)


def _tril_bf16(ws_ref, g):
    r = lax.broadcasted_iota(jnp.int32, (CHUNK, CHUNK), 0)
    c = lax.broadcasted_iota(jnp.int32, (CHUNK, CHUNK), 1)
    return jnp.where(r >= c, ws_ref[g], 0.0).astype(BF16), r >= c


def even_mix(z, wp, scale, ws, bs_col, name, tm=512):
    t = z.shape[0]
    gd = CB // len(POOL_SIZES)

    def body(a_ref, ga_ref, u_ref, v_ref, gb_ref, wp_ref, sc_ref, ws_ref, bs_ref, y_ref, pooled_ref, aext):
        i = pl.program_id(0)

        @pl.when(i == 0)
        def _():
            aext[0:HALO, :] = jnp.zeros((HALO, CB), F32)

        @pl.when(i > 0)
        def _():
            aext[0:HALO, :] = aext[tm:tm + HALO, :]

        aext[HALO:HALO + tm, :] = a_ref[...].astype(F32)
        for g, w in enumerate(POOL_SIZES):
            cols = slice(g * gd, (g + 1) * gd)
            tok = aext[HALO:HALO + tm, cols]
            s = tok
            for k in range(1, w):
                s = s + aext[HALO - k:HALO - k + tm, cols]
            pooled = (s / _window_counts(i * tm, tm, w) - tok).astype(BF16)
            pooled_ref[:, cols] = pooled
            mixed = _dot(pooled, wp_ref[g])
            silu_a, _ = _silu_and_grad(ga_ref[:, cols].astype(F32))
            y_ref[:, cols] = (mixed * sc_ref[:, cols] * silu_a).astype(BF16)
        for g in range(4):
            cols = slice(g * gd, (g + 1) * gd)
            wg, _ = _tril_bf16(ws_ref, g)
            for c in range(tm // CHUNK):
                rows = slice(c * CHUNK, (c + 1) * CHUNK)
                m = _dot(wg, v_ref[rows, cols]) + bs_ref[g]
                silu_b, _ = _silu_and_grad(gb_ref[rows, cols].astype(F32))
                y_ref[rows, CB + g * gd:CB + (g + 1) * gd] = (
                    u_ref[rows, cols].astype(F32) * m * silu_b).astype(BF16)

    zspec = lambda cb: pl.BlockSpec((tm, CB), lambda i, cb=cb: (i, cb))
    full = lambda shape: pl.BlockSpec(shape, lambda i: (0,) * len(shape))
    return pl.pallas_call(
        body, name=name, grid=(t // tm,),
        in_specs=[zspec(0), zspec(1), zspec(2), zspec(3), zspec(4),
                  full(wp.shape), full(scale.shape), full(ws.shape), full(bs_col.shape)],
        out_specs=[pl.BlockSpec((tm, 2 * CB), lambda i: (i, 0)), pl.BlockSpec((tm, CB), lambda i: (i, 0))],
        out_shape=[_sds((t, 2 * CB), BF16), _sds((t, CB), BF16)],
        scratch_shapes=[pltpu.VMEM((HALO + tm, CB), F32)],
        compiler_params=_cp("arbitrary"),
    )(z, z, z, z, z, wp, scale, ws, bs_col)


def even_mix_bwd(dy, z, pooled, wp, scale, ws, bs_col, name, tm=512):
    t = z.shape[0]
    nt = t // tm
    gd = CB // len(POOL_SIZES)

    def body(dya_ref, dyb_ref, ga_ref, u_ref, v_ref, gb_ref, pooled_ref, wp_ref, sc_ref, ws_ref, bs_ref,
             dz_ref, dwp_ref, dsc_ref, dws_ref, dbs_ref, dpext):
        i = pl.program_id(0)
        tile = nt - 1 - i

        @pl.when(i == 0)
        def _():
            dpext[tm:tm + HALO, :] = jnp.zeros((HALO, CB), F32)
            dwp_ref[...] = jnp.zeros_like(dwp_ref)
            dsc_ref[...] = jnp.zeros_like(dsc_ref)
            dws_ref[...] = jnp.zeros_like(dws_ref)
            dbs_ref[...] = jnp.zeros_like(dbs_ref)

        @pl.when(i > 0)
        def _():
            dpext[tm:tm + HALO, :] = dpext[0:HALO, :]

        for g, w in enumerate(POOL_SIZES):
            cols = slice(g * gd, (g + 1) * gd)
            pooled_g = pooled_ref[:, cols]
            mixed = _dot(pooled_g, wp_ref[g])
            silu_a, dsilu_a = _silu_and_grad(ga_ref[:, cols].astype(F32))
            dya = dya_ref[:, cols].astype(F32)
            sc = sc_ref[:, cols]
            dmixed = (dya * sc * silu_a).astype(BF16)
            dsc_ref[:, cols] += jnp.sum(dya * mixed * silu_a, axis=0, keepdims=True)
            dz_ref[:, CB + g * gd:CB + (g + 1) * gd] = (dya * mixed * sc * dsilu_a).astype(BF16)
            dwp_ref[g] += _dot_tn(pooled_g, dmixed)
            dpooled = _dot_nt(dmixed, wp_ref[g])
            dpext[0:tm, cols] = dpooled / _window_counts(tile * tm, tm, w)
            s = dpext[0:tm, cols]
            for k in range(1, w):
                s = s + dpext[k:k + tm, cols]
            dz_ref[:, cols] = (s - dpooled).astype(BF16)
        for g in range(4):
            cols = slice(g * gd, (g + 1) * gd)
            wg, lower = _tril_bf16(ws_ref, g)
            dws_g = jnp.zeros((CHUNK, CHUNK), F32)
            dbs_g = jnp.zeros((CHUNK, 1), F32)
            for c in range(tm // CHUNK):
                rows = slice(c * CHUNK, (c + 1) * CHUNK)
                vb = v_ref[rows, cols]
                m = _dot(wg, vb) + bs_ref[g]
                gbv = gb_ref[rows, cols].astype(F32)
                silu_b, dsilu_b = _silu_and_grad(gbv)
                dyb = dyb_ref[rows, cols].astype(F32)
                uv = u_ref[rows, cols].astype(F32)
                dm = dyb * silu_b * uv
                dmb = dm.astype(BF16)
                dz_ref[rows, 2 * CB + g * gd:2 * CB + (g + 1) * gd] = (dyb * silu_b * m).astype(BF16)
                dz_ref[rows, 3 * CB + g * gd:3 * CB + (g + 1) * gd] = _dot_tn(wg, dmb).astype(BF16)
                dz_ref[rows, 4 * CB + g * gd:4 * CB + (g + 1) * gd] = (dyb * uv * m * dsilu_b).astype(BF16)
                dws_g = dws_g + _dot_nt(dmb, vb)
                dbs_g = dbs_g + jnp.sum(dm, axis=1, keepdims=True)
            dws_ref[g] += jnp.where(lower, dws_g, 0.0)
            dbs_ref[g] += dbs_g

    rev = lambda cb: pl.BlockSpec((tm, CB), lambda i, cb=cb: (nt - 1 - i, cb))
    full = lambda shape: pl.BlockSpec(shape, lambda i: (0,) * len(shape))
    return pl.pallas_call(
        body, name=name, grid=(nt,),
        in_specs=[rev(0), rev(1), rev(1), rev(2), rev(3), rev(4), rev(0),
                  full(wp.shape), full(scale.shape), full(ws.shape), full(bs_col.shape)],
        out_specs=[pl.BlockSpec((tm, 5 * CB), lambda i: (nt - 1 - i, 0)),
                   full(wp.shape), full(scale.shape), full(ws.shape), full(bs_col.shape)],
        out_shape=[_sds((t, 5 * CB), BF16), _sds(wp.shape, F32), _sds(scale.shape, F32),
                   _sds(ws.shape, F32), _sds(bs_col.shape, F32)],
        scratch_shapes=[pltpu.VMEM((tm + HALO, CB), F32)],
        compiler_params=_cp("arbitrary"),
    )(dy, dy, z, z, z, z, pooled, wp, scale, ws, bs_col)


def _slope(group, head):
    return float(2.0 ** (-8.0 * (group * HEADS_PER_GROUP + head + 1) / N_HEADS))


def _band(dilation):
    qi = lax.broadcasted_iota(jnp.int32, (ATTN_BLOCK, ATTN_BLOCK), 0)
    ki = lax.broadcasted_iota(jnp.int32, (ATTN_BLOCK, ATTN_BLOCK), 1)
    dist_prev = ((qi + ATTN_BLOCK - ki) * dilation).astype(F32)
    dist_cur = ((qi - ki) * dilation).astype(F32)
    return dist_prev, ki >= qi, dist_cur, ki <= qi


def _permute_rows(name, d, inverse, arrays, in_specs, out_shapes, out_specs, n_chunks, widths):
    rows = ATTN_BLOCK * d
    t = out_shapes[0].shape[-2]
    n = len(arrays)

    lanes = HEAD_DIM

    def body(*refs):
        for a, o, s, w in zip(refs[:n], refs[n:2 * n], refs[2 * n:], widths):
            for k in range(w // lanes):
                cols = slice(k * lanes, (k + 1) * lanes)
                if inverse:
                    for r in range(d):
                        s[k, pl.ds(r, ATTN_BLOCK, stride=d), :] = (
                            a[r * ATTN_BLOCK:(r + 1) * ATTN_BLOCK, cols].astype(F32))
                    o[:, cols] = s[k].astype(o.dtype)
                else:
                    s[k] = a[:, cols].astype(F32)
                    for r in range(d):
                        o[r * ATTN_BLOCK:(r + 1) * ATTN_BLOCK, cols] = (
                            s[k, pl.ds(r, ATTN_BLOCK, stride=d), :].astype(o.dtype))

    return pl.pallas_call(
        body, name=name, grid=(t // rows, n_chunks),
        in_specs=in_specs, out_specs=out_specs, out_shape=out_shapes,
        scratch_shapes=[pltpu.VMEM((w // lanes, rows, lanes), F32) for w in widths],
        compiler_params=_cp("parallel", "arbitrary"),
    )(*arrays)


def _chunk_width(d):
    return 4 * CB // d


def permute_mats(wide, narrow, group, inverse, name):
    d = DILATIONS[group]
    rows, cw = ATTN_BLOCK * d, _chunk_width(d)
    wide_spec = pl.BlockSpec((rows, cw), lambda i, c: (i, c))
    narrow_spec = pl.BlockSpec((rows, HEAD_DIM), lambda i, c: (i, 0))
    arrays = [wide] + list(narrow)
    specs = [wide_spec] + [narrow_spec] * len(narrow)
    return _permute_rows(name, d, inverse, arrays, specs, [_sds(a.shape, a.dtype) for a in arrays], specs,
                         CB // cw, [cw] + [HEAD_DIM] * len(narrow))


def permute_slabs(s, group, inverse, name):
    d = DILATIONS[group]
    rows, cw = ATTN_BLOCK * d, _chunk_width(d)
    per = CB // cw
    spec = pl.BlockSpec((None, rows, cw), lambda i, c: (c // per, i, c % per))
    (out,) = _permute_rows(name, d, inverse, [s], [spec], [_sds(s.shape, s.dtype)], [spec], 3 * per, [cw])
    return out


def _lane_pack(cols):
    rows = cols[0].shape[0]
    lane = lax.broadcasted_iota(jnp.int32, (rows, HEAD_DIM), 1)
    out = jnp.zeros((rows, HEAD_DIM), F32)
    for h, c in enumerate(cols):
        out = jnp.where(lane == h, c, out)
    return out


def _qkv_col(group, from_z):
    return (lambda which: which * 3 + group) if from_z else (lambda which: which)


SCORE_SCALE = HEAD_DIM ** -0.5
LOG2_E = 1.4426950408889634
EXP2_SCALE = SCORE_SCALE * LOG2_E


def _fill_bias(bias_ref, group, d):
    dist_p, ok_p, dist_c, ok_c = _band(d)
    for h in range(HEADS_PER_GROUP):
        k = -_slope(group, h) / SCORE_SCALE
        bias_ref[h, :, 0:ATTN_BLOCK] = jnp.where(ok_p, k * dist_p, NEG)
        bias_ref[h, :, ATTN_BLOCK:2 * ATTN_BLOCK] = jnp.where(ok_c, k * dist_c, NEG)


STEP_BLOCKS = 4


def _step_plan(d):
    return d < STEP_BLOCKS, max(d // STEP_BLOCKS, 1)


def _joined(prev_ref, cur_ref, j, cols, chained):
    rows = slice(j * ATTN_BLOCK, (j + 1) * ATTN_BLOCK)
    if not chained:
        before = prev_ref[rows, cols]
    elif j == 0:
        before = prev_ref[(STEP_BLOCKS - 1) * ATTN_BLOCK:STEP_BLOCKS * ATTN_BLOCK, cols]
    else:
        before = cur_ref[(j - 1) * ATTN_BLOCK:j * ATTN_BLOCK, cols]
    return jnp.concatenate([before, cur_ref[rows, cols]], axis=0)


def _raw_scores(q, kk, bias_ref, h, has_prev):
    is_cur = lax.broadcasted_iota(jnp.int32, (ATTN_BLOCK, 2 * ATTN_BLOCK), 1) >= ATTN_BLOCK
    return jnp.where(jnp.logical_or(is_cur, has_prev), _dot_nt(q, kk) + bias_ref[h], NEG)


def attn_fwd(qkv, group, name):
    t = qkv.shape[0]
    d = DILATIONS[group]
    sub = STEP_BLOCKS
    chained, back = _step_plan(d)
    token_order = d == sub
    col = _qkv_col(group, qkv.shape[1] != 3 * CB)
    heads = range(HEADS_PER_GROUP)

    def body(q_ref, kp_ref, kc_ref, vp_ref, vc_ref, o_ref, lse_ref, bias_ref, *stage):
        b = pl.program_id(0)

        @pl.when(b == 0)
        def _():
            _fill_bias(bias_ref, group, d)

        hs = [slice(h * HEAD_DIM, (h + 1) * HEAD_DIM) for h in heads]
        for j in range(sub):
            rows = slice(j * ATTN_BLOCK, (j + 1) * ATTN_BLOCK)
            has_prev = True if chained and j > 0 else b >= back
            s = [_raw_scores(q_ref[rows, hs[h]], _joined(kp_ref, kc_ref, j, hs[h], chained), bias_ref, h, has_prev)
                 for h in heads]
            m = [jnp.max(t_, axis=1, keepdims=True) for t_ in s]
            e = [jnp.exp2(EXP2_SCALE * (t_ - m[h])) for h, t_ in enumerate(s)]
            l = [jnp.sum(e_, axis=1, keepdims=True) for e_ in e]
            o = [_dot(e_.astype(BF16), _joined(vp_ref, vc_ref, j, hs[h], chained)) * (1.0 / l[h])
                 for h, e_ in enumerate(e)]
            lse_j = _lane_pack([SCORE_SCALE * m[h] + jnp.log(l[h]) for h in heads])
            if token_order:
                strided = pl.ds(j, ATTN_BLOCK, stride=d)
                for h in heads:
                    stage[0][h, strided, :] = o[h]
                stage[0][HEADS_PER_GROUP, strided, :] = lse_j
            else:
                for h in heads:
                    o_ref[rows, hs[h]] = o[h].astype(BF16)
                lse_ref[rows, :] = lse_j
        if token_order:
            for h in heads:
                o_ref[:, hs[h]] = stage[0][h].astype(BF16)
            lse_ref[...] = stage[0][HEADS_PER_GROUP]

    blk = (sub * ATTN_BLOCK, CB)
    cur = lambda which: pl.BlockSpec(blk, lambda b: (b, col(which)))
    prev = lambda which: pl.BlockSpec(blk, lambda b: (jnp.maximum(b - back, 0), col(which)))
    scratch = [pltpu.VMEM((HEADS_PER_GROUP, ATTN_BLOCK, 2 * ATTN_BLOCK), F32)]
    if token_order:
        scratch.append(pltpu.VMEM((HEADS_PER_GROUP + 1, blk[0], HEAD_DIM), F32))
    return pl.pallas_call(
        body, name=name, grid=(t // blk[0],),
        in_specs=[cur(0), prev(1), cur(1), prev(2), cur(2)],
        out_specs=[pl.BlockSpec(blk, lambda b: (b, 0)), pl.BlockSpec((blk[0], HEAD_DIM), lambda b: (b, 0))],
        out_shape=[_sds((t, CB), BF16), _sds((t, HEAD_DIM), F32)],
        scratch_shapes=scratch,
        compiler_params=_cp("arbitrary"),
    )(qkv, qkv, qkv, qkv, qkv)


def attn_bwd(z, dyc, lse, dd, group, name):
    t = z.shape[0]
    d = DILATIONS[group]
    sub = STEP_BLOCKS
    chained, back = _step_plan(d)
    nb = t // (ATTN_BLOCK * d) // (sub if chained else 1)
    col = _qkv_col(group, z.shape[1] != 3 * CB)
    heads = range(HEADS_PER_GROUP)
    steps = back * nb
    token_order = d == sub

    def body(q_ref, kp_ref, kc_ref, vp_ref, vc_ref, dy_ref, lse_ref, dd_ref, out_ref, carry, bias_ref, *stage):
        s_ = pl.program_id(0)
        n = s_ % nb
        hs = [slice(h * HEAD_DIM, (h + 1) * HEAD_DIM) for h in heads]
        rows = [slice(j * ATTN_BLOCK, (j + 1) * ATTN_BLOCK) for j in range(sub)]

        def put(w, j, h, value):
            if token_order:
                stage[0][w * HEADS_PER_GROUP + h, pl.ds(j, ATTN_BLOCK, stride=d), :] = value
            else:
                out_ref[w, rows[j], hs[h]] = value.astype(BF16)

        def put_done():
            if token_order:
                for w in range(3):
                    for h in heads:
                        out_ref[w, :, hs[h]] = stage[0][w * HEADS_PER_GROUP + h].astype(BF16)

        @pl.when(s_ == 0)
        def _():
            _fill_bias(bias_ref, group, d)
            carry[...] = jnp.zeros_like(carry)

        @pl.when(s_ == steps)
        def _():
            for w in range(3):
                for j in range(sub):
                    for h in heads:
                        put(w, j, h, carry[w, rows[j], hs[h]])
            put_done()

        @pl.when(s_ < steps)
        def _():
            old = [[[carry[w, rows[j], hs[h]] for h in heads] for j in range(sub)] for w in range(3)]
            grads = []
            first, second = slice(0, ATTN_BLOCK), slice(ATTN_BLOCK, 2 * ATTN_BLOCK)

            def emit(j, grads):
                for h in heads:
                    put(0, j, h, old[0][j][h])
                    carry[0, rows[j], hs[h]] = grads[j][h][0]
                    for w in (1, 2):
                        if not chained:
                            done = old[w][j][h] + grads[j][h][w][first]
                            kept = grads[j][h][w][second]
                        else:
                            done = old[w][j][h] + grads[0][h][w][first] if j == sub - 1 else old[w][j][h]
                            kept = grads[j][h][w][second]
                            if j < sub - 1:
                                kept = kept + grads[j + 1][h][w][first]
                        put(w, j, h, done)
                        carry[w, rows[j], hs[h]] = kept

            for j in range(sub):
                has_prev = True if chained and j > 0 else n > 0
                q = [q_ref[rows[j], hs[h]] for h in heads]
                kk = [_joined(kp_ref, kc_ref, j, hs[h], chained) for h in heads]
                dy = [dy_ref[rows[j], hs[h]] for h in heads]
                s = [_raw_scores(q[h], kk[h], bias_ref, h, has_prev) for h in heads]
                p = [jnp.exp2(EXP2_SCALE * t_ - lse_ref[rows[j], h:h + 1] * LOG2_E) for h, t_ in enumerate(s)]
                ds = [(p[h] * (_dot_nt(dy[h], _joined(vp_ref, vc_ref, j, hs[h], chained)) - dd_ref[rows[j], h:h + 1])
                       * SCORE_SCALE).astype(BF16) for h in heads]
                grads.append([(_dot(ds[h], kk[h]), _dot_tn(ds[h], q[h]), _dot_tn(p[h].astype(BF16), dy[h]))
                              for h in heads])
                if not chained:
                    emit(j, grads)
            if chained:
                for j in range(sub):
                    emit(j, grads)
            put_done()

    blk = (sub * ATTN_BLOCK, CB)
    scratch = [pltpu.VMEM((3, blk[0], CB), F32), pltpu.VMEM((HEADS_PER_GROUP, ATTN_BLOCK, 2 * ATTN_BLOCK), F32)]
    if token_order:
        scratch.append(pltpu.VMEM((3 * HEADS_PER_GROUP, blk[0], HEAD_DIM), F32))

    def row_of(s_):
        s_ = jnp.clip(s_, 0, steps - 1)
        return (s_ % nb) * back + s_ // nb

    def prev_row_of(s_):
        s_ = jnp.clip(s_, 0, steps - 1)
        return row_of(s_) - jnp.where(s_ % nb > 0, back, 0)

    cur = lambda which: pl.BlockSpec(blk, lambda s_: (row_of(s_), col(which)))
    prev = lambda which: pl.BlockSpec(blk, lambda s_: (prev_row_of(s_), col(which)))
    narrow = pl.BlockSpec((blk[0], HEAD_DIM), lambda s_: (row_of(s_), 0))
    return pl.pallas_call(
        body, name=name, grid=(steps + 1,),
        in_specs=[cur(0), prev(1), cur(1), prev(2), cur(2), pl.BlockSpec(blk, lambda s_: (row_of(s_), 0)), narrow, narrow],
        out_specs=pl.BlockSpec((3, blk[0], CB), lambda s_: (0, row_of(s_ - 1), 0)),
        out_shape=_sds((3, t, CB), BF16),
        scratch_shapes=scratch,
        compiler_params=_cp("arbitrary"),
    )(z, z, z, z, z, dyc, lse, dd)


def odd_mix(os_, lses, z, conv_w, name, tm=512):
    t = z.shape[0]

    def body(o0, o1, o2, l0, l1, l2, gc_ref, db_ref, dc_ref, dx_ref, gd_ref, cw_ref,
             y_ref, yc_ref, lse_ref, conv_ref, zext):
        i = pl.program_id(0)
        a0, a1, a2 = l0[...], l1[...], l2[...]
        m = jnp.maximum(jnp.maximum(a0, a1), a2)
        tot = m + jnp.log(jnp.exp(a0 - m) + jnp.exp(a1 - m) + jnp.exp(a2 - m))
        lse_ref[...] = tot
        w0, w1, w2 = jnp.exp(a0 - tot), jnp.exp(a1 - tot), jnp.exp(a2 - tot)
        for h in range(HEADS_PER_GROUP):
            hs = slice(h * HEAD_DIM, (h + 1) * HEAD_DIM)
            yc = (w0[:, h:h + 1] * o0[:, hs].astype(F32) + w1[:, h:h + 1] * o1[:, hs].astype(F32)
                  + w2[:, h:h + 1] * o2[:, hs].astype(F32))
            yc_ref[:, hs] = yc.astype(BF16)
            silu_c, _ = _silu_and_grad(gc_ref[:, hs].astype(F32))
            y_ref[:, hs] = (yc * silu_c).astype(BF16)

        @pl.when(i == 0)
        def _():
            zext[0:HALO, :] = jnp.zeros((HALO, CB), F32)

        @pl.when(i > 0)
        def _():
            zext[0:HALO, :] = zext[tm:tm + HALO, :]

        zext[HALO:HALO + tm, :] = dc_ref[...].astype(F32) * dx_ref[...].astype(F32)
        conv = (cw_ref[0:1, :] * zext[HALO - 2:HALO - 2 + tm, :] + cw_ref[1:2, :] * zext[HALO - 1:HALO - 1 + tm, :]
                + cw_ref[2:3, :] * zext[HALO:HALO + tm, :])
        conv_ref[...] = conv.astype(BF16)
        silu_d, _ = _silu_and_grad(gd_ref[...].astype(F32))
        y_ref[:, CB:2 * CB] = (db_ref[...].astype(F32) * conv * silu_d).astype(BF16)

    row = pl.BlockSpec((tm, CB), lambda i: (i, 0))
    narrow = pl.BlockSpec((tm, HEAD_DIM), lambda i: (i, 0))
    zspec = lambda cb: pl.BlockSpec((tm, CB), lambda i, cb=cb: (i, cb))
    return pl.pallas_call(
        body, name=name, grid=(t // tm,),
        in_specs=[row] * 3 + [narrow] * 3 + [zspec(9), zspec(10), zspec(11), zspec(12), zspec(13),
                                             pl.BlockSpec(conv_w.shape, lambda i: (0, 0))],
        out_specs=[pl.BlockSpec((tm, 2 * CB), lambda i: (i, 0)), row, narrow, row],
        out_shape=[_sds((t, 2 * CB), BF16), _sds((t, CB), BF16), _sds((t, HEAD_DIM), F32), _sds((t, CB), BF16)],
        scratch_shapes=[pltpu.VMEM((HALO + tm, CB), F32)],
        compiler_params=_cp("arbitrary"),
    )(*os_, *lses, z, z, z, z, z, conv_w)


def odd_mix_bwd(dy, yc, conv, z, conv_w, name, tm=256):
    t = z.shape[0]
    nt = t // tm

    def body(dyc_ref, dyd_ref, yc_ref, conv_ref, gc_ref, db_ref, dc_ref, dx_ref, gd_ref, cw_ref,
             dz_ref, dyo_ref, dd_ref, dcw_ref, dcext):
        i = pl.program_id(0)

        @pl.when(i == 0)
        def _():
            dcext[tm:tm + HALO, :] = jnp.zeros((HALO, CB), F32)
            dcw_ref[...] = jnp.zeros_like(dcw_ref)

        @pl.when(i > 0)
        def _():
            dcext[tm:tm + HALO, :] = dcext[0:HALO, :]

        silu_c, dsilu_c = _silu_and_grad(gc_ref[...].astype(F32))
        dyc = dyc_ref[...].astype(F32)
        ycv = yc_ref[...].astype(F32)
        dyo = dyc * silu_c
        dyo_ref[...] = dyo.astype(BF16)
        dz_ref[:, 0:CB] = (dyc * ycv * dsilu_c).astype(BF16)
        prod = dyo * ycv
        dd_ref[...] = _lane_pack([jnp.sum(prod[:, h * HEAD_DIM:(h + 1) * HEAD_DIM], axis=1, keepdims=True)
                                  for h in range(HEADS_PER_GROUP)])

        silu_d, dsilu_d = _silu_and_grad(gd_ref[...].astype(F32))
        dyd = dyd_ref[...].astype(F32)
        convv = conv_ref[...].astype(F32)
        dbv = db_ref[...].astype(F32)
        dz_ref[:, CB:2 * CB] = (dyd * convv * silu_d).astype(BF16)
        dz_ref[:, 4 * CB:5 * CB] = (dyd * dbv * convv * dsilu_d).astype(BF16)
        dcext[0:tm, :] = dyd * dbv * silu_d
        dcv, dxv = dc_ref[...].astype(F32), dx_ref[...].astype(F32)
        zc = dcv * dxv
        d0, d1, d2 = dcext[0:tm, :], dcext[1:1 + tm, :], dcext[2:2 + tm, :]
        dzc = cw_ref[2:3, :] * d0 + cw_ref[1:2, :] * d1 + cw_ref[0:1, :] * d2
        dz_ref[:, 2 * CB:3 * CB] = (dzc * dxv).astype(BF16)
        dz_ref[:, 3 * CB:4 * CB] = (dzc * dcv).astype(BF16)
        dcw_ref[0:1, :] += jnp.sum(zc * d2, axis=0, keepdims=True)
        dcw_ref[1:2, :] += jnp.sum(zc * d1, axis=0, keepdims=True)
        dcw_ref[2:3, :] += jnp.sum(zc * d0, axis=0, keepdims=True)

    rev = lambda cb: pl.BlockSpec((tm, CB), lambda i, cb=cb: (nt - 1 - i, cb))
    return pl.pallas_call(
        body, name=name, grid=(nt,),
        in_specs=[rev(0), rev(1), rev(0), rev(0), rev(9), rev(10), rev(11), rev(12), rev(13),
                  pl.BlockSpec(conv_w.shape, lambda i: (0, 0))],
        out_specs=[pl.BlockSpec((tm, 5 * CB), lambda i: (nt - 1 - i, 0)), rev(0),
                   pl.BlockSpec((tm, HEAD_DIM), lambda i: (nt - 1 - i, 0)),
                   pl.BlockSpec(conv_w.shape, lambda i: (0, 0))],
        out_shape=[_sds((t, 5 * CB), BF16), _sds((t, CB), BF16), _sds((t, HEAD_DIM), F32), _sds(conv_w.shape, F32)],
        scratch_shapes=[pltpu.VMEM((tm + HALO, CB), F32)],
        compiler_params=_cp("arbitrary"),
    )(dy, dy, yc, conv, z, z, z, z, z, conv_w)


def out_proj_final_loss(y, w, x, g, target, name, tm=1024, rows=256):
    t, d = x.shape
    k = y.shape[1]

    def body(y_ref, w_ref, x_ref, g_ref, t_ref, dx_ref, loss_ref, dg_ref):
        i = pl.program_id(0)
        dx_ref[...] = x_ref[...] + _dot(y_ref[...], w_ref[...])
        gv = g_ref[...]
        loss = jnp.zeros((1, 1), F32)
        dg = jnp.zeros((1, d), F32)
        for c in range(tm // rows):
            rs = slice(c * rows, (c + 1) * rows)
            xv = dx_ref[rs, :]
            r = lax.rsqrt(jnp.mean(xv * xv, axis=-1, keepdims=True) + EPS)
            xn = xv * r
            err = xn * gv - t_ref[rs, :]
            loss = loss + 0.5 * jnp.sum(jnp.mean(err * err, axis=-1, keepdims=True), axis=0, keepdims=True)
            dy = err * (1.0 / d)
            dyg = dy * gv
            dx_ref[rs, :] = r * (dyg - xn * jnp.mean(dyg * xn, axis=-1, keepdims=True))
            dg = dg + jnp.sum(dy * xn, axis=0, keepdims=True)

        @pl.when(i == 0)
        def _():
            loss_ref[...] = jnp.broadcast_to(loss, loss_ref.shape)
            dg_ref[...] = dg

        @pl.when(i > 0)
        def _():
            loss_ref[...] += jnp.broadcast_to(loss, loss_ref.shape)
            dg_ref[...] += dg

    row = pl.BlockSpec((tm, d), lambda i: (i, 0))
    return pl.pallas_call(
        body, name=name, grid=(t // tm,),
        in_specs=[pl.BlockSpec((tm, k), lambda i: (i, 0)), pl.BlockSpec((k, d), lambda i: (0, 0)), row,
                  pl.BlockSpec((1, d), lambda i: (0, 0)), row],
        out_specs=[row, pl.BlockSpec((8, 128), lambda i: (0, 0)), pl.BlockSpec((1, d), lambda i: (0, 0))],
        out_shape=[_sds((t, d), F32), _sds((8, 128), F32), _sds((1, d), F32)],
        compiler_params=_cp("arbitrary"),
    )(y, w, x, g, target)


def _position():
    x, y, c = lax.axis_index("x"), lax.axis_index("y"), lax.axis_index("c")
    return x, y, c, 4 * x + 2 * y + c


def _peer(x, y, c, k):
    px = 1 - x if k & 4 else x
    py = 1 - y if k & 2 else y
    pc = 1 - c if k & 1 else c
    return (px, py, pc), 4 * px + 2 * py + pc


def _block(ref, axis, size, idx):
    index = [slice(None)] * len(ref.shape)
    index[axis] = pl.ds(idx * size, size)
    return ref.at[tuple(index)]


def _other_chips(x, y):
    return [(1 - x, y), (x, 1 - y), (1 - x, 1 - y)]


def _gather_copy(ins, lands, axes, send_sems, recv_sems, k, p, block, to, from_shard=False):
    dst = _block(lands[p], axes[p], lands[p].shape[axes[p]] // N_DEV, block)
    sem = k * len(lands) + p
    return pltpu.make_async_remote_copy(
        src_ref=ins[p] if from_shard else dst, dst_ref=dst, send_sem=send_sems.at[sem], recv_sem=recv_sems.at[sem],
        device_id=to, device_id_type=MESH_ID)


def _own_block_copy(ins, lands, axes, sems, p, me):
    n = len(lands)
    dst = _block(lands[p], axes[p], lands[p].shape[axes[p]] // N_DEV, me)
    return pltpu.make_async_copy(ins[p], dst, sems.at[4 * n + p])


def all_gather_two_level(shards, axes, name):
    n = len(shards)
    fulls = []
    for s, ax in zip(shards, axes):
        shape = list(s.shape)
        shape[ax] *= N_DEV
        fulls.append(_sds(tuple(shape), s.dtype))

    def body(*refs):
        ins, outs = refs[:n], refs[n:2 * n]
        send_sems, recv_sems, local_sems = refs[2 * n:]
        x, y, c, me = _position()
        sibling, here = (x, y, 1 - c), (x, y, c)
        chips = _other_chips(x, y)
        copy = functools.partial(_gather_copy, ins, outs, axes, send_sems, recv_sems)
        local = [pltpu.make_async_copy(ins[p], _block(outs[p], axes[p], ins[p].shape[axes[p]], me), local_sems.at[p])
                 for p in range(n)]
        sent = []
        for p in range(n):
            sent.append(copy(0, p, me, sibling, from_shard=True))
            sent += [copy(1 + j, p, me, (*chip, c), from_shard=True) for j, chip in enumerate(chips)]
        for cp in local + sent:
            cp.start()
        for j, (px, py) in enumerate(chips):
            for p in range(n):
                arrived = 4 * px + 2 * py + c
                copy(1 + j, p, arrived, here).wait_recv()
                sent.append(copy(4 + j, p, arrived, sibling))
                sent[-1].start()
        for p in range(n):
            copy(0, p, 4 * x + 2 * y + 1 - c, here).wait_recv()
            for j, (px, py) in enumerate(chips):
                copy(4 + j, p, 4 * px + 2 * py + 1 - c, here).wait_recv()
        for cp in sent:
            cp.wait_send()
        for cp in local:
            cp.wait()

    any_spec = pl.BlockSpec(memory_space=pl.ANY)
    return pl.pallas_call(
        body, name=name,
        in_specs=[any_spec] * n, out_specs=[any_spec] * n, out_shape=fulls,
        scratch_shapes=[pltpu.SemaphoreType.DMA((7 * n,)), pltpu.SemaphoreType.DMA((7 * n,)), pltpu.SemaphoreType.DMA((n,))],
    )(*shards)


def gather_two_level_start(shards, axes, after, name):
    n = len(shards)
    lands = []
    for s, ax in zip(shards, axes):
        shape = list(s.shape)
        shape[ax] *= N_DEV
        lands.append(lax.empty(tuple(shape), s.dtype))

    def body(*refs):
        ins, land = refs[:n], refs[n:2 * n]
        send_sems, recv_sems, token = refs[2 * n + 1], refs[2 * n + 2], refs[-1]
        x, y, c, me = _position()
        copy = functools.partial(_gather_copy, ins, land, axes, send_sems, recv_sems)
        for p in range(n):
            copy(0, p, me, (x, y, 1 - c), from_shard=True).start()
            for j, chip in enumerate(_other_chips(x, y)):
                copy(1 + j, p, me, (*chip, c), from_shard=True).start()
            _own_block_copy(ins, land, axes, send_sems, p, me).start()
        token[...] = jnp.zeros_like(token)

    outs = pl.pallas_call(
        body, name=name,
        out_shape=(pltpu.SemaphoreType.DMA((5 * n,)), pltpu.SemaphoreType.DMA((4 * n,)),
                   *[pltpu.HBM(s.shape, s.dtype) for s in shards], *[pltpu.HBM(l.shape, l.dtype) for l in lands],
                   _sds((8, 128), F32)),
        in_specs=[_HBM] * (2 * n) + [pl.BlockSpec(memory_space=pl.ANY)],
        out_specs=(_SEM, _SEM, *[_HBM] * (2 * n), pl.BlockSpec(memory_space=pltpu.VMEM)),
        input_output_aliases={i: 2 + i for i in range(2 * n)},
        compiler_params=pltpu.CompilerParams(has_side_effects=_EFFECT),
    )(*[_in_hbm(s) for s in shards], *[_in_hbm(l) for l in lands], after)
    return outs[0], outs[1], list(outs[2:2 + n]), list(outs[2 + n:2 + 2 * n]), outs[-1]


def gather_two_level_forward(started, axes, after, name):
    send_a, recv_a, shards, lands, _ = started
    n = len(shards)

    def body(*refs):
        land = refs[:n]
        send_first, recv_first = refs[n], refs[n + 1]
        send_fwd, recv_fwd, token = refs[n + 3], refs[n + 4], refs[-1]
        x, y, c, _ = _position()
        for j, (px, py) in enumerate(_other_chips(x, y)):
            for p in range(n):
                arrived = 4 * px + 2 * py + c
                _gather_copy(None, land, axes, send_first, recv_first, 1 + j, p, arrived, (x, y, c)).wait_recv()
                _gather_copy(None, land, axes, send_fwd, recv_fwd, j, p, arrived, (x, y, 1 - c)).start()
        token[...] = jnp.zeros_like(token)

    outs = pl.pallas_call(
        body, name=name,
        out_shape=(pltpu.SemaphoreType.DMA((3 * n,)), pltpu.SemaphoreType.DMA((3 * n,)),
                   *[pltpu.HBM(l.shape, l.dtype) for l in lands], _sds((8, 128), F32)),
        in_specs=[_HBM] * n + [_SEM, _SEM, pl.BlockSpec(memory_space=pl.ANY)],
        out_specs=(_SEM, _SEM, *[_HBM] * n, pl.BlockSpec(memory_space=pltpu.VMEM)),
        input_output_aliases={i: 2 + i for i in range(n)},
        compiler_params=pltpu.CompilerParams(has_side_effects=_EFFECT),
    )(*lands, send_a, recv_a, after)
    return send_a, recv_a, outs[0], outs[1], shards, list(outs[2:2 + n]), outs[-1]


def gather_two_level_wait(forwarded, axes, after, name):
    send_a, recv_a, send_f, recv_f, shards, lands, _ = forwarded
    n = len(shards)

    def body(*refs):
        ins, land = refs[:n], refs[n:2 * n]
        sa, ra, sf, rf = refs[2 * n:2 * n + 4]
        x, y, c, me = _position()
        here = (x, y, c)
        chips = _other_chips(x, y)
        for p in range(n):
            for k in range(4):
                _gather_copy(ins, land, axes, sa, ra, k, p, me, here, from_shard=True).wait_send()
            _own_block_copy(ins, land, axes, sa, p, me).wait()
            _gather_copy(ins, land, axes, sa, ra, 0, p, 4 * x + 2 * y + 1 - c, here).wait_recv()
            for j, (px, py) in enumerate(chips):
                _gather_copy(ins, land, axes, sf, rf, j, p, 4 * px + 2 * py + c, here).wait_send()
                _gather_copy(ins, land, axes, sf, rf, j, p, 4 * px + 2 * py + 1 - c, here).wait_recv()

    outs = pl.pallas_call(
        body, name=name,
        out_shape=(*[pltpu.HBM(s.shape, s.dtype) for s in shards], *[pltpu.HBM(l.shape, l.dtype) for l in lands]),
        in_specs=[_HBM] * (2 * n) + [_SEM] * 4 + [pl.BlockSpec(memory_space=pl.ANY)],
        out_specs=tuple([_HBM] * (2 * n)),
        input_output_aliases={i: i for i in range(2 * n)},
        compiler_params=pltpu.CompilerParams(has_side_effects=_EFFECT),
    )(*shards, *lands, send_a, recv_a, send_f, recv_f, after)
    return list(outs[n:])


_HBM = pl.BlockSpec(memory_space=pltpu.HBM)
_SEM = pl.BlockSpec(memory_space=pltpu.SEMAPHORE)
_EFFECT = pltpu.SideEffectType.DATAFLOW_SIDE_EFFECTING


def _in_hbm(a):
    return pltpu.with_memory_space_constraint(a, pltpu.HBM)


def _landing(mode, src, axis):
    me = 4 * lax.axis_index("x") + 2 * lax.axis_index("y") + lax.axis_index("c")
    if mode == "gather":
        shape = list(src.shape)
        size = shape[axis]
        shape[axis] *= N_DEV
        return lax.dynamic_update_slice_in_dim(lax.empty(tuple(shape), src.dtype), src, me * size, axis)
    size = src.shape[axis] // N_DEV
    own = lax.dynamic_slice_in_dim(src, me * size, size, axis)
    return lax.dynamic_update_slice_in_dim(lax.empty((N_DEV, *own.shape), src.dtype), own[None], me, 0)


def _seven(mode, land_ref, axis):
    if mode == "gather":
        return _block(land_ref, axis, (N_DEV - 1) * (land_ref.shape[axis] // N_DEV), 0)
    return land_ref.at[pl.ds(0, N_DEV - 1)]


def exchange_start(mode, srcs, axes, after, name):
    n = len(srcs)
    lands = [_landing(mode, s, ax) for s, ax in zip(srcs, axes)]

    def body(*refs):
        src, land = refs[:n], refs[n:2 * n]
        send_sems, recv_sems = refs[2 * n + 1], refs[2 * n + 2]
        token = refs[-1]
        x, y, c, me = _position()
        for k in range(1, N_DEV):
            peer, pj = _peer(x, y, c, k)
            for p in range(n):
                if mode == "gather":
                    s = src[p]
                    dst = _block(land[p], axes[p], src[p].shape[axes[p]], me)
                else:
                    s = _block(src[p], axes[p], src[p].shape[axes[p]] // N_DEV, pj)
                    dst = land[p].at[me]
                pltpu.make_async_remote_copy(
                    src_ref=s, dst_ref=dst, send_sem=send_sems.at[p], recv_sem=recv_sems.at[p],
                    device_id=peer, device_id_type=MESH_ID).start()
        token[...] = jnp.zeros_like(token)

    outs = pl.pallas_call(
        body, name=name,
        out_shape=(pltpu.SemaphoreType.DMA((n,)), pltpu.SemaphoreType.DMA((n,)),
                   *[pltpu.HBM(s.shape, s.dtype) for s in srcs], *[pltpu.HBM(l.shape, l.dtype) for l in lands],
                   _sds((8, 128), F32)),
        in_specs=[_HBM] * (2 * n) + [pl.BlockSpec(memory_space=pl.ANY)],
        out_specs=(_SEM, _SEM, *[_HBM] * (2 * n), pl.BlockSpec(memory_space=pltpu.VMEM)),
        input_output_aliases={i: 2 + i for i in range(2 * n)},
        compiler_params=pltpu.CompilerParams(has_side_effects=_EFFECT),
    )(*[_in_hbm(s) for s in srcs], *[_in_hbm(l) for l in lands], after)
    return outs[0], outs[1], list(outs[2:2 + n]), list(outs[2 + n:2 + 2 * n]), outs[-1]


def exchange_wait(mode, started, axes, after, name):
    send_sems, recv_sems, srcs, lands, _ = started
    n = len(srcs)

    def body(*refs):
        land = refs[n:2 * n]
        send_ref, recv_ref = refs[2 * n], refs[2 * n + 1]
        x, y, c, _ = _position()
        for p in range(n):
            seven = _seven(mode, land[p], axes[p])
            cp = pltpu.make_async_remote_copy(
                src_ref=seven, dst_ref=seven, send_sem=send_ref.at[p], recv_sem=recv_ref.at[p],
                device_id=(x, y, c), device_id_type=MESH_ID)
            cp.wait_send()
            cp.wait_recv()

    outs = pl.pallas_call(
        body, name=name,
        out_shape=(*[pltpu.HBM(s.shape, s.dtype) for s in srcs], *[pltpu.HBM(l.shape, l.dtype) for l in lands]),
        in_specs=[_HBM] * (2 * n) + [_SEM, _SEM, pl.BlockSpec(memory_space=pl.ANY)],
        out_specs=tuple([_HBM] * (2 * n)),
        input_output_aliases={i: i for i in range(2 * n)},
        compiler_params=pltpu.CompilerParams(has_side_effects=_EFFECT),
    )(*srcs, *lands, send_sems, recv_sems, after)
    return list(outs[n:])


def _adam_math(g, w, m, v):
    m2 = ADAM_B1 * m + (1.0 - ADAM_B1) * g
    v2 = ADAM_B2 * v + (1.0 - ADAM_B2) * (g * g)
    m_hat = m2 / (1.0 - ADAM_B1 ** ADAM_STEP)
    v_hat = v2 / (1.0 - ADAM_B2 ** ADAM_STEP)
    delta = -ADAM_LR * (m_hat / (jnp.sqrt(v_hat) + ADAM_EPS) + ADAM_WD * w)
    return delta, m2, v2


def adamw(g, w, m, v, name, slots, tr=128):
    r, c = w.shape
    tr = min(tr, r)

    def body(g_ref, w_ref, m_ref, v_ref, go_ref, d_ref, mo_ref, vo_ref):
        if slots:
            gs = g_ref[0].astype(F32)
            for s in range(1, N_DEV):
                gs = gs + g_ref[s].astype(F32)
        else:
            gs = g_ref[...]
        go_ref[...] = gs
        d_ref[...], mo_ref[...], vo_ref[...] = _adam_math(gs, w_ref[...], m_ref[...], v_ref[...])

    row = pl.BlockSpec((tr, c), lambda i: (i, 0))
    gspec = pl.BlockSpec((N_DEV, tr, c), lambda i: (0, i, 0)) if slots else row
    return pl.pallas_call(
        body, name=name, grid=(r // tr,),
        in_specs=[gspec, row, row, row], out_specs=[row] * 4, out_shape=[_sds((r, c), F32)] * 4,
        compiler_params=_cp("parallel"),
    )(g, w, m, v)


def sum_slots(g, name):
    _, r, c = g.shape

    def body(g_ref, o_ref):
        gs = g_ref[0]
        for s in range(1, N_DEV):
            gs = gs + g_ref[s]
        o_ref[...] = gs

    return pl.pallas_call(
        body, name=name, grid=(1,),
        in_specs=[pl.BlockSpec((N_DEV, r, c), lambda i: (0, 0, 0))],
        out_specs=pl.BlockSpec((r, c), lambda i: (0, 0)), out_shape=_sds((r, c), F32),
        compiler_params=_cp("arbitrary"),
    )(g)


def _rows128(a, pad_to=8):
    a = a.reshape(-1, 128)
    pad = (-a.shape[0]) % pad_to
    return jnp.pad(a, ((0, pad), (0, 0))) if pad else a


def kernel(x, even_norm, even_w_in, even_pool_w, even_pool_scale, even_ws, even_bs, even_w_out, odd_norm, odd_w_in, odd_conv_w, odd_w_out, final_norm, loss_target, m_even_norm, m_even_w_in, m_even_pool_w, m_even_pool_scale, m_even_ws, m_even_bs, m_even_w_out, m_odd_norm, m_odd_w_in, m_odd_conv_w, m_odd_w_out, m_final_norm, v_even_norm, v_even_w_in, v_even_pool_w, v_even_pool_scale, v_even_ws, v_even_bs, v_even_w_out, v_odd_norm, v_odd_w_in, v_odd_conv_w, v_odd_w_out, v_final_norm):
    x0 = x[0]
    target = loss_target[0]
    me = 4 * lax.axis_index("x") + 2 * lax.axis_index("y") + lax.axis_index("c")

    (we_in,) = all_gather_two_level([even_w_in[0].astype(BF16)], [1], "gather_even_in")
    odd_small = jnp.pad(odd_norm, ((0, 7), (0, 0))) + jnp.pad(odd_conv_w[0], ((1, 4), (0, 0)))
    rest_axes, odd_axes = [1, 0, 1], [1, 0]
    rest_flight = exchange_start("gather", [even_pool_w[0].astype(BF16), even_w_out[0].astype(BF16), odd_small],
                                 rest_axes, we_in, "gather_even_rest_start")
    odd_flight = gather_two_level_start([odd_w_in[0].astype(BF16), odd_w_out[0].astype(BF16)],
                                        odd_axes, rest_flight[4], "gather_odd_start")
    scale = even_pool_scale
    ws = even_ws[0]
    bs_col = even_bs[0][:, :, None]
    g_fin = final_norm[None, :]

    z_e, h_e_t, we_in_t = norm_matmul(x0, even_norm + odd_flight[4][0:1, 0:1], we_in, "even_in", tn=1280)
    wp, we_out, odd_small = exchange_wait("gather", rest_flight, rest_axes, z_e, "gather_even_rest_wait")
    g_odd, conv_w = odd_small[0:1], odd_small[1:4]
    y_e, pooled = even_mix(z_e, wp, scale, ws, bs_col, "even_mix")
    x1, h_o, h_o_t = matmul_residual(y_e, we_out, x0, "even_out", next_gain=g_odd)
    odd_flight = gather_two_level_forward(odd_flight, odd_axes, x1, "gather_odd_forward")
    h_layouts = [h_o] + [permute_mats(h_o, [], gi, False, f"dilate_h_{gi}")[0] for gi in (1, 2)]
    wo_in, wo_out = gather_two_level_wait(odd_flight, odd_axes, h_layouts[2], "gather_odd_wait")
    z_o, wo_in_t = matmul_layouts(h_layouts, wo_in, "odd_in")
    attn = [attn_fwd(z_o, gi, f"attn_fwd_{gi}") for gi in range(3)]
    attn[2] = permute_mats(attn[2][0], [attn[2][1]], 2, True, "undilate_attn_2")
    y_o, yc, lse_tot, conv = odd_mix([a[0] for a in attn], [a[1] for a in attn], z_o, conv_w, "odd_mix")
    dx2, loss_blk, dg_fin = out_proj_final_loss(y_o, wo_out, x1, g_fin, target, "odd_out_loss")

    dy_o, dwo_out = out_proj_bwd(dx2, wo_out.T, y_o, "odd_out_bwd")
    dz5_o, dyc, dd, dconv_w = odd_mix_bwd(dy_o, yc, conv, z_o, conv_w, "odd_mix_bwd")
    dqkv = [attn_bwd(z_o, dyc, lse_tot, dd, 0, "attn_bwd_0")]
    for gi in (1, 2):
        dyc_g, lse_g, dd_g = permute_mats(dyc, [lse_tot, dd], gi, False, f"dilate_dy_{gi}")
        dqkv.append(attn_bwd(z_o, dyc_g, lse_g, dd_g, gi, f"attn_bwd_{gi}"))
    dqkv[2] = permute_slabs(dqkv[2], 2, True, "undilate_dqkv_2")
    src_o = [(dqkv[0], ("slabs", 0)), (dqkv[1], ("slabs", 1)), (dqkv[2], ("slabs", 2)), (dz5_o, ("cols", 9, 5))]
    dwo_in = in_proj_dw(h_o_t, src_o, ODD_COLBLOCKS, "odd_in_dw")
    odd_grads = exchange_start("scatter", [dwo_in, dwo_out], [1, 0], dconv_w, "scatter_odd_start")
    dx1, dg_odd = in_proj_dx(src_o, ODD_COLBLOCKS, wo_in_t, x1, g_odd + odd_grads[4][0:1, 0:1], dx2, "odd_in_dx")

    dy_e, dwe_out = out_proj_bwd(dx1, we_out.T, y_e, "even_out_bwd")
    dz_e, dwp, dscale, dws, dbs_col = even_mix_bwd(dy_e, z_e, pooled, wp, scale, ws, bs_col, "even_mix_bwd")
    src_e = [(dz_e, ("cols", 0, 5))]
    even_grads_a = exchange_start("scatter", [dwp.astype(BF16), dwe_out], [1, 0], dscale, "scatter_even_rest_start")
    dwe_in = in_proj_dw(h_e_t, src_e, 5, "even_in_dw")
    even_grads_b = exchange_start("scatter", [dwe_in], [1], even_grads_a[4], "scatter_even_in_start")
    dx0, dg_even = in_proj_dx(src_e, 5, we_in_t, x0, even_norm + even_grads_b[4][0:1, 0:1], dx1, "even_in_dx")
    small = jnp.concatenate([
        _rows128(dg_even), _rows128(dscale), _rows128(dws), _rows128(dbs_col), _rows128(dg_fin),
        _rows128(dg_odd), _rows128(dconv_w), loss_blk], axis=0)
    n_small = small.shape[0]
    small_flight = exchange_start("gather", [small], [0], dg_fin, "gather_small_start")
    p_wo_in, p_wo_out = exchange_wait("scatter", odd_grads, [1, 0], small_flight[4], "scatter_odd_wait")
    p_wp, p_we_out = exchange_wait("scatter", even_grads_a, [1, 0], small_flight[4], "scatter_even_rest_wait")
    (p_we_in,) = exchange_wait("scatter", even_grads_b, [1], small_flight[4], "scatter_even_in_wait")

    two_d = lambda a, r, c: a.reshape(r, c)
    sharded = [
        ("even_w_in", p_we_in, even_w_in, m_even_w_in, v_even_w_in, (1024, 640)),
        ("even_pool_w", p_wp.reshape(N_DEV, 128, 256), even_pool_w, m_even_pool_w, v_even_pool_w, (128, 256)),
        ("even_w_out", p_we_out, even_w_out, m_even_w_out, v_even_w_out, (256, 1024)),
        ("odd_w_in", p_wo_in, odd_w_in, m_odd_w_in, v_odd_w_in, (1024, 1792)),
        ("odd_w_out", p_wo_out, odd_w_out, m_odd_w_out, v_odd_w_out, (256, 1024)),
    ]
    res = {}
    for name, g, w, m, v, (r, c) in sharded:
        outs = adamw(g, two_d(w, r, c), two_d(m, r, c), two_d(v, r, c), "adamw_" + name, slots=True)
        res[name] = [o.reshape(w.shape) for o in outs]

    (small_all,) = exchange_wait("gather", small_flight, [0], res["odd_w_out"][0], "gather_small_wait")
    small_sum = sum_slots(small_all.reshape(N_DEV, n_small, 128), "sum_small_grads")
    g_even_norm = small_sum[0:8].reshape(1, 1024)
    g_scale = small_sum[8:16].reshape(1, 1024)
    g_ws = small_sum[16:528]
    g_bs = small_sum[528:532]
    g_final = small_sum[536:544].reshape(1, 1024)
    g_odd_norm = lax.dynamic_slice_in_dim(small_sum[544:552], me, 1, axis=0)
    g_conv = lax.dynamic_index_in_dim(small_sum[552:576].reshape(3, 8, 128), me, axis=1, keepdims=False)
    plain = [
        ("even_norm", g_even_norm, even_norm, m_even_norm, v_even_norm, (1, 1024)),
        ("even_pool_scale", g_scale, even_pool_scale, m_even_pool_scale, v_even_pool_scale, (1, 1024)),
        ("even_ws", g_ws, even_ws, m_even_ws, v_even_ws, (512, 128)),
        ("even_bs", g_bs, even_bs, m_even_bs, v_even_bs, (4, 128)),
        ("odd_norm", g_odd_norm, odd_norm, m_odd_norm, v_odd_norm, (1, 128)),
        ("odd_conv_w", g_conv, odd_conv_w, m_odd_conv_w, v_odd_conv_w, (3, 128)),
        ("final_norm", g_final, final_norm, m_final_norm, v_final_norm, (1, 1024)),
    ]
    for name, g, w, m, v, (r, c) in plain:
        outs = adamw(two_d(g, r, c), two_d(w, r, c), two_d(m, r, c), two_d(v, r, c), "adamw_" + name, slots=False)
        res[name] = [o.reshape(w.shape) for o in outs]

    loss = small_sum[576, 0]
    order = ["even_norm", "even_w_in", "even_pool_w", "even_pool_scale", "even_ws", "even_bs", "even_w_out",
             "odd_norm", "odd_w_in", "odd_conv_w", "odd_w_out", "final_norm"]
    return (loss, dx0[None], *[res[n][0] for n in order], *[res[n][1] for n in order],
            *[res[n][2] for n in order], *[res[n][3] for n in order])
```

```python
import functools

import jax
import jax.numpy as jnp
from jax import lax
from jax.experimental import pallas as pl
from jax.experimental.pallas import tpu as pltpu

F32 = jnp.float32
BF16 = jnp.bfloat16
MESH_ID = pl.DeviceIdType.MESH

EPS = 1e-6
NEG = -1e30
N_DEV = 8
POOL_SIZES = (2, 4, 8, 16)
DILATIONS = (1, 4, 16)
N_HEADS = 24
HEADS_PER_GROUP = 8
HEAD_DIM = 128
ATTN_BLOCK = 128
CHUNK = 128
CB = 1024
HALO = 16
ODD_COLBLOCKS = 14
ADAM_LR = 0.001
ADAM_B1 = 0.9
ADAM_B2 = 0.999
ADAM_EPS = 1e-08
ADAM_WD = 0.01
ADAM_STEP = 10
VMEM_LIMIT = 52 * 1024 * 1024


def _cp(*sem):
    return pltpu.CompilerParams(dimension_semantics=sem, vmem_limit_bytes=VMEM_LIMIT)


def _dot(a, b):
    return jnp.dot(a, b, preferred_element_type=F32)


def _dot_nt(a, b):
    return lax.dot_general(a, b, (((1,), (1,)), ((), ())), preferred_element_type=F32)


def _dot_tn(a, b):
    return lax.dot_general(a, b, (((0,), (0,)), ((), ())), preferred_element_type=F32)


def _sigmoid(x):
    return 0.5 * jnp.tanh(0.5 * x) + 0.5


def _silu_and_grad(x):
    s = _sigmoid(x)
    return x * s, s * (1.0 + x * (1.0 - s))


def _sds(shape, dtype):
    return jax.ShapeDtypeStruct(shape, dtype)


def norm_matmul(x, g, w, name, tn, tm=1024, rows=256):
    t, d = x.shape
    n = w.shape[1]
    n_col = n // tn

    def body(x_ref, g_ref, w_ref, z_ref, ht_ref, wt_ref, h_ref):
        @pl.when(pl.program_id(1) == 0)
        def _():
            for c in range(tm // rows):
                rs = slice(c * rows, (c + 1) * rows)
                xv = x_ref[rs, :]
                r = lax.rsqrt(jnp.mean(xv * xv, axis=-1, keepdims=True) + EPS)
                h_ref[rs, :] = ((xv * r) * g_ref[...]).astype(BF16)
            ht_ref[...] = h_ref[...].T

        z_ref[...] = _dot(h_ref[...], w_ref[...]).astype(BF16)

        @pl.when(pl.program_id(0) == 0)
        def _():
            wt_ref[...] = w_ref[...].T

    return pl.pallas_call(
        body, name=name, grid=(t // tm, n_col),
        in_specs=[pl.BlockSpec((tm, d), lambda i, j: (i, 0)),
                  pl.BlockSpec((1, d), lambda i, j: (0, 0)),
                  pl.BlockSpec((d, tn), lambda i, j: (0, j))],
        out_specs=[pl.BlockSpec((tm, tn), lambda i, j: (i, j)),
                   pl.BlockSpec((d, tm), lambda i, j: (0, i)),
                   pl.BlockSpec((tn, d), lambda i, j: (jnp.where(i == 0, j, n_col), 0))],
        out_shape=[_sds((t, n), BF16), _sds((d, t), BF16), _sds((n + tn, d), BF16)],
        scratch_shapes=[pltpu.VMEM((tm, d), BF16)],
        compiler_params=_cp("arbitrary", "arbitrary"),
    )(x, g, w)


def matmul_residual(y, w, x, name, next_gain=None, tm=1024, rows=256):
    t, k = y.shape
    d = w.shape[1]

    def body(y_ref, w_ref, x_ref, *rest):
        o_ref = rest[-1] if next_gain is None else rest[-3]
        o_ref[...] = x_ref[...] + _dot(y_ref[...], w_ref[...])
        if next_gain is not None:
            g_ref, h_ref, ht_ref = rest[0], rest[-2], rest[-1]
            for c in range(tm // rows):
                rs = slice(c * rows, (c + 1) * rows)
                xv = o_ref[rs, :]
                r = lax.rsqrt(jnp.mean(xv * xv, axis=-1, keepdims=True) + EPS)
                h_ref[rs, :] = ((xv * r) * g_ref[...]).astype(BF16)
            ht_ref[...] = h_ref[...].T

    row = pl.BlockSpec((tm, d), lambda i: (i, 0))
    in_specs = [pl.BlockSpec((tm, k), lambda i: (i, 0)), pl.BlockSpec((k, d), lambda i: (0, 0)), row]
    if next_gain is None:
        return pl.pallas_call(
            body, name=name, grid=(t // tm,), in_specs=in_specs, out_specs=row, out_shape=_sds((t, d), F32),
            compiler_params=_cp("parallel"),
        )(y, w, x)
    return pl.pallas_call(
        body, name=name, grid=(t // tm,),
        in_specs=in_specs + [pl.BlockSpec((1, d), lambda i: (0, 0))],
        out_specs=[row, row, pl.BlockSpec((d, tm), lambda i: (0, i))],
        out_shape=[_sds((t, d), F32), _sds((t, d), BF16), _sds((d, t), BF16)],
        compiler_params=_cp("parallel"),
    )(y, w, x, next_gain)


def matmul_layouts(hs, w, name, tm=2048):
    t, d = hs[0].shape
    n_blocks = w.shape[1] // CB

    def layout_of(j):
        return jnp.where(j < 9, j // 3, 0)

    def column_of(j):
        return jnp.where(j < 9, (j % 3) * 3 + j // 3, j)

    def body(h0, h1, h2, w_ref, z_ref, wt_ref):
        layout = layout_of(pl.program_id(1))
        for v, h_ref in enumerate((h0, h1, h2)):
            @pl.when(layout == v)
            def _(h_ref=h_ref):
                z_ref[...] = _dot(h_ref[...], w_ref[...]).astype(BF16)
        @pl.when(pl.program_id(0) == 0)
        def _():
            wt_ref[...] = w_ref[...].T

    row = pl.BlockSpec((tm, d), lambda i, j: (i, 0))
    wt_block = lambda i, j: (jnp.where(i == 0, column_of(j), n_blocks), 0)
    return pl.pallas_call(
        body, name=name, grid=(t // tm, n_blocks),
        in_specs=[row, row, row, pl.BlockSpec((d, CB), lambda i, j: (0, column_of(j)))],
        out_specs=[pl.BlockSpec((tm, CB), lambda i, j: (i, column_of(j))), pl.BlockSpec((CB, d), wt_block)],
        out_shape=[_sds((t, n_blocks * CB), BF16), _sds(((n_blocks + 1) * CB, d), BF16)],
        compiler_params=_cp("arbitrary", "arbitrary"),
    )(*hs, w)


def out_proj_bwd(dx, wt, y, name, tm=512):
    t, d = dx.shape
    k = wt.shape[1]
    steps = t // tm
    half = d // 2

    def body(dx_ref, wt_ref, y_ref, dy_ref, dw_ref, acc):
        i = pl.program_id(0)
        dxb = dx_ref[...].astype(BF16)
        dy_ref[...] = _dot(dxb, wt_ref[...]).astype(BF16)

        @pl.when(i == 0)
        def _():
            acc[...] = jnp.zeros_like(acc)

        for c in range(2):
            acc[:, c * half:(c + 1) * half] += _dot_tn(y_ref[...], dxb[:, c * half:(c + 1) * half])

        @pl.when(i == steps - 1)
        def _():
            dw_ref[...] = acc[...].astype(BF16)

    return pl.pallas_call(
        body, name=name, grid=(steps,),
        in_specs=[pl.BlockSpec((tm, d), lambda i: (i, 0)),
                  pl.BlockSpec((d, k), lambda i: (0, 0)),
                  pl.BlockSpec((tm, k), lambda i: (i, 0))],
        out_specs=[pl.BlockSpec((tm, k), lambda i: (i, 0)),
                   pl.BlockSpec((k, d), lambda i: (0, 0))],
        out_shape=[_sds((t, k), BF16), _sds((k, d), BF16)],
        scratch_shapes=[pltpu.VMEM((k, d), F32)],
        compiler_params=_cp("arbitrary"),
    )(dx, wt, y)


def _source_block(kind, rows):
    if kind[0] == "cols":
        return (rows, CB)
    return (None, rows, CB)


def _source_active(kind, j):
    if kind[0] == "cols":
        _, first, n = kind
        return (j >= first) & (j < first + n), jnp.clip(j - first, 0, n - 1)
    _, group = kind
    return (j < 9) & (j % 3 == group), jnp.clip(j // 3, 0, 2)


def _source_index(kind, row_block, inner):
    if kind[0] == "cols":
        return (row_block, inner)
    return (inner, row_block, 0)


def in_proj_dw(ht, sources, n_blocks, name, tk=1024):
    d, t = ht.shape
    steps = t // tk
    ns = len(sources)

    def body(*refs):
        h_ref, src_refs, dw_ref, acc = refs[0], refs[1:1 + ns], refs[1 + ns], refs[2 + ns]
        j, k = pl.program_id(0), pl.program_id(1)

        @pl.when(k == 0)
        def _():
            acc[...] = jnp.zeros_like(acc)

        for s, (_, kind) in enumerate(sources):
            active, _ = _source_active(kind, j)

            @pl.when(active)
            def _(s=s):
                acc[...] += _dot(h_ref[...], src_refs[s][...])

        @pl.when(k == steps - 1)
        def _():
            dw_ref[...] = acc[...].astype(BF16)

    def src_spec(kind):
        def index(j, k):
            active, inner = _source_active(kind, j)
            return _source_index(kind, jnp.where(active, k, 0), inner)
        return pl.BlockSpec(_source_block(kind, tk), index)

    return pl.pallas_call(
        body, name=name, grid=(n_blocks, steps),
        in_specs=[pl.BlockSpec((d, tk), lambda j, k: (0, k))] + [src_spec(kind) for _, kind in sources],
        out_specs=pl.BlockSpec((d, CB), lambda j, k: (0, j)),
        out_shape=_sds((d, n_blocks * CB), BF16),
        scratch_shapes=[pltpu.VMEM((d, CB), F32)],
        compiler_params=_cp("parallel", "arbitrary"),
    )(ht, *[a for a, _ in sources])


def in_proj_dx(sources, n_blocks, wt, x, g, dres, name, tm=1024, rows=256):
    t, d = x.shape
    ns = len(sources)

    def body(*refs):
        src_refs = refs[:ns]
        w_ref, x_hbm, g_ref, dres_hbm, dx_ref, dg_ref, acc, x_ref, dres_ref, sems = refs[ns:]
        i, p = pl.program_id(0), pl.program_id(1)
        tile = pl.ds(pl.multiple_of(i * tm, tm), tm)
        fetch = [pltpu.make_async_copy(x_hbm.at[tile, :], x_ref, sems.at[0]),
                 pltpu.make_async_copy(dres_hbm.at[tile, :], dres_ref, sems.at[1])]

        @pl.when(p == 0)
        def _():
            acc[...] = jnp.zeros_like(acc)
            for cp in fetch:
                cp.start()

        for s, (_, kind) in enumerate(sources):
            active, _ = _source_active(kind, p)

            @pl.when(active)
            def _(s=s):
                acc[...] += _dot(src_refs[s][...], w_ref[...])

        @pl.when(p == n_blocks - 1)
        def _():
            for cp in fetch:
                cp.wait()
            part = jnp.zeros((1, d), F32)
            for c in range(tm // rows):
                rs = slice(c * rows, (c + 1) * rows)
                xv = x_ref[rs, :]
                r = lax.rsqrt(jnp.mean(xv * xv, axis=-1, keepdims=True) + EPS)
                xn = xv * r
                dh = acc[rs, :]
                dhg = dh * g_ref[...]
                dx_ref[rs, :] = dres_ref[rs, :] + r * (dhg - xn * jnp.mean(dhg * xn, axis=-1, keepdims=True))
                part = part + jnp.sum(dh * xn, axis=0, keepdims=True)

            @pl.when(i == 0)
            def _():
                dg_ref[...] = part

            @pl.when(i > 0)
            def _():
                dg_ref[...] += part

    def src_spec(kind):
        def index(i, p):
            if kind[0] == "cols":
                _, first, n = kind
                inner = jnp.clip(p - first, 0, n - 1)
            else:
                _, group = kind
                inner = jnp.clip((p - group + 2) // 3, 0, 2)
            return _source_index(kind, i, inner)
        return pl.BlockSpec(_source_block(kind, tm), index)

    any_spec = pl.BlockSpec(memory_space=pl.ANY)
    return pl.pallas_call(
        body, name=name, grid=(t // tm, n_blocks),
        in_specs=[src_spec(kind) for _, kind in sources] + [
            pl.BlockSpec((CB, d), lambda i, p: (p, 0)), any_spec, pl.BlockSpec((1, d), lambda i, p: (0, 0)), any_spec],
        out_specs=[pl.BlockSpec((tm, d), lambda i, p: (i, 0)),
                   pl.BlockSpec((1, d), lambda i, p: (0, 0))],
        out_shape=[_sds((t, d), F32), _sds((1, d), F32)],
        scratch_shapes=[pltpu.VMEM((tm, d), F32), pltpu.VMEM((tm, d), F32), pltpu.VMEM((tm, d), F32),
                        pltpu.SemaphoreType.DMA((2,))],
        compiler_params=_cp("arbitrary", "arbitrary"),
    )(*[a for a, _ in sources], wt, x, g, dres)


def _window_counts(first_row, rows, w):
    t = first_row + lax.broadcasted_iota(jnp.int32, (rows, 1), 0)
    return jnp.minimum(t + 1, w).astype(F32)


def _tril_bf16(ws_ref, g):
    r = lax.broadcasted_iota(jnp.int32, (CHUNK, CHUNK), 0)
    c = lax.broadcasted_iota(jnp.int32, (CHUNK, CHUNK), 1)
    return jnp.where(r >= c, ws_ref[g], 0.0).astype(BF16), r >= c


def even_mix(z, wp, scale, ws, bs_col, name, tm=512):
    t = z.shape[0]
    gd = CB // len(POOL_SIZES)

    def body(a_ref, ga_ref, u_ref, v_ref, gb_ref, wp_ref, sc_ref, ws_ref, bs_ref, y_ref, pooled_ref, aext):
        i = pl.program_id(0)

        @pl.when(i == 0)
        def _():
            aext[0:HALO, :] = jnp.zeros((HALO, CB), F32)

        @pl.when(i > 0)
        def _():
            aext[0:HALO, :] = aext[tm:tm + HALO, :]

        aext[HALO:HALO + tm, :] = a_ref[...].astype(F32)
        for g, w in enumerate(POOL_SIZES):
            cols = slice(g * gd, (g + 1) * gd)
            tok = aext[HALO:HALO + tm, cols]
            s = tok
            for k in range(1, w):
                s = s + aext[HALO - k:HALO - k + tm, cols]
            pooled = (s / _window_counts(i * tm, tm, w) - tok).astype(BF16)
            pooled_ref[:, cols] = pooled
            mixed = _dot(pooled, wp_ref[g])
            silu_a, _ = _silu_and_grad(ga_ref[:, cols].astype(F32))
            y_ref[:, cols] = (mixed * sc_ref[:, cols] * silu_a).astype(BF16)
        for g in range(4):
            cols = slice(g * gd, (g + 1) * gd)
            wg, _ = _tril_bf16(ws_ref, g)
            for c in range(tm // CHUNK):
                rows = slice(c * CHUNK, (c + 1) * CHUNK)
                m = _dot(wg, v_ref[rows, cols]) + bs_ref[g]
                silu_b, _ = _silu_and_grad(gb_ref[rows, cols].astype(F32))
                y_ref[rows, CB + g * gd:CB + (g + 1) * gd] = (
                    u_ref[rows, cols].astype(F32) * m * silu_b).astype(BF16)

    zspec = lambda cb: pl.BlockSpec((tm, CB), lambda i, cb=cb: (i, cb))
    full = lambda shape: pl.BlockSpec(shape, lambda i: (0,) * len(shape))
    return pl.pallas_call(
        body, name=name, grid=(t // tm,),
        in_specs=[zspec(0), zspec(1), zspec(2), zspec(3), zspec(4),
                  full(wp.shape), full(scale.shape), full(ws.shape), full(bs_col.shape)],
        out_specs=[pl.BlockSpec((tm, 2 * CB), lambda i: (i, 0)), pl.BlockSpec((tm, CB), lambda i: (i, 0))],
        out_shape=[_sds((t, 2 * CB), BF16), _sds((t, CB), BF16)],
        scratch_shapes=[pltpu.VMEM((HALO + tm, CB), F32)],
        compiler_params=_cp("arbitrary"),
    )(z, z, z, z, z, wp, scale, ws, bs_col)


def even_mix_bwd(dy, z, pooled, wp, scale, ws, bs_col, name, tm=512):
    t = z.shape[0]
    nt = t // tm
    gd = CB // len(POOL_SIZES)

    def body(dya_ref, dyb_ref, ga_ref, u_ref, v_ref, gb_ref, pooled_ref, wp_ref, sc_ref, ws_ref, bs_ref,
             dz_ref, dwp_ref, dsc_ref, dws_ref, dbs_ref, dpext):
        i = pl.program_id(0)
        tile = nt - 1 - i

        @pl.when(i == 0)
        def _():
            dpext[tm:tm + HALO, :] = jnp.zeros((HALO, CB), F32)
            dwp_ref[...] = jnp.zeros_like(dwp_ref)
            dsc_ref[...] = jnp.zeros_like(dsc_ref)
            dws_ref[...] = jnp.zeros_like(dws_ref)
            dbs_ref[...] = jnp.zeros_like(dbs_ref)

        @pl.when(i > 0)
        def _():
            dpext[tm:tm + HALO, :] = dpext[0:HALO, :]

        for g, w in enumerate(POOL_SIZES):
            cols = slice(g * gd, (g + 1) * gd)
            pooled_g = pooled_ref[:, cols]
            mixed = _dot(pooled_g, wp_ref[g])
            silu_a, dsilu_a = _silu_and_grad(ga_ref[:, cols].astype(F32))
            dya = dya_ref[:, cols].astype(F32)
            sc = sc_ref[:, cols]
            dmixed = (dya * sc * silu_a).astype(BF16)
            dsc_ref[:, cols] += jnp.sum(dya * mixed * silu_a, axis=0, keepdims=True)
            dz_ref[:, CB + g * gd:CB + (g + 1) * gd] = (dya * mixed * sc * dsilu_a).astype(BF16)
            dwp_ref[g] += _dot_tn(pooled_g, dmixed)
            dpooled = _dot_nt(dmixed, wp_ref[g])
            dpext[0:tm, cols] = dpooled / _window_counts(tile * tm, tm, w)
            s = dpext[0:tm, cols]
            for k in range(1, w):
                s = s + dpext[k:k + tm, cols]
            dz_ref[:, cols] = (s - dpooled).astype(BF16)
        for g in range(4):
            cols = slice(g * gd, (g + 1) * gd)
            wg, lower = _tril_bf16(ws_ref, g)
            dws_g = jnp.zeros((CHUNK, CHUNK), F32)
            dbs_g = jnp.zeros((CHUNK, 1), F32)
            for c in range(tm // CHUNK):
                rows = slice(c * CHUNK, (c + 1) * CHUNK)
                vb = v_ref[rows, cols]
                m = _dot(wg, vb) + bs_ref[g]
                gbv = gb_ref[rows, cols].astype(F32)
                silu_b, dsilu_b = _silu_and_grad(gbv)
                dyb = dyb_ref[rows, cols].astype(F32)
                uv = u_ref[rows, cols].astype(F32)
                dm = dyb * silu_b * uv
                dmb = dm.astype(BF16)
                dz_ref[rows, 2 * CB + g * gd:2 * CB + (g + 1) * gd] = (dyb * silu_b * m).astype(BF16)
                dz_ref[rows, 3 * CB + g * gd:3 * CB + (g + 1) * gd] = _dot_tn(wg, dmb).astype(BF16)
                dz_ref[rows, 4 * CB + g * gd:4 * CB + (g + 1) * gd] = (dyb * uv * m * dsilu_b).astype(BF16)
                dws_g = dws_g + _dot_nt(dmb, vb)
                dbs_g = dbs_g + jnp.sum(dm, axis=1, keepdims=True)
            dws_ref[g] += jnp.where(lower, dws_g, 0.0)
            dbs_ref[g] += dbs_g

    rev = lambda cb: pl.BlockSpec((tm, CB), lambda i, cb=cb: (nt - 1 - i, cb))
    full = lambda shape: pl.BlockSpec(shape, lambda i: (0,) * len(shape))
    return pl.pallas_call(
        body, name=name, grid=(nt,),
        in_specs=[rev(0), rev(1), rev(1), rev(2), rev(3), rev(4), rev(0),
                  full(wp.shape), full(scale.shape), full(ws.shape), full(bs_col.shape)],
        out_specs=[pl.BlockSpec((tm, 5 * CB), lambda i: (nt - 1 - i, 0)),
                   full(wp.shape), full(scale.shape), full(ws.shape), full(bs_col.shape)],
        out_shape=[_sds((t, 5 * CB), BF16), _sds(wp.shape, F32), _sds(scale.shape, F32),
                   _sds(ws.shape, F32), _sds(bs_col.shape, F32)],
        scratch_shapes=[pltpu.VMEM((tm + HALO, CB), F32)],
        compiler_params=_cp("arbitrary"),
    )(dy, dy, z, z, z, z, pooled, wp, scale, ws, bs_col)


def _slope(group, head):
    return float(2.0 ** (-8.0 * (group * HEADS_PER_GROUP + head + 1) / N_HEADS))


def _band(dilation):
    qi = lax.broadcasted_iota(jnp.int32, (ATTN_BLOCK, ATTN_BLOCK), 0)
    ki = lax.broadcasted_iota(jnp.int32, (ATTN_BLOCK, ATTN_BLOCK), 1)
    dist_prev = ((qi + ATTN_BLOCK - ki) * dilation).astype(F32)
    dist_cur = ((qi - ki) * dilation).astype(F32)
    return dist_prev, ki >= qi, dist_cur, ki <= qi


def _permute_rows(name, d, inverse, arrays, in_specs, out_shapes, out_specs, n_chunks, widths):
    rows = ATTN_BLOCK * d
    t = out_shapes[0].shape[-2]
    n = len(arrays)

    lanes = HEAD_DIM

    def body(*refs):
        for a, o, s, w in zip(refs[:n], refs[n:2 * n], refs[2 * n:], widths):
            for k in range(w // lanes):
                cols = slice(k * lanes, (k + 1) * lanes)
                if inverse:
                    for r in range(d):
                        s[k, pl.ds(r, ATTN_BLOCK, stride=d), :] = (
                            a[r * ATTN_BLOCK:(r + 1) * ATTN_BLOCK, cols].astype(F32))
                    o[:, cols] = s[k].astype(o.dtype)
                else:
                    s[k] = a[:, cols].astype(F32)
                    for r in range(d):
                        o[r * ATTN_BLOCK:(r + 1) * ATTN_BLOCK, cols] = (
                            s[k, pl.ds(r, ATTN_BLOCK, stride=d), :].astype(o.dtype))

    return pl.pallas_call(
        body, name=name, grid=(t // rows, n_chunks),
        in_specs=in_specs, out_specs=out_specs, out_shape=out_shapes,
        scratch_shapes=[pltpu.VMEM((w // lanes, rows, lanes), F32) for w in widths],
        compiler_params=_cp("parallel", "arbitrary"),
    )(*arrays)


def _chunk_width(d):
    return 4 * CB // d


def permute_mats(wide, narrow, group, inverse, name):
    d = DILATIONS[group]
    rows, cw = ATTN_BLOCK * d, _chunk_width(d)
    wide_spec = pl.BlockSpec((rows, cw), lambda i, c: (i, c))
    narrow_spec = pl.BlockSpec((rows, HEAD_DIM), lambda i, c: (i, 0))
    arrays = [wide] + list(narrow)
    specs = [wide_spec] + [narrow_spec] * len(narrow)
    return _permute_rows(name, d, inverse, arrays, specs, [_sds(a.shape, a.dtype) for a in arrays], specs,
                         CB // cw, [cw] + [HEAD_DIM] * len(narrow))


def permute_slabs(s, group, inverse, name):
    d = DILATIONS[group]
    rows, cw = ATTN_BLOCK * d, _chunk_width(d)
    per = CB // cw
    spec = pl.BlockSpec((None, rows, cw), lambda i, c: (c // per, i, c % per))
    (out,) = _permute_rows(name, d, inverse, [s], [spec], [_sds(s.shape, s.dtype)], [spec], 3 * per, [cw])
    return out


def _lane_pack(cols):
    rows = cols[0].shape[0]
    lane = lax.broadcasted_iota(jnp.int32, (rows, HEAD_DIM), 1)
    out = jnp.zeros((rows, HEAD_DIM), F32)
    for h, c in enumerate(cols):
        out = jnp.where(lane == h, c, out)
    return out


def _qkv_col(group, from_z):
    return (lambda which: which * 3 + group) if from_z else (lambda which: which)


SCORE_SCALE = HEAD_DIM ** -0.5
LOG2_E = 1.4426950408889634
EXP2_SCALE = SCORE_SCALE * LOG2_E


def _fill_bias(bias_ref, group, d):
    dist_p, ok_p, dist_c, ok_c = _band(d)
    for h in range(HEADS_PER_GROUP):
        k = -_slope(group, h) / SCORE_SCALE
        bias_ref[h, :, 0:ATTN_BLOCK] = jnp.where(ok_p, k * dist_p, NEG)
        bias_ref[h, :, ATTN_BLOCK:2 * ATTN_BLOCK] = jnp.where(ok_c, k * dist_c, NEG)


STEP_BLOCKS = 4


def _step_plan(d):
    return d < STEP_BLOCKS, max(d // STEP_BLOCKS, 1)


def _joined(prev_ref, cur_ref, j, cols, chained):
    rows = slice(j * ATTN_BLOCK, (j + 1) * ATTN_BLOCK)
    if not chained:
        before = prev_ref[rows, cols]
    elif j == 0:
        before = prev_ref[(STEP_BLOCKS - 1) * ATTN_BLOCK:STEP_BLOCKS * ATTN_BLOCK, cols]
    else:
        before = cur_ref[(j - 1) * ATTN_BLOCK:j * ATTN_BLOCK, cols]
    return jnp.concatenate([before, cur_ref[rows, cols]], axis=0)


def _raw_scores(q, kk, bias_ref, h, has_prev):
    is_cur = lax.broadcasted_iota(jnp.int32, (ATTN_BLOCK, 2 * ATTN_BLOCK), 1) >= ATTN_BLOCK
    return jnp.where(jnp.logical_or(is_cur, has_prev), _dot_nt(q, kk) + bias_ref[h], NEG)


def attn_fwd(qkv, group, name):
    t = qkv.shape[0]
    d = DILATIONS[group]
    sub = STEP_BLOCKS
    chained, back = _step_plan(d)
    token_order = d == sub
    col = _qkv_col(group, qkv.shape[1] != 3 * CB)
    heads = range(HEADS_PER_GROUP)

    def body(q_ref, kp_ref, kc_ref, vp_ref, vc_ref, o_ref, lse_ref, bias_ref, *stage):
        b = pl.program_id(0)

        @pl.when(b == 0)
        def _():
            _fill_bias(bias_ref, group, d)

        hs = [slice(h * HEAD_DIM, (h + 1) * HEAD_DIM) for h in heads]
        for j in range(sub):
            rows = slice(j * ATTN_BLOCK, (j + 1) * ATTN_BLOCK)
            has_prev = True if chained and j > 0 else b >= back
            s = [_raw_scores(q_ref[rows, hs[h]], _joined(kp_ref, kc_ref, j, hs[h], chained), bias_ref, h, has_prev)
                 for h in heads]
            m = [jnp.max(t_, axis=1, keepdims=True) for t_ in s]
            e = [jnp.exp2(EXP2_SCALE * (t_ - m[h])) for h, t_ in enumerate(s)]
            l = [jnp.sum(e_, axis=1, keepdims=True) for e_ in e]
            o = [_dot(e_.astype(BF16), _joined(vp_ref, vc_ref, j, hs[h], chained)) * (1.0 / l[h])
                 for h, e_ in enumerate(e)]
            lse_j = _lane_pack([SCORE_SCALE * m[h] + jnp.log(l[h]) for h in heads])
            if token_order:
                strided = pl.ds(j, ATTN_BLOCK, stride=d)
                for h in heads:
                    stage[0][h, strided, :] = o[h]
                stage[0][HEADS_PER_GROUP, strided, :] = lse_j
            else:
                for h in heads:
                    o_ref[rows, hs[h]] = o[h].astype(BF16)
                lse_ref[rows, :] = lse_j
        if token_order:
            for h in heads:
                o_ref[:, hs[h]] = stage[0][h].astype(BF16)
            lse_ref[...] = stage[0][HEADS_PER_GROUP]

    blk = (sub * ATTN_BLOCK, CB)
    cur = lambda which: pl.BlockSpec(blk, lambda b: (b, col(which)))
    prev = lambda which: pl.BlockSpec(blk, lambda b: (jnp.maximum(b - back, 0), col(which)))
    scratch = [pltpu.VMEM((HEADS_PER_GROUP, ATTN_BLOCK, 2 * ATTN_BLOCK), F32)]
    if token_order:
        scratch.append(pltpu.VMEM((HEADS_PER_GROUP + 1, blk[0], HEAD_DIM), F32))
    return pl.pallas_call(
        body, name=name, grid=(t // blk[0],),
        in_specs=[cur(0), prev(1), cur(1), prev(2), cur(2)],
        out_specs=[pl.BlockSpec(blk, lambda b: (b, 0)), pl.BlockSpec((blk[0], HEAD_DIM), lambda b: (b, 0))],
        out_shape=[_sds((t, CB), BF16), _sds((t, HEAD_DIM), F32)],
        scratch_shapes=scratch,
        compiler_params=_cp("arbitrary"),
    )(qkv, qkv, qkv, qkv, qkv)


def attn_bwd(z, dyc, lse, dd, group, name):
    t = z.shape[0]
    d = DILATIONS[group]
    sub = STEP_BLOCKS
    chained, back = _step_plan(d)
    nb = t // (ATTN_BLOCK * d) // (sub if chained else 1)
    col = _qkv_col(group, z.shape[1] != 3 * CB)
    heads = range(HEADS_PER_GROUP)
    steps = back * nb
    token_order = d == sub

    def body(q_ref, kp_ref, kc_ref, vp_ref, vc_ref, dy_ref, lse_ref, dd_ref, out_ref, carry, bias_ref, *stage):
        s_ = pl.program_id(0)
        n = s_ % nb
        hs = [slice(h * HEAD_DIM, (h + 1) * HEAD_DIM) for h in heads]
        rows = [slice(j * ATTN_BLOCK, (j + 1) * ATTN_BLOCK) for j in range(sub)]

        def put(w, j, h, value):
            if token_order:
                stage[0][w * HEADS_PER_GROUP + h, pl.ds(j, ATTN_BLOCK, stride=d), :] = value
            else:
                out_ref[w, rows[j], hs[h]] = value.astype(BF16)

        def put_done():
            if token_order:
                for w in range(3):
                    for h in heads:
                        out_ref[w, :, hs[h]] = stage[0][w * HEADS_PER_GROUP + h].astype(BF16)

        @pl.when(s_ == 0)
        def _():
            _fill_bias(bias_ref, group, d)
            carry[...] = jnp.zeros_like(carry)

        @pl.when(s_ == steps)
        def _():
            for w in range(3):
                for j in range(sub):
                    for h in heads:
                        put(w, j, h, carry[w, rows[j], hs[h]])
            put_done()

        @pl.when(s_ < steps)
        def _():
            old = [[[carry[w, rows[j], hs[h]] for h in heads] for j in range(sub)] for w in range(3)]
            if token_order:
                for h in heads:
                    stage[1][h] = dy_ref[:, hs[h]].astype(F32)
            grads = []
            first, second = slice(0, ATTN_BLOCK), slice(ATTN_BLOCK, 2 * ATTN_BLOCK)

            def emit(j, grads):
                for h in heads:
                    put(0, j, h, old[0][j][h])
                    carry[0, rows[j], hs[h]] = grads[j][h][0]
                    for w in (1, 2):
                        if not chained:
                            done = old[w][j][h] + grads[j][h][w][first]
                            kept = grads[j][h][w][second]
                        else:
                            done = old[w][j][h] + grads[0][h][w][first] if j == sub - 1 else old[w][j][h]
                            kept = grads[j][h][w][second]
                            if j < sub - 1:
                                kept = kept + grads[j + 1][h][w][first]
                        put(w, j, h, done)
                        carry[w, rows[j], hs[h]] = kept

            for j in range(sub):
                has_prev = True if chained and j > 0 else n > 0
                q = [q_ref[rows[j], hs[h]] for h in heads]
                kk = [_joined(kp_ref, kc_ref, j, hs[h], chained) for h in heads]
                if token_order:
                    strided = pl.ds(j, ATTN_BLOCK, stride=d)
                    dy = [stage[1][h, strided, :].astype(BF16) for h in heads]
                    lse_j, dd_j = lse_ref[strided, :], dd_ref[strided, :]
                else:
                    dy = [dy_ref[rows[j], hs[h]] for h in heads]
                    lse_j, dd_j = lse_ref[rows[j], :], dd_ref[rows[j], :]
                s = [_raw_scores(q[h], kk[h], bias_ref, h, has_prev) for h in heads]
                p = [jnp.exp2(EXP2_SCALE * t_ - lse_j[:, h:h + 1] * LOG2_E) for h, t_ in enumerate(s)]
                ds = [(p[h] * (_dot_nt(dy[h], _joined(vp_ref, vc_ref, j, hs[h], chained)) - dd_j[:, h:h + 1])
                       * SCORE_SCALE).astype(BF16) for h in heads]
                grads.append([(_dot(ds[h], kk[h]), _dot_tn(ds[h], q[h]), _dot_tn(p[h].astype(BF16), dy[h]))
                              for h in heads])
                if not chained:
                    emit(j, grads)
            if chained:
                for j in range(sub):
                    emit(j, grads)
            put_done()

    blk = (sub * ATTN_BLOCK, CB)
    scratch = [pltpu.VMEM((3, blk[0], CB), F32), pltpu.VMEM((HEADS_PER_GROUP, ATTN_BLOCK, 2 * ATTN_BLOCK), F32)]
    if token_order:
        scratch.append(pltpu.VMEM((3 * HEADS_PER_GROUP, blk[0], HEAD_DIM), F32))
        scratch.append(pltpu.VMEM((HEADS_PER_GROUP, blk[0], HEAD_DIM), F32))

    def row_of(s_):
        s_ = jnp.clip(s_, 0, steps - 1)
        return (s_ % nb) * back + s_ // nb

    def prev_row_of(s_):
        s_ = jnp.clip(s_, 0, steps - 1)
        return row_of(s_) - jnp.where(s_ % nb > 0, back, 0)

    cur = lambda which: pl.BlockSpec(blk, lambda s_: (row_of(s_), col(which)))
    prev = lambda which: pl.BlockSpec(blk, lambda s_: (prev_row_of(s_), col(which)))
    narrow = pl.BlockSpec((blk[0], HEAD_DIM), lambda s_: (row_of(s_), 0))
    return pl.pallas_call(
        body, name=name, grid=(steps + 1,),
        in_specs=[cur(0), prev(1), cur(1), prev(2), cur(2), pl.BlockSpec(blk, lambda s_: (row_of(s_), 0)), narrow, narrow],
        out_specs=pl.BlockSpec((3, blk[0], CB), lambda s_: (0, row_of(s_ - 1), 0)),
        out_shape=_sds((3, t, CB), BF16),
        scratch_shapes=scratch,
        compiler_params=_cp("arbitrary"),
    )(z, z, z, z, z, dyc, lse, dd)


def odd_mix(os_, lses, z, conv_w, name, tm=512):
    t = z.shape[0]

    def body(o0, o1, o2, l0, l1, l2, gc_ref, db_ref, dc_ref, dx_ref, gd_ref, cw_ref,
             y_ref, yc_ref, lse_ref, conv_ref, zext):
        i = pl.program_id(0)
        a0, a1, a2 = l0[...], l1[...], l2[...]
        m = jnp.maximum(jnp.maximum(a0, a1), a2)
        tot = m + jnp.log(jnp.exp(a0 - m) + jnp.exp(a1 - m) + jnp.exp(a2 - m))
        lse_ref[...] = tot
        w0, w1, w2 = jnp.exp(a0 - tot), jnp.exp(a1 - tot), jnp.exp(a2 - tot)
        for h in range(HEADS_PER_GROUP):
            hs = slice(h * HEAD_DIM, (h + 1) * HEAD_DIM)
            yc = (w0[:, h:h + 1] * o0[:, hs].astype(F32) + w1[:, h:h + 1] * o1[:, hs].astype(F32)
                  + w2[:, h:h + 1] * o2[:, hs].astype(F32))
            yc_ref[:, hs] = yc.astype(BF16)
            silu_c, _ = _silu_and_grad(gc_ref[:, hs].astype(F32))
            y_ref[:, hs] = (yc * silu_c).astype(BF16)

        @pl.when(i == 0)
        def _():
            zext[0:HALO, :] = jnp.zeros((HALO, CB), F32)

        @pl.when(i > 0)
        def _():
            zext[0:HALO, :] = zext[tm:tm + HALO, :]

        zext[HALO:HALO + tm, :] = dc_ref[...].astype(F32) * dx_ref[...].astype(F32)
        conv = (cw_ref[0:1, :] * zext[HALO - 2:HALO - 2 + tm, :] + cw_ref[1:2, :] * zext[HALO - 1:HALO - 1 + tm, :]
                + cw_ref[2:3, :] * zext[HALO:HALO + tm, :])
        conv_ref[...] = conv.astype(BF16)
        silu_d, _ = _silu_and_grad(gd_ref[...].astype(F32))
        y_ref[:, CB:2 * CB] = (db_ref[...].astype(F32) * conv * silu_d).astype(BF16)

    row = pl.BlockSpec((tm, CB), lambda i: (i, 0))
    narrow = pl.BlockSpec((tm, HEAD_DIM), lambda i: (i, 0))
    zspec = lambda cb: pl.BlockSpec((tm, CB), lambda i, cb=cb: (i, cb))
    return pl.pallas_call(
        body, name=name, grid=(t // tm,),
        in_specs=[row] * 3 + [narrow] * 3 + [zspec(9), zspec(10), zspec(11), zspec(12), zspec(13),
                                             pl.BlockSpec(conv_w.shape, lambda i: (0, 0))],
        out_specs=[pl.BlockSpec((tm, 2 * CB), lambda i: (i, 0)), row, narrow, row],
        out_shape=[_sds((t, 2 * CB), BF16), _sds((t, CB), BF16), _sds((t, HEAD_DIM), F32), _sds((t, CB), BF16)],
        scratch_shapes=[pltpu.VMEM((HALO + tm, CB), F32)],
        compiler_params=_cp("arbitrary"),
    )(*os_, *lses, z, z, z, z, z, conv_w)


def odd_mix_bwd(dy, yc, conv, z, conv_w, name, tm=256):
    t = z.shape[0]
    nt = t // tm

    def body(dyc_ref, dyd_ref, yc_ref, conv_ref, gc_ref, db_ref, dc_ref, dx_ref, gd_ref, cw_ref,
             dz_ref, dyo_ref, dd_ref, dcw_ref, dcext):
        i = pl.program_id(0)

        @pl.when(i == 0)
        def _():
            dcext[tm:tm + HALO, :] = jnp.zeros((HALO, CB), F32)
            dcw_ref[...] = jnp.zeros_like(dcw_ref)

        @pl.when(i > 0)
        def _():
            dcext[tm:tm + HALO, :] = dcext[0:HALO, :]

        silu_c, dsilu_c = _silu_and_grad(gc_ref[...].astype(F32))
        dyc = dyc_ref[...].astype(F32)
        ycv = yc_ref[...].astype(F32)
        dyo = dyc * silu_c
        dyo_ref[...] = dyo.astype(BF16)
        dz_ref[:, 0:CB] = (dyc * ycv * dsilu_c).astype(BF16)
        prod = dyo * ycv
        dd_ref[...] = _lane_pack([jnp.sum(prod[:, h * HEAD_DIM:(h + 1) * HEAD_DIM], axis=1, keepdims=True)
                                  for h in range(HEADS_PER_GROUP)])

        silu_d, dsilu_d = _silu_and_grad(gd_ref[...].astype(F32))
        dyd = dyd_ref[...].astype(F32)
        convv = conv_ref[...].astype(F32)
        dbv = db_ref[...].astype(F32)
        dz_ref[:, CB:2 * CB] = (dyd * convv * silu_d).astype(BF16)
        dz_ref[:, 4 * CB:5 * CB] = (dyd * dbv * convv * dsilu_d).astype(BF16)
        dcext[0:tm, :] = dyd * dbv * silu_d
        dcv, dxv = dc_ref[...].astype(F32), dx_ref[...].astype(F32)
        zc = dcv * dxv
        d0, d1, d2 = dcext[0:tm, :], dcext[1:1 + tm, :], dcext[2:2 + tm, :]
        dzc = cw_ref[2:3, :] * d0 + cw_ref[1:2, :] * d1 + cw_ref[0:1, :] * d2
        dz_ref[:, 2 * CB:3 * CB] = (dzc * dxv).astype(BF16)
        dz_ref[:, 3 * CB:4 * CB] = (dzc * dcv).astype(BF16)
        dcw_ref[0:1, :] += jnp.sum(zc * d2, axis=0, keepdims=True)
        dcw_ref[1:2, :] += jnp.sum(zc * d1, axis=0, keepdims=True)
        dcw_ref[2:3, :] += jnp.sum(zc * d0, axis=0, keepdims=True)

    rev = lambda cb: pl.BlockSpec((tm, CB), lambda i, cb=cb: (nt - 1 - i, cb))
    return pl.pallas_call(
        body, name=name, grid=(nt,),
        in_specs=[rev(0), rev(1), rev(0), rev(0), rev(9), rev(10), rev(11), rev(12), rev(13),
                  pl.BlockSpec(conv_w.shape, lambda i: (0, 0))],
        out_specs=[pl.BlockSpec((tm, 5 * CB), lambda i: (nt - 1 - i, 0)), rev(0),
                   pl.BlockSpec((tm, HEAD_DIM), lambda i: (nt - 1 - i, 0)),
                   pl.BlockSpec(conv_w.shape, lambda i: (0, 0))],
        out_shape=[_sds((t, 5 * CB), BF16), _sds((t, CB), BF16), _sds((t, HEAD_DIM), F32), _sds(conv_w.shape, F32)],
        scratch_shapes=[pltpu.VMEM((tm + HALO, CB), F32)],
        compiler_params=_cp("arbitrary"),
    )(dy, dy, yc, conv, z, z, z, z, z, conv_w)


def out_proj_final_loss(y, w, x, g, target, name, tm=1024, rows=256):
    t, d = x.shape
    k = y.shape[1]

    def body(y_ref, w_ref, x_ref, g_ref, t_ref, dx_ref, loss_ref, dg_ref):
        i = pl.program_id(0)
        dx_ref[...] = x_ref[...] + _dot(y_ref[...], w_ref[...])
        gv = g_ref[...]
        loss = jnp.zeros((1, 1), F32)
        dg = jnp.zeros((1, d), F32)
        for c in range(tm // rows):
            rs = slice(c * rows, (c + 1) * rows)
            xv = dx_ref[rs, :]
            r = lax.rsqrt(jnp.mean(xv * xv, axis=-1, keepdims=True) + EPS)
            xn = xv * r
            err = xn * gv - t_ref[rs, :]
            loss = loss + 0.5 * jnp.sum(jnp.mean(err * err, axis=-1, keepdims=True), axis=0, keepdims=True)
            dy = err * (1.0 / d)
            dyg = dy * gv
            dx_ref[rs, :] = r * (dyg - xn * jnp.mean(dyg * xn, axis=-1, keepdims=True))
            dg = dg + jnp.sum(dy * xn, axis=0, keepdims=True)

        @pl.when(i == 0)
        def _():
            loss_ref[...] = jnp.broadcast_to(loss, loss_ref.shape)
            dg_ref[...] = dg

        @pl.when(i > 0)
        def _():
            loss_ref[...] += jnp.broadcast_to(loss, loss_ref.shape)
            dg_ref[...] += dg

    row = pl.BlockSpec((tm, d), lambda i: (i, 0))
    return pl.pallas_call(
        body, name=name, grid=(t // tm,),
        in_specs=[pl.BlockSpec((tm, k), lambda i: (i, 0)), pl.BlockSpec((k, d), lambda i: (0, 0)), row,
                  pl.BlockSpec((1, d), lambda i: (0, 0)), row],
        out_specs=[row, pl.BlockSpec((8, 128), lambda i: (0, 0)), pl.BlockSpec((1, d), lambda i: (0, 0))],
        out_shape=[_sds((t, d), F32), _sds((8, 128), F32), _sds((1, d), F32)],
        compiler_params=_cp("arbitrary"),
    )(y, w, x, g, target)


def _position():
    x, y, c = lax.axis_index("x"), lax.axis_index("y"), lax.axis_index("c")
    return x, y, c, 4 * x + 2 * y + c


def _peer(x, y, c, k):
    px = 1 - x if k & 4 else x
    py = 1 - y if k & 2 else y
    pc = 1 - c if k & 1 else c
    return (px, py, pc), 4 * px + 2 * py + pc


def _block(ref, axis, size, idx):
    index = [slice(None)] * len(ref.shape)
    index[axis] = pl.ds(idx * size, size)
    return ref.at[tuple(index)]


def _other_chips(x, y):
    return [(1 - x, y), (x, 1 - y), (1 - x, 1 - y)]


def _gather_copy(ins, lands, axes, send_sems, recv_sems, k, p, block, to, from_shard=False):
    dst = _block(lands[p], axes[p], lands[p].shape[axes[p]] // N_DEV, block)
    sem = k * len(lands) + p
    return pltpu.make_async_remote_copy(
        src_ref=ins[p] if from_shard else dst, dst_ref=dst, send_sem=send_sems.at[sem], recv_sem=recv_sems.at[sem],
        device_id=to, device_id_type=MESH_ID)


def _own_block_copy(ins, lands, axes, sems, p, me):
    n = len(lands)
    dst = _block(lands[p], axes[p], lands[p].shape[axes[p]] // N_DEV, me)
    return pltpu.make_async_copy(ins[p], dst, sems.at[4 * n + p])


def all_gather_two_level(shards, axes, name):
    n = len(shards)
    fulls = []
    for s, ax in zip(shards, axes):
        shape = list(s.shape)
        shape[ax] *= N_DEV
        fulls.append(_sds(tuple(shape), s.dtype))

    def body(*refs):
        ins, outs = refs[:n], refs[n:2 * n]
        send_sems, recv_sems, local_sems = refs[2 * n:]
        x, y, c, me = _position()
        sibling, here = (x, y, 1 - c), (x, y, c)
        chips = _other_chips(x, y)
        copy = functools.partial(_gather_copy, ins, outs, axes, send_sems, recv_sems)
        local = [pltpu.make_async_copy(ins[p], _block(outs[p], axes[p], ins[p].shape[axes[p]], me), local_sems.at[p])
                 for p in range(n)]
        sent = []
        for p in range(n):
            sent.append(copy(0, p, me, sibling, from_shard=True))
            sent += [copy(1 + j, p, me, (*chip, c), from_shard=True) for j, chip in enumerate(chips)]
        for cp in local + sent:
            cp.start()
        for j, (px, py) in enumerate(chips):
            for p in range(n):
                arrived = 4 * px + 2 * py + c
                copy(1 + j, p, arrived, here).wait_recv()
                sent.append(copy(4 + j, p, arrived, sibling))
                sent[-1].start()
        for p in range(n):
            copy(0, p, 4 * x + 2 * y + 1 - c, here).wait_recv()
            for j, (px, py) in enumerate(chips):
                copy(4 + j, p, 4 * px + 2 * py + 1 - c, here).wait_recv()
        for cp in sent:
            cp.wait_send()
        for cp in local:
            cp.wait()

    any_spec = pl.BlockSpec(memory_space=pl.ANY)
    return pl.pallas_call(
        body, name=name,
        in_specs=[any_spec] * n, out_specs=[any_spec] * n, out_shape=fulls,
        scratch_shapes=[pltpu.SemaphoreType.DMA((7 * n,)), pltpu.SemaphoreType.DMA((7 * n,)), pltpu.SemaphoreType.DMA((n,))],
    )(*shards)


def gather_two_level_start(shards, axes, after, name):
    n = len(shards)
    lands = []
    for s, ax in zip(shards, axes):
        shape = list(s.shape)
        shape[ax] *= N_DEV
        lands.append(lax.empty(tuple(shape), s.dtype))

    def body(*refs):
        ins, land = refs[:n], refs[n:2 * n]
        send_sems, recv_sems, token = refs[2 * n + 1], refs[2 * n + 2], refs[-1]
        x, y, c, me = _position()
        copy = functools.partial(_gather_copy, ins, land, axes, send_sems, recv_sems)
        for p in range(n):
            copy(0, p, me, (x, y, 1 - c), from_shard=True).start()
            for j, chip in enumerate(_other_chips(x, y)):
                copy(1 + j, p, me, (*chip, c), from_shard=True).start()
            _own_block_copy(ins, land, axes, send_sems, p, me).start()
        token[...] = jnp.zeros_like(token)

    outs = pl.pallas_call(
        body, name=name,
        out_shape=(pltpu.SemaphoreType.DMA((5 * n,)), pltpu.SemaphoreType.DMA((4 * n,)),
                   *[pltpu.HBM(s.shape, s.dtype) for s in shards], *[pltpu.HBM(l.shape, l.dtype) for l in lands],
                   _sds((8, 128), F32)),
        in_specs=[_HBM] * (2 * n) + [pl.BlockSpec(memory_space=pl.ANY)],
        out_specs=(_SEM, _SEM, *[_HBM] * (2 * n), pl.BlockSpec(memory_space=pltpu.VMEM)),
        input_output_aliases={i: 2 + i for i in range(2 * n)},
        compiler_params=pltpu.CompilerParams(has_side_effects=_EFFECT),
    )(*[_in_hbm(s) for s in shards], *[_in_hbm(l) for l in lands], after)
    return outs[0], outs[1], list(outs[2:2 + n]), list(outs[2 + n:2 + 2 * n]), outs[-1]


def gather_two_level_forward(started, axes, after, name):
    send_a, recv_a, shards, lands, _ = started
    n = len(shards)

    def body(*refs):
        land = refs[:n]
        send_first, recv_first = refs[n], refs[n + 1]
        send_fwd, recv_fwd, token = refs[n + 3], refs[n + 4], refs[-1]
        x, y, c, _ = _position()
        for j, (px, py) in enumerate(_other_chips(x, y)):
            for p in range(n):
                arrived = 4 * px + 2 * py + c
                _gather_copy(None, land, axes, send_first, recv_first, 1 + j, p, arrived, (x, y, c)).wait_recv()
                _gather_copy(None, land, axes, send_fwd, recv_fwd, j, p, arrived, (x, y, 1 - c)).start()
        token[...] = jnp.zeros_like(token)

    outs = pl.pallas_call(
        body, name=name,
        out_shape=(pltpu.SemaphoreType.DMA((3 * n,)), pltpu.SemaphoreType.DMA((3 * n,)),
                   *[pltpu.HBM(l.shape, l.dtype) for l in lands], _sds((8, 128), F32)),
        in_specs=[_HBM] * n + [_SEM, _SEM, pl.BlockSpec(memory_space=pl.ANY)],
        out_specs=(_SEM, _SEM, *[_HBM] * n, pl.BlockSpec(memory_space=pltpu.VMEM)),
        input_output_aliases={i: 2 + i for i in range(n)},
        compiler_params=pltpu.CompilerParams(has_side_effects=_EFFECT),
    )(*lands, send_a, recv_a, after)
    return send_a, recv_a, outs[0], outs[1], shards, list(outs[2:2 + n]), outs[-1]


def gather_two_level_wait(forwarded, axes, after, name):
    send_a, recv_a, send_f, recv_f, shards, lands, _ = forwarded
    n = len(shards)

    def body(*refs):
        ins, land = refs[:n], refs[n:2 * n]
        sa, ra, sf, rf = refs[2 * n:2 * n + 4]
        x, y, c, me = _position()
        here = (x, y, c)
        chips = _other_chips(x, y)
        for p in range(n):
            for k in range(4):
                _gather_copy(ins, land, axes, sa, ra, k, p, me, here, from_shard=True).wait_send()
            _own_block_copy(ins, land, axes, sa, p, me).wait()
            _gather_copy(ins, land, axes, sa, ra, 0, p, 4 * x + 2 * y + 1 - c, here).wait_recv()
            for j, (px, py) in enumerate(chips):
                _gather_copy(ins, land, axes, sf, rf, j, p, 4 * px + 2 * py + c, here).wait_send()
                _gather_copy(ins, land, axes, sf, rf, j, p, 4 * px + 2 * py + 1 - c, here).wait_recv()

    outs = pl.pallas_call(
        body, name=name,
        out_shape=(*[pltpu.HBM(s.shape, s.dtype) for s in shards], *[pltpu.HBM(l.shape, l.dtype) for l in lands]),
        in_specs=[_HBM] * (2 * n) + [_SEM] * 4 + [pl.BlockSpec(memory_space=pl.ANY)],
        out_specs=tuple([_HBM] * (2 * n)),
        input_output_aliases={i: i for i in range(2 * n)},
        compiler_params=pltpu.CompilerParams(has_side_effects=_EFFECT),
    )(*shards, *lands, send_a, recv_a, send_f, recv_f, after)
    return list(outs[n:])


_HBM = pl.BlockSpec(memory_space=pltpu.HBM)
_SEM = pl.BlockSpec(memory_space=pltpu.SEMAPHORE)
_EFFECT = pltpu.SideEffectType.DATAFLOW_SIDE_EFFECTING


def _in_hbm(a):
    return pltpu.with_memory_space_constraint(a, pltpu.HBM)


def _landing(mode, src, axis):
    me = 4 * lax.axis_index("x") + 2 * lax.axis_index("y") + lax.axis_index("c")
    if mode == "gather":
        shape = list(src.shape)
        size = shape[axis]
        shape[axis] *= N_DEV
        return lax.dynamic_update_slice_in_dim(lax.empty(tuple(shape), src.dtype), src, me * size, axis)
    size = src.shape[axis] // N_DEV
    own = lax.dynamic_slice_in_dim(src, me * size, size, axis)
    return lax.dynamic_update_slice_in_dim(lax.empty((N_DEV, *own.shape), src.dtype), own[None], me, 0)


def _seven(mode, land_ref, axis):
    if mode == "gather":
        return _block(land_ref, axis, (N_DEV - 1) * (land_ref.shape[axis] // N_DEV), 0)
    return land_ref.at[pl.ds(0, N_DEV - 1)]


def exchange_start(mode, srcs, axes, after, name):
    n = len(srcs)
    lands = [_landing(mode, s, ax) for s, ax in zip(srcs, axes)]

    def body(*refs):
        src, land = refs[:n], refs[n:2 * n]
        send_sems, recv_sems = refs[2 * n + 1], refs[2 * n + 2]
        token = refs[-1]
        x, y, c, me = _position()
        for k in range(1, N_DEV):
            peer, pj = _peer(x, y, c, k)
            for p in range(n):
                if mode == "gather":
                    s = src[p]
                    dst = _block(land[p], axes[p], src[p].shape[axes[p]], me)
                else:
                    s = _block(src[p], axes[p], src[p].shape[axes[p]] // N_DEV, pj)
                    dst = land[p].at[me]
                pltpu.make_async_remote_copy(
                    src_ref=s, dst_ref=dst, send_sem=send_sems.at[p], recv_sem=recv_sems.at[p],
                    device_id=peer, device_id_type=MESH_ID).start()
        token[...] = jnp.zeros_like(token)

    outs = pl.pallas_call(
        body, name=name,
        out_shape=(pltpu.SemaphoreType.DMA((n,)), pltpu.SemaphoreType.DMA((n,)),
                   *[pltpu.HBM(s.shape, s.dtype) for s in srcs], *[pltpu.HBM(l.shape, l.dtype) for l in lands],
                   _sds((8, 128), F32)),
        in_specs=[_HBM] * (2 * n) + [pl.BlockSpec(memory_space=pl.ANY)],
        out_specs=(_SEM, _SEM, *[_HBM] * (2 * n), pl.BlockSpec(memory_space=pltpu.VMEM)),
        input_output_aliases={i: 2 + i for i in range(2 * n)},
        compiler_params=pltpu.CompilerParams(has_side_effects=_EFFECT),
    )(*[_in_hbm(s) for s in srcs], *[_in_hbm(l) for l in lands], after)
    return outs[0], outs[1], list(outs[2:2 + n]), list(outs[2 + n:2 + 2 * n]), outs[-1]


def exchange_wait(mode, started, axes, after, name):
    send_sems, recv_sems, srcs, lands, _ = started
    n = len(srcs)

    def body(*refs):
        land = refs[n:2 * n]
        send_ref, recv_ref = refs[2 * n], refs[2 * n + 1]
        x, y, c, _ = _position()
        for p in range(n):
            seven = _seven(mode, land[p], axes[p])
            cp = pltpu.make_async_remote_copy(
                src_ref=seven, dst_ref=seven, send_sem=send_ref.at[p], recv_sem=recv_ref.at[p],
                device_id=(x, y, c), device_id_type=MESH_ID)
            cp.wait_send()
            cp.wait_recv()

    outs = pl.pallas_call(
        body, name=name,
        out_shape=(*[pltpu.HBM(s.shape, s.dtype) for s in srcs], *[pltpu.HBM(l.shape, l.dtype) for l in lands]),
        in_specs=[_HBM] * (2 * n) + [_SEM, _SEM, pl.BlockSpec(memory_space=pl.ANY)],
        out_specs=tuple([_HBM] * (2 * n)),
        input_output_aliases={i: i for i in range(2 * n)},
        compiler_params=pltpu.CompilerParams(has_side_effects=_EFFECT),
    )(*srcs, *lands, send_sems, recv_sems, after)
    return list(outs[n:])


def _adam_math(g, w, m, v):
    m2 = ADAM_B1 * m + (1.0 - ADAM_B1) * g
    v2 = ADAM_B2 * v + (1.0 - ADAM_B2) * (g * g)
    m_hat = m2 / (1.0 - ADAM_B1 ** ADAM_STEP)
    v_hat = v2 / (1.0 - ADAM_B2 ** ADAM_STEP)
    delta = -ADAM_LR * (m_hat / (jnp.sqrt(v_hat) + ADAM_EPS) + ADAM_WD * w)
    return delta, m2, v2


def adamw(g, w, m, v, name, slots, tr=128):
    r, c = w.shape
    tr = min(tr, r)

    def body(g_ref, w_ref, m_ref, v_ref, go_ref, d_ref, mo_ref, vo_ref):
        if slots:
            gs = g_ref[0].astype(F32)
            for s in range(1, N_DEV):
                gs = gs + g_ref[s].astype(F32)
        else:
            gs = g_ref[...]
        go_ref[...] = gs
        d_ref[...], mo_ref[...], vo_ref[...] = _adam_math(gs, w_ref[...], m_ref[...], v_ref[...])

    row = pl.BlockSpec((tr, c), lambda i: (i, 0))
    gspec = pl.BlockSpec((N_DEV, tr, c), lambda i: (0, i, 0)) if slots else row
    return pl.pallas_call(
        body, name=name, grid=(r // tr,),
        in_specs=[gspec, row, row, row], out_specs=[row] * 4, out_shape=[_sds((r, c), F32)] * 4,
        compiler_params=_cp("parallel"),
    )(g, w, m, v)


def sum_slots(g, name):
    _, r, c = g.shape

    def body(g_ref, o_ref):
        gs = g_ref[0]
        for s in range(1, N_DEV):
            gs = gs + g_ref[s]
        o_ref[...] = gs

    return pl.pallas_call(
        body, name=name, grid=(1,),
        in_specs=[pl.BlockSpec((N_DEV, r, c), lambda i: (0, 0, 0))],
        out_specs=pl.BlockSpec((r, c), lambda i: (0, 0)), out_shape=_sds((r, c), F32),
        compiler_params=_cp("arbitrary"),
    )(g)


def _rows128(a, pad_to=8):
    a = a.reshape(-1, 128)
    pad = (-a.shape[0]) % pad_to
    return jnp.pad(a, ((0, pad), (0, 0))) if pad else a


def kernel(x, even_norm, even_w_in, even_pool_w, even_pool_scale, even_ws, even_bs, even_w_out, odd_norm, odd_w_in, odd_conv_w, odd_w_out, final_norm, loss_target, m_even_norm, m_even_w_in, m_even_pool_w, m_even_pool_scale, m_even_ws, m_even_bs, m_even_w_out, m_odd_norm, m_odd_w_in, m_odd_conv_w, m_odd_w_out, m_final_norm, v_even_norm, v_even_w_in, v_even_pool_w, v_even_pool_scale, v_even_ws, v_even_bs, v_even_w_out, v_odd_norm, v_odd_w_in, v_odd_conv_w, v_odd_w_out, v_final_norm):
    x0 = x[0]
    target = loss_target[0]
    me = 4 * lax.axis_index("x") + 2 * lax.axis_index("y") + lax.axis_index("c")

    (we_in,) = all_gather_two_level([even_w_in[0].astype(BF16)], [1], "gather_even_in")
    odd_small = jnp.pad(odd_norm, ((0, 7), (0, 0))) + jnp.pad(odd_conv_w[0], ((1, 4), (0, 0)))
    rest_axes, odd_axes = [1, 0, 1], [1, 0]
    rest_flight = exchange_start("gather", [even_pool_w[0].astype(BF16), even_w_out[0].astype(BF16), odd_small],
                                 rest_axes, we_in, "gather_even_rest_start")
    odd_flight = gather_two_level_start([odd_w_in[0].astype(BF16), odd_w_out[0].astype(BF16)],
                                        odd_axes, rest_flight[4], "gather_odd_start")
    scale = even_pool_scale
    ws = even_ws[0]
    bs_col = even_bs[0][:, :, None]
    g_fin = final_norm[None, :]

    z_e, h_e_t, we_in_t = norm_matmul(x0, even_norm + odd_flight[4][0:1, 0:1], we_in, "even_in", tn=1280)
    wp, we_out, odd_small = exchange_wait("gather", rest_flight, rest_axes, z_e, "gather_even_rest_wait")
    g_odd, conv_w = odd_small[0:1], odd_small[1:4]
    y_e, pooled = even_mix(z_e, wp, scale, ws, bs_col, "even_mix")
    x1, h_o, h_o_t = matmul_residual(y_e, we_out, x0, "even_out", next_gain=g_odd)
    odd_flight = gather_two_level_forward(odd_flight, odd_axes, x1, "gather_odd_forward")
    h_layouts = [h_o] + [permute_mats(h_o, [], gi, False, f"dilate_h_{gi}")[0] for gi in (1, 2)]
    wo_in, wo_out = gather_two_level_wait(odd_flight, odd_axes, h_layouts[2], "gather_odd_wait")
    z_o, wo_in_t = matmul_layouts(h_layouts, wo_in, "odd_in")
    attn = [attn_fwd(z_o, gi, f"attn_fwd_{gi}") for gi in range(3)]
    attn[2] = permute_mats(attn[2][0], [attn[2][1]], 2, True, "undilate_attn_2")
    y_o, yc, lse_tot, conv = odd_mix([a[0] for a in attn], [a[1] for a in attn], z_o, conv_w, "odd_mix")
    dx2, loss_blk, dg_fin = out_proj_final_loss(y_o, wo_out, x1, g_fin, target, "odd_out_loss")

    dy_o, dwo_out = out_proj_bwd(dx2, wo_out.T, y_o, "odd_out_bwd")
    dz5_o, dyc, dd, dconv_w = odd_mix_bwd(dy_o, yc, conv, z_o, conv_w, "odd_mix_bwd")
    dqkv = [attn_bwd(z_o, dyc, lse_tot, dd, 0, "attn_bwd_0")]
    dqkv.append(attn_bwd(z_o, dyc, lse_tot, dd, 1, "attn_bwd_1"))
    dyc_2, lse_2, dd_2 = permute_mats(dyc, [lse_tot, dd], 2, False, "dilate_dy_2")
    dqkv.append(permute_slabs(attn_bwd(z_o, dyc_2, lse_2, dd_2, 2, "attn_bwd_2"), 2, True, "undilate_dqkv_2"))
    src_o = [(dqkv[0], ("slabs", 0)), (dqkv[1], ("slabs", 1)), (dqkv[2], ("slabs", 2)), (dz5_o, ("cols", 9, 5))]
    dwo_in = in_proj_dw(h_o_t, src_o, ODD_COLBLOCKS, "odd_in_dw")
    odd_grads = exchange_start("scatter", [dwo_in, dwo_out], [1, 0], dconv_w, "scatter_odd_start")
    dx1, dg_odd = in_proj_dx(src_o, ODD_COLBLOCKS, wo_in_t, x1, g_odd + odd_grads[4][0:1, 0:1], dx2, "odd_in_dx")

    dy_e, dwe_out = out_proj_bwd(dx1, we_out.T, y_e, "even_out_bwd")
    dz_e, dwp, dscale, dws, dbs_col = even_mix_bwd(dy_e, z_e, pooled, wp, scale, ws, bs_col, "even_mix_bwd")
    src_e = [(dz_e, ("cols", 0, 5))]
    even_grads_a = exchange_start("scatter", [dwp.astype(BF16), dwe_out], [1, 0], dscale, "scatter_even_rest_start")
    dwe_in = in_proj_dw(h_e_t, src_e, 5, "even_in_dw")
    even_grads_b = exchange_start("scatter", [dwe_in], [1], even_grads_a[4], "scatter_even_in_start")
    dx0, dg_even = in_proj_dx(src_e, 5, we_in_t, x0, even_norm + even_grads_b[4][0:1, 0:1], dx1, "even_in_dx")
    small = jnp.concatenate([
        _rows128(dg_even), _rows128(dscale), _rows128(dws), _rows128(dbs_col), _rows128(dg_fin),
        _rows128(dg_odd), _rows128(dconv_w), loss_blk], axis=0)
    n_small = small.shape[0]
    small_flight = exchange_start("gather", [small], [0], dg_fin, "gather_small_start")
    p_wo_in, p_wo_out = exchange_wait("scatter", odd_grads, [1, 0], small_flight[4], "scatter_odd_wait")
    p_wp, p_we_out = exchange_wait("scatter", even_grads_a, [1, 0], small_flight[4], "scatter_even_rest_wait")
    (p_we_in,) = exchange_wait("scatter", even_grads_b, [1], small_flight[4], "scatter_even_in_wait")

    two_d = lambda a, r, c: a.reshape(r, c)
    sharded = [
        ("even_w_in", p_we_in, even_w_in, m_even_w_in, v_even_w_in, (1024, 640)),
        ("even_pool_w", p_wp.reshape(N_DEV, 128, 256), even_pool_w, m_even_pool_w, v_even_pool_w, (128, 256)),
        ("even_w_out", p_we_out, even_w_out, m_even_w_out, v_even_w_out, (256, 1024)),
        ("odd_w_in", p_wo_in, odd_w_in, m_odd_w_in, v_odd_w_in, (1024, 1792)),
        ("odd_w_out", p_wo_out, odd_w_out, m_odd_w_out, v_odd_w_out, (256, 1024)),
    ]
    res = {}
    for name, g, w, m, v, (r, c) in sharded:
        outs = adamw(g, two_d(w, r, c), two_d(m, r, c), two_d(v, r, c), "adamw_" + name, slots=True)
        res[name] = [o.reshape(w.shape) for o in outs]

    (small_all,) = exchange_wait("gather", small_flight, [0], res["odd_w_out"][0], "gather_small_wait")
    small_sum = sum_slots(small_all.reshape(N_DEV, n_small, 128), "sum_small_grads")
    g_even_norm = small_sum[0:8].reshape(1, 1024)
    g_scale = small_sum[8:16].reshape(1, 1024)
    g_ws = small_sum[16:528]
    g_bs = small_sum[528:532]
    g_final = small_sum[536:544].reshape(1, 1024)
    g_odd_norm = lax.dynamic_slice_in_dim(small_sum[544:552], me, 1, axis=0)
    g_conv = lax.dynamic_index_in_dim(small_sum[552:576].reshape(3, 8, 128), me, axis=1, keepdims=False)
    plain = [
        ("even_norm", g_even_norm, even_norm, m_even_norm, v_even_norm, (1, 1024)),
        ("even_pool_scale", g_scale, even_pool_scale, m_even_pool_scale, v_even_pool_scale, (1, 1024)),
        ("even_ws", g_ws, even_ws, m_even_ws, v_even_ws, (512, 128)),
        ("even_bs", g_bs, even_bs, m_even_bs, v_even_bs, (4, 128)),
        ("odd_norm", g_odd_norm, odd_norm, m_odd_norm, v_odd_norm, (1, 128)),
        ("odd_conv_w", g_conv, odd_conv_w, m_odd_conv_w, v_odd_conv_w, (3, 128)),
        ("final_norm", g_final, final_norm, m_final_norm, v_final_norm, (1, 1024)),
    ]
    for name, g, w, m, v, (r, c) in plain:
        outs = adamw(two_d(g, r, c), two_d(w, r, c), two_d(m, r, c), two_d(v, r, c), "adamw_" + name, slots=False)
        res[name] = [o.reshape(w.shape) for o in outs]

    loss = small_sum[576, 0]
    order = ["even_norm", "even_w_in", "even_pool_w", "even_pool_scale", "even_ws", "even_bs", "even_w_out",
             "odd_norm", "odd_w_in", "odd_conv_w", "odd_w_out", "final_norm"]
    return (loss, dx0[None], *[res[n][0] for n in order], *[res[n][1] for n in order],
            *[res[n][2] for n in order], *[res[n][3] for n in order])
```

```python
import functools

import jax
import jax.numpy as jnp
from jax import lax
from jax.experimental import pallas as pl
from jax.experimental.pallas import tpu as pltpu

F32 = jnp.float32
BF16 = jnp.bfloat16
MESH_ID = pl.DeviceIdType.MESH

EPS = 1e-6
NEG = -1e30
N_DEV = 8
POOL_SIZES = (2, 4, 8, 16)
DILATIONS = (1, 4, 16)
N_HEADS = 24
HEADS_PER_GROUP = 8
HEAD_DIM = 128
ATTN_BLOCK = 128
CHUNK = 128
CB = 1024
HALO = 16
ODD_COLBLOCKS = 14
ADAM_LR = 0.001
ADAM_B1 = 0.9
ADAM_B2 = 0.999
ADAM_EPS = 1e-08
ADAM_WD = 0.01
ADAM_STEP = 10
VMEM_LIMIT = 52 * 1024 * 1024


def _cp(*sem):
    return pltpu.CompilerParams(dimension_semantics=sem, vmem_limit_bytes=VMEM_LIMIT)


def _dot(a, b):
    return jnp.dot(a, b, preferred_element_type=F32)


def _dot_nt(a, b):
    return lax.dot_general(a, b, (((1,), (1,)), ((), ())), preferred_element_type=F32)


def _dot_tn(a, b):
    return lax.dot_general(a, b, (((0,), (0,)), ((), ())), preferred_element_type=F32)


def _sigmoid(x):
    return 0.5 * jnp.tanh(0.5 * x) + 0.5


def _silu_and_grad(x):
    s = _sigmoid(x)
    return x * s, s * (1.0 + x * (1.0 - s))


def _sds(shape, dtype):
    return jax.ShapeDtypeStruct(shape, dtype)


def norm_matmul(x, g, w, name, tn, tm=1024, rows=256):
    t, d = x.shape
    n = w.shape[1]
    n_col = n // tn

    def body(x_ref, g_ref, w_ref, z_ref, ht_ref, wt_ref, h_ref):
        @pl.when(pl.program_id(1) == 0)
        def _():
            for c in range(tm // rows):
                rs = slice(c * rows, (c + 1) * rows)
                xv = x_ref[rs, :]
                r = lax.rsqrt(jnp.mean(xv * xv, axis=-1, keepdims=True) + EPS)
                h_ref[rs, :] = ((xv * r) * g_ref[...]).astype(BF16)
            ht_ref[...] = h_ref[...].T

        z_ref[...] = _dot(h_ref[...], w_ref[...]).astype(BF16)

        @pl.when(pl.program_id(0) == 0)
        def _():
            wt_ref[...] = w_ref[...].T

    return pl.pallas_call(
        body, name=name, grid=(t // tm, n_col),
        in_specs=[pl.BlockSpec((tm, d), lambda i, j: (i, 0)),
                  pl.BlockSpec((1, d), lambda i, j: (0, 0)),
                  pl.BlockSpec((d, tn), lambda i, j: (0, j))],
        out_specs=[pl.BlockSpec((tm, tn), lambda i, j: (i, j)),
                   pl.BlockSpec((d, tm), lambda i, j: (0, i)),
                   pl.BlockSpec((tn, d), lambda i, j: (jnp.where(i == 0, j, n_col), 0))],
        out_shape=[_sds((t, n), BF16), _sds((d, t), BF16), _sds((n + tn, d), BF16)],
        scratch_shapes=[pltpu.VMEM((tm, d), BF16)],
        compiler_params=_cp("arbitrary", "arbitrary"),
    )(x, g, w)


def matmul_residual(y, w, x, name, next_gain=None, tm=1024, rows=256):
    t, k = y.shape
    d = w.shape[1]

    def body(y_ref, w_ref, x_ref, *rest):
        o_ref = rest[-1] if next_gain is None else rest[-3]
        o_ref[...] = x_ref[...] + _dot(y_ref[...], w_ref[...])
        if next_gain is not None:
            g_ref, h_ref, ht_ref = rest[0], rest[-2], rest[-1]
            for c in range(tm // rows):
                rs = slice(c * rows, (c + 1) * rows)
                xv = o_ref[rs, :]
                r = lax.rsqrt(jnp.mean(xv * xv, axis=-1, keepdims=True) + EPS)
                h_ref[rs, :] = ((xv * r) * g_ref[...]).astype(BF16)
            ht_ref[...] = h_ref[...].T

    row = pl.BlockSpec((tm, d), lambda i: (i, 0))
    in_specs = [pl.BlockSpec((tm, k), lambda i: (i, 0)), pl.BlockSpec((k, d), lambda i: (0, 0)), row]
    if next_gain is None:
        return pl.pallas_call(
            body, name=name, grid=(t // tm,), in_specs=in_specs, out_specs=row, out_shape=_sds((t, d), F32),
            compiler_params=_cp("parallel"),
        )(y, w, x)
    return pl.pallas_call(
        body, name=name, grid=(t // tm,),
        in_specs=in_specs + [pl.BlockSpec((1, d), lambda i: (0, 0))],
        out_specs=[row, row, pl.BlockSpec((d, tm), lambda i: (0, i))],
        out_shape=[_sds((t, d), F32), _sds((t, d), BF16), _sds((d, t), BF16)],
        compiler_params=_cp("parallel"),
    )(y, w, x, next_gain)


def matmul_layouts(hs, w, name, tm=2048):
    t, d = hs[0].shape
    n_blocks = w.shape[1] // CB

    def layout_of(j):
        return jnp.where(j < 9, j // 3, 0)

    def column_of(j):
        return jnp.where(j < 9, (j % 3) * 3 + j // 3, j)

    def body(h0, h1, h2, w_ref, z_ref, wt_ref):
        layout = layout_of(pl.program_id(1))
        for v, h_ref in enumerate((h0, h1, h2)):
            @pl.when(layout == v)
            def _(h_ref=h_ref):
                z_ref[...] = _dot(h_ref[...], w_ref[...]).astype(BF16)
        @pl.when(pl.program_id(0) == 0)
        def _():
            wt_ref[...] = w_ref[...].T

    row = pl.BlockSpec((tm, d), lambda i, j: (i, 0))
    wt_block = lambda i, j: (jnp.where(i == 0, column_of(j), n_blocks), 0)
    return pl.pallas_call(
        body, name=name, grid=(t // tm, n_blocks),
        in_specs=[row, row, row, pl.BlockSpec((d, CB), lambda i, j: (0, column_of(j)))],
        out_specs=[pl.BlockSpec((tm, CB), lambda i, j: (i, column_of(j))), pl.BlockSpec((CB, d), wt_block)],
        out_shape=[_sds((t, n_blocks * CB), BF16), _sds(((n_blocks + 1) * CB, d), BF16)],
        compiler_params=_cp("arbitrary", "arbitrary"),
    )(*hs, w)


def out_proj_bwd(dx, wt, y, name, tm=512):
    t, d = dx.shape
    k = wt.shape[1]
    steps = t // tm
    half = d // 2

    def body(dx_ref, wt_ref, y_ref, dy_ref, dw_ref, acc):
        i = pl.program_id(0)
        dxb = dx_ref[...].astype(BF16)
        dy_ref[...] = _dot(dxb, wt_ref[...]).astype(BF16)

        @pl.when(i == 0)
        def _():
            acc[...] = jnp.zeros_like(acc)

        for c in range(2):
            acc[:, c * half:(c + 1) * half] += _dot_tn(y_ref[...], dxb[:, c * half:(c + 1) * half])

        @pl.when(i == steps - 1)
        def _():
            dw_ref[...] = acc[...].astype(BF16)

    return pl.pallas_call(
        body, name=name, grid=(steps,),
        in_specs=[pl.BlockSpec((tm, d), lambda i: (i, 0)),
                  pl.BlockSpec((d, k), lambda i: (0, 0)),
                  pl.BlockSpec((tm, k), lambda i: (i, 0))],
        out_specs=[pl.BlockSpec((tm, k), lambda i: (i, 0)),
                   pl.BlockSpec((k, d), lambda i: (0, 0))],
        out_shape=[_sds((t, k), BF16), _sds((k, d), BF16)],
        scratch_shapes=[pltpu.VMEM((k, d), F32)],
        compiler_params=_cp("arbitrary"),
    )(dx, wt, y)


def _source_block(kind, rows):
    if kind[0] == "cols":
        return (rows, CB)
    return (None, rows, CB)


def _source_active(kind, j):
    if kind[0] == "cols":
        _, first, n = kind
        return (j >= first) & (j < first + n), jnp.clip(j - first, 0, n - 1)
    _, group = kind
    return (j < 9) & (j % 3 == group), jnp.clip(j // 3, 0, 2)


def _source_index(kind, row_block, inner):
    if kind[0] == "cols":
        return (row_block, inner)
    return (inner, row_block, 0)


def in_proj_dw(ht, sources, n_blocks, name, tk=1024):
    d, t = ht.shape
    steps = t // tk
    ns = len(sources)

    def body(*refs):
        h_ref, src_refs, dw_ref, acc = refs[0], refs[1:1 + ns], refs[1 + ns], refs[2 + ns]
        j, k = pl.program_id(0), pl.program_id(1)

        @pl.when(k == 0)
        def _():
            acc[...] = jnp.zeros_like(acc)

        for s, (_, kind) in enumerate(sources):
            active, _ = _source_active(kind, j)

            @pl.when(active)
            def _(s=s):
                acc[...] += _dot(h_ref[...], src_refs[s][...])

        @pl.when(k == steps - 1)
        def _():
            dw_ref[...] = acc[...].astype(BF16)

    def src_spec(kind):
        def index(j, k):
            active, inner = _source_active(kind, j)
            return _source_index(kind, jnp.where(active, k, 0), inner)
        return pl.BlockSpec(_source_block(kind, tk), index)

    return pl.pallas_call(
        body, name=name, grid=(n_blocks, steps),
        in_specs=[pl.BlockSpec((d, tk), lambda j, k: (0, k))] + [src_spec(kind) for _, kind in sources],
        out_specs=pl.BlockSpec((d, CB), lambda j, k: (0, j)),
        out_shape=_sds((d, n_blocks * CB), BF16),
        scratch_shapes=[pltpu.VMEM((d, CB), F32)],
        compiler_params=_cp("parallel", "arbitrary"),
    )(ht, *[a for a, _ in sources])


def in_proj_dx(sources, n_blocks, wt, x, g, dres, name, tm=1024, rows=256):
    t, d = x.shape
    ns = len(sources)

    def body(*refs):
        src_refs = refs[:ns]
        w_ref, x_hbm, g_ref, dres_hbm, dx_ref, dg_ref, acc, x_ref, dres_ref, sems = refs[ns:]
        i, p = pl.program_id(0), pl.program_id(1)
        tile = pl.ds(pl.multiple_of(i * tm, tm), tm)
        fetch = [pltpu.make_async_copy(x_hbm.at[tile, :], x_ref, sems.at[0]),
                 pltpu.make_async_copy(dres_hbm.at[tile, :], dres_ref, sems.at[1])]

        @pl.when(p == 0)
        def _():
            acc[...] = jnp.zeros_like(acc)
            for cp in fetch:
                cp.start()

        for s, (_, kind) in enumerate(sources):
            active, _ = _source_active(kind, p)

            @pl.when(active)
            def _(s=s):
                acc[...] += _dot(src_refs[s][...], w_ref[...])

        @pl.when(p == n_blocks - 1)
        def _():
            for cp in fetch:
                cp.wait()
            part = jnp.zeros((1, d), F32)
            for c in range(tm // rows):
                rs = slice(c * rows, (c + 1) * rows)
                xv = x_ref[rs, :]
                r = lax.rsqrt(jnp.mean(xv * xv, axis=-1, keepdims=True) + EPS)
                xn = xv * r
                dh = acc[rs, :]
                dhg = dh * g_ref[...]
                dx_ref[rs, :] = dres_ref[rs, :] + r * (dhg - xn * jnp.mean(dhg * xn, axis=-1, keepdims=True))
                part = part + jnp.sum(dh * xn, axis=0, keepdims=True)

            @pl.when(i == 0)
            def _():
                dg_ref[...] = part

            @pl.when(i > 0)
            def _():
                dg_ref[...] += part

    def src_spec(kind):
        def index(i, p):
            if kind[0] == "cols":
                _, first, n = kind
                inner = jnp.clip(p - first, 0, n - 1)
            else:
                _, group = kind
                inner = jnp.clip((p - group + 2) // 3, 0, 2)
            return _source_index(kind, i, inner)
        return pl.BlockSpec(_source_block(kind, tm), index)

    any_spec = pl.BlockSpec(memory_space=pl.ANY)
    return pl.pallas_call(
        body, name=name, grid=(t // tm, n_blocks),
        in_specs=[src_spec(kind) for _, kind in sources] + [
            pl.BlockSpec((CB, d), lambda i, p: (p, 0)), any_spec, pl.BlockSpec((1, d), lambda i, p: (0, 0)), any_spec],
        out_specs=[pl.BlockSpec((tm, d), lambda i, p: (i, 0)),
                   pl.BlockSpec((1, d), lambda i, p: (0, 0))],
        out_shape=[_sds((t, d), F32), _sds((1, d), F32)],
        scratch_shapes=[pltpu.VMEM((tm, d), F32), pltpu.VMEM((tm, d), F32), pltpu.VMEM((tm, d), F32),
                        pltpu.SemaphoreType.DMA((2,))],
        compiler_params=_cp("arbitrary", "arbitrary"),
    )(*[a for a, _ in sources], wt, x, g, dres)


def _window_counts(first_row, rows, w):
    t = first_row + lax.broadcasted_iota(jnp.int32, (rows, 1), 0)
    return jnp.minimum(t + 1, w).astype(F32)


def _tril_bf16(ws_ref, g):
    r = lax.broadcasted_iota(jnp.int32, (CHUNK, CHUNK), 0)
    c = lax.broadcasted_iota(jnp.int32, (CHUNK, CHUNK), 1)
    return jnp.where(r >= c, ws_ref[g], 0.0).astype(BF16), r >= c


def even_mix(z, wp, scale, ws, bs_col, name, tm=512):
    t = z.shape[0]
    gd = CB // len(POOL_SIZES)

    def body(a_ref, ga_ref, u_ref, v_ref, gb_ref, wp_ref, sc_ref, ws_ref, bs_ref, y_ref, pooled_ref, aext):
        i = pl.program_id(0)

        @pl.when(i == 0)
        def _():
            aext[0:HALO, :] = jnp.zeros((HALO, CB), F32)

        @pl.when(i > 0)
        def _():
            aext[0:HALO, :] = aext[tm:tm + HALO, :]

        aext[HALO:HALO + tm, :] = a_ref[...].astype(F32)
        for g, w in enumerate(POOL_SIZES):
            cols = slice(g * gd, (g + 1) * gd)
            tok = aext[HALO:HALO + tm, cols]
            s = tok
            for k in range(1, w):
                s = s + aext[HALO - k:HALO - k + tm, cols]
            pooled = (s / _window_counts(i * tm, tm, w) - tok).astype(BF16)
            pooled_ref[:, cols] = pooled
            mixed = _dot(pooled, wp_ref[g])
            silu_a, _ = _silu_and_grad(ga_ref[:, cols].astype(F32))
            y_ref[:, cols] = (mixed * sc_ref[:, cols] * silu_a).astype(BF16)
        for g in range(4):
            cols = slice(g * gd, (g + 1) * gd)
            wg, _ = _tril_bf16(ws_ref, g)
            for c in range(tm // CHUNK):
                rows = slice(c * CHUNK, (c + 1) * CHUNK)
                m = _dot(wg, v_ref[rows, cols]) + bs_ref[g]
                silu_b, _ = _silu_and_grad(gb_ref[rows, cols].astype(F32))
                y_ref[rows, CB + g * gd:CB + (g + 1) * gd] = (
                    u_ref[rows, cols].astype(F32) * m * silu_b).astype(BF16)

    zspec = lambda cb: pl.BlockSpec((tm, CB), lambda i, cb=cb: (i, cb))
    full = lambda shape: pl.BlockSpec(shape, lambda i: (0,) * len(shape))
    return pl.pallas_call(
        body, name=name, grid=(t // tm,),
        in_specs=[zspec(0), zspec(1), zspec(2), zspec(3), zspec(4),
                  full(wp.shape), full(scale.shape), full(ws.shape), full(bs_col.shape)],
        out_specs=[pl.BlockSpec((tm, 2 * CB), lambda i: (i, 0)), pl.BlockSpec((tm, CB), lambda i: (i, 0))],
        out_shape=[_sds((t, 2 * CB), BF16), _sds((t, CB), BF16)],
        scratch_shapes=[pltpu.VMEM((HALO + tm, CB), F32)],
        compiler_params=_cp("arbitrary"),
    )(z, z, z, z, z, wp, scale, ws, bs_col)


def even_mix_bwd(dy, z, pooled, wp, scale, ws, bs_col, name, tm=512):
    t = z.shape[0]
    nt = t // tm
    gd = CB // len(POOL_SIZES)

    def body(dya_ref, dyb_ref, ga_ref, u_ref, v_ref, gb_ref, pooled_ref, wp_ref, sc_ref, ws_ref, bs_ref,
             dz_ref, dwp_ref, dsc_ref, dws_ref, dbs_ref, dpext):
        i = pl.program_id(0)
        tile = nt - 1 - i

        @pl.when(i == 0)
        def _():
            dpext[tm:tm + HALO, :] = jnp.zeros((HALO, CB), F32)
            dwp_ref[...] = jnp.zeros_like(dwp_ref)
            dsc_ref[...] = jnp.zeros_like(dsc_ref)
            dws_ref[...] = jnp.zeros_like(dws_ref)
            dbs_ref[...] = jnp.zeros_like(dbs_ref)

        @pl.when(i > 0)
        def _():
            dpext[tm:tm + HALO, :] = dpext[0:HALO, :]

        for g, w in enumerate(POOL_SIZES):
            cols = slice(g * gd, (g + 1) * gd)
            pooled_g = pooled_ref[:, cols]
            mixed = _dot(pooled_g, wp_ref[g])
            silu_a, dsilu_a = _silu_and_grad(ga_ref[:, cols].astype(F32))
            dya = dya_ref[:, cols].astype(F32)
            sc = sc_ref[:, cols]
            dmixed = (dya * sc * silu_a).astype(BF16)
            dsc_ref[:, cols] += jnp.sum(dya * mixed * silu_a, axis=0, keepdims=True)
            dz_ref[:, CB + g * gd:CB + (g + 1) * gd] = (dya * mixed * sc * dsilu_a).astype(BF16)
            dwp_ref[g] += _dot_tn(pooled_g, dmixed)
            dpooled = _dot_nt(dmixed, wp_ref[g])
            dpext[0:tm, cols] = dpooled / _window_counts(tile * tm, tm, w)
            s = dpext[0:tm, cols]
            for k in range(1, w):
                s = s + dpext[k:k + tm, cols]
            dz_ref[:, cols] = (s - dpooled).astype(BF16)
        for g in range(4):
            cols = slice(g * gd, (g + 1) * gd)
            wg, lower = _tril_bf16(ws_ref, g)
            dws_g = jnp.zeros((CHUNK, CHUNK), F32)
            dbs_g = jnp.zeros((CHUNK, 1), F32)
            for c in range(tm // CHUNK):
                rows = slice(c * CHUNK, (c + 1) * CHUNK)
                vb = v_ref[rows, cols]
                m = _dot(wg, vb) + bs_ref[g]
                gbv = gb_ref[rows, cols].astype(F32)
                silu_b, dsilu_b = _silu_and_grad(gbv)
                dyb = dyb_ref[rows, cols].astype(F32)
                uv = u_ref[rows, cols].astype(F32)
                dm = dyb * silu_b * uv
                dmb = dm.astype(BF16)
                dz_ref[rows, 2 * CB + g * gd:2 * CB + (g + 1) * gd] = (dyb * silu_b * m).astype(BF16)
                dz_ref[rows, 3 * CB + g * gd:3 * CB + (g + 1) * gd] = _dot_tn(wg, dmb).astype(BF16)
                dz_ref[rows, 4 * CB + g * gd:4 * CB + (g + 1) * gd] = (dyb * uv * m * dsilu_b).astype(BF16)
                dws_g = dws_g + _dot_nt(dmb, vb)
                dbs_g = dbs_g + jnp.sum(dm, axis=1, keepdims=True)
            dws_ref[g] += jnp.where(lower, dws_g, 0.0)
            dbs_ref[g] += dbs_g

    rev = lambda cb: pl.BlockSpec((tm, CB), lambda i, cb=cb: (nt - 1 - i, cb))
    full = lambda shape: pl.BlockSpec(shape, lambda i: (0,) * len(shape))
    return pl.pallas_call(
        body, name=name, grid=(nt,),
        in_specs=[rev(0), rev(1), rev(1), rev(2), rev(3), rev(4), rev(0),
                  full(wp.shape), full(scale.shape), full(ws.shape), full(bs_col.shape)],
        out_specs=[pl.BlockSpec((tm, 5 * CB), lambda i: (nt - 1 - i, 0)),
                   full(wp.shape), full(scale.shape), full(ws.shape), full(bs_col.shape)],
        out_shape=[_sds((t, 5 * CB), BF16), _sds(wp.shape, F32), _sds(scale.shape, F32),
                   _sds(ws.shape, F32), _sds(bs_col.shape, F32)],
        scratch_shapes=[pltpu.VMEM((tm + HALO, CB), F32)],
        compiler_params=_cp("arbitrary"),
    )(dy, dy, z, z, z, z, pooled, wp, scale, ws, bs_col)


def _slope(group, head):
    return float(2.0 ** (-8.0 * (group * HEADS_PER_GROUP + head + 1) / N_HEADS))


def _band(dilation):
    qi = lax.broadcasted_iota(jnp.int32, (ATTN_BLOCK, ATTN_BLOCK), 0)
    ki = lax.broadcasted_iota(jnp.int32, (ATTN_BLOCK, ATTN_BLOCK), 1)
    dist_prev = ((qi + ATTN_BLOCK - ki) * dilation).astype(F32)
    dist_cur = ((qi - ki) * dilation).astype(F32)
    return dist_prev, ki >= qi, dist_cur, ki <= qi


def _permute_rows(name, d, inverse, arrays, in_specs, out_shapes, out_specs, n_chunks, widths):
    rows = ATTN_BLOCK * d
    t = out_shapes[0].shape[-2]
    n = len(arrays)

    lanes = HEAD_DIM

    def body(*refs):
        for a, o, s, w in zip(refs[:n], refs[n:2 * n], refs[2 * n:], widths):
            for k in range(w // lanes):
                cols = slice(k * lanes, (k + 1) * lanes)
                if inverse:
                    for r in range(d):
                        s[k, pl.ds(r, ATTN_BLOCK, stride=d), :] = (
                            a[r * ATTN_BLOCK:(r + 1) * ATTN_BLOCK, cols].astype(F32))
                    o[:, cols] = s[k].astype(o.dtype)
                else:
                    s[k] = a[:, cols].astype(F32)
                    for r in range(d):
                        o[r * ATTN_BLOCK:(r + 1) * ATTN_BLOCK, cols] = (
                            s[k, pl.ds(r, ATTN_BLOCK, stride=d), :].astype(o.dtype))

    return pl.pallas_call(
        body, name=name, grid=(t // rows, n_chunks),
        in_specs=in_specs, out_specs=out_specs, out_shape=out_shapes,
        scratch_shapes=[pltpu.VMEM((w // lanes, rows, lanes), F32) for w in widths],
        compiler_params=_cp("parallel", "arbitrary"),
    )(*arrays)


def _chunk_width(d):
    return 4 * CB // d


def permute_mats(wide, narrow, group, inverse, name):
    d = DILATIONS[group]
    rows, cw = ATTN_BLOCK * d, _chunk_width(d)
    wide_spec = pl.BlockSpec((rows, cw), lambda i, c: (i, c))
    narrow_spec = pl.BlockSpec((rows, HEAD_DIM), lambda i, c: (i, 0))
    arrays = [wide] + list(narrow)
    specs = [wide_spec] + [narrow_spec] * len(narrow)
    return _permute_rows(name, d, inverse, arrays, specs, [_sds(a.shape, a.dtype) for a in arrays], specs,
                         CB // cw, [cw] + [HEAD_DIM] * len(narrow))


def permute_slabs(s, group, inverse, name):
    d = DILATIONS[group]
    rows, cw = ATTN_BLOCK * d, _chunk_width(d)
    per = CB // cw
    spec = pl.BlockSpec((None, rows, cw), lambda i, c: (c // per, i, c % per))
    (out,) = _permute_rows(name, d, inverse, [s], [spec], [_sds(s.shape, s.dtype)], [spec], 3 * per, [cw])
    return out


def _lane_pack(cols):
    rows = cols[0].shape[0]
    lane = lax.broadcasted_iota(jnp.int32, (rows, HEAD_DIM), 1)
    out = jnp.zeros((rows, HEAD_DIM), F32)
    for h, c in enumerate(cols):
        out = jnp.where(lane == h, c, out)
    return out


def _qkv_col(group, from_z):
    return (lambda which: which * 3 + group) if from_z else (lambda which: which)


SCORE_SCALE = HEAD_DIM ** -0.5
LOG2_E = 1.4426950408889634
EXP2_SCALE = SCORE_SCALE * LOG2_E


def _fill_bias(bias_ref, group, d):
    dist_p, ok_p, dist_c, ok_c = _band(d)
    for h in range(HEADS_PER_GROUP):
        k = -_slope(group, h) / SCORE_SCALE
        bias_ref[h, :, 0:ATTN_BLOCK] = jnp.where(ok_p, k * dist_p, NEG)
        bias_ref[h, :, ATTN_BLOCK:2 * ATTN_BLOCK] = jnp.where(ok_c, k * dist_c, NEG)


STEP_BLOCKS = 4


def _step_plan(d):
    return d < STEP_BLOCKS, max(d // STEP_BLOCKS, 1)


def _joined(prev_ref, cur_ref, j, cols, chained):
    rows = slice(j * ATTN_BLOCK, (j + 1) * ATTN_BLOCK)
    if not chained:
        before = prev_ref[rows, cols]
    elif j == 0:
        before = prev_ref[(STEP_BLOCKS - 1) * ATTN_BLOCK:STEP_BLOCKS * ATTN_BLOCK, cols]
    else:
        before = cur_ref[(j - 1) * ATTN_BLOCK:j * ATTN_BLOCK, cols]
    return jnp.concatenate([before, cur_ref[rows, cols]], axis=0)


def _raw_scores(q, kk, bias_ref, h, has_prev):
    is_cur = lax.broadcasted_iota(jnp.int32, (ATTN_BLOCK, 2 * ATTN_BLOCK), 1) >= ATTN_BLOCK
    return jnp.where(jnp.logical_or(is_cur, has_prev), _dot_nt(q, kk) + bias_ref[h], NEG)


def attn_fwd(qkv, group, name):
    t = qkv.shape[0]
    d = DILATIONS[group]
    sub = STEP_BLOCKS
    chained, back = _step_plan(d)
    token_order = d == sub
    col = _qkv_col(group, qkv.shape[1] != 3 * CB)
    heads = range(HEADS_PER_GROUP)

    def body(q_ref, kp_ref, kc_ref, vp_ref, vc_ref, o_ref, lse_ref, bias_ref, *stage):
        b = pl.program_id(0)

        @pl.when(b == 0)
        def _():
            _fill_bias(bias_ref, group, d)

        hs = [slice(h * HEAD_DIM, (h + 1) * HEAD_DIM) for h in heads]
        for j in range(sub):
            rows = slice(j * ATTN_BLOCK, (j + 1) * ATTN_BLOCK)
            has_prev = True if chained and j > 0 else b >= back
            s = [_raw_scores(q_ref[rows, hs[h]], _joined(kp_ref, kc_ref, j, hs[h], chained), bias_ref, h, has_prev)
                 for h in heads]
            m = [jnp.max(t_, axis=1, keepdims=True) for t_ in s]
            e = [jnp.exp2(EXP2_SCALE * (t_ - m[h])) for h, t_ in enumerate(s)]
            l = [jnp.sum(e_, axis=1, keepdims=True) for e_ in e]
            o = [_dot(e_.astype(BF16), _joined(vp_ref, vc_ref, j, hs[h], chained)) * (1.0 / l[h])
                 for h, e_ in enumerate(e)]
            lse_j = _lane_pack([SCORE_SCALE * m[h] + jnp.log(l[h]) for h in heads])
            if token_order:
                strided = pl.ds(j, ATTN_BLOCK, stride=d)
                for h in heads:
                    stage[0][h, strided, :] = o[h]
                stage[0][HEADS_PER_GROUP, strided, :] = lse_j
            else:
                for h in heads:
                    o_ref[rows, hs[h]] = o[h].astype(BF16)
                lse_ref[rows, :] = lse_j
        if token_order:
            for h in heads:
                o_ref[:, hs[h]] = stage[0][h].astype(BF16)
            lse_ref[...] = stage[0][HEADS_PER_GROUP]

    blk = (sub * ATTN_BLOCK, CB)
    cur = lambda which: pl.BlockSpec(blk, lambda b: (b, col(which)))
    prev = lambda which: pl.BlockSpec(blk, lambda b: (jnp.maximum(b - back, 0), col(which)))
    scratch = [pltpu.VMEM((HEADS_PER_GROUP, ATTN_BLOCK, 2 * ATTN_BLOCK), F32)]
    if token_order:
        scratch.append(pltpu.VMEM((HEADS_PER_GROUP + 1, blk[0], HEAD_DIM), F32))
    return pl.pallas_call(
        body, name=name, grid=(t // blk[0],),
        in_specs=[cur(0), prev(1), cur(1), prev(2), cur(2)],
        out_specs=[pl.BlockSpec(blk, lambda b: (b, 0)), pl.BlockSpec((blk[0], HEAD_DIM), lambda b: (b, 0))],
        out_shape=[_sds((t, CB), BF16), _sds((t, HEAD_DIM), F32)],
        scratch_shapes=scratch,
        compiler_params=_cp("arbitrary"),
    )(qkv, qkv, qkv, qkv, qkv)


def attn_bwd(z, dyc, lse, dd, group, name):
    t = z.shape[0]
    d = DILATIONS[group]
    sub = STEP_BLOCKS
    chained, back = _step_plan(d)
    nb = t // (ATTN_BLOCK * d) // (sub if chained else 1)
    col = _qkv_col(group, z.shape[1] != 3 * CB)
    heads = range(HEADS_PER_GROUP)
    steps = back * nb
    token_order = d == sub

    def body(q_ref, kp_ref, kc_ref, vp_ref, vc_ref, dy_ref, lse_ref, dd_ref, out_ref, carry, bias_ref, *stage):
        s_ = pl.program_id(0)
        n = s_ % nb
        hs = [slice(h * HEAD_DIM, (h + 1) * HEAD_DIM) for h in heads]
        rows = [slice(j * ATTN_BLOCK, (j + 1) * ATTN_BLOCK) for j in range(sub)]

        def put(w, j, h, value):
            if token_order:
                stage[0][w * HEADS_PER_GROUP + h, pl.ds(j, ATTN_BLOCK, stride=d), :] = value
            else:
                out_ref[w, rows[j], hs[h]] = value.astype(BF16)

        def put_done():
            if token_order:
                for w in range(3):
                    for h in heads:
                        out_ref[w, :, hs[h]] = stage[0][w * HEADS_PER_GROUP + h].astype(BF16)

        @pl.when(s_ == 0)
        def _():
            _fill_bias(bias_ref, group, d)
            carry[...] = jnp.zeros_like(carry)

        @pl.when(s_ == steps)
        def _():
            for w in range(3):
                for j in range(sub):
                    for h in heads:
                        put(w, j, h, carry[w, rows[j], hs[h]])
            put_done()

        @pl.when(s_ < steps)
        def _():
            old = [[[carry[w, rows[j], hs[h]] for h in heads] for j in range(sub)] for w in range(3)]
            if token_order:
                for h in heads:
                    stage[1][h] = dy_ref[:, hs[h]].astype(F32)
            grads = []
            first, second = slice(0, ATTN_BLOCK), slice(ATTN_BLOCK, 2 * ATTN_BLOCK)

            def emit(j, grads):
                for h in heads:
                    put(0, j, h, old[0][j][h])
                    carry[0, rows[j], hs[h]] = grads[j][h][0]
                    for w in (1, 2):
                        if not chained:
                            done = old[w][j][h] + grads[j][h][w][first]
                            kept = grads[j][h][w][second]
                        else:
                            done = old[w][j][h] + grads[0][h][w][first] if j == sub - 1 else old[w][j][h]
                            kept = grads[j][h][w][second]
                            if j < sub - 1:
                                kept = kept + grads[j + 1][h][w][first]
                        put(w, j, h, done)
                        carry[w, rows[j], hs[h]] = kept

            for j in range(sub):
                has_prev = True if chained and j > 0 else n > 0
                q = [q_ref[rows[j], hs[h]] for h in heads]
                kk = [_joined(kp_ref, kc_ref, j, hs[h], chained) for h in heads]
                if token_order:
                    strided = pl.ds(j, ATTN_BLOCK, stride=d)
                    dy = [stage[1][h, strided, :].astype(BF16) for h in heads]
                    lse_j, dd_j = lse_ref[strided, :], dd_ref[strided, :]
                else:
                    dy = [dy_ref[rows[j], hs[h]] for h in heads]
                    lse_j, dd_j = lse_ref[rows[j], :], dd_ref[rows[j], :]
                s = [_raw_scores(q[h], kk[h], bias_ref, h, has_prev) for h in heads]
                p = [jnp.exp2(EXP2_SCALE * t_ - lse_j[:, h:h + 1] * LOG2_E) for h, t_ in enumerate(s)]
                ds = [(p[h] * (_dot_nt(dy[h], _joined(vp_ref, vc_ref, j, hs[h], chained)) - dd_j[:, h:h + 1])
                       * SCORE_SCALE).astype(BF16) for h in heads]
                grads.append([(_dot(ds[h], kk[h]), _dot_tn(ds[h], q[h]), _dot_tn(p[h].astype(BF16), dy[h]))
                              for h in heads])
                if not chained:
                    emit(j, grads)
            if chained:
                for j in range(sub):
                    emit(j, grads)
            put_done()

    blk = (sub * ATTN_BLOCK, CB)
    scratch = [pltpu.VMEM((3, blk[0], CB), F32), pltpu.VMEM((HEADS_PER_GROUP, ATTN_BLOCK, 2 * ATTN_BLOCK), F32)]
    if token_order:
        scratch.append(pltpu.VMEM((3 * HEADS_PER_GROUP, blk[0], HEAD_DIM), F32))
        scratch.append(pltpu.VMEM((HEADS_PER_GROUP, blk[0], HEAD_DIM), F32))

    def row_of(s_):
        s_ = jnp.clip(s_, 0, steps - 1)
        return (s_ % nb) * back + s_ // nb

    def prev_row_of(s_):
        s_ = jnp.clip(s_, 0, steps - 1)
        return row_of(s_) - jnp.where(s_ % nb > 0, back, 0)

    cur = lambda which: pl.BlockSpec(blk, lambda s_: (row_of(s_), col(which)))
    prev = lambda which: pl.BlockSpec(blk, lambda s_: (prev_row_of(s_), col(which)))
    narrow = pl.BlockSpec((blk[0], HEAD_DIM), lambda s_: (row_of(s_), 0))
    return pl.pallas_call(
        body, name=name, grid=(steps + 1,),
        in_specs=[cur(0), prev(1), cur(1), prev(2), cur(2), pl.BlockSpec(blk, lambda s_: (row_of(s_), 0)), narrow, narrow],
        out_specs=pl.BlockSpec((3, blk[0], CB), lambda s_: (0, row_of(s_ - 1), 0)),
        out_shape=_sds((3, t, CB), BF16),
        scratch_shapes=scratch,
        compiler_params=_cp("arbitrary"),
    )(z, z, z, z, z, dyc, lse, dd)


def odd_mix(os_, lses, z, conv_w, name, tm=512):
    t = z.shape[0]

    def body(o0, o1, o2, l0, l1, l2, gc_ref, db_ref, dc_ref, dx_ref, gd_ref, cw_ref,
             y_ref, yc_ref, lse_ref, conv_ref, zext):
        i = pl.program_id(0)
        a0, a1, a2 = l0[...], l1[...], l2[...]
        m = jnp.maximum(jnp.maximum(a0, a1), a2)
        tot = m + jnp.log(jnp.exp(a0 - m) + jnp.exp(a1 - m) + jnp.exp(a2 - m))
        lse_ref[...] = tot
        w0, w1, w2 = jnp.exp(a0 - tot), jnp.exp(a1 - tot), jnp.exp(a2 - tot)
        for h in range(HEADS_PER_GROUP):
            hs = slice(h * HEAD_DIM, (h + 1) * HEAD_DIM)
            yc = (w0[:, h:h + 1] * o0[:, hs].astype(F32) + w1[:, h:h + 1] * o1[:, hs].astype(F32)
                  + w2[:, h:h + 1] * o2[:, hs].astype(F32))
            yc_ref[:, hs] = yc.astype(BF16)
            silu_c, _ = _silu_and_grad(gc_ref[:, hs].astype(F32))
            y_ref[:, hs] = (yc * silu_c).astype(BF16)

        @pl.when(i == 0)
        def _():
            zext[0:HALO, :] = jnp.zeros((HALO, CB), F32)

        @pl.when(i > 0)
        def _():
            zext[0:HALO, :] = zext[tm:tm + HALO, :]

        zext[HALO:HALO + tm, :] = dc_ref[...].astype(F32) * dx_ref[...].astype(F32)
        conv = (cw_ref[0:1, :] * zext[HALO - 2:HALO - 2 + tm, :] + cw_ref[1:2, :] * zext[HALO - 1:HALO - 1 + tm, :]
                + cw_ref[2:3, :] * zext[HALO:HALO + tm, :])
        conv_ref[...] = conv.astype(BF16)
        silu_d, _ = _silu_and_grad(gd_ref[...].astype(F32))
        y_ref[:, CB:2 * CB] = (db_ref[...].astype(F32) * conv * silu_d).astype(BF16)

    row = pl.BlockSpec((tm, CB), lambda i: (i, 0))
    narrow = pl.BlockSpec((tm, HEAD_DIM), lambda i: (i, 0))
    zspec = lambda cb: pl.BlockSpec((tm, CB), lambda i, cb=cb: (i, cb))
    return pl.pallas_call(
        body, name=name, grid=(t // tm,),
        in_specs=[row] * 3 + [narrow] * 3 + [zspec(9), zspec(10), zspec(11), zspec(12), zspec(13),
                                             pl.BlockSpec(conv_w.shape, lambda i: (0, 0))],
        out_specs=[pl.BlockSpec((tm, 2 * CB), lambda i: (i, 0)), row, narrow, row],
        out_shape=[_sds((t, 2 * CB), BF16), _sds((t, CB), BF16), _sds((t, HEAD_DIM), F32), _sds((t, CB), BF16)],
        scratch_shapes=[pltpu.VMEM((HALO + tm, CB), F32)],
        compiler_params=_cp("arbitrary"),
    )(*os_, *lses, z, z, z, z, z, conv_w)


def odd_mix_bwd(dy, yc, conv, z, conv_w, name, tm=256):
    t = z.shape[0]
    nt = t // tm

    def body(dyc_ref, dyd_ref, yc_ref, conv_ref, gc_ref, db_ref, dc_ref, dx_ref, gd_ref, cw_ref,
             dz_ref, dyo_ref, dd_ref, dcw_ref, dcext):
        i = pl.program_id(0)

        @pl.when(i == 0)
        def _():
            dcext[tm:tm + HALO, :] = jnp.zeros((HALO, CB), F32)
            dcw_ref[...] = jnp.zeros_like(dcw_ref)

        @pl.when(i > 0)
        def _():
            dcext[tm:tm + HALO, :] = dcext[0:HALO, :]

        silu_c, dsilu_c = _silu_and_grad(gc_ref[...].astype(F32))
        dyc = dyc_ref[...].astype(F32)
        ycv = yc_ref[...].astype(F32)
        dyo = dyc * silu_c
        dyo_ref[...] = dyo.astype(BF16)
        dz_ref[:, 0:CB] = (dyc * ycv * dsilu_c).astype(BF16)
        prod = dyo * ycv
        dd_ref[...] = _lane_pack([jnp.sum(prod[:, h * HEAD_DIM:(h + 1) * HEAD_DIM], axis=1, keepdims=True)
                                  for h in range(HEADS_PER_GROUP)])

        silu_d, dsilu_d = _silu_and_grad(gd_ref[...].astype(F32))
        dyd = dyd_ref[...].astype(F32)
        convv = conv_ref[...].astype(F32)
        dbv = db_ref[...].astype(F32)
        dz_ref[:, CB:2 * CB] = (dyd * convv * silu_d).astype(BF16)
        dz_ref[:, 4 * CB:5 * CB] = (dyd * dbv * convv * dsilu_d).astype(BF16)
        dcext[0:tm, :] = dyd * dbv * silu_d
        dcv, dxv = dc_ref[...].astype(F32), dx_ref[...].astype(F32)
        zc = dcv * dxv
        d0, d1, d2 = dcext[0:tm, :], dcext[1:1 + tm, :], dcext[2:2 + tm, :]
        dzc = cw_ref[2:3, :] * d0 + cw_ref[1:2, :] * d1 + cw_ref[0:1, :] * d2
        dz_ref[:, 2 * CB:3 * CB] = (dzc * dxv).astype(BF16)
        dz_ref[:, 3 * CB:4 * CB] = (dzc * dcv).astype(BF16)
        dcw_ref[0:1, :] += jnp.sum(zc * d2, axis=0, keepdims=True)
        dcw_ref[1:2, :] += jnp.sum(zc * d1, axis=0, keepdims=True)
        dcw_ref[2:3, :] += jnp.sum(zc * d0, axis=0, keepdims=True)

    rev = lambda cb: pl.BlockSpec((tm, CB), lambda i, cb=cb: (nt - 1 - i, cb))
    return pl.pallas_call(
        body, name=name, grid=(nt,),
        in_specs=[rev(0), rev(1), rev(0), rev(0), rev(9), rev(10), rev(11), rev(12), rev(13),
                  pl.BlockSpec(conv_w.shape, lambda i: (0, 0))],
        out_specs=[pl.BlockSpec((tm, 5 * CB), lambda i: (nt - 1 - i, 0)), rev(0),
                   pl.BlockSpec((tm, HEAD_DIM), lambda i: (nt - 1 - i, 0)),
                   pl.BlockSpec(conv_w.shape, lambda i: (0, 0))],
        out_shape=[_sds((t, 5 * CB), BF16), _sds((t, CB), BF16), _sds((t, HEAD_DIM), F32), _sds(conv_w.shape, F32)],
        scratch_shapes=[pltpu.VMEM((tm + HALO, CB), F32)],
        compiler_params=_cp("arbitrary"),
    )(dy, dy, yc, conv, z, z, z, z, z, conv_w)


def out_proj_final_loss(y, w, x, g, target, name, tm=1024, rows=256):
    t, d = x.shape
    k = y.shape[1]

    def body(y_ref, w_ref, x_ref, g_ref, t_ref, dx_ref, loss_ref, dg_ref):
        i = pl.program_id(0)
        dx_ref[...] = x_ref[...] + _dot(y_ref[...], w_ref[...])
        gv = g_ref[...]
        loss = jnp.zeros((1, 1), F32)
        dg = jnp.zeros((1, d), F32)
        for c in range(tm // rows):
            rs = slice(c * rows, (c + 1) * rows)
            xv = dx_ref[rs, :]
            r = lax.rsqrt(jnp.mean(xv * xv, axis=-1, keepdims=True) + EPS)
            xn = xv * r
            err = xn * gv - t_ref[rs, :]
            loss = loss + 0.5 * jnp.sum(jnp.mean(err * err, axis=-1, keepdims=True), axis=0, keepdims=True)
            dy = err * (1.0 / d)
            dyg = dy * gv
            dx_ref[rs, :] = r * (dyg - xn * jnp.mean(dyg * xn, axis=-1, keepdims=True))
            dg = dg + jnp.sum(dy * xn, axis=0, keepdims=True)

        @pl.when(i == 0)
        def _():
            loss_ref[...] = jnp.broadcast_to(loss, loss_ref.shape)
            dg_ref[...] = dg

        @pl.when(i > 0)
        def _():
            loss_ref[...] += jnp.broadcast_to(loss, loss_ref.shape)
            dg_ref[...] += dg

    row = pl.BlockSpec((tm, d), lambda i: (i, 0))
    return pl.pallas_call(
        body, name=name, grid=(t // tm,),
        in_specs=[pl.BlockSpec((tm, k), lambda i: (i, 0)), pl.BlockSpec((k, d), lambda i: (0, 0)), row,
                  pl.BlockSpec((1, d), lambda i: (0, 0)), row],
        out_specs=[row, pl.BlockSpec((8, 128), lambda i: (0, 0)), pl.BlockSpec((1, d), lambda i: (0, 0))],
        out_shape=[_sds((t, d), F32), _sds((8, 128), F32), _sds((1, d), F32)],
        compiler_params=_cp("arbitrary"),
    )(y, w, x, g, target)


def _position():
    x, y, c = lax.axis_index("x"), lax.axis_index("y"), lax.axis_index("c")
    return x, y, c, 4 * x + 2 * y + c


def _peer(x, y, c, k):
    px = 1 - x if k & 4 else x
    py = 1 - y if k & 2 else y
    pc = 1 - c if k & 1 else c
    return (px, py, pc), 4 * px + 2 * py + pc


def _block(ref, axis, size, idx):
    index = [slice(None)] * len(ref.shape)
    index[axis] = pl.ds(idx * size, size)
    return ref.at[tuple(index)]


def _other_chips(x, y):
    return [(1 - x, y), (x, 1 - y), (1 - x, 1 - y)]


def _gather_copy(ins, lands, axes, send_sems, recv_sems, k, p, block, to, from_shard=False):
    dst = _block(lands[p], axes[p], lands[p].shape[axes[p]] // N_DEV, block)
    sem = k * len(lands) + p
    return pltpu.make_async_remote_copy(
        src_ref=ins[p] if from_shard else dst, dst_ref=dst, send_sem=send_sems.at[sem], recv_sem=recv_sems.at[sem],
        device_id=to, device_id_type=MESH_ID)


def _own_block_copy(ins, lands, axes, sems, p, me):
    n = len(lands)
    dst = _block(lands[p], axes[p], lands[p].shape[axes[p]] // N_DEV, me)
    return pltpu.make_async_copy(ins[p], dst, sems.at[4 * n + p])


def all_gather_two_level(shards, axes, name):
    n = len(shards)
    fulls = []
    for s, ax in zip(shards, axes):
        shape = list(s.shape)
        shape[ax] *= N_DEV
        fulls.append(_sds(tuple(shape), s.dtype))

    def body(*refs):
        ins, outs = refs[:n], refs[n:2 * n]
        send_sems, recv_sems, local_sems = refs[2 * n:]
        x, y, c, me = _position()
        sibling, here = (x, y, 1 - c), (x, y, c)
        chips = _other_chips(x, y)
        copy = functools.partial(_gather_copy, ins, outs, axes, send_sems, recv_sems)
        local = [pltpu.make_async_copy(ins[p], _block(outs[p], axes[p], ins[p].shape[axes[p]], me), local_sems.at[p])
                 for p in range(n)]
        sent = []
        for p in range(n):
            sent.append(copy(0, p, me, sibling, from_shard=True))
            sent += [copy(1 + j, p, me, (*chip, c), from_shard=True) for j, chip in enumerate(chips)]
        for cp in local + sent:
            cp.start()
        for j, (px, py) in enumerate(chips):
            for p in range(n):
                arrived = 4 * px + 2 * py + c
                copy(1 + j, p, arrived, here).wait_recv()
                sent.append(copy(4 + j, p, arrived, sibling))
                sent[-1].start()
        for p in range(n):
            copy(0, p, 4 * x + 2 * y + 1 - c, here).wait_recv()
            for j, (px, py) in enumerate(chips):
                copy(4 + j, p, 4 * px + 2 * py + 1 - c, here).wait_recv()
        for cp in sent:
            cp.wait_send()
        for cp in local:
            cp.wait()

    any_spec = pl.BlockSpec(memory_space=pl.ANY)
    return pl.pallas_call(
        body, name=name,
        in_specs=[any_spec] * n, out_specs=[any_spec] * n, out_shape=fulls,
        scratch_shapes=[pltpu.SemaphoreType.DMA((7 * n,)), pltpu.SemaphoreType.DMA((7 * n,)), pltpu.SemaphoreType.DMA((n,))],
    )(*shards)


def gather_two_level_start(shards, axes, after, name):
    n = len(shards)
    lands = []
    for s, ax in zip(shards, axes):
        shape = list(s.shape)
        shape[ax] *= N_DEV
        lands.append(lax.empty(tuple(shape), s.dtype))

    def body(*refs):
        ins, land = refs[:n], refs[n:2 * n]
        send_sems, recv_sems, token = refs[2 * n + 1], refs[2 * n + 2], refs[-1]
        x, y, c, me = _position()
        copy = functools.partial(_gather_copy, ins, land, axes, send_sems, recv_sems)
        for p in range(n):
            copy(0, p, me, (x, y, 1 - c), from_shard=True).start()
            for j, chip in enumerate(_other_chips(x, y)):
                copy(1 + j, p, me, (*chip, c), from_shard=True).start()
            _own_block_copy(ins, land, axes, send_sems, p, me).start()
        token[...] = jnp.zeros_like(token)

    outs = pl.pallas_call(
        body, name=name,
        out_shape=(pltpu.SemaphoreType.DMA((5 * n,)), pltpu.SemaphoreType.DMA((4 * n,)),
                   *[pltpu.HBM(s.shape, s.dtype) for s in shards], *[pltpu.HBM(l.shape, l.dtype) for l in lands],
                   _sds((8, 128), F32)),
        in_specs=[_HBM] * (2 * n) + [pl.BlockSpec(memory_space=pl.ANY)],
        out_specs=(_SEM, _SEM, *[_HBM] * (2 * n), pl.BlockSpec(memory_space=pltpu.VMEM)),
        input_output_aliases={i: 2 + i for i in range(2 * n)},
        compiler_params=pltpu.CompilerParams(has_side_effects=_EFFECT),
    )(*[_in_hbm(s) for s in shards], *[_in_hbm(l) for l in lands], after)
    return outs[0], outs[1], list(outs[2:2 + n]), list(outs[2 + n:2 + 2 * n]), outs[-1]


def gather_two_level_forward(started, axes, after, name):
    send_a, recv_a, shards, lands, _ = started
    n = len(shards)

    def body(*refs):
        land = refs[:n]
        send_first, recv_first = refs[n], refs[n + 1]
        send_fwd, recv_fwd, token = refs[n + 3], refs[n + 4], refs[-1]
        x, y, c, _ = _position()
        for j, (px, py) in enumerate(_other_chips(x, y)):
            for p in range(n):
                arrived = 4 * px + 2 * py + c
                _gather_copy(None, land, axes, send_first, recv_first, 1 + j, p, arrived, (x, y, c)).wait_recv()
                _gather_copy(None, land, axes, send_fwd, recv_fwd, j, p, arrived, (x, y, 1 - c)).start()
        token[...] = jnp.zeros_like(token)

    outs = pl.pallas_call(
        body, name=name,
        out_shape=(pltpu.SemaphoreType.DMA((3 * n,)), pltpu.SemaphoreType.DMA((3 * n,)),
                   *[pltpu.HBM(l.shape, l.dtype) for l in lands], _sds((8, 128), F32)),
        in_specs=[_HBM] * n + [_SEM, _SEM, pl.BlockSpec(memory_space=pl.ANY)],
        out_specs=(_SEM, _SEM, *[_HBM] * n, pl.BlockSpec(memory_space=pltpu.VMEM)),
        input_output_aliases={i: 2 + i for i in range(n)},
        compiler_params=pltpu.CompilerParams(has_side_effects=_EFFECT),
    )(*lands, send_a, recv_a, after)
    return send_a, recv_a, outs[0], outs[1], shards, list(outs[2:2 + n]), outs[-1]


def gather_two_level_wait(forwarded, axes, after, name):
    send_a, recv_a, send_f, recv_f, shards, lands, _ = forwarded
    n = len(shards)

    def body(*refs):
        ins, land = refs[:n], refs[n:2 * n]
        sa, ra, sf, rf = refs[2 * n:2 * n + 4]
        x, y, c, me = _position()
        here = (x, y, c)
        chips = _other_chips(x, y)
        for p in range(n):
            for k in range(4):
                _gather_copy(ins, land, axes, sa, ra, k, p, me, here, from_shard=True).wait_send()
            _own_block_copy(ins, land, axes, sa, p, me).wait()
            _gather_copy(ins, land, axes, sa, ra, 0, p, 4 * x + 2 * y + 1 - c, here).wait_recv()
            for j, (px, py) in enumerate(chips):
                _gather_copy(ins, land, axes, sf, rf, j, p, 4 * px + 2 * py + c, here).wait_send()
                _gather_copy(ins, land, axes, sf, rf, j, p, 4 * px + 2 * py + 1 - c, here).wait_recv()

    outs = pl.pallas_call(
        body, name=name,
        out_shape=(*[pltpu.HBM(s.shape, s.dtype) for s in shards], *[pltpu.HBM(l.shape, l.dtype) for l in lands]),
        in_specs=[_HBM] * (2 * n) + [_SEM] * 4 + [pl.BlockSpec(memory_space=pl.ANY)],
        out_specs=tuple([_HBM] * (2 * n)),
        input_output_aliases={i: i for i in range(2 * n)},
        compiler_params=pltpu.CompilerParams(has_side_effects=_EFFECT),
    )(*shards, *lands, send_a, recv_a, send_f, recv_f, after)
    return list(outs[n:])


_HBM = pl.BlockSpec(memory_space=pltpu.HBM)
_SEM = pl.BlockSpec(memory_space=pltpu.SEMAPHORE)
_EFFECT = pltpu.SideEffectType.DATAFLOW_SIDE_EFFECTING


def _in_hbm(a):
    return pltpu.with_memory_space_constraint(a, pltpu.HBM)


def _landing(mode, src, axis):
    me = 4 * lax.axis_index("x") + 2 * lax.axis_index("y") + lax.axis_index("c")
    if mode == "gather":
        shape = list(src.shape)
        size = shape[axis]
        shape[axis] *= N_DEV
        return lax.dynamic_update_slice_in_dim(lax.empty(tuple(shape), src.dtype), src, me * size, axis)
    size = src.shape[axis] // N_DEV
    own = lax.dynamic_slice_in_dim(src, me * size, size, axis)
    return lax.dynamic_update_slice_in_dim(lax.empty((N_DEV, *own.shape), src.dtype), own[None], me, 0)


def _seven(mode, land_ref, axis):
    if mode == "gather":
        return _block(land_ref, axis, (N_DEV - 1) * (land_ref.shape[axis] // N_DEV), 0)
    return land_ref.at[pl.ds(0, N_DEV - 1)]


def exchange_start(mode, srcs, axes, after, name):
    n = len(srcs)
    lands = [_landing(mode, s, ax) for s, ax in zip(srcs, axes)]

    def body(*refs):
        src, land = refs[:n], refs[n:2 * n]
        send_sems, recv_sems = refs[2 * n + 1], refs[2 * n + 2]
        token = refs[-1]
        x, y, c, me = _position()
        for k in range(1, N_DEV):
            peer, pj = _peer(x, y, c, k)
            for p in range(n):
                if mode == "gather":
                    s = src[p]
                    dst = _block(land[p], axes[p], src[p].shape[axes[p]], me)
                else:
                    s = _block(src[p], axes[p], src[p].shape[axes[p]] // N_DEV, pj)
                    dst = land[p].at[me]
                pltpu.make_async_remote_copy(
                    src_ref=s, dst_ref=dst, send_sem=send_sems.at[p], recv_sem=recv_sems.at[p],
                    device_id=peer, device_id_type=MESH_ID).start()
        token[...] = jnp.zeros_like(token)

    outs = pl.pallas_call(
        body, name=name,
        out_shape=(pltpu.SemaphoreType.DMA((n,)), pltpu.SemaphoreType.DMA((n,)),
                   *[pltpu.HBM(s.shape, s.dtype) for s in srcs], *[pltpu.HBM(l.shape, l.dtype) for l in lands],
                   _sds((8, 128), F32)),
        in_specs=[_HBM] * (2 * n) + [pl.BlockSpec(memory_space=pl.ANY)],
        out_specs=(_SEM, _SEM, *[_HBM] * (2 * n), pl.BlockSpec(memory_space=pltpu.VMEM)),
        input_output_aliases={i: 2 + i for i in range(2 * n)},
        compiler_params=pltpu.CompilerParams(has_side_effects=_EFFECT),
    )(*[_in_hbm(s) for s in srcs], *[_in_hbm(l) for l in lands], after)
    return outs[0], outs[1], list(outs[2:2 + n]), list(outs[2 + n:2 + 2 * n]), outs[-1]


def exchange_wait(mode, started, axes, after, name):
    send_sems, recv_sems, srcs, lands, _ = started
    n = len(srcs)

    def body(*refs):
        land = refs[n:2 * n]
        send_ref, recv_ref = refs[2 * n], refs[2 * n + 1]
        x, y, c, _ = _position()
        for p in range(n):
            seven = _seven(mode, land[p], axes[p])
            cp = pltpu.make_async_remote_copy(
                src_ref=seven, dst_ref=seven, send_sem=send_ref.at[p], recv_sem=recv_ref.at[p],
                device_id=(x, y, c), device_id_type=MESH_ID)
            cp.wait_send()
            cp.wait_recv()

    outs = pl.pallas_call(
        body, name=name,
        out_shape=(*[pltpu.HBM(s.shape, s.dtype) for s in srcs], *[pltpu.HBM(l.shape, l.dtype) for l in lands]),
        in_specs=[_HBM] * (2 * n) + [_SEM, _SEM, pl.BlockSpec(memory_space=pl.ANY)],
        out_specs=tuple([_HBM] * (2 * n)),
        input_output_aliases={i: i for i in range(2 * n)},
        compiler_params=pltpu.CompilerParams(has_side_effects=_EFFECT),
    )(*srcs, *lands, send_sems, recv_sems, after)
    return list(outs[n:])


def _adam_math(g, w, m, v):
    m2 = ADAM_B1 * m + (1.0 - ADAM_B1) * g
    v2 = ADAM_B2 * v + (1.0 - ADAM_B2) * (g * g)
    m_hat = m2 / (1.0 - ADAM_B1 ** ADAM_STEP)
    v_hat = v2 / (1.0 - ADAM_B2 ** ADAM_STEP)
    delta = -ADAM_LR * (m_hat / (jnp.sqrt(v_hat) + ADAM_EPS) + ADAM_WD * w)
    return delta, m2, v2


def adamw(g, w, m, v, name, slots, tr=128):
    r, c = w.shape
    tr = min(tr, r)

    def body(g_ref, w_ref, m_ref, v_ref, go_ref, d_ref, mo_ref, vo_ref):
        if slots:
            gs = g_ref[0].astype(F32)
            for s in range(1, N_DEV):
                gs = gs + g_ref[s].astype(F32)
        else:
            gs = g_ref[...]
        go_ref[...] = gs
        d_ref[...], mo_ref[...], vo_ref[...] = _adam_math(gs, w_ref[...], m_ref[...], v_ref[...])

    row = pl.BlockSpec((tr, c), lambda i: (i, 0))
    gspec = pl.BlockSpec((N_DEV, tr, c), lambda i: (0, i, 0)) if slots else row
    return pl.pallas_call(
        body, name=name, grid=(r // tr,),
        in_specs=[gspec, row, row, row], out_specs=[row] * 4, out_shape=[_sds((r, c), F32)] * 4,
        compiler_params=_cp("parallel"),
    )(g, w, m, v)


def sum_slots(g, name):
    _, r, c = g.shape

    def body(g_ref, o_ref):
        gs = g_ref[0]
        for s in range(1, N_DEV):
            gs = gs + g_ref[s]
        o_ref[...] = gs

    return pl.pallas_call(
        body, name=name, grid=(1,),
        in_specs=[pl.BlockSpec((N_DEV, r, c), lambda i: (0, 0, 0))],
        out_specs=pl.BlockSpec((r, c), lambda i: (0, 0)), out_shape=_sds((r, c), F32),
        compiler_params=_cp("arbitrary"),
    )(g)


def _rows128(a, pad_to=8):
    a = a.reshape(-1, 128)
    pad = (-a.shape[0]) % pad_to
    return jnp.pad(a, ((0, pad), (0, 0))) if pad else a


def kernel(x, even_norm, even_w_in, even_pool_w, even_pool_scale, even_ws, even_bs, even_w_out, odd_norm, odd_w_in, odd_conv_w, odd_w_out, final_norm, loss_target, m_even_norm, m_even_w_in, m_even_pool_w, m_even_pool_scale, m_even_ws, m_even_bs, m_even_w_out, m_odd_norm, m_odd_w_in, m_odd_conv_w, m_odd_w_out, m_final_norm, v_even_norm, v_even_w_in, v_even_pool_w, v_even_pool_scale, v_even_ws, v_even_bs, v_even_w_out, v_odd_norm, v_odd_w_in, v_odd_conv_w, v_odd_w_out, v_final_norm):
    x0 = x[0]
    target = loss_target[0]
    me = 4 * lax.axis_index("x") + 2 * lax.axis_index("y") + lax.axis_index("c")

    (we_in,) = all_gather_two_level([even_w_in[0].astype(BF16)], [1], "gather_even_in")
    odd_small = jnp.pad(odd_norm, ((0, 7), (0, 0))) + jnp.pad(odd_conv_w[0], ((1, 4), (0, 0)))
    rest_axes, odd_axes = [1, 0, 1], [1, 0]
    rest_flight = exchange_start("gather", [even_pool_w[0].astype(BF16), even_w_out[0].astype(BF16), odd_small],
                                 rest_axes, we_in, "gather_even_rest_start")
    odd_flight = gather_two_level_start([odd_w_in[0].astype(BF16), odd_w_out[0].astype(BF16)],
                                        odd_axes, rest_flight[4], "gather_odd_start")
    scale = even_pool_scale
    ws = even_ws[0]
    bs_col = even_bs[0][:, :, None]
    g_fin = final_norm[None, :]

    z_e, h_e_t, we_in_t = norm_matmul(x0, even_norm + odd_flight[4][0:1, 0:1], we_in, "even_in", tn=1280)
    wp, we_out, odd_small = exchange_wait("gather", rest_flight, rest_axes, z_e, "gather_even_rest_wait")
    g_odd, conv_w = odd_small[0:1], odd_small[1:4]
    y_e, pooled = even_mix(z_e, wp, scale, ws, bs_col, "even_mix")
    x1, h_o, h_o_t = matmul_residual(y_e, we_out, x0, "even_out", next_gain=g_odd)
    odd_flight = gather_two_level_forward(odd_flight, odd_axes, x1, "gather_odd_forward")
    h_layouts = [h_o] + [permute_mats(h_o, [], gi, False, f"dilate_h_{gi}")[0] for gi in (1, 2)]
    wo_in, wo_out = gather_two_level_wait(odd_flight, odd_axes, h_layouts[2], "gather_odd_wait")
    z_o, wo_in_t = matmul_layouts(h_layouts, wo_in, "odd_in")
    attn = [attn_fwd(z_o, gi, f"attn_fwd_{gi}") for gi in range(3)]
    attn[2] = permute_mats(attn[2][0], [attn[2][1]], 2, True, "undilate_attn_2")
    y_o, yc, lse_tot, conv = odd_mix([a[0] for a in attn], [a[1] for a in attn], z_o, conv_w, "odd_mix")
    dx2, loss_blk, dg_fin = out_proj_final_loss(y_o, wo_out, x1, g_fin, target, "odd_out_loss")

    dy_o, dwo_out = out_proj_bwd(dx2, wo_out.T, y_o, "odd_out_bwd")
    dz5_o, dyc, dd, dconv_w = odd_mix_bwd(dy_o, yc, conv, z_o, conv_w, "odd_mix_bwd")
    dqkv = [attn_bwd(z_o, dyc, lse_tot, dd, 0, "attn_bwd_0")]
    dqkv.append(attn_bwd(z_o, dyc, lse_tot, dd, 1, "attn_bwd_1"))
    dyc_2, lse_2, dd_2 = permute_mats(dyc, [lse_tot, dd], 2, False, "dilate_dy_2")
    dqkv.append(permute_slabs(attn_bwd(z_o, dyc_2, lse_2, dd_2, 2, "attn_bwd_2"), 2, True, "undilate_dqkv_2"))
    src_o = [(dqkv[0], ("slabs", 0)), (dqkv[1], ("slabs", 1)), (dqkv[2], ("slabs", 2)), (dz5_o, ("cols", 9, 5))]
    dwo_in = in_proj_dw(h_o_t, src_o, ODD_COLBLOCKS, "odd_in_dw")
    odd_grads = exchange_start("scatter", [dwo_in, dwo_out], [1, 0], dconv_w, "scatter_odd_start")
    dx1, dg_odd = in_proj_dx(src_o, ODD_COLBLOCKS, wo_in_t, x1, g_odd + odd_grads[4][0:1, 0:1], dx2, "odd_in_dx")

    dy_e, dwe_out = out_proj_bwd(dx1, we_out.T, y_e, "even_out_bwd")
    dz_e, dwp, dscale, dws, dbs_col = even_mix_bwd(dy_e, z_e, pooled, wp, scale, ws, bs_col, "even_mix_bwd")
    src_e = [(dz_e, ("cols", 0, 5))]
    even_grads_a = exchange_start("scatter", [dwp.astype(BF16), dwe_out], [1, 0], dscale, "scatter_even_rest_start")
    dwe_in = in_proj_dw(h_e_t, src_e, 5, "even_in_dw", tk=2048)
    even_grads_b = exchange_start("scatter", [dwe_in], [1], even_grads_a[4], "scatter_even_in_start")
    dx0, dg_even = in_proj_dx(src_e, 5, we_in_t, x0, even_norm + even_grads_b[4][0:1, 0:1], dx1, "even_in_dx")
    small = jnp.concatenate([
        _rows128(dg_even), _rows128(dscale), _rows128(dws), _rows128(dbs_col), _rows128(dg_fin),
        _rows128(dg_odd), _rows128(dconv_w), loss_blk], axis=0)
    n_small = small.shape[0]
    small_flight = exchange_start("gather", [small], [0], dg_fin, "gather_small_start")
    p_wo_in, p_wo_out = exchange_wait("scatter", odd_grads, [1, 0], small_flight[4], "scatter_odd_wait")
    p_wp, p_we_out = exchange_wait("scatter", even_grads_a, [1, 0], small_flight[4], "scatter_even_rest_wait")
    (p_we_in,) = exchange_wait("scatter", even_grads_b, [1], small_flight[4], "scatter_even_in_wait")

    two_d = lambda a, r, c: a.reshape(r, c)
    sharded = [
        ("even_w_in", p_we_in, even_w_in, m_even_w_in, v_even_w_in, (1024, 640)),
        ("even_pool_w", p_wp.reshape(N_DEV, 128, 256), even_pool_w, m_even_pool_w, v_even_pool_w, (128, 256)),
        ("even_w_out", p_we_out, even_w_out, m_even_w_out, v_even_w_out, (256, 1024)),
        ("odd_w_in", p_wo_in, odd_w_in, m_odd_w_in, v_odd_w_in, (1024, 1792)),
        ("odd_w_out", p_wo_out, odd_w_out, m_odd_w_out, v_odd_w_out, (256, 1024)),
    ]
    res = {}
    for name, g, w, m, v, (r, c) in sharded:
        outs = adamw(g, two_d(w, r, c), two_d(m, r, c), two_d(v, r, c), "adamw_" + name, slots=True)
        res[name] = [o.reshape(w.shape) for o in outs]

    (small_all,) = exchange_wait("gather", small_flight, [0], res["odd_w_out"][0], "gather_small_wait")
    small_sum = sum_slots(small_all.reshape(N_DEV, n_small, 128), "sum_small_grads")
    g_even_norm = small_sum[0:8].reshape(1, 1024)
    g_scale = small_sum[8:16].reshape(1, 1024)
    g_ws = small_sum[16:528]
    g_bs = small_sum[528:532]
    g_final = small_sum[536:544].reshape(1, 1024)
    g_odd_norm = lax.dynamic_slice_in_dim(small_sum[544:552], me, 1, axis=0)
    g_conv = lax.dynamic_index_in_dim(small_sum[552:576].reshape(3, 8, 128), me, axis=1, keepdims=False)
    plain = [
        ("even_norm", g_even_norm, even_norm, m_even_norm, v_even_norm, (1, 1024)),
        ("even_pool_scale", g_scale, even_pool_scale, m_even_pool_scale, v_even_pool_scale, (1, 1024)),
        ("even_ws", g_ws, even_ws, m_even_ws, v_even_ws, (512, 128)),
        ("even_bs", g_bs, even_bs, m_even_bs, v_even_bs, (4, 128)),
        ("odd_norm", g_odd_norm, odd_norm, m_odd_norm, v_odd_norm, (1, 128)),
        ("odd_conv_w", g_conv, odd_conv_w, m_odd_conv_w, v_odd_conv_w, (3, 128)),
        ("final_norm", g_final, final_norm, m_final_norm, v_final_norm, (1, 1024)),
    ]
    for name, g, w, m, v, (r, c) in plain:
        outs = adamw(two_d(g, r, c), two_d(w, r, c), two_d(m, r, c), two_d(v, r, c), "adamw_" + name, slots=False)
        res[name] = [o.reshape(w.shape) for o in outs]

    loss = small_sum[576, 0]
    order = ["even_norm", "even_w_in", "even_pool_w", "even_pool_scale", "even_ws", "even_bs", "even_w_out",
             "odd_norm", "odd_w_in", "odd_conv_w", "odd_w_out", "final_norm"]
    return (loss, dx0[None], *[res[n][0] for n in order], *[res[n][1] for n in order],
            *[res[n][2] for n in order], *[res[n][3] for n in order])
```

```python
import functools

import jax
import jax.numpy as jnp
from jax import lax
from jax.experimental import pallas as pl
from jax.experimental.pallas import tpu as pltpu

F32 = jnp.float32
BF16 = jnp.bfloat16
MESH_ID = pl.DeviceIdType.MESH

EPS = 1e-6
NEG = -1e30
N_DEV = 8
POOL_SIZES = (2, 4, 8, 16)
DILATIONS = (1, 4, 16)
N_HEADS = 24
HEADS_PER_GROUP = 8
HEAD_DIM = 128
ATTN_BLOCK = 128
CHUNK = 128
CB = 1024
HALO = 16
ODD_COLBLOCKS = 14
ADAM_LR = 0.001
ADAM_B1 = 0.9
ADAM_B2 = 0.999
ADAM_EPS = 1e-08
ADAM_WD = 0.01
ADAM_STEP = 10
VMEM_LIMIT = 52 * 1024 * 1024


def _cp(*sem):
    return pltpu.CompilerParams(dimension_semantics=sem, vmem_limit_bytes=VMEM_LIMIT)


def _dot(a, b):
    return jnp.dot(a, b, preferred_element_type=F32)


def _dot_nt(a, b):
    return lax.dot_general(a, b, (((1,), (1,)), ((), ())), preferred_element_type=F32)


def _dot_tn(a, b):
    return lax.dot_general(a, b, (((0,), (0,)), ((), ())), preferred_element_type=F32)


def _sigmoid(x):
    return 0.5 * jnp.tanh(0.5 * x) + 0.5


def _silu_and_grad(x):
    s = _sigmoid(x)
    return x * s, s * (1.0 + x * (1.0 - s))


def _sds(shape, dtype):
    return jax.ShapeDtypeStruct(shape, dtype)


def norm_matmul(x, g, w, name, tn, tm=1024, rows=256):
    t, d = x.shape
    n = w.shape[1]
    n_col = n // tn

    def body(x_ref, g_ref, w_ref, z_ref, ht_ref, wt_ref, h_ref):
        @pl.when(pl.program_id(1) == 0)
        def _():
            for c in range(tm // rows):
                rs = slice(c * rows, (c + 1) * rows)
                xv = x_ref[rs, :]
                r = lax.rsqrt(jnp.mean(xv * xv, axis=-1, keepdims=True) + EPS)
                h_ref[rs, :] = ((xv * r) * g_ref[...]).astype(BF16)
            ht_ref[...] = h_ref[...].T

        z_ref[...] = _dot(h_ref[...], w_ref[...]).astype(BF16)

        @pl.when(pl.program_id(0) == 0)
        def _():
            wt_ref[...] = w_ref[...].T

    return pl.pallas_call(
        body, name=name, grid=(t // tm, n_col),
        in_specs=[pl.BlockSpec((tm, d), lambda i, j: (i, 0)),
                  pl.BlockSpec((1, d), lambda i, j: (0, 0)),
                  pl.BlockSpec((d, tn), lambda i, j: (0, j))],
        out_specs=[pl.BlockSpec((tm, tn), lambda i, j: (i, j)),
                   pl.BlockSpec((d, tm), lambda i, j: (0, i)),
                   pl.BlockSpec((tn, d), lambda i, j: (jnp.where(i == 0, j, n_col), 0))],
        out_shape=[_sds((t, n), BF16), _sds((d, t), BF16), _sds((n + tn, d), BF16)],
        scratch_shapes=[pltpu.VMEM((tm, d), BF16)],
        compiler_params=_cp("arbitrary", "arbitrary"),
    )(x, g, w)


def matmul_residual(y, w, x, name, next_gain=None, tm=1024, rows=256):
    t, k = y.shape
    d = w.shape[1]
    dil = STEP_BLOCKS
    run = ATTN_BLOCK * dil

    def body(y_ref, w_ref, x_ref, *rest):
        o_ref = rest[-1] if next_gain is None else rest[1]
        o_ref[...] = x_ref[...] + _dot(y_ref[...], w_ref[...])
        if next_gain is not None:
            g_ref, _, h_ref, ht_ref, hd_ref, stage = rest
            for c in range(tm // rows):
                rs = slice(c * rows, (c + 1) * rows)
                xv = o_ref[rs, :]
                r = lax.rsqrt(jnp.mean(xv * xv, axis=-1, keepdims=True) + EPS)
                hv = ((xv * r) * g_ref[...]).astype(BF16)
                h_ref[rs, :] = hv
                for lt in range(d // HEAD_DIM):
                    stage[lt, rs, :] = hv[:, lt * HEAD_DIM:(lt + 1) * HEAD_DIM].astype(F32)
            ht_ref[...] = h_ref[...].T
            for u in range(tm // run):
                for res in range(dil):
                    dst = slice(u * run + res * ATTN_BLOCK, u * run + (res + 1) * ATTN_BLOCK)
                    for lt in range(d // HEAD_DIM):
                        hd_ref[dst, lt * HEAD_DIM:(lt + 1) * HEAD_DIM] = (
                            stage[lt, pl.ds(u * run + res, ATTN_BLOCK, stride=dil), :].astype(BF16))

    row = pl.BlockSpec((tm, d), lambda i: (i, 0))
    in_specs = [pl.BlockSpec((tm, k), lambda i: (i, 0)), pl.BlockSpec((k, d), lambda i: (0, 0)), row]
    if next_gain is None:
        return pl.pallas_call(
            body, name=name, grid=(t // tm,), in_specs=in_specs, out_specs=row, out_shape=_sds((t, d), F32),
            compiler_params=_cp("parallel"),
        )(y, w, x)
    return pl.pallas_call(
        body, name=name, grid=(t // tm,),
        in_specs=in_specs + [pl.BlockSpec((1, d), lambda i: (0, 0))],
        out_specs=[row, row, pl.BlockSpec((d, tm), lambda i: (0, i)), row],
        out_shape=[_sds((t, d), F32), _sds((t, d), BF16), _sds((d, t), BF16), _sds((t, d), BF16)],
        scratch_shapes=[pltpu.VMEM((d // HEAD_DIM, tm, HEAD_DIM), F32)],
        compiler_params=_cp("parallel"),
    )(y, w, x, next_gain)


def matmul_layouts(hs, w, name, tm=2048):
    t, d = hs[0].shape
    n_blocks = w.shape[1] // CB

    def layout_of(j):
        return jnp.where(j < 9, j // 3, 0)

    def column_of(j):
        return jnp.where(j < 9, (j % 3) * 3 + j // 3, j)

    def body(h0, h1, h2, w_ref, z_ref, wt_ref):
        layout = layout_of(pl.program_id(1))
        for v, h_ref in enumerate((h0, h1, h2)):
            @pl.when(layout == v)
            def _(h_ref=h_ref):
                z_ref[...] = _dot(h_ref[...], w_ref[...]).astype(BF16)
        @pl.when(pl.program_id(0) == 0)
        def _():
            wt_ref[...] = w_ref[...].T

    row = pl.BlockSpec((tm, d), lambda i, j: (i, 0))
    wt_block = lambda i, j: (jnp.where(i == 0, column_of(j), n_blocks), 0)
    return pl.pallas_call(
        body, name=name, grid=(t // tm, n_blocks),
        in_specs=[row, row, row, pl.BlockSpec((d, CB), lambda i, j: (0, column_of(j)))],
        out_specs=[pl.BlockSpec((tm, CB), lambda i, j: (i, column_of(j))), pl.BlockSpec((CB, d), wt_block)],
        out_shape=[_sds((t, n_blocks * CB), BF16), _sds(((n_blocks + 1) * CB, d), BF16)],
        compiler_params=_cp("arbitrary", "arbitrary"),
    )(*hs, w)


def out_proj_bwd(dx, wt, y, name, tm=512):
    t, d = dx.shape
    k = wt.shape[1]
    steps = t // tm
    half = d // 2

    def body(dx_ref, wt_ref, y_ref, dy_ref, dw_ref, acc):
        i = pl.program_id(0)
        dxb = dx_ref[...].astype(BF16)
        dy_ref[...] = _dot(dxb, wt_ref[...]).astype(BF16)

        @pl.when(i == 0)
        def _():
            acc[...] = jnp.zeros_like(acc)

        for c in range(2):
            acc[:, c * half:(c + 1) * half] += _dot_tn(y_ref[...], dxb[:, c * half:(c + 1) * half])

        @pl.when(i == steps - 1)
        def _():
            dw_ref[...] = acc[...].astype(BF16)

    return pl.pallas_call(
        body, name=name, grid=(steps,),
        in_specs=[pl.BlockSpec((tm, d), lambda i: (i, 0)),
                  pl.BlockSpec((d, k), lambda i: (0, 0)),
                  pl.BlockSpec((tm, k), lambda i: (i, 0))],
        out_specs=[pl.BlockSpec((tm, k), lambda i: (i, 0)),
                   pl.BlockSpec((k, d), lambda i: (0, 0))],
        out_shape=[_sds((t, k), BF16), _sds((k, d), BF16)],
        scratch_shapes=[pltpu.VMEM((k, d), F32)],
        compiler_params=_cp("arbitrary"),
    )(dx, wt, y)


def _source_block(kind, rows):
    if kind[0] == "cols":
        return (rows, CB)
    return (None, rows, CB)


def _source_active(kind, j):
    if kind[0] == "cols":
        _, first, n = kind
        return (j >= first) & (j < first + n), jnp.clip(j - first, 0, n - 1)
    _, group = kind
    return (j < 9) & (j % 3 == group), jnp.clip(j // 3, 0, 2)


def _source_index(kind, row_block, inner):
    if kind[0] == "cols":
        return (row_block, inner)
    return (inner, row_block, 0)


def in_proj_dw(ht, sources, n_blocks, name, tk=1024):
    d, t = ht.shape
    steps = t // tk
    ns = len(sources)

    def body(*refs):
        h_ref, src_refs, dw_ref, acc = refs[0], refs[1:1 + ns], refs[1 + ns], refs[2 + ns]
        j, k = pl.program_id(0), pl.program_id(1)

        @pl.when(k == 0)
        def _():
            acc[...] = jnp.zeros_like(acc)

        for s, (_, kind) in enumerate(sources):
            active, _ = _source_active(kind, j)

            @pl.when(active)
            def _(s=s):
                acc[...] += _dot(h_ref[...], src_refs[s][...])

        @pl.when(k == steps - 1)
        def _():
            dw_ref[...] = acc[...].astype(BF16)

    def src_spec(kind):
        def index(j, k):
            active, inner = _source_active(kind, j)
            return _source_index(kind, jnp.where(active, k, 0), inner)
        return pl.BlockSpec(_source_block(kind, tk), index)

    return pl.pallas_call(
        body, name=name, grid=(n_blocks, steps),
        in_specs=[pl.BlockSpec((d, tk), lambda j, k: (0, k))] + [src_spec(kind) for _, kind in sources],
        out_specs=pl.BlockSpec((d, CB), lambda j, k: (0, j)),
        out_shape=_sds((d, n_blocks * CB), BF16),
        scratch_shapes=[pltpu.VMEM((d, CB), F32)],
        compiler_params=_cp("parallel", "arbitrary"),
    )(ht, *[a for a, _ in sources])


def in_proj_dx(sources, n_blocks, wt, x, g, dres, name, tm=1024, rows=256):
    t, d = x.shape
    ns = len(sources)

    def body(*refs):
        src_refs = refs[:ns]
        w_ref, x_hbm, g_ref, dres_hbm, dx_ref, dg_ref, acc, x_ref, dres_ref, sems = refs[ns:]
        i, p = pl.program_id(0), pl.program_id(1)
        tile = pl.ds(pl.multiple_of(i * tm, tm), tm)
        fetch = [pltpu.make_async_copy(x_hbm.at[tile, :], x_ref, sems.at[0]),
                 pltpu.make_async_copy(dres_hbm.at[tile, :], dres_ref, sems.at[1])]

        @pl.when(p == 0)
        def _():
            acc[...] = jnp.zeros_like(acc)
            for cp in fetch:
                cp.start()

        for s, (_, kind) in enumerate(sources):
            active, _ = _source_active(kind, p)

            @pl.when(active)
            def _(s=s):
                acc[...] += _dot(src_refs[s][...], w_ref[...])

        @pl.when(p == n_blocks - 1)
        def _():
            for cp in fetch:
                cp.wait()
            part = jnp.zeros((1, d), F32)
            for c in range(tm // rows):
                rs = slice(c * rows, (c + 1) * rows)
                xv = x_ref[rs, :]
                r = lax.rsqrt(jnp.mean(xv * xv, axis=-1, keepdims=True) + EPS)
                xn = xv * r
                dh = acc[rs, :]
                dhg = dh * g_ref[...]
                dx_ref[rs, :] = dres_ref[rs, :] + r * (dhg - xn * jnp.mean(dhg * xn, axis=-1, keepdims=True))
                part = part + jnp.sum(dh * xn, axis=0, keepdims=True)

            @pl.when(i == 0)
            def _():
                dg_ref[...] = part

            @pl.when(i > 0)
            def _():
                dg_ref[...] += part

    def src_spec(kind):
        def index(i, p):
            if kind[0] == "cols":
                _, first, n = kind
                inner = jnp.clip(p - first, 0, n - 1)
            else:
                _, group = kind
                inner = jnp.clip((p - group + 2) // 3, 0, 2)
            return _source_index(kind, i, inner)
        return pl.BlockSpec(_source_block(kind, tm), index)

    any_spec = pl.BlockSpec(memory_space=pl.ANY)
    return pl.pallas_call(
        body, name=name, grid=(t // tm, n_blocks),
        in_specs=[src_spec(kind) for _, kind in sources] + [
            pl.BlockSpec((CB, d), lambda i, p: (p, 0)), any_spec, pl.BlockSpec((1, d), lambda i, p: (0, 0)), any_spec],
        out_specs=[pl.BlockSpec((tm, d), lambda i, p: (i, 0)),
                   pl.BlockSpec((1, d), lambda i, p: (0, 0))],
        out_shape=[_sds((t, d), F32), _sds((1, d), F32)],
        scratch_shapes=[pltpu.VMEM((tm, d), F32), pltpu.VMEM((tm, d), F32), pltpu.VMEM((tm, d), F32),
                        pltpu.SemaphoreType.DMA((2,))],
        compiler_params=_cp("arbitrary", "arbitrary"),
    )(*[a for a, _ in sources], wt, x, g, dres)


def _window_counts(first_row, rows, w):
    t = first_row + lax.broadcasted_iota(jnp.int32, (rows, 1), 0)
    return jnp.minimum(t + 1, w).astype(F32)


def _tril_bf16(ws_ref, g):
    r = lax.broadcasted_iota(jnp.int32, (CHUNK, CHUNK), 0)
    c = lax.broadcasted_iota(jnp.int32, (CHUNK, CHUNK), 1)
    return jnp.where(r >= c, ws_ref[g], 0.0).astype(BF16), r >= c


def even_mix(z, wp, scale, ws, bs_col, name, tm=512):
    t = z.shape[0]
    gd = CB // len(POOL_SIZES)

    def body(a_ref, ga_ref, u_ref, v_ref, gb_ref, wp_ref, sc_ref, ws_ref, bs_ref, y_ref, pooled_ref, aext):
        i = pl.program_id(0)

        @pl.when(i == 0)
        def _():
            aext[0:HALO, :] = jnp.zeros((HALO, CB), F32)

        @pl.when(i > 0)
        def _():
            aext[0:HALO, :] = aext[tm:tm + HALO, :]

        aext[HALO:HALO + tm, :] = a_ref[...].astype(F32)
        for g, w in enumerate(POOL_SIZES):
            cols = slice(g * gd, (g + 1) * gd)
            tok = aext[HALO:HALO + tm, cols]
            s = tok
            for k in range(1, w):
                s = s + aext[HALO - k:HALO - k + tm, cols]
            pooled = (s / _window_counts(i * tm, tm, w) - tok).astype(BF16)
            pooled_ref[:, cols] = pooled
            mixed = _dot(pooled, wp_ref[g])
            silu_a, _ = _silu_and_grad(ga_ref[:, cols].astype(F32))
            y_ref[:, cols] = (mixed * sc_ref[:, cols] * silu_a).astype(BF16)
        for g in range(4):
            cols = slice(g * gd, (g + 1) * gd)
            wg, _ = _tril_bf16(ws_ref, g)
            for c in range(tm // CHUNK):
                rows = slice(c * CHUNK, (c + 1) * CHUNK)
                m = _dot(wg, v_ref[rows, cols]) + bs_ref[g]
                silu_b, _ = _silu_and_grad(gb_ref[rows, cols].astype(F32))
                y_ref[rows, CB + g * gd:CB + (g + 1) * gd] = (
                    u_ref[rows, cols].astype(F32) * m * silu_b).astype(BF16)

    zspec = lambda cb: pl.BlockSpec((tm, CB), lambda i, cb=cb: (i, cb))
    full = lambda shape: pl.BlockSpec(shape, lambda i: (0,) * len(shape))
    return pl.pallas_call(
        body, name=name, grid=(t // tm,),
        in_specs=[zspec(0), zspec(1), zspec(2), zspec(3), zspec(4),
                  full(wp.shape), full(scale.shape), full(ws.shape), full(bs_col.shape)],
        out_specs=[pl.BlockSpec((tm, 2 * CB), lambda i: (i, 0)), pl.BlockSpec((tm, CB), lambda i: (i, 0))],
        out_shape=[_sds((t, 2 * CB), BF16), _sds((t, CB), BF16)],
        scratch_shapes=[pltpu.VMEM((HALO + tm, CB), F32)],
        compiler_params=_cp("arbitrary"),
    )(z, z, z, z, z, wp, scale, ws, bs_col)


def even_mix_bwd(dy, z, pooled, wp, scale, ws, bs_col, name, tm=512):
    t = z.shape[0]
    nt = t // tm
    gd = CB // len(POOL_SIZES)

    def body(dya_ref, dyb_ref, ga_ref, u_ref, v_ref, gb_ref, pooled_ref, wp_ref, sc_ref, ws_ref, bs_ref,
             dz_ref, dwp_ref, dsc_ref, dws_ref, dbs_ref, dpext):
        i = pl.program_id(0)
        tile = nt - 1 - i

        @pl.when(i == 0)
        def _():
            dpext[tm:tm + HALO, :] = jnp.zeros((HALO, CB), F32)
            dwp_ref[...] = jnp.zeros_like(dwp_ref)
            dsc_ref[...] = jnp.zeros_like(dsc_ref)
            dws_ref[...] = jnp.zeros_like(dws_ref)
            dbs_ref[...] = jnp.zeros_like(dbs_ref)

        @pl.when(i > 0)
        def _():
            dpext[tm:tm + HALO, :] = dpext[0:HALO, :]

        for g, w in enumerate(POOL_SIZES):
            cols = slice(g * gd, (g + 1) * gd)
            pooled_g = pooled_ref[:, cols]
            mixed = _dot(pooled_g, wp_ref[g])
            silu_a, dsilu_a = _silu_and_grad(ga_ref[:, cols].astype(F32))
            dya = dya_ref[:, cols].astype(F32)
            sc = sc_ref[:, cols]
            dmixed = (dya * sc * silu_a).astype(BF16)
            dsc_ref[:, cols] += jnp.sum(dya * mixed * silu_a, axis=0, keepdims=True)
            dz_ref[:, CB + g * gd:CB + (g + 1) * gd] = (dya * mixed * sc * dsilu_a).astype(BF16)
            dwp_ref[g] += _dot_tn(pooled_g, dmixed)
            dpooled = _dot_nt(dmixed, wp_ref[g])
            dpext[0:tm, cols] = dpooled / _window_counts(tile * tm, tm, w)
            s = dpext[0:tm, cols]
            for k in range(1, w):
                s = s + dpext[k:k + tm, cols]
            dz_ref[:, cols] = (s - dpooled).astype(BF16)
        for g in range(4):
            cols = slice(g * gd, (g + 1) * gd)
            wg, lower = _tril_bf16(ws_ref, g)
            dws_g = jnp.zeros((CHUNK, CHUNK), F32)
            dbs_g = jnp.zeros((CHUNK, 1), F32)
            for c in range(tm // CHUNK):
                rows = slice(c * CHUNK, (c + 1) * CHUNK)
                vb = v_ref[rows, cols]
                m = _dot(wg, vb) + bs_ref[g]
                gbv = gb_ref[rows, cols].astype(F32)
                silu_b, dsilu_b = _silu_and_grad(gbv)
                dyb = dyb_ref[rows, cols].astype(F32)
                uv = u_ref[rows, cols].astype(F32)
                dm = dyb * silu_b * uv
                dmb = dm.astype(BF16)
                dz_ref[rows, 2 * CB + g * gd:2 * CB + (g + 1) * gd] = (dyb * silu_b * m).astype(BF16)
                dz_ref[rows, 3 * CB + g * gd:3 * CB + (g + 1) * gd] = _dot_tn(wg, dmb).astype(BF16)
                dz_ref[rows, 4 * CB + g * gd:4 * CB + (g + 1) * gd] = (dyb * uv * m * dsilu_b).astype(BF16)
                dws_g = dws_g + _dot_nt(dmb, vb)
                dbs_g = dbs_g + jnp.sum(dm, axis=1, keepdims=True)
            dws_ref[g] += jnp.where(lower, dws_g, 0.0)
            dbs_ref[g] += dbs_g

    rev = lambda cb: pl.BlockSpec((tm, CB), lambda i, cb=cb: (nt - 1 - i, cb))
    full = lambda shape: pl.BlockSpec(shape, lambda i: (0,) * len(shape))
    return pl.pallas_call(
        body, name=name, grid=(nt,),
        in_specs=[rev(0), rev(1), rev(1), rev(2), rev(3), rev(4), rev(0),
                  full(wp.shape), full(scale.shape), full(ws.shape), full(bs_col.shape)],
        out_specs=[pl.BlockSpec((tm, 5 * CB), lambda i: (nt - 1 - i, 0)),
                   full(wp.shape), full(scale.shape), full(ws.shape), full(bs_col.shape)],
        out_shape=[_sds((t, 5 * CB), BF16), _sds(wp.shape, F32), _sds(scale.shape, F32),
                   _sds(ws.shape, F32), _sds(bs_col.shape, F32)],
        scratch_shapes=[pltpu.VMEM((tm + HALO, CB), F32)],
        compiler_params=_cp("arbitrary"),
    )(dy, dy, z, z, z, z, pooled, wp, scale, ws, bs_col)


def _slope(group, head):
    return float(2.0 ** (-8.0 * (group * HEADS_PER_GROUP + head + 1) / N_HEADS))


def _band(dilation):
    qi = lax.broadcasted_iota(jnp.int32, (ATTN_BLOCK, ATTN_BLOCK), 0)
    ki = lax.broadcasted_iota(jnp.int32, (ATTN_BLOCK, ATTN_BLOCK), 1)
    dist_prev = ((qi + ATTN_BLOCK - ki) * dilation).astype(F32)
    dist_cur = ((qi - ki) * dilation).astype(F32)
    return dist_prev, ki >= qi, dist_cur, ki <= qi


def _permute_rows(name, d, inverse, arrays, in_specs, out_shapes, out_specs, n_chunks, widths):
    rows = ATTN_BLOCK * d
    t = out_shapes[0].shape[-2]
    n = len(arrays)

    lanes = HEAD_DIM

    def body(*refs):
        for a, o, s, w in zip(refs[:n], refs[n:2 * n], refs[2 * n:], widths):
            for k in range(w // lanes):
                cols = slice(k * lanes, (k + 1) * lanes)
                if inverse:
                    for r in range(d):
                        s[k, pl.ds(r, ATTN_BLOCK, stride=d), :] = (
                            a[r * ATTN_BLOCK:(r + 1) * ATTN_BLOCK, cols].astype(F32))
                    o[:, cols] = s[k].astype(o.dtype)
                else:
                    s[k] = a[:, cols].astype(F32)
                    for r in range(d):
                        o[r * ATTN_BLOCK:(r + 1) * ATTN_BLOCK, cols] = (
                            s[k, pl.ds(r, ATTN_BLOCK, stride=d), :].astype(o.dtype))

    return pl.pallas_call(
        body, name=name, grid=(t // rows, n_chunks),
        in_specs=in_specs, out_specs=out_specs, out_shape=out_shapes,
        scratch_shapes=[pltpu.VMEM((w // lanes, rows, lanes), F32) for w in widths],
        compiler_params=_cp("parallel", "arbitrary"),
    )(*arrays)


def _chunk_width(d):
    return 4 * CB // d


def permute_mats(wide, narrow, group, inverse, name):
    d = DILATIONS[group]
    rows, cw = ATTN_BLOCK * d, _chunk_width(d)
    wide_spec = pl.BlockSpec((rows, cw), lambda i, c: (i, c))
    narrow_spec = pl.BlockSpec((rows, HEAD_DIM), lambda i, c: (i, 0))
    arrays = [wide] + list(narrow)
    specs = [wide_spec] + [narrow_spec] * len(narrow)
    return _permute_rows(name, d, inverse, arrays, specs, [_sds(a.shape, a.dtype) for a in arrays], specs,
                         CB // cw, [cw] + [HEAD_DIM] * len(narrow))


def permute_slabs(s, group, inverse, name):
    d = DILATIONS[group]
    rows, cw = ATTN_BLOCK * d, _chunk_width(d)
    per = CB // cw
    spec = pl.BlockSpec((None, rows, cw), lambda i, c: (c // per, i, c % per))
    (out,) = _permute_rows(name, d, inverse, [s], [spec], [_sds(s.shape, s.dtype)], [spec], 3 * per, [cw])
    return out


def _lane_pack(cols):
    rows = cols[0].shape[0]
    lane = lax.broadcasted_iota(jnp.int32, (rows, HEAD_DIM), 1)
    out = jnp.zeros((rows, HEAD_DIM), F32)
    for h, c in enumerate(cols):
        out = jnp.where(lane == h, c, out)
    return out


def _qkv_col(group, from_z):
    return (lambda which: which * 3 + group) if from_z else (lambda which: which)


SCORE_SCALE = HEAD_DIM ** -0.5
LOG2_E = 1.4426950408889634
EXP2_SCALE = SCORE_SCALE * LOG2_E


def _fill_bias(bias_ref, group, d):
    dist_p, ok_p, dist_c, ok_c = _band(d)
    for h in range(HEADS_PER_GROUP):
        k = -_slope(group, h) / SCORE_SCALE
        bias_ref[h, :, 0:ATTN_BLOCK] = jnp.where(ok_p, k * dist_p, NEG)
        bias_ref[h, :, ATTN_BLOCK:2 * ATTN_BLOCK] = jnp.where(ok_c, k * dist_c, NEG)


STEP_BLOCKS = 4


def _step_plan(d):
    return d < STEP_BLOCKS, max(d // STEP_BLOCKS, 1)


def _joined(prev_ref, cur_ref, j, cols, chained):
    rows = slice(j * ATTN_BLOCK, (j + 1) * ATTN_BLOCK)
    if not chained:
        before = prev_ref[rows, cols]
    elif j == 0:
        before = prev_ref[(STEP_BLOCKS - 1) * ATTN_BLOCK:STEP_BLOCKS * ATTN_BLOCK, cols]
    else:
        before = cur_ref[(j - 1) * ATTN_BLOCK:j * ATTN_BLOCK, cols]
    return jnp.concatenate([before, cur_ref[rows, cols]], axis=0)


def _raw_scores(q, kk, bias_ref, h, has_prev):
    is_cur = lax.broadcasted_iota(jnp.int32, (ATTN_BLOCK, 2 * ATTN_BLOCK), 1) >= ATTN_BLOCK
    return jnp.where(jnp.logical_or(is_cur, has_prev), _dot_nt(q, kk) + bias_ref[h], NEG)


def attn_fwd(qkv, group, name):
    t = qkv.shape[0]
    d = DILATIONS[group]
    sub = STEP_BLOCKS
    chained, back = _step_plan(d)
    token_order = d == sub
    col = _qkv_col(group, qkv.shape[1] != 3 * CB)
    heads = range(HEADS_PER_GROUP)

    def body(q_ref, kp_ref, kc_ref, vp_ref, vc_ref, o_ref, lse_ref, bias_ref, *stage):
        b = pl.program_id(0)

        @pl.when(b == 0)
        def _():
            _fill_bias(bias_ref, group, d)

        hs = [slice(h * HEAD_DIM, (h + 1) * HEAD_DIM) for h in heads]
        for j in range(sub):
            rows = slice(j * ATTN_BLOCK, (j + 1) * ATTN_BLOCK)
            has_prev = True if chained and j > 0 else b >= back
            s = [_raw_scores(q_ref[rows, hs[h]], _joined(kp_ref, kc_ref, j, hs[h], chained), bias_ref, h, has_prev)
                 for h in heads]
            m = [jnp.max(t_, axis=1, keepdims=True) for t_ in s]
            e = [jnp.exp2(EXP2_SCALE * (t_ - m[h])) for h, t_ in enumerate(s)]
            l = [jnp.sum(e_, axis=1, keepdims=True) for e_ in e]
            o = [_dot(e_.astype(BF16), _joined(vp_ref, vc_ref, j, hs[h], chained)) * (1.0 / l[h])
                 for h, e_ in enumerate(e)]
            lse_j = _lane_pack([SCORE_SCALE * m[h] + jnp.log(l[h]) for h in heads])
            if token_order:
                strided = pl.ds(j, ATTN_BLOCK, stride=d)
                for h in heads:
                    stage[0][h, strided, :] = o[h]
                stage[0][HEADS_PER_GROUP, strided, :] = lse_j
            else:
                for h in heads:
                    o_ref[rows, hs[h]] = o[h].astype(BF16)
                lse_ref[rows, :] = lse_j
        if token_order:
            for h in heads:
                o_ref[:, hs[h]] = stage[0][h].astype(BF16)
            lse_ref[...] = stage[0][HEADS_PER_GROUP]

    blk = (sub * ATTN_BLOCK, CB)
    cur = lambda which: pl.BlockSpec(blk, lambda b: (b, col(which)))
    prev = lambda which: pl.BlockSpec(blk, lambda b: (jnp.maximum(b - back, 0), col(which)))
    scratch = [pltpu.VMEM((HEADS_PER_GROUP, ATTN_BLOCK, 2 * ATTN_BLOCK), F32)]
    if token_order:
        scratch.append(pltpu.VMEM((HEADS_PER_GROUP + 1, blk[0], HEAD_DIM), F32))
    return pl.pallas_call(
        body, name=name, grid=(t // blk[0],),
        in_specs=[cur(0), prev(1), cur(1), prev(2), cur(2)],
        out_specs=[pl.BlockSpec(blk, lambda b: (b, 0)), pl.BlockSpec((blk[0], HEAD_DIM), lambda b: (b, 0))],
        out_shape=[_sds((t, CB), BF16), _sds((t, HEAD_DIM), F32)],
        scratch_shapes=scratch,
        compiler_params=_cp("arbitrary"),
    )(qkv, qkv, qkv, qkv, qkv)


def attn_bwd(z, dyc, lse, dd, group, name):
    t = z.shape[0]
    d = DILATIONS[group]
    sub = STEP_BLOCKS
    chained, back = _step_plan(d)
    nb = t // (ATTN_BLOCK * d) // (sub if chained else 1)
    col = _qkv_col(group, z.shape[1] != 3 * CB)
    heads = range(HEADS_PER_GROUP)
    steps = back * nb
    token_order = d == sub

    def body(q_ref, kp_ref, kc_ref, vp_ref, vc_ref, dy_ref, lse_ref, dd_ref, out_ref, carry, bias_ref, *stage):
        s_ = pl.program_id(0)
        n = s_ % nb
        hs = [slice(h * HEAD_DIM, (h + 1) * HEAD_DIM) for h in heads]
        rows = [slice(j * ATTN_BLOCK, (j + 1) * ATTN_BLOCK) for j in range(sub)]

        def put(w, j, h, value):
            if token_order:
                stage[0][w * HEADS_PER_GROUP + h, pl.ds(j, ATTN_BLOCK, stride=d), :] = value
            else:
                out_ref[w, rows[j], hs[h]] = value.astype(BF16)

        def put_done():
            if token_order:
                for w in range(3):
                    for h in heads:
                        out_ref[w, :, hs[h]] = stage[0][w * HEADS_PER_GROUP + h].astype(BF16)

        @pl.when(s_ == 0)
        def _():
            _fill_bias(bias_ref, group, d)
            carry[...] = jnp.zeros_like(carry)

        @pl.when(s_ == steps)
        def _():
            for w in range(3):
                for j in range(sub):
                    for h in heads:
                        put(w, j, h, carry[w, rows[j], hs[h]])
            put_done()

        @pl.when(s_ < steps)
        def _():
            old = [[[carry[w, rows[j], hs[h]] for h in heads] for j in range(sub)] for w in range(3)]
            if token_order:
                for h in heads:
                    stage[1][h] = dy_ref[:, hs[h]].astype(F32)
            grads = []
            first, second = slice(0, ATTN_BLOCK), slice(ATTN_BLOCK, 2 * ATTN_BLOCK)

            def emit(j, grads):
                for h in heads:
                    put(0, j, h, old[0][j][h])
                    carry[0, rows[j], hs[h]] = grads[j][h][0]
                    for w in (1, 2):
                        if not chained:
                            done = old[w][j][h] + grads[j][h][w][first]
                            kept = grads[j][h][w][second]
                        else:
                            done = old[w][j][h] + grads[0][h][w][first] if j == sub - 1 else old[w][j][h]
                            kept = grads[j][h][w][second]
                            if j < sub - 1:
                                kept = kept + grads[j + 1][h][w][first]
                        put(w, j, h, done)
                        carry[w, rows[j], hs[h]] = kept

            for j in range(sub):
                has_prev = True if chained and j > 0 else n > 0
                q = [q_ref[rows[j], hs[h]] for h in heads]
                kk = [_joined(kp_ref, kc_ref, j, hs[h], chained) for h in heads]
                if token_order:
                    strided = pl.ds(j, ATTN_BLOCK, stride=d)
                    dy = [stage[1][h, strided, :].astype(BF16) for h in heads]
                    lse_j, dd_j = lse_ref[strided, :], dd_ref[strided, :]
                else:
                    dy = [dy_ref[rows[j], hs[h]] for h in heads]
                    lse_j, dd_j = lse_ref[rows[j], :], dd_ref[rows[j], :]
                s = [_raw_scores(q[h], kk[h], bias_ref, h, has_prev) for h in heads]
                p = [jnp.exp2(EXP2_SCALE * t_ - lse_j[:, h:h + 1] * LOG2_E) for h, t_ in enumerate(s)]
                ds = [(p[h] * (_dot_nt(dy[h], _joined(vp_ref, vc_ref, j, hs[h], chained)) - dd_j[:, h:h + 1])
                       * SCORE_SCALE).astype(BF16) for h in heads]
                grads.append([(_dot(ds[h], kk[h]), _dot_tn(ds[h], q[h]), _dot_tn(p[h].astype(BF16), dy[h]))
                              for h in heads])
                if not chained:
                    emit(j, grads)
            if chained:
                for j in range(sub):
                    emit(j, grads)
            put_done()

    blk = (sub * ATTN_BLOCK, CB)
    scratch = [pltpu.VMEM((3, blk[0], CB), F32), pltpu.VMEM((HEADS_PER_GROUP, ATTN_BLOCK, 2 * ATTN_BLOCK), F32)]
    if token_order:
        scratch.append(pltpu.VMEM((3 * HEADS_PER_GROUP, blk[0], HEAD_DIM), F32))
        scratch.append(pltpu.VMEM((HEADS_PER_GROUP, blk[0], HEAD_DIM), F32))

    def row_of(s_):
        s_ = jnp.clip(s_, 0, steps - 1)
        return (s_ % nb) * back + s_ // nb

    def prev_row_of(s_):
        s_ = jnp.clip(s_, 0, steps - 1)
        return row_of(s_) - jnp.where(s_ % nb > 0, back, 0)

    cur = lambda which: pl.BlockSpec(blk, lambda s_: (row_of(s_), col(which)))
    prev = lambda which: pl.BlockSpec(blk, lambda s_: (prev_row_of(s_), col(which)))
    narrow = pl.BlockSpec((blk[0], HEAD_DIM), lambda s_: (row_of(s_), 0))
    return pl.pallas_call(
        body, name=name, grid=(steps + 1,),
        in_specs=[cur(0), prev(1), cur(1), prev(2), cur(2), pl.BlockSpec(blk, lambda s_: (row_of(s_), 0)), narrow, narrow],
        out_specs=pl.BlockSpec((3, blk[0], CB), lambda s_: (0, row_of(s_ - 1), 0)),
        out_shape=_sds((3, t, CB), BF16),
        scratch_shapes=scratch,
        compiler_params=_cp("arbitrary"),
    )(z, z, z, z, z, dyc, lse, dd)


def odd_mix(os_, lses, z, conv_w, name, tm=512):
    t = z.shape[0]

    def body(o0, o1, o2, l0, l1, l2, gc_ref, db_ref, dc_ref, dx_ref, gd_ref, cw_ref,
             y_ref, yc_ref, lse_ref, conv_ref, zext):
        i = pl.program_id(0)
        a0, a1, a2 = l0[...], l1[...], l2[...]
        m = jnp.maximum(jnp.maximum(a0, a1), a2)
        tot = m + jnp.log(jnp.exp(a0 - m) + jnp.exp(a1 - m) + jnp.exp(a2 - m))
        lse_ref[...] = tot
        w0, w1, w2 = jnp.exp(a0 - tot), jnp.exp(a1 - tot), jnp.exp(a2 - tot)
        for h in range(HEADS_PER_GROUP):
            hs = slice(h * HEAD_DIM, (h + 1) * HEAD_DIM)
            yc = (w0[:, h:h + 1] * o0[:, hs].astype(F32) + w1[:, h:h + 1] * o1[:, hs].astype(F32)
                  + w2[:, h:h + 1] * o2[:, hs].astype(F32))
            yc_ref[:, hs] = yc.astype(BF16)
            silu_c, _ = _silu_and_grad(gc_ref[:, hs].astype(F32))
            y_ref[:, hs] = (yc * silu_c).astype(BF16)

        @pl.when(i == 0)
        def _():
            zext[0:HALO, :] = jnp.zeros((HALO, CB), F32)

        @pl.when(i > 0)
        def _():
            zext[0:HALO, :] = zext[tm:tm + HALO, :]

        zext[HALO:HALO + tm, :] = dc_ref[...].astype(F32) * dx_ref[...].astype(F32)
        conv = (cw_ref[0:1, :] * zext[HALO - 2:HALO - 2 + tm, :] + cw_ref[1:2, :] * zext[HALO - 1:HALO - 1 + tm, :]
                + cw_ref[2:3, :] * zext[HALO:HALO + tm, :])
        conv_ref[...] = conv.astype(BF16)
        silu_d, _ = _silu_and_grad(gd_ref[...].astype(F32))
        y_ref[:, CB:2 * CB] = (db_ref[...].astype(F32) * conv * silu_d).astype(BF16)

    row = pl.BlockSpec((tm, CB), lambda i: (i, 0))
    narrow = pl.BlockSpec((tm, HEAD_DIM), lambda i: (i, 0))
    zspec = lambda cb: pl.BlockSpec((tm, CB), lambda i, cb=cb: (i, cb))
    return pl.pallas_call(
        body, name=name, grid=(t // tm,),
        in_specs=[row] * 3 + [narrow] * 3 + [zspec(9), zspec(10), zspec(11), zspec(12), zspec(13),
                                             pl.BlockSpec(conv_w.shape, lambda i: (0, 0))],
        out_specs=[pl.BlockSpec((tm, 2 * CB), lambda i: (i, 0)), row, narrow, row],
        out_shape=[_sds((t, 2 * CB), BF16), _sds((t, CB), BF16), _sds((t, HEAD_DIM), F32), _sds((t, CB), BF16)],
        scratch_shapes=[pltpu.VMEM((HALO + tm, CB), F32)],
        compiler_params=_cp("arbitrary"),
    )(*os_, *lses, z, z, z, z, z, conv_w)


def odd_mix_bwd(dy, yc, conv, z, conv_w, name, tm=256):
    t = z.shape[0]
    nt = t // tm

    def body(dyc_ref, dyd_ref, yc_ref, conv_ref, gc_ref, db_ref, dc_ref, dx_ref, gd_ref, cw_ref,
             dz_ref, dyo_ref, dd_ref, dcw_ref, dcext):
        i = pl.program_id(0)

        @pl.when(i == 0)
        def _():
            dcext[tm:tm + HALO, :] = jnp.zeros((HALO, CB), F32)
            dcw_ref[...] = jnp.zeros_like(dcw_ref)

        @pl.when(i > 0)
        def _():
            dcext[tm:tm + HALO, :] = dcext[0:HALO, :]

        silu_c, dsilu_c = _silu_and_grad(gc_ref[...].astype(F32))
        dyc = dyc_ref[...].astype(F32)
        ycv = yc_ref[...].astype(F32)
        dyo = dyc * silu_c
        dyo_ref[...] = dyo.astype(BF16)
        dz_ref[:, 0:CB] = (dyc * ycv * dsilu_c).astype(BF16)
        prod = dyo * ycv
        dd_ref[...] = _lane_pack([jnp.sum(prod[:, h * HEAD_DIM:(h + 1) * HEAD_DIM], axis=1, keepdims=True)
                                  for h in range(HEADS_PER_GROUP)])

        silu_d, dsilu_d = _silu_and_grad(gd_ref[...].astype(F32))
        dyd = dyd_ref[...].astype(F32)
        convv = conv_ref[...].astype(F32)
        dbv = db_ref[...].astype(F32)
        dz_ref[:, CB:2 * CB] = (dyd * convv * silu_d).astype(BF16)
        dz_ref[:, 4 * CB:5 * CB] = (dyd * dbv * convv * dsilu_d).astype(BF16)
        dcext[0:tm, :] = dyd * dbv * silu_d
        dcv, dxv = dc_ref[...].astype(F32), dx_ref[...].astype(F32)
        zc = dcv * dxv
        d0, d1, d2 = dcext[0:tm, :], dcext[1:1 + tm, :], dcext[2:2 + tm, :]
        dzc = cw_ref[2:3, :] * d0 + cw_ref[1:2, :] * d1 + cw_ref[0:1, :] * d2
        dz_ref[:, 2 * CB:3 * CB] = (dzc * dxv).astype(BF16)
        dz_ref[:, 3 * CB:4 * CB] = (dzc * dcv).astype(BF16)
        dcw_ref[0:1, :] += jnp.sum(zc * d2, axis=0, keepdims=True)
        dcw_ref[1:2, :] += jnp.sum(zc * d1, axis=0, keepdims=True)
        dcw_ref[2:3, :] += jnp.sum(zc * d0, axis=0, keepdims=True)

    rev = lambda cb: pl.BlockSpec((tm, CB), lambda i, cb=cb: (nt - 1 - i, cb))
    return pl.pallas_call(
        body, name=name, grid=(nt,),
        in_specs=[rev(0), rev(1), rev(0), rev(0), rev(9), rev(10), rev(11), rev(12), rev(13),
                  pl.BlockSpec(conv_w.shape, lambda i: (0, 0))],
        out_specs=[pl.BlockSpec((tm, 5 * CB), lambda i: (nt - 1 - i, 0)), rev(0),
                   pl.BlockSpec((tm, HEAD_DIM), lambda i: (nt - 1 - i, 0)),
                   pl.BlockSpec(conv_w.shape, lambda i: (0, 0))],
        out_shape=[_sds((t, 5 * CB), BF16), _sds((t, CB), BF16), _sds((t, HEAD_DIM), F32), _sds(conv_w.shape, F32)],
        scratch_shapes=[pltpu.VMEM((tm + HALO, CB), F32)],
        compiler_params=_cp("arbitrary"),
    )(dy, dy, yc, conv, z, z, z, z, z, conv_w)


def out_proj_final_loss(y, w, x, g, target, name, tm=1024, rows=256):
    t, d = x.shape
    k = y.shape[1]

    def body(y_ref, w_ref, x_ref, g_ref, t_ref, dx_ref, loss_ref, dg_ref):
        i = pl.program_id(0)
        dx_ref[...] = x_ref[...] + _dot(y_ref[...], w_ref[...])
        gv = g_ref[...]
        loss = jnp.zeros((1, 1), F32)
        dg = jnp.zeros((1, d), F32)
        for c in range(tm // rows):
            rs = slice(c * rows, (c + 1) * rows)
            xv = dx_ref[rs, :]
            r = lax.rsqrt(jnp.mean(xv * xv, axis=-1, keepdims=True) + EPS)
            xn = xv * r
            err = xn * gv - t_ref[rs, :]
            loss = loss + 0.5 * jnp.sum(jnp.mean(err * err, axis=-1, keepdims=True), axis=0, keepdims=True)
            dy = err * (1.0 / d)
            dyg = dy * gv
            dx_ref[rs, :] = r * (dyg - xn * jnp.mean(dyg * xn, axis=-1, keepdims=True))
            dg = dg + jnp.sum(dy * xn, axis=0, keepdims=True)

        @pl.when(i == 0)
        def _():
            loss_ref[...] = jnp.broadcast_to(loss, loss_ref.shape)
            dg_ref[...] = dg

        @pl.when(i > 0)
        def _():
            loss_ref[...] += jnp.broadcast_to(loss, loss_ref.shape)
            dg_ref[...] += dg

    row = pl.BlockSpec((tm, d), lambda i: (i, 0))
    return pl.pallas_call(
        body, name=name, grid=(t // tm,),
        in_specs=[pl.BlockSpec((tm, k), lambda i: (i, 0)), pl.BlockSpec((k, d), lambda i: (0, 0)), row,
                  pl.BlockSpec((1, d), lambda i: (0, 0)), row],
        out_specs=[row, pl.BlockSpec((8, 128), lambda i: (0, 0)), pl.BlockSpec((1, d), lambda i: (0, 0))],
        out_shape=[_sds((t, d), F32), _sds((8, 128), F32), _sds((1, d), F32)],
        compiler_params=_cp("arbitrary"),
    )(y, w, x, g, target)


def _position():
    x, y, c = lax.axis_index("x"), lax.axis_index("y"), lax.axis_index("c")
    return x, y, c, 4 * x + 2 * y + c


def _peer(x, y, c, k):
    px = 1 - x if k & 4 else x
    py = 1 - y if k & 2 else y
    pc = 1 - c if k & 1 else c
    return (px, py, pc), 4 * px + 2 * py + pc


def _block(ref, axis, size, idx):
    index = [slice(None)] * len(ref.shape)
    index[axis] = pl.ds(idx * size, size)
    return ref.at[tuple(index)]


def _other_chips(x, y):
    return [(1 - x, y), (x, 1 - y), (1 - x, 1 - y)]


def _gather_copy(ins, lands, axes, send_sems, recv_sems, k, p, block, to, from_shard=False):
    dst = _block(lands[p], axes[p], lands[p].shape[axes[p]] // N_DEV, block)
    sem = k * len(lands) + p
    return pltpu.make_async_remote_copy(
        src_ref=ins[p] if from_shard else dst, dst_ref=dst, send_sem=send_sems.at[sem], recv_sem=recv_sems.at[sem],
        device_id=to, device_id_type=MESH_ID)


def _own_block_copy(ins, lands, axes, sems, p, me):
    n = len(lands)
    dst = _block(lands[p], axes[p], lands[p].shape[axes[p]] // N_DEV, me)
    return pltpu.make_async_copy(ins[p], dst, sems.at[4 * n + p])


def all_gather_two_level(shards, axes, name):
    n = len(shards)
    fulls = []
    for s, ax in zip(shards, axes):
        shape = list(s.shape)
        shape[ax] *= N_DEV
        fulls.append(_sds(tuple(shape), s.dtype))

    def body(*refs):
        ins, outs = refs[:n], refs[n:2 * n]
        send_sems, recv_sems, local_sems = refs[2 * n:]
        x, y, c, me = _position()
        sibling, here = (x, y, 1 - c), (x, y, c)
        chips = _other_chips(x, y)
        copy = functools.partial(_gather_copy, ins, outs, axes, send_sems, recv_sems)
        local = [pltpu.make_async_copy(ins[p], _block(outs[p], axes[p], ins[p].shape[axes[p]], me), local_sems.at[p])
                 for p in range(n)]
        sent = []
        for p in range(n):
            sent.append(copy(0, p, me, sibling, from_shard=True))
            sent += [copy(1 + j, p, me, (*chip, c), from_shard=True) for j, chip in enumerate(chips)]
        for cp in local + sent:
            cp.start()
        for j, (px, py) in enumerate(chips):
            for p in range(n):
                arrived = 4 * px + 2 * py + c
                copy(1 + j, p, arrived, here).wait_recv()
                sent.append(copy(4 + j, p, arrived, sibling))
                sent[-1].start()
        for p in range(n):
            copy(0, p, 4 * x + 2 * y + 1 - c, here).wait_recv()
            for j, (px, py) in enumerate(chips):
                copy(4 + j, p, 4 * px + 2 * py + 1 - c, here).wait_recv()
        for cp in sent:
            cp.wait_send()
        for cp in local:
            cp.wait()

    any_spec = pl.BlockSpec(memory_space=pl.ANY)
    return pl.pallas_call(
        body, name=name,
        in_specs=[any_spec] * n, out_specs=[any_spec] * n, out_shape=fulls,
        scratch_shapes=[pltpu.SemaphoreType.DMA((7 * n,)), pltpu.SemaphoreType.DMA((7 * n,)), pltpu.SemaphoreType.DMA((n,))],
    )(*shards)


def gather_two_level_start(shards, axes, after, name):
    n = len(shards)
    lands = []
    for s, ax in zip(shards, axes):
        shape = list(s.shape)
        shape[ax] *= N_DEV
        lands.append(lax.empty(tuple(shape), s.dtype))

    def body(*refs):
        ins, land = refs[:n], refs[n:2 * n]
        send_sems, recv_sems, token = refs[2 * n + 1], refs[2 * n + 2], refs[-1]
        x, y, c, me = _position()
        copy = functools.partial(_gather_copy, ins, land, axes, send_sems, recv_sems)
        for p in range(n):
            copy(0, p, me, (x, y, 1 - c), from_shard=True).start()
            for j, chip in enumerate(_other_chips(x, y)):
                copy(1 + j, p, me, (*chip, c), from_shard=True).start()
            _own_block_copy(ins, land, axes, send_sems, p, me).start()
        token[...] = jnp.zeros_like(token)

    outs = pl.pallas_call(
        body, name=name,
        out_shape=(pltpu.SemaphoreType.DMA((5 * n,)), pltpu.SemaphoreType.DMA((4 * n,)),
                   *[pltpu.HBM(s.shape, s.dtype) for s in shards], *[pltpu.HBM(l.shape, l.dtype) for l in lands],
                   _sds((8, 128), F32)),
        in_specs=[_HBM] * (2 * n) + [pl.BlockSpec(memory_space=pl.ANY)],
        out_specs=(_SEM, _SEM, *[_HBM] * (2 * n), pl.BlockSpec(memory_space=pltpu.VMEM)),
        input_output_aliases={i: 2 + i for i in range(2 * n)},
        compiler_params=pltpu.CompilerParams(has_side_effects=_EFFECT),
    )(*[_in_hbm(s) for s in shards], *[_in_hbm(l) for l in lands], after)
    return outs[0], outs[1], list(outs[2:2 + n]), list(outs[2 + n:2 + 2 * n]), outs[-1]


def gather_two_level_forward(started, axes, after, name):
    send_a, recv_a, shards, lands, _ = started
    n = len(shards)

    def body(*refs):
        land = refs[:n]
        send_first, recv_first = refs[n], refs[n + 1]
        send_fwd, recv_fwd, token = refs[n + 3], refs[n + 4], refs[-1]
        x, y, c, _ = _position()
        for j, (px, py) in enumerate(_other_chips(x, y)):
            for p in range(n):
                arrived = 4 * px + 2 * py + c
                _gather_copy(None, land, axes, send_first, recv_first, 1 + j, p, arrived, (x, y, c)).wait_recv()
                _gather_copy(None, land, axes, send_fwd, recv_fwd, j, p, arrived, (x, y, 1 - c)).start()
        token[...] = jnp.zeros_like(token)

    outs = pl.pallas_call(
        body, name=name,
        out_shape=(pltpu.SemaphoreType.DMA((3 * n,)), pltpu.SemaphoreType.DMA((3 * n,)),
                   *[pltpu.HBM(l.shape, l.dtype) for l in lands], _sds((8, 128), F32)),
        in_specs=[_HBM] * n + [_SEM, _SEM, pl.BlockSpec(memory_space=pl.ANY)],
        out_specs=(_SEM, _SEM, *[_HBM] * n, pl.BlockSpec(memory_space=pltpu.VMEM)),
        input_output_aliases={i: 2 + i for i in range(n)},
        compiler_params=pltpu.CompilerParams(has_side_effects=_EFFECT),
    )(*lands, send_a, recv_a, after)
    return send_a, recv_a, outs[0], outs[1], shards, list(outs[2:2 + n]), outs[-1]


def gather_two_level_wait(forwarded, axes, after, name):
    send_a, recv_a, send_f, recv_f, shards, lands, _ = forwarded
    n = len(shards)

    def body(*refs):
        ins, land = refs[:n], refs[n:2 * n]
        sa, ra, sf, rf = refs[2 * n:2 * n + 4]
        x, y, c, me = _position()
        here = (x, y, c)
        chips = _other_chips(x, y)
        for p in range(n):
            for k in range(4):
                _gather_copy(ins, land, axes, sa, ra, k, p, me, here, from_shard=True).wait_send()
            _own_block_copy(ins, land, axes, sa, p, me).wait()
            _gather_copy(ins, land, axes, sa, ra, 0, p, 4 * x + 2 * y + 1 - c, here).wait_recv()
            for j, (px, py) in enumerate(chips):
                _gather_copy(ins, land, axes, sf, rf, j, p, 4 * px + 2 * py + c, here).wait_send()
                _gather_copy(ins, land, axes, sf, rf, j, p, 4 * px + 2 * py + 1 - c, here).wait_recv()

    outs = pl.pallas_call(
        body, name=name,
        out_shape=(*[pltpu.HBM(s.shape, s.dtype) for s in shards], *[pltpu.HBM(l.shape, l.dtype) for l in lands]),
        in_specs=[_HBM] * (2 * n) + [_SEM] * 4 + [pl.BlockSpec(memory_space=pl.ANY)],
        out_specs=tuple([_HBM] * (2 * n)),
        input_output_aliases={i: i for i in range(2 * n)},
        compiler_params=pltpu.CompilerParams(has_side_effects=_EFFECT),
    )(*shards, *lands, send_a, recv_a, send_f, recv_f, after)
    return list(outs[n:])


_HBM = pl.BlockSpec(memory_space=pltpu.HBM)
_SEM = pl.BlockSpec(memory_space=pltpu.SEMAPHORE)
_EFFECT = pltpu.SideEffectType.DATAFLOW_SIDE_EFFECTING


def _in_hbm(a):
    return pltpu.with_memory_space_constraint(a, pltpu.HBM)


def _landing(mode, src, axis):
    me = 4 * lax.axis_index("x") + 2 * lax.axis_index("y") + lax.axis_index("c")
    if mode == "gather":
        shape = list(src.shape)
        size = shape[axis]
        shape[axis] *= N_DEV
        return lax.dynamic_update_slice_in_dim(lax.empty(tuple(shape), src.dtype), src, me * size, axis)
    size = src.shape[axis] // N_DEV
    own = lax.dynamic_slice_in_dim(src, me * size, size, axis)
    return lax.dynamic_update_slice_in_dim(lax.empty((N_DEV, *own.shape), src.dtype), own[None], me, 0)


def _seven(mode, land_ref, axis):
    if mode == "gather":
        return _block(land_ref, axis, (N_DEV - 1) * (land_ref.shape[axis] // N_DEV), 0)
    return land_ref.at[pl.ds(0, N_DEV - 1)]


def exchange_start(mode, srcs, axes, after, name):
    n = len(srcs)
    lands = [_landing(mode, s, ax) for s, ax in zip(srcs, axes)]

    def body(*refs):
        src, land = refs[:n], refs[n:2 * n]
        send_sems, recv_sems = refs[2 * n + 1], refs[2 * n + 2]
        token = refs[-1]
        x, y, c, me = _position()
        for k in range(1, N_DEV):
            peer, pj = _peer(x, y, c, k)
            for p in range(n):
                if mode == "gather":
                    s = src[p]
                    dst = _block(land[p], axes[p], src[p].shape[axes[p]], me)
                else:
                    s = _block(src[p], axes[p], src[p].shape[axes[p]] // N_DEV, pj)
                    dst = land[p].at[me]
                pltpu.make_async_remote_copy(
                    src_ref=s, dst_ref=dst, send_sem=send_sems.at[p], recv_sem=recv_sems.at[p],
                    device_id=peer, device_id_type=MESH_ID).start()
        token[...] = jnp.zeros_like(token)

    outs = pl.pallas_call(
        body, name=name,
        out_shape=(pltpu.SemaphoreType.DMA((n,)), pltpu.SemaphoreType.DMA((n,)),
                   *[pltpu.HBM(s.shape, s.dtype) for s in srcs], *[pltpu.HBM(l.shape, l.dtype) for l in lands],
                   _sds((8, 128), F32)),
        in_specs=[_HBM] * (2 * n) + [pl.BlockSpec(memory_space=pl.ANY)],
        out_specs=(_SEM, _SEM, *[_HBM] * (2 * n), pl.BlockSpec(memory_space=pltpu.VMEM)),
        input_output_aliases={i: 2 + i for i in range(2 * n)},
        compiler_params=pltpu.CompilerParams(has_side_effects=_EFFECT),
    )(*[_in_hbm(s) for s in srcs], *[_in_hbm(l) for l in lands], after)
    return outs[0], outs[1], list(outs[2:2 + n]), list(outs[2 + n:2 + 2 * n]), outs[-1]


def exchange_wait(mode, started, axes, after, name):
    send_sems, recv_sems, srcs, lands, _ = started
    n = len(srcs)

    def body(*refs):
        land = refs[n:2 * n]
        send_ref, recv_ref = refs[2 * n], refs[2 * n + 1]
        x, y, c, _ = _position()
        for p in range(n):
            seven = _seven(mode, land[p], axes[p])
            cp = pltpu.make_async_remote_copy(
                src_ref=seven, dst_ref=seven, send_sem=send_ref.at[p], recv_sem=recv_ref.at[p],
                device_id=(x, y, c), device_id_type=MESH_ID)
            cp.wait_send()
            cp.wait_recv()

    outs = pl.pallas_call(
        body, name=name,
        out_shape=(*[pltpu.HBM(s.shape, s.dtype) for s in srcs], *[pltpu.HBM(l.shape, l.dtype) for l in lands]),
        in_specs=[_HBM] * (2 * n) + [_SEM, _SEM, pl.BlockSpec(memory_space=pl.ANY)],
        out_specs=tuple([_HBM] * (2 * n)),
        input_output_aliases={i: i for i in range(2 * n)},
        compiler_params=pltpu.CompilerParams(has_side_effects=_EFFECT),
    )(*srcs, *lands, send_sems, recv_sems, after)
    return list(outs[n:])


def _adam_math(g, w, m, v):
    m2 = ADAM_B1 * m + (1.0 - ADAM_B1) * g
    v2 = ADAM_B2 * v + (1.0 - ADAM_B2) * (g * g)
    m_hat = m2 / (1.0 - ADAM_B1 ** ADAM_STEP)
    v_hat = v2 / (1.0 - ADAM_B2 ** ADAM_STEP)
    delta = -ADAM_LR * (m_hat / (jnp.sqrt(v_hat) + ADAM_EPS) + ADAM_WD * w)
    return delta, m2, v2


def adamw(g, w, m, v, name, slots, tr=128):
    r, c = w.shape
    tr = min(tr, r)

    def body(g_ref, w_ref, m_ref, v_ref, go_ref, d_ref, mo_ref, vo_ref):
        if slots:
            gs = g_ref[0].astype(F32)
            for s in range(1, N_DEV):
                gs = gs + g_ref[s].astype(F32)
        else:
            gs = g_ref[...]
        go_ref[...] = gs
        d_ref[...], mo_ref[...], vo_ref[...] = _adam_math(gs, w_ref[...], m_ref[...], v_ref[...])

    row = pl.BlockSpec((tr, c), lambda i: (i, 0))
    gspec = pl.BlockSpec((N_DEV, tr, c), lambda i: (0, i, 0)) if slots else row
    return pl.pallas_call(
        body, name=name, grid=(r // tr,),
        in_specs=[gspec, row, row, row], out_specs=[row] * 4, out_shape=[_sds((r, c), F32)] * 4,
        compiler_params=_cp("parallel"),
    )(g, w, m, v)


def sum_slots(g, name):
    _, r, c = g.shape

    def body(g_ref, o_ref):
        gs = g_ref[0]
        for s in range(1, N_DEV):
            gs = gs + g_ref[s]
        o_ref[...] = gs

    return pl.pallas_call(
        body, name=name, grid=(1,),
        in_specs=[pl.BlockSpec((N_DEV, r, c), lambda i: (0, 0, 0))],
        out_specs=pl.BlockSpec((r, c), lambda i: (0, 0)), out_shape=_sds((r, c), F32),
        compiler_params=_cp("arbitrary"),
    )(g)


def _rows128(a, pad_to=8):
    a = a.reshape(-1, 128)
    pad = (-a.shape[0]) % pad_to
    return jnp.pad(a, ((0, pad), (0, 0))) if pad else a


def kernel(x, even_norm, even_w_in, even_pool_w, even_pool_scale, even_ws, even_bs, even_w_out, odd_norm, odd_w_in, odd_conv_w, odd_w_out, final_norm, loss_target, m_even_norm, m_even_w_in, m_even_pool_w, m_even_pool_scale, m_even_ws, m_even_bs, m_even_w_out, m_odd_norm, m_odd_w_in, m_odd_conv_w, m_odd_w_out, m_final_norm, v_even_norm, v_even_w_in, v_even_pool_w, v_even_pool_scale, v_even_ws, v_even_bs, v_even_w_out, v_odd_norm, v_odd_w_in, v_odd_conv_w, v_odd_w_out, v_final_norm):
    x0 = x[0]
    target = loss_target[0]
    me = 4 * lax.axis_index("x") + 2 * lax.axis_index("y") + lax.axis_index("c")

    (we_in,) = all_gather_two_level([even_w_in[0].astype(BF16)], [1], "gather_even_in")
    odd_small = jnp.pad(odd_norm, ((0, 7), (0, 0))) + jnp.pad(odd_conv_w[0], ((1, 4), (0, 0)))
    rest_axes, odd_axes = [1, 0, 1], [1, 0]
    rest_flight = exchange_start("gather", [even_pool_w[0].astype(BF16), even_w_out[0].astype(BF16), odd_small],
                                 rest_axes, we_in, "gather_even_rest_start")
    odd_flight = gather_two_level_start([odd_w_in[0].astype(BF16), odd_w_out[0].astype(BF16)],
                                        odd_axes, rest_flight[4], "gather_odd_start")
    scale = even_pool_scale
    ws = even_ws[0]
    bs_col = even_bs[0][:, :, None]
    g_fin = final_norm[None, :]

    z_e, h_e_t, we_in_t = norm_matmul(x0, even_norm + odd_flight[4][0:1, 0:1], we_in, "even_in", tn=1280)
    wp, we_out, odd_small = exchange_wait("gather", rest_flight, rest_axes, z_e, "gather_even_rest_wait")
    g_odd, conv_w = odd_small[0:1], odd_small[1:4]
    y_e, pooled = even_mix(z_e, wp, scale, ws, bs_col, "even_mix")
    x1, h_o, h_o_t, h_o_d4 = matmul_residual(y_e, we_out, x0, "even_out", next_gain=g_odd)
    odd_flight = gather_two_level_forward(odd_flight, odd_axes, x1, "gather_odd_forward")
    h_layouts = [h_o, h_o_d4, permute_mats(h_o, [], 2, False, "dilate_h_2")[0]]
    wo_in, wo_out = gather_two_level_wait(odd_flight, odd_axes, h_layouts[2], "gather_odd_wait")
    z_o, wo_in_t = matmul_layouts(h_layouts, wo_in, "odd_in")
    attn = [attn_fwd(z_o, gi, f"attn_fwd_{gi}") for gi in range(3)]
    attn[2] = permute_mats(attn[2][0], [attn[2][1]], 2, True, "undilate_attn_2")
    y_o, yc, lse_tot, conv = odd_mix([a[0] for a in attn], [a[1] for a in attn], z_o, conv_w, "odd_mix")
    dx2, loss_blk, dg_fin = out_proj_final_loss(y_o, wo_out, x1, g_fin, target, "odd_out_loss")

    dy_o, dwo_out = out_proj_bwd(dx2, wo_out.T, y_o, "odd_out_bwd")
    dz5_o, dyc, dd, dconv_w = odd_mix_bwd(dy_o, yc, conv, z_o, conv_w, "odd_mix_bwd")
    dqkv = [attn_bwd(z_o, dyc, lse_tot, dd, 0, "attn_bwd_0")]
    dqkv.append(attn_bwd(z_o, dyc, lse_tot, dd, 1, "attn_bwd_1"))
    dyc_2, lse_2, dd_2 = permute_mats(dyc, [lse_tot, dd], 2, False, "dilate_dy_2")
    dqkv.append(permute_slabs(attn_bwd(z_o, dyc_2, lse_2, dd_2, 2, "attn_bwd_2"), 2, True, "undilate_dqkv_2"))
    src_o = [(dqkv[0], ("slabs", 0)), (dqkv[1], ("slabs", 1)), (dqkv[2], ("slabs", 2)), (dz5_o, ("cols", 9, 5))]
    dwo_in = in_proj_dw(h_o_t, src_o, ODD_COLBLOCKS, "odd_in_dw")
    odd_grads = exchange_start("scatter", [dwo_in, dwo_out], [1, 0], dconv_w, "scatter_odd_start")
    dx1, dg_odd = in_proj_dx(src_o, ODD_COLBLOCKS, wo_in_t, x1, g_odd + odd_grads[4][0:1, 0:1], dx2, "odd_in_dx")

    dy_e, dwe_out = out_proj_bwd(dx1, we_out.T, y_e, "even_out_bwd")
    dz_e, dwp, dscale, dws, dbs_col = even_mix_bwd(dy_e, z_e, pooled, wp, scale, ws, bs_col, "even_mix_bwd")
    src_e = [(dz_e, ("cols", 0, 5))]
    even_grads_a = exchange_start("scatter", [dwp.astype(BF16), dwe_out], [1, 0], dscale, "scatter_even_rest_start")
    dwe_in = in_proj_dw(h_e_t, src_e, 5, "even_in_dw", tk=2048)
    even_grads_b = exchange_start("scatter", [dwe_in], [1], even_grads_a[4], "scatter_even_in_start")
    dx0, dg_even = in_proj_dx(src_e, 5, we_in_t, x0, even_norm + even_grads_b[4][0:1, 0:1], dx1, "even_in_dx")
    small = jnp.concatenate([
        _rows128(dg_even), _rows128(dscale), _rows128(dws), _rows128(dbs_col), _rows128(dg_fin),
        _rows128(dg_odd), _rows128(dconv_w), loss_blk], axis=0)
    n_small = small.shape[0]
    small_flight = exchange_start("gather", [small], [0], dg_fin, "gather_small_start")
    p_wo_in, p_wo_out = exchange_wait("scatter", odd_grads, [1, 0], small_flight[4], "scatter_odd_wait")
    p_wp, p_we_out = exchange_wait("scatter", even_grads_a, [1, 0], small_flight[4], "scatter_even_rest_wait")
    (p_we_in,) = exchange_wait("scatter", even_grads_b, [1], small_flight[4], "scatter_even_in_wait")

    two_d = lambda a, r, c: a.reshape(r, c)
    sharded = [
        ("even_w_in", p_we_in, even_w_in, m_even_w_in, v_even_w_in, (1024, 640)),
        ("even_pool_w", p_wp.reshape(N_DEV, 128, 256), even_pool_w, m_even_pool_w, v_even_pool_w, (128, 256)),
        ("even_w_out", p_we_out, even_w_out, m_even_w_out, v_even_w_out, (256, 1024)),
        ("odd_w_in", p_wo_in, odd_w_in, m_odd_w_in, v_odd_w_in, (1024, 1792)),
        ("odd_w_out", p_wo_out, odd_w_out, m_odd_w_out, v_odd_w_out, (256, 1024)),
    ]
    res = {}
    for name, g, w, m, v, (r, c) in sharded:
        outs = adamw(g, two_d(w, r, c), two_d(m, r, c), two_d(v, r, c), "adamw_" + name, slots=True)
        res[name] = [o.reshape(w.shape) for o in outs]

    (small_all,) = exchange_wait("gather", small_flight, [0], res["odd_w_out"][0], "gather_small_wait")
    small_sum = sum_slots(small_all.reshape(N_DEV, n_small, 128), "sum_small_grads")
    g_even_norm = small_sum[0:8].reshape(1, 1024)
    g_scale = small_sum[8:16].reshape(1, 1024)
    g_ws = small_sum[16:528]
    g_bs = small_sum[528:532]
    g_final = small_sum[536:544].reshape(1, 1024)
    g_odd_norm = lax.dynamic_slice_in_dim(small_sum[544:552], me, 1, axis=0)
    g_conv = lax.dynamic_index_in_dim(small_sum[552:576].reshape(3, 8, 128), me, axis=1, keepdims=False)
    plain = [
        ("even_norm", g_even_norm, even_norm, m_even_norm, v_even_norm, (1, 1024)),
        ("even_pool_scale", g_scale, even_pool_scale, m_even_pool_scale, v_even_pool_scale, (1, 1024)),
        ("even_ws", g_ws, even_ws, m_even_ws, v_even_ws, (512, 128)),
        ("even_bs", g_bs, even_bs, m_even_bs, v_even_bs, (4, 128)),
        ("odd_norm", g_odd_norm, odd_norm, m_odd_norm, v_odd_norm, (1, 128)),
        ("odd_conv_w", g_conv, odd_conv_w, m_odd_conv_w, v_odd_conv_w, (3, 128)),
        ("final_norm", g_final, final_norm, m_final_norm, v_final_norm, (1, 1024)),
    ]
    for name, g, w, m, v, (r, c) in plain:
        outs = adamw(two_d(g, r, c), two_d(w, r, c), two_d(m, r, c), two_d(v, r, c), "adamw_" + name, slots=False)
        res[name] = [o.reshape(w.shape) for o in outs]

    loss = small_sum[576, 0]
    order = ["even_norm", "even_w_in", "even_pool_w", "even_pool_scale", "even_ws", "even_bs", "even_w_out",
             "odd_norm", "odd_w_in", "odd_conv_w", "odd_w_out", "final_norm"]
    return (loss, dx0[None], *[res[n][0] for n in order], *[res[n][1] for n in order],
            *[res[n][2] for n in order], *[res[n][3] for n in order])
```
